```python
import math
import jax, jax.numpy as jnp
from jax import lax
import numpy as np

D_MODEL = 1024
BATCH = 16
SEQ = 2048
DEPTH = 4

GRID_W = 64
CTX_LEN = 256
MIXERS = ("s5", "conv")
N_MIX = len(MIXERS)
S5_GROUP = 16
S5_GROUPS = D_MODEL // S5_GROUP
S5_STATE = 64
N_DIR = 2
CONV_WIDTH = 31
CONV_HALF = CONV_WIDTH // 2
D_FF = 4 * D_MODEL
N_S5_LAYERS = sum(1 for _i in range(DEPTH) if MIXERS[_i % N_MIX] == "s5")
N_CONV_LAYERS = DEPTH - N_S5_LAYERS
DN_ALPHA = (2.0 * DEPTH) ** 0.25
DN_BETA = (8.0 * DEPTH) ** -0.25
LN_EPS = 1e-5
DT_MIN = 1e-3
DT_MAX = 1e-1
POS_TEMP = 10000.0
LAMBDA_RE_MAX = -1e-4

kernel_name = "hybrid_s5_conformer_dit_trunk"


def layer_norm(x, g, b):
    xf = x.astype(jnp.float32)
    mu = jnp.mean(xf, axis=-1, keepdims=True)
    var = jnp.mean(jnp.square(xf - mu), axis=-1, keepdims=True)
    y = (xf - mu) * lax.rsqrt(var + LN_EPS) * g.astype(jnp.float32) + b.astype(jnp.float32)
    return y.astype(x.dtype)


def modulate(x, shift, scale):
    return x * (1 + scale) + shift


def sincos_1d(pos, dim):
    quarter = dim // 2
    omega = POS_TEMP ** (-jnp.arange(quarter, dtype=jnp.float32) / quarter)
    ang = pos[:, None] * omega[None, :]
    return jnp.concatenate([jnp.sin(ang), jnp.cos(ang)], axis=-1)


def grid_pos_embed(rows, dim):
    row_idx = jnp.repeat(jnp.arange(rows), GRID_W).astype(jnp.float32)
    col_idx = jnp.tile(jnp.arange(GRID_W), rows).astype(jnp.float32)
    return jnp.concatenate([sincos_1d(row_idx, dim // 2), sincos_1d(col_idx, dim // 2)], axis=-1)


def s5_discretise(lam_re, lam_im, log_dt, b_re, b_im):
    lr = jnp.minimum(lam_re.astype(jnp.float32), LAMBDA_RE_MAX)
    li = lam_im.astype(jnp.float32)
    dt = jnp.exp(log_dt.astype(jnp.float32))[:, None]
    mag = jnp.exp(lr * dt)
    ab_re = mag * jnp.cos(li * dt)
    ab_im = mag * jnp.sin(li * dt)
    den = lr * lr + li * li
    nr = ab_re - 1.0
    ni = ab_im
    coef_re = (nr * lr + ni * li) / den
    coef_im = (ni * lr - nr * li) / den
    br = b_re.astype(jnp.float32)
    bi = b_im.astype(jnp.float32)
    bb_re = coef_re[..., None] * br - coef_im[..., None] * bi
    bb_im = coef_re[..., None] * bi + coef_im[..., None] * br
    return ab_re, ab_im, bb_re, bb_im


def _scan_op(e1, e2):
    ar1, ai1, br1, bi1 = e1
    ar2, ai2, br2, bi2 = e2
    return (ar2 * ar1 - ai2 * ai1,
            ar2 * ai1 + ai2 * ar1,
            ar2 * br1 - ai2 * bi1 + br2,
            ar2 * bi1 + ai2 * br1 + bi2)


def s5_scan(u, lam_re, lam_im, log_dt, b_re, b_im, h0s):
    length = u.shape[1]
    states = []
    for d in range(N_DIR):
        reverse = d == 1
        ab_re, ab_im, bb_re, bb_im = s5_discretise(lam_re[d], lam_im[d], log_dt[d], b_re[d], b_im[d])
        bu_re = jnp.einsum("blgc,gpc->blgp", u, bb_re)
        bu_im = jnp.einsum("blgc,gpc->blgp", u, bb_im)
        if h0s is not None:
            h0_re, h0_im = h0s[d]
            edge = -1 if reverse else 0
            bu_re = bu_re.at[:, edge].add(ab_re * h0_re - ab_im * h0_im)
            bu_im = bu_im.at[:, edge].add(ab_re * h0_im + ab_im * h0_re)
        a_re = jnp.broadcast_to(ab_re[None, None], (1, length) + ab_re.shape)
        a_im = jnp.broadcast_to(ab_im[None, None], (1, length) + ab_im.shape)
        _, _, h_re, h_im = lax.associative_scan(_scan_op, (a_re, a_im, bu_re, bu_im), reverse=reverse, axis=1)
        states.append((h_re, h_im))
    return states


def s5_readout(u, states, c_re, c_im, d_skip, w_glu, b_glu, out_dtype):
    y = d_skip.astype(jnp.float32).reshape(S5_GROUPS, S5_GROUP) * u
    for d, (h_re, h_im) in enumerate(states):
        y = y + jnp.einsum("blgp,gcp->blgc", h_re, c_re[d].astype(jnp.float32)) \
              - jnp.einsum("blgp,gcp->blgc", h_im, c_im[d].astype(jnp.float32))
    bsz, length = u.shape[0], u.shape[1]
    z = jax.nn.gelu(y.reshape(bsz, length, D_MODEL), approximate=False).astype(out_dtype)
    zz = z @ w_glu + b_glu
    return zz[..., :D_MODEL] * jax.nn.sigmoid(zz[..., D_MODEL:])


def to_groups(h):
    return h.astype(jnp.float32).reshape(h.shape[0], h.shape[1], S5_GROUPS, S5_GROUP)


def conv_module(h, w_pw1, b_pw1, w_dw, b_dw, ln_g, ln_b, w_pw2, b_pw2):
    a = h @ w_pw1 + b_pw1
    a = a[..., :D_MODEL] * jax.nn.sigmoid(a[..., D_MODEL:])
    a = lax.conv_general_dilated(a, w_dw[:, None, :].astype(a.dtype), window_strides=(1,),
                                 padding=[(CONV_HALF, CONV_HALF)],
                                 dimension_numbers=("NWC", "WIO", "NWC"),
                                 feature_group_count=D_MODEL) + b_dw
    a = jax.nn.silu(layer_norm(a, ln_g, ln_b))
    return a @ w_pw2 + b_pw2


def sq_relu_mlp(h, w1, w2):
    return jnp.square(jax.nn.relu(h @ w1)) @ w2


def _fwd_setup_inputs(seed: int = 0) -> dict:
    key = jax.random.key(seed)
    ks = jax.random.split(key, 32)
    f32 = jnp.float32
    D, G, P, CH = D_MODEL, S5_GROUPS, S5_STATE, S5_GROUP
    nrm = lambda k, shape, s: jax.random.normal(k, shape, f32) * s
    x = nrm(ks[0], (BATCH, SEQ, D), 1.0)
    c = nrm(ks[1], (BATCH, D), 1.0)
    ctx = nrm(ks[2], (BATCH, CTX_LEN, D), 1.0)
    c_ctx = nrm(ks[3], (D,), 1.0)
    w_ada = nrm(ks[4], (DEPTH, D, 6 * D), D ** -0.5)
    b_ada = nrm(ks[5], (DEPTH, 6 * D), 0.02)
    ln_gain = 1.0 + nrm(ks[6], (DEPTH, 2, D), 0.02)
    ln_bias = nrm(ks[7], (DEPTH, 2, D), 0.02)
    n_idx = jnp.arange(P, dtype=f32)
    s5_lam_re = -0.5 + nrm(ks[8], (N_S5_LAYERS, N_DIR, G, P), 0.01)
    s5_lam_im = math.pi * n_idx + nrm(ks[9], (N_S5_LAYERS, N_DIR, G, P), 0.01)
    s5_log_dt = jax.random.uniform(ks[10], (N_S5_LAYERS, N_DIR, G), f32, math.log(DT_MIN), math.log(DT_MAX))
    s5_b_re = nrm(ks[11], (N_S5_LAYERS, N_DIR, G, P, CH), (2.0 * CH) ** -0.5)
    s5_b_im = nrm(ks[12], (N_S5_LAYERS, N_DIR, G, P, CH), (2.0 * CH) ** -0.5)
    s5_c_re = nrm(ks[13], (N_S5_LAYERS, N_DIR, G, CH, P), P ** -0.5)
    s5_c_im = nrm(ks[14], (N_S5_LAYERS, N_DIR, G, CH, P), P ** -0.5)
    s5_d = 1.0 + nrm(ks[15], (N_S5_LAYERS, D), 0.1)
    glu_out = nrm(ks[16], (N_S5_LAYERS, D, D), DN_BETA * D ** -0.5)
    glu_gate = nrm(ks[17], (N_S5_LAYERS, D, D), D ** -0.5)
    s5_w_glu = jnp.concatenate([glu_out, glu_gate], axis=-1)
    s5_b_glu = nrm(ks[18], (N_S5_LAYERS, 2 * D), 0.02)
    cv_w_pw1 = nrm(ks[19], (N_CONV_LAYERS, D, 2 * D), D ** -0.5)
    cv_b_pw1 = nrm(ks[20], (N_CONV_LAYERS, 2 * D), 0.02)
    cv_w_dw = nrm(ks[21], (N_CONV_LAYERS, CONV_WIDTH, D), CONV_WIDTH ** -0.5)
    cv_b_dw = nrm(ks[22], (N_CONV_LAYERS, D), 0.02)
    cv_ln_g = 1.0 + nrm(ks[23], (N_CONV_LAYERS, D), 0.02)
    cv_ln_b = nrm(ks[24], (N_CONV_LAYERS, D), 0.02)
    cv_w_pw2 = nrm(ks[25], (N_CONV_LAYERS, D, D), DN_BETA * D ** -0.5)
    cv_b_pw2 = nrm(ks[26], (N_CONV_LAYERS, D), 0.02)
    mlp_w1 = nrm(ks[27], (DEPTH, D, D_FF), D ** -0.5)
    mlp_w2 = nrm(ks[28], (DEPTH, D_FF, D), DN_BETA * D_FF ** -0.5)
    return {"x": x, "c": c, "ctx": ctx, "c_ctx": c_ctx,
            "w_ada": w_ada, "b_ada": b_ada, "ln_gain": ln_gain, "ln_bias": ln_bias,
            "s5_lam_re": s5_lam_re, "s5_lam_im": s5_lam_im, "s5_log_dt": s5_log_dt,
            "s5_b_re": s5_b_re, "s5_b_im": s5_b_im, "s5_c_re": s5_c_re, "s5_c_im": s5_c_im,
            "s5_d": s5_d, "s5_w_glu": s5_w_glu, "s5_b_glu": s5_b_glu,
            "cv_w_pw1": cv_w_pw1, "cv_b_pw1": cv_b_pw1, "cv_w_dw": cv_w_dw, "cv_b_dw": cv_b_dw,
            "cv_ln_g": cv_ln_g, "cv_ln_b": cv_ln_b, "cv_w_pw2": cv_w_pw2, "cv_b_pw2": cv_b_pw2,
            "mlp_w1": mlp_w1, "mlp_w2": mlp_w2}


def _fwd_reference(x, c, ctx, c_ctx, w_ada, b_ada, ln_gain, ln_bias,
              s5_lam_re, s5_lam_im, s5_log_dt, s5_b_re, s5_b_im, s5_c_re, s5_c_im,
              s5_d, s5_w_glu, s5_b_glu,
              cv_w_pw1, cv_b_pw1, cv_w_dw, cv_b_dw, cv_ln_g, cv_ln_b, cv_w_pw2, cv_b_pw2,
              mlp_w1, mlp_w2):
    rows = x.shape[1] // GRID_W
    x = x + grid_pos_embed(rows, D_MODEL).astype(x.dtype)[None]
    cond = jax.nn.silu(c)
    cond_ctx = jax.nn.silu(c_ctx)
    kinds = [MIXERS[i % N_MIX] for i in range(DEPTH)]
    s5_j = 0
    cv_j = 0
    for i in range(DEPTH):
        kind = kinds[i]
        ctx_read_here = kind == "s5"
        ctx_needed_later = any(k == "s5" for k in kinds[i + 1:])
        use_ctx = ctx_read_here or ctx_needed_later
        mod = (cond @ w_ada[i] + b_ada[i])[:, None, :]
        sh1, sc1, g1, sh2, sc2, g2 = jnp.split(mod, 6, axis=-1)
        if use_ctx:
            mod_c = (cond_ctx @ w_ada[i] + b_ada[i])[None, None, :]
            csh1, csc1, cg1, csh2, csc2, cg2 = jnp.split(mod_c, 6, axis=-1)
            hc = modulate(ctx, csh1, csc1)
        h = modulate(x, sh1, sc1)
        if kind == "s5":
            j = s5_j
            s5_j += 1
            uc = to_groups(hc)
            states_c = s5_scan(uc, s5_lam_re[j], s5_lam_im[j], s5_log_dt[j], s5_b_re[j], s5_b_im[j], None)
            h0s = [(states_c[0][0][:, -1], states_c[0][1][:, -1]),
                   (states_c[1][0][:, 0], states_c[1][1][:, 0])]
            u = to_groups(h)
            states = s5_scan(u, s5_lam_re[j], s5_lam_im[j], s5_log_dt[j], s5_b_re[j], s5_b_im[j], h0s)
            mix = s5_readout(u, states, s5_c_re[j], s5_c_im[j], s5_d[j], s5_w_glu[j], s5_b_glu[j], x.dtype)
            if ctx_needed_later:
                mix_c = s5_readout(uc, states_c, s5_c_re[j], s5_c_im[j], s5_d[j], s5_w_glu[j], s5_b_glu[j], ctx.dtype)
        else:
            j = cv_j
            cv_j += 1
            cv_args = (cv_w_pw1[j], cv_b_pw1[j], cv_w_dw[j], cv_b_dw[j], cv_ln_g[j], cv_ln_b[j], cv_w_pw2[j], cv_b_pw2[j])
            mix = conv_module(h, *cv_args)
            if ctx_needed_later:
                mix_c = conv_module(hc, *cv_args)
        x = layer_norm(DN_ALPHA * x + g1 * mix, ln_gain[i, 0], ln_bias[i, 0])
        h = modulate(x, sh2, sc2)
        x = layer_norm(DN_ALPHA * x + g2 * sq_relu_mlp(h, mlp_w1[i], mlp_w2[i]), ln_gain[i, 1], ln_bias[i, 1])
        if ctx_needed_later:
            ctx = layer_norm(DN_ALPHA * ctx + cg1 * mix_c, ln_gain[i, 0], ln_bias[i, 0])
            hc2 = modulate(ctx, csh2, csc2)
            ctx = layer_norm(DN_ALPHA * ctx + cg2 * sq_relu_mlp(hc2, mlp_w1[i], mlp_w2[i]), ln_gain[i, 1], ln_bias[i, 1])
    return x


import jax as _jax
import jax.numpy as _jnp

TWIN_FORMAT = 'train_step'
FWD_PARAMS = ['x', 'c', 'ctx', 'c_ctx', 'w_ada', 'b_ada', 'ln_gain', 'ln_bias', 's5_lam_re', 's5_lam_im', 's5_log_dt', 's5_b_re', 's5_b_im', 's5_c_re', 's5_c_im', 's5_d', 's5_w_glu', 's5_b_glu', 'cv_w_pw1', 'cv_b_pw1', 'cv_w_dw', 'cv_b_dw', 'cv_ln_g', 'cv_ln_b', 'cv_w_pw2', 'cv_b_pw2', 'mlp_w1', 'mlp_w2']
TWIN_WEIGHTS = ['c_ctx', 'w_ada', 'b_ada', 'ln_gain', 'ln_bias', 's5_lam_re', 's5_lam_im', 's5_log_dt', 's5_b_re', 's5_b_im', 's5_c_re', 's5_c_im', 's5_d', 's5_w_glu', 's5_b_glu', 'cv_w_pw1', 'cv_b_pw1', 'cv_w_dw', 'cv_b_dw', 'cv_ln_g', 'cv_ln_b', 'cv_w_pw2', 'cv_b_pw2', 'mlp_w1', 'mlp_w2']
TWIN_DIFF_INPUT = 'x'
TWIN_INPUTS = ['x', 'c', 'ctx', 'c_ctx', 'w_ada', 'b_ada', 'ln_gain', 'ln_bias', 's5_lam_re', 's5_lam_im', 's5_log_dt', 's5_b_re', 's5_b_im', 's5_c_re', 's5_c_im', 's5_d', 's5_w_glu', 's5_b_glu', 'cv_w_pw1', 'cv_b_pw1', 'cv_w_dw', 'cv_b_dw', 'cv_ln_g', 'cv_ln_b', 'cv_w_pw2', 'cv_b_pw2', 'mlp_w1', 'mlp_w2', 'loss_target', 'm_c_ctx', 'm_w_ada', 'm_b_ada', 'm_ln_gain', 'm_ln_bias', 'm_s5_lam_re', 'm_s5_lam_im', 'm_s5_log_dt', 'm_s5_b_re', 'm_s5_b_im', 'm_s5_c_re', 'm_s5_c_im', 'm_s5_d', 'm_s5_w_glu', 'm_s5_b_glu', 'm_cv_w_pw1', 'm_cv_b_pw1', 'm_cv_w_dw', 'm_cv_b_dw', 'm_cv_ln_g', 'm_cv_ln_b', 'm_cv_w_pw2', 'm_cv_b_pw2', 'm_mlp_w1', 'm_mlp_w2', 'v_c_ctx', 'v_w_ada', 'v_b_ada', 'v_ln_gain', 'v_ln_bias', 'v_s5_lam_re', 'v_s5_lam_im', 'v_s5_log_dt', 'v_s5_b_re', 'v_s5_b_im', 'v_s5_c_re', 'v_s5_c_im', 'v_s5_d', 'v_s5_w_glu', 'v_s5_b_glu', 'v_cv_w_pw1', 'v_cv_b_pw1', 'v_cv_w_dw', 'v_cv_b_dw', 'v_cv_ln_g', 'v_cv_ln_b', 'v_cv_w_pw2', 'v_cv_b_pw2', 'v_mlp_w1', 'v_mlp_w2']
TWIN_OUTPUTS = ['loss', 'grad_x', 'grad_c_ctx', 'grad_w_ada', 'grad_b_ada', 'grad_ln_gain', 'grad_ln_bias', 'grad_s5_lam_re', 'grad_s5_lam_im', 'grad_s5_log_dt', 'grad_s5_b_re', 'grad_s5_b_im', 'grad_s5_c_re', 'grad_s5_c_im', 'grad_s5_d', 'grad_s5_w_glu', 'grad_s5_b_glu', 'grad_cv_w_pw1', 'grad_cv_b_pw1', 'grad_cv_w_dw', 'grad_cv_b_dw', 'grad_cv_ln_g', 'grad_cv_ln_b', 'grad_cv_w_pw2', 'grad_cv_b_pw2', 'grad_mlp_w1', 'grad_mlp_w2', 'delta_c_ctx', 'delta_w_ada', 'delta_b_ada', 'delta_ln_gain', 'delta_ln_bias', 'delta_s5_lam_re', 'delta_s5_lam_im', 'delta_s5_log_dt', 'delta_s5_b_re', 'delta_s5_b_im', 'delta_s5_c_re', 'delta_s5_c_im', 'delta_s5_d', 'delta_s5_w_glu', 'delta_s5_b_glu', 'delta_cv_w_pw1', 'delta_cv_b_pw1', 'delta_cv_w_dw', 'delta_cv_b_dw', 'delta_cv_ln_g', 'delta_cv_ln_b', 'delta_cv_w_pw2', 'delta_cv_b_pw2', 'delta_mlp_w1', 'delta_mlp_w2', 'new_m_c_ctx', 'new_m_w_ada', 'new_m_b_ada', 'new_m_ln_gain', 'new_m_ln_bias', 'new_m_s5_lam_re', 'new_m_s5_lam_im', 'new_m_s5_log_dt', 'new_m_s5_b_re', 'new_m_s5_b_im', 'new_m_s5_c_re', 'new_m_s5_c_im', 'new_m_s5_d', 'new_m_s5_w_glu', 'new_m_s5_b_glu', 'new_m_cv_w_pw1', 'new_m_cv_b_pw1', 'new_m_cv_w_dw', 'new_m_cv_b_dw', 'new_m_cv_ln_g', 'new_m_cv_ln_b', 'new_m_cv_w_pw2', 'new_m_cv_b_pw2', 'new_m_mlp_w1', 'new_m_mlp_w2', 'new_v_c_ctx', 'new_v_w_ada', 'new_v_b_ada', 'new_v_ln_gain', 'new_v_ln_bias', 'new_v_s5_lam_re', 'new_v_s5_lam_im', 'new_v_s5_log_dt', 'new_v_s5_b_re', 'new_v_s5_b_im', 'new_v_s5_c_re', 'new_v_s5_c_im', 'new_v_s5_d', 'new_v_s5_w_glu', 'new_v_s5_b_glu', 'new_v_cv_w_pw1', 'new_v_cv_b_pw1', 'new_v_cv_w_dw', 'new_v_cv_b_dw', 'new_v_cv_ln_g', 'new_v_cv_ln_b', 'new_v_cv_w_pw2', 'new_v_cv_b_pw2', 'new_v_mlp_w1', 'new_v_mlp_w2']
TWIN_LEAF_KINDS = {'loss': 'loss', 'grad_x': 'grad_x', 'grad_c_ctx': 'grad_w', 'grad_w_ada': 'grad_w', 'grad_b_ada': 'grad_w', 'grad_ln_gain': 'grad_w', 'grad_ln_bias': 'grad_w', 'grad_s5_lam_re': 'grad_w', 'grad_s5_lam_im': 'grad_w', 'grad_s5_log_dt': 'grad_w', 'grad_s5_b_re': 'grad_w', 'grad_s5_b_im': 'grad_w', 'grad_s5_c_re': 'grad_w', 'grad_s5_c_im': 'grad_w', 'grad_s5_d': 'grad_w', 'grad_s5_w_glu': 'grad_w', 'grad_s5_b_glu': 'grad_w', 'grad_cv_w_pw1': 'grad_w', 'grad_cv_b_pw1': 'grad_w', 'grad_cv_w_dw': 'grad_w', 'grad_cv_b_dw': 'grad_w', 'grad_cv_ln_g': 'grad_w', 'grad_cv_ln_b': 'grad_w', 'grad_cv_w_pw2': 'grad_w', 'grad_cv_b_pw2': 'grad_w', 'grad_mlp_w1': 'grad_w', 'grad_mlp_w2': 'grad_w', 'delta_c_ctx': 'delta_w', 'delta_w_ada': 'delta_w', 'delta_b_ada': 'delta_w', 'delta_ln_gain': 'delta_w', 'delta_ln_bias': 'delta_w', 'delta_s5_lam_re': 'delta_w', 'delta_s5_lam_im': 'delta_w', 'delta_s5_log_dt': 'delta_w', 'delta_s5_b_re': 'delta_w', 'delta_s5_b_im': 'delta_w', 'delta_s5_c_re': 'delta_w', 'delta_s5_c_im': 'delta_w', 'delta_s5_d': 'delta_w', 'delta_s5_w_glu': 'delta_w', 'delta_s5_b_glu': 'delta_w', 'delta_cv_w_pw1': 'delta_w', 'delta_cv_b_pw1': 'delta_w', 'delta_cv_w_dw': 'delta_w', 'delta_cv_b_dw': 'delta_w', 'delta_cv_ln_g': 'delta_w', 'delta_cv_ln_b': 'delta_w', 'delta_cv_w_pw2': 'delta_w', 'delta_cv_b_pw2': 'delta_w', 'delta_mlp_w1': 'delta_w', 'delta_mlp_w2': 'delta_w', 'new_m_c_ctx': 'new_m', 'new_m_w_ada': 'new_m', 'new_m_b_ada': 'new_m', 'new_m_ln_gain': 'new_m', 'new_m_ln_bias': 'new_m', 'new_m_s5_lam_re': 'new_m', 'new_m_s5_lam_im': 'new_m', 'new_m_s5_log_dt': 'new_m', 'new_m_s5_b_re': 'new_m', 'new_m_s5_b_im': 'new_m', 'new_m_s5_c_re': 'new_m', 'new_m_s5_c_im': 'new_m', 'new_m_s5_d': 'new_m', 'new_m_s5_w_glu': 'new_m', 'new_m_s5_b_glu': 'new_m', 'new_m_cv_w_pw1': 'new_m', 'new_m_cv_b_pw1': 'new_m', 'new_m_cv_w_dw': 'new_m', 'new_m_cv_b_dw': 'new_m', 'new_m_cv_ln_g': 'new_m', 'new_m_cv_ln_b': 'new_m', 'new_m_cv_w_pw2': 'new_m', 'new_m_cv_b_pw2': 'new_m', 'new_m_mlp_w1': 'new_m', 'new_m_mlp_w2': 'new_m', 'new_v_c_ctx': 'new_v', 'new_v_w_ada': 'new_v', 'new_v_b_ada': 'new_v', 'new_v_ln_gain': 'new_v', 'new_v_ln_bias': 'new_v', 'new_v_s5_lam_re': 'new_v', 'new_v_s5_lam_im': 'new_v', 'new_v_s5_log_dt': 'new_v', 'new_v_s5_b_re': 'new_v', 'new_v_s5_b_im': 'new_v', 'new_v_s5_c_re': 'new_v', 'new_v_s5_c_im': 'new_v', 'new_v_s5_d': 'new_v', 'new_v_s5_w_glu': 'new_v', 'new_v_s5_b_glu': 'new_v', 'new_v_cv_w_pw1': 'new_v', 'new_v_cv_b_pw1': 'new_v', 'new_v_cv_w_dw': 'new_v', 'new_v_cv_b_dw': 'new_v', 'new_v_cv_ln_g': 'new_v', 'new_v_cv_ln_b': 'new_v', 'new_v_cv_w_pw2': 'new_v', 'new_v_cv_b_pw2': 'new_v', 'new_v_mlp_w1': 'new_v', 'new_v_mlp_w2': 'new_v'}


def _forward(args):
    return _fwd_reference(*[args[k] for k in FWD_PARAMS])


def _output_shape():
    out = _jax.eval_shape(lambda: _forward(_fwd_setup_inputs(0)))
    return out.shape, out.dtype

N_MICROBATCH = 1
ADAM_LR = 0.001
ADAM_B1 = 0.9
ADAM_B2 = 0.999
ADAM_EPS = 1e-08
ADAM_WD = 0.01
ADAM_STEP = 10
PER_EXAMPLE_BATCH_AXIS = {'x': 0, 'c': 0, 'ctx': 0, 'loss_target': 0}
SHARED_INPUTS = []
_WEIGHT_DTYPES = {'c_ctx': _jnp.float32, 'w_ada': _jnp.float32, 'b_ada': _jnp.float32, 'ln_gain': _jnp.float32, 'ln_bias': _jnp.float32, 's5_lam_re': _jnp.float32, 's5_lam_im': _jnp.float32, 's5_log_dt': _jnp.float32, 's5_b_re': _jnp.float32, 's5_b_im': _jnp.float32, 's5_c_re': _jnp.float32, 's5_c_im': _jnp.float32, 's5_d': _jnp.float32, 's5_w_glu': _jnp.float32, 's5_b_glu': _jnp.float32, 'cv_w_pw1': _jnp.float32, 'cv_b_pw1': _jnp.float32, 'cv_w_dw': _jnp.float32, 'cv_b_dw': _jnp.float32, 'cv_ln_g': _jnp.float32, 'cv_ln_b': _jnp.float32, 'cv_w_pw2': _jnp.float32, 'cv_b_pw2': _jnp.float32, 'mlp_w1': _jnp.float32, 'mlp_w2': _jnp.float32}
MOMENT_SCALE = {'c_ctx': 1.533168e-03, 'w_ada': 4.151681e-02, 'b_ada': 7.819650e-02, 'ln_gain': 1.163407e+01, 'ln_bias': 3.948959e+00, 's5_lam_re': 3.894659e-03, 's5_lam_im': 3.413826e-03, 's5_log_dt': 5.468732e-01, 's5_b_re': 2.016794e-03, 's5_b_im': 1.807883e-03, 's5_c_re': 2.890723e-03, 's5_c_im': 3.225174e-03, 's5_d': 2.536626e-02, 's5_w_glu': 3.577865e-02, 's5_b_glu': 4.323187e-02, 'cv_w_pw1': 1.605579e-02, 'cv_b_pw1': 1.622929e-02, 'cv_w_dw': 2.097449e-02, 'cv_b_dw': 3.519468e-02, 'cv_ln_g': 2.835246e-02, 'cv_ln_b': 2.669517e-02, 'cv_w_pw2': 5.204380e-02, 'cv_b_pw2': 1.202537e-01, 'mlp_w1': 3.855953e-02, 'mlp_w2': 1.866360e-01}


def _to_microbatches(a, axis):
    t = _jnp.moveaxis(a, axis, 0)
    t = t.reshape((N_MICROBATCH, t.shape[0] // N_MICROBATCH) + t.shape[1:])
    return _jnp.moveaxis(t, 1, axis + 1)


def setup_inputs(seed: int = 0) -> dict:
    inp = _fwd_setup_inputs(seed)
    key = _jax.random.fold_in(_jax.random.key(seed), 7919)
    shape, _ = _output_shape()
    out = dict(inp)
    out["loss_target"] = _jax.random.normal(_jax.random.fold_in(key, 0), shape, _jnp.float32)
    for i, name in enumerate(TWIN_WEIGHTS):
        w = inp[name].astype(_jnp.float32)
        if MOMENT_SCALE is None:
            s = _jnp.sqrt(_jnp.mean(_jnp.square(w)) + 1e-30)
        else:
            s = MOMENT_SCALE[name]
        km, kv = _jax.random.split(_jax.random.fold_in(key, i + 1))
        out[name] = w
        out["m_" + name] = s * _jax.random.normal(km, w.shape, _jnp.float32)
        out["v_" + name] = (s * s) * _jax.random.uniform(kv, w.shape, _jnp.float32, 0.5, 1.5)
    if N_MICROBATCH > 1:
        for name, axis in PER_EXAMPLE_BATCH_AXIS.items():
            out[name] = _to_microbatches(out[name], axis)
    return {'x': out['x'], 'c': out['c'], 'ctx': out['ctx'], 'c_ctx': out['c_ctx'], 'w_ada': out['w_ada'], 'b_ada': out['b_ada'], 'ln_gain': out['ln_gain'], 'ln_bias': out['ln_bias'], 's5_lam_re': out['s5_lam_re'], 's5_lam_im': out['s5_lam_im'], 's5_log_dt': out['s5_log_dt'], 's5_b_re': out['s5_b_re'], 's5_b_im': out['s5_b_im'], 's5_c_re': out['s5_c_re'], 's5_c_im': out['s5_c_im'], 's5_d': out['s5_d'], 's5_w_glu': out['s5_w_glu'], 's5_b_glu': out['s5_b_glu'], 'cv_w_pw1': out['cv_w_pw1'], 'cv_b_pw1': out['cv_b_pw1'], 'cv_w_dw': out['cv_w_dw'], 'cv_b_dw': out['cv_b_dw'], 'cv_ln_g': out['cv_ln_g'], 'cv_ln_b': out['cv_ln_b'], 'cv_w_pw2': out['cv_w_pw2'], 'cv_b_pw2': out['cv_b_pw2'], 'mlp_w1': out['mlp_w1'], 'mlp_w2': out['mlp_w2'], 'loss_target': out['loss_target'], 'm_c_ctx': out['m_c_ctx'], 'm_w_ada': out['m_w_ada'], 'm_b_ada': out['m_b_ada'], 'm_ln_gain': out['m_ln_gain'], 'm_ln_bias': out['m_ln_bias'], 'm_s5_lam_re': out['m_s5_lam_re'], 'm_s5_lam_im': out['m_s5_lam_im'], 'm_s5_log_dt': out['m_s5_log_dt'], 'm_s5_b_re': out['m_s5_b_re'], 'm_s5_b_im': out['m_s5_b_im'], 'm_s5_c_re': out['m_s5_c_re'], 'm_s5_c_im': out['m_s5_c_im'], 'm_s5_d': out['m_s5_d'], 'm_s5_w_glu': out['m_s5_w_glu'], 'm_s5_b_glu': out['m_s5_b_glu'], 'm_cv_w_pw1': out['m_cv_w_pw1'], 'm_cv_b_pw1': out['m_cv_b_pw1'], 'm_cv_w_dw': out['m_cv_w_dw'], 'm_cv_b_dw': out['m_cv_b_dw'], 'm_cv_ln_g': out['m_cv_ln_g'], 'm_cv_ln_b': out['m_cv_ln_b'], 'm_cv_w_pw2': out['m_cv_w_pw2'], 'm_cv_b_pw2': out['m_cv_b_pw2'], 'm_mlp_w1': out['m_mlp_w1'], 'm_mlp_w2': out['m_mlp_w2'], 'v_c_ctx': out['v_c_ctx'], 'v_w_ada': out['v_w_ada'], 'v_b_ada': out['v_b_ada'], 'v_ln_gain': out['v_ln_gain'], 'v_ln_bias': out['v_ln_bias'], 'v_s5_lam_re': out['v_s5_lam_re'], 'v_s5_lam_im': out['v_s5_lam_im'], 'v_s5_log_dt': out['v_s5_log_dt'], 'v_s5_b_re': out['v_s5_b_re'], 'v_s5_b_im': out['v_s5_b_im'], 'v_s5_c_re': out['v_s5_c_re'], 'v_s5_c_im': out['v_s5_c_im'], 'v_s5_d': out['v_s5_d'], 'v_s5_w_glu': out['v_s5_w_glu'], 'v_s5_b_glu': out['v_s5_b_glu'], 'v_cv_w_pw1': out['v_cv_w_pw1'], 'v_cv_b_pw1': out['v_cv_b_pw1'], 'v_cv_w_dw': out['v_cv_w_dw'], 'v_cv_b_dw': out['v_cv_b_dw'], 'v_cv_ln_g': out['v_cv_ln_g'], 'v_cv_ln_b': out['v_cv_ln_b'], 'v_cv_w_pw2': out['v_cv_w_pw2'], 'v_cv_b_pw2': out['v_cv_b_pw2'], 'v_mlp_w1': out['v_mlp_w1'], 'v_mlp_w2': out['v_mlp_w2']}


def _loss(weights, diff, rest, loss_target):
    with _jax.named_scope("forward"):
        args = {**rest, TWIN_DIFF_INPUT: diff, **{k: w.astype(_WEIGHT_DTYPES[k]) for k, w in weights.items()}}
        y = _forward(args)
    with _jax.named_scope("loss_head"):
        err = _jnp.square(y.astype(_jnp.float32) - loss_target)
        return 0.5 * _jnp.sum(_jnp.mean(err, axis=-1)) if err.ndim else 0.5 * err


def _adamw(w, g, m, v):
    m = ADAM_B1 * m + (1.0 - ADAM_B1) * g
    v = ADAM_B2 * v + (1.0 - ADAM_B2) * _jnp.square(g)
    m_hat = m / (1.0 - ADAM_B1 ** ADAM_STEP)
    v_hat = v / (1.0 - ADAM_B2 ** ADAM_STEP)
    delta = -ADAM_LR * (m_hat / (_jnp.sqrt(v_hat) + ADAM_EPS) + ADAM_WD * w)
    return delta, m, v


def reference(x, c, ctx, c_ctx, w_ada, b_ada, ln_gain, ln_bias, s5_lam_re, s5_lam_im, s5_log_dt, s5_b_re, s5_b_im, s5_c_re, s5_c_im, s5_d, s5_w_glu, s5_b_glu, cv_w_pw1, cv_b_pw1, cv_w_dw, cv_b_dw, cv_ln_g, cv_ln_b, cv_w_pw2, cv_b_pw2, mlp_w1, mlp_w2, loss_target, m_c_ctx, m_w_ada, m_b_ada, m_ln_gain, m_ln_bias, m_s5_lam_re, m_s5_lam_im, m_s5_log_dt, m_s5_b_re, m_s5_b_im, m_s5_c_re, m_s5_c_im, m_s5_d, m_s5_w_glu, m_s5_b_glu, m_cv_w_pw1, m_cv_b_pw1, m_cv_w_dw, m_cv_b_dw, m_cv_ln_g, m_cv_ln_b, m_cv_w_pw2, m_cv_b_pw2, m_mlp_w1, m_mlp_w2, v_c_ctx, v_w_ada, v_b_ada, v_ln_gain, v_ln_bias, v_s5_lam_re, v_s5_lam_im, v_s5_log_dt, v_s5_b_re, v_s5_b_im, v_s5_c_re, v_s5_c_im, v_s5_d, v_s5_w_glu, v_s5_b_glu, v_cv_w_pw1, v_cv_b_pw1, v_cv_w_dw, v_cv_b_dw, v_cv_ln_g, v_cv_ln_b, v_cv_w_pw2, v_cv_b_pw2, v_mlp_w1, v_mlp_w2):
    given = dict(x=x, c=c, ctx=ctx, c_ctx=c_ctx, w_ada=w_ada, b_ada=b_ada, ln_gain=ln_gain, ln_bias=ln_bias, s5_lam_re=s5_lam_re, s5_lam_im=s5_lam_im, s5_log_dt=s5_log_dt, s5_b_re=s5_b_re, s5_b_im=s5_b_im, s5_c_re=s5_c_re, s5_c_im=s5_c_im, s5_d=s5_d, s5_w_glu=s5_w_glu, s5_b_glu=s5_b_glu, cv_w_pw1=cv_w_pw1, cv_b_pw1=cv_b_pw1, cv_w_dw=cv_w_dw, cv_b_dw=cv_b_dw, cv_ln_g=cv_ln_g, cv_ln_b=cv_ln_b, cv_w_pw2=cv_w_pw2, cv_b_pw2=cv_b_pw2, mlp_w1=mlp_w1, mlp_w2=mlp_w2, loss_target=loss_target, m_c_ctx=m_c_ctx, m_w_ada=m_w_ada, m_b_ada=m_b_ada, m_ln_gain=m_ln_gain, m_ln_bias=m_ln_bias, m_s5_lam_re=m_s5_lam_re, m_s5_lam_im=m_s5_lam_im, m_s5_log_dt=m_s5_log_dt, m_s5_b_re=m_s5_b_re, m_s5_b_im=m_s5_b_im, m_s5_c_re=m_s5_c_re, m_s5_c_im=m_s5_c_im, m_s5_d=m_s5_d, m_s5_w_glu=m_s5_w_glu, m_s5_b_glu=m_s5_b_glu, m_cv_w_pw1=m_cv_w_pw1, m_cv_b_pw1=m_cv_b_pw1, m_cv_w_dw=m_cv_w_dw, m_cv_b_dw=m_cv_b_dw, m_cv_ln_g=m_cv_ln_g, m_cv_ln_b=m_cv_ln_b, m_cv_w_pw2=m_cv_w_pw2, m_cv_b_pw2=m_cv_b_pw2, m_mlp_w1=m_mlp_w1, m_mlp_w2=m_mlp_w2, v_c_ctx=v_c_ctx, v_w_ada=v_w_ada, v_b_ada=v_b_ada, v_ln_gain=v_ln_gain, v_ln_bias=v_ln_bias, v_s5_lam_re=v_s5_lam_re, v_s5_lam_im=v_s5_lam_im, v_s5_log_dt=v_s5_log_dt, v_s5_b_re=v_s5_b_re, v_s5_b_im=v_s5_b_im, v_s5_c_re=v_s5_c_re, v_s5_c_im=v_s5_c_im, v_s5_d=v_s5_d, v_s5_w_glu=v_s5_w_glu, v_s5_b_glu=v_s5_b_glu, v_cv_w_pw1=v_cv_w_pw1, v_cv_b_pw1=v_cv_b_pw1, v_cv_w_dw=v_cv_w_dw, v_cv_b_dw=v_cv_b_dw, v_cv_ln_g=v_cv_ln_g, v_cv_ln_b=v_cv_ln_b, v_cv_w_pw2=v_cv_w_pw2, v_cv_b_pw2=v_cv_b_pw2, v_mlp_w1=v_mlp_w1, v_mlp_w2=v_mlp_w2)
    weights = {n: given[n] for n in TWIN_WEIGHTS}
    shared = {n: given[n] for n in SHARED_INPUTS}
    per_example = {n: given[n] for n in ['x', 'c', 'ctx']}
    grad_fn = _jax.value_and_grad(_loss, argnums=(0, 1))

    def one_microbatch(ex, loss_target):
        ex = dict(ex)
        diff = ex.pop(TWIN_DIFF_INPUT)
        return grad_fn(weights, diff, {**shared, **ex}, loss_target)

    if N_MICROBATCH == 1:
        loss, (grad_w, grad_x) = one_microbatch(per_example, given["loss_target"])
    else:
        def body(carry, xs):
            loss_sum, grad_sum = carry
            l_k, (gw_k, gx_k) = one_microbatch(xs[0], xs[1])
            with _jax.named_scope("update"):
                return (loss_sum + l_k, _jax.tree.map(_jnp.add, grad_sum, gw_k)), gx_k

        init = (_jnp.zeros((), _jnp.float32), _jax.tree.map(_jnp.zeros_like, weights))
        (loss, grad_w), grad_x = _jax.lax.scan(body, init, (per_example, given["loss_target"]))
    with _jax.named_scope("update"):
        delta_w, new_m, new_v = {}, {}, {}
        for n in TWIN_WEIGHTS:
            delta_w[n], new_m[n], new_v[n] = _adamw(weights[n], grad_w[n], given["m_" + n], given["v_" + n])
    return (loss, grad_x, *[grad_w[n] for n in TWIN_WEIGHTS], *[delta_w[n] for n in TWIN_WEIGHTS],
            *[new_m[n] for n in TWIN_WEIGHTS], *[new_v[n] for n in TWIN_WEIGHTS])
```

```python
import functools
import math

import jax
import jax.numpy as jnp
from jax import lax
from jax.experimental import pallas as pl
from jax.experimental.pallas import tpu as pltpu

F32 = jnp.float32
BF16 = jnp.bfloat16
MESH = pl.DeviceIdType.MESH
ANY = pl.BlockSpec(memory_space=pl.ANY)

DEPTH = 4
S5_GROUP = 16
S5_STATE = 64
GRID_W = 64
POS_TEMP = 10000.0
LAMBDA_RE_MAX = -1e-4
LN_EPS = 1e-5
DN_ALPHA = (2.0 * DEPTH) ** 0.25
ADAM_LR, ADAM_B1, ADAM_B2, ADAM_EPS, ADAM_WD, ADAM_STEP = 0.001, 0.9, 0.999, 1e-08, 0.01, 10

SUBLANES = 8
LANES = 128
OCT_CH = 128
OCT_ST = 512
VMEM_LIMIT = 56 * 1024 * 1024


def _cp(n_axes):
    return pltpu.CompilerParams(dimension_semantics=("arbitrary",) * n_axes, vmem_limit_bytes=VMEM_LIMIT)


def _full(shape, single=False):
    nd = len(shape)
    if single:
        return pl.BlockSpec(shape, lambda *i: (0,) * nd, pipeline_mode=pl.Buffered(1))
    return pl.BlockSpec(shape, lambda *i: (0,) * nd)


def _sds(shape, dtype=F32):
    return jax.ShapeDtypeStruct(tuple(shape), dtype)


def _mod(x, sh8, sc8):
    r, d = x.shape
    return (x.reshape(r // 8, 8, d) * (1.0 + sc8[None]) + sh8[None]).reshape(r, d)


def _rowscale(x, g8):
    r, d = x.shape
    return (x.reshape(r // 8, 8, d) * g8[None]).reshape(r, d)


def _sum8(x):
    r, w = x.shape
    return jnp.sum(x.reshape(r // 8, 8, w), axis=0)


def _ln_stats(r):
    mu = jnp.mean(r, axis=-1, keepdims=True)
    xc = r - mu
    var = jnp.mean(xc * xc, axis=-1, keepdims=True)
    rstd = lax.rsqrt(var + LN_EPS)
    return xc * rstd, rstd


def _ln_bwd(dxh, xhat, rstd):
    m1 = jnp.mean(dxh, axis=-1, keepdims=True)
    m2 = jnp.mean(dxh * xhat, axis=-1, keepdims=True)
    return rstd * (dxh - m1 - xhat * m2)


def _sigmoid(x):
    return 1.0 / (1.0 + jnp.exp(-x))


def _gelu(y):
    return 0.5 * y * (1.0 + lax.erf(y * (1.0 / math.sqrt(2.0))))


def _gelu_grad(y):
    return 0.5 * (1.0 + lax.erf(y * (1.0 / math.sqrt(2.0)))) + y * jnp.exp(-0.5 * y * y) * (1.0 / math.sqrt(2.0 * math.pi))


def _dot(a, b):
    return jnp.dot(a, b, preferred_element_type=F32)


def _dot_nt(a, b):
    return lax.dot_general(a, b, (((1,), (1,)), ((), ())), preferred_element_type=F32)


def _dot_tn(a, b):
    return lax.dot_general(a, b, (((0,), (0,)), ((), ())), preferred_element_type=F32)


class _Cfg:
    def __init__(self, x, ctx, mlp_w1, cv_w_dw):
        self.B, self.L, self.D = x.shape
        self.Lc = ctx.shape[1]
        assert self.B * 4 == SUBLANES, "two examples per device, four chunks each"
        self.F = mlp_w1.shape[2] * 4
        self.KW = cv_w_dw.shape[1]
        self.half = self.KW // 2
        self.G = self.D // S5_GROUP
        self.P = S5_STATE
        self.NS = self.G * self.P
        self.NO = self.D // OCT_CH
        assert self.NO % 2 == 0
        self.nx = self.L // 4
        self.nc = self.Lc // 4
        self.Tc = self.B * self.Lc
        self.Tx = self.B * self.L
        self.T = self.Tc + self.Tx
        self.TM = 256 if self.Tc % 256 == 0 else self.Tc
        assert self.Tx % self.TM == 0 and self.TM % 16 == 0
        self.HB = self.TM // 2
        assert SUBLANES * self.half <= self.HB
        self.TW = 512 if (self.Tc % 512 == 0 and self.Tx % 512 == 0) else self.TM

    def ti(self, n):
        t = 16 if self.nc % 16 == 0 else self.nc
        assert n % t == 0 and self.Tc % (8 * t) == 0
        return t

    def rows(self, ctx_too):
        return (0, self.T) if ctx_too else (self.Tc, self.Tx)


def _allgather8(name, x_shard):
    m_per, n = x_shard.shape
    assert m_per % 8 == 0

    def body(x_ref, out_ref, send_sems, recv_sems, local_sem):
        x, y, c = lax.axis_index("x"), lax.axis_index("y"), lax.axis_index("c")
        me, sibling = (x, y, c), (x, y, 1 - c)
        chips = [(1 - x, y), (x, 1 - y), (1 - x, 1 - y)]

        def rows(px, py, pc):
            return out_ref.at[pl.ds((4 * px + 2 * py + pc) * m_per, m_per), :]

        def copy(k, block, to, src=None):
            return pltpu.make_async_remote_copy(
                src_ref=rows(*block) if src is None else src, dst_ref=rows(*block),
                send_sem=send_sems.at[k], recv_sem=recv_sems.at[k], device_id=to, device_id_type=MESH)

        mine = pltpu.make_async_copy(x_ref, rows(*me), local_sem)
        mine.start()
        first = [copy(0, me, sibling, src=x_ref)]
        first += [copy(1 + j, me, (*chip, c), src=x_ref) for j, chip in enumerate(chips)]
        for cp in first:
            cp.start()
        passed = [copy(4 + j, (*chip, c), sibling) for j, chip in enumerate(chips)]
        for j, chip in enumerate(chips):
            copy(1 + j, (*chip, c), me).wait_recv()
            passed[j].start()
        copy(0, sibling, me).wait_recv()
        for j, chip in enumerate(chips):
            copy(4 + j, (*chip, 1 - c), me).wait_recv()
        for cp in first + passed:
            cp.wait_send()
        mine.wait()

    return pl.pallas_call(
        body, name=name, out_shape=_sds((8 * m_per, n), x_shard.dtype),
        in_specs=[pl.BlockSpec(memory_space=pltpu.VMEM)], out_specs=pl.BlockSpec(memory_space=pltpu.VMEM),
        scratch_shapes=[pltpu.SemaphoreType.DMA((7,)), pltpu.SemaphoreType.DMA((7,)), pltpu.SemaphoreType.DMA],
        compiler_params=pltpu.CompilerParams(vmem_limit_bytes=VMEM_LIMIT),
    )(x_shard)


def _peer(axis):
    x, y, c = lax.axis_index("x"), lax.axis_index("y"), lax.axis_index("c")
    return {"x": (1 - x, y, c), "y": (x, 1 - y, c), "c": (x, y, 1 - c)}[axis]


def _pair_exchange(name, axis, inputs, out_shapes, aliases, plan):
    n_in = len(inputs)
    n_out = len(out_shapes)

    def body(*refs):
        ins, outs = refs[:n_in], refs[n_in:n_in + n_out]
        send_sems, recv_sems, local_sems = refs[n_in + n_out:]
        x, y, c = lax.axis_index("x"), lax.axis_index("y"), lax.axis_index("c")
        remote, local = plan(x, y, c, ins, outs)
        lcs = [pltpu.make_async_copy(s, d, local_sems.at[k]) for k, (s, d) in enumerate(local)]
        for cp in lcs:
            cp.start()
        rcs = [pltpu.make_async_remote_copy(src_ref=s, dst_ref=d, send_sem=send_sems.at[k], recv_sem=recv_sems.at[k],
                                            device_id=_peer(axis), device_id_type=MESH) for k, (s, d) in enumerate(remote)]
        for cp in rcs:
            cp.start()
        for cp in rcs:
            cp.wait()
        for cp in lcs:
            cp.wait()

    n_remote, n_local = plan.counts
    return pl.pallas_call(
        body, name=name, out_shape=tuple(out_shapes),
        in_specs=[ANY] * n_in, out_specs=tuple([ANY] * n_out),
        input_output_aliases=dict(aliases),
        scratch_shapes=[pltpu.SemaphoreType.DMA((n_remote,)), pltpu.SemaphoreType.DMA((n_remote,)),
                        pltpu.SemaphoreType.DMA((max(n_local, 1),))],
    )(*inputs)


def _plan(n_remote, n_local=0):
    def deco(fn):
        fn.counts = (n_remote, n_local)
        return fn
    return deco


def _gather_weights(fams):
    nf = len(fams)
    outs = [_sds((4,) + f.shape, f.dtype) for f in fams]

    @_plan(nf, nf)
    def plan_x(x, y, c, ins, outs_):
        s = 2 * x + y
        return ([(ins[k].at[:, c], outs_[k].at[s, :, c]) for k in range(nf)],
                [(ins[k], outs_[k].at[s]) for k in range(nf)])

    full = _pair_exchange("gatherw_x", "x", fams, outs, {}, plan_x)

    @_plan(2 * nf)
    def plan_y(x, y, c, ins, outs_):
        shards = [2 * x + y, 2 * (1 - x) + y]
        return ([(ins[k].at[s, :, c], outs_[k].at[s, :, c]) for k in range(nf) for s in shards], [])

    full = _pair_exchange("gatherw_y", "y", list(full), outs, {k: k for k in range(nf)}, plan_y)

    @_plan(3 * nf)
    def plan_c(x, y, c, ins, outs_):
        shards = [2 * (1 - x) + y, 2 * x + (1 - y), 2 * (1 - x) + (1 - y)]
        return ([(ins[k].at[s, :, c], outs_[k].at[s, :, c]) for k in range(nf) for s in shards], [])

    full = _pair_exchange("gatherw_c", "c", list(full), outs, {k: k for k in range(nf)}, plan_c)
    return list(full)


def _halves_add(name, a, a_sel, r, emit_bf16):
    nr, rows, w = r.shape
    tr = 256 if rows % 256 == 0 else rows
    sel = a_sel

    def body(scal, a_ref, r_ref, *outs):
        s = a_ref[...] + r_ref[...].astype(F32)
        outs[0][...] = s
        if emit_bf16:
            outs[1][...] = s.astype(BF16)

    lead = a.ndim - 2
    a_block = (None,) * lead + (tr, w)
    out_shape = [_sds((nr, rows, w), F32)] + ([_sds((nr, rows, w), BF16)] if emit_bf16 else [])
    gs = pltpu.PrefetchScalarGridSpec(
        num_scalar_prefetch=1, grid=(nr, rows // tr),
        in_specs=[pl.BlockSpec(a_block, lambda j, t, sc: sel(j, sc) + (t, 0)),
                  pl.BlockSpec((None, tr, w), lambda j, t, sc: (j, t, 0))],
        out_specs=[pl.BlockSpec((None, tr, w), lambda j, t, sc: (j, t, 0))] * len(out_shape))
    xyc = jnp.stack([lax.axis_index("x"), lax.axis_index("y"), lax.axis_index("c")]).astype(jnp.int32)
    return pl.pallas_call(body, name=name, grid_spec=gs, out_shape=out_shape, compiler_params=_cp(2))(xyc, a, r)


def _reduce_scatter(tag, grads):
    res = []
    ng = len(grads)
    flat = [g.reshape(4 * g.shape[1], 2, g.shape[3], g.shape[4]) for g in grads]
    ns = [g.shape[1] for g in grads]

    @_plan(ng)
    def plan1(x, y, c, ins, outs_):
        return ([(ins[k].at[:, 1 - c], outs_[k]) for k in range(ng)], [])

    r1 = _pair_exchange(tag + "_rs_c", "c", flat, [_sds((f.shape[0],) + f.shape[2:], F32) for f in flat], {}, plan1)
    p1 = []
    for k in range(ng):
        f = flat[k]
        p, pb = _halves_add(f"{tag}_add1_{k}", f, lambda j, sc: (j, sc[2]), r1[k], True)
        p1.append((p, pb))

    @_plan(ng)
    def plan2(x, y, c, ins, outs_):
        rem = []
        for k in range(ng):
            v = ins[k]
            rem.append((v.at[:, 1 - y], outs_[k]))
        return (rem, [])

    v1 = [pb.reshape(2, 2, ns[k], pb.shape[1], pb.shape[2]) for k, (p, pb) in enumerate(p1)]
    r2 = _pair_exchange(tag + "_rs_y", "y", v1, [_sds((2,) + v.shape[2:], BF16) for v in v1], {}, plan2)
    p2 = []
    for k in range(ng):
        p = p1[k][0].reshape(2, 2, ns[k] * p1[k][0].shape[1], p1[k][0].shape[2])
        r = r2[k].reshape(2, ns[k] * r2[k].shape[2], r2[k].shape[3])
        q, qb = _halves_add(f"{tag}_add2_{k}", p, lambda j, sc: (j, sc[1]), r, True)
        p2.append((q, qb))

    @_plan(ng)
    def plan3(x, y, c, ins, outs_):
        return ([(ins[k].at[1 - x], outs_[k]) for k in range(ng)], [])

    r3 = _pair_exchange(tag + "_rs_x", "x", [qb for q, qb in p2], [_sds(qb.shape[1:], BF16) for q, qb in p2], {}, plan3)
    fin = []
    for k in range(ng):
        q = p2[k][0]
        (f,) = _halves_add(f"{tag}_add3_{k}", q, lambda j, sc: (sc[0],), r3[k][None], False)
        fin.append(f[0])

    @_plan(ng, ng)
    def plan4(x, y, c, ins, outs_):
        return ([(ins[k], outs_[k].at[c]) for k in range(ng)], [(ins[k], outs_[k].at[c]) for k in range(ng)])

    full = _pair_exchange(tag + "_rs_c2", "c", fin, [_sds((2,) + f.shape, F32) for f in fin], {}, plan4)
    for k in range(ng):
        n, h, w = ns[k], grads[k].shape[3], grads[k].shape[4]
        res.append(full[k].reshape(2, n, h, w).swapaxes(0, 1))
    return res


def _allreduce8(tag, buf):
    rows, w = buf.shape
    cur = buf
    for axis in ("c", "y", "x"):
        @_plan(1)
        def plan(x, y, c, ins, outs_):
            return ([(ins[0], outs_[0])], [])

        (got,) = _pair_exchange(f"{tag}_ar_{axis}", axis, [cur], [_sds(cur.shape, F32)], {}, plan)
        cur = _ew(f"{tag}_aradd_{axis}", lambda a, b: a + b, [cur, got], [_sds(cur.shape, F32)])[0]
    return cur


def _ew(name, fn, ins, outs):
    rows, w = ins[0].shape
    tr = rows
    for cand in (512, 256, 128, 64, 32, 16, 8):
        if rows % cand == 0 and rows > cand and cand * w * 4 <= (1 << 20):
            tr = cand
            break
    n_in = len(ins)

    def body(*refs):
        vals = fn(*[r[...] for r in refs[:n_in]])
        if not isinstance(vals, (tuple, list)):
            vals = (vals,)
        for o, v in zip(refs[n_in:], vals):
            o[...] = v.astype(o.dtype)

    spec = pl.BlockSpec((tr, w), lambda i: (i, 0))
    return pl.pallas_call(body, name=name, grid=(rows // tr,), in_specs=[spec] * n_in,
                          out_specs=[spec] * len(outs), out_shape=list(outs), compiler_params=_cp(1))(*ins)


def _as2d(a):
    n = a.size
    if a.ndim >= 2 and a.shape[-1] % LANES == 0:
        return a.reshape(-1, a.shape[-1])
    if n % LANES == 0:
        return a.reshape(-1, LANES)
    return a.reshape(1, n) if a.ndim < 2 else a.reshape(-1, a.shape[-1])


def _adamw(name, w, g, m, v):
    def fn(w, g, m, v):
        m = ADAM_B1 * m + (1.0 - ADAM_B1) * g
        v = ADAM_B2 * v + (1.0 - ADAM_B2) * (g * g)
        m_hat = m / (1.0 - ADAM_B1 ** ADAM_STEP)
        v_hat = v / (1.0 - ADAM_B2 ** ADAM_STEP)
        delta = -ADAM_LR * (m_hat / (jnp.sqrt(v_hat) + ADAM_EPS) + ADAM_WD * w)
        return delta, m, v

    shp = w.shape
    a = [_as2d(t) for t in (w, g, m, v)]
    o = _ew(name, fn, a, [_sds(a[0].shape)] * 3)
    return tuple(t.reshape(shp) for t in o)


def _to_perm(a):
    b, ls, d = a.shape
    n = ls // 4
    return a.reshape(b * 4, n, d).swapaxes(0, 1).reshape(n * 8, d)


def _from_perm(p, b, ls):
    n = ls // 4
    return p.reshape(n, b * 4, p.shape[-1]).swapaxes(0, 1).reshape(b, ls, p.shape[-1])


def _pos_embed(rows, dim):
    def sincos(pos, d):
        quarter = d // 2
        omega = POS_TEMP ** (-jnp.arange(quarter, dtype=F32) / quarter)
        ang = pos[:, None] * omega[None, :]
        return jnp.concatenate([jnp.sin(ang), jnp.cos(ang)], axis=-1)

    row_idx = jnp.repeat(jnp.arange(rows), GRID_W).astype(F32)
    col_idx = jnp.tile(jnp.arange(GRID_W), rows).astype(F32)
    return jnp.concatenate([sincos(row_idx, dim // 2), sincos(col_idx, dim // 2)], axis=-1)


def _stream_of(cfg, off_tiles, tile_rows):
    nct = cfg.Tc // tile_rows
    return lambda i: jnp.where(i + off_tiles >= nct, 1, 0)


def _ada_fwd(c_all, w_ada, b_shard):
    nl, d, w = w_ada.shape
    tn = 512 if w % 512 == 0 else w

    def body(c_ref, w_ref, b_ref, o_ref):
        cv = c_ref[...]
        cond = (cv * _sigmoid(cv)).astype(BF16)
        o_ref[...] = _dot(cond, w_ref[...].astype(BF16)) + b_ref[...]

    return pl.pallas_call(
        body, name="ada_fwd", grid=(nl, w // tn),
        in_specs=[_full(c_all.shape), pl.BlockSpec((None, d, tn), lambda l, j: (l, 0, j)),
                  pl.BlockSpec((None, 1, tn), lambda l, j: (l, 0, j))],
        out_specs=pl.BlockSpec((None, c_all.shape[0], tn), lambda l, j: (l, 0, j)),
        out_shape=_sds((nl, c_all.shape[0], w)), compiler_params=_cp(2))(c_all, w_ada, b_shard)


def _ada_bwd(c_all, dmod_shard, w_ada):
    nl, d, w = w_ada.shape
    tn = 512 if w % 512 == 0 else w
    nr = c_all.shape[0]

    def body(c_ref, dm_ref, w_ref, gw_ref, dc_ref):
        j = pl.program_id(0) * (w // tn) + pl.program_id(1)
        cv = c_ref[...]
        cond = (cv * _sigmoid(cv)).astype(BF16)
        dm = dm_ref[...].astype(BF16)
        gw_ref[...] = _dot_tn(cond, dm)
        part = _dot_nt(dm[16:24], w_ref[...].astype(BF16))

        @pl.when(j == 0)
        def _():
            dc_ref[...] = part

        @pl.when(j > 0)
        def _():
            dc_ref[...] += part

    return pl.pallas_call(
        body, name="ada_bwd", grid=(nl, w // tn),
        in_specs=[_full(c_all.shape), pl.BlockSpec((None, nr, tn), lambda l, j: (l, 0, j)),
                  pl.BlockSpec((None, d, tn), lambda l, j: (l, 0, j))],
        out_specs=[pl.BlockSpec((None, d, tn), lambda l, j: (l, 0, j)), _full((8, d))],
        out_shape=[_sds((nl, d, w)), _sds((8, d))], compiler_params=_cp(2))(c_all, dmod_shard, w_ada)


def _disc(lr, li, ldt, br, bi):
    lr = jnp.minimum(lr, LAMBDA_RE_MAX)
    dt = jnp.exp(ldt)
    mag = jnp.exp(lr * dt)
    abr = mag * jnp.cos(li * dt)
    abi = mag * jnp.sin(li * dt)
    den = lr * lr + li * li
    nr = abr - 1.0
    ni = abi
    cr = (nr * lr + ni * li) / den
    ci = (ni * lr - nr * li) / den
    return abr, abi, cr[None] * br - ci[None] * bi, cr[None] * bi + ci[None] * br


def _disc_fwd(lr, li, ldt, br, bi):
    def body(a, b, c, d, e, o1, o2, o3, o4):
        r = _disc(a[...], b[...], c[...], d[...], e[...])
        o1[...], o2[...], o3[...], o4[...] = r

    return pl.pallas_call(body, name="s5_disc_fwd", out_shape=[_sds(lr.shape), _sds(lr.shape), _sds(br.shape), _sds(br.shape)])(
        lr, li, ldt, br, bi)


def _disc_bwd(lr, li, ldt, br, bi, g_abr, g_abi, g_bbr, g_bbi):
    def body(a, b, c, d, e, g1, g2, g3, g4, o1, o2, o3, o4, o5):
        _, vjp = jax.vjp(_disc, a[...], b[...], c[...], d[...], e[...])
        r = vjp((g1[...], g2[...], g3[...], g4[...]))
        o1[...], o2[...], o3[...], o4[...], o5[...] = r

    return pl.pallas_call(
        body, name="s5_disc_bwd",
        out_shape=[_sds(lr.shape), _sds(li.shape), _sds(ldt.shape), _sds(br.shape), _sds(bi.shape)])(
        lr, li, ldt, br, bi, g_abr, g_abi, g_bbr, g_bbi)


def _s5_layouts(cfg, lam_re, lam_im, log_dt, b_re, b_im):
    P, G = cfg.P, cfg.G
    lr = lam_re.transpose(2, 0, 1).reshape(P, 2 * G)
    li = lam_im.transpose(2, 0, 1).reshape(P, 2 * G)
    ldt = log_dt.reshape(1, 2 * G)
    br = b_re.transpose(3, 2, 0, 1).reshape(S5_GROUP, P, 2 * G)
    bi = b_im.transpose(3, 2, 0, 1).reshape(S5_GROUP, P, 2 * G)
    return lr, li, ldt, br, bi


def _coef_rows(cfg, abr, abi, conj):
    def one(t):
        return t.reshape(cfg.P, 2, cfg.G).transpose(1, 2, 0).reshape(2, cfg.NS)
    a = jnp.stack([one(abr), -one(abi) if conj else one(abi)], axis=1)
    return jnp.broadcast_to(a[:, :, None, :], (2, 2, SUBLANES, cfg.NS))


def _blockdiag_b(cfg, bbr, bbi):
    eye = jnp.eye(8, dtype=F32)

    def one(t):
        t = t.reshape(S5_GROUP, cfg.P, 2, cfg.G).transpose(2, 3, 0, 1)
        t = t.reshape(2, cfg.NO, 8, S5_GROUP, cfg.P)
        return jnp.einsum("dogcp,gh->dogchp", t, eye).reshape(2, cfg.NO, OCT_CH, OCT_ST)

    return jnp.concatenate([one(bbr), one(bbi)], axis=-1).astype(BF16)


def _blockdiag_c(cfg, c_re, c_im):
    eye = jnp.eye(8, dtype=F32)

    def one(t):
        t = t.transpose(0, 1, 3, 2).reshape(2, cfg.NO, 8, cfg.P, S5_GROUP)
        return jnp.einsum("dogpc,gh->dogphc", t, eye).reshape(2, cfg.NO, OCT_ST, OCT_CH)

    return jnp.concatenate([one(c_re), -one(c_im)], axis=2).astype(BF16)


def _diag_b(cfg, dbf):
    eye = jnp.eye(8, dtype=F32)

    def one(t):
        t = t.reshape(2, cfg.NO, 8, S5_GROUP, 8, cfg.P)
        t = jnp.einsum("dogchp,gh->dogcp", t, eye).reshape(2, cfg.G, S5_GROUP, cfg.P)
        return t.transpose(2, 3, 0, 1).reshape(S5_GROUP, cfg.P, 2 * cfg.G)

    return one(dbf[..., :OCT_ST]), one(dbf[..., OCT_ST:])


def _diag_c(cfg, dcf):
    eye = jnp.eye(8, dtype=F32)

    def one(t):
        t = t.reshape(2, cfg.NO, 8, cfg.P, 8, S5_GROUP)
        t = jnp.einsum("dogphc,gh->dogpc", t, eye).reshape(2, cfg.G, cfg.P, S5_GROUP)
        return t.transpose(0, 1, 3, 2)

    return one(dcf[:, :, :OCT_ST]), -one(dcf[:, :, OCT_ST:])


def _recur(buf, st, a_ref, n_oct, ti, rev, store):
    for o in range(0, n_oct, 2):
        cols = [(pl.ds(oo * 2 * OCT_ST, OCT_ST), pl.ds(oo * 2 * OCT_ST + OCT_ST, OCT_ST)) for oo in (o, o + 1)]
        scol = [pl.ds(oo * OCT_ST, OCT_ST) for oo in (o, o + 1)]
        coef = [(a_ref[0, :, sc], a_ref[1, :, sc]) for sc in scol]
        init = (st[0, :, scol[0]], st[1, :, scol[0]], st[0, :, scol[1]], st[1, :, scol[1]])

        def step(i, carry, cols=cols, coef=coef):
            r = pl.multiple_of((i + rev * (ti - 1 - 2 * i)) * 8, 8)
            out = []
            for s in range(2):
                sr, si = carry[2 * s], carry[2 * s + 1]
                ar, ai = coef[s]
                zr = buf[pl.ds(r, 8), cols[s][0]]
                zi = buf[pl.ds(r, 8), cols[s][1]]
                nr = ar * sr - ai * si + zr
                ni = ar * si + ai * sr + zi
                if store:
                    buf[pl.ds(r, 8), cols[s][0]] = nr
                    buf[pl.ds(r, 8), cols[s][1]] = ni
                out += [nr, ni]
            return tuple(out)

        fin = lax.fori_loop(0, ti, step, init, unroll=2)
        st[0, :, scol[0]] = fin[0]
        st[1, :, scol[0]] = fin[1]
        st[0, :, scol[1]] = fin[2]
        st[1, :, scol[1]] = fin[3]


def _s5_fwd_pass(cfg, name, tok, mod8, col_sh, col_sc, bf, acoef, r0, n, s_init=None, cf=None, y_prev=None):
    D, NO, NS = cfg.D, cfg.NO, cfg.NS
    ti = cfg.ti(n)
    nb = n // ti
    R = 8 * ti
    ob = r0 // R
    second = s_init is not None
    blk = lambda d, j: ob + j + d * (nb - 1 - 2 * j)

    def body(*refs):
        if second:
            tok_ref, mod_ref, bf_ref, a_ref, si_ref, cf_ref, yp_ref, y_ref, ck_ref, fin_ref, zbuf, st = refs
        else:
            tok_ref, mod_ref, bf_ref, a_ref, fin_ref, zbuf, st = refs
        d = pl.program_id(0)
        j = pl.program_id(1)

        @pl.when(j == 0)
        def _():
            if second:
                st[...] = si_ref[...]
            else:
                st[...] = jnp.zeros_like(st)

        if second:
            ck_ref[...] = st[...]
        u = _mod(tok_ref[...], mod_ref[:, col_sh:col_sh + D], mod_ref[:, col_sc:col_sc + D]).astype(BF16)
        for o in range(NO):
            zbuf[:, o * 1024:(o + 1) * 1024] = _dot(u[:, o * OCT_CH:(o + 1) * OCT_CH], bf_ref[o])
        _recur(zbuf, st, a_ref, NO, ti, d, second)
        if second:
            for o in range(NO):
                y_ref[:, o * OCT_CH:(o + 1) * OCT_CH] = _dot(zbuf[:, o * 1024:(o + 1) * 1024].astype(BF16), cf_ref[o])

        @pl.when(j == nb - 1)
        def _():
            fin_ref[...] = st[...]

    st_spec = pl.BlockSpec((None, 2, 8, NS), lambda d, j: (d, 0, 0, 0))
    in_specs = [pl.BlockSpec((R, D), lambda d, j: (blk(d, j), 0)), _full(mod8.shape),
                pl.BlockSpec((None, NO, OCT_CH, 1024), lambda d, j: (d, 0, 0, 0)), st_spec]
    args = [tok, mod8, bf, acoef]
    scratch = [pltpu.VMEM((R, NO * 1024), F32), pltpu.VMEM((2, 8, NS), F32)]
    if not second:
        return pl.pallas_call(body, name=name, grid=(2, nb), in_specs=in_specs, out_specs=st_spec,
                              out_shape=_sds((2, 2, 8, NS)), scratch_shapes=scratch, compiler_params=_cp(2))(*args)
    in_specs += [st_spec, pl.BlockSpec((None, NO, 1024, OCT_CH), lambda d, j: (d, 0, 0, 0))]
    args += [s_init, cf]
    aliases = {}
    if y_prev is not None:
        in_specs.append(ANY)
        args.append(y_prev)
        aliases = {6: 0}
    else:
        in_specs.append(_full((8, LANES)))
        args.append(jnp.zeros((8, LANES), F32))
    out_specs = [pl.BlockSpec((None, R, D), lambda d, j: (d, blk(d, j), 0)),
                 pl.BlockSpec((None, None, 2, 8, NS), lambda d, j: (d, j + d * (nb - 1 - 2 * j), 0, 0, 0)), st_spec]
    out_shape = [_sds((2, cfg.T, D)), _sds((2, nb, 2, 8, NS)), _sds((2, 2, 8, NS))]
    return pl.pallas_call(body, name=name, grid=(2, nb), in_specs=in_specs, out_specs=out_specs, out_shape=out_shape,
                          input_output_aliases=aliases, scratch_shapes=scratch, compiler_params=_cp(2))(*args)


def _s5_chain(cfg, name, fin_local, acoef, n, inc, prev_fin=None):
    NS = cfg.NS
    nsq = int(round(math.log2(n)))
    assert 2 ** nsq == n

    def body(*refs):
        if prev_fin is not None:
            f_ref, a_ref, p_ref, o_ref = refs
        else:
            f_ref, a_ref, o_ref = refs
        for d in range(2):
            pr, pi = a_ref[d, 0, 0:1, :], a_ref[d, 1, 0:1, :]
            for _ in range(nsq):
                pr, pi = pr * pr - pi * pi, 2.0 * pr * pi
            for b in range(2):
                order = [4 * b + k for k in range(4)]
                if not inc[d]:
                    order = order[::-1]
                if prev_fin is not None:
                    last = order[-1]
                    sr, si = p_ref[d, 0, last:last + 1, :], p_ref[d, 1, last:last + 1, :]
                else:
                    sr = jnp.zeros((1, NS), F32)
                    si = jnp.zeros((1, NS), F32)
                for k in order:
                    o_ref[d, 0, k:k + 1, :] = sr
                    o_ref[d, 1, k:k + 1, :] = si
                    fr, fi = f_ref[d, 0, k:k + 1, :], f_ref[d, 1, k:k + 1, :]
                    sr, si = pr * sr - pi * si + fr, pr * si + pi * sr + fi

    args = [fin_local, acoef] + ([prev_fin] if prev_fin is not None else [])
    return pl.pallas_call(body, name=name, out_shape=_sds((2, 2, 8, NS)))(*args)


def _s5_forward(cfg, tag, tok, mod8, col_sh, col_sc, bf, cf, acoef):
    saved = {}
    fin_prev = None
    y = None
    for ph, (r0, n) in (("c", (0, cfg.nc)), ("x", (cfg.Tc, cfg.nx))):
        m8 = mod8[0 if ph == "c" else 1]
        loc = _s5_fwd_pass(cfg, f"{tag}_scan1{ph}", tok, m8, col_sh, col_sc, bf, acoef, r0, n)
        s_in = _s5_chain(cfg, f"{tag}_chain{ph}", loc, acoef, n, (True, False), fin_prev)
        y, ck, fin_prev = _s5_fwd_pass(cfg, f"{tag}_scan2{ph}", tok, m8, col_sh, col_sc, bf, acoef, r0, n, s_in, cf, y)
        saved[ph] = ck
    return y, saved


def _s5_bwd_pass(cfg, name, dy, tok, mod8, col_sh, col_sc, bf, cf, acoef, acoef_adj, r0, n, g_init=None, ck=None,
                 du_prev=None):
    D, NO, NS = cfg.D, cfg.NO, cfg.NS
    ti = cfg.ti(n)
    nb = n // ti
    R = 8 * ti
    ob = r0 // R
    second = g_init is not None
    has_dy = dy is not None
    blk = lambda d, j: ob + j + (1 - d) * (nb - 1 - 2 * j)

    def body(*refs):
        refs = list(refs)
        dy_ref = refs.pop(0) if has_dy else None
        if second:
            (tok_ref, mod_ref, bf_ref, cf_ref, a_ref, aa_ref, gi_ref, ck_ref, dup_ref,
             du_ref, da_ref, dbf_ref, dcf_ref, gfin_ref, qbuf, zbuf, gst, hst) = refs
        else:
            cf_ref, aa_ref, gfin_ref, qbuf, gst = refs
        d = pl.program_id(0)
        j = pl.program_id(1)

        @pl.when(j == 0)
        def _():
            if second:
                gst[...] = gi_ref[...]
                da_ref[...] = jnp.zeros_like(da_ref)
                dbf_ref[...] = jnp.zeros_like(dbf_ref)
                dcf_ref[...] = jnp.zeros_like(dcf_ref)
            else:
                gst[...] = jnp.zeros_like(gst)

        if has_dy:
            dyb = dy_ref[...].astype(BF16)
            for o in range(NO):
                qbuf[:, o * 1024:(o + 1) * 1024] = _dot_nt(dyb[:, o * OCT_CH:(o + 1) * OCT_CH], cf_ref[o])
        else:
            qbuf[...] = jnp.zeros_like(qbuf)
        _recur(qbuf, gst, aa_ref, NO, ti, 1 - d, second)

        if second:
            u = _mod(tok_ref[...], mod_ref[:, col_sh:col_sh + D], mod_ref[:, col_sc:col_sc + D]).astype(BF16)
            for o in range(NO):
                zbuf[:, o * 1024:(o + 1) * 1024] = _dot(u[:, o * OCT_CH:(o + 1) * OCT_CH], bf_ref[o])
            hst[...] = ck_ref[...]
            _recur(zbuf, hst, a_ref, NO, ti, d, True)

            def da_acc(g_lo, h_lo, g_edge):
                for o in range(NO):
                    cr, ci = pl.ds(o * 1024, OCT_ST), pl.ds(o * 1024 + OCT_ST, OCT_ST)
                    sc = pl.ds(o * OCT_ST, OCT_ST)
                    gr, gi = qbuf[pl.ds(g_lo, R - 8), cr], qbuf[pl.ds(g_lo, R - 8), ci]
                    hr, hi = zbuf[pl.ds(h_lo, R - 8), cr], zbuf[pl.ds(h_lo, R - 8), ci]
                    er, ei = qbuf[pl.ds(g_edge, 8), cr], qbuf[pl.ds(g_edge, 8), ci]
                    kr, ki = ck_ref[0, :, sc], ck_ref[1, :, sc]
                    da_ref[0, :, sc] += _sum8(gr * hr + gi * hi) + (er * kr + ei * ki)
                    da_ref[1, :, sc] += _sum8(gi * hr - gr * hi) + (ei * kr - er * ki)

            if R > 8:
                @pl.when(d == 0)
                def _():
                    da_acc(8, 0, 0)

                @pl.when(d == 1)
                def _():
                    da_acc(0, 8, R - 8)
            else:
                for o in range(NO):
                    cr, ci = pl.ds(o * 1024, OCT_ST), pl.ds(o * 1024 + OCT_ST, OCT_ST)
                    sc = pl.ds(o * OCT_ST, OCT_ST)
                    er, ei = qbuf[:, cr], qbuf[:, ci]
                    kr, ki = ck_ref[0, :, sc], ck_ref[1, :, sc]
                    da_ref[0, :, sc] += er * kr + ei * ki
                    da_ref[1, :, sc] += ei * kr - er * ki

            for o in range(NO):
                gb = qbuf[:, o * 1024:(o + 1) * 1024].astype(BF16)
                uo = u[:, o * OCT_CH:(o + 1) * OCT_CH]
                dbf_ref[o] += _dot_tn(uo, gb)
                if has_dy:
                    dcf_ref[o] += _dot_tn(zbuf[:, o * 1024:(o + 1) * 1024].astype(BF16), dyb[:, o * OCT_CH:(o + 1) * OCT_CH])
                du_ref[:, o * OCT_CH:(o + 1) * OCT_CH] = _dot_nt(gb, bf_ref[o])

        @pl.when(j == nb - 1)
        def _():
            gfin_ref[...] = gst[...]

    st_spec = pl.BlockSpec((None, 2, 8, NS), lambda d, j: (d, 0, 0, 0))
    row_spec = pl.BlockSpec((R, D), lambda d, j: (blk(d, j), 0))
    bf_spec = pl.BlockSpec((None, NO, OCT_CH, 1024), lambda d, j: (d, 0, 0, 0))
    cf_spec = pl.BlockSpec((None, NO, 1024, OCT_CH), lambda d, j: (d, 0, 0, 0))
    in_specs, args = [], []
    if has_dy:
        in_specs.append(row_spec)
        args.append(dy)
    if not second:
        in_specs += [cf_spec, st_spec]
        args += [cf, acoef_adj]
        return pl.pallas_call(body, name=name, grid=(2, nb), in_specs=in_specs, out_specs=st_spec,
                              out_shape=_sds((2, 2, 8, NS)),
                              scratch_shapes=[pltpu.VMEM((R, NO * 1024), F32), pltpu.VMEM((2, 8, NS), F32)],
                              compiler_params=_cp(2))(*args)
    ck_spec = pl.BlockSpec((None, None, 2, 8, NS), lambda d, j: (d, j + (1 - d) * (nb - 1 - 2 * j), 0, 0, 0))
    in_specs += [row_spec, _full(mod8.shape), bf_spec, cf_spec, st_spec, st_spec, st_spec, ck_spec]
    args += [tok, mod8, bf, cf, acoef, acoef_adj, g_init, ck]
    n_before = len(args)
    aliases = {}
    if du_prev is not None:
        in_specs.append(ANY)
        args.append(du_prev)
        aliases = {n_before: 0}
    else:
        in_specs.append(_full((8, LANES)))
        args.append(jnp.zeros((8, LANES), F32))
    out_specs = [pl.BlockSpec((None, R, D), lambda d, j: (d, blk(d, j), 0)), st_spec, bf_spec, cf_spec, st_spec]
    out_shape = [_sds((2, cfg.T, D)), _sds((2, 2, 8, NS)), _sds((2, NO, OCT_CH, 1024)), _sds((2, NO, 1024, OCT_CH)),
                 _sds((2, 2, 8, NS))]
    scratch = [pltpu.VMEM((R, NO * 1024), F32), pltpu.VMEM((R, NO * 1024), F32), pltpu.VMEM((2, 8, NS), F32),
               pltpu.VMEM((2, 8, NS), F32)]
    return pl.pallas_call(body, name=name, grid=(2, nb), in_specs=in_specs, out_specs=out_specs, out_shape=out_shape,
                          input_output_aliases=aliases, scratch_shapes=scratch, compiler_params=_cp(2))(*args)


def _s5_backward(cfg, tag, dy, dy_ctx, tok, mod8, col_sh, col_sc, bf, cf, acoef, acoef_adj, saved):
    g_prev = None
    acc = None
    du = None
    for ph, (r0, n) in (("x", (cfg.Tc, cfg.nx)), ("c", (0, cfg.nc))):
        m8 = mod8[0 if ph == "c" else 1]
        dyp = dy if (ph == "x" or dy_ctx) else None
        loc = _s5_bwd_pass(cfg, f"{tag}_adjA{ph}", dyp, tok, m8, col_sh, col_sc, bf, cf, acoef, acoef_adj, r0, n)
        g_in = _s5_chain(cfg, f"{tag}_adjchain{ph}", loc, acoef_adj, n, (False, True), g_prev)
        du, da, dbf, dcf, g_prev = _s5_bwd_pass(cfg, f"{tag}_adjB{ph}", dyp, tok, m8, col_sh, col_sc, bf, cf, acoef,
                                                acoef_adj, r0, n, g_in, saved[ph], du)
        new = (da, dbf, dcf)
        if acc is None:
            acc = new
        else:
            acc = tuple(_ew(f"{tag}_accsum{q}", lambda a, b: a + b, [a.reshape(-1, a.shape[-1]), b.reshape(-1, b.shape[-1])],
                            [_sds((a.size // a.shape[-1], a.shape[-1]))])[0].reshape(a.shape)
                        for q, (a, b) in enumerate(zip(acc, new)))
    return du, acc


def _tok_specs(cfg, rows, width, tile=None):
    tm = tile or cfg.TM
    off = rows[0] // tm
    return pl.BlockSpec((tm, width), lambda i: (i + off, 0)), rows[1] // tm, off


def _mod_spec(cfg, mod8, off):
    st = _stream_of(cfg, off, cfg.TM)
    return pl.BlockSpec((None, 8, mod8.shape[-1]), lambda i: (st(i), 0, 0))


def _glu_ln(cfg, name, rows, tok, y, mod8, cols, dskip, w, b, gain, bias):
    D, TM = cfg.D, cfg.TM
    csh, csc, cg = cols
    spec, nt, off = _tok_specs(cfg, rows, D)
    spec2, _, _ = _tok_specs(cfg, rows, 2 * D)
    hw = w.shape[-1]

    def body(tok_ref, y_ref, mod_ref, ds_ref, w_ref, b_ref, g_ref, bi_ref, x1_ref, r1_ref, mix_ref, zz_ref, zb_ref, yy_ref):
        t = tok_ref[...]
        u = _mod(t, mod_ref[:, csh:csh + D], mod_ref[:, csc:csc + D])
        yy = ds_ref[...] * u + y_ref[0] + y_ref[1]
        zb = _gelu(yy).astype(BF16)
        zz = jnp.concatenate([_dot(zb, w_ref[s]) for s in range(4)], axis=-1) + b_ref[...]
        mix = zz[:, :D] * _sigmoid(zz[:, D:])
        r1 = DN_ALPHA * t + _rowscale(mix, mod_ref[:, cg:cg + D])
        xhat, _ = _ln_stats(r1)
        x1_ref[...] = xhat * g_ref[...] + bi_ref[...]
        r1_ref[...] = r1
        mix_ref[...] = mix
        zz_ref[...] = zz
        zb_ref[...] = zb
        yy_ref[...] = yy

    T = cfg.T
    return pl.pallas_call(
        body, name=name, grid=(nt,),
        in_specs=[spec, pl.BlockSpec((2, TM, D), lambda i: (0, i + off, 0)), _mod_spec(cfg, mod8, off), _full((1, D)),
                  _full(w.shape, True), _full((1, 2 * D)), _full((1, D)), _full((1, D))],
        out_specs=[spec, spec, spec, spec2, spec, spec],
        out_shape=[_sds((T, D)), _sds((T, D)), _sds((T, D)), _sds((T, 2 * D)), _sds((T, D), BF16), _sds((T, D))],
        compiler_params=_cp(1))(tok, y, mod8, dskip, w, b, gain, bias)


def _mlp_ln(cfg, name, rows, x1, mod8, cols, w1, w2, gain, bias):
    D, TM = cfg.D, cfg.TM
    csh, csc, cg = cols
    spec, nt, off = _tok_specs(cfg, rows, D)
    specf, _, _ = _tok_specs(cfg, rows, cfg.F)
    fb = cfg.F // 4

    def body(x_ref, mod_ref, w1_ref, w2_ref, g_ref, bi_ref, x2_ref, r2_ref, out_ref, a_ref, h_ref):
        t = x_ref[...]
        h = _mod(t, mod_ref[:, csh:csh + D], mod_ref[:, csc:csc + D]).astype(BF16)
        out = jnp.zeros((TM, D), F32)
        for s in range(4):
            hid = jnp.maximum(_dot(h, w1_ref[s]), 0.0)
            a = (hid * hid).astype(BF16)
            a_ref[:, s * fb:(s + 1) * fb] = a
            out = out + _dot(a, w2_ref[s])
        r2 = DN_ALPHA * t + _rowscale(out, mod_ref[:, cg:cg + D])
        xhat, _ = _ln_stats(r2)
        x2_ref[...] = xhat * g_ref[...] + bi_ref[...]
        r2_ref[...] = r2
        out_ref[...] = out
        h_ref[...] = h

    T = cfg.T
    return pl.pallas_call(
        body, name=name, grid=(nt,),
        in_specs=[spec, _mod_spec(cfg, mod8, off), _full(w1.shape, True), _full(w2.shape, True), _full((1, D)), _full((1, D))],
        out_specs=[spec, spec, spec, specf, spec],
        out_shape=[_sds((T, D)), _sds((T, D)), _sds((T, D)), _sds((T, cfg.F), BF16), _sds((T, D), BF16)],
        compiler_params=_cp(1))(x1, mod8, w1, w2, gain, bias)


def _pw1_glu(cfg, name, rows, tok, mod8, cols, w, b):
    D, TM = cfg.D, cfg.TM
    csh, csc = cols
    spec, nt, off = _tok_specs(cfg, rows, D)
    spec2, _, _ = _tok_specs(cfg, rows, 2 * D)

    def body(tok_ref, mod_ref, w_ref, b_ref, aa_ref, ag_ref, h_ref):
        h = _mod(tok_ref[...], mod_ref[:, csh:csh + D], mod_ref[:, csc:csc + D]).astype(BF16)
        aa = jnp.concatenate([_dot(h, w_ref[s]) for s in range(4)], axis=-1) + b_ref[...]
        aa_ref[...] = aa
        ag_ref[...] = aa[:, :D] * _sigmoid(aa[:, D:])
        h_ref[...] = h

    T = cfg.T
    return pl.pallas_call(
        body, name=name, grid=(nt,),
        in_specs=[spec, _mod_spec(cfg, mod8, off), _full(w.shape, True), _full((1, 2 * D))],
        out_specs=[spec2, spec, spec],
        out_shape=[_sds((T, 2 * D)), _sds((T, D)), _sds((T, D), BF16)], compiler_params=_cp(1))(tok, mod8, w, b)


def _halo_maps(cfg, rows):
    TM, HB = cfg.TM, cfg.HB
    off = rows[0] // TM
    nct = cfg.Tc // TM
    ntx = cfg.Tx // TM

    def phase(i):
        t = i + off
        is_x = t >= nct
        first = jnp.where(is_x, nct, 0)
        cnt = jnp.where(is_x, ntx, nct)
        return t, first, cnt

    def prev(i):
        t, first, cnt = phase(i)
        return jnp.where(t == first, 2 * (first + cnt) - 1, 2 * t - 1), 0

    def nxt(i):
        t, first, cnt = phase(i)
        return jnp.where(t == first + cnt - 1, 2 * first, 2 * t + 2), 0

    def edge(i):
        t, first, cnt = phase(i)
        return t == first, t == first + cnt - 1

    return prev, nxt, edge, off


def _halo_fix(prev, nxt, is_first, is_last):
    hb, d = prev.shape
    k = lax.broadcasted_iota(jnp.int32, (hb // 8, 8, d), 1)
    p3 = prev.reshape(hb // 8, 8, d)
    n3 = nxt.reshape(hb // 8, 8, d)
    p_roll = jnp.where((k % 4) == 0, 0.0, pltpu.roll(p3, 1, 1))
    n_roll = jnp.where((k % 4) == 3, 0.0, pltpu.roll(n3, 7, 1))
    p3 = jnp.where(is_first, p_roll, p3)
    n3 = jnp.where(is_last, n_roll, n3)
    return p3.reshape(hb, d), n3.reshape(hb, d)


def _dwconv_ln(cfg, name, rows, ag, w_dw, b_dw, ln_g, ln_b):
    D, TM, HB, KW, half = cfg.D, cfg.TM, cfg.HB, cfg.KW, cfg.half
    prev_map, next_map, edge, off = _halo_maps(cfg, rows)
    spec, nt, _ = _tok_specs(cfg, rows, D)

    def body(cur_ref, prev_ref, next_ref, w_ref, b_ref, g_ref, bi_ref, cv_ref, s_ref, ext):
        i = pl.program_id(0)
        is_first, is_last = edge(i)
        p, n = _halo_fix(prev_ref[...], next_ref[...], is_first, is_last)
        ext[0:HB, :] = p
        ext[HB:HB + TM, :] = cur_ref[...]
        ext[HB + TM:, :] = n
        acc = jnp.zeros((TM, D), F32)
        for k in range(KW):
            lo = HB + 8 * (k - half)
            acc = acc + w_ref[k:k + 1, :] * ext[lo:lo + TM, :]
        cv = acc + b_ref[...]
        xhat, _ = _ln_stats(cv)
        nn = xhat * g_ref[...] + bi_ref[...]
        cv_ref[...] = cv
        s_ref[...] = (nn * _sigmoid(nn)).astype(BF16)

    T = cfg.T
    return pl.pallas_call(
        body, name=name, grid=(nt,),
        in_specs=[spec, pl.BlockSpec((HB, D), prev_map), pl.BlockSpec((HB, D), next_map), _full((KW, D)),
                  _full((1, D)), _full((1, D)), _full((1, D))],
        out_specs=[spec, spec], out_shape=[_sds((T, D)), _sds((T, D), BF16)],
        scratch_shapes=[pltpu.VMEM((TM + 2 * HB, D), F32)], compiler_params=_cp(1))(ag, ag, ag, w_dw, b_dw, ln_g, ln_b)


def _pw2_ln(cfg, name, rows, s, tok, mod8, cg, w, b, gain, bias):
    D, TM = cfg.D, cfg.TM
    spec, nt, off = _tok_specs(cfg, rows, D)
    kb = D // 4

    def body(s_ref, tok_ref, mod_ref, w_ref, b_ref, g_ref, bi_ref, x1_ref, r1_ref, mix_ref):
        sv = s_ref[...]
        mix = b_ref[...] + jnp.zeros((TM, D), F32)
        for q in range(4):
            mix = mix + _dot(sv[:, q * kb:(q + 1) * kb], w_ref[q])
        r1 = DN_ALPHA * tok_ref[...] + _rowscale(mix, mod_ref[:, cg:cg + D])
        xhat, _ = _ln_stats(r1)
        x1_ref[...] = xhat * g_ref[...] + bi_ref[...]
        r1_ref[...] = r1
        mix_ref[...] = mix

    T = cfg.T
    return pl.pallas_call(
        body, name=name, grid=(nt,),
        in_specs=[spec, spec, _mod_spec(cfg, mod8, off), _full(w.shape, True), _full((1, D)), _full((1, D)), _full((1, D))],
        out_specs=[spec, spec, spec], out_shape=[_sds((T, D))] * 3, compiler_params=_cp(1))(s, tok, mod8, w, b, gain, bias)


def _loss(cfg, xf, tgt):
    D, TM = cfg.D, cfg.TM
    spec, nt, off = _tok_specs(cfg, cfg.rows(False), D)

    def body(x_ref, t_ref, l_ref, dx_ref, acc):
        i = pl.program_id(0)
        dlt = x_ref[...] - t_ref[...]

        @pl.when(i == 0)
        def _():
            acc[...] = jnp.zeros_like(acc)

        acc[...] += _sum8(dlt * dlt)
        dx_ref[...] = dlt * (1.0 / D)

        @pl.when(i == nt - 1)
        def _():
            l_ref[...] = jnp.zeros((8, LANES), F32) + jnp.sum(acc[...]) * (0.5 / D)

    return pl.pallas_call(
        body, name="loss", grid=(nt,),
        in_specs=[spec, pl.BlockSpec((TM, D), lambda i: (i, 0))],
        out_specs=[_full((8, LANES)), spec], out_shape=[_sds((8, LANES)), _sds((cfg.T, D))],
        scratch_shapes=[pltpu.VMEM((8, D), F32)], compiler_params=_cp(1))(xf, tgt)


def _masked_spec(cfg, rows, width, valid_from_tile):
    tm = cfg.TM
    off = rows[0] // tm
    return pl.BlockSpec((tm, width), lambda i: (jnp.maximum(i + off, valid_from_tile), 0))


def _lnb(cfg, name, rows, dres, dres_ctx_ok, dh, r, aux, gain, mod_gate, cg, mod_next, csc):
    D, TM = cfg.D, cfg.TM
    spec, nt, off = _tok_specs(cfg, rows, D)
    nct = cfg.Tc // TM
    has_dres, has_dh = dres is not None, dh is not None

    def body(*refs):
        refs = list(refs)
        dres_ref = refs.pop(0) if has_dres else None
        dh_ref = refs.pop(0) if has_dh else None
        r_ref, aux_ref, g_ref, mg_ref = refs[:4]
        refs = refs[4:]
        mn_ref = refs.pop(0) if has_dh else None
        dprev_ref, dbr_ref, dgain_ref, dbias_ref, dg_ref, dsc_ref, dsh_ref, acc_g, acc_b = refs
        i = pl.program_id(0)
        t = i + off
        first_of_stream = (i == 0) | (t == nct)
        xhat, rstd = _ln_stats(r_ref[...])
        dy = jnp.zeros((TM, D), F32)
        if has_dres:
            dv = dres_ref[...]
            if not dres_ctx_ok:
                dv = jnp.where(t >= nct, dv, 0.0)
            dy = dy + dv
        if has_dh:
            dhv = dh_ref[...]
            dy = dy + _rowscale(dhv, 1.0 + mn_ref[:, csc:csc + D])
            x_out = xhat * g_ref[0:1, :] + g_ref[1:2, :]
            s_sc, s_sh = _sum8(dhv * x_out), _sum8(dhv)
        else:
            s_sc = s_sh = jnp.zeros((8, D), F32)
        dr = _ln_bwd(dy * g_ref[0:1, :], xhat, rstd)
        s_g = _sum8(dr * aux_ref[...])

        @pl.when(i == 0)
        def _():
            acc_g[...] = jnp.zeros_like(acc_g)
            acc_b[...] = jnp.zeros_like(acc_b)

        acc_g[...] += _sum8(dy * xhat)
        acc_b[...] += _sum8(dy)

        @pl.when(first_of_stream)
        def _():
            dg_ref[...] = s_g
            dsc_ref[...] = s_sc
            dsh_ref[...] = s_sh

        @pl.when(jnp.logical_not(first_of_stream))
        def _():
            dg_ref[...] += s_g
            dsc_ref[...] += s_sc
            dsh_ref[...] += s_sh

        dprev_ref[...] = DN_ALPHA * dr
        dbr_ref[...] = _rowscale(dr, mg_ref[:, cg:cg + D])

        @pl.when(i == nt - 1)
        def _():
            dgain_ref[...] = jnp.sum(acc_g[...], axis=0, keepdims=True)
            dbias_ref[...] = jnp.sum(acc_b[...], axis=0, keepdims=True)

    st = _stream_of(cfg, off, TM)
    in_specs, args = [], []
    if has_dres:
        in_specs.append(spec if dres_ctx_ok else _masked_spec(cfg, rows, D, nct))
        args.append(dres)
    if has_dh:
        in_specs.append(spec)
        args.append(dh)
    in_specs += [spec, spec, _full((2, D)), _mod_spec(cfg, mod_gate, off)]
    args += [r, aux, gain, mod_gate]
    if has_dh:
        in_specs.append(_mod_spec(cfg, mod_next, off))
        args.append(mod_next)
    acc_spec = pl.BlockSpec((None, 8, D), lambda i: (st(i), 0, 0))
    T = cfg.T
    return pl.pallas_call(
        body, name=name, grid=(nt,), in_specs=in_specs,
        out_specs=[spec, spec, _full((1, D)), _full((1, D)), acc_spec, acc_spec, acc_spec],
        out_shape=[_sds((T, D)), _sds((T, D)), _sds((1, D)), _sds((1, D)), _sds((2, 8, D)), _sds((2, 8, D)), _sds((2, 8, D))],
        scratch_shapes=[pltpu.VMEM((8, D), F32), pltpu.VMEM((8, D), F32)], compiler_params=_cp(1))(*args)


def _mlp_bwd(cfg, name, rows, dbr, a, w1, w2):
    D, TM = cfg.D, cfg.TM
    spec, nt, off = _tok_specs(cfg, rows, D)
    specf, _, _ = _tok_specs(cfg, rows, cfg.F)
    fb = cfg.F // 4

    def body(d_ref, a_ref, w1_ref, w2_ref, dh_ref, dhid_ref, dout_ref):
        dout = d_ref[...].astype(BF16)
        dh = jnp.zeros((TM, D), F32)
        for s in range(4):
            da = _dot_nt(dout, w2_ref[s])
            dhid = (da * (2.0 * jnp.sqrt(a_ref[:, s * fb:(s + 1) * fb].astype(F32)))).astype(BF16)
            dhid_ref[:, s * fb:(s + 1) * fb] = dhid
            dh = dh + _dot_nt(dhid, w1_ref[s])
        dh_ref[...] = dh
        dout_ref[...] = dout

    T = cfg.T
    return pl.pallas_call(
        body, name=name, grid=(nt,),
        in_specs=[spec, specf, _full(w1.shape, True), _full(w2.shape, True)],
        out_specs=[spec, specf, spec],
        out_shape=[_sds((T, D)), _sds((T, cfg.F), BF16), _sds((T, D), BF16)], compiler_params=_cp(1))(dbr, a, w1, w2)


def _wgrad(cfg, name, rows, a, b, mode, fam, slot):
    tw = cfg.TW
    off = rows[0] // tw
    nt = rows[1] // tw
    _, n, kk, nn = fam.shape

    def body(a_ref, b_ref, f_ref, o_ref):
        t = pl.program_id(1)
        part = _dot_tn(a_ref[...], b_ref[...])

        @pl.when(t == 0)
        def _():
            o_ref[...] = part

        @pl.when(t > 0)
        def _():
            o_ref[...] += part

    if mode == "col":
        a_spec = pl.BlockSpec((tw, kk), lambda s, t: (t + off, 0))
        b_spec = pl.BlockSpec((tw, nn), lambda s, t: (t + off, s))
    else:
        a_spec = pl.BlockSpec((tw, kk), lambda s, t: (t + off, s))
        b_spec = pl.BlockSpec((tw, nn), lambda s, t: (t + off, 0))
    return pl.pallas_call(
        body, name=name, grid=(4, nt), in_specs=[a_spec, b_spec, ANY],
        out_specs=pl.BlockSpec((None, None, kk, nn), lambda s, t: (s, slot, 0, 0)),
        out_shape=_sds(fam.shape), input_output_aliases={2: 0}, compiler_params=_cp(2))(a, b, fam)


def _glu_bwd(cfg, name, rows, dmix, pre, w, yy=None):
    D, TM = cfg.D, cfg.TM
    spec, nt, off = _tok_specs(cfg, rows, D)
    spec2, _, _ = _tok_specs(cfg, rows, 2 * D)
    hw = w.shape[-1]
    has_y = yy is not None

    def body(*refs):
        refs = list(refs)
        d_ref, p_ref, w_ref = refs[:3]
        y_ref = refs[3] if has_y else None
        dz_ref, dp_ref, db_ref, acc = refs[-4:]
        i = pl.program_id(0)
        dm = d_ref[...]
        po, pg = p_ref[:, :D], p_ref[:, D:]
        sg = _sigmoid(pg)
        dpre = jnp.concatenate([dm * sg, dm * po * sg * (1.0 - sg)], axis=-1)

        @pl.when(i == 0)
        def _():
            acc[...] = jnp.zeros_like(acc)

        acc[...] += _sum8(dpre)
        dpb = dpre.astype(BF16)
        dz = jnp.zeros((TM, D), F32)
        for s in range(4):
            dz = dz + _dot_nt(dpb[:, s * hw:(s + 1) * hw], w_ref[s])
        if has_y:
            dz = dz * _gelu_grad(y_ref[...])
        dz_ref[...] = dz
        dp_ref[...] = dpb

        @pl.when(i == nt - 1)
        def _():
            db_ref[...] = jnp.sum(acc[...], axis=0, keepdims=True)

    T = cfg.T
    in_specs = [spec, spec2, _full(w.shape, True)] + ([spec] if has_y else [])
    args = [dmix, pre, w] + ([yy] if has_y else [])
    return pl.pallas_call(
        body, name=name, grid=(nt,), in_specs=in_specs, out_specs=[spec, spec2, _full((1, 2 * D))],
        out_shape=[_sds((T, D)), _sds((T, 2 * D), BF16), _sds((1, 2 * D))],
        scratch_shapes=[pltpu.VMEM((8, 2 * D), F32)], compiler_params=_cp(1))(*args)


def _s5_du(cfg, name, rows, du, dy, dy_from_tile, tok, mod8, cols, dskip):
    D, TM = cfg.D, cfg.TM
    csh, csc = cols
    spec, nt, off = _tok_specs(cfg, rows, D)

    def body(du_ref, dy_ref, tok_ref, mod_ref, ds_ref, dh_ref, dd_ref, acc):
        i = pl.program_id(0)
        dyv = jnp.where(i + off >= dy_from_tile, dy_ref[...], 0.0)
        u = _mod(tok_ref[...], mod_ref[:, csh:csh + D], mod_ref[:, csc:csc + D])
        dh_ref[...] = du_ref[0] + du_ref[1] + ds_ref[...] * dyv

        @pl.when(i == 0)
        def _():
            acc[...] = jnp.zeros_like(acc)

        acc[...] += _sum8(dyv * u)

        @pl.when(i == nt - 1)
        def _():
            dd_ref[...] = jnp.sum(acc[...], axis=0, keepdims=True)

    T = cfg.T
    return pl.pallas_call(
        body, name=name, grid=(nt,),
        in_specs=[pl.BlockSpec((2, TM, D), lambda i: (0, i + off, 0)), _masked_spec(cfg, rows, D, dy_from_tile), spec,
                  _mod_spec(cfg, mod8, off), _full((1, D))],
        out_specs=[spec, _full((1, D))], out_shape=[_sds((T, D)), _sds((1, D))],
        scratch_shapes=[pltpu.VMEM((8, D), F32)], compiler_params=_cp(1))(du, dy, tok, mod8, dskip)


def _pw2_bwd(cfg, name, rows, dmix, cv, w, ln_g, ln_b):
    D, TM = cfg.D, cfg.TM
    spec, nt, off = _tok_specs(cfg, rows, D)
    kb = D // 4

    def body(d_ref, cv_ref, w_ref, g_ref, b_ref, dcv_ref, dmb_ref, sums_ref, acc):
        i = pl.program_id(0)
        dm = d_ref[...]
        dmb = dm.astype(BF16)
        ds = jnp.concatenate([_dot_nt(dmb, w_ref[q]) for q in range(4)], axis=-1)
        xhat, rstd = _ln_stats(cv_ref[...])
        nn = xhat * g_ref[...] + b_ref[...]
        sg = _sigmoid(nn)
        dn = ds * (sg * (1.0 + nn * (1.0 - sg)))
        dcv = _ln_bwd(dn * g_ref[...], xhat, rstd)

        @pl.when(i == 0)
        def _():
            acc[...] = jnp.zeros_like(acc)

        acc[0] += _sum8(dn * xhat)
        acc[1] += _sum8(dn)
        acc[2] += _sum8(dcv)
        acc[3] += _sum8(dm)
        dcv_ref[...] = dcv
        dmb_ref[...] = dmb

        @pl.when(i == nt - 1)
        def _():
            for q in range(4):
                sums_ref[q:q + 1, :] = jnp.sum(acc[q], axis=0, keepdims=True)

    T = cfg.T
    return pl.pallas_call(
        body, name=name, grid=(nt,),
        in_specs=[spec, spec, _full(w.shape, True), _full((1, D)), _full((1, D))],
        out_specs=[spec, spec, _full((4, D))], out_shape=[_sds((T, D)), _sds((T, D), BF16), _sds((4, D))],
        scratch_shapes=[pltpu.VMEM((4, 8, D), F32)], compiler_params=_cp(1))(dmix, cv, w, ln_g, ln_b)


def _dwconv_bwd(cfg, name, rows, dcv, ag, w_dw):
    D, TM, HB, KW, half = cfg.D, cfg.TM, cfg.HB, cfg.KW, cfg.half
    prev_map, next_map, edge, off = _halo_maps(cfg, rows)
    spec, nt, _ = _tok_specs(cfg, rows, D)

    def body(dc_ref, dp_ref, dn_ref, ac_ref, ap_ref, an_ref, w_ref, dag_ref, dw_ref, extd, exta, acc):
        i = pl.program_id(0)
        is_first, is_last = edge(i)
        p, n = _halo_fix(dp_ref[...], dn_ref[...], is_first, is_last)
        extd[0:HB, :] = p
        extd[HB:HB + TM, :] = dc_ref[...]
        extd[HB + TM:, :] = n
        p, n = _halo_fix(ap_ref[...], an_ref[...], is_first, is_last)
        exta[0:HB, :] = p
        exta[HB:HB + TM, :] = ac_ref[...]
        exta[HB + TM:, :] = n

        @pl.when(i == 0)
        def _():
            acc[...] = jnp.zeros_like(acc)

        dcur = dc_ref[...]
        dag = jnp.zeros((TM, D), F32)
        for k in range(KW):
            lo = HB + 8 * (half - k)
            dag = dag + w_ref[k:k + 1, :] * extd[lo:lo + TM, :]
            la = HB + 8 * (k - half)
            acc[k] += _sum8(dcur * exta[la:la + TM, :])
        dag_ref[...] = dag

        @pl.when(i == nt - 1)
        def _():
            for k in range(KW):
                dw_ref[k:k + 1, :] = jnp.sum(acc[k], axis=0, keepdims=True)

    T = cfg.T
    hp, hn = pl.BlockSpec((HB, D), prev_map), pl.BlockSpec((HB, D), next_map)
    return pl.pallas_call(
        body, name=name, grid=(nt,), in_specs=[spec, hp, hn, spec, hp, hn, _full((KW, D))],
        out_specs=[spec, _full((KW, D))], out_shape=[_sds((T, D)), _sds((KW, D))],
        scratch_shapes=[pltpu.VMEM((TM + 2 * HB, D), F32), pltpu.VMEM((TM + 2 * HB, D), F32), pltpu.VMEM((KW, 8, D), F32)],
        compiler_params=_cp(1))(dcv, dcv, dcv, ag, ag, ag, w_dw)


def _input_bwd(cfg, dres, dh, tok0, mod8, csc):
    D, TM = cfg.D, cfg.TM
    rows = cfg.rows(True)
    spec, nt, off = _tok_specs(cfg, rows, D)
    nct = cfg.Tc // TM
    st = _stream_of(cfg, off, TM)

    def body(dr_ref, dh_ref, t_ref, mod_ref, gx_ref, dsc_ref, dsh_ref):
        i = pl.program_id(0)
        dhv = dh_ref[...]
        gx_ref[...] = dr_ref[...] + _rowscale(dhv, 1.0 + mod_ref[:, csc:csc + D])
        first = (i == 0) | (i == nct)
        s_sc, s_sh = _sum8(dhv * t_ref[...]), _sum8(dhv)

        @pl.when(first)
        def _():
            dsc_ref[...] = s_sc
            dsh_ref[...] = s_sh

        @pl.when(jnp.logical_not(first))
        def _():
            dsc_ref[...] += s_sc
            dsh_ref[...] += s_sh

    acc_spec = pl.BlockSpec((None, 8, D), lambda i: (st(i), 0, 0))
    return pl.pallas_call(
        body, name="input_bwd", grid=(nt,), in_specs=[spec, spec, spec, _mod_spec(cfg, mod8, off)],
        out_specs=[spec, acc_spec, acc_spec], out_shape=[_sds((cfg.T, D)), _sds((2, 8, D)), _sds((2, 8, D))],
        compiler_params=_cp(1))(dres, dh, tok0, mod8)


def _dmod_rows(dmod8):
    nl, _, _, w = dmod8.shape

    def body(d_ref, o_ref):
        xs = d_ref[1]
        cs = d_ref[0]
        o_ref[...] = jnp.zeros((8, w), F32)
        o_ref[0:1, :] = jnp.sum(xs[0:4], axis=0, keepdims=True)
        o_ref[1:2, :] = jnp.sum(xs[4:8], axis=0, keepdims=True)
        o_ref[2:3, :] = jnp.sum(cs, axis=0, keepdims=True)

    return pl.pallas_call(body, name="dmod_rows", grid=(nl,),
                          in_specs=[pl.BlockSpec((None, 2, 8, w), lambda l: (l, 0, 0, 0))],
                          out_specs=pl.BlockSpec((None, 8, w), lambda l: (l, 0, 0)), out_shape=_sds((nl, 8, w)),
                          compiler_params=_cp(1))(dmod8)


def _x_only(acc):
    return jnp.concatenate([jnp.zeros_like(acc[:1]), acc[1:]], axis=0)


def _pack(parts):
    bufs, meta, off = [], [], 0
    for p in parts:
        n = p.size
        rows = -(-n // (8 * LANES)) * 8
        flat = jnp.pad(p.reshape(-1).astype(F32), (0, rows * LANES - n)).reshape(rows, LANES)
        bufs.append(flat)
        meta.append((off, rows, p.shape))
        off += rows
    return jnp.concatenate(bufs, axis=0), meta


def _unpack(buf, meta):
    out = []
    for off, rows, shape in meta:
        n = 1
        for s in shape:
            n *= s
        out.append(buf[off:off + rows].reshape(-1)[:n].reshape(shape))
    return out


def kernel(x, c, ctx, c_ctx, w_ada, b_ada, ln_gain, ln_bias, s5_lam_re, s5_lam_im, s5_log_dt, s5_b_re, s5_b_im, s5_c_re, s5_c_im, s5_d, s5_w_glu, s5_b_glu, cv_w_pw1, cv_b_pw1, cv_w_dw, cv_b_dw, cv_ln_g, cv_ln_b, cv_w_pw2, cv_b_pw2, mlp_w1, mlp_w2, loss_target, m_c_ctx, m_w_ada, m_b_ada, m_ln_gain, m_ln_bias, m_s5_lam_re, m_s5_lam_im, m_s5_log_dt, m_s5_b_re, m_s5_b_im, m_s5_c_re, m_s5_c_im, m_s5_d, m_s5_w_glu, m_s5_b_glu, m_cv_w_pw1, m_cv_b_pw1, m_cv_w_dw, m_cv_b_dw, m_cv_ln_g, m_cv_ln_b, m_cv_w_pw2, m_cv_b_pw2, m_mlp_w1, m_mlp_w2, v_c_ctx, v_w_ada, v_b_ada, v_ln_gain, v_ln_bias, v_s5_lam_re, v_s5_lam_im, v_s5_log_dt, v_s5_b_re, v_s5_b_im, v_s5_c_re, v_s5_c_im, v_s5_d, v_s5_w_glu, v_s5_b_glu, v_cv_w_pw1, v_cv_b_pw1, v_cv_w_dw, v_cv_b_dw, v_cv_ln_g, v_cv_ln_b, v_cv_w_pw2, v_cv_b_pw2, v_mlp_w1, v_mlp_w2):
    cfg = _Cfg(x, ctx, mlp_w1, cv_w_dw)
    D, T, Tc, Tx, B = cfg.D, cfg.T, cfg.Tc, cfg.Tx, cfg.B
    ax, ay, ac = lax.axis_index("x"), lax.axis_index("y"), lax.axis_index("c")
    shard = 2 * ax + ay
    dev = 4 * ax + 2 * ay + ac
    Ds = D // 4
    Wa = w_ada.shape[2]

    c_pad = jnp.concatenate([c, jnp.zeros((8 - B, D), F32)], axis=0)
    c_gath = _allgather8("gather_c", c_pad).reshape(8, 8, D)[:, :B].reshape(8 * B, D)
    c_all = jnp.concatenate([c_gath, c_ctx[None], jnp.zeros((7, D), F32)], axis=0)
    b_sh = lax.dynamic_slice_in_dim(b_ada, shard * Wa, Wa, axis=1)[:, None, :]
    mod_sh = _ada_fwd(c_all, w_ada, b_sh)
    mod_g = _allgather8("gather_mod", mod_sh.reshape(DEPTH * 24, Wa)).reshape(4, 2, DEPTH, 24, Wa)[:, 0]
    mods = mod_g.transpose(1, 2, 0, 3).reshape(DEPTH, 24, 4 * Wa)
    mine = lax.dynamic_slice_in_dim(mods, B * dev, B, axis=1)
    mod8 = jnp.stack([jnp.broadcast_to(mods[:, 16:17], (DEPTH, 8, 6 * D)), jnp.repeat(mine, 4, axis=1)], axis=1)
    SH1, SC1, G1, SH2, SC2, G2 = (k * D for k in range(6))

    small_parts = [ln_gain.reshape(-1, Ds), ln_bias.reshape(-1, Ds), cv_b_pw1.reshape(-1, Ds), cv_w_dw.reshape(-1, Ds),
                   cv_b_dw, cv_ln_g, cv_ln_b, cv_b_pw2]
    small_rows = [p.shape[0] for p in small_parts]
    sm = jnp.concatenate(small_parts, axis=0)
    pad_r = -sm.shape[0] % 8
    sm = jnp.pad(sm, ((0, pad_r), (0, 0)))
    sm_g = _allgather8("gather_small", sm).reshape(4, 2, sm.shape[0], Ds)[:, 0]
    pieces, o = [], 0
    for nr in small_rows:
        pieces.append(sm_g[:, o:o + nr])
        o += nr

    def unshard(p, lead):
        return p.reshape((4,) + lead + (Ds,)).transpose(tuple(range(1, len(lead) + 1)) + (0, len(lead) + 1)).reshape(lead + (4 * Ds,))

    ln_gain_f = unshard(pieces[0], (DEPTH, 2))
    ln_bias_f = unshard(pieces[1], (DEPTH, 2))
    nconv = cv_w_dw.shape[0]
    b_pw1_f = pieces[2].reshape(4, nconv, 2 * D // 4).transpose(1, 0, 2).reshape(nconv, 2 * D)
    w_dw_f = unshard(pieces[3], (nconv, cfg.KW))
    b_dw_f, cvg_f, cvb_f, b_pw2_f = (unshard(p, (nconv,)) for p in pieces[4:8])

    def halves(w):
        return w.astype(BF16).reshape(w.shape[0], 2, w.shape[1] // 2, w.shape[2])

    fam_a = halves(jnp.concatenate([mlp_w1, mlp_w2], axis=0))
    fam_b = halves(jnp.concatenate([s5_w_glu, cv_w_pw1], axis=0))
    fam_c = halves(cv_w_pw2)
    ga, gb, gc = _gather_weights([fam_a, fam_b, fam_c])
    ns5 = s5_w_glu.shape[0]
    wa_full = ga.reshape(4, 2 * DEPTH, D, cfg.F // 4)
    wb_full = gb.reshape(4, ns5 + nconv, D, D // 2)
    wc_full = gc.reshape(4, nconv, D // 4, D)

    pos = jnp.broadcast_to(_pos_embed(cfg.L // GRID_W, D)[None], (B, cfg.L, D))
    tok_in = jnp.concatenate([_to_perm(ctx), _to_perm(x)], axis=0)
    pos_in = jnp.concatenate([jnp.zeros((Tc, D), F32), _to_perm(pos)], axis=0)
    tok0 = _ew("add_pos", lambda a, b: a + b, [tok_in, pos_in], [_sds((T, D))])[0]
    tgt = _to_perm(loss_target)

    s5p = []
    for j in range(ns5):
        lay = _s5_layouts(cfg, s5_lam_re[j], s5_lam_im[j], s5_log_dt[j], s5_b_re[j], s5_b_im[j])
        abr, abi, bbr, bbi = _disc_fwd(*lay)
        s5p.append(dict(lay=lay, acoef=_coef_rows(cfg, abr, abi, False), acoef_adj=_coef_rows(cfg, abr, abi, True),
                        bf=_blockdiag_b(cfg, bbr, bbi), cf=_blockdiag_c(cfg, s5_c_re[j], s5_c_im[j])))

    kinds = ["s5" if i % 2 == 0 else "conv" for i in range(DEPTH)]
    tok = tok0
    saved = []
    s5_j = cv_j = 0
    for i in range(DEPTH):
        later_s5 = any(k == "s5" for k in kinds[i + 1:])
        rows = cfg.rows(later_s5)
        m8 = mod8[i]
        sv = dict(tok=tok, rows=rows, kind=kinds[i])
        g0, b0 = ln_gain_f[i, 0][None], ln_bias_f[i, 0][None]
        g1, b1 = ln_gain_f[i, 1][None], ln_bias_f[i, 1][None]
        if kinds[i] == "s5":
            j = s5_j
            s5_j += 1
            p = s5p[j]
            y, ck = _s5_forward(cfg, f"l{i}", tok, m8, SH1, SC1, p["bf"], p["cf"], p["acoef"])
            wg = wb_full[:, j]
            x1, r1, mix, zz, zb, yy = _glu_ln(cfg, f"l{i}_glu", rows, tok, y, m8, (SH1, SC1, G1), s5_d[j][None], wg,
                                              s5_b_glu[j][None], g0, b0)
            sv.update(j=j, ck=ck, zz=zz, zb=zb, yy=yy, wg=wg)
        else:
            j = cv_j
            cv_j += 1
            w1c, w2c = wb_full[:, ns5 + j], wc_full[:, j]
            aa, ag, hb = _pw1_glu(cfg, f"l{i}_pw1", rows, tok, m8, (SH1, SC1), w1c, b_pw1_f[j][None])
            cvv, sb = _dwconv_ln(cfg, f"l{i}_dw", rows, ag, w_dw_f[j], b_dw_f[j][None], cvg_f[j][None], cvb_f[j][None])
            x1, r1, mix = _pw2_ln(cfg, f"l{i}_pw2", rows, sb, tok, m8, G1, w2c, b_pw2_f[j][None], g0, b0)
            sv.update(j=j, aa=aa, ag=ag, hb=hb, cvv=cvv, sb=sb, w1c=w1c, w2c=w2c)
        w1m, w2m = wa_full[:, i], wa_full[:, DEPTH + i]
        x2, r2, mout, am, hm = _mlp_ln(cfg, f"l{i}_mlp", rows, x1, m8, (SH2, SC2, G2), w1m, w2m, g1, b1)
        sv.update(r1=r1, mix=mix, x1=x1, r2=r2, mout=mout, am=am, hm=hm, w1m=w1m, w2m=w2m, g0=g0, b0=b0, g1=g1, b1=b1)
        saved.append(sv)
        tok = x2

    loss8, dxf = _loss(cfg, tok, tgt)
    loss = lax.psum(loss8[0, 0], ("x", "y", "c"))

    gfam_a = jnp.zeros((4, 2 * DEPTH, D, cfg.F // 4), F32)
    gfam_b = jnp.zeros((4, ns5 + nconv, D, D // 2), F32)
    gfam_c = jnp.zeros((4, nconv, D // 4, D), F32)
    dmod8 = [None] * DEPTH
    g_ln_gain = [[None, None] for _ in range(DEPTH)]
    g_ln_bias = [[None, None] for _ in range(DEPTH)]
    g_s5 = [None] * ns5
    g_cv = [None] * nconv
    dres, dh = dxf, None
    for i in reversed(range(DEPTH)):
        sv = saved[i]
        rows = sv["rows"]
        m8 = mod8[i]
        nxt_m8 = mod8[i + 1] if i + 1 < DEPTH else None
        ctx_ok = True if i + 1 >= DEPTH else (saved[i + 1]["rows"][0] == 0)
        if rows[0] != 0:
            ctx_ok = True
        dprev, dbr, dgn, dbs, dg2, dsc_n, dsh_n = _lnb(
            cfg, f"l{i}_lnb2", rows, dres, ctx_ok, dh, sv["r2"], sv["mout"], jnp.concatenate([sv["g1"], sv["b1"]], 0),
            m8, G2, nxt_m8, SC1)
        if rows[0] != 0:
            dg2, dsc_n, dsh_n = (_x_only(t) for t in (dg2, dsc_n, dsh_n))
        g_ln_gain[i][1], g_ln_bias[i][1] = dgn[0], dbs[0]
        if i + 1 < DEPTH:
            dmod8[i + 1]["sc1"], dmod8[i + 1]["sh1"] = dsc_n, dsh_n
        dmod8[i] = dict(g2=dg2)
        dh2, dhid, dout = _mlp_bwd(cfg, f"l{i}_mlpb", rows, dbr, sv["am"], sv["w1m"], sv["w2m"])
        gfam_a = _wgrad(cfg, f"l{i}_gw1", rows, sv["hm"], dhid, "col", gfam_a, i)
        gfam_a = _wgrad(cfg, f"l{i}_gw2", rows, sv["am"], dout, "row", gfam_a, DEPTH + i)
        dprev1, dbr1, dgn, dbs, dg1, dsc2, dsh2 = _lnb(
            cfg, f"l{i}_lnb1", rows, dprev, True, dh2, sv["r1"], sv["mix"], jnp.concatenate([sv["g0"], sv["b0"]], 0),
            m8, G1, m8, SC2)
        if rows[0] != 0:
            dg1, dsc2, dsh2 = (_x_only(t) for t in (dg1, dsc2, dsh2))
        g_ln_gain[i][0], g_ln_bias[i][0] = dgn[0], dbs[0]
        dmod8[i].update(g1=dg1, sc2=dsc2, sh2=dsh2)
        j = sv["j"]
        if sv["kind"] == "s5":
            p = s5p[j]
            dyy, dzz, dbglu = _glu_bwd(cfg, f"l{i}_glub", rows, dbr1, sv["zz"], sv["wg"], sv["yy"])
            gfam_b = _wgrad(cfg, f"l{i}_gwg", rows, sv["zb"], dzz, "col", gfam_b, j)
            du, (da, dbf, dcf) = _s5_backward(cfg, f"l{i}", dyy, rows[0] == 0, sv["tok"], m8, SH1, SC1, p["bf"], p["cf"],
                                              p["acoef"], p["acoef_adj"], sv["ck"])
            dh, dds = _s5_du(cfg, f"l{i}_du", cfg.rows(True), du, dyy, rows[0] // cfg.TM, sv["tok"], m8, (SH1, SC1),
                             s5_d[j][None])
            g_s5[j] = dict(da=da, dbf=dbf, dcf=dcf, dd=dds[0], dbglu=dbglu[0])
            dres = dprev1
        else:
            dcv, dmb, sums = _pw2_bwd(cfg, f"l{i}_pw2b", rows, dbr1, sv["cvv"], sv["w2c"], cvg_f[j][None], cvb_f[j][None])
            gfam_c = _wgrad(cfg, f"l{i}_gwp2", rows, sv["sb"], dmb, "row", gfam_c, j)
            dag, dwdw = _dwconv_bwd(cfg, f"l{i}_dwb", rows, dcv, sv["ag"], w_dw_f[j])
            dh, daa, dbpw1 = _glu_bwd(cfg, f"l{i}_pw1b", rows, dag, sv["aa"], sv["w1c"])
            gfam_b = _wgrad(cfg, f"l{i}_gwp1", rows, sv["hb"], daa, "col", gfam_b, ns5 + j)
            g_cv[j] = dict(ln_g=sums[0], ln_b=sums[1], b_dw=sums[2], b_pw2=sums[3], w_dw=dwdw, b_pw1=dbpw1[0])
            dres = dprev1
    gx_perm, dsc0, dsh0 = _input_bwd(cfg, dres, dh, tok0, mod8[0], SC1)
    dmod8[0]["sc1"], dmod8[0]["sh1"] = dsc0, dsh0
    grad_x = _from_perm(gx_perm[Tc:], B, cfg.L)

    zero28 = jnp.zeros((2, 8, D), F32)
    dm8 = jnp.stack([jnp.concatenate([dmod8[i].get(k, zero28) for k in ("sh1", "sc1", "g1", "sh2", "sc2", "g2")], axis=-1)
                     for i in range(DEPTH)])
    dm_rows = _dmod_rows(dm8)
    dm_tab = jnp.zeros((DEPTH, 24, 6 * D), F32)
    dm_tab = lax.dynamic_update_slice_in_dim(dm_tab, dm_rows[:, 0:B], B * dev, axis=1)
    dm_tab = lax.dynamic_update_slice_in_dim(dm_tab, dm_rows[:, 2:3], 16, axis=1)

    eye_parts = []
    for j in range(ns5):
        g = g_s5[j]
        dbbr, dbbi = _diag_b(cfg, g["dbf"])
        dcr, dci = _diag_c(cfg, g["dcf"])
        eye_parts += [g["da"], dbbr, dbbi, dcr, dci, g["dd"], g["dbglu"]]
    for j in range(nconv):
        g = g_cv[j]
        eye_parts += [g["ln_g"], g["ln_b"], g["b_dw"], g["b_pw2"], g["w_dw"], g["b_pw1"]]
    eye_parts += [jnp.stack([jnp.stack(r) for r in g_ln_gain]), jnp.stack([jnp.stack(r) for r in g_ln_bias]), dm_tab]
    buf, meta = _pack(eye_parts)
    red = _unpack(_allreduce8("small", buf), meta)

    grads = {}
    k = 0
    lam_re_g, lam_im_g, ldt_g, bre_g, bim_g, cre_g, cim_g, dd_g, bglu_g = ([] for _ in range(9))
    for j in range(ns5):
        da, dbbr, dbbi, dcr, dci, dd, dbglu = red[k:k + 7]
        k += 7
        da_s = _sublane_sum(f"s5_dasum_{j}", da.reshape(4, 8, cfg.NS)).reshape(2, 2, cfg.NS)
        g_abr = da_s[:, 0].reshape(2, cfg.G, cfg.P).transpose(2, 0, 1).reshape(cfg.P, 2 * cfg.G)
        g_abi = da_s[:, 1].reshape(2, cfg.G, cfg.P).transpose(2, 0, 1).reshape(cfg.P, 2 * cfg.G)
        glr, gli, gldt, gbr, gbi = _disc_bwd(*s5p[j]["lay"], g_abr, g_abi, dbbr, dbbi)
        lam_re_g.append(glr.reshape(cfg.P, 2, cfg.G).transpose(1, 2, 0))
        lam_im_g.append(gli.reshape(cfg.P, 2, cfg.G).transpose(1, 2, 0))
        ldt_g.append(gldt.reshape(2, cfg.G))
        bre_g.append(gbr.reshape(S5_GROUP, cfg.P, 2, cfg.G).transpose(2, 3, 1, 0))
        bim_g.append(gbi.reshape(S5_GROUP, cfg.P, 2, cfg.G).transpose(2, 3, 1, 0))
        cre_g.append(dcr)
        cim_g.append(dci)
        dd_g.append(dd)
        bglu_g.append(dbglu)
    grads.update(s5_lam_re=jnp.stack(lam_re_g), s5_lam_im=jnp.stack(lam_im_g), s5_log_dt=jnp.stack(ldt_g),
                 s5_b_re=jnp.stack(bre_g), s5_b_im=jnp.stack(bim_g), s5_c_re=jnp.stack(cre_g), s5_c_im=jnp.stack(cim_g),
                 s5_d=jnp.stack(dd_g), s5_b_glu=jnp.stack(bglu_g))

    def my_cols(full, width):
        return lax.dynamic_slice_in_dim(full, shard * width, width, axis=full.ndim - 1)

    cvs = {n: [] for n in ("ln_g", "ln_b", "b_dw", "b_pw2", "w_dw", "b_pw1")}
    for j in range(nconv):
        for n, val in zip(("ln_g", "ln_b", "b_dw", "b_pw2", "w_dw", "b_pw1"), red[k:k + 6]):
            cvs[n].append(val)
        k += 6
    grads.update(cv_ln_g=my_cols(jnp.stack(cvs["ln_g"]), Ds), cv_ln_b=my_cols(jnp.stack(cvs["ln_b"]), Ds),
                 cv_b_dw=my_cols(jnp.stack(cvs["b_dw"]), Ds), cv_b_pw2=my_cols(jnp.stack(cvs["b_pw2"]), Ds),
                 cv_w_dw=my_cols(jnp.stack(cvs["w_dw"]), Ds), cv_b_pw1=my_cols(jnp.stack(cvs["b_pw1"]), 2 * D // 4))
    grads.update(ln_gain=my_cols(red[k], Ds), ln_bias=my_cols(red[k + 1], Ds))
    dm_all = red[k + 2]

    dm_sh = lax.dynamic_slice_in_dim(dm_all, shard * Wa, Wa, axis=2)
    gw_ada, dcond = _ada_bwd(c_all, dm_sh, w_ada)
    grads["w_ada"] = gw_ada
    grads["b_ada"] = _colsum_groups("ada_bsum", dm_all)
    dc_part = dcond[0:1]
    dc_buf = jnp.concatenate([jnp.where(ac == 0, dc_part, 0.0), jnp.zeros((7, D), F32)], axis=0)
    dc_tot = _allreduce8("cctx", dc_buf.reshape(8 * D // LANES, LANES)).reshape(8, D)[0:1]
    grads["c_ctx"] = _ew("cctx_grad", lambda g, cv: g * (_sigmoid(cv) * (1.0 + cv * (1.0 - _sigmoid(cv)))),
                         [jnp.broadcast_to(dc_tot, (8, D)), jnp.broadcast_to(c_ctx[None], (8, D))], [_sds((8, D))])[0][0]

    def fam_halves(g):
        return g.reshape(g.shape[0], g.shape[1], 2, g.shape[2] // 2, g.shape[3])

    ra, rb, rc = _reduce_scatter("gw", [fam_halves(gfam_a), fam_halves(gfam_b), fam_halves(gfam_c)])
    ra = ra.reshape(2 * DEPTH, D, cfg.F // 4)
    rb = rb.reshape(ns5 + nconv, D, D // 2)
    rc = rc.reshape(nconv, D // 4, D)
    grads.update(mlp_w1=ra[:DEPTH], mlp_w2=ra[DEPTH:], s5_w_glu=rb[:ns5], cv_w_pw1=rb[ns5:], cv_w_pw2=rc)

    weights = dict(c_ctx=c_ctx, w_ada=w_ada, b_ada=b_ada, ln_gain=ln_gain, ln_bias=ln_bias, s5_lam_re=s5_lam_re,
                   s5_lam_im=s5_lam_im, s5_log_dt=s5_log_dt, s5_b_re=s5_b_re, s5_b_im=s5_b_im, s5_c_re=s5_c_re,
                   s5_c_im=s5_c_im, s5_d=s5_d, s5_w_glu=s5_w_glu, s5_b_glu=s5_b_glu, cv_w_pw1=cv_w_pw1, cv_b_pw1=cv_b_pw1,
                   cv_w_dw=cv_w_dw, cv_b_dw=cv_b_dw, cv_ln_g=cv_ln_g, cv_ln_b=cv_ln_b, cv_w_pw2=cv_w_pw2, cv_b_pw2=cv_b_pw2,
                   mlp_w1=mlp_w1, mlp_w2=mlp_w2)
    ms = dict(c_ctx=m_c_ctx, w_ada=m_w_ada, b_ada=m_b_ada, ln_gain=m_ln_gain, ln_bias=m_ln_bias, s5_lam_re=m_s5_lam_re,
              s5_lam_im=m_s5_lam_im, s5_log_dt=m_s5_log_dt, s5_b_re=m_s5_b_re, s5_b_im=m_s5_b_im, s5_c_re=m_s5_c_re,
              s5_c_im=m_s5_c_im, s5_d=m_s5_d, s5_w_glu=m_s5_w_glu, s5_b_glu=m_s5_b_glu, cv_w_pw1=m_cv_w_pw1,
              cv_b_pw1=m_cv_b_pw1, cv_w_dw=m_cv_w_dw, cv_b_dw=m_cv_b_dw, cv_ln_g=m_cv_ln_g, cv_ln_b=m_cv_ln_b,
              cv_w_pw2=m_cv_w_pw2, cv_b_pw2=m_cv_b_pw2, mlp_w1=m_mlp_w1, mlp_w2=m_mlp_w2)
    vs = dict(c_ctx=v_c_ctx, w_ada=v_w_ada, b_ada=v_b_ada, ln_gain=v_ln_gain, ln_bias=v_ln_bias, s5_lam_re=v_s5_lam_re,
              s5_lam_im=v_s5_lam_im, s5_log_dt=v_s5_log_dt, s5_b_re=v_s5_b_re, s5_b_im=v_s5_b_im, s5_c_re=v_s5_c_re,
              s5_c_im=v_s5_c_im, s5_d=v_s5_d, s5_w_glu=v_s5_w_glu, s5_b_glu=v_s5_b_glu, cv_w_pw1=v_cv_w_pw1,
              cv_b_pw1=v_cv_b_pw1, cv_w_dw=v_cv_w_dw, cv_b_dw=v_cv_b_dw, cv_ln_g=v_cv_ln_g, cv_ln_b=v_cv_ln_b,
              cv_w_pw2=v_cv_w_pw2, cv_b_pw2=v_cv_b_pw2, mlp_w1=v_mlp_w1, mlp_w2=v_mlp_w2)
    names = list(weights)
    deltas, new_m, new_v = {}, {}, {}
    for n in names:
        g = grads[n].reshape(weights[n].shape)
        grads[n] = g
        deltas[n], new_m[n], new_v[n] = _adamw("adamw_" + n, weights[n], g, ms[n], vs[n])
    return (loss, grad_x, *[grads[n] for n in names], *[deltas[n] for n in names], *[new_m[n] for n in names],
            *[new_v[n] for n in names])


def _sublane_sum(name, a):
    n, _, w = a.shape

    def body(a_ref, o_ref):
        for q in range(n):
            o_ref[q:q + 1, :] = jnp.sum(a_ref[q], axis=0, keepdims=True)

    return pl.pallas_call(body, name=name, out_shape=_sds((n, w)))(a)


def _colsum_groups(name, dm_all):
    nl, nr, w = dm_all.shape

    def body(d_ref, o_ref):
        o_ref[...] = jnp.zeros((8, w), F32) + jnp.sum(d_ref[...], axis=0, keepdims=True)

    out = pl.pallas_call(body, name=name, grid=(nl,), in_specs=[pl.BlockSpec((None, nr, w), lambda l: (l, 0, 0))],
                         out_specs=pl.BlockSpec((None, 8, w), lambda l: (l, 0, 0)), out_shape=_sds((nl, 8, w)),
                         compiler_params=_cp(1))(dm_all)
    return out[:, 0]
```

```python
import functools
import math

import jax
import jax.numpy as jnp
from jax import lax
from jax.experimental import pallas as pl
from jax.experimental.pallas import tpu as pltpu

F32 = jnp.float32
BF16 = jnp.bfloat16
MESH = pl.DeviceIdType.MESH
ANY = pl.BlockSpec(memory_space=pl.ANY)

DEPTH = 4
S5_GROUP = 16
S5_STATE = 64
GRID_W = 64
POS_TEMP = 10000.0
LAMBDA_RE_MAX = -1e-4
LN_EPS = 1e-5
DN_ALPHA = (2.0 * DEPTH) ** 0.25
ADAM_LR, ADAM_B1, ADAM_B2, ADAM_EPS, ADAM_WD, ADAM_STEP = 0.001, 0.9, 0.999, 1e-08, 0.01, 10

SUBLANES = 8
LANES = 128
OCT_CH = 128
OCT_ST = 512
VMEM_LIMIT = 56 * 1024 * 1024


def _cp(n_axes):
    return pltpu.CompilerParams(dimension_semantics=("arbitrary",) * n_axes, vmem_limit_bytes=VMEM_LIMIT)


def _full(shape, single=False):
    nd = len(shape)
    if single:
        return pl.BlockSpec(shape, lambda *i: (0,) * nd, pipeline_mode=pl.Buffered(1))
    return pl.BlockSpec(shape, lambda *i: (0,) * nd)


def _sds(shape, dtype=F32):
    return jax.ShapeDtypeStruct(tuple(shape), dtype)


def _mod(x, sh8, sc8):
    r, d = x.shape
    return (x.reshape(r // 8, 8, d) * (1.0 + sc8[None]) + sh8[None]).reshape(r, d)


def _rowscale(x, g8):
    r, d = x.shape
    return (x.reshape(r // 8, 8, d) * g8[None]).reshape(r, d)


def _sum8(x):
    r, w = x.shape
    return jnp.sum(x.reshape(r // 8, 8, w), axis=0)


def _ln_stats(r):
    mu = jnp.mean(r, axis=-1, keepdims=True)
    xc = r - mu
    var = jnp.mean(xc * xc, axis=-1, keepdims=True)
    rstd = lax.rsqrt(var + LN_EPS)
    return xc * rstd, rstd


def _ln_bwd(dxh, xhat, rstd):
    m1 = jnp.mean(dxh, axis=-1, keepdims=True)
    m2 = jnp.mean(dxh * xhat, axis=-1, keepdims=True)
    return rstd * (dxh - m1 - xhat * m2)


def _sigmoid(x):
    return 1.0 / (1.0 + jnp.exp(-x))


def _gelu(y):
    return 0.5 * y * (1.0 + lax.erf(y * (1.0 / math.sqrt(2.0))))


def _gelu_grad(y):
    return 0.5 * (1.0 + lax.erf(y * (1.0 / math.sqrt(2.0)))) + y * jnp.exp(-0.5 * y * y) * (1.0 / math.sqrt(2.0 * math.pi))


def _dot(a, b):
    return jnp.dot(a, b, preferred_element_type=F32)


def _dot_nt(a, b):
    return lax.dot_general(a, b, (((1,), (1,)), ((), ())), preferred_element_type=F32)


def _dot_tn(a, b):
    return lax.dot_general(a, b, (((0,), (0,)), ((), ())), preferred_element_type=F32)


class _Cfg:
    def __init__(self, x, ctx, mlp_w1, cv_w_dw):
        self.B, self.L, self.D = x.shape
        self.Lc = ctx.shape[1]
        assert self.B * 4 == SUBLANES, "two examples per device, four chunks each"
        self.F = mlp_w1.shape[2] * 4
        self.KW = cv_w_dw.shape[1]
        self.half = self.KW // 2
        self.G = self.D // S5_GROUP
        self.P = S5_STATE
        self.NS = self.G * self.P
        self.NO = self.D // OCT_CH
        assert self.NO % 2 == 0
        self.nx = self.L // 4
        self.nc = self.Lc // 4
        self.Tc = self.B * self.Lc
        self.Tx = self.B * self.L
        self.T = self.Tc + self.Tx
        self.TM = 256 if self.Tc % 256 == 0 else self.Tc
        assert self.Tx % self.TM == 0 and self.TM % 16 == 0
        self.HB = self.TM // 2
        assert SUBLANES * self.half <= self.HB
        self.TW = 512 if (self.Tc % 512 == 0 and self.Tx % 512 == 0) else self.TM

    def ti(self, n):
        t = 16 if self.nc % 16 == 0 else self.nc
        assert n % t == 0 and self.Tc % (8 * t) == 0
        return t

    def rows(self, ctx_too):
        return (0, self.T) if ctx_too else (self.Tc, self.Tx)


def _allgather8(name, x_shard):
    m_per, n = x_shard.shape
    assert m_per % 8 == 0

    def body(x_ref, out_ref, send_sems, recv_sems, local_sem):
        x, y, c = lax.axis_index("x"), lax.axis_index("y"), lax.axis_index("c")
        me, sibling = (x, y, c), (x, y, 1 - c)
        chips = [(1 - x, y), (x, 1 - y), (1 - x, 1 - y)]

        def rows(px, py, pc):
            return out_ref.at[pl.ds((4 * px + 2 * py + pc) * m_per, m_per), :]

        def copy(k, block, to, src=None):
            return pltpu.make_async_remote_copy(
                src_ref=rows(*block) if src is None else src, dst_ref=rows(*block),
                send_sem=send_sems.at[k], recv_sem=recv_sems.at[k], device_id=to, device_id_type=MESH)

        mine = pltpu.make_async_copy(x_ref, rows(*me), local_sem)
        mine.start()
        first = [copy(0, me, sibling, src=x_ref)]
        first += [copy(1 + j, me, (*chip, c), src=x_ref) for j, chip in enumerate(chips)]
        for cp in first:
            cp.start()
        passed = [copy(4 + j, (*chip, c), sibling) for j, chip in enumerate(chips)]
        for j, chip in enumerate(chips):
            copy(1 + j, (*chip, c), me).wait_recv()
            passed[j].start()
        copy(0, sibling, me).wait_recv()
        for j, chip in enumerate(chips):
            copy(4 + j, (*chip, 1 - c), me).wait_recv()
        for cp in first + passed:
            cp.wait_send()
        mine.wait()

    return pl.pallas_call(
        body, name=name, out_shape=_sds((8 * m_per, n), x_shard.dtype),
        in_specs=[pl.BlockSpec(memory_space=pltpu.VMEM)], out_specs=pl.BlockSpec(memory_space=pltpu.VMEM),
        scratch_shapes=[pltpu.SemaphoreType.DMA((7,)), pltpu.SemaphoreType.DMA((7,)), pltpu.SemaphoreType.DMA],
        compiler_params=pltpu.CompilerParams(vmem_limit_bytes=VMEM_LIMIT),
    )(x_shard)


def _flip(v, m):
    return v + m - 2 * v * m


def _peer(axis):
    x, y, c = lax.axis_index("x"), lax.axis_index("y"), lax.axis_index("c")
    if axis == "c":
        return (x, y, 1 - c)
    if axis == "xy":
        return (_flip(x, 1 - c), _flip(y, c), c)
    if axis == "yx":
        return (_flip(x, c), _flip(y, 1 - c), c)
    raise ValueError(axis)


def _pair_exchange(name, axis, inputs, out_shapes, aliases, plan):
    n_in = len(inputs)
    n_out = len(out_shapes)

    def body(*refs):
        ins, outs = refs[:n_in], refs[n_in:n_in + n_out]
        send_sems, recv_sems, local_sems = refs[n_in + n_out:]
        x, y, c = lax.axis_index("x"), lax.axis_index("y"), lax.axis_index("c")
        remote, local = plan(x, y, c, ins, outs)
        lcs = [pltpu.make_async_copy(s, d, local_sems.at[k]) for k, (s, d) in enumerate(local)]
        for cp in lcs:
            cp.start()
        rcs = [pltpu.make_async_remote_copy(src_ref=s, dst_ref=d, send_sem=send_sems.at[k], recv_sem=recv_sems.at[k],
                                            device_id=_peer(axis), device_id_type=MESH) for k, (s, d) in enumerate(remote)]
        for cp in rcs:
            cp.start()
        for cp in rcs:
            cp.wait()
        for cp in lcs:
            cp.wait()

    n_remote, n_local = plan.counts
    return pl.pallas_call(
        body, name=name, out_shape=tuple(out_shapes),
        in_specs=[ANY] * n_in, out_specs=tuple([ANY] * n_out),
        input_output_aliases=dict(aliases),
        scratch_shapes=[pltpu.SemaphoreType.DMA((n_remote,)), pltpu.SemaphoreType.DMA((n_remote,)),
                        pltpu.SemaphoreType.DMA((max(n_local, 1),))],
    )(*inputs)


def _plan(n_remote, n_local=0):
    def deco(fn):
        fn.counts = (n_remote, n_local)
        return fn
    return deco


def _xyc():
    return jnp.stack([lax.axis_index("x"), lax.axis_index("y"), lax.axis_index("c")]).astype(jnp.int32)


def _place_shard(name, w, fam, slot0, n_slots):
    n, kk, nn = w.shape
    kt = 256 if kk % 256 == 0 else kk

    def body(scal, w_ref, *rest):
        rest[-1][...] = w_ref[...].astype(BF16)

    in_specs = [pl.BlockSpec((None, kt, nn), lambda t, i, sc: (t, i, 0))]
    args = [_xyc(), w]
    aliases = {}
    if fam is not None:
        in_specs.append(ANY)
        args.append(fam)
        aliases = {2: 0}
    gs = pltpu.PrefetchScalarGridSpec(
        num_scalar_prefetch=1, grid=(n, kk // kt), in_specs=in_specs,
        out_specs=pl.BlockSpec((None, None, kt, nn), lambda t, i, sc: (2 * sc[0] + sc[1], slot0 + t, i, 0)))
    return pl.pallas_call(body, name=name, grid_spec=gs, out_shape=_sds((4, n_slots, kk, nn), BF16),
                          input_output_aliases=aliases, compiler_params=_cp(2))(*args)


def _gather_weights(fams):
    nf = len(fams)
    shapes = [f.shape for f in fams]
    views = [f.reshape(4, 2, -1, f.shape[-1]) for f in fams]
    outs = [_sds(v.shape, v.dtype) for v in views]
    alias = {k: k for k in range(nf)}

    @_plan(nf)
    def plan1(x, y, c, ins, outs_):
        s = 2 * x + y
        return ([(ins[k].at[s, c], outs_[k].at[s, c]) for k in range(nf)], [])

    views = _pair_exchange("gatherw_1", "xy", list(views), outs, alias, plan1)

    @_plan(2 * nf)
    def plan2(x, y, c, ins, outs_):
        shards = [2 * x + y, 2 * _flip(x, 1 - c) + _flip(y, c)]
        return ([(ins[k].at[s, c], outs_[k].at[s, c]) for k in range(nf) for s in shards], [])

    views = _pair_exchange("gatherw_2", "yx", list(views), outs, alias, plan2)

    @_plan(3 * nf)
    def plan3(x, y, c, ins, outs_):
        shards = [2 * (1 - x) + y, 2 * x + (1 - y), 2 * (1 - x) + (1 - y)]
        return ([(ins[k].at[s, c], outs_[k].at[s, c]) for k in range(nf) for s in shards], [])

    views = _pair_exchange("gatherw_c", "c", list(views), outs, alias, plan3)
    return [v.reshape(sh) for v, sh in zip(views, shapes)]


def _sel_add(name, a, a_sel, r, emit_bf16, out_slots=None):
    nr, rows, w = r.shape
    tr = 256 if rows % 256 == 0 else rows

    def body(scal, a_ref, r_ref, *outs):
        s = a_ref[...] + r_ref[...].astype(F32)
        outs[0][...] = s
        if emit_bf16:
            outs[1][...] = s.astype(BF16)

    lead = a.ndim - 2
    a_block = (None,) * lead + (tr, w)
    n_out, o_fn = out_slots if out_slots is not None else (nr, lambda j, sc: j)
    out_shape = [_sds((n_out, rows, w), F32)] + ([_sds((nr, rows, w), BF16)] if emit_bf16 else [])
    out_specs = [pl.BlockSpec((None, tr, w), lambda j, t, sc: (o_fn(j, sc), t, 0))]
    if emit_bf16:
        out_specs.append(pl.BlockSpec((None, tr, w), lambda j, t, sc: (j, t, 0)))
    gs = pltpu.PrefetchScalarGridSpec(
        num_scalar_prefetch=1, grid=(nr, rows // tr),
        in_specs=[pl.BlockSpec(a_block, lambda j, t, sc: tuple(a_sel(j, sc)) + (t, 0)),
                  pl.BlockSpec((None, tr, w), lambda j, t, sc: (j, t, 0))],
        out_specs=out_specs)
    return pl.pallas_call(body, name=name, grid_spec=gs, out_shape=out_shape, compiler_params=_cp(2))(_xyc(), a, r)


def _reduce_scatter(tag, grads):
    ng = len(grads)
    flat = [g.reshape(4, 2, -1, g.shape[-1]) for g in grads]

    @_plan(ng)
    def plan1(x, y, c, ins, outs_):
        return ([(ins[k].at[:, 1 - c], outs_[k]) for k in range(ng)], [])

    r1 = _pair_exchange(tag + "_rs_c", "c", flat, [_sds((4,) + f.shape[2:], F32) for f in flat], {}, plan1)
    p1 = [_sel_add(f"{tag}_add1_{k}", flat[k], lambda j, sc: (j, sc[2]), r1[k], True) for k in range(ng)]

    def sent1(kk, x, y, c):
        return ((1 - c) * kk + c * (1 - x), (1 - c) * (1 - y) + c * kk)

    def kept1(j, sc):
        x, y, c = sc[0], sc[1], sc[2]
        return ((1 - c) * j + c * x, (1 - c) * y + c * j)

    @_plan(2 * ng)
    def plan2(x, y, c, ins, outs_):
        return ([(ins[k].at[sent1(kk, x, y, c)], outs_[k].at[kk]) for k in range(ng) for kk in range(2)], [])

    v1 = [pb.reshape(2, 2, pb.shape[1], pb.shape[2]) for p, pb in p1]
    r2 = _pair_exchange(tag + "_rs_1", "yx", v1, [_sds((2,) + v.shape[2:], BF16) for v in v1], {}, plan2)
    p2 = [_sel_add(f"{tag}_add2_{k}", p1[k][0].reshape(2, 2, p1[k][0].shape[1], p1[k][0].shape[2]), kept1, r2[k], True)
          for k in range(ng)]

    @_plan(ng)
    def plan3(x, y, c, ins, outs_):
        return ([(ins[k].at[(1 - c) * (1 - x) + c * (1 - y)], outs_[k]) for k in range(ng)], [])

    r3 = _pair_exchange(tag + "_rs_2", "xy", [qb for q, qb in p2], [_sds(qb.shape[1:], BF16) for q, qb in p2], {}, plan3)
    fin = [_sel_add(f"{tag}_add3_{k}", p2[k][0], lambda j, sc: ((1 - sc[2]) * sc[0] + sc[2] * sc[1],), r3[k][None], False,
                    out_slots=(2, lambda j, sc: sc[2]))[0] for k in range(ng)]

    @_plan(ng)
    def plan4(x, y, c, ins, outs_):
        return ([(ins[k].at[c], outs_[k].at[c]) for k in range(ng)], [])

    full = _pair_exchange(tag + "_rs_c2", "c", fin, [_sds(f.shape, F32) for f in fin], {k: k for k in range(ng)}, plan4)
    return [full[k].reshape(grads[k].shape[1:]) for k in range(ng)]


def _allreduce8(tag, buf):
    rows, w = buf.shape
    assert rows % 16 == 0
    one = lambda: _plan(1)(lambda x, y, c, ins, outs_: ([(ins[0], outs_[0])], []))
    (got,) = _pair_exchange(f"{tag}_ar_c", "c", [buf], [_sds(buf.shape, F32)], {}, one())
    cur = _ew(f"{tag}_aradd_c", lambda a, b: a + b, [buf, got], [_sds(buf.shape, F32)])[0].reshape(2, rows // 2, w)
    mine = _plan(1)(lambda x, y, c, ins, outs_: ([(ins[0].at[c], outs_[0])], []))
    (got,) = _pair_exchange(f"{tag}_ar_1", "xy", [cur], [_sds(cur.shape[1:], F32)], {}, mine)
    (h1,) = _sel_add(f"{tag}_aradd_1", cur, lambda j, sc: (sc[2],), got[None], False)
    (got,) = _pair_exchange(f"{tag}_ar_2", "yx", [h1[0]], [_sds(h1.shape[1:], F32)], {}, one())
    (h2,) = _sel_add(f"{tag}_aradd_2", h1, lambda j, sc: (0,), got[None], False, out_slots=(2, lambda j, sc: sc[2]))
    swap = _plan(1)(lambda x, y, c, ins, outs_: ([(ins[0].at[c], outs_[0].at[c])], []))
    (full,) = _pair_exchange(f"{tag}_ar_c2", "c", [h2], [_sds(h2.shape, F32)], {0: 0}, swap)
    return full.reshape(rows, w)


def _ew(name, fn, ins, outs):
    rows, w = ins[0].shape
    tr = rows
    for cand in (512, 256, 128, 64, 32, 16, 8):
        if rows % cand == 0 and rows > cand and cand * w * 4 <= (1 << 20):
            tr = cand
            break
    n_in = len(ins)

    def body(*refs):
        vals = fn(*[r[...] for r in refs[:n_in]])
        if not isinstance(vals, (tuple, list)):
            vals = (vals,)
        for o, v in zip(refs[n_in:], vals):
            o[...] = v.astype(o.dtype)

    spec = pl.BlockSpec((tr, w), lambda i: (i, 0))
    return pl.pallas_call(body, name=name, grid=(rows // tr,), in_specs=[spec] * n_in,
                          out_specs=[spec] * len(outs), out_shape=list(outs), compiler_params=_cp(1))(*ins)


def _ew3(name, fn, ins, n_out):
    aa, bb, cc = ins[0].shape
    pad_bytes = (-(-bb // SUBLANES) * SUBLANES) * (-(-cc // LANES) * LANES) * 4
    ta = 1
    for cand in range(aa, 0, -1):
        if aa % cand == 0 and cand * pad_bytes <= (1 << 20):
            ta = cand
            break
    n_in = len(ins)

    def body(*refs):
        vals = fn(*[r[...] for r in refs[:n_in]])
        for o, v in zip(refs[n_in:], vals):
            o[...] = v

    spec = pl.BlockSpec((ta, bb, cc), lambda i: (i, 0, 0))
    return pl.pallas_call(body, name=name, grid=(aa // ta,), in_specs=[spec] * n_in, out_specs=[spec] * n_out,
                          out_shape=[_sds((aa, bb, cc))] * n_out, compiler_params=_cp(1))(*ins)


def _view_for_ew(a):
    if a.ndim == 1:
        return a.reshape(1, -1)
    if a.ndim == 2:
        return a
    if a.shape[-1] % LANES == 0 and a.shape[-2] % SUBLANES == 0:
        return a.reshape(-1, a.shape[-1])
    return a.reshape(-1, a.shape[-2], a.shape[-1])


def _adamw(name, w, g, m, v):
    def fn(w, g, m, v):
        m = ADAM_B1 * m + (1.0 - ADAM_B1) * g
        v = ADAM_B2 * v + (1.0 - ADAM_B2) * (g * g)
        m_hat = m / (1.0 - ADAM_B1 ** ADAM_STEP)
        v_hat = v / (1.0 - ADAM_B2 ** ADAM_STEP)
        delta = -ADAM_LR * (m_hat / (jnp.sqrt(v_hat) + ADAM_EPS) + ADAM_WD * w)
        return delta, m, v

    shp = w.shape
    a = [_view_for_ew(t) for t in (w, g, m, v)]
    if a[0].ndim == 3:
        o = _ew3(name, fn, a, 3)
    else:
        o = _ew(name, fn, a, [_sds(a[0].shape)] * 3)
    return tuple(t.reshape(shp) for t in o)


def _to_perm(a):
    b, ls, d = a.shape
    n = ls // 4
    return a.reshape(b * 4, n, d).swapaxes(0, 1).reshape(n * 8, d)


def _from_perm(p, b, ls):
    n = ls // 4
    return p.reshape(n, b * 4, p.shape[-1]).swapaxes(0, 1).reshape(b, ls, p.shape[-1])


def _pos_embed(rows, dim):
    def sincos(pos, d):
        quarter = d // 2
        omega = POS_TEMP ** (-jnp.arange(quarter, dtype=F32) / quarter)
        ang = pos[:, None] * omega[None, :]
        return jnp.concatenate([jnp.sin(ang), jnp.cos(ang)], axis=-1)

    row_idx = jnp.repeat(jnp.arange(rows), GRID_W).astype(F32)
    col_idx = jnp.tile(jnp.arange(GRID_W), rows).astype(F32)
    return jnp.concatenate([sincos(row_idx, dim // 2), sincos(col_idx, dim // 2)], axis=-1)


def _stream_of(cfg, off_tiles, tile_rows):
    nct = cfg.Tc // tile_rows
    return lambda i: jnp.where(i + off_tiles >= nct, 1, 0)


def _ada_fwd(c_all, w_ada, b_shard):
    nl, d, w = w_ada.shape
    tn = 512 if w % 512 == 0 else w

    def body(c_ref, w_ref, b_ref, o_ref):
        cv = c_ref[...]
        cond = (cv * _sigmoid(cv)).astype(BF16)
        o_ref[...] = _dot(cond, w_ref[...].astype(BF16)) + b_ref[...]

    return pl.pallas_call(
        body, name="ada_fwd", grid=(nl, w // tn),
        in_specs=[_full(c_all.shape), pl.BlockSpec((None, d, tn), lambda l, j: (l, 0, j)),
                  pl.BlockSpec((None, 1, tn), lambda l, j: (l, 0, j))],
        out_specs=pl.BlockSpec((None, c_all.shape[0], tn), lambda l, j: (l, 0, j)),
        out_shape=_sds((nl, c_all.shape[0], w)), compiler_params=_cp(2))(c_all, w_ada, b_shard)


def _ada_bwd(c_all, dmod_shard, w_ada):
    nl, d, w = w_ada.shape
    tn = 512 if w % 512 == 0 else w
    nr = c_all.shape[0]

    def body(c_ref, dm_ref, w_ref, gw_ref, dc_ref):
        j = pl.program_id(0) * (w // tn) + pl.program_id(1)
        cv = c_ref[...]
        cond = (cv * _sigmoid(cv)).astype(BF16)
        dm = dm_ref[...].astype(BF16)
        gw_ref[...] = _dot_tn(cond, dm)
        part = _dot_nt(dm[16:24], w_ref[...].astype(BF16))

        @pl.when(j == 0)
        def _():
            dc_ref[...] = part

        @pl.when(j > 0)
        def _():
            dc_ref[...] += part

    return pl.pallas_call(
        body, name="ada_bwd", grid=(nl, w // tn),
        in_specs=[_full(c_all.shape), pl.BlockSpec((None, nr, tn), lambda l, j: (l, 0, j)),
                  pl.BlockSpec((None, d, tn), lambda l, j: (l, 0, j))],
        out_specs=[pl.BlockSpec((None, d, tn), lambda l, j: (l, 0, j)), _full((8, d))],
        out_shape=[_sds((nl, d, w)), _sds((8, d))], compiler_params=_cp(2))(c_all, dmod_shard, w_ada)


def _disc(lr, li, ldt, br, bi):
    lr = jnp.minimum(lr, LAMBDA_RE_MAX)
    dt = jnp.exp(ldt)
    mag = jnp.exp(lr * dt)
    abr = mag * jnp.cos(li * dt)
    abi = mag * jnp.sin(li * dt)
    den = lr * lr + li * li
    nr = abr - 1.0
    ni = abi
    cr = (nr * lr + ni * li) / den
    ci = (ni * lr - nr * li) / den
    return abr, abi, cr[None] * br - ci[None] * bi, cr[None] * bi + ci[None] * br


def _disc_fwd(lr, li, ldt, br, bi):
    def body(a, b, c, d, e, o1, o2, o3, o4):
        r = _disc(a[...], b[...], c[...], d[...], e[...])
        o1[...], o2[...], o3[...], o4[...] = r

    return pl.pallas_call(body, name="s5_disc_fwd", out_shape=[_sds(lr.shape), _sds(lr.shape), _sds(br.shape), _sds(br.shape)])(
        lr, li, ldt, br, bi)


def _disc_bwd(lr, li, ldt, br, bi, g_abr, g_abi, g_bbr, g_bbi):
    def body(a, b, c, d, e, g1, g2, g3, g4, o1, o2, o3, o4, o5):
        _, vjp = jax.vjp(_disc, a[...], b[...], c[...], d[...], e[...])
        r = vjp((g1[...], g2[...], g3[...], g4[...]))
        o1[...], o2[...], o3[...], o4[...], o5[...] = r

    return pl.pallas_call(
        body, name="s5_disc_bwd",
        out_shape=[_sds(lr.shape), _sds(li.shape), _sds(ldt.shape), _sds(br.shape), _sds(bi.shape)])(
        lr, li, ldt, br, bi, g_abr, g_abi, g_bbr, g_bbi)


def _s5_layouts(cfg, lam_re, lam_im, log_dt, b_re, b_im):
    P, G = cfg.P, cfg.G
    lr = lam_re.transpose(2, 0, 1).reshape(P, 2 * G)
    li = lam_im.transpose(2, 0, 1).reshape(P, 2 * G)
    ldt = log_dt.reshape(1, 2 * G)
    br = b_re.transpose(3, 2, 0, 1).reshape(S5_GROUP, P, 2 * G)
    bi = b_im.transpose(3, 2, 0, 1).reshape(S5_GROUP, P, 2 * G)
    return lr, li, ldt, br, bi


def _coef_rows(cfg, abr, abi, conj):
    def one(t):
        return t.reshape(cfg.P, 2, cfg.G).transpose(1, 2, 0).reshape(2, cfg.NS)
    a = jnp.stack([one(abr), -one(abi) if conj else one(abi)], axis=1)
    return jnp.broadcast_to(a[:, :, None, :], (2, 2, SUBLANES, cfg.NS))


def _blockdiag_b(cfg, bbr, bbi):
    eye = jnp.eye(8, dtype=F32)

    def one(t):
        t = t.reshape(S5_GROUP, cfg.P, 2, cfg.G).transpose(2, 3, 0, 1)
        t = t.reshape(2, cfg.NO, 8, S5_GROUP, cfg.P)
        return jnp.einsum("dogcp,gh->dogchp", t, eye).reshape(2, cfg.NO, OCT_CH, OCT_ST)

    return jnp.concatenate([one(bbr), one(bbi)], axis=-1).astype(BF16)


def _blockdiag_c(cfg, c_re, c_im):
    eye = jnp.eye(8, dtype=F32)

    def one(t):
        t = t.transpose(0, 1, 3, 2).reshape(2, cfg.NO, 8, cfg.P, S5_GROUP)
        return jnp.einsum("dogpc,gh->dogphc", t, eye).reshape(2, cfg.NO, OCT_ST, OCT_CH)

    return jnp.concatenate([one(c_re), -one(c_im)], axis=2).astype(BF16)


def _diag_b(cfg, dbf):
    eye = jnp.eye(8, dtype=F32)

    def one(t):
        t = t.reshape(2, cfg.NO, 8, S5_GROUP, 8, cfg.P)
        t = jnp.einsum("dogchp,gh->dogcp", t, eye).reshape(2, cfg.G, S5_GROUP, cfg.P)
        return t.transpose(2, 3, 0, 1).reshape(S5_GROUP, cfg.P, 2 * cfg.G)

    return one(dbf[..., :OCT_ST]), one(dbf[..., OCT_ST:])


def _diag_c(cfg, dcf):
    eye = jnp.eye(8, dtype=F32)

    def one(t):
        t = t.reshape(2, cfg.NO, 8, cfg.P, 8, S5_GROUP)
        t = jnp.einsum("dogphc,gh->dogpc", t, eye).reshape(2, cfg.G, cfg.P, S5_GROUP)
        return t.transpose(0, 1, 3, 2)

    return one(dcf[:, :, :OCT_ST]), -one(dcf[:, :, OCT_ST:])


def _recur(buf, st, a_ref, n_oct, ti, rev, store):
    for o in range(0, n_oct, 2):
        cols = [(pl.ds(oo * 2 * OCT_ST, OCT_ST), pl.ds(oo * 2 * OCT_ST + OCT_ST, OCT_ST)) for oo in (o, o + 1)]
        scol = [pl.ds(oo * OCT_ST, OCT_ST) for oo in (o, o + 1)]
        coef = [(a_ref[0, :, sc], a_ref[1, :, sc]) for sc in scol]
        init = (st[0, :, scol[0]], st[1, :, scol[0]], st[0, :, scol[1]], st[1, :, scol[1]])

        def step(i, carry, cols=cols, coef=coef):
            r = pl.multiple_of((i + rev * (ti - 1 - 2 * i)) * 8, 8)
            out = []
            for s in range(2):
                sr, si = carry[2 * s], carry[2 * s + 1]
                ar, ai = coef[s]
                zr = buf[pl.ds(r, 8), cols[s][0]]
                zi = buf[pl.ds(r, 8), cols[s][1]]
                nr = ar * sr - ai * si + zr
                ni = ar * si + ai * sr + zi
                if store:
                    buf[pl.ds(r, 8), cols[s][0]] = nr
                    buf[pl.ds(r, 8), cols[s][1]] = ni
                out += [nr, ni]
            return tuple(out)

        fin = lax.fori_loop(0, ti, step, init, unroll=2)
        st[0, :, scol[0]] = fin[0]
        st[1, :, scol[0]] = fin[1]
        st[0, :, scol[1]] = fin[2]
        st[1, :, scol[1]] = fin[3]


def _s5_fwd_pass(cfg, name, tok, mod8, col_sh, col_sc, bf, acoef, r0, n, s_init=None, cf=None, y_prev=None):
    D, NO, NS = cfg.D, cfg.NO, cfg.NS
    ti = cfg.ti(n)
    nb = n // ti
    R = 8 * ti
    ob = r0 // R
    second = s_init is not None
    blk = lambda d, j: ob + j + d * (nb - 1 - 2 * j)

    def body(*refs):
        if second:
            tok_ref, mod_ref, bf_ref, a_ref, si_ref, cf_ref, yp_ref, y_ref, ck_ref, fin_ref, zbuf, st = refs
        else:
            tok_ref, mod_ref, bf_ref, a_ref, fin_ref, zbuf, st = refs
        d = pl.program_id(0)
        j = pl.program_id(1)

        @pl.when(j == 0)
        def _():
            if second:
                st[...] = si_ref[...]
            else:
                st[...] = jnp.zeros_like(st)

        if second:
            ck_ref[...] = st[...]
        u = _mod(tok_ref[...], mod_ref[:, col_sh:col_sh + D], mod_ref[:, col_sc:col_sc + D]).astype(BF16)
        for o in range(NO):
            zbuf[:, o * 1024:(o + 1) * 1024] = _dot(u[:, o * OCT_CH:(o + 1) * OCT_CH], bf_ref[o])
        _recur(zbuf, st, a_ref, NO, ti, d, second)
        if second:
            for o in range(NO):
                y_ref[:, o * OCT_CH:(o + 1) * OCT_CH] = _dot(zbuf[:, o * 1024:(o + 1) * 1024].astype(BF16), cf_ref[o])

        @pl.when(j == nb - 1)
        def _():
            fin_ref[...] = st[...]

    st_spec = pl.BlockSpec((None, 2, 8, NS), lambda d, j: (d, 0, 0, 0))
    in_specs = [pl.BlockSpec((R, D), lambda d, j: (blk(d, j), 0)), _full(mod8.shape),
                pl.BlockSpec((None, NO, OCT_CH, 1024), lambda d, j: (d, 0, 0, 0)), st_spec]
    args = [tok, mod8, bf, acoef]
    scratch = [pltpu.VMEM((R, NO * 1024), F32), pltpu.VMEM((2, 8, NS), F32)]
    if not second:
        return pl.pallas_call(body, name=name, grid=(2, nb), in_specs=in_specs, out_specs=st_spec,
                              out_shape=_sds((2, 2, 8, NS)), scratch_shapes=scratch, compiler_params=_cp(2))(*args)
    in_specs += [st_spec, pl.BlockSpec((None, NO, 1024, OCT_CH), lambda d, j: (d, 0, 0, 0))]
    args += [s_init, cf]
    aliases = {}
    if y_prev is not None:
        in_specs.append(ANY)
        args.append(y_prev)
        aliases = {6: 0}
    else:
        in_specs.append(_full((8, LANES)))
        args.append(jnp.zeros((8, LANES), F32))
    out_specs = [pl.BlockSpec((None, R, D), lambda d, j: (d, blk(d, j), 0)),
                 pl.BlockSpec((None, None, 2, 8, NS), lambda d, j: (d, j + d * (nb - 1 - 2 * j), 0, 0, 0)), st_spec]
    out_shape = [_sds((2, cfg.T, D)), _sds((2, nb, 2, 8, NS)), _sds((2, 2, 8, NS))]
    return pl.pallas_call(body, name=name, grid=(2, nb), in_specs=in_specs, out_specs=out_specs, out_shape=out_shape,
                          input_output_aliases=aliases, scratch_shapes=scratch, compiler_params=_cp(2))(*args)


def _s5_chain(cfg, name, fin_local, acoef, n, inc, prev_fin=None):
    NS = cfg.NS
    nsq = int(round(math.log2(n)))
    assert 2 ** nsq == n

    def body(*refs):
        if prev_fin is not None:
            f_ref, a_ref, p_ref, o_ref = refs
        else:
            f_ref, a_ref, o_ref = refs
        for d in range(2):
            pr, pi = a_ref[d, 0, 0:1, :], a_ref[d, 1, 0:1, :]
            for _ in range(nsq):
                pr, pi = pr * pr - pi * pi, 2.0 * pr * pi
            for b in range(2):
                order = [4 * b + k for k in range(4)]
                if not inc[d]:
                    order = order[::-1]
                if prev_fin is not None:
                    last = order[-1]
                    sr, si = p_ref[d, 0, last:last + 1, :], p_ref[d, 1, last:last + 1, :]
                else:
                    sr = jnp.zeros((1, NS), F32)
                    si = jnp.zeros((1, NS), F32)
                for k in order:
                    o_ref[d, 0, k:k + 1, :] = sr
                    o_ref[d, 1, k:k + 1, :] = si
                    fr, fi = f_ref[d, 0, k:k + 1, :], f_ref[d, 1, k:k + 1, :]
                    sr, si = pr * sr - pi * si + fr, pr * si + pi * sr + fi

    args = [fin_local, acoef] + ([prev_fin] if prev_fin is not None else [])
    return pl.pallas_call(body, name=name, out_shape=_sds((2, 2, 8, NS)))(*args)


def _s5_forward(cfg, tag, tok, mod8, col_sh, col_sc, bf, cf, acoef):
    saved = {}
    fin_prev = None
    y = None
    for ph, (r0, n) in (("c", (0, cfg.nc)), ("x", (cfg.Tc, cfg.nx))):
        m8 = mod8[0 if ph == "c" else 1]
        loc = _s5_fwd_pass(cfg, f"{tag}_scan1{ph}", tok, m8, col_sh, col_sc, bf, acoef, r0, n)
        s_in = _s5_chain(cfg, f"{tag}_chain{ph}", loc, acoef, n, (True, False), fin_prev)
        y, ck, fin_prev = _s5_fwd_pass(cfg, f"{tag}_scan2{ph}", tok, m8, col_sh, col_sc, bf, acoef, r0, n, s_in, cf, y)
        saved[ph] = ck
    return y, saved


def _s5_bwd_pass(cfg, name, dy, tok, mod8, col_sh, col_sc, bf, cf, acoef, acoef_adj, r0, n, g_init=None, ck=None,
                 du_prev=None):
    D, NO, NS = cfg.D, cfg.NO, cfg.NS
    ti = cfg.ti(n)
    nb = n // ti
    R = 8 * ti
    ob = r0 // R
    second = g_init is not None
    has_dy = dy is not None
    blk = lambda d, j: ob + j + (1 - d) * (nb - 1 - 2 * j)

    def body(*refs):
        refs = list(refs)
        dy_ref = refs.pop(0) if has_dy else None
        if second:
            (tok_ref, mod_ref, bf_ref, cf_ref, a_ref, aa_ref, gi_ref, ck_ref, dup_ref,
             du_ref, da_ref, dbf_ref, dcf_ref, gfin_ref, qbuf, zbuf, gst, hst) = refs
        else:
            cf_ref, aa_ref, gfin_ref, qbuf, gst = refs
        d = pl.program_id(0)
        j = pl.program_id(1)

        @pl.when(j == 0)
        def _():
            if second:
                gst[...] = gi_ref[...]
                da_ref[...] = jnp.zeros_like(da_ref)
                dbf_ref[...] = jnp.zeros_like(dbf_ref)
                dcf_ref[...] = jnp.zeros_like(dcf_ref)
            else:
                gst[...] = jnp.zeros_like(gst)

        if has_dy:
            dyb = dy_ref[...].astype(BF16)
            for o in range(NO):
                qbuf[:, o * 1024:(o + 1) * 1024] = _dot_nt(dyb[:, o * OCT_CH:(o + 1) * OCT_CH], cf_ref[o])
        else:
            qbuf[...] = jnp.zeros_like(qbuf)
        _recur(qbuf, gst, aa_ref, NO, ti, 1 - d, second)

        if second:
            u = _mod(tok_ref[...], mod_ref[:, col_sh:col_sh + D], mod_ref[:, col_sc:col_sc + D]).astype(BF16)
            for o in range(NO):
                zbuf[:, o * 1024:(o + 1) * 1024] = _dot(u[:, o * OCT_CH:(o + 1) * OCT_CH], bf_ref[o])
            hst[...] = ck_ref[...]
            _recur(zbuf, hst, a_ref, NO, ti, d, True)

            def da_acc(g_lo, h_lo, g_edge):
                for o in range(NO):
                    cr, ci = pl.ds(o * 1024, OCT_ST), pl.ds(o * 1024 + OCT_ST, OCT_ST)
                    sc = pl.ds(o * OCT_ST, OCT_ST)
                    gr, gi = qbuf[pl.ds(g_lo, R - 8), cr], qbuf[pl.ds(g_lo, R - 8), ci]
                    hr, hi = zbuf[pl.ds(h_lo, R - 8), cr], zbuf[pl.ds(h_lo, R - 8), ci]
                    er, ei = qbuf[pl.ds(g_edge, 8), cr], qbuf[pl.ds(g_edge, 8), ci]
                    kr, ki = ck_ref[0, :, sc], ck_ref[1, :, sc]
                    da_ref[0, :, sc] += _sum8(gr * hr + gi * hi) + (er * kr + ei * ki)
                    da_ref[1, :, sc] += _sum8(gi * hr - gr * hi) + (ei * kr - er * ki)

            if R > 8:
                @pl.when(d == 0)
                def _():
                    da_acc(8, 0, 0)

                @pl.when(d == 1)
                def _():
                    da_acc(0, 8, R - 8)
            else:
                for o in range(NO):
                    cr, ci = pl.ds(o * 1024, OCT_ST), pl.ds(o * 1024 + OCT_ST, OCT_ST)
                    sc = pl.ds(o * OCT_ST, OCT_ST)
                    er, ei = qbuf[:, cr], qbuf[:, ci]
                    kr, ki = ck_ref[0, :, sc], ck_ref[1, :, sc]
                    da_ref[0, :, sc] += er * kr + ei * ki
                    da_ref[1, :, sc] += ei * kr - er * ki

            for o in range(NO):
                gb = qbuf[:, o * 1024:(o + 1) * 1024].astype(BF16)
                uo = u[:, o * OCT_CH:(o + 1) * OCT_CH]
                dbf_ref[o] += _dot_tn(uo, gb)
                if has_dy:
                    dcf_ref[o] += _dot_tn(zbuf[:, o * 1024:(o + 1) * 1024].astype(BF16), dyb[:, o * OCT_CH:(o + 1) * OCT_CH])
                du_ref[:, o * OCT_CH:(o + 1) * OCT_CH] = _dot_nt(gb, bf_ref[o])

        @pl.when(j == nb - 1)
        def _():
            gfin_ref[...] = gst[...]

    st_spec = pl.BlockSpec((None, 2, 8, NS), lambda d, j: (d, 0, 0, 0))
    row_spec = pl.BlockSpec((R, D), lambda d, j: (blk(d, j), 0))
    bf_spec = pl.BlockSpec((None, NO, OCT_CH, 1024), lambda d, j: (d, 0, 0, 0))
    cf_spec = pl.BlockSpec((None, NO, 1024, OCT_CH), lambda d, j: (d, 0, 0, 0))
    in_specs, args = [], []
    if has_dy:
        in_specs.append(row_spec)
        args.append(dy)
    if not second:
        in_specs += [cf_spec, st_spec]
        args += [cf, acoef_adj]
        return pl.pallas_call(body, name=name, grid=(2, nb), in_specs=in_specs, out_specs=st_spec,
                              out_shape=_sds((2, 2, 8, NS)),
                              scratch_shapes=[pltpu.VMEM((R, NO * 1024), F32), pltpu.VMEM((2, 8, NS), F32)],
                              compiler_params=_cp(2))(*args)
    ck_spec = pl.BlockSpec((None, None, 2, 8, NS), lambda d, j: (d, j + (1 - d) * (nb - 1 - 2 * j), 0, 0, 0))
    in_specs += [row_spec, _full(mod8.shape), bf_spec, cf_spec, st_spec, st_spec, st_spec, ck_spec]
    args += [tok, mod8, bf, cf, acoef, acoef_adj, g_init, ck]
    n_before = len(args)
    aliases = {}
    if du_prev is not None:
        in_specs.append(ANY)
        args.append(du_prev)
        aliases = {n_before: 0}
    else:
        in_specs.append(_full((8, LANES)))
        args.append(jnp.zeros((8, LANES), F32))
    out_specs = [pl.BlockSpec((None, R, D), lambda d, j: (d, blk(d, j), 0)), st_spec, bf_spec, cf_spec, st_spec]
    out_shape = [_sds((2, cfg.T, D)), _sds((2, 2, 8, NS)), _sds((2, NO, OCT_CH, 1024)), _sds((2, NO, 1024, OCT_CH)),
                 _sds((2, 2, 8, NS))]
    scratch = [pltpu.VMEM((R, NO * 1024), F32), pltpu.VMEM((R, NO * 1024), F32), pltpu.VMEM((2, 8, NS), F32),
               pltpu.VMEM((2, 8, NS), F32)]
    return pl.pallas_call(body, name=name, grid=(2, nb), in_specs=in_specs, out_specs=out_specs, out_shape=out_shape,
                          input_output_aliases=aliases, scratch_shapes=scratch, compiler_params=_cp(2))(*args)


def _s5_backward(cfg, tag, dy, dy_ctx, tok, mod8, col_sh, col_sc, bf, cf, acoef, acoef_adj, saved):
    g_prev = None
    acc = None
    du = None
    for ph, (r0, n) in (("x", (cfg.Tc, cfg.nx)), ("c", (0, cfg.nc))):
        m8 = mod8[0 if ph == "c" else 1]
        dyp = dy if (ph == "x" or dy_ctx) else None
        loc = _s5_bwd_pass(cfg, f"{tag}_adjA{ph}", dyp, tok, m8, col_sh, col_sc, bf, cf, acoef, acoef_adj, r0, n)
        g_in = _s5_chain(cfg, f"{tag}_adjchain{ph}", loc, acoef_adj, n, (False, True), g_prev)
        du, da, dbf, dcf, g_prev = _s5_bwd_pass(cfg, f"{tag}_adjB{ph}", dyp, tok, m8, col_sh, col_sc, bf, cf, acoef,
                                                acoef_adj, r0, n, g_in, saved[ph], du)
        new = (da, dbf, dcf)
        if acc is None:
            acc = new
        else:
            acc = tuple(_ew(f"{tag}_accsum{q}", lambda a, b: a + b, [a.reshape(-1, a.shape[-1]), b.reshape(-1, b.shape[-1])],
                            [_sds((a.size // a.shape[-1], a.shape[-1]))])[0].reshape(a.shape)
                        for q, (a, b) in enumerate(zip(acc, new)))
    return du, acc


def _tok_specs(cfg, rows, width, tile=None):
    tm = tile or cfg.TM
    off = rows[0] // tm
    return pl.BlockSpec((tm, width), lambda i: (i + off, 0)), rows[1] // tm, off


def _mod_spec(cfg, mod8, off):
    st = _stream_of(cfg, off, cfg.TM)
    return pl.BlockSpec((None, 8, mod8.shape[-1]), lambda i: (st(i), 0, 0))


def _wspec(w):
    fam, slot = w
    _, _, kk, nn = fam.shape
    return pl.BlockSpec((4, None, kk, nn), lambda *i: (0, slot, 0, 0), pipeline_mode=pl.Buffered(1))


def _glu_ln(cfg, name, rows, tok, y, mod8, cols, dskip, w, b, gain, bias):
    D, TM = cfg.D, cfg.TM
    csh, csc, cg = cols
    spec, nt, off = _tok_specs(cfg, rows, D)
    spec2, _, _ = _tok_specs(cfg, rows, 2 * D)

    def body(tok_ref, y_ref, mod_ref, ds_ref, w_ref, b_ref, g_ref, bi_ref, x1_ref, r1_ref, mix_ref, zz_ref, zb_ref, yy_ref):
        t = tok_ref[...]
        u = _mod(t, mod_ref[:, csh:csh + D], mod_ref[:, csc:csc + D])
        yy = ds_ref[...] * u + y_ref[0] + y_ref[1]
        zb = _gelu(yy).astype(BF16)
        zz = jnp.concatenate([_dot(zb, w_ref[s]) for s in range(4)], axis=-1) + b_ref[...]
        mix = zz[:, :D] * _sigmoid(zz[:, D:])
        r1 = DN_ALPHA * t + _rowscale(mix, mod_ref[:, cg:cg + D])
        xhat, _ = _ln_stats(r1)
        x1_ref[...] = xhat * g_ref[...] + bi_ref[...]
        r1_ref[...] = r1
        mix_ref[...] = mix
        zz_ref[...] = zz
        zb_ref[...] = zb
        yy_ref[...] = yy

    T = cfg.T
    return pl.pallas_call(
        body, name=name, grid=(nt,),
        in_specs=[spec, pl.BlockSpec((2, TM, D), lambda i: (0, i + off, 0)), _mod_spec(cfg, mod8, off), _full((1, D)),
                  _wspec(w), _full((1, 2 * D)), _full((1, D)), _full((1, D))],
        out_specs=[spec, spec, spec, spec2, spec, spec],
        out_shape=[_sds((T, D)), _sds((T, D)), _sds((T, D)), _sds((T, 2 * D)), _sds((T, D), BF16), _sds((T, D))],
        compiler_params=_cp(1))(tok, y, mod8, dskip, w[0], b, gain, bias)


def _mlp_ln(cfg, name, rows, x1, mod8, cols, w1, w2, gain, bias):
    D, TM = cfg.D, cfg.TM
    csh, csc, cg = cols
    spec, nt, off = _tok_specs(cfg, rows, D)
    specf, _, _ = _tok_specs(cfg, rows, cfg.F)
    fb = cfg.F // 4

    def body(x_ref, mod_ref, w1_ref, w2_ref, g_ref, bi_ref, x2_ref, r2_ref, out_ref, a_ref, h_ref):
        t = x_ref[...]
        h = _mod(t, mod_ref[:, csh:csh + D], mod_ref[:, csc:csc + D]).astype(BF16)
        out = jnp.zeros((TM, D), F32)
        for s in range(4):
            hid = jnp.maximum(_dot(h, w1_ref[s]), 0.0)
            a = (hid * hid).astype(BF16)
            a_ref[:, s * fb:(s + 1) * fb] = a
            out = out + _dot(a, w2_ref[s])
        r2 = DN_ALPHA * t + _rowscale(out, mod_ref[:, cg:cg + D])
        xhat, _ = _ln_stats(r2)
        x2_ref[...] = xhat * g_ref[...] + bi_ref[...]
        r2_ref[...] = r2
        out_ref[...] = out
        h_ref[...] = h

    T = cfg.T
    return pl.pallas_call(
        body, name=name, grid=(nt,),
        in_specs=[spec, _mod_spec(cfg, mod8, off), _wspec(w1), _wspec(w2), _full((1, D)), _full((1, D))],
        out_specs=[spec, spec, spec, specf, spec],
        out_shape=[_sds((T, D)), _sds((T, D)), _sds((T, D)), _sds((T, cfg.F), BF16), _sds((T, D), BF16)],
        compiler_params=_cp(1))(x1, mod8, w1[0], w2[0], gain, bias)


def _pw1_glu(cfg, name, rows, tok, mod8, cols, w, b):
    D, TM = cfg.D, cfg.TM
    csh, csc = cols
    spec, nt, off = _tok_specs(cfg, rows, D)
    spec2, _, _ = _tok_specs(cfg, rows, 2 * D)

    def body(tok_ref, mod_ref, w_ref, b_ref, aa_ref, ag_ref, h_ref):
        h = _mod(tok_ref[...], mod_ref[:, csh:csh + D], mod_ref[:, csc:csc + D]).astype(BF16)
        aa = jnp.concatenate([_dot(h, w_ref[s]) for s in range(4)], axis=-1) + b_ref[...]
        aa_ref[...] = aa
        ag_ref[...] = aa[:, :D] * _sigmoid(aa[:, D:])
        h_ref[...] = h

    T = cfg.T
    return pl.pallas_call(
        body, name=name, grid=(nt,),
        in_specs=[spec, _mod_spec(cfg, mod8, off), _wspec(w), _full((1, 2 * D))],
        out_specs=[spec2, spec, spec],
        out_shape=[_sds((T, 2 * D)), _sds((T, D)), _sds((T, D), BF16)], compiler_params=_cp(1))(tok, mod8, w[0], b)


def _halo_maps(cfg, rows):
    TM, HB = cfg.TM, cfg.HB
    off = rows[0] // TM
    nct = cfg.Tc // TM
    ntx = cfg.Tx // TM

    def phase(i):
        t = i + off
        is_x = t >= nct
        first = jnp.where(is_x, nct, 0)
        cnt = jnp.where(is_x, ntx, nct)
        return t, first, cnt

    def prev(i):
        t, first, cnt = phase(i)
        return jnp.where(t == first, 2 * (first + cnt) - 1, 2 * t - 1), 0

    def nxt(i):
        t, first, cnt = phase(i)
        return jnp.where(t == first + cnt - 1, 2 * first, 2 * t + 2), 0

    def edge(i):
        t, first, cnt = phase(i)
        return t == first, t == first + cnt - 1

    return prev, nxt, edge, off


def _halo_fix(prev, nxt, is_first, is_last):
    hb, d = prev.shape
    k = lax.broadcasted_iota(jnp.int32, (hb // 8, 8, d), 1)
    p3 = prev.reshape(hb // 8, 8, d)
    n3 = nxt.reshape(hb // 8, 8, d)
    p_roll = jnp.where((k % 4) == 0, 0.0, pltpu.roll(p3, 1, 1))
    n_roll = jnp.where((k % 4) == 3, 0.0, pltpu.roll(n3, 7, 1))
    p3 = jnp.where(is_first, p_roll, p3)
    n3 = jnp.where(is_last, n_roll, n3)
    return p3.reshape(hb, d), n3.reshape(hb, d)


def _dwconv_ln(cfg, name, rows, ag, w_dw, b_dw, ln_g, ln_b):
    D, TM, HB, KW, half = cfg.D, cfg.TM, cfg.HB, cfg.KW, cfg.half
    prev_map, next_map, edge, off = _halo_maps(cfg, rows)
    spec, nt, _ = _tok_specs(cfg, rows, D)

    def body(cur_ref, prev_ref, next_ref, w_ref, b_ref, g_ref, bi_ref, cv_ref, s_ref, ext):
        i = pl.program_id(0)
        is_first, is_last = edge(i)
        p, n = _halo_fix(prev_ref[...], next_ref[...], is_first, is_last)
        ext[0:HB, :] = p
        ext[HB:HB + TM, :] = cur_ref[...]
        ext[HB + TM:, :] = n
        acc = jnp.zeros((TM, D), F32)
        for k in range(KW):
            lo = HB + 8 * (k - half)
            acc = acc + w_ref[k:k + 1, :] * ext[lo:lo + TM, :]
        cv = acc + b_ref[...]
        xhat, _ = _ln_stats(cv)
        nn = xhat * g_ref[...] + bi_ref[...]
        cv_ref[...] = cv
        s_ref[...] = (nn * _sigmoid(nn)).astype(BF16)

    T = cfg.T
    return pl.pallas_call(
        body, name=name, grid=(nt,),
        in_specs=[spec, pl.BlockSpec((HB, D), prev_map), pl.BlockSpec((HB, D), next_map), _full((KW, D)),
                  _full((1, D)), _full((1, D)), _full((1, D))],
        out_specs=[spec, spec], out_shape=[_sds((T, D)), _sds((T, D), BF16)],
        scratch_shapes=[pltpu.VMEM((TM + 2 * HB, D), F32)], compiler_params=_cp(1))(ag, ag, ag, w_dw, b_dw, ln_g, ln_b)


def _pw2_ln(cfg, name, rows, s, tok, mod8, cg, w, b, gain, bias):
    D, TM = cfg.D, cfg.TM
    spec, nt, off = _tok_specs(cfg, rows, D)
    kb = D // 4

    def body(s_ref, tok_ref, mod_ref, w_ref, b_ref, g_ref, bi_ref, x1_ref, r1_ref, mix_ref):
        sv = s_ref[...]
        mix = b_ref[...] + jnp.zeros((TM, D), F32)
        for q in range(4):
            mix = mix + _dot(sv[:, q * kb:(q + 1) * kb], w_ref[q])
        r1 = DN_ALPHA * tok_ref[...] + _rowscale(mix, mod_ref[:, cg:cg + D])
        xhat, _ = _ln_stats(r1)
        x1_ref[...] = xhat * g_ref[...] + bi_ref[...]
        r1_ref[...] = r1
        mix_ref[...] = mix

    T = cfg.T
    return pl.pallas_call(
        body, name=name, grid=(nt,),
        in_specs=[spec, spec, _mod_spec(cfg, mod8, off), _wspec(w), _full((1, D)), _full((1, D)), _full((1, D))],
        out_specs=[spec, spec, spec], out_shape=[_sds((T, D))] * 3, compiler_params=_cp(1))(s, tok, mod8, w[0], b, gain, bias)


def _loss(cfg, xf, tgt):
    D, TM = cfg.D, cfg.TM
    spec, nt, off = _tok_specs(cfg, cfg.rows(False), D)

    def body(x_ref, t_ref, l_ref, dx_ref, acc):
        i = pl.program_id(0)
        dlt = x_ref[...] - t_ref[...]

        @pl.when(i == 0)
        def _():
            acc[...] = jnp.zeros_like(acc)

        acc[...] += _sum8(dlt * dlt)
        dx_ref[...] = dlt * (1.0 / D)

        @pl.when(i == nt - 1)
        def _():
            l_ref[...] = jnp.zeros((8, LANES), F32) + jnp.sum(acc[...]) * (0.5 / D)

    return pl.pallas_call(
        body, name="loss", grid=(nt,),
        in_specs=[spec, pl.BlockSpec((TM, D), lambda i: (i, 0))],
        out_specs=[_full((8, LANES)), spec], out_shape=[_sds((8, LANES)), _sds((cfg.T, D))],
        scratch_shapes=[pltpu.VMEM((8, D), F32)], compiler_params=_cp(1))(xf, tgt)


def _masked_spec(cfg, rows, width, valid_from_tile):
    tm = cfg.TM
    off = rows[0] // tm
    return pl.BlockSpec((tm, width), lambda i: (jnp.maximum(i + off, valid_from_tile), 0))


def _lnb(cfg, name, rows, dres, dres_ctx_ok, dh, r, aux, gain, mod_gate, cg, mod_next, csc):
    D, TM = cfg.D, cfg.TM
    spec, nt, off = _tok_specs(cfg, rows, D)
    nct = cfg.Tc // TM
    has_dres, has_dh = dres is not None, dh is not None

    def body(*refs):
        refs = list(refs)
        dres_ref = refs.pop(0) if has_dres else None
        dh_ref = refs.pop(0) if has_dh else None
        r_ref, aux_ref, g_ref, mg_ref = refs[:4]
        refs = refs[4:]
        mn_ref = refs.pop(0) if has_dh else None
        dprev_ref, dbr_ref, dgain_ref, dbias_ref, dg_ref, dsc_ref, dsh_ref, acc_g, acc_b = refs
        i = pl.program_id(0)
        t = i + off
        first_of_stream = (i == 0) | (t == nct)
        xhat, rstd = _ln_stats(r_ref[...])
        dy = jnp.zeros((TM, D), F32)
        if has_dres:
            dv = dres_ref[...]
            if not dres_ctx_ok:
                dv = jnp.where(t >= nct, dv, 0.0)
            dy = dy + dv
        if has_dh:
            dhv = dh_ref[...]
            dy = dy + _rowscale(dhv, 1.0 + mn_ref[:, csc:csc + D])
            x_out = xhat * g_ref[0:1, :] + g_ref[1:2, :]
            s_sc, s_sh = _sum8(dhv * x_out), _sum8(dhv)
        else:
            s_sc = s_sh = jnp.zeros((8, D), F32)
        dr = _ln_bwd(dy * g_ref[0:1, :], xhat, rstd)
        s_g = _sum8(dr * aux_ref[...])

        @pl.when(i == 0)
        def _():
            acc_g[...] = jnp.zeros_like(acc_g)
            acc_b[...] = jnp.zeros_like(acc_b)

        acc_g[...] += _sum8(dy * xhat)
        acc_b[...] += _sum8(dy)

        @pl.when(first_of_stream)
        def _():
            dg_ref[...] = s_g
            dsc_ref[...] = s_sc
            dsh_ref[...] = s_sh

        @pl.when(jnp.logical_not(first_of_stream))
        def _():
            dg_ref[...] += s_g
            dsc_ref[...] += s_sc
            dsh_ref[...] += s_sh

        dprev_ref[...] = DN_ALPHA * dr
        dbr_ref[...] = _rowscale(dr, mg_ref[:, cg:cg + D])

        @pl.when(i == nt - 1)
        def _():
            dgain_ref[...] = jnp.sum(acc_g[...], axis=0, keepdims=True)
            dbias_ref[...] = jnp.sum(acc_b[...], axis=0, keepdims=True)

    st = _stream_of(cfg, off, TM)
    in_specs, args = [], []
    if has_dres:
        in_specs.append(spec if dres_ctx_ok else _masked_spec(cfg, rows, D, nct))
        args.append(dres)
    if has_dh:
        in_specs.append(spec)
        args.append(dh)
    in_specs += [spec, spec, _full((2, D)), _mod_spec(cfg, mod_gate, off)]
    args += [r, aux, gain, mod_gate]
    if has_dh:
        in_specs.append(_mod_spec(cfg, mod_next, off))
        args.append(mod_next)
    acc_spec = pl.BlockSpec((None, 8, D), lambda i: (st(i), 0, 0))
    T = cfg.T
    return pl.pallas_call(
        body, name=name, grid=(nt,), in_specs=in_specs,
        out_specs=[spec, spec, _full((1, D)), _full((1, D)), acc_spec, acc_spec, acc_spec],
        out_shape=[_sds((T, D)), _sds((T, D)), _sds((1, D)), _sds((1, D)), _sds((2, 8, D)), _sds((2, 8, D)), _sds((2, 8, D))],
        scratch_shapes=[pltpu.VMEM((8, D), F32), pltpu.VMEM((8, D), F32)], compiler_params=_cp(1))(*args)


def _mlp_bwd(cfg, name, rows, dbr, a, w1, w2):
    D, TM = cfg.D, cfg.TM
    spec, nt, off = _tok_specs(cfg, rows, D)
    specf, _, _ = _tok_specs(cfg, rows, cfg.F)
    fb = cfg.F // 4

    def body(d_ref, a_ref, w1_ref, w2_ref, dh_ref, dhid_ref, dout_ref):
        dout = d_ref[...].astype(BF16)
        dh = jnp.zeros((TM, D), F32)
        for s in range(4):
            da = _dot_nt(dout, w2_ref[s])
            dhid = (da * (2.0 * jnp.sqrt(a_ref[:, s * fb:(s + 1) * fb].astype(F32)))).astype(BF16)
            dhid_ref[:, s * fb:(s + 1) * fb] = dhid
            dh = dh + _dot_nt(dhid, w1_ref[s])
        dh_ref[...] = dh
        dout_ref[...] = dout

    T = cfg.T
    return pl.pallas_call(
        body, name=name, grid=(nt,),
        in_specs=[spec, specf, _wspec(w1), _wspec(w2)],
        out_specs=[spec, specf, spec],
        out_shape=[_sds((T, D)), _sds((T, cfg.F), BF16), _sds((T, D), BF16)], compiler_params=_cp(1))(dbr, a, w1[0], w2[0])


def _wgrad(cfg, name, rows, a, b, mode, fam, slot):
    tw = cfg.TW
    off = rows[0] // tw
    nt = rows[1] // tw
    fresh = not hasattr(fam, "dtype")
    fam_shape = tuple(fam) if fresh else fam.shape
    _, n, kk, nn = fam_shape

    def body(a_ref, b_ref, *rest):
        o_ref = rest[-1]
        t = pl.program_id(1)
        part = _dot_tn(a_ref[...], b_ref[...])

        @pl.when(t == 0)
        def _():
            o_ref[...] = part

        @pl.when(t > 0)
        def _():
            o_ref[...] += part

    if mode == "col":
        a_spec = pl.BlockSpec((tw, kk), lambda s, t: (t + off, 0))
        b_spec = pl.BlockSpec((tw, nn), lambda s, t: (t + off, s))
    else:
        a_spec = pl.BlockSpec((tw, kk), lambda s, t: (t + off, s))
        b_spec = pl.BlockSpec((tw, nn), lambda s, t: (t + off, 0))
    out_spec = pl.BlockSpec((None, None, kk, nn), lambda s, t: (s, slot, 0, 0))
    if fresh:
        return pl.pallas_call(body, name=name, grid=(4, nt), in_specs=[a_spec, b_spec], out_specs=out_spec,
                              out_shape=_sds(fam_shape), compiler_params=_cp(2))(a, b)
    return pl.pallas_call(body, name=name, grid=(4, nt), in_specs=[a_spec, b_spec, ANY], out_specs=out_spec,
                          out_shape=_sds(fam_shape), input_output_aliases={2: 0}, compiler_params=_cp(2))(a, b, fam)


def _glu_bwd(cfg, name, rows, dmix, pre, w, yy=None):
    D, TM = cfg.D, cfg.TM
    spec, nt, off = _tok_specs(cfg, rows, D)
    spec2, _, _ = _tok_specs(cfg, rows, 2 * D)
    hw = w[0].shape[-1]
    has_y = yy is not None

    def body(*refs):
        refs = list(refs)
        d_ref, p_ref, w_ref = refs[:3]
        y_ref = refs[3] if has_y else None
        dz_ref, dp_ref, db_ref, acc = refs[-4:]
        i = pl.program_id(0)
        dm = d_ref[...]
        po, pg = p_ref[:, :D], p_ref[:, D:]
        sg = _sigmoid(pg)
        dpre = jnp.concatenate([dm * sg, dm * po * sg * (1.0 - sg)], axis=-1)

        @pl.when(i == 0)
        def _():
            acc[...] = jnp.zeros_like(acc)

        acc[...] += _sum8(dpre)
        dpb = dpre.astype(BF16)
        dz = jnp.zeros((TM, D), F32)
        for s in range(4):
            dz = dz + _dot_nt(dpb[:, s * hw:(s + 1) * hw], w_ref[s])
        if has_y:
            dz = dz * _gelu_grad(y_ref[...])
        dz_ref[...] = dz
        dp_ref[...] = dpb

        @pl.when(i == nt - 1)
        def _():
            db_ref[...] = jnp.sum(acc[...], axis=0, keepdims=True)

    T = cfg.T
    in_specs = [spec, spec2, _wspec(w)] + ([spec] if has_y else [])
    args = [dmix, pre, w[0]] + ([yy] if has_y else [])
    return pl.pallas_call(
        body, name=name, grid=(nt,), in_specs=in_specs, out_specs=[spec, spec2, _full((1, 2 * D))],
        out_shape=[_sds((T, D)), _sds((T, 2 * D), BF16), _sds((1, 2 * D))],
        scratch_shapes=[pltpu.VMEM((8, 2 * D), F32)], compiler_params=_cp(1))(*args)


def _s5_du(cfg, name, rows, du, dy, dy_from_tile, tok, mod8, cols, dskip):
    D, TM = cfg.D, cfg.TM
    csh, csc = cols
    spec, nt, off = _tok_specs(cfg, rows, D)

    def body(du_ref, dy_ref, tok_ref, mod_ref, ds_ref, dh_ref, dd_ref, acc):
        i = pl.program_id(0)
        dyv = jnp.where(i + off >= dy_from_tile, dy_ref[...], 0.0)
        u = _mod(tok_ref[...], mod_ref[:, csh:csh + D], mod_ref[:, csc:csc + D])
        dh_ref[...] = du_ref[0] + du_ref[1] + ds_ref[...] * dyv

        @pl.when(i == 0)
        def _():
            acc[...] = jnp.zeros_like(acc)

        acc[...] += _sum8(dyv * u)

        @pl.when(i == nt - 1)
        def _():
            dd_ref[...] = jnp.sum(acc[...], axis=0, keepdims=True)

    T = cfg.T
    return pl.pallas_call(
        body, name=name, grid=(nt,),
        in_specs=[pl.BlockSpec((2, TM, D), lambda i: (0, i + off, 0)), _masked_spec(cfg, rows, D, dy_from_tile), spec,
                  _mod_spec(cfg, mod8, off), _full((1, D))],
        out_specs=[spec, _full((1, D))], out_shape=[_sds((T, D)), _sds((1, D))],
        scratch_shapes=[pltpu.VMEM((8, D), F32)], compiler_params=_cp(1))(du, dy, tok, mod8, dskip)


def _pw2_bwd(cfg, name, rows, dmix, cv, w, ln_g, ln_b):
    D, TM = cfg.D, cfg.TM
    spec, nt, off = _tok_specs(cfg, rows, D)
    kb = D // 4

    def body(d_ref, cv_ref, w_ref, g_ref, b_ref, dcv_ref, dmb_ref, sums_ref, acc):
        i = pl.program_id(0)
        dm = d_ref[...]
        dmb = dm.astype(BF16)
        ds = jnp.concatenate([_dot_nt(dmb, w_ref[q]) for q in range(4)], axis=-1)
        xhat, rstd = _ln_stats(cv_ref[...])
        nn = xhat * g_ref[...] + b_ref[...]
        sg = _sigmoid(nn)
        dn = ds * (sg * (1.0 + nn * (1.0 - sg)))
        dcv = _ln_bwd(dn * g_ref[...], xhat, rstd)

        @pl.when(i == 0)
        def _():
            acc[...] = jnp.zeros_like(acc)

        acc[0] += _sum8(dn * xhat)
        acc[1] += _sum8(dn)
        acc[2] += _sum8(dcv)
        acc[3] += _sum8(dm)
        dcv_ref[...] = dcv
        dmb_ref[...] = dmb

        @pl.when(i == nt - 1)
        def _():
            for q in range(4):
                sums_ref[q:q + 1, :] = jnp.sum(acc[q], axis=0, keepdims=True)

    T = cfg.T
    return pl.pallas_call(
        body, name=name, grid=(nt,),
        in_specs=[spec, spec, _wspec(w), _full((1, D)), _full((1, D))],
        out_specs=[spec, spec, _full((4, D))], out_shape=[_sds((T, D)), _sds((T, D), BF16), _sds((4, D))],
        scratch_shapes=[pltpu.VMEM((4, 8, D), F32)], compiler_params=_cp(1))(dmix, cv, w[0], ln_g, ln_b)


def _dwconv_bwd(cfg, name, rows, dcv, ag, w_dw):
    D, TM, HB, KW, half = cfg.D, cfg.TM, cfg.HB, cfg.KW, cfg.half
    prev_map, next_map, edge, off = _halo_maps(cfg, rows)
    spec, nt, _ = _tok_specs(cfg, rows, D)

    def body(dc_ref, dp_ref, dn_ref, ac_ref, ap_ref, an_ref, w_ref, dag_ref, dw_ref, extd, exta, acc):
        i = pl.program_id(0)
        is_first, is_last = edge(i)
        p, n = _halo_fix(dp_ref[...], dn_ref[...], is_first, is_last)
        extd[0:HB, :] = p
        extd[HB:HB + TM, :] = dc_ref[...]
        extd[HB + TM:, :] = n
        p, n = _halo_fix(ap_ref[...], an_ref[...], is_first, is_last)
        exta[0:HB, :] = p
        exta[HB:HB + TM, :] = ac_ref[...]
        exta[HB + TM:, :] = n

        @pl.when(i == 0)
        def _():
            acc[...] = jnp.zeros_like(acc)

        dcur = dc_ref[...]
        dag = jnp.zeros((TM, D), F32)
        for k in range(KW):
            lo = HB + 8 * (half - k)
            dag = dag + w_ref[k:k + 1, :] * extd[lo:lo + TM, :]
            la = HB + 8 * (k - half)
            acc[k] += _sum8(dcur * exta[la:la + TM, :])
        dag_ref[...] = dag

        @pl.when(i == nt - 1)
        def _():
            for k in range(KW):
                dw_ref[k:k + 1, :] = jnp.sum(acc[k], axis=0, keepdims=True)

    T = cfg.T
    hp, hn = pl.BlockSpec((HB, D), prev_map), pl.BlockSpec((HB, D), next_map)
    return pl.pallas_call(
        body, name=name, grid=(nt,), in_specs=[spec, hp, hn, spec, hp, hn, _full((KW, D))],
        out_specs=[spec, _full((KW, D))], out_shape=[_sds((T, D)), _sds((KW, D))],
        scratch_shapes=[pltpu.VMEM((TM + 2 * HB, D), F32), pltpu.VMEM((TM + 2 * HB, D), F32), pltpu.VMEM((KW, 8, D), F32)],
        compiler_params=_cp(1))(dcv, dcv, dcv, ag, ag, ag, w_dw)


def _input_bwd(cfg, dres, dh, tok0, mod8, csc):
    D, TM = cfg.D, cfg.TM
    rows = cfg.rows(True)
    spec, nt, off = _tok_specs(cfg, rows, D)
    nct = cfg.Tc // TM
    st = _stream_of(cfg, off, TM)

    def body(dr_ref, dh_ref, t_ref, mod_ref, gx_ref, dsc_ref, dsh_ref):
        i = pl.program_id(0)
        dhv = dh_ref[...]
        gx_ref[...] = dr_ref[...] + _rowscale(dhv, 1.0 + mod_ref[:, csc:csc + D])
        first = (i == 0) | (i == nct)
        s_sc, s_sh = _sum8(dhv * t_ref[...]), _sum8(dhv)

        @pl.when(first)
        def _():
            dsc_ref[...] = s_sc
            dsh_ref[...] = s_sh

        @pl.when(jnp.logical_not(first))
        def _():
            dsc_ref[...] += s_sc
            dsh_ref[...] += s_sh

    acc_spec = pl.BlockSpec((None, 8, D), lambda i: (st(i), 0, 0))
    return pl.pallas_call(
        body, name="input_bwd", grid=(nt,), in_specs=[spec, spec, spec, _mod_spec(cfg, mod8, off)],
        out_specs=[spec, acc_spec, acc_spec], out_shape=[_sds((cfg.T, D)), _sds((2, 8, D)), _sds((2, 8, D))],
        compiler_params=_cp(1))(dres, dh, tok0, mod8)


def _dmod_rows(dmod8):
    nl, _, _, w = dmod8.shape

    def body(d_ref, o_ref):
        xs = d_ref[1]
        cs = d_ref[0]
        o_ref[...] = jnp.zeros((8, w), F32)
        o_ref[0:1, :] = jnp.sum(xs[0:4], axis=0, keepdims=True)
        o_ref[1:2, :] = jnp.sum(xs[4:8], axis=0, keepdims=True)
        o_ref[2:3, :] = jnp.sum(cs, axis=0, keepdims=True)

    return pl.pallas_call(body, name="dmod_rows", grid=(nl,),
                          in_specs=[pl.BlockSpec((None, 2, 8, w), lambda l: (l, 0, 0, 0))],
                          out_specs=pl.BlockSpec((None, 8, w), lambda l: (l, 0, 0)), out_shape=_sds((nl, 8, w)),
                          compiler_params=_cp(1))(dmod8)


def _x_only(acc):
    return jnp.concatenate([jnp.zeros_like(acc[:1]), acc[1:]], axis=0)


def _pack(parts):
    bufs, meta, off = [], [], 0
    for p in parts:
        n = p.size
        rows = -(-n // (8 * LANES)) * 8
        flat = jnp.pad(p.reshape(-1).astype(F32), (0, rows * LANES - n)).reshape(rows, LANES)
        bufs.append(flat)
        meta.append((off, rows, p.shape))
        off += rows
    if off % 16:
        bufs.append(jnp.zeros((8, LANES), F32))
    return jnp.concatenate(bufs, axis=0), meta


def _unpack(buf, meta):
    out = []
    for off, rows, shape in meta:
        n = 1
        for s in shape:
            n *= s
        out.append(buf[off:off + rows].reshape(-1)[:n].reshape(shape))
    return out


def kernel(x, c, ctx, c_ctx, w_ada, b_ada, ln_gain, ln_bias, s5_lam_re, s5_lam_im, s5_log_dt, s5_b_re, s5_b_im, s5_c_re, s5_c_im, s5_d, s5_w_glu, s5_b_glu, cv_w_pw1, cv_b_pw1, cv_w_dw, cv_b_dw, cv_ln_g, cv_ln_b, cv_w_pw2, cv_b_pw2, mlp_w1, mlp_w2, loss_target, m_c_ctx, m_w_ada, m_b_ada, m_ln_gain, m_ln_bias, m_s5_lam_re, m_s5_lam_im, m_s5_log_dt, m_s5_b_re, m_s5_b_im, m_s5_c_re, m_s5_c_im, m_s5_d, m_s5_w_glu, m_s5_b_glu, m_cv_w_pw1, m_cv_b_pw1, m_cv_w_dw, m_cv_b_dw, m_cv_ln_g, m_cv_ln_b, m_cv_w_pw2, m_cv_b_pw2, m_mlp_w1, m_mlp_w2, v_c_ctx, v_w_ada, v_b_ada, v_ln_gain, v_ln_bias, v_s5_lam_re, v_s5_lam_im, v_s5_log_dt, v_s5_b_re, v_s5_b_im, v_s5_c_re, v_s5_c_im, v_s5_d, v_s5_w_glu, v_s5_b_glu, v_cv_w_pw1, v_cv_b_pw1, v_cv_w_dw, v_cv_b_dw, v_cv_ln_g, v_cv_ln_b, v_cv_w_pw2, v_cv_b_pw2, v_mlp_w1, v_mlp_w2):
    cfg = _Cfg(x, ctx, mlp_w1, cv_w_dw)
    D, T, Tc, Tx, B = cfg.D, cfg.T, cfg.Tc, cfg.Tx, cfg.B
    ax, ay, ac = lax.axis_index("x"), lax.axis_index("y"), lax.axis_index("c")
    shard = 2 * ax + ay
    dev = 4 * ax + 2 * ay + ac
    Ds = D // 4
    Wa = w_ada.shape[2]

    c_pad = jnp.concatenate([c, jnp.zeros((8 - B, D), F32)], axis=0)
    c_gath = _allgather8("gather_c", c_pad).reshape(8, 8, D)[:, :B].reshape(8 * B, D)
    c_all = jnp.concatenate([c_gath, c_ctx[None], jnp.zeros((7, D), F32)], axis=0)
    b_sh = lax.dynamic_slice_in_dim(b_ada, shard * Wa, Wa, axis=1)[:, None, :]
    mod_sh = _ada_fwd(c_all, w_ada, b_sh)
    mod_g = _allgather8("gather_mod", mod_sh.reshape(DEPTH * 24, Wa)).reshape(4, 2, DEPTH, 24, Wa)[:, 0]
    mods = mod_g.transpose(1, 2, 0, 3).reshape(DEPTH, 24, 4 * Wa)
    mine = lax.dynamic_slice_in_dim(mods, B * dev, B, axis=1)
    mod8 = jnp.stack([jnp.broadcast_to(mods[:, 16:17], (DEPTH, 8, 6 * D)), jnp.repeat(mine, 4, axis=1)], axis=1)
    SH1, SC1, G1, SH2, SC2, G2 = (k * D for k in range(6))

    small_parts = [ln_gain.reshape(-1, Ds), ln_bias.reshape(-1, Ds), cv_b_pw1.reshape(-1, Ds), cv_w_dw.reshape(-1, Ds),
                   cv_b_dw, cv_ln_g, cv_ln_b, cv_b_pw2]
    small_rows = [p.shape[0] for p in small_parts]
    sm = jnp.concatenate(small_parts, axis=0)
    pad_r = -sm.shape[0] % 8
    sm = jnp.pad(sm, ((0, pad_r), (0, 0)))
    sm_g = _allgather8("gather_small", sm).reshape(4, 2, sm.shape[0], Ds)[:, 0]
    pieces, o = [], 0
    for nr in small_rows:
        pieces.append(sm_g[:, o:o + nr])
        o += nr

    def unshard(p, lead):
        return p.reshape((4,) + lead + (Ds,)).transpose(tuple(range(1, len(lead) + 1)) + (0, len(lead) + 1)).reshape(lead + (4 * Ds,))

    ln_gain_f = unshard(pieces[0], (DEPTH, 2))
    ln_bias_f = unshard(pieces[1], (DEPTH, 2))
    nconv = cv_w_dw.shape[0]
    b_pw1_f = pieces[2].reshape(4, nconv, 2 * D // 4).transpose(1, 0, 2).reshape(nconv, 2 * D)
    w_dw_f = unshard(pieces[3], (nconv, cfg.KW))
    b_dw_f, cvg_f, cvb_f, b_pw2_f = (unshard(p, (nconv,)) for p in pieces[4:8])

    ns5 = s5_w_glu.shape[0]
    assert mlp_w1.shape[1:] == mlp_w2.shape[1:]
    fam_a = _place_shard("place_w1", mlp_w1, None, 0, 2 * DEPTH)
    fam_a = _place_shard("place_w2", mlp_w2, fam_a, DEPTH, 2 * DEPTH)
    fam_b = _place_shard("place_wglu", s5_w_glu, None, 0, ns5 + nconv)
    fam_b = _place_shard("place_wpw1", cv_w_pw1, fam_b, ns5, ns5 + nconv)
    fam_c = _place_shard("place_wpw2", cv_w_pw2, None, 0, nconv)
    wa_full, wb_full, wc_full = _gather_weights([fam_a, fam_b, fam_c])

    pos = jnp.broadcast_to(_pos_embed(cfg.L // GRID_W, D)[None], (B, cfg.L, D))
    tok_in = jnp.concatenate([_to_perm(ctx), _to_perm(x)], axis=0)
    pos_in = jnp.concatenate([jnp.zeros((Tc, D), F32), _to_perm(pos)], axis=0)
    tok0 = _ew("add_pos", lambda a, b: a + b, [tok_in, pos_in], [_sds((T, D))])[0]
    tgt = _to_perm(loss_target)

    s5p = []
    for j in range(ns5):
        lay = _s5_layouts(cfg, s5_lam_re[j], s5_lam_im[j], s5_log_dt[j], s5_b_re[j], s5_b_im[j])
        abr, abi, bbr, bbi = _disc_fwd(*lay)
        s5p.append(dict(lay=lay, acoef=_coef_rows(cfg, abr, abi, False), acoef_adj=_coef_rows(cfg, abr, abi, True),
                        bf=_blockdiag_b(cfg, bbr, bbi), cf=_blockdiag_c(cfg, s5_c_re[j], s5_c_im[j])))

    kinds = ["s5" if i % 2 == 0 else "conv" for i in range(DEPTH)]
    tok = tok0
    saved = []
    s5_j = cv_j = 0
    for i in range(DEPTH):
        later_s5 = any(k == "s5" for k in kinds[i + 1:])
        rows = cfg.rows(later_s5)
        m8 = mod8[i]
        sv = dict(tok=tok, rows=rows, kind=kinds[i])
        g0, b0 = ln_gain_f[i, 0][None], ln_bias_f[i, 0][None]
        g1, b1 = ln_gain_f[i, 1][None], ln_bias_f[i, 1][None]
        if kinds[i] == "s5":
            j = s5_j
            s5_j += 1
            p = s5p[j]
            y, ck = _s5_forward(cfg, f"l{i}", tok, m8, SH1, SC1, p["bf"], p["cf"], p["acoef"])
            wg = (wb_full, j)
            x1, r1, mix, zz, zb, yy = _glu_ln(cfg, f"l{i}_glu", rows, tok, y, m8, (SH1, SC1, G1), s5_d[j][None], wg,
                                              s5_b_glu[j][None], g0, b0)
            sv.update(j=j, ck=ck, zz=zz, zb=zb, yy=yy, wg=wg)
        else:
            j = cv_j
            cv_j += 1
            w1c, w2c = (wb_full, ns5 + j), (wc_full, j)
            aa, ag, hb = _pw1_glu(cfg, f"l{i}_pw1", rows, tok, m8, (SH1, SC1), w1c, b_pw1_f[j][None])
            cvv, sb = _dwconv_ln(cfg, f"l{i}_dw", rows, ag, w_dw_f[j], b_dw_f[j][None], cvg_f[j][None], cvb_f[j][None])
            x1, r1, mix = _pw2_ln(cfg, f"l{i}_pw2", rows, sb, tok, m8, G1, w2c, b_pw2_f[j][None], g0, b0)
            sv.update(j=j, aa=aa, ag=ag, hb=hb, cvv=cvv, sb=sb, w1c=w1c, w2c=w2c)
        w1m, w2m = (wa_full, i), (wa_full, DEPTH + i)
        x2, r2, mout, am, hm = _mlp_ln(cfg, f"l{i}_mlp", rows, x1, m8, (SH2, SC2, G2), w1m, w2m, g1, b1)
        sv.update(r1=r1, mix=mix, x1=x1, r2=r2, mout=mout, am=am, hm=hm, w1m=w1m, w2m=w2m, g0=g0, b0=b0, g1=g1, b1=b1)
        saved.append(sv)
        tok = x2

    loss8, dxf = _loss(cfg, tok, tgt)
    loss = lax.psum(loss8[0, 0], ("x", "y", "c"))

    gfam_a = (4, 2 * DEPTH, D, cfg.F // 4)
    gfam_b = (4, ns5 + nconv, D, D // 2)
    gfam_c = (4, nconv, D // 4, D)
    dmod8 = [None] * DEPTH
    g_ln_gain = [[None, None] for _ in range(DEPTH)]
    g_ln_bias = [[None, None] for _ in range(DEPTH)]
    g_s5 = [None] * ns5
    g_cv = [None] * nconv
    dres, dh = dxf, None
    for i in reversed(range(DEPTH)):
        sv = saved[i]
        rows = sv["rows"]
        m8 = mod8[i]
        nxt_m8 = mod8[i + 1] if i + 1 < DEPTH else None
        ctx_ok = True if i + 1 >= DEPTH else (saved[i + 1]["rows"][0] == 0)
        if rows[0] != 0:
            ctx_ok = True
        dprev, dbr, dgn, dbs, dg2, dsc_n, dsh_n = _lnb(
            cfg, f"l{i}_lnb2", rows, dres, ctx_ok, dh, sv["r2"], sv["mout"], jnp.concatenate([sv["g1"], sv["b1"]], 0),
            m8, G2, nxt_m8, SC1)
        if rows[0] != 0:
            dg2, dsc_n, dsh_n = (_x_only(t) for t in (dg2, dsc_n, dsh_n))
        g_ln_gain[i][1], g_ln_bias[i][1] = dgn[0], dbs[0]
        if i + 1 < DEPTH:
            dmod8[i + 1]["sc1"], dmod8[i + 1]["sh1"] = dsc_n, dsh_n
        dmod8[i] = dict(g2=dg2)
        dh2, dhid, dout = _mlp_bwd(cfg, f"l{i}_mlpb", rows, dbr, sv["am"], sv["w1m"], sv["w2m"])
        gfam_a = _wgrad(cfg, f"l{i}_gw1", rows, sv["hm"], dhid, "col", gfam_a, i)
        gfam_a = _wgrad(cfg, f"l{i}_gw2", rows, sv["am"], dout, "row", gfam_a, DEPTH + i)
        dprev1, dbr1, dgn, dbs, dg1, dsc2, dsh2 = _lnb(
            cfg, f"l{i}_lnb1", rows, dprev, True, dh2, sv["r1"], sv["mix"], jnp.concatenate([sv["g0"], sv["b0"]], 0),
            m8, G1, m8, SC2)
        if rows[0] != 0:
            dg1, dsc2, dsh2 = (_x_only(t) for t in (dg1, dsc2, dsh2))
        g_ln_gain[i][0], g_ln_bias[i][0] = dgn[0], dbs[0]
        dmod8[i].update(g1=dg1, sc2=dsc2, sh2=dsh2)
        j = sv["j"]
        if sv["kind"] == "s5":
            p = s5p[j]
            dyy, dzz, dbglu = _glu_bwd(cfg, f"l{i}_glub", rows, dbr1, sv["zz"], sv["wg"], sv["yy"])
            gfam_b = _wgrad(cfg, f"l{i}_gwg", rows, sv["zb"], dzz, "col", gfam_b, j)
            du, (da, dbf, dcf) = _s5_backward(cfg, f"l{i}", dyy, rows[0] == 0, sv["tok"], m8, SH1, SC1, p["bf"], p["cf"],
                                              p["acoef"], p["acoef_adj"], sv["ck"])
            dh, dds = _s5_du(cfg, f"l{i}_du", cfg.rows(True), du, dyy, rows[0] // cfg.TM, sv["tok"], m8, (SH1, SC1),
                             s5_d[j][None])
            g_s5[j] = dict(da=da, dbf=dbf, dcf=dcf, dd=dds[0], dbglu=dbglu[0])
            dres = dprev1
        else:
            dcv, dmb, sums = _pw2_bwd(cfg, f"l{i}_pw2b", rows, dbr1, sv["cvv"], sv["w2c"], cvg_f[j][None], cvb_f[j][None])
            gfam_c = _wgrad(cfg, f"l{i}_gwp2", rows, sv["sb"], dmb, "row", gfam_c, j)
            dag, dwdw = _dwconv_bwd(cfg, f"l{i}_dwb", rows, dcv, sv["ag"], w_dw_f[j])
            dh, daa, dbpw1 = _glu_bwd(cfg, f"l{i}_pw1b", rows, dag, sv["aa"], sv["w1c"])
            gfam_b = _wgrad(cfg, f"l{i}_gwp1", rows, sv["hb"], daa, "col", gfam_b, ns5 + j)
            g_cv[j] = dict(ln_g=sums[0], ln_b=sums[1], b_dw=sums[2], b_pw2=sums[3], w_dw=dwdw, b_pw1=dbpw1[0])
            dres = dprev1
    gx_perm, dsc0, dsh0 = _input_bwd(cfg, dres, dh, tok0, mod8[0], SC1)
    dmod8[0]["sc1"], dmod8[0]["sh1"] = dsc0, dsh0
    grad_x = _from_perm(gx_perm[Tc:], B, cfg.L)

    zero28 = jnp.zeros((2, 8, D), F32)
    dm8 = jnp.stack([jnp.concatenate([dmod8[i].get(k, zero28) for k in ("sh1", "sc1", "g1", "sh2", "sc2", "g2")], axis=-1)
                     for i in range(DEPTH)])
    dm_rows = _dmod_rows(dm8)
    dm_tab = jnp.zeros((DEPTH, 24, 6 * D), F32)
    dm_tab = lax.dynamic_update_slice_in_dim(dm_tab, dm_rows[:, 0:B], B * dev, axis=1)
    dm_tab = lax.dynamic_update_slice_in_dim(dm_tab, dm_rows[:, 2:3], 16, axis=1)

    eye_parts = []
    for j in range(ns5):
        g = g_s5[j]
        dbbr, dbbi = _diag_b(cfg, g["dbf"])
        dcr, dci = _diag_c(cfg, g["dcf"])
        eye_parts += [g["da"], dbbr, dbbi, dcr, dci, g["dd"], g["dbglu"]]
    for j in range(nconv):
        g = g_cv[j]
        eye_parts += [g["ln_g"], g["ln_b"], g["b_dw"], g["b_pw2"], g["w_dw"], g["b_pw1"]]
    eye_parts += [jnp.stack([jnp.stack(r) for r in g_ln_gain]), jnp.stack([jnp.stack(r) for r in g_ln_bias]), dm_tab]
    buf, meta = _pack(eye_parts)
    red = _unpack(_allreduce8("small", buf), meta)

    grads = {}
    k = 0
    lam_re_g, lam_im_g, ldt_g, bre_g, bim_g, cre_g, cim_g, dd_g, bglu_g = ([] for _ in range(9))
    for j in range(ns5):
        da, dbbr, dbbi, dcr, dci, dd, dbglu = red[k:k + 7]
        k += 7
        da_s = _sublane_sum(f"s5_dasum_{j}", da.reshape(4, 8, cfg.NS)).reshape(2, 2, cfg.NS)
        g_abr = da_s[:, 0].reshape(2, cfg.G, cfg.P).transpose(2, 0, 1).reshape(cfg.P, 2 * cfg.G)
        g_abi = da_s[:, 1].reshape(2, cfg.G, cfg.P).transpose(2, 0, 1).reshape(cfg.P, 2 * cfg.G)
        glr, gli, gldt, gbr, gbi = _disc_bwd(*s5p[j]["lay"], g_abr, g_abi, dbbr, dbbi)
        lam_re_g.append(glr.reshape(cfg.P, 2, cfg.G).transpose(1, 2, 0))
        lam_im_g.append(gli.reshape(cfg.P, 2, cfg.G).transpose(1, 2, 0))
        ldt_g.append(gldt.reshape(2, cfg.G))
        bre_g.append(gbr.reshape(S5_GROUP, cfg.P, 2, cfg.G).transpose(2, 3, 1, 0))
        bim_g.append(gbi.reshape(S5_GROUP, cfg.P, 2, cfg.G).transpose(2, 3, 1, 0))
        cre_g.append(dcr)
        cim_g.append(dci)
        dd_g.append(dd)
        bglu_g.append(dbglu)
    grads.update(s5_lam_re=jnp.stack(lam_re_g), s5_lam_im=jnp.stack(lam_im_g), s5_log_dt=jnp.stack(ldt_g),
                 s5_b_re=jnp.stack(bre_g), s5_b_im=jnp.stack(bim_g), s5_c_re=jnp.stack(cre_g), s5_c_im=jnp.stack(cim_g),
                 s5_d=jnp.stack(dd_g), s5_b_glu=jnp.stack(bglu_g))

    def my_cols(full, width):
        return lax.dynamic_slice_in_dim(full, shard * width, width, axis=full.ndim - 1)

    cvs = {n: [] for n in ("ln_g", "ln_b", "b_dw", "b_pw2", "w_dw", "b_pw1")}
    for j in range(nconv):
        for n, val in zip(("ln_g", "ln_b", "b_dw", "b_pw2", "w_dw", "b_pw1"), red[k:k + 6]):
            cvs[n].append(val)
        k += 6
    grads.update(cv_ln_g=my_cols(jnp.stack(cvs["ln_g"]), Ds), cv_ln_b=my_cols(jnp.stack(cvs["ln_b"]), Ds),
                 cv_b_dw=my_cols(jnp.stack(cvs["b_dw"]), Ds), cv_b_pw2=my_cols(jnp.stack(cvs["b_pw2"]), Ds),
                 cv_w_dw=my_cols(jnp.stack(cvs["w_dw"]), Ds), cv_b_pw1=my_cols(jnp.stack(cvs["b_pw1"]), 2 * D // 4))
    grads.update(ln_gain=my_cols(red[k], Ds), ln_bias=my_cols(red[k + 1], Ds))
    dm_all = red[k + 2]

    dm_sh = lax.dynamic_slice_in_dim(dm_all, shard * Wa, Wa, axis=2)
    gw_ada, dcond = _ada_bwd(c_all, dm_sh, w_ada)
    grads["w_ada"] = gw_ada
    grads["b_ada"] = _colsum_groups("ada_bsum", dm_all)
    dc_part = dcond[0:1]
    dc_buf = jnp.concatenate([jnp.where(ac == 0, dc_part, 0.0), jnp.zeros((7, D), F32)], axis=0)
    dc_tot = _allreduce8("cctx", dc_buf.reshape(8 * D // LANES, LANES)).reshape(8, D)[0:1]
    grads["c_ctx"] = _ew("cctx_grad", lambda g, cv: g * (_sigmoid(cv) * (1.0 + cv * (1.0 - _sigmoid(cv)))),
                         [jnp.broadcast_to(dc_tot, (8, D)), jnp.broadcast_to(c_ctx[None], (8, D))], [_sds((8, D))])[0][0]

    ra, rb, rc = _reduce_scatter("gw", [gfam_a, gfam_b, gfam_c])
    grads.update(mlp_w1=ra[:DEPTH], mlp_w2=ra[DEPTH:], s5_w_glu=rb[:ns5], cv_w_pw1=rb[ns5:], cv_w_pw2=rc)

    weights = dict(c_ctx=c_ctx, w_ada=w_ada, b_ada=b_ada, ln_gain=ln_gain, ln_bias=ln_bias, s5_lam_re=s5_lam_re,
                   s5_lam_im=s5_lam_im, s5_log_dt=s5_log_dt, s5_b_re=s5_b_re, s5_b_im=s5_b_im, s5_c_re=s5_c_re,
                   s5_c_im=s5_c_im, s5_d=s5_d, s5_w_glu=s5_w_glu, s5_b_glu=s5_b_glu, cv_w_pw1=cv_w_pw1, cv_b_pw1=cv_b_pw1,
                   cv_w_dw=cv_w_dw, cv_b_dw=cv_b_dw, cv_ln_g=cv_ln_g, cv_ln_b=cv_ln_b, cv_w_pw2=cv_w_pw2, cv_b_pw2=cv_b_pw2,
                   mlp_w1=mlp_w1, mlp_w2=mlp_w2)
    ms = dict(c_ctx=m_c_ctx, w_ada=m_w_ada, b_ada=m_b_ada, ln_gain=m_ln_gain, ln_bias=m_ln_bias, s5_lam_re=m_s5_lam_re,
              s5_lam_im=m_s5_lam_im, s5_log_dt=m_s5_log_dt, s5_b_re=m_s5_b_re, s5_b_im=m_s5_b_im, s5_c_re=m_s5_c_re,
              s5_c_im=m_s5_c_im, s5_d=m_s5_d, s5_w_glu=m_s5_w_glu, s5_b_glu=m_s5_b_glu, cv_w_pw1=m_cv_w_pw1,
              cv_b_pw1=m_cv_b_pw1, cv_w_dw=m_cv_w_dw, cv_b_dw=m_cv_b_dw, cv_ln_g=m_cv_ln_g, cv_ln_b=m_cv_ln_b,
              cv_w_pw2=m_cv_w_pw2, cv_b_pw2=m_cv_b_pw2, mlp_w1=m_mlp_w1, mlp_w2=m_mlp_w2)
    vs = dict(c_ctx=v_c_ctx, w_ada=v_w_ada, b_ada=v_b_ada, ln_gain=v_ln_gain, ln_bias=v_ln_bias, s5_lam_re=v_s5_lam_re,
              s5_lam_im=v_s5_lam_im, s5_log_dt=v_s5_log_dt, s5_b_re=v_s5_b_re, s5_b_im=v_s5_b_im, s5_c_re=v_s5_c_re,
              s5_c_im=v_s5_c_im, s5_d=v_s5_d, s5_w_glu=v_s5_w_glu, s5_b_glu=v_s5_b_glu, cv_w_pw1=v_cv_w_pw1,
              cv_b_pw1=v_cv_b_pw1, cv_w_dw=v_cv_w_dw, cv_b_dw=v_cv_b_dw, cv_ln_g=v_cv_ln_g, cv_ln_b=v_cv_ln_b,
              cv_w_pw2=v_cv_w_pw2, cv_b_pw2=v_cv_b_pw2, mlp_w1=v_mlp_w1, mlp_w2=v_mlp_w2)
    names = list(weights)
    deltas, new_m, new_v = {}, {}, {}
    for n in names:
        g = grads[n].reshape(weights[n].shape)
        grads[n] = g
        deltas[n], new_m[n], new_v[n] = _adamw("adamw_" + n, weights[n], g, ms[n], vs[n])
    return (loss, grad_x, *[grads[n] for n in names], *[deltas[n] for n in names], *[new_m[n] for n in names],
            *[new_v[n] for n in names])


def _sublane_sum(name, a):
    n, _, w = a.shape

    def body(a_ref, o_ref):
        for q in range(n):
            o_ref[q:q + 1, :] = jnp.sum(a_ref[q], axis=0, keepdims=True)

    return pl.pallas_call(body, name=name, out_shape=_sds((n, w)))(a)


def _colsum_groups(name, dm_all):
    nl, nr, w = dm_all.shape

    def body(d_ref, o_ref):
        o_ref[...] = jnp.zeros((8, w), F32) + jnp.sum(d_ref[...], axis=0, keepdims=True)

    out = pl.pallas_call(body, name=name, grid=(nl,), in_specs=[pl.BlockSpec((None, nr, w), lambda l: (l, 0, 0))],
                         out_specs=pl.BlockSpec((None, 8, w), lambda l: (l, 0, 0)), out_shape=_sds((nl, 8, w)),
                         compiler_params=_cp(1))(dm_all)
    return out[:, 0]
```

```python
import functools
import math

import jax
import jax.numpy as jnp
from jax import lax
from jax.experimental import pallas as pl
from jax.experimental.pallas import tpu as pltpu

F32 = jnp.float32
BF16 = jnp.bfloat16
MESH = pl.DeviceIdType.MESH
ANY = pl.BlockSpec(memory_space=pl.ANY)

DEPTH = 4
S5_GROUP = 16
S5_STATE = 64
GRID_W = 64
POS_TEMP = 10000.0
LAMBDA_RE_MAX = -1e-4
LN_EPS = 1e-5
DN_ALPHA = (2.0 * DEPTH) ** 0.25
ADAM_LR, ADAM_B1, ADAM_B2, ADAM_EPS, ADAM_WD, ADAM_STEP = 0.001, 0.9, 0.999, 1e-08, 0.01, 10

SUBLANES = 8
LANES = 128
OCT_CH = 128
OCT_ST = 512
CONV_ROWS = 64
VMEM_LIMIT = 56 * 1024 * 1024


def _cp(n_axes):
    return pltpu.CompilerParams(dimension_semantics=("arbitrary",) * n_axes, vmem_limit_bytes=VMEM_LIMIT)


def _full(shape, single=False):
    nd = len(shape)
    if single:
        return pl.BlockSpec(shape, lambda *i: (0,) * nd, pipeline_mode=pl.Buffered(1))
    return pl.BlockSpec(shape, lambda *i: (0,) * nd)


def _sds(shape, dtype=F32):
    return jax.ShapeDtypeStruct(tuple(shape), dtype)


def _mod(x, sh8, sc8):
    r, d = x.shape
    return (x.reshape(r // 8, 8, d) * (1.0 + sc8[None]) + sh8[None]).reshape(r, d)


def _rowscale(x, g8):
    r, d = x.shape
    return (x.reshape(r // 8, 8, d) * g8[None]).reshape(r, d)


def _sum8(x):
    r, w = x.shape
    return jnp.sum(x.reshape(r // 8, 8, w), axis=0)


def _ln_stats(r):
    mu = jnp.mean(r, axis=-1, keepdims=True)
    xc = r - mu
    var = jnp.mean(xc * xc, axis=-1, keepdims=True)
    rstd = lax.rsqrt(var + LN_EPS)
    return xc * rstd, rstd


def _ln_bwd(dxh, xhat, rstd):
    m1 = jnp.mean(dxh, axis=-1, keepdims=True)
    m2 = jnp.mean(dxh * xhat, axis=-1, keepdims=True)
    return rstd * (dxh - m1 - xhat * m2)


def _sigmoid(x):
    return 1.0 / (1.0 + jnp.exp(-x))


def _gelu(y):
    return 0.5 * y * (1.0 + lax.erf(y * (1.0 / math.sqrt(2.0))))


def _gelu_grad(y):
    return 0.5 * (1.0 + lax.erf(y * (1.0 / math.sqrt(2.0)))) + y * jnp.exp(-0.5 * y * y) * (1.0 / math.sqrt(2.0 * math.pi))


def _dot(a, b):
    return jnp.dot(a, b, preferred_element_type=F32)


def _dot_nt(a, b):
    return lax.dot_general(a, b, (((1,), (1,)), ((), ())), preferred_element_type=F32)


def _dot_tn(a, b):
    return lax.dot_general(a, b, (((0,), (0,)), ((), ())), preferred_element_type=F32)


class _Cfg:
    def __init__(self, x, ctx, mlp_w1, cv_w_dw):
        self.B, self.L, self.D = x.shape
        self.Lc = ctx.shape[1]
        assert self.B * 4 == SUBLANES, "two examples per device, four chunks each"
        self.F = mlp_w1.shape[2] * 4
        self.KW = cv_w_dw.shape[1]
        self.half = self.KW // 2
        self.G = self.D // S5_GROUP
        self.P = S5_STATE
        self.NS = self.G * self.P
        self.NO = self.D // OCT_CH
        assert self.NO % 2 == 0
        self.nx = self.L // 4
        self.nc = self.Lc // 4
        self.Tc = self.B * self.Lc
        self.Tx = self.B * self.L
        self.T = self.Tc + self.Tx
        self.TM = 256 if self.Tc % 256 == 0 else self.Tc
        assert self.Tx % self.TM == 0 and self.TM % 16 == 0
        self.HB = self.TM // 2
        assert SUBLANES * self.half <= self.HB
        self.TW = 512 if (self.Tc % 512 == 0 and self.Tx % 512 == 0) else self.TM

    def ti(self, n):
        t = 32 if self.nc % 32 == 0 else self.nc
        assert n % t == 0 and self.Tc % (8 * t) == 0
        return t

    def rows(self, ctx_too):
        return (0, self.T) if ctx_too else (self.Tc, self.Tx)


def _allgather8(name, x_shard):
    m_per, n = x_shard.shape
    assert m_per % 8 == 0

    def body(x_ref, out_ref, send_sems, recv_sems, local_sem):
        x, y, c = lax.axis_index("x"), lax.axis_index("y"), lax.axis_index("c")
        me, sibling = (x, y, c), (x, y, 1 - c)
        chips = [(1 - x, y), (x, 1 - y), (1 - x, 1 - y)]

        def rows(px, py, pc):
            return out_ref.at[pl.ds((4 * px + 2 * py + pc) * m_per, m_per), :]

        def copy(k, block, to, src=None):
            return pltpu.make_async_remote_copy(
                src_ref=rows(*block) if src is None else src, dst_ref=rows(*block),
                send_sem=send_sems.at[k], recv_sem=recv_sems.at[k], device_id=to, device_id_type=MESH)

        mine = pltpu.make_async_copy(x_ref, rows(*me), local_sem)
        mine.start()
        first = [copy(0, me, sibling, src=x_ref)]
        first += [copy(1 + j, me, (*chip, c), src=x_ref) for j, chip in enumerate(chips)]
        for cp in first:
            cp.start()
        passed = [copy(4 + j, (*chip, c), sibling) for j, chip in enumerate(chips)]
        for j, chip in enumerate(chips):
            copy(1 + j, (*chip, c), me).wait_recv()
            passed[j].start()
        copy(0, sibling, me).wait_recv()
        for j, chip in enumerate(chips):
            copy(4 + j, (*chip, 1 - c), me).wait_recv()
        for cp in first + passed:
            cp.wait_send()
        mine.wait()

    return pl.pallas_call(
        body, name=name, out_shape=_sds((8 * m_per, n), x_shard.dtype),
        in_specs=[pl.BlockSpec(memory_space=pltpu.VMEM)], out_specs=pl.BlockSpec(memory_space=pltpu.VMEM),
        scratch_shapes=[pltpu.SemaphoreType.DMA((7,)), pltpu.SemaphoreType.DMA((7,)), pltpu.SemaphoreType.DMA],
        compiler_params=pltpu.CompilerParams(vmem_limit_bytes=VMEM_LIMIT),
    )(x_shard)


def _flip(v, m):
    return v + m - 2 * v * m


def _peer(axis):
    x, y, c = lax.axis_index("x"), lax.axis_index("y"), lax.axis_index("c")
    if axis == "c":
        return (x, y, 1 - c)
    if axis == "xy":
        return (_flip(x, 1 - c), _flip(y, c), c)
    if axis == "yx":
        return (_flip(x, c), _flip(y, 1 - c), c)
    raise ValueError(axis)


def _pair_exchange(name, axis, inputs, out_shapes, aliases, plan):
    n_in = len(inputs)
    n_out = len(out_shapes)

    def body(*refs):
        ins, outs = refs[:n_in], refs[n_in:n_in + n_out]
        send_sems, recv_sems, local_sems = refs[n_in + n_out:]
        x, y, c = lax.axis_index("x"), lax.axis_index("y"), lax.axis_index("c")
        remote, local = plan(x, y, c, ins, outs)
        lcs = [pltpu.make_async_copy(s, d, local_sems.at[k]) for k, (s, d) in enumerate(local)]
        for cp in lcs:
            cp.start()
        rcs = [pltpu.make_async_remote_copy(src_ref=s, dst_ref=d, send_sem=send_sems.at[k], recv_sem=recv_sems.at[k],
                                            device_id=_peer(axis), device_id_type=MESH) for k, (s, d) in enumerate(remote)]
        for cp in rcs:
            cp.start()
        for cp in rcs:
            cp.wait()
        for cp in lcs:
            cp.wait()

    n_remote, n_local = plan.counts
    return pl.pallas_call(
        body, name=name, out_shape=tuple(out_shapes),
        in_specs=[ANY] * n_in, out_specs=tuple([ANY] * n_out),
        input_output_aliases=dict(aliases),
        scratch_shapes=[pltpu.SemaphoreType.DMA((n_remote,)), pltpu.SemaphoreType.DMA((n_remote,)),
                        pltpu.SemaphoreType.DMA((max(n_local, 1),))],
    )(*inputs)


def _plan(n_remote, n_local=0):
    def deco(fn):
        fn.counts = (n_remote, n_local)
        return fn
    return deco


def _xyc():
    return jnp.stack([lax.axis_index("x"), lax.axis_index("y"), lax.axis_index("c")]).astype(jnp.int32)


def _place_shard(name, w, fam, slot0, n_slots):
    n, kk, nn = w.shape
    kt = 256 if kk % 256 == 0 else kk

    def body(scal, w_ref, *rest):
        rest[-1][...] = w_ref[...].astype(BF16)

    in_specs = [pl.BlockSpec((None, kt, nn), lambda t, i, sc: (t, i, 0))]
    args = [_xyc(), w]
    aliases = {}
    if fam is not None:
        in_specs.append(ANY)
        args.append(fam)
        aliases = {2: 0}
    gs = pltpu.PrefetchScalarGridSpec(
        num_scalar_prefetch=1, grid=(n, kk // kt), in_specs=in_specs,
        out_specs=pl.BlockSpec((None, None, kt, nn), lambda t, i, sc: (2 * sc[0] + sc[1], slot0 + t, i, 0)))
    return pl.pallas_call(body, name=name, grid_spec=gs, out_shape=_sds((4, n_slots, kk, nn), BF16),
                          input_output_aliases=aliases, compiler_params=_cp(2))(*args)


def _gather_weights(fams):
    nf = len(fams)
    shapes = [f.shape for f in fams]
    views = [f.reshape(4, 2, -1, f.shape[-1]) for f in fams]
    outs = [_sds(v.shape, v.dtype) for v in views]
    alias = {k: k for k in range(nf)}

    @_plan(nf)
    def plan1(x, y, c, ins, outs_):
        s = 2 * x + y
        return ([(ins[k].at[s, c], outs_[k].at[s, c]) for k in range(nf)], [])

    views = _pair_exchange("gatherw_1", "xy", list(views), outs, alias, plan1)

    @_plan(2 * nf)
    def plan2(x, y, c, ins, outs_):
        shards = [2 * x + y, 2 * _flip(x, 1 - c) + _flip(y, c)]
        return ([(ins[k].at[s, c], outs_[k].at[s, c]) for k in range(nf) for s in shards], [])

    views = _pair_exchange("gatherw_2", "yx", list(views), outs, alias, plan2)

    @_plan(3 * nf)
    def plan3(x, y, c, ins, outs_):
        shards = [2 * (1 - x) + y, 2 * x + (1 - y), 2 * (1 - x) + (1 - y)]
        return ([(ins[k].at[s, c], outs_[k].at[s, c]) for k in range(nf) for s in shards], [])

    views = _pair_exchange("gatherw_c", "c", list(views), outs, alias, plan3)
    return [v.reshape(sh) for v, sh in zip(views, shapes)]


def _sel_add(name, a, a_sel, r, emit_bf16, out_slots=None):
    nr, rows, w = r.shape
    tr = 256 if rows % 256 == 0 else rows

    def body(scal, a_ref, r_ref, *outs):
        s = a_ref[...] + r_ref[...].astype(F32)
        outs[0][...] = s
        if emit_bf16:
            outs[1][...] = s.astype(BF16)

    lead = a.ndim - 2
    a_block = (None,) * lead + (tr, w)
    n_out, o_fn = out_slots if out_slots is not None else (nr, lambda j, sc: j)
    out_shape = [_sds((n_out, rows, w), F32)] + ([_sds((nr, rows, w), BF16)] if emit_bf16 else [])
    out_specs = [pl.BlockSpec((None, tr, w), lambda j, t, sc: (o_fn(j, sc), t, 0))]
    if emit_bf16:
        out_specs.append(pl.BlockSpec((None, tr, w), lambda j, t, sc: (j, t, 0)))
    gs = pltpu.PrefetchScalarGridSpec(
        num_scalar_prefetch=1, grid=(nr, rows // tr),
        in_specs=[pl.BlockSpec(a_block, lambda j, t, sc: tuple(a_sel(j, sc)) + (t, 0)),
                  pl.BlockSpec((None, tr, w), lambda j, t, sc: (j, t, 0))],
        out_specs=out_specs)
    return pl.pallas_call(body, name=name, grid_spec=gs, out_shape=out_shape, compiler_params=_cp(2))(_xyc(), a, r)


def _reduce_scatter(tag, grads):
    ng = len(grads)
    flat = [g.reshape(4, 2, -1, g.shape[-1]) for g in grads]

    @_plan(ng)
    def plan1(x, y, c, ins, outs_):
        return ([(ins[k].at[:, 1 - c], outs_[k]) for k in range(ng)], [])

    r1 = _pair_exchange(tag + "_rs_c", "c", flat, [_sds((4,) + f.shape[2:], F32) for f in flat], {}, plan1)
    p1 = [_sel_add(f"{tag}_add1_{k}", flat[k], lambda j, sc: (j, sc[2]), r1[k], True) for k in range(ng)]

    def sent1(kk, x, y, c):
        return ((1 - c) * kk + c * (1 - x), (1 - c) * (1 - y) + c * kk)

    def kept1(j, sc):
        x, y, c = sc[0], sc[1], sc[2]
        return ((1 - c) * j + c * x, (1 - c) * y + c * j)

    @_plan(2 * ng)
    def plan2(x, y, c, ins, outs_):
        return ([(ins[k].at[sent1(kk, x, y, c)], outs_[k].at[kk]) for k in range(ng) for kk in range(2)], [])

    v1 = [pb.reshape(2, 2, pb.shape[1], pb.shape[2]) for p, pb in p1]
    r2 = _pair_exchange(tag + "_rs_1", "yx", v1, [_sds((2,) + v.shape[2:], BF16) for v in v1], {}, plan2)
    p2 = [_sel_add(f"{tag}_add2_{k}", p1[k][0].reshape(2, 2, p1[k][0].shape[1], p1[k][0].shape[2]), kept1, r2[k], True)
          for k in range(ng)]

    @_plan(ng)
    def plan3(x, y, c, ins, outs_):
        return ([(ins[k].at[(1 - c) * (1 - x) + c * (1 - y)], outs_[k]) for k in range(ng)], [])

    r3 = _pair_exchange(tag + "_rs_2", "xy", [qb for q, qb in p2], [_sds(qb.shape[1:], BF16) for q, qb in p2], {}, plan3)
    fin = [_sel_add(f"{tag}_add3_{k}", p2[k][0], lambda j, sc: ((1 - sc[2]) * sc[0] + sc[2] * sc[1],), r3[k][None], False,
                    out_slots=(2, lambda j, sc: sc[2]))[0] for k in range(ng)]

    @_plan(ng)
    def plan4(x, y, c, ins, outs_):
        return ([(ins[k].at[c], outs_[k].at[c]) for k in range(ng)], [])

    full = _pair_exchange(tag + "_rs_c2", "c", fin, [_sds(f.shape, F32) for f in fin], {k: k for k in range(ng)}, plan4)
    return [full[k].reshape(grads[k].shape[1:]) for k in range(ng)]


def _allreduce8(tag, buf):
    rows, w = buf.shape
    assert rows % 16 == 0
    one = lambda: _plan(1)(lambda x, y, c, ins, outs_: ([(ins[0], outs_[0])], []))
    (got,) = _pair_exchange(f"{tag}_ar_c", "c", [buf], [_sds(buf.shape, F32)], {}, one())
    cur = _ew(f"{tag}_aradd_c", lambda a, b: a + b, [buf, got], [_sds(buf.shape, F32)])[0].reshape(2, rows // 2, w)
    mine = _plan(1)(lambda x, y, c, ins, outs_: ([(ins[0].at[c], outs_[0])], []))
    (got,) = _pair_exchange(f"{tag}_ar_1", "xy", [cur], [_sds(cur.shape[1:], F32)], {}, mine)
    (h1,) = _sel_add(f"{tag}_aradd_1", cur, lambda j, sc: (sc[2],), got[None], False)
    (got,) = _pair_exchange(f"{tag}_ar_2", "yx", [h1[0]], [_sds(h1.shape[1:], F32)], {}, one())
    (h2,) = _sel_add(f"{tag}_aradd_2", h1, lambda j, sc: (0,), got[None], False, out_slots=(2, lambda j, sc: sc[2]))
    swap = _plan(1)(lambda x, y, c, ins, outs_: ([(ins[0].at[c], outs_[0].at[c])], []))
    (full,) = _pair_exchange(f"{tag}_ar_c2", "c", [h2], [_sds(h2.shape, F32)], {0: 0}, swap)
    return full.reshape(rows, w)


def _ew(name, fn, ins, outs):
    rows, w = ins[0].shape
    tr = rows
    for cand in (512, 256, 128, 64, 32, 16, 8):
        if rows % cand == 0 and rows > cand and cand * w * 4 <= (1 << 20):
            tr = cand
            break
    n_in = len(ins)

    def body(*refs):
        vals = fn(*[r[...] for r in refs[:n_in]])
        if not isinstance(vals, (tuple, list)):
            vals = (vals,)
        for o, v in zip(refs[n_in:], vals):
            o[...] = v.astype(o.dtype)

    spec = pl.BlockSpec((tr, w), lambda i: (i, 0))
    return pl.pallas_call(body, name=name, grid=(rows // tr,), in_specs=[spec] * n_in,
                          out_specs=[spec] * len(outs), out_shape=list(outs), compiler_params=_cp(1))(*ins)


def _ew3(name, fn, ins, n_out):
    aa, bb, cc = ins[0].shape
    pad_bytes = (-(-bb // SUBLANES) * SUBLANES) * (-(-cc // LANES) * LANES) * 4
    ta = 1
    for cand in range(aa, 0, -1):
        if aa % cand == 0 and cand * pad_bytes <= (1 << 20):
            ta = cand
            break
    n_in = len(ins)

    def body(*refs):
        vals = fn(*[r[...] for r in refs[:n_in]])
        for o, v in zip(refs[n_in:], vals):
            o[...] = v

    spec = pl.BlockSpec((ta, bb, cc), lambda i: (i, 0, 0))
    return pl.pallas_call(body, name=name, grid=(aa // ta,), in_specs=[spec] * n_in, out_specs=[spec] * n_out,
                          out_shape=[_sds((aa, bb, cc))] * n_out, compiler_params=_cp(1))(*ins)


def _view_for_ew(a):
    if a.ndim == 1:
        return a.reshape(1, -1)
    if a.ndim == 2:
        return a
    if a.shape[-1] % LANES == 0 and a.shape[-2] % SUBLANES == 0:
        return a.reshape(-1, a.shape[-1])
    return a.reshape(-1, a.shape[-2], a.shape[-1])


def _adamw(name, w, g, m, v):
    def fn(w, g, m, v):
        m = ADAM_B1 * m + (1.0 - ADAM_B1) * g
        v = ADAM_B2 * v + (1.0 - ADAM_B2) * (g * g)
        m_hat = m / (1.0 - ADAM_B1 ** ADAM_STEP)
        v_hat = v / (1.0 - ADAM_B2 ** ADAM_STEP)
        delta = -ADAM_LR * (m_hat / (jnp.sqrt(v_hat) + ADAM_EPS) + ADAM_WD * w)
        return delta, m, v

    shp = w.shape
    a = [_view_for_ew(t) for t in (w, g, m, v)]
    if a[0].ndim == 3:
        o = _ew3(name, fn, a, 3)
    else:
        o = _ew(name, fn, a, [_sds(a[0].shape)] * 3)
    return tuple(t.reshape(shp) for t in o)


def _to_perm(a):
    b, ls, d = a.shape
    n = ls // 4
    return a.reshape(b * 4, n, d).swapaxes(0, 1).reshape(n * 8, d)


def _from_perm(p, b, ls):
    n = ls // 4
    return p.reshape(n, b * 4, p.shape[-1]).swapaxes(0, 1).reshape(b, ls, p.shape[-1])


def _pos_embed(rows, dim):
    def sincos(pos, d):
        quarter = d // 2
        omega = POS_TEMP ** (-jnp.arange(quarter, dtype=F32) / quarter)
        ang = pos[:, None] * omega[None, :]
        return jnp.concatenate([jnp.sin(ang), jnp.cos(ang)], axis=-1)

    row_idx = jnp.repeat(jnp.arange(rows), GRID_W).astype(F32)
    col_idx = jnp.tile(jnp.arange(GRID_W), rows).astype(F32)
    return jnp.concatenate([sincos(row_idx, dim // 2), sincos(col_idx, dim // 2)], axis=-1)


def _stream_of(cfg, off_tiles, tile_rows):
    nct = cfg.Tc // tile_rows
    return lambda i: jnp.where(i + off_tiles >= nct, 1, 0)


def _ada_fwd(c_all, w_ada, b_shard):
    nl, d, w = w_ada.shape
    tn = 512 if w % 512 == 0 else w

    def body(c_ref, w_ref, b_ref, o_ref):
        cv = c_ref[...]
        cond = (cv * _sigmoid(cv)).astype(BF16)
        o_ref[...] = _dot(cond, w_ref[...].astype(BF16)) + b_ref[...]

    return pl.pallas_call(
        body, name="ada_fwd", grid=(nl, w // tn),
        in_specs=[_full(c_all.shape), pl.BlockSpec((None, d, tn), lambda l, j: (l, 0, j)),
                  pl.BlockSpec((None, 1, tn), lambda l, j: (l, 0, j))],
        out_specs=pl.BlockSpec((None, c_all.shape[0], tn), lambda l, j: (l, 0, j)),
        out_shape=_sds((nl, c_all.shape[0], w)), compiler_params=_cp(2))(c_all, w_ada, b_shard)


def _ada_bwd(c_all, dmod_shard, w_ada):
    nl, d, w = w_ada.shape
    tn = 512 if w % 512 == 0 else w
    nr = c_all.shape[0]

    def body(c_ref, dm_ref, w_ref, gw_ref, dc_ref):
        j = pl.program_id(0) * (w // tn) + pl.program_id(1)
        cv = c_ref[...]
        cond = (cv * _sigmoid(cv)).astype(BF16)
        dm = dm_ref[...].astype(BF16)
        gw_ref[...] = _dot_tn(cond, dm)
        part = _dot_nt(dm[16:24], w_ref[...].astype(BF16))

        @pl.when(j == 0)
        def _():
            dc_ref[...] = part

        @pl.when(j > 0)
        def _():
            dc_ref[...] += part

    return pl.pallas_call(
        body, name="ada_bwd", grid=(nl, w // tn),
        in_specs=[_full(c_all.shape), pl.BlockSpec((None, nr, tn), lambda l, j: (l, 0, j)),
                  pl.BlockSpec((None, d, tn), lambda l, j: (l, 0, j))],
        out_specs=[pl.BlockSpec((None, d, tn), lambda l, j: (l, 0, j)), _full((8, d))],
        out_shape=[_sds((nl, d, w)), _sds((8, d))], compiler_params=_cp(2))(c_all, dmod_shard, w_ada)


def _disc(lr, li, ldt, br, bi):
    lr = jnp.minimum(lr, LAMBDA_RE_MAX)
    dt = jnp.exp(ldt)
    mag = jnp.exp(lr * dt)
    abr = mag * jnp.cos(li * dt)
    abi = mag * jnp.sin(li * dt)
    den = lr * lr + li * li
    nr = abr - 1.0
    ni = abi
    cr = (nr * lr + ni * li) / den
    ci = (ni * lr - nr * li) / den
    return abr, abi, cr[None] * br - ci[None] * bi, cr[None] * bi + ci[None] * br


def _disc_fwd(lr, li, ldt, br, bi):
    def body(a, b, c, d, e, o1, o2, o3, o4):
        r = _disc(a[...], b[...], c[...], d[...], e[...])
        o1[...], o2[...], o3[...], o4[...] = r

    return pl.pallas_call(body, name="s5_disc_fwd", out_shape=[_sds(lr.shape), _sds(lr.shape), _sds(br.shape), _sds(br.shape)])(
        lr, li, ldt, br, bi)


def _disc_bwd(lr, li, ldt, br, bi, g_abr, g_abi, g_bbr, g_bbi):
    def body(a, b, c, d, e, g1, g2, g3, g4, o1, o2, o3, o4, o5):
        _, vjp = jax.vjp(_disc, a[...], b[...], c[...], d[...], e[...])
        r = vjp((g1[...], g2[...], g3[...], g4[...]))
        o1[...], o2[...], o3[...], o4[...], o5[...] = r

    return pl.pallas_call(
        body, name="s5_disc_bwd",
        out_shape=[_sds(lr.shape), _sds(li.shape), _sds(ldt.shape), _sds(br.shape), _sds(bi.shape)])(
        lr, li, ldt, br, bi, g_abr, g_abi, g_bbr, g_bbi)


def _s5_layouts(cfg, lam_re, lam_im, log_dt, b_re, b_im):
    P, G = cfg.P, cfg.G
    lr = lam_re.transpose(2, 0, 1).reshape(P, 2 * G)
    li = lam_im.transpose(2, 0, 1).reshape(P, 2 * G)
    ldt = log_dt.reshape(1, 2 * G)
    br = b_re.transpose(3, 2, 0, 1).reshape(S5_GROUP, P, 2 * G)
    bi = b_im.transpose(3, 2, 0, 1).reshape(S5_GROUP, P, 2 * G)
    return lr, li, ldt, br, bi


def _coef_rows(cfg, abr, abi, conj):
    def one(t):
        return t.reshape(cfg.P, 2, cfg.G).transpose(1, 2, 0).reshape(2, cfg.NS)
    a = jnp.stack([one(abr), -one(abi) if conj else one(abi)], axis=1)
    return jnp.broadcast_to(a[:, :, None, :], (2, 2, SUBLANES, cfg.NS))


def _blockdiag_b(cfg, bbr, bbi):
    eye = jnp.eye(8, dtype=F32)

    def one(t):
        t = t.reshape(S5_GROUP, cfg.P, 2, cfg.G).transpose(2, 3, 0, 1)
        t = t.reshape(2, cfg.NO, 8, S5_GROUP, cfg.P)
        return jnp.einsum("dogcp,gh->dogchp", t, eye).reshape(2, cfg.NO, OCT_CH, OCT_ST)

    return jnp.concatenate([one(bbr), one(bbi)], axis=-1).astype(BF16)


def _blockdiag_c(cfg, c_re, c_im):
    eye = jnp.eye(8, dtype=F32)

    def one(t):
        t = t.transpose(0, 1, 3, 2).reshape(2, cfg.NO, 8, cfg.P, S5_GROUP)
        return jnp.einsum("dogpc,gh->dogphc", t, eye).reshape(2, cfg.NO, OCT_ST, OCT_CH)

    return jnp.concatenate([one(c_re), -one(c_im)], axis=2).astype(BF16)


def _diag_b(cfg, dbf):
    eye = jnp.eye(8, dtype=F32)

    def one(t):
        t = t.reshape(2, cfg.NO, 8, S5_GROUP, 8, cfg.P)
        t = jnp.einsum("dogchp,gh->dogcp", t, eye).reshape(2, cfg.G, S5_GROUP, cfg.P)
        return t.transpose(2, 3, 0, 1).reshape(S5_GROUP, cfg.P, 2 * cfg.G)

    return one(dbf[..., :OCT_ST]), one(dbf[..., OCT_ST:])


def _diag_c(cfg, dcft):
    eye = jnp.eye(8, dtype=F32)

    def one(t):
        t = t.reshape(2, cfg.NO, 8, S5_GROUP, 8, cfg.P)
        return jnp.einsum("dohcgp,gh->dogcp", t, eye).reshape(2, cfg.G, S5_GROUP, cfg.P)

    return one(dcft[..., :OCT_ST]), -one(dcft[..., OCT_ST:])


def _recur(buf, st, a_ref, n_oct, ti, rev, store):
    for o in range(0, n_oct, 2):
        cols = [(pl.ds(oo * 2 * OCT_ST, OCT_ST), pl.ds(oo * 2 * OCT_ST + OCT_ST, OCT_ST)) for oo in (o, o + 1)]
        scol = [pl.ds(oo * OCT_ST, OCT_ST) for oo in (o, o + 1)]
        coef = [(a_ref[0, :, sc], a_ref[1, :, sc]) for sc in scol]
        init = (st[0, :, scol[0]], st[1, :, scol[0]], st[0, :, scol[1]], st[1, :, scol[1]])

        def step(i4, carry, cols=cols, coef=coef):
            carry = list(carry)
            for q in range(unroll):
                i = i4 * unroll + q
                r = pl.multiple_of((i + rev * (ti - 1 - 2 * i)) * 8, 8)
                for s in range(2):
                    sr, si = carry[2 * s], carry[2 * s + 1]
                    ar, ai = coef[s]
                    zr = buf[pl.ds(r, 8), cols[s][0]]
                    zi = buf[pl.ds(r, 8), cols[s][1]]
                    nr = ar * sr - ai * si + zr
                    ni = ar * si + ai * sr + zi
                    if store:
                        buf[pl.ds(r, 8), cols[s][0]] = nr
                        buf[pl.ds(r, 8), cols[s][1]] = ni
                    carry[2 * s], carry[2 * s + 1] = nr, ni
            return tuple(carry)

        unroll = 4 if ti % 4 == 0 else 1
        fin = lax.fori_loop(0, ti // unroll, step, init)
        st[0, :, scol[0]] = fin[0]
        st[1, :, scol[0]] = fin[1]
        st[0, :, scol[1]] = fin[2]
        st[1, :, scol[1]] = fin[3]


def _s5_fwd_pass(cfg, name, tok, mod8, col_sh, col_sc, bf, acoef, r0, n, s_init=None, cf=None, y_prev=None):
    D, NO, NS = cfg.D, cfg.NO, cfg.NS
    ti = cfg.ti(n)
    nb = n // ti
    R = 8 * ti
    ob = r0 // R
    second = s_init is not None
    blk = lambda d, j: ob + j + d * (nb - 1 - 2 * j)

    def body(*refs):
        if second:
            tok_ref, mod_ref, bf_ref, a_ref, si_ref, cf_ref, yp_ref, y_ref, ck_ref, fin_ref, zbuf, st = refs
        else:
            tok_ref, mod_ref, bf_ref, a_ref, fin_ref, zbuf, st = refs
        d = pl.program_id(0)
        j = pl.program_id(1)

        @pl.when(j == 0)
        def _():
            if second:
                st[...] = si_ref[...]
            else:
                st[...] = jnp.zeros_like(st)

        if second:
            ck_ref[...] = st[...]
        u = _mod(tok_ref[...], mod_ref[:, col_sh:col_sh + D], mod_ref[:, col_sc:col_sc + D]).astype(BF16)
        for o in range(NO):
            zbuf[:, o * 1024:(o + 1) * 1024] = _dot(u[:, o * OCT_CH:(o + 1) * OCT_CH], bf_ref[o])
        _recur(zbuf, st, a_ref, NO, ti, d, second)
        if second:
            for o in range(NO):
                y_ref[:, o * OCT_CH:(o + 1) * OCT_CH] = _dot(zbuf[:, o * 1024:(o + 1) * 1024].astype(BF16), cf_ref[o])

        @pl.when(j == nb - 1)
        def _():
            fin_ref[...] = st[...]

    st_spec = pl.BlockSpec((None, 2, 8, NS), lambda d, j: (d, 0, 0, 0))
    in_specs = [pl.BlockSpec((R, D), lambda d, j: (blk(d, j), 0)), _full(mod8.shape),
                pl.BlockSpec((None, NO, OCT_CH, 1024), lambda d, j: (d, 0, 0, 0)), st_spec]
    args = [tok, mod8, bf, acoef]
    scratch = [pltpu.VMEM((R, NO * 1024), F32), pltpu.VMEM((2, 8, NS), F32)]
    if not second:
        return pl.pallas_call(body, name=name, grid=(2, nb), in_specs=in_specs, out_specs=st_spec,
                              out_shape=_sds((2, 2, 8, NS)), scratch_shapes=scratch, compiler_params=_cp(2))(*args)
    in_specs += [st_spec, pl.BlockSpec((None, NO, 1024, OCT_CH), lambda d, j: (d, 0, 0, 0))]
    args += [s_init, cf]
    aliases = {}
    if y_prev is not None:
        in_specs.append(ANY)
        args.append(y_prev)
        aliases = {6: 0}
    else:
        in_specs.append(_full((8, LANES)))
        args.append(jnp.zeros((8, LANES), F32))
    out_specs = [pl.BlockSpec((None, R, D), lambda d, j: (d, blk(d, j), 0)),
                 pl.BlockSpec((None, None, 2, 8, NS), lambda d, j: (d, j + d * (nb - 1 - 2 * j), 0, 0, 0)), st_spec]
    out_shape = [_sds((2, cfg.T, D)), _sds((2, nb, 2, 8, NS)), _sds((2, 2, 8, NS))]
    return pl.pallas_call(body, name=name, grid=(2, nb), in_specs=in_specs, out_specs=out_specs, out_shape=out_shape,
                          input_output_aliases=aliases, scratch_shapes=scratch, compiler_params=_cp(2))(*args)


def _s5_chain(cfg, name, fin_local, acoef, n, inc, prev_fin=None):
    NS = cfg.NS
    nsq = int(round(math.log2(n)))
    assert 2 ** nsq == n

    def body(*refs):
        if prev_fin is not None:
            f_ref, a_ref, p_ref, o_ref = refs
        else:
            f_ref, a_ref, o_ref = refs
        for d in range(2):
            pr, pi = a_ref[d, 0, 0:1, :], a_ref[d, 1, 0:1, :]
            for _ in range(nsq):
                pr, pi = pr * pr - pi * pi, 2.0 * pr * pi
            for b in range(2):
                order = [4 * b + k for k in range(4)]
                if not inc[d]:
                    order = order[::-1]
                if prev_fin is not None:
                    last = order[-1]
                    sr, si = p_ref[d, 0, last:last + 1, :], p_ref[d, 1, last:last + 1, :]
                else:
                    sr = jnp.zeros((1, NS), F32)
                    si = jnp.zeros((1, NS), F32)
                for k in order:
                    o_ref[d, 0, k:k + 1, :] = sr
                    o_ref[d, 1, k:k + 1, :] = si
                    fr, fi = f_ref[d, 0, k:k + 1, :], f_ref[d, 1, k:k + 1, :]
                    sr, si = pr * sr - pi * si + fr, pr * si + pi * sr + fi

    args = [fin_local, acoef] + ([prev_fin] if prev_fin is not None else [])
    return pl.pallas_call(body, name=name, out_shape=_sds((2, 2, 8, NS)))(*args)


def _s5_forward(cfg, tag, tok, mod8, col_sh, col_sc, bf, cf, acoef):
    saved = {}
    fin_prev = None
    y = None
    for ph, (r0, n) in (("c", (0, cfg.nc)), ("x", (cfg.Tc, cfg.nx))):
        m8 = mod8[0 if ph == "c" else 1]
        loc = _s5_fwd_pass(cfg, f"{tag}_scan1{ph}", tok, m8, col_sh, col_sc, bf, acoef, r0, n)
        s_in = _s5_chain(cfg, f"{tag}_chain{ph}", loc, acoef, n, (True, False), fin_prev)
        y, ck, fin_prev = _s5_fwd_pass(cfg, f"{tag}_scan2{ph}", tok, m8, col_sh, col_sc, bf, acoef, r0, n, s_in, cf, y)
        saved[ph] = ck
    return y, saved


def _s5_bwd_pass(cfg, name, dy, tok, mod8, col_sh, col_sc, bf, cf, acoef, acoef_adj, r0, n, g_init=None, ck=None,
                 du_prev=None):
    D, NO, NS = cfg.D, cfg.NO, cfg.NS
    ti = cfg.ti(n)
    nb = n // ti
    R = 8 * ti
    ob = r0 // R
    second = g_init is not None
    has_dy = dy is not None
    blk = lambda d, j: ob + j + (1 - d) * (nb - 1 - 2 * j)

    def body(*refs):
        refs = list(refs)
        dy_ref = refs.pop(0) if has_dy else None
        if second:
            (tok_ref, mod_ref, bf_ref, cf_ref, a_ref, aa_ref, gi_ref, ck_ref, dup_ref,
             du_ref, da_ref, dbf_ref, dcf_ref, gfin_ref, qbuf, zbuf, gst, hst) = refs
        else:
            cf_ref, aa_ref, gfin_ref, qbuf, gst = refs
        d = pl.program_id(0)
        j = pl.program_id(1)

        @pl.when(j == 0)
        def _():
            if second:
                gst[...] = gi_ref[...]
                da_ref[...] = jnp.zeros_like(da_ref)
                dbf_ref[...] = jnp.zeros_like(dbf_ref)
                dcf_ref[...] = jnp.zeros_like(dcf_ref)
            else:
                gst[...] = jnp.zeros_like(gst)

        if has_dy:
            dyb = dy_ref[...].astype(BF16)
            for o in range(NO):
                qbuf[:, o * 1024:(o + 1) * 1024] = _dot_nt(dyb[:, o * OCT_CH:(o + 1) * OCT_CH], cf_ref[o])
        else:
            qbuf[...] = jnp.zeros_like(qbuf)
        _recur(qbuf, gst, aa_ref, NO, ti, 1 - d, second)

        if second:
            u = _mod(tok_ref[...], mod_ref[:, col_sh:col_sh + D], mod_ref[:, col_sc:col_sc + D]).astype(BF16)
            for o in range(NO):
                zbuf[:, o * 1024:(o + 1) * 1024] = _dot(u[:, o * OCT_CH:(o + 1) * OCT_CH], bf_ref[o])
            hst[...] = ck_ref[...]
            _recur(zbuf, hst, a_ref, NO, ti, d, True)

            g_off, h_off = (1 - d) * 8, d * 8
            edge = pl.multiple_of(d * (R - 8), 8)
            for o in range(0, NO, 2):
                cols = [(pl.ds(oo * 1024, OCT_ST), pl.ds(oo * 1024 + OCT_ST, OCT_ST)) for oo in (o, o + 1)]
                scol = [pl.ds(oo * OCT_ST, OCT_ST) for oo in (o, o + 1)]
                init = []
                for s in range(2):
                    er, ei = qbuf[pl.ds(edge, 8), cols[s][0]], qbuf[pl.ds(edge, 8), cols[s][1]]
                    kr, ki = ck_ref[0, :, scol[s]], ck_ref[1, :, scol[s]]
                    init += [er * kr + ei * ki, ei * kr - er * ki]

                def stp(i, carry, cols=cols):
                    rg = pl.multiple_of(i * 8 + g_off, 8)
                    rh = pl.multiple_of(i * 8 + h_off, 8)
                    out = []
                    for s in range(2):
                        gr, gi = qbuf[pl.ds(rg, 8), cols[s][0]], qbuf[pl.ds(rg, 8), cols[s][1]]
                        hr, hi = zbuf[pl.ds(rh, 8), cols[s][0]], zbuf[pl.ds(rh, 8), cols[s][1]]
                        out += [carry[2 * s] + (gr * hr + gi * hi), carry[2 * s + 1] + (gi * hr - gr * hi)]
                    return tuple(out)

                fin = lax.fori_loop(0, ti - 1, stp, tuple(init))
                for s in range(2):
                    da_ref[0, :, scol[s]] += fin[2 * s]
                    da_ref[1, :, scol[s]] += fin[2 * s + 1]

            for o in range(NO):
                gb = qbuf[:, o * 1024:(o + 1) * 1024].astype(BF16)
                uo = u[:, o * OCT_CH:(o + 1) * OCT_CH]
                dbf_ref[o] += _dot_tn(uo, gb)
                if has_dy:
                    dcf_ref[o] += _dot_tn(dyb[:, o * OCT_CH:(o + 1) * OCT_CH], zbuf[:, o * 1024:(o + 1) * 1024].astype(BF16))
                du_ref[:, o * OCT_CH:(o + 1) * OCT_CH] = _dot_nt(gb, bf_ref[o])

        @pl.when(j == nb - 1)
        def _():
            gfin_ref[...] = gst[...]

    st_spec = pl.BlockSpec((None, 2, 8, NS), lambda d, j: (d, 0, 0, 0))
    row_spec = pl.BlockSpec((R, D), lambda d, j: (blk(d, j), 0))
    bf_spec = pl.BlockSpec((None, NO, OCT_CH, 1024), lambda d, j: (d, 0, 0, 0))
    cf_spec = pl.BlockSpec((None, NO, 1024, OCT_CH), lambda d, j: (d, 0, 0, 0))
    in_specs, args = [], []
    if has_dy:
        in_specs.append(row_spec)
        args.append(dy)
    if not second:
        in_specs += [cf_spec, st_spec]
        args += [cf, acoef_adj]
        return pl.pallas_call(body, name=name, grid=(2, nb), in_specs=in_specs, out_specs=st_spec,
                              out_shape=_sds((2, 2, 8, NS)),
                              scratch_shapes=[pltpu.VMEM((R, NO * 1024), F32), pltpu.VMEM((2, 8, NS), F32)],
                              compiler_params=_cp(2))(*args)
    ck_spec = pl.BlockSpec((None, None, 2, 8, NS), lambda d, j: (d, j + (1 - d) * (nb - 1 - 2 * j), 0, 0, 0))
    in_specs += [row_spec, _full(mod8.shape), bf_spec, cf_spec, st_spec, st_spec, st_spec, ck_spec]
    args += [tok, mod8, bf, cf, acoef, acoef_adj, g_init, ck]
    n_before = len(args)
    aliases = {}
    if du_prev is not None:
        in_specs.append(ANY)
        args.append(du_prev)
        aliases = {n_before: 0}
    else:
        in_specs.append(_full((8, LANES)))
        args.append(jnp.zeros((8, LANES), F32))
    out_specs = [pl.BlockSpec((None, R, D), lambda d, j: (d, blk(d, j), 0)), st_spec, bf_spec, bf_spec, st_spec]
    out_shape = [_sds((2, cfg.T, D)), _sds((2, 2, 8, NS)), _sds((2, NO, OCT_CH, 1024)), _sds((2, NO, OCT_CH, 1024)),
                 _sds((2, 2, 8, NS))]
    scratch = [pltpu.VMEM((R, NO * 1024), F32), pltpu.VMEM((R, NO * 1024), F32), pltpu.VMEM((2, 8, NS), F32),
               pltpu.VMEM((2, 8, NS), F32)]
    return pl.pallas_call(body, name=name, grid=(2, nb), in_specs=in_specs, out_specs=out_specs, out_shape=out_shape,
                          input_output_aliases=aliases, scratch_shapes=scratch, compiler_params=_cp(2))(*args)


def _s5_backward(cfg, tag, dy, dy_ctx, tok, mod8, col_sh, col_sc, bf, cf, acoef, acoef_adj, saved):
    g_prev = None
    acc = None
    du = None
    for ph, (r0, n) in (("x", (cfg.Tc, cfg.nx)), ("c", (0, cfg.nc))):
        m8 = mod8[0 if ph == "c" else 1]
        dyp = dy if (ph == "x" or dy_ctx) else None
        loc = _s5_bwd_pass(cfg, f"{tag}_adjA{ph}", dyp, tok, m8, col_sh, col_sc, bf, cf, acoef, acoef_adj, r0, n)
        g_in = _s5_chain(cfg, f"{tag}_adjchain{ph}", loc, acoef_adj, n, (False, True), g_prev)
        du, da, dbf, dcf, g_prev = _s5_bwd_pass(cfg, f"{tag}_adjB{ph}", dyp, tok, m8, col_sh, col_sc, bf, cf, acoef,
                                                acoef_adj, r0, n, g_in, saved[ph], du)
        new = (da, dbf, dcf)
        if acc is None:
            acc = new
        else:
            acc = tuple(_ew(f"{tag}_accsum{q}", lambda a, b: a + b, [a.reshape(-1, a.shape[-1]), b.reshape(-1, b.shape[-1])],
                            [_sds((a.size // a.shape[-1], a.shape[-1]))])[0].reshape(a.shape)
                        for q, (a, b) in enumerate(zip(acc, new)))
    return du, acc


def _tok_specs(cfg, rows, width, tile=None):
    tm = tile or cfg.TM
    off = rows[0] // tm
    return pl.BlockSpec((tm, width), lambda i: (i + off, 0)), rows[1] // tm, off


def _mod_spec(cfg, mod8, off):
    st = _stream_of(cfg, off, cfg.TM)
    return pl.BlockSpec((None, 8, mod8.shape[-1]), lambda i: (st(i), 0, 0))


def _wspec(w):
    fam, slot = w
    _, _, kk, nn = fam.shape
    return pl.BlockSpec((4, None, kk, nn), lambda *i: (0, slot, 0, 0), pipeline_mode=pl.Buffered(1))


def _glu_ln(cfg, name, rows, tok, y, mod8, cols, dskip, w, b, gain, bias):
    D, TM = cfg.D, cfg.TM
    csh, csc, cg = cols
    spec, nt, off = _tok_specs(cfg, rows, D)
    spec2, _, _ = _tok_specs(cfg, rows, 2 * D)

    def body(tok_ref, y_ref, mod_ref, ds_ref, w_ref, b_ref, g_ref, bi_ref, x1_ref, r1_ref, mix_ref, zz_ref, zb_ref, yy_ref):
        t = tok_ref[...]
        u = _mod(t, mod_ref[:, csh:csh + D], mod_ref[:, csc:csc + D])
        yy = ds_ref[...] * u + y_ref[0] + y_ref[1]
        zb = _gelu(yy).astype(BF16)
        zz = jnp.concatenate([_dot(zb, w_ref[s]) for s in range(4)], axis=-1) + b_ref[...]
        mix = zz[:, :D] * _sigmoid(zz[:, D:])
        r1 = DN_ALPHA * t + _rowscale(mix, mod_ref[:, cg:cg + D])
        xhat, _ = _ln_stats(r1)
        x1_ref[...] = xhat * g_ref[...] + bi_ref[...]
        r1_ref[...] = r1
        mix_ref[...] = mix
        zz_ref[...] = zz
        zb_ref[...] = zb
        yy_ref[...] = yy

    T = cfg.T
    return pl.pallas_call(
        body, name=name, grid=(nt,),
        in_specs=[spec, pl.BlockSpec((2, TM, D), lambda i: (0, i + off, 0)), _mod_spec(cfg, mod8, off), _full((1, D)),
                  _wspec(w), _full((1, 2 * D)), _full((1, D)), _full((1, D))],
        out_specs=[spec, spec, spec, spec2, spec, spec],
        out_shape=[_sds((T, D)), _sds((T, D)), _sds((T, D)), _sds((T, 2 * D)), _sds((T, D), BF16), _sds((T, D))],
        compiler_params=_cp(1))(tok, y, mod8, dskip, w[0], b, gain, bias)


def _mlp_ln(cfg, name, rows, x1, mod8, cols, w1, w2, gain, bias):
    D, TM = cfg.D, cfg.TM
    csh, csc, cg = cols
    spec, nt, off = _tok_specs(cfg, rows, D)
    specf, _, _ = _tok_specs(cfg, rows, cfg.F)
    fb = cfg.F // 4

    def body(x_ref, mod_ref, w1_ref, w2_ref, g_ref, bi_ref, x2_ref, r2_ref, out_ref, a_ref, h_ref):
        t = x_ref[...]
        h = _mod(t, mod_ref[:, csh:csh + D], mod_ref[:, csc:csc + D]).astype(BF16)
        out = jnp.zeros((TM, D), F32)
        for s in range(4):
            hid = jnp.maximum(_dot(h, w1_ref[s]), 0.0)
            a = (hid * hid).astype(BF16)
            a_ref[:, s * fb:(s + 1) * fb] = a
            out = out + _dot(a, w2_ref[s])
        r2 = DN_ALPHA * t + _rowscale(out, mod_ref[:, cg:cg + D])
        xhat, _ = _ln_stats(r2)
        x2_ref[...] = xhat * g_ref[...] + bi_ref[...]
        r2_ref[...] = r2
        out_ref[...] = out
        h_ref[...] = h

    T = cfg.T
    return pl.pallas_call(
        body, name=name, grid=(nt,),
        in_specs=[spec, _mod_spec(cfg, mod8, off), _wspec(w1), _wspec(w2), _full((1, D)), _full((1, D))],
        out_specs=[spec, spec, spec, specf, spec],
        out_shape=[_sds((T, D)), _sds((T, D)), _sds((T, D)), _sds((T, cfg.F), BF16), _sds((T, D), BF16)],
        compiler_params=_cp(1))(x1, mod8, w1[0], w2[0], gain, bias)


def _pw1_glu(cfg, name, rows, tok, mod8, cols, w, b):
    D, TM = cfg.D, cfg.TM
    csh, csc = cols
    spec, nt, off = _tok_specs(cfg, rows, D)
    spec2, _, _ = _tok_specs(cfg, rows, 2 * D)

    def body(tok_ref, mod_ref, w_ref, b_ref, aa_ref, ag_ref, h_ref):
        h = _mod(tok_ref[...], mod_ref[:, csh:csh + D], mod_ref[:, csc:csc + D]).astype(BF16)
        aa = jnp.concatenate([_dot(h, w_ref[s]) for s in range(4)], axis=-1) + b_ref[...]
        aa_ref[...] = aa
        ag_ref[...] = aa[:, :D] * _sigmoid(aa[:, D:])
        h_ref[...] = h

    T = cfg.T
    return pl.pallas_call(
        body, name=name, grid=(nt,),
        in_specs=[spec, _mod_spec(cfg, mod8, off), _wspec(w), _full((1, 2 * D))],
        out_specs=[spec2, spec, spec],
        out_shape=[_sds((T, 2 * D)), _sds((T, D)), _sds((T, D), BF16)], compiler_params=_cp(1))(tok, mod8, w[0], b)


def _halo_maps(cfg, rows):
    TM, HB = cfg.TM, cfg.HB
    off = rows[0] // TM
    nct = cfg.Tc // TM
    ntx = cfg.Tx // TM

    def phase(i):
        t = i + off
        is_x = t >= nct
        first = jnp.where(is_x, nct, 0)
        cnt = jnp.where(is_x, ntx, nct)
        return t, first, cnt

    def prev(i):
        t, first, cnt = phase(i)
        return jnp.where(t == first, 2 * (first + cnt) - 1, 2 * t - 1), 0

    def nxt(i):
        t, first, cnt = phase(i)
        return jnp.where(t == first + cnt - 1, 2 * first, 2 * t + 2), 0

    def edge(i):
        t, first, cnt = phase(i)
        return t == first, t == first + cnt - 1

    return prev, nxt, edge, off


def _halo_fix(prev, nxt, is_first, is_last):
    hb, d = prev.shape
    k = lax.broadcasted_iota(jnp.int32, (hb // 8, 8, d), 1)
    p3 = prev.reshape(hb // 8, 8, d)
    n3 = nxt.reshape(hb // 8, 8, d)
    p_roll = jnp.where((k % 4) == 0, 0.0, pltpu.roll(p3, 1, 1))
    n_roll = jnp.where((k % 4) == 3, 0.0, pltpu.roll(n3, 7, 1))
    p3 = jnp.where(is_first, p_roll, p3)
    n3 = jnp.where(is_last, n_roll, n3)
    return p3.reshape(hb, d), n3.reshape(hb, d)


def _dwconv_ln(cfg, name, rows, ag, w_dw, b_dw, ln_g, ln_b):
    D, TM, HB, KW, half = cfg.D, cfg.TM, cfg.HB, cfg.KW, cfg.half
    prev_map, next_map, edge, off = _halo_maps(cfg, rows)
    spec, nt, _ = _tok_specs(cfg, rows, D)

    def body(cur_ref, prev_ref, next_ref, w_ref, b_ref, g_ref, bi_ref, cv_ref, s_ref, ext):
        i = pl.program_id(0)
        is_first, is_last = edge(i)
        p, n = _halo_fix(prev_ref[...], next_ref[...], is_first, is_last)
        ext[0:HB, :] = p
        ext[HB:HB + TM, :] = cur_ref[...]
        ext[HB + TM:, :] = n

        def row_chunk(rc, carry):
            r0 = pl.multiple_of(rc * CONV_ROWS, CONV_ROWS)
            for lc in range(D // LANES):
                ls = pl.ds(lc * LANES, LANES)
                acc = jnp.zeros((CONV_ROWS, LANES), F32)
                for k in range(KW):
                    acc = acc + w_ref[k:k + 1, ls] * ext[pl.ds(r0 + HB + 8 * (k - half), CONV_ROWS), ls]
                cv_ref[pl.ds(r0, CONV_ROWS), ls] = acc + b_ref[:, ls]
            return carry

        lax.fori_loop(0, TM // CONV_ROWS, row_chunk, 0)
        xhat, _ = _ln_stats(cv_ref[...])
        nn = xhat * g_ref[...] + bi_ref[...]
        s_ref[...] = (nn * _sigmoid(nn)).astype(BF16)

    T = cfg.T
    return pl.pallas_call(
        body, name=name, grid=(nt,),
        in_specs=[spec, pl.BlockSpec((HB, D), prev_map), pl.BlockSpec((HB, D), next_map), _full((KW, D)),
                  _full((1, D)), _full((1, D)), _full((1, D))],
        out_specs=[spec, spec], out_shape=[_sds((T, D)), _sds((T, D), BF16)],
        scratch_shapes=[pltpu.VMEM((TM + 2 * HB, D), F32)], compiler_params=_cp(1))(ag, ag, ag, w_dw, b_dw, ln_g, ln_b)


def _pw2_ln(cfg, name, rows, s, tok, mod8, cg, w, b, gain, bias):
    D, TM = cfg.D, cfg.TM
    spec, nt, off = _tok_specs(cfg, rows, D)
    kb = D // 4

    def body(s_ref, tok_ref, mod_ref, w_ref, b_ref, g_ref, bi_ref, x1_ref, r1_ref, mix_ref):
        sv = s_ref[...]
        mix = b_ref[...] + jnp.zeros((TM, D), F32)
        for q in range(4):
            mix = mix + _dot(sv[:, q * kb:(q + 1) * kb], w_ref[q])
        r1 = DN_ALPHA * tok_ref[...] + _rowscale(mix, mod_ref[:, cg:cg + D])
        xhat, _ = _ln_stats(r1)
        x1_ref[...] = xhat * g_ref[...] + bi_ref[...]
        r1_ref[...] = r1
        mix_ref[...] = mix

    T = cfg.T
    return pl.pallas_call(
        body, name=name, grid=(nt,),
        in_specs=[spec, spec, _mod_spec(cfg, mod8, off), _wspec(w), _full((1, D)), _full((1, D)), _full((1, D))],
        out_specs=[spec, spec, spec], out_shape=[_sds((T, D))] * 3, compiler_params=_cp(1))(s, tok, mod8, w[0], b, gain, bias)


def _loss(cfg, xf, tgt):
    D, TM = cfg.D, cfg.TM
    spec, nt, off = _tok_specs(cfg, cfg.rows(False), D)

    def body(x_ref, t_ref, l_ref, dx_ref, acc):
        i = pl.program_id(0)
        dlt = x_ref[...] - t_ref[...]

        @pl.when(i == 0)
        def _():
            acc[...] = jnp.zeros_like(acc)

        acc[...] += _sum8(dlt * dlt)
        dx_ref[...] = dlt * (1.0 / D)

        @pl.when(i == nt - 1)
        def _():
            l_ref[...] = jnp.zeros((8, LANES), F32) + jnp.sum(acc[...]) * (0.5 / D)

    return pl.pallas_call(
        body, name="loss", grid=(nt,),
        in_specs=[spec, pl.BlockSpec((TM, D), lambda i: (i, 0))],
        out_specs=[_full((8, LANES)), spec], out_shape=[_sds((8, LANES)), _sds((cfg.T, D))],
        scratch_shapes=[pltpu.VMEM((8, D), F32)], compiler_params=_cp(1))(xf, tgt)


def _masked_spec(cfg, rows, width, valid_from_tile):
    tm = cfg.TM
    off = rows[0] // tm
    return pl.BlockSpec((tm, width), lambda i: (jnp.maximum(i + off, valid_from_tile), 0))


def _lnb(cfg, name, rows, dres, dres_ctx_ok, dh, r, aux, gain, mod_gate, cg, mod_next, csc):
    D, TM = cfg.D, cfg.TM
    spec, nt, off = _tok_specs(cfg, rows, D)
    nct = cfg.Tc // TM
    has_dres, has_dh = dres is not None, dh is not None

    def body(*refs):
        refs = list(refs)
        dres_ref = refs.pop(0) if has_dres else None
        dh_ref = refs.pop(0) if has_dh else None
        r_ref, aux_ref, g_ref, mg_ref = refs[:4]
        refs = refs[4:]
        mn_ref = refs.pop(0) if has_dh else None
        dprev_ref, dbr_ref, dgain_ref, dbias_ref, dg_ref, dsc_ref, dsh_ref, acc_g, acc_b = refs
        i = pl.program_id(0)
        t = i + off
        first_of_stream = (i == 0) | (t == nct)
        xhat, rstd = _ln_stats(r_ref[...])
        dy = jnp.zeros((TM, D), F32)
        if has_dres:
            dv = dres_ref[...]
            if not dres_ctx_ok:
                dv = jnp.where(t >= nct, dv, 0.0)
            dy = dy + dv
        if has_dh:
            dhv = dh_ref[...]
            dy = dy + _rowscale(dhv, 1.0 + mn_ref[:, csc:csc + D])
            x_out = xhat * g_ref[0:1, :] + g_ref[1:2, :]
            s_sc, s_sh = _sum8(dhv * x_out), _sum8(dhv)
        else:
            s_sc = s_sh = jnp.zeros((8, D), F32)
        dr = _ln_bwd(dy * g_ref[0:1, :], xhat, rstd)
        s_g = _sum8(dr * aux_ref[...])

        @pl.when(i == 0)
        def _():
            acc_g[...] = jnp.zeros_like(acc_g)
            acc_b[...] = jnp.zeros_like(acc_b)

        acc_g[...] += _sum8(dy * xhat)
        acc_b[...] += _sum8(dy)

        @pl.when(first_of_stream)
        def _():
            dg_ref[...] = s_g
            dsc_ref[...] = s_sc
            dsh_ref[...] = s_sh

        @pl.when(jnp.logical_not(first_of_stream))
        def _():
            dg_ref[...] += s_g
            dsc_ref[...] += s_sc
            dsh_ref[...] += s_sh

        dprev_ref[...] = DN_ALPHA * dr
        dbr_ref[...] = _rowscale(dr, mg_ref[:, cg:cg + D])

        @pl.when(i == nt - 1)
        def _():
            dgain_ref[...] = jnp.sum(acc_g[...], axis=0, keepdims=True)
            dbias_ref[...] = jnp.sum(acc_b[...], axis=0, keepdims=True)

    st = _stream_of(cfg, off, TM)
    in_specs, args = [], []
    if has_dres:
        in_specs.append(spec if dres_ctx_ok else _masked_spec(cfg, rows, D, nct))
        args.append(dres)
    if has_dh:
        in_specs.append(spec)
        args.append(dh)
    in_specs += [spec, spec, _full((2, D)), _mod_spec(cfg, mod_gate, off)]
    args += [r, aux, gain, mod_gate]
    if has_dh:
        in_specs.append(_mod_spec(cfg, mod_next, off))
        args.append(mod_next)
    acc_spec = pl.BlockSpec((None, 8, D), lambda i: (st(i), 0, 0))
    T = cfg.T
    return pl.pallas_call(
        body, name=name, grid=(nt,), in_specs=in_specs,
        out_specs=[spec, spec, _full((1, D)), _full((1, D)), acc_spec, acc_spec, acc_spec],
        out_shape=[_sds((T, D)), _sds((T, D)), _sds((1, D)), _sds((1, D)), _sds((2, 8, D)), _sds((2, 8, D)), _sds((2, 8, D))],
        scratch_shapes=[pltpu.VMEM((8, D), F32), pltpu.VMEM((8, D), F32)], compiler_params=_cp(1))(*args)


def _mlp_bwd(cfg, name, rows, dbr, a, w1, w2):
    D, TM = cfg.D, cfg.TM
    spec, nt, off = _tok_specs(cfg, rows, D)
    specf, _, _ = _tok_specs(cfg, rows, cfg.F)
    fb = cfg.F // 4

    def body(d_ref, a_ref, w1_ref, w2_ref, dh_ref, dhid_ref, dout_ref):
        dout = d_ref[...].astype(BF16)
        dh = jnp.zeros((TM, D), F32)
        for s in range(4):
            da = _dot_nt(dout, w2_ref[s])
            dhid = (da * (2.0 * jnp.sqrt(a_ref[:, s * fb:(s + 1) * fb].astype(F32)))).astype(BF16)
            dhid_ref[:, s * fb:(s + 1) * fb] = dhid
            dh = dh + _dot_nt(dhid, w1_ref[s])
        dh_ref[...] = dh
        dout_ref[...] = dout

    T = cfg.T
    return pl.pallas_call(
        body, name=name, grid=(nt,),
        in_specs=[spec, specf, _wspec(w1), _wspec(w2)],
        out_specs=[spec, specf, spec],
        out_shape=[_sds((T, D)), _sds((T, cfg.F), BF16), _sds((T, D), BF16)], compiler_params=_cp(1))(dbr, a, w1[0], w2[0])


def _wgrad(cfg, name, rows, a, b, mode, fam, slot):
    tw = cfg.TW
    off = rows[0] // tw
    nt = rows[1] // tw
    fresh = not hasattr(fam, "dtype")
    fam_shape = tuple(fam) if fresh else fam.shape
    _, n, kk, nn = fam_shape

    def body(a_ref, b_ref, *rest):
        o_ref = rest[-1]
        t = pl.program_id(1)
        part = _dot_tn(a_ref[...], b_ref[...])

        @pl.when(t == 0)
        def _():
            o_ref[...] = part

        @pl.when(t > 0)
        def _():
            o_ref[...] += part

    if mode == "col":
        a_spec = pl.BlockSpec((tw, kk), lambda s, t: (t + off, 0))
        b_spec = pl.BlockSpec((tw, nn), lambda s, t: (t + off, s))
    else:
        a_spec = pl.BlockSpec((tw, kk), lambda s, t: (t + off, s))
        b_spec = pl.BlockSpec((tw, nn), lambda s, t: (t + off, 0))
    out_spec = pl.BlockSpec((None, None, kk, nn), lambda s, t: (s, slot, 0, 0))
    if fresh:
        return pl.pallas_call(body, name=name, grid=(4, nt), in_specs=[a_spec, b_spec], out_specs=out_spec,
                              out_shape=_sds(fam_shape), compiler_params=_cp(2))(a, b)
    return pl.pallas_call(body, name=name, grid=(4, nt), in_specs=[a_spec, b_spec, ANY], out_specs=out_spec,
                          out_shape=_sds(fam_shape), input_output_aliases={2: 0}, compiler_params=_cp(2))(a, b, fam)


def _glu_bwd(cfg, name, rows, dmix, pre, w, yy=None):
    D, TM = cfg.D, cfg.TM
    spec, nt, off = _tok_specs(cfg, rows, D)
    spec2, _, _ = _tok_specs(cfg, rows, 2 * D)
    hw = w[0].shape[-1]
    has_y = yy is not None

    def body(*refs):
        refs = list(refs)
        d_ref, p_ref, w_ref = refs[:3]
        y_ref = refs[3] if has_y else None
        dz_ref, dp_ref, db_ref, acc = refs[-4:]
        i = pl.program_id(0)
        dm = d_ref[...]
        po, pg = p_ref[:, :D], p_ref[:, D:]
        sg = _sigmoid(pg)
        dpre = jnp.concatenate([dm * sg, dm * po * sg * (1.0 - sg)], axis=-1)

        @pl.when(i == 0)
        def _():
            acc[...] = jnp.zeros_like(acc)

        acc[...] += _sum8(dpre)
        dpb = dpre.astype(BF16)
        dz = jnp.zeros((TM, D), F32)
        for s in range(4):
            dz = dz + _dot_nt(dpb[:, s * hw:(s + 1) * hw], w_ref[s])
        if has_y:
            dz = dz * _gelu_grad(y_ref[...])
        dz_ref[...] = dz
        dp_ref[...] = dpb

        @pl.when(i == nt - 1)
        def _():
            db_ref[...] = jnp.sum(acc[...], axis=0, keepdims=True)

    T = cfg.T
    in_specs = [spec, spec2, _wspec(w)] + ([spec] if has_y else [])
    args = [dmix, pre, w[0]] + ([yy] if has_y else [])
    return pl.pallas_call(
        body, name=name, grid=(nt,), in_specs=in_specs, out_specs=[spec, spec2, _full((1, 2 * D))],
        out_shape=[_sds((T, D)), _sds((T, 2 * D), BF16), _sds((1, 2 * D))],
        scratch_shapes=[pltpu.VMEM((8, 2 * D), F32)], compiler_params=_cp(1))(*args)


def _s5_du(cfg, name, rows, du, dy, dy_from_tile, tok, mod8, cols, dskip):
    D, TM = cfg.D, cfg.TM
    csh, csc = cols
    spec, nt, off = _tok_specs(cfg, rows, D)

    def body(du_ref, dy_ref, tok_ref, mod_ref, ds_ref, dh_ref, dd_ref, acc):
        i = pl.program_id(0)
        dyv = jnp.where(i + off >= dy_from_tile, dy_ref[...], 0.0)
        u = _mod(tok_ref[...], mod_ref[:, csh:csh + D], mod_ref[:, csc:csc + D])
        dh_ref[...] = du_ref[0] + du_ref[1] + ds_ref[...] * dyv

        @pl.when(i == 0)
        def _():
            acc[...] = jnp.zeros_like(acc)

        acc[...] += _sum8(dyv * u)

        @pl.when(i == nt - 1)
        def _():
            dd_ref[...] = jnp.sum(acc[...], axis=0, keepdims=True)

    T = cfg.T
    return pl.pallas_call(
        body, name=name, grid=(nt,),
        in_specs=[pl.BlockSpec((2, TM, D), lambda i: (0, i + off, 0)), _masked_spec(cfg, rows, D, dy_from_tile), spec,
                  _mod_spec(cfg, mod8, off), _full((1, D))],
        out_specs=[spec, _full((1, D))], out_shape=[_sds((T, D)), _sds((1, D))],
        scratch_shapes=[pltpu.VMEM((8, D), F32)], compiler_params=_cp(1))(du, dy, tok, mod8, dskip)


def _pw2_bwd(cfg, name, rows, dmix, cv, w, ln_g, ln_b):
    D, TM = cfg.D, cfg.TM
    spec, nt, off = _tok_specs(cfg, rows, D)
    kb = D // 4

    def body(d_ref, cv_ref, w_ref, g_ref, b_ref, dcv_ref, dmb_ref, sums_ref, acc):
        i = pl.program_id(0)
        dm = d_ref[...]
        dmb = dm.astype(BF16)
        ds = jnp.concatenate([_dot_nt(dmb, w_ref[q]) for q in range(4)], axis=-1)
        xhat, rstd = _ln_stats(cv_ref[...])
        nn = xhat * g_ref[...] + b_ref[...]
        sg = _sigmoid(nn)
        dn = ds * (sg * (1.0 + nn * (1.0 - sg)))
        dcv = _ln_bwd(dn * g_ref[...], xhat, rstd)

        @pl.when(i == 0)
        def _():
            acc[...] = jnp.zeros_like(acc)

        acc[0] += _sum8(dn * xhat)
        acc[1] += _sum8(dn)
        acc[2] += _sum8(dcv)
        acc[3] += _sum8(dm)
        dcv_ref[...] = dcv
        dmb_ref[...] = dmb

        @pl.when(i == nt - 1)
        def _():
            for q in range(4):
                sums_ref[q:q + 1, :] = jnp.sum(acc[q], axis=0, keepdims=True)

    T = cfg.T
    return pl.pallas_call(
        body, name=name, grid=(nt,),
        in_specs=[spec, spec, _wspec(w), _full((1, D)), _full((1, D))],
        out_specs=[spec, spec, _full((4, D))], out_shape=[_sds((T, D)), _sds((T, D), BF16), _sds((4, D))],
        scratch_shapes=[pltpu.VMEM((4, 8, D), F32)], compiler_params=_cp(1))(dmix, cv, w[0], ln_g, ln_b)


def _dwconv_bwd(cfg, name, rows, dcv, ag, w_dw):
    D, TM, HB, KW, half = cfg.D, cfg.TM, cfg.HB, cfg.KW, cfg.half
    prev_map, next_map, edge, off = _halo_maps(cfg, rows)
    spec, nt, _ = _tok_specs(cfg, rows, D)

    def body(dc_ref, dp_ref, dn_ref, ac_ref, ap_ref, an_ref, w_ref, dag_ref, dw_ref, extd, exta, acc):
        i = pl.program_id(0)
        is_first, is_last = edge(i)
        p, n = _halo_fix(dp_ref[...], dn_ref[...], is_first, is_last)
        extd[0:HB, :] = p
        extd[HB:HB + TM, :] = dc_ref[...]
        extd[HB + TM:, :] = n
        p, n = _halo_fix(ap_ref[...], an_ref[...], is_first, is_last)
        exta[0:HB, :] = p
        exta[HB:HB + TM, :] = ac_ref[...]
        exta[HB + TM:, :] = n

        @pl.when(i == 0)
        def _():
            acc[...] = jnp.zeros_like(acc)

        def row_chunk(rc, carry):
            r0 = pl.multiple_of(rc * CONV_ROWS, CONV_ROWS)
            for lc in range(D // LANES):
                ls = pl.ds(lc * LANES, LANES)
                dcur = dc_ref[pl.ds(r0, CONV_ROWS), ls]
                dag = jnp.zeros((CONV_ROWS, LANES), F32)
                for k in range(KW):
                    dag = dag + w_ref[k:k + 1, ls] * extd[pl.ds(r0 + HB + 8 * (half - k), CONV_ROWS), ls]
                    acc[k, :, ls] += _sum8(dcur * exta[pl.ds(r0 + HB + 8 * (k - half), CONV_ROWS), ls])
                dag_ref[pl.ds(r0, CONV_ROWS), ls] = dag
            return carry

        lax.fori_loop(0, TM // CONV_ROWS, row_chunk, 0)

        @pl.when(i == nt - 1)
        def _():
            for k in range(KW):
                dw_ref[k:k + 1, :] = jnp.sum(acc[k], axis=0, keepdims=True)

    T = cfg.T
    hp, hn = pl.BlockSpec((HB, D), prev_map), pl.BlockSpec((HB, D), next_map)
    return pl.pallas_call(
        body, name=name, grid=(nt,), in_specs=[spec, hp, hn, spec, hp, hn, _full((KW, D))],
        out_specs=[spec, _full((KW, D))], out_shape=[_sds((T, D)), _sds((KW, D))],
        scratch_shapes=[pltpu.VMEM((TM + 2 * HB, D), F32), pltpu.VMEM((TM + 2 * HB, D), F32), pltpu.VMEM((KW, 8, D), F32)],
        compiler_params=_cp(1))(dcv, dcv, dcv, ag, ag, ag, w_dw)


def _input_bwd(cfg, dres, dh, tok0, mod8, csc):
    D, TM = cfg.D, cfg.TM
    rows = cfg.rows(True)
    spec, nt, off = _tok_specs(cfg, rows, D)
    nct = cfg.Tc // TM
    st = _stream_of(cfg, off, TM)

    def body(dr_ref, dh_ref, t_ref, mod_ref, gx_ref, dsc_ref, dsh_ref):
        i = pl.program_id(0)
        dhv = dh_ref[...]
        gx_ref[...] = dr_ref[...] + _rowscale(dhv, 1.0 + mod_ref[:, csc:csc + D])
        first = (i == 0) | (i == nct)
        s_sc, s_sh = _sum8(dhv * t_ref[...]), _sum8(dhv)

        @pl.when(first)
        def _():
            dsc_ref[...] = s_sc
            dsh_ref[...] = s_sh

        @pl.when(jnp.logical_not(first))
        def _():
            dsc_ref[...] += s_sc
            dsh_ref[...] += s_sh

    acc_spec = pl.BlockSpec((None, 8, D), lambda i: (st(i), 0, 0))
    return pl.pallas_call(
        body, name="input_bwd", grid=(nt,), in_specs=[spec, spec, spec, _mod_spec(cfg, mod8, off)],
        out_specs=[spec, acc_spec, acc_spec], out_shape=[_sds((cfg.T, D)), _sds((2, 8, D)), _sds((2, 8, D))],
        compiler_params=_cp(1))(dres, dh, tok0, mod8)


def _dmod_rows(dmod8):
    nl, _, _, w = dmod8.shape

    def body(d_ref, o_ref):
        xs = d_ref[1]
        cs = d_ref[0]
        o_ref[...] = jnp.zeros((8, w), F32)
        o_ref[0:1, :] = jnp.sum(xs[0:4], axis=0, keepdims=True)
        o_ref[1:2, :] = jnp.sum(xs[4:8], axis=0, keepdims=True)
        o_ref[2:3, :] = jnp.sum(cs, axis=0, keepdims=True)

    return pl.pallas_call(body, name="dmod_rows", grid=(nl,),
                          in_specs=[pl.BlockSpec((None, 2, 8, w), lambda l: (l, 0, 0, 0))],
                          out_specs=pl.BlockSpec((None, 8, w), lambda l: (l, 0, 0)), out_shape=_sds((nl, 8, w)),
                          compiler_params=_cp(1))(dmod8)


def _x_only(acc):
    return jnp.concatenate([jnp.zeros_like(acc[:1]), acc[1:]], axis=0)


def _pack(parts):
    bufs, meta, off = [], [], 0
    for p in parts:
        n = p.size
        rows = -(-n // (8 * LANES)) * 8
        flat = jnp.pad(p.reshape(-1).astype(F32), (0, rows * LANES - n)).reshape(rows, LANES)
        bufs.append(flat)
        meta.append((off, rows, p.shape))
        off += rows
    if off % 16:
        bufs.append(jnp.zeros((8, LANES), F32))
    return jnp.concatenate(bufs, axis=0), meta


def _unpack(buf, meta):
    out = []
    for off, rows, shape in meta:
        n = 1
        for s in shape:
            n *= s
        out.append(buf[off:off + rows].reshape(-1)[:n].reshape(shape))
    return out


def kernel(x, c, ctx, c_ctx, w_ada, b_ada, ln_gain, ln_bias, s5_lam_re, s5_lam_im, s5_log_dt, s5_b_re, s5_b_im, s5_c_re, s5_c_im, s5_d, s5_w_glu, s5_b_glu, cv_w_pw1, cv_b_pw1, cv_w_dw, cv_b_dw, cv_ln_g, cv_ln_b, cv_w_pw2, cv_b_pw2, mlp_w1, mlp_w2, loss_target, m_c_ctx, m_w_ada, m_b_ada, m_ln_gain, m_ln_bias, m_s5_lam_re, m_s5_lam_im, m_s5_log_dt, m_s5_b_re, m_s5_b_im, m_s5_c_re, m_s5_c_im, m_s5_d, m_s5_w_glu, m_s5_b_glu, m_cv_w_pw1, m_cv_b_pw1, m_cv_w_dw, m_cv_b_dw, m_cv_ln_g, m_cv_ln_b, m_cv_w_pw2, m_cv_b_pw2, m_mlp_w1, m_mlp_w2, v_c_ctx, v_w_ada, v_b_ada, v_ln_gain, v_ln_bias, v_s5_lam_re, v_s5_lam_im, v_s5_log_dt, v_s5_b_re, v_s5_b_im, v_s5_c_re, v_s5_c_im, v_s5_d, v_s5_w_glu, v_s5_b_glu, v_cv_w_pw1, v_cv_b_pw1, v_cv_w_dw, v_cv_b_dw, v_cv_ln_g, v_cv_ln_b, v_cv_w_pw2, v_cv_b_pw2, v_mlp_w1, v_mlp_w2):
    cfg = _Cfg(x, ctx, mlp_w1, cv_w_dw)
    D, T, Tc, Tx, B = cfg.D, cfg.T, cfg.Tc, cfg.Tx, cfg.B
    ax, ay, ac = lax.axis_index("x"), lax.axis_index("y"), lax.axis_index("c")
    shard = 2 * ax + ay
    dev = 4 * ax + 2 * ay + ac
    Ds = D // 4
    Wa = w_ada.shape[2]

    c_pad = jnp.concatenate([c, jnp.zeros((8 - B, D), F32)], axis=0)
    c_gath = _allgather8("gather_c", c_pad).reshape(8, 8, D)[:, :B].reshape(8 * B, D)
    c_all = jnp.concatenate([c_gath, c_ctx[None], jnp.zeros((7, D), F32)], axis=0)
    b_sh = lax.dynamic_slice_in_dim(b_ada, shard * Wa, Wa, axis=1)[:, None, :]
    mod_sh = _ada_fwd(c_all, w_ada, b_sh)
    mod_g = _allgather8("gather_mod", mod_sh.reshape(DEPTH * 24, Wa)).reshape(4, 2, DEPTH, 24, Wa)[:, 0]
    mods = mod_g.transpose(1, 2, 0, 3).reshape(DEPTH, 24, 4 * Wa)
    mine = lax.dynamic_slice_in_dim(mods, B * dev, B, axis=1)
    mod8 = jnp.stack([jnp.broadcast_to(mods[:, 16:17], (DEPTH, 8, 6 * D)), jnp.repeat(mine, 4, axis=1)], axis=1)
    SH1, SC1, G1, SH2, SC2, G2 = (k * D for k in range(6))

    small_parts = [ln_gain.reshape(-1, Ds), ln_bias.reshape(-1, Ds), cv_b_pw1.reshape(-1, Ds), cv_w_dw.reshape(-1, Ds),
                   cv_b_dw, cv_ln_g, cv_ln_b, cv_b_pw2]
    small_rows = [p.shape[0] for p in small_parts]
    sm = jnp.concatenate(small_parts, axis=0)
    pad_r = -sm.shape[0] % 8
    sm = jnp.pad(sm, ((0, pad_r), (0, 0)))
    sm_g = _allgather8("gather_small", sm).reshape(4, 2, sm.shape[0], Ds)[:, 0]
    pieces, o = [], 0
    for nr in small_rows:
        pieces.append(sm_g[:, o:o + nr])
        o += nr

    def unshard(p, lead):
        return p.reshape((4,) + lead + (Ds,)).transpose(tuple(range(1, len(lead) + 1)) + (0, len(lead) + 1)).reshape(lead + (4 * Ds,))

    ln_gain_f = unshard(pieces[0], (DEPTH, 2))
    ln_bias_f = unshard(pieces[1], (DEPTH, 2))
    nconv = cv_w_dw.shape[0]
    b_pw1_f = pieces[2].reshape(4, nconv, 2 * D // 4).transpose(1, 0, 2).reshape(nconv, 2 * D)
    w_dw_f = unshard(pieces[3], (nconv, cfg.KW))
    b_dw_f, cvg_f, cvb_f, b_pw2_f = (unshard(p, (nconv,)) for p in pieces[4:8])

    ns5 = s5_w_glu.shape[0]
    assert mlp_w1.shape[1:] == mlp_w2.shape[1:]
    fam_a = _place_shard("place_w1", mlp_w1, None, 0, 2 * DEPTH)
    fam_a = _place_shard("place_w2", mlp_w2, fam_a, DEPTH, 2 * DEPTH)
    fam_b = _place_shard("place_wglu", s5_w_glu, None, 0, ns5 + nconv)
    fam_b = _place_shard("place_wpw1", cv_w_pw1, fam_b, ns5, ns5 + nconv)
    fam_c = _place_shard("place_wpw2", cv_w_pw2, None, 0, nconv)
    wa_full, wb_full, wc_full = _gather_weights([fam_a, fam_b, fam_c])

    pos = jnp.broadcast_to(_pos_embed(cfg.L // GRID_W, D)[None], (B, cfg.L, D))
    tok_in = jnp.concatenate([_to_perm(ctx), _to_perm(x)], axis=0)
    pos_in = jnp.concatenate([jnp.zeros((Tc, D), F32), _to_perm(pos)], axis=0)
    tok0 = _ew("add_pos", lambda a, b: a + b, [tok_in, pos_in], [_sds((T, D))])[0]
    tgt = _to_perm(loss_target)

    s5p = []
    for j in range(ns5):
        lay = _s5_layouts(cfg, s5_lam_re[j], s5_lam_im[j], s5_log_dt[j], s5_b_re[j], s5_b_im[j])
        abr, abi, bbr, bbi = _disc_fwd(*lay)
        s5p.append(dict(lay=lay, acoef=_coef_rows(cfg, abr, abi, False), acoef_adj=_coef_rows(cfg, abr, abi, True),
                        bf=_blockdiag_b(cfg, bbr, bbi), cf=_blockdiag_c(cfg, s5_c_re[j], s5_c_im[j])))

    kinds = ["s5" if i % 2 == 0 else "conv" for i in range(DEPTH)]
    tok = tok0
    saved = []
    s5_j = cv_j = 0
    for i in range(DEPTH):
        later_s5 = any(k == "s5" for k in kinds[i + 1:])
        rows = cfg.rows(later_s5)
        m8 = mod8[i]
        sv = dict(tok=tok, rows=rows, kind=kinds[i])
        g0, b0 = ln_gain_f[i, 0][None], ln_bias_f[i, 0][None]
        g1, b1 = ln_gain_f[i, 1][None], ln_bias_f[i, 1][None]
        if kinds[i] == "s5":
            j = s5_j
            s5_j += 1
            p = s5p[j]
            y, ck = _s5_forward(cfg, f"l{i}", tok, m8, SH1, SC1, p["bf"], p["cf"], p["acoef"])
            wg = (wb_full, j)
            x1, r1, mix, zz, zb, yy = _glu_ln(cfg, f"l{i}_glu", rows, tok, y, m8, (SH1, SC1, G1), s5_d[j][None], wg,
                                              s5_b_glu[j][None], g0, b0)
            sv.update(j=j, ck=ck, zz=zz, zb=zb, yy=yy, wg=wg)
        else:
            j = cv_j
            cv_j += 1
            w1c, w2c = (wb_full, ns5 + j), (wc_full, j)
            aa, ag, hb = _pw1_glu(cfg, f"l{i}_pw1", rows, tok, m8, (SH1, SC1), w1c, b_pw1_f[j][None])
            cvv, sb = _dwconv_ln(cfg, f"l{i}_dw", rows, ag, w_dw_f[j], b_dw_f[j][None], cvg_f[j][None], cvb_f[j][None])
            x1, r1, mix = _pw2_ln(cfg, f"l{i}_pw2", rows, sb, tok, m8, G1, w2c, b_pw2_f[j][None], g0, b0)
            sv.update(j=j, aa=aa, ag=ag, hb=hb, cvv=cvv, sb=sb, w1c=w1c, w2c=w2c)
        w1m, w2m = (wa_full, i), (wa_full, DEPTH + i)
        x2, r2, mout, am, hm = _mlp_ln(cfg, f"l{i}_mlp", rows, x1, m8, (SH2, SC2, G2), w1m, w2m, g1, b1)
        sv.update(r1=r1, mix=mix, x1=x1, r2=r2, mout=mout, am=am, hm=hm, w1m=w1m, w2m=w2m, g0=g0, b0=b0, g1=g1, b1=b1)
        saved.append(sv)
        tok = x2

    loss8, dxf = _loss(cfg, tok, tgt)
    loss = lax.psum(loss8[0, 0], ("x", "y", "c"))

    gfam_a = (4, 2 * DEPTH, D, cfg.F // 4)
    gfam_b = (4, ns5 + nconv, D, D // 2)
    gfam_c = (4, nconv, D // 4, D)
    dmod8 = [None] * DEPTH
    g_ln_gain = [[None, None] for _ in range(DEPTH)]
    g_ln_bias = [[None, None] for _ in range(DEPTH)]
    g_s5 = [None] * ns5
    g_cv = [None] * nconv
    dres, dh = dxf, None
    for i in reversed(range(DEPTH)):
        sv = saved[i]
        rows = sv["rows"]
        m8 = mod8[i]
        nxt_m8 = mod8[i + 1] if i + 1 < DEPTH else None
        ctx_ok = True if i + 1 >= DEPTH else (saved[i + 1]["rows"][0] == 0)
        if rows[0] != 0:
            ctx_ok = True
        dprev, dbr, dgn, dbs, dg2, dsc_n, dsh_n = _lnb(
            cfg, f"l{i}_lnb2", rows, dres, ctx_ok, dh, sv["r2"], sv["mout"], jnp.concatenate([sv["g1"], sv["b1"]], 0),
            m8, G2, nxt_m8, SC1)
        if rows[0] != 0:
            dg2, dsc_n, dsh_n = (_x_only(t) for t in (dg2, dsc_n, dsh_n))
        g_ln_gain[i][1], g_ln_bias[i][1] = dgn[0], dbs[0]
        if i + 1 < DEPTH:
            dmod8[i + 1]["sc1"], dmod8[i + 1]["sh1"] = dsc_n, dsh_n
        dmod8[i] = dict(g2=dg2)
        dh2, dhid, dout = _mlp_bwd(cfg, f"l{i}_mlpb", rows, dbr, sv["am"], sv["w1m"], sv["w2m"])
        gfam_a = _wgrad(cfg, f"l{i}_gw1", rows, sv["hm"], dhid, "col", gfam_a, i)
        gfam_a = _wgrad(cfg, f"l{i}_gw2", rows, sv["am"], dout, "row", gfam_a, DEPTH + i)
        dprev1, dbr1, dgn, dbs, dg1, dsc2, dsh2 = _lnb(
            cfg, f"l{i}_lnb1", rows, dprev, True, dh2, sv["r1"], sv["mix"], jnp.concatenate([sv["g0"], sv["b0"]], 0),
            m8, G1, m8, SC2)
        if rows[0] != 0:
            dg1, dsc2, dsh2 = (_x_only(t) for t in (dg1, dsc2, dsh2))
        g_ln_gain[i][0], g_ln_bias[i][0] = dgn[0], dbs[0]
        dmod8[i].update(g1=dg1, sc2=dsc2, sh2=dsh2)
        j = sv["j"]
        if sv["kind"] == "s5":
            p = s5p[j]
            dyy, dzz, dbglu = _glu_bwd(cfg, f"l{i}_glub", rows, dbr1, sv["zz"], sv["wg"], sv["yy"])
            gfam_b = _wgrad(cfg, f"l{i}_gwg", rows, sv["zb"], dzz, "col", gfam_b, j)
            du, (da, dbf, dcf) = _s5_backward(cfg, f"l{i}", dyy, rows[0] == 0, sv["tok"], m8, SH1, SC1, p["bf"], p["cf"],
                                              p["acoef"], p["acoef_adj"], sv["ck"])
            dh, dds = _s5_du(cfg, f"l{i}_du", cfg.rows(True), du, dyy, rows[0] // cfg.TM, sv["tok"], m8, (SH1, SC1),
                             s5_d[j][None])
            g_s5[j] = dict(da=da, dbf=dbf, dcf=dcf, dd=dds[0], dbglu=dbglu[0])
            dres = dprev1
        else:
            dcv, dmb, sums = _pw2_bwd(cfg, f"l{i}_pw2b", rows, dbr1, sv["cvv"], sv["w2c"], cvg_f[j][None], cvb_f[j][None])
            gfam_c = _wgrad(cfg, f"l{i}_gwp2", rows, sv["sb"], dmb, "row", gfam_c, j)
            dag, dwdw = _dwconv_bwd(cfg, f"l{i}_dwb", rows, dcv, sv["ag"], w_dw_f[j])
            dh, daa, dbpw1 = _glu_bwd(cfg, f"l{i}_pw1b", rows, dag, sv["aa"], sv["w1c"])
            gfam_b = _wgrad(cfg, f"l{i}_gwp1", rows, sv["hb"], daa, "col", gfam_b, ns5 + j)
            g_cv[j] = dict(ln_g=sums[0], ln_b=sums[1], b_dw=sums[2], b_pw2=sums[3], w_dw=dwdw, b_pw1=dbpw1[0])
            dres = dprev1
    gx_perm, dsc0, dsh0 = _input_bwd(cfg, dres, dh, tok0, mod8[0], SC1)
    dmod8[0]["sc1"], dmod8[0]["sh1"] = dsc0, dsh0
    grad_x = _from_perm(gx_perm[Tc:], B, cfg.L)

    zero28 = jnp.zeros((2, 8, D), F32)
    dm8 = jnp.stack([jnp.concatenate([dmod8[i].get(k, zero28) for k in ("sh1", "sc1", "g1", "sh2", "sc2", "g2")], axis=-1)
                     for i in range(DEPTH)])
    dm_rows = _dmod_rows(dm8)
    dm_tab = jnp.zeros((DEPTH, 24, 6 * D), F32)
    dm_tab = lax.dynamic_update_slice_in_dim(dm_tab, dm_rows[:, 0:B], B * dev, axis=1)
    dm_tab = lax.dynamic_update_slice_in_dim(dm_tab, dm_rows[:, 2:3], 16, axis=1)

    eye_parts = []
    for j in range(ns5):
        g = g_s5[j]
        dbbr, dbbi = _diag_b(cfg, g["dbf"])
        dcr, dci = _diag_c(cfg, g["dcf"])
        eye_parts += [g["da"], dbbr, dbbi, dcr, dci, g["dd"], g["dbglu"]]
    for j in range(nconv):
        g = g_cv[j]
        eye_parts += [g["ln_g"], g["ln_b"], g["b_dw"], g["b_pw2"], g["w_dw"], g["b_pw1"]]
    eye_parts += [jnp.stack([jnp.stack(r) for r in g_ln_gain]), jnp.stack([jnp.stack(r) for r in g_ln_bias]), dm_tab]
    buf, meta = _pack(eye_parts)
    red = _unpack(_allreduce8("small", buf), meta)

    grads = {}
    k = 0
    lam_re_g, lam_im_g, ldt_g, bre_g, bim_g, cre_g, cim_g, dd_g, bglu_g = ([] for _ in range(9))
    for j in range(ns5):
        da, dbbr, dbbi, dcr, dci, dd, dbglu = red[k:k + 7]
        k += 7
        da_s = _sublane_sum(f"s5_dasum_{j}", da.reshape(4, 8, cfg.NS)).reshape(2, 2, cfg.NS)
        g_abr = da_s[:, 0].reshape(2, cfg.G, cfg.P).transpose(2, 0, 1).reshape(cfg.P, 2 * cfg.G)
        g_abi = da_s[:, 1].reshape(2, cfg.G, cfg.P).transpose(2, 0, 1).reshape(cfg.P, 2 * cfg.G)
        glr, gli, gldt, gbr, gbi = _disc_bwd(*s5p[j]["lay"], g_abr, g_abi, dbbr, dbbi)
        lam_re_g.append(glr.reshape(cfg.P, 2, cfg.G).transpose(1, 2, 0))
        lam_im_g.append(gli.reshape(cfg.P, 2, cfg.G).transpose(1, 2, 0))
        ldt_g.append(gldt.reshape(2, cfg.G))
        bre_g.append(gbr.reshape(S5_GROUP, cfg.P, 2, cfg.G).transpose(2, 3, 1, 0))
        bim_g.append(gbi.reshape(S5_GROUP, cfg.P, 2, cfg.G).transpose(2, 3, 1, 0))
        cre_g.append(dcr)
        cim_g.append(dci)
        dd_g.append(dd)
        bglu_g.append(dbglu)
    grads.update(s5_lam_re=jnp.stack(lam_re_g), s5_lam_im=jnp.stack(lam_im_g), s5_log_dt=jnp.stack(ldt_g),
                 s5_b_re=jnp.stack(bre_g), s5_b_im=jnp.stack(bim_g), s5_c_re=jnp.stack(cre_g), s5_c_im=jnp.stack(cim_g),
                 s5_d=jnp.stack(dd_g), s5_b_glu=jnp.stack(bglu_g))

    def my_cols(full, width):
        return lax.dynamic_slice_in_dim(full, shard * width, width, axis=full.ndim - 1)

    cvs = {n: [] for n in ("ln_g", "ln_b", "b_dw", "b_pw2", "w_dw", "b_pw1")}
    for j in range(nconv):
        for n, val in zip(("ln_g", "ln_b", "b_dw", "b_pw2", "w_dw", "b_pw1"), red[k:k + 6]):
            cvs[n].append(val)
        k += 6
    grads.update(cv_ln_g=my_cols(jnp.stack(cvs["ln_g"]), Ds), cv_ln_b=my_cols(jnp.stack(cvs["ln_b"]), Ds),
                 cv_b_dw=my_cols(jnp.stack(cvs["b_dw"]), Ds), cv_b_pw2=my_cols(jnp.stack(cvs["b_pw2"]), Ds),
                 cv_w_dw=my_cols(jnp.stack(cvs["w_dw"]), Ds), cv_b_pw1=my_cols(jnp.stack(cvs["b_pw1"]), 2 * D // 4))
    grads.update(ln_gain=my_cols(red[k], Ds), ln_bias=my_cols(red[k + 1], Ds))
    dm_all = red[k + 2]

    dm_sh = lax.dynamic_slice_in_dim(dm_all, shard * Wa, Wa, axis=2)
    gw_ada, dcond = _ada_bwd(c_all, dm_sh, w_ada)
    grads["w_ada"] = gw_ada
    grads["b_ada"] = _colsum_groups("ada_bsum", dm_all)
    dc_part = dcond[0:1]
    dc_buf = jnp.concatenate([jnp.where(ac == 0, dc_part, 0.0), jnp.zeros((7, D), F32)], axis=0)
    dc_tot = _allreduce8("cctx", dc_buf.reshape(8 * D // LANES, LANES)).reshape(8, D)[0:1]
    grads["c_ctx"] = _ew("cctx_grad", lambda g, cv: g * (_sigmoid(cv) * (1.0 + cv * (1.0 - _sigmoid(cv)))),
                         [jnp.broadcast_to(dc_tot, (8, D)), jnp.broadcast_to(c_ctx[None], (8, D))], [_sds((8, D))])[0][0]

    ra, rb, rc = _reduce_scatter("gw", [gfam_a, gfam_b, gfam_c])
    grads.update(mlp_w1=ra[:DEPTH], mlp_w2=ra[DEPTH:], s5_w_glu=rb[:ns5], cv_w_pw1=rb[ns5:], cv_w_pw2=rc)

    weights = dict(c_ctx=c_ctx, w_ada=w_ada, b_ada=b_ada, ln_gain=ln_gain, ln_bias=ln_bias, s5_lam_re=s5_lam_re,
                   s5_lam_im=s5_lam_im, s5_log_dt=s5_log_dt, s5_b_re=s5_b_re, s5_b_im=s5_b_im, s5_c_re=s5_c_re,
                   s5_c_im=s5_c_im, s5_d=s5_d, s5_w_glu=s5_w_glu, s5_b_glu=s5_b_glu, cv_w_pw1=cv_w_pw1, cv_b_pw1=cv_b_pw1,
                   cv_w_dw=cv_w_dw, cv_b_dw=cv_b_dw, cv_ln_g=cv_ln_g, cv_ln_b=cv_ln_b, cv_w_pw2=cv_w_pw2, cv_b_pw2=cv_b_pw2,
                   mlp_w1=mlp_w1, mlp_w2=mlp_w2)
    ms = dict(c_ctx=m_c_ctx, w_ada=m_w_ada, b_ada=m_b_ada, ln_gain=m_ln_gain, ln_bias=m_ln_bias, s5_lam_re=m_s5_lam_re,
              s5_lam_im=m_s5_lam_im, s5_log_dt=m_s5_log_dt, s5_b_re=m_s5_b_re, s5_b_im=m_s5_b_im, s5_c_re=m_s5_c_re,
              s5_c_im=m_s5_c_im, s5_d=m_s5_d, s5_w_glu=m_s5_w_glu, s5_b_glu=m_s5_b_glu, cv_w_pw1=m_cv_w_pw1,
              cv_b_pw1=m_cv_b_pw1, cv_w_dw=m_cv_w_dw, cv_b_dw=m_cv_b_dw, cv_ln_g=m_cv_ln_g, cv_ln_b=m_cv_ln_b,
              cv_w_pw2=m_cv_w_pw2, cv_b_pw2=m_cv_b_pw2, mlp_w1=m_mlp_w1, mlp_w2=m_mlp_w2)
    vs = dict(c_ctx=v_c_ctx, w_ada=v_w_ada, b_ada=v_b_ada, ln_gain=v_ln_gain, ln_bias=v_ln_bias, s5_lam_re=v_s5_lam_re,
              s5_lam_im=v_s5_lam_im, s5_log_dt=v_s5_log_dt, s5_b_re=v_s5_b_re, s5_b_im=v_s5_b_im, s5_c_re=v_s5_c_re,
              s5_c_im=v_s5_c_im, s5_d=v_s5_d, s5_w_glu=v_s5_w_glu, s5_b_glu=v_s5_b_glu, cv_w_pw1=v_cv_w_pw1,
              cv_b_pw1=v_cv_b_pw1, cv_w_dw=v_cv_w_dw, cv_b_dw=v_cv_b_dw, cv_ln_g=v_cv_ln_g, cv_ln_b=v_cv_ln_b,
              cv_w_pw2=v_cv_w_pw2, cv_b_pw2=v_cv_b_pw2, mlp_w1=v_mlp_w1, mlp_w2=v_mlp_w2)
    names = list(weights)
    deltas, new_m, new_v = {}, {}, {}
    for n in names:
        g = grads[n].reshape(weights[n].shape)
        grads[n] = g
        deltas[n], new_m[n], new_v[n] = _adamw("adamw_" + n, weights[n], g, ms[n], vs[n])
    return (loss, grad_x, *[grads[n] for n in names], *[deltas[n] for n in names], *[new_m[n] for n in names],
            *[new_v[n] for n in names])


def _sublane_sum(name, a):
    n, _, w = a.shape

    def body(a_ref, o_ref):
        for q in range(n):
            o_ref[q:q + 1, :] = jnp.sum(a_ref[q], axis=0, keepdims=True)

    return pl.pallas_call(body, name=name, out_shape=_sds((n, w)))(a)


def _colsum_groups(name, dm_all):
    nl, nr, w = dm_all.shape

    def body(d_ref, o_ref):
        o_ref[...] = jnp.zeros((8, w), F32) + jnp.sum(d_ref[...], axis=0, keepdims=True)

    out = pl.pallas_call(body, name=name, grid=(nl,), in_specs=[pl.BlockSpec((None, nr, w), lambda l: (l, 0, 0))],
                         out_specs=pl.BlockSpec((None, 8, w), lambda l: (l, 0, 0)), out_shape=_sds((nl, 8, w)),
                         compiler_params=_cp(1))(dm_all)
    return out[:, 0]
```

```python
import functools
import math

import jax
import jax.numpy as jnp
from jax import lax
from jax.experimental import pallas as pl
from jax.experimental.pallas import tpu as pltpu

F32 = jnp.float32
BF16 = jnp.bfloat16
MESH = pl.DeviceIdType.MESH
ANY = pl.BlockSpec(memory_space=pl.ANY)

DEPTH = 4
S5_GROUP = 16
S5_STATE = 64
GRID_W = 64
POS_TEMP = 10000.0
LAMBDA_RE_MAX = -1e-4
LN_EPS = 1e-5
DN_ALPHA = (2.0 * DEPTH) ** 0.25
ADAM_LR, ADAM_B1, ADAM_B2, ADAM_EPS, ADAM_WD, ADAM_STEP = 0.001, 0.9, 0.999, 1e-08, 0.01, 10

SUBLANES = 8
LANES = 128
OCT_CH = 128
OCT_ST = 512
CONV_ROWS = 64
VMEM_LIMIT = 56 * 1024 * 1024


def _cp(n_axes):
    return pltpu.CompilerParams(dimension_semantics=("arbitrary",) * n_axes, vmem_limit_bytes=VMEM_LIMIT)


def _full(shape, single=False):
    nd = len(shape)
    if single:
        return pl.BlockSpec(shape, lambda *i: (0,) * nd, pipeline_mode=pl.Buffered(1))
    return pl.BlockSpec(shape, lambda *i: (0,) * nd)


def _sds(shape, dtype=F32):
    return jax.ShapeDtypeStruct(tuple(shape), dtype)


def _mod(x, sh8, sc8):
    r, d = x.shape
    return (x.reshape(r // 8, 8, d) * (1.0 + sc8[None]) + sh8[None]).reshape(r, d)


def _rowscale(x, g8):
    r, d = x.shape
    return (x.reshape(r // 8, 8, d) * g8[None]).reshape(r, d)


def _sum8(x):
    r, w = x.shape
    return jnp.sum(x.reshape(r // 8, 8, w), axis=0)


def _ln_stats(r):
    mu = jnp.mean(r, axis=-1, keepdims=True)
    xc = r - mu
    var = jnp.mean(xc * xc, axis=-1, keepdims=True)
    rstd = lax.rsqrt(var + LN_EPS)
    return xc * rstd, rstd


def _ln_bwd(dxh, xhat, rstd):
    m1 = jnp.mean(dxh, axis=-1, keepdims=True)
    m2 = jnp.mean(dxh * xhat, axis=-1, keepdims=True)
    return rstd * (dxh - m1 - xhat * m2)


def _sigmoid(x):
    return 1.0 / (1.0 + jnp.exp(-x))


def _gelu(y):
    return 0.5 * y * (1.0 + lax.erf(y * (1.0 / math.sqrt(2.0))))


def _gelu_grad(y):
    return 0.5 * (1.0 + lax.erf(y * (1.0 / math.sqrt(2.0)))) + y * jnp.exp(-0.5 * y * y) * (1.0 / math.sqrt(2.0 * math.pi))


def _dot(a, b):
    return jnp.dot(a, b, preferred_element_type=F32)


def _dot_nt(a, b):
    return lax.dot_general(a, b, (((1,), (1,)), ((), ())), preferred_element_type=F32)


def _dot_tn(a, b):
    return lax.dot_general(a, b, (((0,), (0,)), ((), ())), preferred_element_type=F32)


class _Cfg:
    def __init__(self, x, ctx, mlp_w1, cv_w_dw):
        self.B, self.L, self.D = x.shape
        self.Lc = ctx.shape[1]
        assert self.B * 4 == SUBLANES, "two examples per device, four chunks each"
        self.F = mlp_w1.shape[2] * 4
        self.KW = cv_w_dw.shape[1]
        self.half = self.KW // 2
        self.G = self.D // S5_GROUP
        self.P = S5_STATE
        self.NS = self.G * self.P
        self.NO = self.D // OCT_CH
        assert self.NO % 2 == 0
        self.nx = self.L // 4
        self.nc = self.Lc // 4
        self.Tc = self.B * self.Lc
        self.Tx = self.B * self.L
        self.T = self.Tc + self.Tx
        self.TM = 256 if self.Tc % 256 == 0 else self.Tc
        assert self.Tx % self.TM == 0 and self.TM % 16 == 0
        self.HB = self.TM // 2
        assert SUBLANES * self.half <= self.HB
        self.TW = 512 if (self.Tc % 512 == 0 and self.Tx % 512 == 0) else self.TM

    def ti(self, n):
        t = 32 if self.nc % 32 == 0 else self.nc
        assert n % t == 0 and self.Tc % (8 * t) == 0
        return t

    def rows(self, ctx_too):
        return (0, self.T) if ctx_too else (self.Tc, self.Tx)


def _allgather8(name, x_shard):
    m_per, n = x_shard.shape
    assert m_per % 8 == 0

    def body(x_ref, out_ref, send_sems, recv_sems, local_sem):
        x, y, c = lax.axis_index("x"), lax.axis_index("y"), lax.axis_index("c")
        me, sibling = (x, y, c), (x, y, 1 - c)
        chips = [(1 - x, y), (x, 1 - y), (1 - x, 1 - y)]

        def rows(px, py, pc):
            return out_ref.at[pl.ds((4 * px + 2 * py + pc) * m_per, m_per), :]

        def copy(k, block, to, src=None):
            return pltpu.make_async_remote_copy(
                src_ref=rows(*block) if src is None else src, dst_ref=rows(*block),
                send_sem=send_sems.at[k], recv_sem=recv_sems.at[k], device_id=to, device_id_type=MESH)

        mine = pltpu.make_async_copy(x_ref, rows(*me), local_sem)
        mine.start()
        first = [copy(0, me, sibling, src=x_ref)]
        first += [copy(1 + j, me, (*chip, c), src=x_ref) for j, chip in enumerate(chips)]
        for cp in first:
            cp.start()
        passed = [copy(4 + j, (*chip, c), sibling) for j, chip in enumerate(chips)]
        for j, chip in enumerate(chips):
            copy(1 + j, (*chip, c), me).wait_recv()
            passed[j].start()
        copy(0, sibling, me).wait_recv()
        for j, chip in enumerate(chips):
            copy(4 + j, (*chip, 1 - c), me).wait_recv()
        for cp in first + passed:
            cp.wait_send()
        mine.wait()

    return pl.pallas_call(
        body, name=name, out_shape=_sds((8 * m_per, n), x_shard.dtype),
        in_specs=[pl.BlockSpec(memory_space=pltpu.VMEM)], out_specs=pl.BlockSpec(memory_space=pltpu.VMEM),
        scratch_shapes=[pltpu.SemaphoreType.DMA((7,)), pltpu.SemaphoreType.DMA((7,)), pltpu.SemaphoreType.DMA],
        compiler_params=pltpu.CompilerParams(vmem_limit_bytes=VMEM_LIMIT),
    )(x_shard)


def _flip(v, m):
    return v + m - 2 * v * m


def _peer(axis):
    x, y, c = lax.axis_index("x"), lax.axis_index("y"), lax.axis_index("c")
    if axis == "c":
        return (x, y, 1 - c)
    if axis == "xy":
        return (_flip(x, 1 - c), _flip(y, c), c)
    if axis == "yx":
        return (_flip(x, c), _flip(y, 1 - c), c)
    raise ValueError(axis)


def _pair_exchange(name, axis, inputs, out_shapes, aliases, plan):
    n_in = len(inputs)
    n_out = len(out_shapes)

    def body(*refs):
        ins, outs = refs[:n_in], refs[n_in:n_in + n_out]
        send_sems, recv_sems, local_sems = refs[n_in + n_out:]
        x, y, c = lax.axis_index("x"), lax.axis_index("y"), lax.axis_index("c")
        remote, local = plan(x, y, c, ins, outs)
        lcs = [pltpu.make_async_copy(s, d, local_sems.at[k]) for k, (s, d) in enumerate(local)]
        for cp in lcs:
            cp.start()
        rcs = [pltpu.make_async_remote_copy(src_ref=s, dst_ref=d, send_sem=send_sems.at[k], recv_sem=recv_sems.at[k],
                                            device_id=_peer(axis), device_id_type=MESH) for k, (s, d) in enumerate(remote)]
        for cp in rcs:
            cp.start()
        for cp in rcs:
            cp.wait()
        for cp in lcs:
            cp.wait()

    n_remote, n_local = plan.counts
    return pl.pallas_call(
        body, name=name, out_shape=tuple(out_shapes),
        in_specs=[ANY] * n_in, out_specs=tuple([ANY] * n_out),
        input_output_aliases=dict(aliases),
        scratch_shapes=[pltpu.SemaphoreType.DMA((n_remote,)), pltpu.SemaphoreType.DMA((n_remote,)),
                        pltpu.SemaphoreType.DMA((max(n_local, 1),))],
    )(*inputs)


def _plan(n_remote, n_local=0):
    def deco(fn):
        fn.counts = (n_remote, n_local)
        return fn
    return deco


HBM = pl.BlockSpec(memory_space=pltpu.HBM)
SEM = pl.BlockSpec(memory_space=pltpu.SEMAPHORE)


def _split_start(name, axis, bufs, plan):
    nb = len(bufs)
    n = plan.counts[0]

    def body(*refs):
        ins, send_sem, recv_sem, token = refs[:nb], refs[nb], refs[nb + 1], refs[-1]
        x, y, c = lax.axis_index("x"), lax.axis_index("y"), lax.axis_index("c")
        for k, (s, d) in enumerate(plan(x, y, c, ins)):
            pltpu.make_async_remote_copy(src_ref=s, dst_ref=d, send_sem=send_sem.at[k], recv_sem=recv_sem.at[k],
                                         device_id=_peer(axis), device_id_type=MESH).start()
        token[...] = jnp.zeros_like(token)

    outs = pl.pallas_call(
        body, name=name,
        out_shape=(pltpu.SemaphoreType.DMA((n,)), pltpu.SemaphoreType.DMA((n,)),
                   *[pltpu.HBM(b.shape, b.dtype) for b in bufs], _sds((8, LANES))),
        in_specs=[HBM] * nb, out_specs=(SEM, SEM, *([HBM] * nb), pl.BlockSpec(memory_space=pltpu.VMEM)),
        input_output_aliases={i: 2 + i for i in range(nb)},
        compiler_params=pltpu.CompilerParams(has_side_effects=pltpu.SideEffectType.DATAFLOW_SIDE_EFFECTING),
    )(*[pltpu.with_memory_space_constraint(b, pltpu.HBM) for b in bufs])
    return dict(name=name, axis=axis, plan=plan, send=outs[0], recv=outs[1], bufs=list(outs[2:2 + nb]), token=outs[-1])


def _split_wait(h, after):
    bufs, plan, axis = h["bufs"], h["plan"], h["axis"]
    nb = len(bufs)

    def body(*refs):
        ins, send_sem, recv_sem = refs[:nb], refs[nb], refs[nb + 1]
        x, y, c = lax.axis_index("x"), lax.axis_index("y"), lax.axis_index("c")
        for k, (s, d) in enumerate(plan(x, y, c, ins)):
            cp = pltpu.make_async_remote_copy(src_ref=s, dst_ref=d, send_sem=send_sem.at[k], recv_sem=recv_sem.at[k],
                                              device_id=_peer(axis), device_id_type=MESH)
            cp.wait_send()
            cp.wait_recv()

    outs = pl.pallas_call(
        body, name=h["name"] + "_wait", out_shape=tuple(pltpu.HBM(b.shape, b.dtype) for b in bufs),
        in_specs=[HBM] * nb + [SEM, SEM, ANY], out_specs=tuple([HBM] * nb),
        input_output_aliases={i: i for i in range(nb)},
        compiler_params=pltpu.CompilerParams(has_side_effects=pltpu.SideEffectType.DATAFLOW_SIDE_EFFECTING),
    )(*bufs, h["send"], h["recv"], after)
    return list(outs)


def _tie(arr, tokens):
    if not tokens:
        return arr
    return lax.optimization_barrier((arr, *tokens))[0]


class _Overlap:
    def __init__(self):
        self.live = {}
        self.done = {}

    def add(self, key, gen):
        self.live[key] = gen
        return [next(gen)]

    def point(self, arr):
        tokens = []
        for key in list(self.live):
            try:
                tokens.append(self.live[key].send(arr))
            except StopIteration as e:
                self.done[key] = e.value
                del self.live[key]
        return _tie(arr, tokens)

    def finish(self, key, arr):
        while key in self.live:
            arr = self.point(arr)
        return self.done.pop(key)


def _gather_gen(tag, fams):
    nf = len(fams)
    shapes = [f.shape for f in fams]
    views = [f.reshape(4, 2, -1, f.shape[-1]) for f in fams]

    @_plan(nf)
    def plan1(x, y, c, refs):
        s = 2 * x + y
        return [(refs[k].at[s, c], refs[k].at[s, c]) for k in range(nf)]

    @_plan(2 * nf)
    def plan2(x, y, c, refs):
        shards = [2 * x + y, 2 * _flip(x, 1 - c) + _flip(y, c)]
        return [(refs[k].at[s, c], refs[k].at[s, c]) for k in range(nf) for s in shards]

    @_plan(3 * nf)
    def plan3(x, y, c, refs):
        shards = [2 * (1 - x) + y, 2 * x + (1 - y), 2 * (1 - x) + (1 - y)]
        return [(refs[k].at[s, c], refs[k].at[s, c]) for k in range(nf) for s in shards]

    for rnd, (axis, plan) in enumerate((("xy", plan1), ("yx", plan2), ("c", plan3))):
        h = _split_start(f"{tag}_g{rnd}", axis, views, plan)
        after = yield h["token"]
        views = _split_wait(h, after)
    return [v.reshape(sh) for v, sh in zip(views, shapes)]


def _reduce_scatter_gen(tag, grads):
    ng = len(grads)
    flat = [g.reshape(4, 2, -1, g.shape[-1]) for g in grads]

    def empty(shape, dtype):
        return lax.empty(tuple(shape), dtype)

    @_plan(ng)
    def plan1(x, y, c, refs):
        return [(refs[k].at[:, 1 - c], refs[ng + k]) for k in range(ng)]

    h = _split_start(tag + "_r0", "c", flat + [empty((4,) + f.shape[2:], F32) for f in flat], plan1)
    after = yield h["token"]
    bufs = _split_wait(h, after)
    p1 = [_sel_add(f"{tag}_add1_{k}", bufs[k], lambda j, sc: (j, sc[2]), bufs[ng + k], True) for k in range(ng)]

    def sent1(kk, x, y, c):
        return ((1 - c) * kk + c * (1 - x), (1 - c) * (1 - y) + c * kk)

    def kept1(j, sc):
        x, y, c = sc[0], sc[1], sc[2]
        return ((1 - c) * j + c * x, (1 - c) * y + c * j)

    @_plan(2 * ng)
    def plan2(x, y, c, refs):
        return [(refs[k].at[sent1(kk, x, y, c)], refs[ng + k].at[kk]) for k in range(ng) for kk in range(2)]

    v1 = [pb.reshape(2, 2, pb.shape[1], pb.shape[2]) for p, pb in p1]
    h = _split_start(tag + "_r1", "yx", v1 + [empty((2,) + v.shape[2:], BF16) for v in v1], plan2)
    after = yield h["token"]
    bufs = _split_wait(h, after)
    p2 = [_sel_add(f"{tag}_add2_{k}", p1[k][0].reshape(2, 2, p1[k][0].shape[1], p1[k][0].shape[2]), kept1, bufs[ng + k], True)
          for k in range(ng)]

    @_plan(ng)
    def plan3(x, y, c, refs):
        return [(refs[k].at[(1 - c) * (1 - x) + c * (1 - y)], refs[ng + k]) for k in range(ng)]

    h = _split_start(tag + "_r2", "xy", [qb for q, qb in p2] + [empty(qb.shape[1:], BF16) for q, qb in p2], plan3)
    after = yield h["token"]
    bufs = _split_wait(h, after)
    fin = [_sel_add(f"{tag}_add3_{k}", p2[k][0], lambda j, sc: ((1 - sc[2]) * sc[0] + sc[2] * sc[1],), bufs[ng + k][None],
                    False, out_slots=(2, lambda j, sc: sc[2]))[0] for k in range(ng)]

    @_plan(ng)
    def plan4(x, y, c, refs):
        return [(refs[k].at[c], refs[k].at[c]) for k in range(ng)]

    h = _split_start(tag + "_r3", "c", fin, plan4)
    after = yield h["token"]
    full = _split_wait(h, after)
    return [full[k].reshape(grads[k].shape[1:]) for k in range(ng)]


def _xyc():
    return jnp.stack([lax.axis_index("x"), lax.axis_index("y"), lax.axis_index("c")]).astype(jnp.int32)


def _place_shard(name, w, fam, slot0, n_slots):
    n, kk, nn = w.shape
    kt = 256 if kk % 256 == 0 else kk

    def body(scal, w_ref, *rest):
        rest[-1][...] = w_ref[...].astype(BF16)

    in_specs = [pl.BlockSpec((None, kt, nn), lambda t, i, sc: (t, i, 0))]
    args = [_xyc(), w]
    aliases = {}
    if fam is not None:
        in_specs.append(ANY)
        args.append(fam)
        aliases = {2: 0}
    gs = pltpu.PrefetchScalarGridSpec(
        num_scalar_prefetch=1, grid=(n, kk // kt), in_specs=in_specs,
        out_specs=pl.BlockSpec((None, None, kt, nn), lambda t, i, sc: (2 * sc[0] + sc[1], slot0 + t, i, 0)))
    return pl.pallas_call(body, name=name, grid_spec=gs, out_shape=_sds((4, n_slots, kk, nn), BF16),
                          input_output_aliases=aliases, compiler_params=_cp(2))(*args)


def _gather_weights(fams):
    nf = len(fams)
    shapes = [f.shape for f in fams]
    views = [f.reshape(4, 2, -1, f.shape[-1]) for f in fams]
    outs = [_sds(v.shape, v.dtype) for v in views]
    alias = {k: k for k in range(nf)}

    @_plan(nf)
    def plan1(x, y, c, ins, outs_):
        s = 2 * x + y
        return ([(ins[k].at[s, c], outs_[k].at[s, c]) for k in range(nf)], [])

    views = _pair_exchange("gatherw_1", "xy", list(views), outs, alias, plan1)

    @_plan(2 * nf)
    def plan2(x, y, c, ins, outs_):
        shards = [2 * x + y, 2 * _flip(x, 1 - c) + _flip(y, c)]
        return ([(ins[k].at[s, c], outs_[k].at[s, c]) for k in range(nf) for s in shards], [])

    views = _pair_exchange("gatherw_2", "yx", list(views), outs, alias, plan2)

    @_plan(3 * nf)
    def plan3(x, y, c, ins, outs_):
        shards = [2 * (1 - x) + y, 2 * x + (1 - y), 2 * (1 - x) + (1 - y)]
        return ([(ins[k].at[s, c], outs_[k].at[s, c]) for k in range(nf) for s in shards], [])

    views = _pair_exchange("gatherw_c", "c", list(views), outs, alias, plan3)
    return [v.reshape(sh) for v, sh in zip(views, shapes)]


def _sel_add(name, a, a_sel, r, emit_bf16, out_slots=None):
    nr, rows, w = r.shape
    tr = 256 if rows % 256 == 0 else rows

    def body(scal, a_ref, r_ref, *outs):
        s = a_ref[...] + r_ref[...].astype(F32)
        outs[0][...] = s
        if emit_bf16:
            outs[1][...] = s.astype(BF16)

    lead = a.ndim - 2
    a_block = (None,) * lead + (tr, w)
    n_out, o_fn = out_slots if out_slots is not None else (nr, lambda j, sc: j)
    out_shape = [_sds((n_out, rows, w), F32)] + ([_sds((nr, rows, w), BF16)] if emit_bf16 else [])
    out_specs = [pl.BlockSpec((None, tr, w), lambda j, t, sc: (o_fn(j, sc), t, 0))]
    if emit_bf16:
        out_specs.append(pl.BlockSpec((None, tr, w), lambda j, t, sc: (j, t, 0)))
    gs = pltpu.PrefetchScalarGridSpec(
        num_scalar_prefetch=1, grid=(nr, rows // tr),
        in_specs=[pl.BlockSpec(a_block, lambda j, t, sc: tuple(a_sel(j, sc)) + (t, 0)),
                  pl.BlockSpec((None, tr, w), lambda j, t, sc: (j, t, 0))],
        out_specs=out_specs)
    return pl.pallas_call(body, name=name, grid_spec=gs, out_shape=out_shape, compiler_params=_cp(2))(_xyc(), a, r)


def _reduce_scatter(tag, grads):
    ng = len(grads)
    flat = [g.reshape(4, 2, -1, g.shape[-1]) for g in grads]

    @_plan(ng)
    def plan1(x, y, c, ins, outs_):
        return ([(ins[k].at[:, 1 - c], outs_[k]) for k in range(ng)], [])

    r1 = _pair_exchange(tag + "_rs_c", "c", flat, [_sds((4,) + f.shape[2:], F32) for f in flat], {}, plan1)
    p1 = [_sel_add(f"{tag}_add1_{k}", flat[k], lambda j, sc: (j, sc[2]), r1[k], True) for k in range(ng)]

    def sent1(kk, x, y, c):
        return ((1 - c) * kk + c * (1 - x), (1 - c) * (1 - y) + c * kk)

    def kept1(j, sc):
        x, y, c = sc[0], sc[1], sc[2]
        return ((1 - c) * j + c * x, (1 - c) * y + c * j)

    @_plan(2 * ng)
    def plan2(x, y, c, ins, outs_):
        return ([(ins[k].at[sent1(kk, x, y, c)], outs_[k].at[kk]) for k in range(ng) for kk in range(2)], [])

    v1 = [pb.reshape(2, 2, pb.shape[1], pb.shape[2]) for p, pb in p1]
    r2 = _pair_exchange(tag + "_rs_1", "yx", v1, [_sds((2,) + v.shape[2:], BF16) for v in v1], {}, plan2)
    p2 = [_sel_add(f"{tag}_add2_{k}", p1[k][0].reshape(2, 2, p1[k][0].shape[1], p1[k][0].shape[2]), kept1, r2[k], True)
          for k in range(ng)]

    @_plan(ng)
    def plan3(x, y, c, ins, outs_):
        return ([(ins[k].at[(1 - c) * (1 - x) + c * (1 - y)], outs_[k]) for k in range(ng)], [])

    r3 = _pair_exchange(tag + "_rs_2", "xy", [qb for q, qb in p2], [_sds(qb.shape[1:], BF16) for q, qb in p2], {}, plan3)
    fin = [_sel_add(f"{tag}_add3_{k}", p2[k][0], lambda j, sc: ((1 - sc[2]) * sc[0] + sc[2] * sc[1],), r3[k][None], False,
                    out_slots=(2, lambda j, sc: sc[2]))[0] for k in range(ng)]

    @_plan(ng)
    def plan4(x, y, c, ins, outs_):
        return ([(ins[k].at[c], outs_[k].at[c]) for k in range(ng)], [])

    full = _pair_exchange(tag + "_rs_c2", "c", fin, [_sds(f.shape, F32) for f in fin], {k: k for k in range(ng)}, plan4)
    return [full[k].reshape(grads[k].shape[1:]) for k in range(ng)]


def _allreduce8(tag, buf):
    rows, w = buf.shape
    assert rows % 16 == 0
    one = lambda: _plan(1)(lambda x, y, c, ins, outs_: ([(ins[0], outs_[0])], []))
    (got,) = _pair_exchange(f"{tag}_ar_c", "c", [buf], [_sds(buf.shape, F32)], {}, one())
    cur = _ew(f"{tag}_aradd_c", lambda a, b: a + b, [buf, got], [_sds(buf.shape, F32)])[0].reshape(2, rows // 2, w)
    mine = _plan(1)(lambda x, y, c, ins, outs_: ([(ins[0].at[c], outs_[0])], []))
    (got,) = _pair_exchange(f"{tag}_ar_1", "xy", [cur], [_sds(cur.shape[1:], F32)], {}, mine)
    (h1,) = _sel_add(f"{tag}_aradd_1", cur, lambda j, sc: (sc[2],), got[None], False)
    (got,) = _pair_exchange(f"{tag}_ar_2", "yx", [h1[0]], [_sds(h1.shape[1:], F32)], {}, one())
    (h2,) = _sel_add(f"{tag}_aradd_2", h1, lambda j, sc: (0,), got[None], False, out_slots=(2, lambda j, sc: sc[2]))
    swap = _plan(1)(lambda x, y, c, ins, outs_: ([(ins[0].at[c], outs_[0].at[c])], []))
    (full,) = _pair_exchange(f"{tag}_ar_c2", "c", [h2], [_sds(h2.shape, F32)], {0: 0}, swap)
    return full.reshape(rows, w)


def _ew(name, fn, ins, outs):
    rows, w = ins[0].shape
    tr = rows
    for cand in (512, 256, 128, 64, 32, 16, 8):
        if rows % cand == 0 and rows > cand and cand * w * 4 <= (1 << 20):
            tr = cand
            break
    n_in = len(ins)

    def body(*refs):
        vals = fn(*[r[...] for r in refs[:n_in]])
        if not isinstance(vals, (tuple, list)):
            vals = (vals,)
        for o, v in zip(refs[n_in:], vals):
            o[...] = v.astype(o.dtype)

    spec = pl.BlockSpec((tr, w), lambda i: (i, 0))
    return pl.pallas_call(body, name=name, grid=(rows // tr,), in_specs=[spec] * n_in,
                          out_specs=[spec] * len(outs), out_shape=list(outs), compiler_params=_cp(1))(*ins)


def _ew3(name, fn, ins, n_out):
    aa, bb, cc = ins[0].shape
    pad_bytes = (-(-bb // SUBLANES) * SUBLANES) * (-(-cc // LANES) * LANES) * 4
    ta = 1
    for cand in range(aa, 0, -1):
        if aa % cand == 0 and cand * pad_bytes <= (1 << 20):
            ta = cand
            break
    n_in = len(ins)

    def body(*refs):
        vals = fn(*[r[...] for r in refs[:n_in]])
        for o, v in zip(refs[n_in:], vals):
            o[...] = v

    spec = pl.BlockSpec((ta, bb, cc), lambda i: (i, 0, 0))
    return pl.pallas_call(body, name=name, grid=(aa // ta,), in_specs=[spec] * n_in, out_specs=[spec] * n_out,
                          out_shape=[_sds((aa, bb, cc))] * n_out, compiler_params=_cp(1))(*ins)


def _view_for_ew(a):
    if a.ndim == 1:
        return a.reshape(1, -1)
    if a.ndim == 2:
        return a
    if a.shape[-1] % LANES == 0 and a.shape[-2] % SUBLANES == 0:
        return a.reshape(-1, a.shape[-1])
    return a.reshape(-1, a.shape[-2], a.shape[-1])


def _adamw(name, w, g, m, v):
    def fn(w, g, m, v):
        m = ADAM_B1 * m + (1.0 - ADAM_B1) * g
        v = ADAM_B2 * v + (1.0 - ADAM_B2) * (g * g)
        m_hat = m / (1.0 - ADAM_B1 ** ADAM_STEP)
        v_hat = v / (1.0 - ADAM_B2 ** ADAM_STEP)
        delta = -ADAM_LR * (m_hat / (jnp.sqrt(v_hat) + ADAM_EPS) + ADAM_WD * w)
        return delta, m, v

    shp = w.shape
    a = [_view_for_ew(t) for t in (w, g, m, v)]
    if a[0].ndim == 3:
        o = _ew3(name, fn, a, 3)
    else:
        o = _ew(name, fn, a, [_sds(a[0].shape)] * 3)
    return tuple(t.reshape(shp) for t in o)


def _to_perm(a):
    b, ls, d = a.shape
    n = ls // 4
    return a.reshape(b * 4, n, d).swapaxes(0, 1).reshape(n * 8, d)


def _from_perm(p, b, ls):
    n = ls // 4
    return p.reshape(n, b * 4, p.shape[-1]).swapaxes(0, 1).reshape(b, ls, p.shape[-1])


def _pos_embed(rows, dim):
    def sincos(pos, d):
        quarter = d // 2
        omega = POS_TEMP ** (-jnp.arange(quarter, dtype=F32) / quarter)
        ang = pos[:, None] * omega[None, :]
        return jnp.concatenate([jnp.sin(ang), jnp.cos(ang)], axis=-1)

    row_idx = jnp.repeat(jnp.arange(rows), GRID_W).astype(F32)
    col_idx = jnp.tile(jnp.arange(GRID_W), rows).astype(F32)
    return jnp.concatenate([sincos(row_idx, dim // 2), sincos(col_idx, dim // 2)], axis=-1)


def _stream_of(cfg, off_tiles, tile_rows):
    nct = cfg.Tc // tile_rows
    return lambda i: jnp.where(i + off_tiles >= nct, 1, 0)


def _ada_fwd(c_all, w_ada, b_shard):
    nl, d, w = w_ada.shape
    tn = 512 if w % 512 == 0 else w

    def body(c_ref, w_ref, b_ref, o_ref):
        cv = c_ref[...]
        cond = (cv * _sigmoid(cv)).astype(BF16)
        o_ref[...] = _dot(cond, w_ref[...].astype(BF16)) + b_ref[...]

    return pl.pallas_call(
        body, name="ada_fwd", grid=(nl, w // tn),
        in_specs=[_full(c_all.shape), pl.BlockSpec((None, d, tn), lambda l, j: (l, 0, j)),
                  pl.BlockSpec((None, 1, tn), lambda l, j: (l, 0, j))],
        out_specs=pl.BlockSpec((None, c_all.shape[0], tn), lambda l, j: (l, 0, j)),
        out_shape=_sds((nl, c_all.shape[0], w)), compiler_params=_cp(2))(c_all, w_ada, b_shard)


def _ada_bwd(c_all, dmod_shard, w_ada):
    nl, d, w = w_ada.shape
    tn = 512 if w % 512 == 0 else w
    nr = c_all.shape[0]

    def body(c_ref, dm_ref, w_ref, gw_ref, dc_ref):
        j = pl.program_id(0) * (w // tn) + pl.program_id(1)
        cv = c_ref[...]
        cond = (cv * _sigmoid(cv)).astype(BF16)
        dm = dm_ref[...].astype(BF16)
        gw_ref[...] = _dot_tn(cond, dm)
        part = _dot_nt(dm[16:24], w_ref[...].astype(BF16))

        @pl.when(j == 0)
        def _():
            dc_ref[...] = part

        @pl.when(j > 0)
        def _():
            dc_ref[...] += part

    return pl.pallas_call(
        body, name="ada_bwd", grid=(nl, w // tn),
        in_specs=[_full(c_all.shape), pl.BlockSpec((None, nr, tn), lambda l, j: (l, 0, j)),
                  pl.BlockSpec((None, d, tn), lambda l, j: (l, 0, j))],
        out_specs=[pl.BlockSpec((None, d, tn), lambda l, j: (l, 0, j)), _full((8, d))],
        out_shape=[_sds((nl, d, w)), _sds((8, d))], compiler_params=_cp(2))(c_all, dmod_shard, w_ada)


def _disc(lr, li, ldt, br, bi):
    lr = jnp.minimum(lr, LAMBDA_RE_MAX)
    dt = jnp.exp(ldt)
    mag = jnp.exp(lr * dt)
    abr = mag * jnp.cos(li * dt)
    abi = mag * jnp.sin(li * dt)
    den = lr * lr + li * li
    nr = abr - 1.0
    ni = abi
    cr = (nr * lr + ni * li) / den
    ci = (ni * lr - nr * li) / den
    return abr, abi, cr[None] * br - ci[None] * bi, cr[None] * bi + ci[None] * br


def _disc_fwd(lr, li, ldt, br, bi):
    def body(a, b, c, d, e, o1, o2, o3, o4):
        r = _disc(a[...], b[...], c[...], d[...], e[...])
        o1[...], o2[...], o3[...], o4[...] = r

    return pl.pallas_call(body, name="s5_disc_fwd", out_shape=[_sds(lr.shape), _sds(lr.shape), _sds(br.shape), _sds(br.shape)])(
        lr, li, ldt, br, bi)


def _disc_bwd(lr, li, ldt, br, bi, g_abr, g_abi, g_bbr, g_bbi):
    def body(a, b, c, d, e, g1, g2, g3, g4, o1, o2, o3, o4, o5):
        _, vjp = jax.vjp(_disc, a[...], b[...], c[...], d[...], e[...])
        r = vjp((g1[...], g2[...], g3[...], g4[...]))
        o1[...], o2[...], o3[...], o4[...], o5[...] = r

    return pl.pallas_call(
        body, name="s5_disc_bwd",
        out_shape=[_sds(lr.shape), _sds(li.shape), _sds(ldt.shape), _sds(br.shape), _sds(bi.shape)])(
        lr, li, ldt, br, bi, g_abr, g_abi, g_bbr, g_bbi)


def _s5_layouts(cfg, lam_re, lam_im, log_dt, b_re, b_im):
    P, G = cfg.P, cfg.G
    lr = lam_re.transpose(2, 0, 1).reshape(P, 2 * G)
    li = lam_im.transpose(2, 0, 1).reshape(P, 2 * G)
    ldt = log_dt.reshape(1, 2 * G)
    br = b_re.transpose(3, 2, 0, 1).reshape(S5_GROUP, P, 2 * G)
    bi = b_im.transpose(3, 2, 0, 1).reshape(S5_GROUP, P, 2 * G)
    return lr, li, ldt, br, bi


def _coef_rows(cfg, abr, abi, conj):
    def one(t):
        return t.reshape(cfg.P, 2, cfg.G).transpose(1, 2, 0).reshape(2, cfg.NS)
    a = jnp.stack([one(abr), -one(abi) if conj else one(abi)], axis=1)
    return jnp.broadcast_to(a[:, :, None, :], (2, 2, SUBLANES, cfg.NS))


def _blockdiag_b(cfg, bbr, bbi):
    eye = jnp.eye(8, dtype=F32)

    def one(t):
        t = t.reshape(S5_GROUP, cfg.P, 2, cfg.G).transpose(2, 3, 0, 1)
        t = t.reshape(2, cfg.NO, 8, S5_GROUP, cfg.P)
        return jnp.einsum("dogcp,gh->dogchp", t, eye).reshape(2, cfg.NO, OCT_CH, OCT_ST)

    return jnp.concatenate([one(bbr), one(bbi)], axis=-1).astype(BF16)


def _blockdiag_c(cfg, c_re, c_im):
    eye = jnp.eye(8, dtype=F32)

    def one(t):
        t = t.transpose(0, 1, 3, 2).reshape(2, cfg.NO, 8, cfg.P, S5_GROUP)
        return jnp.einsum("dogpc,gh->dogphc", t, eye).reshape(2, cfg.NO, OCT_ST, OCT_CH)

    return jnp.concatenate([one(c_re), -one(c_im)], axis=2).astype(BF16)


def _diag_b(cfg, dbf):
    eye = jnp.eye(8, dtype=F32)

    def one(t):
        t = t.reshape(2, cfg.NO, 8, S5_GROUP, 8, cfg.P)
        t = jnp.einsum("dogchp,gh->dogcp", t, eye).reshape(2, cfg.G, S5_GROUP, cfg.P)
        return t.transpose(2, 3, 0, 1).reshape(S5_GROUP, cfg.P, 2 * cfg.G)

    return one(dbf[..., :OCT_ST]), one(dbf[..., OCT_ST:])


def _diag_c(cfg, dcft):
    eye = jnp.eye(8, dtype=F32)

    def one(t):
        t = t.reshape(2, cfg.NO, 8, S5_GROUP, 8, cfg.P)
        return jnp.einsum("dohcgp,gh->dogcp", t, eye).reshape(2, cfg.G, S5_GROUP, cfg.P)

    return one(dcft[..., :OCT_ST]), -one(dcft[..., OCT_ST:])


def _recur(buf, st, a_ref, n_oct, ti, rev, store):
    for o in range(0, n_oct, 2):
        cols = [(pl.ds(oo * 2 * OCT_ST, OCT_ST), pl.ds(oo * 2 * OCT_ST + OCT_ST, OCT_ST)) for oo in (o, o + 1)]
        scol = [pl.ds(oo * OCT_ST, OCT_ST) for oo in (o, o + 1)]
        coef = [(a_ref[0, :, sc], a_ref[1, :, sc]) for sc in scol]
        init = (st[0, :, scol[0]], st[1, :, scol[0]], st[0, :, scol[1]], st[1, :, scol[1]])

        def step(i4, carry, cols=cols, coef=coef):
            carry = list(carry)
            for q in range(unroll):
                i = i4 * unroll + q
                r = pl.multiple_of((i + rev * (ti - 1 - 2 * i)) * 8, 8)
                for s in range(2):
                    sr, si = carry[2 * s], carry[2 * s + 1]
                    ar, ai = coef[s]
                    zr = buf[pl.ds(r, 8), cols[s][0]]
                    zi = buf[pl.ds(r, 8), cols[s][1]]
                    nr = ar * sr - ai * si + zr
                    ni = ar * si + ai * sr + zi
                    if store:
                        buf[pl.ds(r, 8), cols[s][0]] = nr
                        buf[pl.ds(r, 8), cols[s][1]] = ni
                    carry[2 * s], carry[2 * s + 1] = nr, ni
            return tuple(carry)

        unroll = 4 if ti % 4 == 0 else 1
        fin = lax.fori_loop(0, ti // unroll, step, init)
        st[0, :, scol[0]] = fin[0]
        st[1, :, scol[0]] = fin[1]
        st[0, :, scol[1]] = fin[2]
        st[1, :, scol[1]] = fin[3]


def _s5_fwd_pass(cfg, name, tok, mod8, col_sh, col_sc, bf, acoef, r0, n, s_init=None, cf=None, y_prev=None):
    D, NO, NS = cfg.D, cfg.NO, cfg.NS
    ti = cfg.ti(n)
    nb = n // ti
    R = 8 * ti
    ob = r0 // R
    second = s_init is not None
    blk = lambda d, j: ob + j + d * (nb - 1 - 2 * j)

    def body(*refs):
        if second:
            tok_ref, mod_ref, bf_ref, a_ref, si_ref, cf_ref, yp_ref, y_ref, ck_ref, fin_ref, zbuf, st = refs
        else:
            tok_ref, mod_ref, bf_ref, a_ref, fin_ref, zbuf, st = refs
        d = pl.program_id(0)
        j = pl.program_id(1)

        @pl.when(j == 0)
        def _():
            if second:
                st[...] = si_ref[...]
            else:
                st[...] = jnp.zeros_like(st)

        if second:
            ck_ref[...] = st[...]
        u = _mod(tok_ref[...], mod_ref[:, col_sh:col_sh + D], mod_ref[:, col_sc:col_sc + D]).astype(BF16)
        for o in range(NO):
            zbuf[:, o * 1024:(o + 1) * 1024] = _dot(u[:, o * OCT_CH:(o + 1) * OCT_CH], bf_ref[o])
        _recur(zbuf, st, a_ref, NO, ti, d, second)
        if second:
            for o in range(NO):
                y_ref[:, o * OCT_CH:(o + 1) * OCT_CH] = _dot(zbuf[:, o * 1024:(o + 1) * 1024].astype(BF16), cf_ref[o])

        @pl.when(j == nb - 1)
        def _():
            fin_ref[...] = st[...]

    st_spec = pl.BlockSpec((None, 2, 8, NS), lambda d, j: (d, 0, 0, 0))
    in_specs = [pl.BlockSpec((R, D), lambda d, j: (blk(d, j), 0)), _full(mod8.shape),
                pl.BlockSpec((None, NO, OCT_CH, 1024), lambda d, j: (d, 0, 0, 0)), st_spec]
    args = [tok, mod8, bf, acoef]
    scratch = [pltpu.VMEM((R, NO * 1024), F32), pltpu.VMEM((2, 8, NS), F32)]
    if not second:
        return pl.pallas_call(body, name=name, grid=(2, nb), in_specs=in_specs, out_specs=st_spec,
                              out_shape=_sds((2, 2, 8, NS)), scratch_shapes=scratch, compiler_params=_cp(2))(*args)
    in_specs += [st_spec, pl.BlockSpec((None, NO, 1024, OCT_CH), lambda d, j: (d, 0, 0, 0))]
    args += [s_init, cf]
    aliases = {}
    if y_prev is not None:
        in_specs.append(ANY)
        args.append(y_prev)
        aliases = {6: 0}
    else:
        in_specs.append(_full((8, LANES)))
        args.append(jnp.zeros((8, LANES), F32))
    out_specs = [pl.BlockSpec((None, R, D), lambda d, j: (d, blk(d, j), 0)),
                 pl.BlockSpec((None, None, 2, 8, NS), lambda d, j: (d, j + d * (nb - 1 - 2 * j), 0, 0, 0)), st_spec]
    out_shape = [_sds((2, cfg.T, D)), _sds((2, nb, 2, 8, NS)), _sds((2, 2, 8, NS))]
    return pl.pallas_call(body, name=name, grid=(2, nb), in_specs=in_specs, out_specs=out_specs, out_shape=out_shape,
                          input_output_aliases=aliases, scratch_shapes=scratch, compiler_params=_cp(2))(*args)


def _s5_chain(cfg, name, fin_local, acoef, n, inc, prev_fin=None):
    NS = cfg.NS
    nsq = int(round(math.log2(n)))
    assert 2 ** nsq == n

    def body(*refs):
        if prev_fin is not None:
            f_ref, a_ref, p_ref, o_ref = refs
        else:
            f_ref, a_ref, o_ref = refs
        for d in range(2):
            pr, pi = a_ref[d, 0, 0:1, :], a_ref[d, 1, 0:1, :]
            for _ in range(nsq):
                pr, pi = pr * pr - pi * pi, 2.0 * pr * pi
            for b in range(2):
                order = [4 * b + k for k in range(4)]
                if not inc[d]:
                    order = order[::-1]
                if prev_fin is not None:
                    last = order[-1]
                    sr, si = p_ref[d, 0, last:last + 1, :], p_ref[d, 1, last:last + 1, :]
                else:
                    sr = jnp.zeros((1, NS), F32)
                    si = jnp.zeros((1, NS), F32)
                for k in order:
                    o_ref[d, 0, k:k + 1, :] = sr
                    o_ref[d, 1, k:k + 1, :] = si
                    fr, fi = f_ref[d, 0, k:k + 1, :], f_ref[d, 1, k:k + 1, :]
                    sr, si = pr * sr - pi * si + fr, pr * si + pi * sr + fi

    args = [fin_local, acoef] + ([prev_fin] if prev_fin is not None else [])
    return pl.pallas_call(body, name=name, out_shape=_sds((2, 2, 8, NS)))(*args)


def _s5_forward(cfg, tag, tok, mod8, col_sh, col_sc, bf, cf, acoef, point=None):
    saved = {}
    fin_prev = None
    y = None
    for ph, (r0, n) in (("c", (0, cfg.nc)), ("x", (cfg.Tc, cfg.nx))):
        m8 = mod8[0 if ph == "c" else 1]
        loc = _s5_fwd_pass(cfg, f"{tag}_scan1{ph}", tok, m8, col_sh, col_sc, bf, acoef, r0, n)
        if point is not None and ph == "x":
            loc = point(loc)
        s_in = _s5_chain(cfg, f"{tag}_chain{ph}", loc, acoef, n, (True, False), fin_prev)
        y, ck, fin_prev = _s5_fwd_pass(cfg, f"{tag}_scan2{ph}", tok, m8, col_sh, col_sc, bf, acoef, r0, n, s_in, cf, y)
        saved[ph] = ck
    if point is not None:
        y = point(y)
    return y, saved


def _s5_bwd_pass(cfg, name, dy, tok, mod8, col_sh, col_sc, bf, cf, acoef, acoef_adj, r0, n, g_init=None, ck=None,
                 du_prev=None):
    D, NO, NS = cfg.D, cfg.NO, cfg.NS
    ti = cfg.ti(n)
    nb = n // ti
    R = 8 * ti
    ob = r0 // R
    second = g_init is not None
    has_dy = dy is not None
    blk = lambda d, j: ob + j + (1 - d) * (nb - 1 - 2 * j)

    def body(*refs):
        refs = list(refs)
        dy_ref = refs.pop(0) if has_dy else None
        if second:
            (tok_ref, mod_ref, bf_ref, cf_ref, a_ref, aa_ref, gi_ref, ck_ref, dup_ref,
             du_ref, da_ref, dbf_ref, dcf_ref, gfin_ref, qbuf, zbuf, gst, hst) = refs
        else:
            cf_ref, aa_ref, gfin_ref, qbuf, gst = refs
        d = pl.program_id(0)
        j = pl.program_id(1)

        @pl.when(j == 0)
        def _():
            if second:
                gst[...] = gi_ref[...]
                da_ref[...] = jnp.zeros_like(da_ref)
                dbf_ref[...] = jnp.zeros_like(dbf_ref)
                dcf_ref[...] = jnp.zeros_like(dcf_ref)
            else:
                gst[...] = jnp.zeros_like(gst)

        if has_dy:
            dyb = dy_ref[...].astype(BF16)
            for o in range(NO):
                qbuf[:, o * 1024:(o + 1) * 1024] = _dot_nt(dyb[:, o * OCT_CH:(o + 1) * OCT_CH], cf_ref[o])
        else:
            qbuf[...] = jnp.zeros_like(qbuf)
        _recur(qbuf, gst, aa_ref, NO, ti, 1 - d, second)

        if second:
            u = _mod(tok_ref[...], mod_ref[:, col_sh:col_sh + D], mod_ref[:, col_sc:col_sc + D]).astype(BF16)
            for o in range(NO):
                zbuf[:, o * 1024:(o + 1) * 1024] = _dot(u[:, o * OCT_CH:(o + 1) * OCT_CH], bf_ref[o])
            hst[...] = ck_ref[...]
            _recur(zbuf, hst, a_ref, NO, ti, d, True)

            g_off, h_off = (1 - d) * 8, d * 8
            edge = pl.multiple_of(d * (R - 8), 8)
            for o in range(0, NO, 2):
                cols = [(pl.ds(oo * 1024, OCT_ST), pl.ds(oo * 1024 + OCT_ST, OCT_ST)) for oo in (o, o + 1)]
                scol = [pl.ds(oo * OCT_ST, OCT_ST) for oo in (o, o + 1)]
                init = []
                for s in range(2):
                    er, ei = qbuf[pl.ds(edge, 8), cols[s][0]], qbuf[pl.ds(edge, 8), cols[s][1]]
                    kr, ki = ck_ref[0, :, scol[s]], ck_ref[1, :, scol[s]]
                    init += [er * kr + ei * ki, ei * kr - er * ki]

                def stp(i, carry, cols=cols):
                    rg = pl.multiple_of(i * 8 + g_off, 8)
                    rh = pl.multiple_of(i * 8 + h_off, 8)
                    out = []
                    for s in range(2):
                        gr, gi = qbuf[pl.ds(rg, 8), cols[s][0]], qbuf[pl.ds(rg, 8), cols[s][1]]
                        hr, hi = zbuf[pl.ds(rh, 8), cols[s][0]], zbuf[pl.ds(rh, 8), cols[s][1]]
                        out += [carry[2 * s] + (gr * hr + gi * hi), carry[2 * s + 1] + (gi * hr - gr * hi)]
                    return tuple(out)

                fin = lax.fori_loop(0, ti - 1, stp, tuple(init))
                for s in range(2):
                    da_ref[0, :, scol[s]] += fin[2 * s]
                    da_ref[1, :, scol[s]] += fin[2 * s + 1]

            for o in range(NO):
                gb = qbuf[:, o * 1024:(o + 1) * 1024].astype(BF16)
                uo = u[:, o * OCT_CH:(o + 1) * OCT_CH]
                dbf_ref[o] += _dot_tn(uo, gb)
                if has_dy:
                    dcf_ref[o] += _dot_tn(dyb[:, o * OCT_CH:(o + 1) * OCT_CH], zbuf[:, o * 1024:(o + 1) * 1024].astype(BF16))
                du_ref[:, o * OCT_CH:(o + 1) * OCT_CH] = _dot_nt(gb, bf_ref[o])

        @pl.when(j == nb - 1)
        def _():
            gfin_ref[...] = gst[...]

    st_spec = pl.BlockSpec((None, 2, 8, NS), lambda d, j: (d, 0, 0, 0))
    row_spec = pl.BlockSpec((R, D), lambda d, j: (blk(d, j), 0))
    bf_spec = pl.BlockSpec((None, NO, OCT_CH, 1024), lambda d, j: (d, 0, 0, 0))
    cf_spec = pl.BlockSpec((None, NO, 1024, OCT_CH), lambda d, j: (d, 0, 0, 0))
    in_specs, args = [], []
    if has_dy:
        in_specs.append(row_spec)
        args.append(dy)
    if not second:
        in_specs += [cf_spec, st_spec]
        args += [cf, acoef_adj]
        return pl.pallas_call(body, name=name, grid=(2, nb), in_specs=in_specs, out_specs=st_spec,
                              out_shape=_sds((2, 2, 8, NS)),
                              scratch_shapes=[pltpu.VMEM((R, NO * 1024), F32), pltpu.VMEM((2, 8, NS), F32)],
                              compiler_params=_cp(2))(*args)
    ck_spec = pl.BlockSpec((None, None, 2, 8, NS), lambda d, j: (d, j + (1 - d) * (nb - 1 - 2 * j), 0, 0, 0))
    in_specs += [row_spec, _full(mod8.shape), bf_spec, cf_spec, st_spec, st_spec, st_spec, ck_spec]
    args += [tok, mod8, bf, cf, acoef, acoef_adj, g_init, ck]
    n_before = len(args)
    aliases = {}
    if du_prev is not None:
        in_specs.append(ANY)
        args.append(du_prev)
        aliases = {n_before: 0}
    else:
        in_specs.append(_full((8, LANES)))
        args.append(jnp.zeros((8, LANES), F32))
    out_specs = [pl.BlockSpec((None, R, D), lambda d, j: (d, blk(d, j), 0)), st_spec, bf_spec, bf_spec, st_spec]
    out_shape = [_sds((2, cfg.T, D)), _sds((2, 2, 8, NS)), _sds((2, NO, OCT_CH, 1024)), _sds((2, NO, OCT_CH, 1024)),
                 _sds((2, 2, 8, NS))]
    scratch = [pltpu.VMEM((R, NO * 1024), F32), pltpu.VMEM((R, NO * 1024), F32), pltpu.VMEM((2, 8, NS), F32),
               pltpu.VMEM((2, 8, NS), F32)]
    return pl.pallas_call(body, name=name, grid=(2, nb), in_specs=in_specs, out_specs=out_specs, out_shape=out_shape,
                          input_output_aliases=aliases, scratch_shapes=scratch, compiler_params=_cp(2))(*args)


def _s5_backward(cfg, tag, dy, dy_ctx, tok, mod8, col_sh, col_sc, bf, cf, acoef, acoef_adj, saved):
    g_prev = None
    acc = None
    du = None
    for ph, (r0, n) in (("x", (cfg.Tc, cfg.nx)), ("c", (0, cfg.nc))):
        m8 = mod8[0 if ph == "c" else 1]
        dyp = dy if (ph == "x" or dy_ctx) else None
        loc = _s5_bwd_pass(cfg, f"{tag}_adjA{ph}", dyp, tok, m8, col_sh, col_sc, bf, cf, acoef, acoef_adj, r0, n)
        g_in = _s5_chain(cfg, f"{tag}_adjchain{ph}", loc, acoef_adj, n, (False, True), g_prev)
        du, da, dbf, dcf, g_prev = _s5_bwd_pass(cfg, f"{tag}_adjB{ph}", dyp, tok, m8, col_sh, col_sc, bf, cf, acoef,
                                                acoef_adj, r0, n, g_in, saved[ph], du)
        new = (da, dbf, dcf)
        if acc is None:
            acc = new
        else:
            acc = tuple(_ew(f"{tag}_accsum{q}", lambda a, b: a + b, [a.reshape(-1, a.shape[-1]), b.reshape(-1, b.shape[-1])],
                            [_sds((a.size // a.shape[-1], a.shape[-1]))])[0].reshape(a.shape)
                        for q, (a, b) in enumerate(zip(acc, new)))
    return du, acc


def _tok_specs(cfg, rows, width, tile=None):
    tm = tile or cfg.TM
    off = rows[0] // tm
    return pl.BlockSpec((tm, width), lambda i: (i + off, 0)), rows[1] // tm, off


def _mod_spec(cfg, mod8, off):
    st = _stream_of(cfg, off, cfg.TM)
    return pl.BlockSpec((None, 8, mod8.shape[-1]), lambda i: (st(i), 0, 0))


def _wspec(w):
    fam, slot = w
    _, _, kk, nn = fam.shape
    return pl.BlockSpec((4, None, kk, nn), lambda *i: (0, slot, 0, 0), pipeline_mode=pl.Buffered(1))


def _glu_ln(cfg, name, rows, tok, y, mod8, cols, dskip, w, b, gain, bias):
    D, TM = cfg.D, cfg.TM
    csh, csc, cg = cols
    spec, nt, off = _tok_specs(cfg, rows, D)
    spec2, _, _ = _tok_specs(cfg, rows, 2 * D)

    def body(tok_ref, y_ref, mod_ref, ds_ref, w_ref, b_ref, g_ref, bi_ref, x1_ref, r1_ref, mix_ref, zz_ref, zb_ref, yy_ref):
        t = tok_ref[...]
        u = _mod(t, mod_ref[:, csh:csh + D], mod_ref[:, csc:csc + D])
        yy = ds_ref[...] * u + y_ref[0] + y_ref[1]
        zb = _gelu(yy).astype(BF16)
        zz = jnp.concatenate([_dot(zb, w_ref[s]) for s in range(4)], axis=-1) + b_ref[...]
        mix = zz[:, :D] * _sigmoid(zz[:, D:])
        r1 = DN_ALPHA * t + _rowscale(mix, mod_ref[:, cg:cg + D])
        xhat, _ = _ln_stats(r1)
        x1_ref[...] = xhat * g_ref[...] + bi_ref[...]
        r1_ref[...] = r1
        mix_ref[...] = mix
        zz_ref[...] = zz
        zb_ref[...] = zb
        yy_ref[...] = yy

    T = cfg.T
    return pl.pallas_call(
        body, name=name, grid=(nt,),
        in_specs=[spec, pl.BlockSpec((2, TM, D), lambda i: (0, i + off, 0)), _mod_spec(cfg, mod8, off), _full((1, D)),
                  _wspec(w), _full((1, 2 * D)), _full((1, D)), _full((1, D))],
        out_specs=[spec, spec, spec, spec2, spec, spec],
        out_shape=[_sds((T, D)), _sds((T, D)), _sds((T, D)), _sds((T, 2 * D)), _sds((T, D), BF16), _sds((T, D))],
        compiler_params=_cp(1))(tok, y, mod8, dskip, w[0], b, gain, bias)


def _mlp_ln(cfg, name, rows, x1, mod8, cols, w1, w2, gain, bias):
    D, TM = cfg.D, cfg.TM
    csh, csc, cg = cols
    spec, nt, off = _tok_specs(cfg, rows, D)
    specf, _, _ = _tok_specs(cfg, rows, cfg.F)
    fb = cfg.F // 4

    def body(x_ref, mod_ref, w1_ref, w2_ref, g_ref, bi_ref, x2_ref, r2_ref, out_ref, a_ref, h_ref):
        t = x_ref[...]
        h = _mod(t, mod_ref[:, csh:csh + D], mod_ref[:, csc:csc + D]).astype(BF16)
        out = jnp.zeros((TM, D), F32)
        for s in range(4):
            hid = jnp.maximum(_dot(h, w1_ref[s]), 0.0)
            a = (hid * hid).astype(BF16)
            a_ref[:, s * fb:(s + 1) * fb] = a
            out = out + _dot(a, w2_ref[s])
        r2 = DN_ALPHA * t + _rowscale(out, mod_ref[:, cg:cg + D])
        xhat, _ = _ln_stats(r2)
        x2_ref[...] = xhat * g_ref[...] + bi_ref[...]
        r2_ref[...] = r2
        out_ref[...] = out
        h_ref[...] = h

    T = cfg.T
    return pl.pallas_call(
        body, name=name, grid=(nt,),
        in_specs=[spec, _mod_spec(cfg, mod8, off), _wspec(w1), _wspec(w2), _full((1, D)), _full((1, D))],
        out_specs=[spec, spec, spec, specf, spec],
        out_shape=[_sds((T, D)), _sds((T, D)), _sds((T, D)), _sds((T, cfg.F), BF16), _sds((T, D), BF16)],
        compiler_params=_cp(1))(x1, mod8, w1[0], w2[0], gain, bias)


def _pw1_glu(cfg, name, rows, tok, mod8, cols, w, b):
    D, TM = cfg.D, cfg.TM
    csh, csc = cols
    spec, nt, off = _tok_specs(cfg, rows, D)
    spec2, _, _ = _tok_specs(cfg, rows, 2 * D)

    def body(tok_ref, mod_ref, w_ref, b_ref, aa_ref, ag_ref, h_ref):
        h = _mod(tok_ref[...], mod_ref[:, csh:csh + D], mod_ref[:, csc:csc + D]).astype(BF16)
        aa = jnp.concatenate([_dot(h, w_ref[s]) for s in range(4)], axis=-1) + b_ref[...]
        aa_ref[...] = aa
        ag_ref[...] = aa[:, :D] * _sigmoid(aa[:, D:])
        h_ref[...] = h

    T = cfg.T
    return pl.pallas_call(
        body, name=name, grid=(nt,),
        in_specs=[spec, _mod_spec(cfg, mod8, off), _wspec(w), _full((1, 2 * D))],
        out_specs=[spec2, spec, spec],
        out_shape=[_sds((T, 2 * D)), _sds((T, D)), _sds((T, D), BF16)], compiler_params=_cp(1))(tok, mod8, w[0], b)


def _halo_maps(cfg, rows):
    TM, HB = cfg.TM, cfg.HB
    off = rows[0] // TM
    nct = cfg.Tc // TM
    ntx = cfg.Tx // TM

    def phase(i):
        t = i + off
        is_x = t >= nct
        first = jnp.where(is_x, nct, 0)
        cnt = jnp.where(is_x, ntx, nct)
        return t, first, cnt

    def prev(i):
        t, first, cnt = phase(i)
        return jnp.where(t == first, 2 * (first + cnt) - 1, 2 * t - 1), 0

    def nxt(i):
        t, first, cnt = phase(i)
        return jnp.where(t == first + cnt - 1, 2 * first, 2 * t + 2), 0

    def edge(i):
        t, first, cnt = phase(i)
        return t == first, t == first + cnt - 1

    return prev, nxt, edge, off


def _halo_fix(prev, nxt, is_first, is_last):
    hb, d = prev.shape
    k = lax.broadcasted_iota(jnp.int32, (hb // 8, 8, d), 1)
    p3 = prev.reshape(hb // 8, 8, d)
    n3 = nxt.reshape(hb // 8, 8, d)
    p_roll = jnp.where((k % 4) == 0, 0.0, pltpu.roll(p3, 1, 1))
    n_roll = jnp.where((k % 4) == 3, 0.0, pltpu.roll(n3, 7, 1))
    p3 = jnp.where(is_first, p_roll, p3)
    n3 = jnp.where(is_last, n_roll, n3)
    return p3.reshape(hb, d), n3.reshape(hb, d)


def _dwconv_ln(cfg, name, rows, ag, w_dw, b_dw, ln_g, ln_b):
    D, TM, HB, KW, half = cfg.D, cfg.TM, cfg.HB, cfg.KW, cfg.half
    prev_map, next_map, edge, off = _halo_maps(cfg, rows)
    spec, nt, _ = _tok_specs(cfg, rows, D)

    def body(cur_ref, prev_ref, next_ref, w_ref, b_ref, g_ref, bi_ref, cv_ref, s_ref, ext):
        i = pl.program_id(0)
        is_first, is_last = edge(i)

        @pl.when(i >= 0)
        def _():
            p, n = _halo_fix(prev_ref[...], next_ref[...], is_first, is_last)
            ext[0:HB, :] = p
            ext[HB:HB + TM, :] = cur_ref[...]
            ext[HB + TM:, :] = n

        acc = jnp.zeros((TM, D), F32)
        for k in range(KW):
            lo = HB + 8 * (k - half)
            acc = acc + w_ref[k:k + 1, :] * ext[lo:lo + TM, :]
        cv_ref[...] = acc + b_ref[...]
        xhat, _ = _ln_stats(cv_ref[...])
        nn = xhat * g_ref[...] + bi_ref[...]
        s_ref[...] = (nn * _sigmoid(nn)).astype(BF16)

    T = cfg.T
    return pl.pallas_call(
        body, name=name, grid=(nt,),
        in_specs=[spec, pl.BlockSpec((HB, D), prev_map), pl.BlockSpec((HB, D), next_map), _full((KW, D)),
                  _full((1, D)), _full((1, D)), _full((1, D))],
        out_specs=[spec, spec], out_shape=[_sds((T, D)), _sds((T, D), BF16)],
        scratch_shapes=[pltpu.VMEM((TM + 2 * HB, D), F32)], compiler_params=_cp(1))(ag, ag, ag, w_dw, b_dw, ln_g, ln_b)


def _pw2_ln(cfg, name, rows, s, tok, mod8, cg, w, b, gain, bias):
    D, TM = cfg.D, cfg.TM
    spec, nt, off = _tok_specs(cfg, rows, D)
    kb = D // 4

    def body(s_ref, tok_ref, mod_ref, w_ref, b_ref, g_ref, bi_ref, x1_ref, r1_ref, mix_ref):
        sv = s_ref[...]
        mix = b_ref[...] + jnp.zeros((TM, D), F32)
        for q in range(4):
            mix = mix + _dot(sv[:, q * kb:(q + 1) * kb], w_ref[q])
        r1 = DN_ALPHA * tok_ref[...] + _rowscale(mix, mod_ref[:, cg:cg + D])
        xhat, _ = _ln_stats(r1)
        x1_ref[...] = xhat * g_ref[...] + bi_ref[...]
        r1_ref[...] = r1
        mix_ref[...] = mix

    T = cfg.T
    return pl.pallas_call(
        body, name=name, grid=(nt,),
        in_specs=[spec, spec, _mod_spec(cfg, mod8, off), _wspec(w), _full((1, D)), _full((1, D)), _full((1, D))],
        out_specs=[spec, spec, spec], out_shape=[_sds((T, D))] * 3, compiler_params=_cp(1))(s, tok, mod8, w[0], b, gain, bias)


def _loss(cfg, xf, tgt):
    D, TM = cfg.D, cfg.TM
    spec, nt, off = _tok_specs(cfg, cfg.rows(False), D)

    def body(x_ref, t_ref, l_ref, dx_ref, acc):
        i = pl.program_id(0)
        dlt = x_ref[...] - t_ref[...]

        @pl.when(i == 0)
        def _():
            acc[...] = jnp.zeros_like(acc)

        acc[...] += _sum8(dlt * dlt)
        dx_ref[...] = dlt * (1.0 / D)

        @pl.when(i == nt - 1)
        def _():
            l_ref[...] = jnp.zeros((8, LANES), F32) + jnp.sum(acc[...]) * (0.5 / D)

    return pl.pallas_call(
        body, name="loss", grid=(nt,),
        in_specs=[spec, pl.BlockSpec((TM, D), lambda i: (i, 0))],
        out_specs=[_full((8, LANES)), spec], out_shape=[_sds((8, LANES)), _sds((cfg.T, D))],
        scratch_shapes=[pltpu.VMEM((8, D), F32)], compiler_params=_cp(1))(xf, tgt)


def _masked_spec(cfg, rows, width, valid_from_tile):
    tm = cfg.TM
    off = rows[0] // tm
    return pl.BlockSpec((tm, width), lambda i: (jnp.maximum(i + off, valid_from_tile), 0))


def _lnb(cfg, name, rows, dres, dres_ctx_ok, dh, r, aux, gain, mod_gate, cg, mod_next, csc):
    D, TM = cfg.D, cfg.TM
    spec, nt, off = _tok_specs(cfg, rows, D)
    nct = cfg.Tc // TM
    has_dres, has_dh = dres is not None, dh is not None

    def body(*refs):
        refs = list(refs)
        dres_ref = refs.pop(0) if has_dres else None
        dh_ref = refs.pop(0) if has_dh else None
        r_ref, aux_ref, g_ref, mg_ref = refs[:4]
        refs = refs[4:]
        mn_ref = refs.pop(0) if has_dh else None
        dprev_ref, dbr_ref, dgain_ref, dbias_ref, dg_ref, dsc_ref, dsh_ref, acc_g, acc_b = refs
        i = pl.program_id(0)
        t = i + off
        first_of_stream = (i == 0) | (t == nct)
        xhat, rstd = _ln_stats(r_ref[...])
        dy = jnp.zeros((TM, D), F32)
        if has_dres:
            dv = dres_ref[...]
            if not dres_ctx_ok:
                dv = jnp.where(t >= nct, dv, 0.0)
            dy = dy + dv
        if has_dh:
            dhv = dh_ref[...]
            dy = dy + _rowscale(dhv, 1.0 + mn_ref[:, csc:csc + D])
            x_out = xhat * g_ref[0:1, :] + g_ref[1:2, :]
            s_sc, s_sh = _sum8(dhv * x_out), _sum8(dhv)
        else:
            s_sc = s_sh = jnp.zeros((8, D), F32)
        dr = _ln_bwd(dy * g_ref[0:1, :], xhat, rstd)
        s_g = _sum8(dr * aux_ref[...])

        @pl.when(i == 0)
        def _():
            acc_g[...] = jnp.zeros_like(acc_g)
            acc_b[...] = jnp.zeros_like(acc_b)

        acc_g[...] += _sum8(dy * xhat)
        acc_b[...] += _sum8(dy)

        @pl.when(first_of_stream)
        def _():
            dg_ref[...] = s_g
            dsc_ref[...] = s_sc
            dsh_ref[...] = s_sh

        @pl.when(jnp.logical_not(first_of_stream))
        def _():
            dg_ref[...] += s_g
            dsc_ref[...] += s_sc
            dsh_ref[...] += s_sh

        dprev_ref[...] = DN_ALPHA * dr
        dbr_ref[...] = _rowscale(dr, mg_ref[:, cg:cg + D])

        @pl.when(i == nt - 1)
        def _():
            dgain_ref[...] = jnp.sum(acc_g[...], axis=0, keepdims=True)
            dbias_ref[...] = jnp.sum(acc_b[...], axis=0, keepdims=True)

    st = _stream_of(cfg, off, TM)
    in_specs, args = [], []
    if has_dres:
        in_specs.append(spec if dres_ctx_ok else _masked_spec(cfg, rows, D, nct))
        args.append(dres)
    if has_dh:
        in_specs.append(spec)
        args.append(dh)
    in_specs += [spec, spec, _full((2, D)), _mod_spec(cfg, mod_gate, off)]
    args += [r, aux, gain, mod_gate]
    if has_dh:
        in_specs.append(_mod_spec(cfg, mod_next, off))
        args.append(mod_next)
    acc_spec = pl.BlockSpec((None, 8, D), lambda i: (st(i), 0, 0))
    T = cfg.T
    return pl.pallas_call(
        body, name=name, grid=(nt,), in_specs=in_specs,
        out_specs=[spec, spec, _full((1, D)), _full((1, D)), acc_spec, acc_spec, acc_spec],
        out_shape=[_sds((T, D)), _sds((T, D)), _sds((1, D)), _sds((1, D)), _sds((2, 8, D)), _sds((2, 8, D)), _sds((2, 8, D))],
        scratch_shapes=[pltpu.VMEM((8, D), F32), pltpu.VMEM((8, D), F32)], compiler_params=_cp(1))(*args)


def _mlp_bwd(cfg, name, rows, dbr, a, w1, w2):
    D, TM = cfg.D, cfg.TM
    spec, nt, off = _tok_specs(cfg, rows, D)
    specf, _, _ = _tok_specs(cfg, rows, cfg.F)
    fb = cfg.F // 4

    def body(d_ref, a_ref, w1_ref, w2_ref, dh_ref, dhid_ref, dout_ref):
        dout = d_ref[...].astype(BF16)
        dh = jnp.zeros((TM, D), F32)
        for s in range(4):
            da = _dot_nt(dout, w2_ref[s])
            dhid = (da * (2.0 * jnp.sqrt(a_ref[:, s * fb:(s + 1) * fb].astype(F32)))).astype(BF16)
            dhid_ref[:, s * fb:(s + 1) * fb] = dhid
            dh = dh + _dot_nt(dhid, w1_ref[s])
        dh_ref[...] = dh
        dout_ref[...] = dout

    T = cfg.T
    return pl.pallas_call(
        body, name=name, grid=(nt,),
        in_specs=[spec, specf, _wspec(w1), _wspec(w2)],
        out_specs=[spec, specf, spec],
        out_shape=[_sds((T, D)), _sds((T, cfg.F), BF16), _sds((T, D), BF16)], compiler_params=_cp(1))(dbr, a, w1[0], w2[0])


def _wgrad(cfg, name, rows, a, b, mode, fam, slot):
    tw = cfg.TW
    off = rows[0] // tw
    nt = rows[1] // tw
    fresh = not hasattr(fam, "dtype")
    fam_shape = tuple(fam) if fresh else fam.shape
    _, n, kk, nn = fam_shape

    def body(a_ref, b_ref, *rest):
        o_ref = rest[-1]
        t = pl.program_id(1)
        part = _dot_tn(a_ref[...], b_ref[...])

        @pl.when(t == 0)
        def _():
            o_ref[...] = part

        @pl.when(t > 0)
        def _():
            o_ref[...] += part

    if mode == "col":
        a_spec = pl.BlockSpec((tw, kk), lambda s, t: (t + off, 0))
        b_spec = pl.BlockSpec((tw, nn), lambda s, t: (t + off, s))
    else:
        a_spec = pl.BlockSpec((tw, kk), lambda s, t: (t + off, s))
        b_spec = pl.BlockSpec((tw, nn), lambda s, t: (t + off, 0))
    out_spec = pl.BlockSpec((None, None, kk, nn), lambda s, t: (s, slot, 0, 0))
    if fresh:
        return pl.pallas_call(body, name=name, grid=(4, nt), in_specs=[a_spec, b_spec], out_specs=out_spec,
                              out_shape=_sds(fam_shape), compiler_params=_cp(2))(a, b)
    return pl.pallas_call(body, name=name, grid=(4, nt), in_specs=[a_spec, b_spec, ANY], out_specs=out_spec,
                          out_shape=_sds(fam_shape), input_output_aliases={2: 0}, compiler_params=_cp(2))(a, b, fam)


def _glu_bwd(cfg, name, rows, dmix, pre, w, yy=None):
    D, TM = cfg.D, cfg.TM
    spec, nt, off = _tok_specs(cfg, rows, D)
    spec2, _, _ = _tok_specs(cfg, rows, 2 * D)
    hw = w[0].shape[-1]
    has_y = yy is not None

    def body(*refs):
        refs = list(refs)
        d_ref, p_ref, w_ref = refs[:3]
        y_ref = refs[3] if has_y else None
        dz_ref, dp_ref, db_ref, acc = refs[-4:]
        i = pl.program_id(0)
        dm = d_ref[...]
        po, pg = p_ref[:, :D], p_ref[:, D:]
        sg = _sigmoid(pg)
        dpre = jnp.concatenate([dm * sg, dm * po * sg * (1.0 - sg)], axis=-1)

        @pl.when(i == 0)
        def _():
            acc[...] = jnp.zeros_like(acc)

        acc[...] += _sum8(dpre)
        dpb = dpre.astype(BF16)
        dz = jnp.zeros((TM, D), F32)
        for s in range(4):
            dz = dz + _dot_nt(dpb[:, s * hw:(s + 1) * hw], w_ref[s])
        if has_y:
            dz = dz * _gelu_grad(y_ref[...])
        dz_ref[...] = dz
        dp_ref[...] = dpb

        @pl.when(i == nt - 1)
        def _():
            db_ref[...] = jnp.sum(acc[...], axis=0, keepdims=True)

    T = cfg.T
    in_specs = [spec, spec2, _wspec(w)] + ([spec] if has_y else [])
    args = [dmix, pre, w[0]] + ([yy] if has_y else [])
    return pl.pallas_call(
        body, name=name, grid=(nt,), in_specs=in_specs, out_specs=[spec, spec2, _full((1, 2 * D))],
        out_shape=[_sds((T, D)), _sds((T, 2 * D), BF16), _sds((1, 2 * D))],
        scratch_shapes=[pltpu.VMEM((8, 2 * D), F32)], compiler_params=_cp(1))(*args)


def _s5_du(cfg, name, rows, du, dy, dy_from_tile, tok, mod8, cols, dskip):
    D, TM = cfg.D, cfg.TM
    csh, csc = cols
    spec, nt, off = _tok_specs(cfg, rows, D)

    def body(du_ref, dy_ref, tok_ref, mod_ref, ds_ref, dh_ref, dd_ref, acc):
        i = pl.program_id(0)
        dyv = jnp.where(i + off >= dy_from_tile, dy_ref[...], 0.0)
        u = _mod(tok_ref[...], mod_ref[:, csh:csh + D], mod_ref[:, csc:csc + D])
        dh_ref[...] = du_ref[0] + du_ref[1] + ds_ref[...] * dyv

        @pl.when(i == 0)
        def _():
            acc[...] = jnp.zeros_like(acc)

        acc[...] += _sum8(dyv * u)

        @pl.when(i == nt - 1)
        def _():
            dd_ref[...] = jnp.sum(acc[...], axis=0, keepdims=True)

    T = cfg.T
    return pl.pallas_call(
        body, name=name, grid=(nt,),
        in_specs=[pl.BlockSpec((2, TM, D), lambda i: (0, i + off, 0)), _masked_spec(cfg, rows, D, dy_from_tile), spec,
                  _mod_spec(cfg, mod8, off), _full((1, D))],
        out_specs=[spec, _full((1, D))], out_shape=[_sds((T, D)), _sds((1, D))],
        scratch_shapes=[pltpu.VMEM((8, D), F32)], compiler_params=_cp(1))(du, dy, tok, mod8, dskip)


def _pw2_bwd(cfg, name, rows, dmix, cv, w, ln_g, ln_b):
    D, TM = cfg.D, cfg.TM
    spec, nt, off = _tok_specs(cfg, rows, D)
    kb = D // 4

    def body(d_ref, cv_ref, w_ref, g_ref, b_ref, dcv_ref, dmb_ref, sums_ref, acc):
        i = pl.program_id(0)
        dm = d_ref[...]
        dmb = dm.astype(BF16)
        ds = jnp.concatenate([_dot_nt(dmb, w_ref[q]) for q in range(4)], axis=-1)
        xhat, rstd = _ln_stats(cv_ref[...])
        nn = xhat * g_ref[...] + b_ref[...]
        sg = _sigmoid(nn)
        dn = ds * (sg * (1.0 + nn * (1.0 - sg)))
        dcv = _ln_bwd(dn * g_ref[...], xhat, rstd)

        @pl.when(i == 0)
        def _():
            acc[...] = jnp.zeros_like(acc)

        acc[0] += _sum8(dn * xhat)
        acc[1] += _sum8(dn)
        acc[2] += _sum8(dcv)
        acc[3] += _sum8(dm)
        dcv_ref[...] = dcv
        dmb_ref[...] = dmb

        @pl.when(i == nt - 1)
        def _():
            for q in range(4):
                sums_ref[q:q + 1, :] = jnp.sum(acc[q], axis=0, keepdims=True)

    T = cfg.T
    return pl.pallas_call(
        body, name=name, grid=(nt,),
        in_specs=[spec, spec, _wspec(w), _full((1, D)), _full((1, D))],
        out_specs=[spec, spec, _full((4, D))], out_shape=[_sds((T, D)), _sds((T, D), BF16), _sds((4, D))],
        scratch_shapes=[pltpu.VMEM((4, 8, D), F32)], compiler_params=_cp(1))(dmix, cv, w[0], ln_g, ln_b)


def _dwconv_bwd(cfg, name, rows, dcv, ag, w_dw):
    D, TM, HB, KW, half = cfg.D, cfg.TM, cfg.HB, cfg.KW, cfg.half
    prev_map, next_map, edge, off = _halo_maps(cfg, rows)
    spec, nt, _ = _tok_specs(cfg, rows, D)

    def body(dc_ref, dp_ref, dn_ref, ac_ref, ap_ref, an_ref, w_ref, dag_ref, dw_ref, extd, exta, acc):
        i = pl.program_id(0)
        is_first, is_last = edge(i)

        @pl.when(i >= 0)
        def _():
            p, n = _halo_fix(dp_ref[...], dn_ref[...], is_first, is_last)
            extd[0:HB, :] = p
            extd[HB:HB + TM, :] = dc_ref[...]
            extd[HB + TM:, :] = n
            p, n = _halo_fix(ap_ref[...], an_ref[...], is_first, is_last)
            exta[0:HB, :] = p
            exta[HB:HB + TM, :] = ac_ref[...]
            exta[HB + TM:, :] = n

        @pl.when(i == 0)
        def _():
            acc[...] = jnp.zeros_like(acc)

        cr = min(CONV_ROWS, TM)
        for r0 in range(0, TM, cr):
            for lc in range(D // LANES):
                ls = pl.ds(lc * LANES, LANES)
                dcur = dc_ref[r0:r0 + cr, ls]
                dag = jnp.zeros((cr, LANES), F32)
                for k in range(KW):
                    lo = r0 + HB + 8 * (half - k)
                    la = r0 + HB + 8 * (k - half)
                    dag = dag + w_ref[k:k + 1, ls] * extd[lo:lo + cr, ls]
                    acc[k, :, ls] += _sum8(dcur * exta[la:la + cr, ls])
                dag_ref[r0:r0 + cr, ls] = dag

        @pl.when(i == nt - 1)
        def _():
            for k in range(KW):
                dw_ref[k:k + 1, :] = jnp.sum(acc[k], axis=0, keepdims=True)

    T = cfg.T
    hp, hn = pl.BlockSpec((HB, D), prev_map), pl.BlockSpec((HB, D), next_map)
    return pl.pallas_call(
        body, name=name, grid=(nt,), in_specs=[spec, hp, hn, spec, hp, hn, _full((KW, D))],
        out_specs=[spec, _full((KW, D))], out_shape=[_sds((T, D)), _sds((KW, D))],
        scratch_shapes=[pltpu.VMEM((TM + 2 * HB, D), F32), pltpu.VMEM((TM + 2 * HB, D), F32), pltpu.VMEM((KW, 8, D), F32)],
        compiler_params=_cp(1))(dcv, dcv, dcv, ag, ag, ag, w_dw)


def _input_bwd(cfg, dres, dh, tok0, mod8, csc):
    D, TM = cfg.D, cfg.TM
    rows = cfg.rows(True)
    spec, nt, off = _tok_specs(cfg, rows, D)
    nct = cfg.Tc // TM
    st = _stream_of(cfg, off, TM)

    def body(dr_ref, dh_ref, t_ref, mod_ref, gx_ref, dsc_ref, dsh_ref):
        i = pl.program_id(0)
        dhv = dh_ref[...]
        gx_ref[...] = dr_ref[...] + _rowscale(dhv, 1.0 + mod_ref[:, csc:csc + D])
        first = (i == 0) | (i == nct)
        s_sc, s_sh = _sum8(dhv * t_ref[...]), _sum8(dhv)

        @pl.when(first)
        def _():
            dsc_ref[...] = s_sc
            dsh_ref[...] = s_sh

        @pl.when(jnp.logical_not(first))
        def _():
            dsc_ref[...] += s_sc
            dsh_ref[...] += s_sh

    acc_spec = pl.BlockSpec((None, 8, D), lambda i: (st(i), 0, 0))
    return pl.pallas_call(
        body, name="input_bwd", grid=(nt,), in_specs=[spec, spec, spec, _mod_spec(cfg, mod8, off)],
        out_specs=[spec, acc_spec, acc_spec], out_shape=[_sds((cfg.T, D)), _sds((2, 8, D)), _sds((2, 8, D))],
        compiler_params=_cp(1))(dres, dh, tok0, mod8)


def _dmod_rows(dmod8):
    nl, _, _, w = dmod8.shape

    def body(d_ref, o_ref):
        xs = d_ref[1]
        cs = d_ref[0]
        o_ref[...] = jnp.zeros((8, w), F32)
        o_ref[0:1, :] = jnp.sum(xs[0:4], axis=0, keepdims=True)
        o_ref[1:2, :] = jnp.sum(xs[4:8], axis=0, keepdims=True)
        o_ref[2:3, :] = jnp.sum(cs, axis=0, keepdims=True)

    return pl.pallas_call(body, name="dmod_rows", grid=(nl,),
                          in_specs=[pl.BlockSpec((None, 2, 8, w), lambda l: (l, 0, 0, 0))],
                          out_specs=pl.BlockSpec((None, 8, w), lambda l: (l, 0, 0)), out_shape=_sds((nl, 8, w)),
                          compiler_params=_cp(1))(dmod8)


def _x_only(acc):
    return jnp.concatenate([jnp.zeros_like(acc[:1]), acc[1:]], axis=0)


def _pack(parts):
    bufs, meta, off = [], [], 0
    for p in parts:
        n = p.size
        rows = -(-n // (8 * LANES)) * 8
        flat = jnp.pad(p.reshape(-1).astype(F32), (0, rows * LANES - n)).reshape(rows, LANES)
        bufs.append(flat)
        meta.append((off, rows, p.shape))
        off += rows
    if off % 16:
        bufs.append(jnp.zeros((8, LANES), F32))
    return jnp.concatenate(bufs, axis=0), meta


def _unpack(buf, meta):
    out = []
    for off, rows, shape in meta:
        n = 1
        for s in shape:
            n *= s
        out.append(buf[off:off + rows].reshape(-1)[:n].reshape(shape))
    return out


def kernel(x, c, ctx, c_ctx, w_ada, b_ada, ln_gain, ln_bias, s5_lam_re, s5_lam_im, s5_log_dt, s5_b_re, s5_b_im, s5_c_re, s5_c_im, s5_d, s5_w_glu, s5_b_glu, cv_w_pw1, cv_b_pw1, cv_w_dw, cv_b_dw, cv_ln_g, cv_ln_b, cv_w_pw2, cv_b_pw2, mlp_w1, mlp_w2, loss_target, m_c_ctx, m_w_ada, m_b_ada, m_ln_gain, m_ln_bias, m_s5_lam_re, m_s5_lam_im, m_s5_log_dt, m_s5_b_re, m_s5_b_im, m_s5_c_re, m_s5_c_im, m_s5_d, m_s5_w_glu, m_s5_b_glu, m_cv_w_pw1, m_cv_b_pw1, m_cv_w_dw, m_cv_b_dw, m_cv_ln_g, m_cv_ln_b, m_cv_w_pw2, m_cv_b_pw2, m_mlp_w1, m_mlp_w2, v_c_ctx, v_w_ada, v_b_ada, v_ln_gain, v_ln_bias, v_s5_lam_re, v_s5_lam_im, v_s5_log_dt, v_s5_b_re, v_s5_b_im, v_s5_c_re, v_s5_c_im, v_s5_d, v_s5_w_glu, v_s5_b_glu, v_cv_w_pw1, v_cv_b_pw1, v_cv_w_dw, v_cv_b_dw, v_cv_ln_g, v_cv_ln_b, v_cv_w_pw2, v_cv_b_pw2, v_mlp_w1, v_mlp_w2):
    cfg = _Cfg(x, ctx, mlp_w1, cv_w_dw)
    D, T, Tc, Tx, B = cfg.D, cfg.T, cfg.Tc, cfg.Tx, cfg.B
    ax, ay, ac = lax.axis_index("x"), lax.axis_index("y"), lax.axis_index("c")
    shard = 2 * ax + ay
    dev = 4 * ax + 2 * ay + ac
    Ds = D // 4
    Wa = w_ada.shape[2]

    c_pad = jnp.concatenate([c, jnp.zeros((8 - B, D), F32)], axis=0)
    c_gath = _allgather8("gather_c", c_pad).reshape(8, 8, D)[:, :B].reshape(8 * B, D)
    c_all = jnp.concatenate([c_gath, c_ctx[None], jnp.zeros((7, D), F32)], axis=0)
    b_sh = lax.dynamic_slice_in_dim(b_ada, shard * Wa, Wa, axis=1)[:, None, :]
    mod_sh = _ada_fwd(c_all, w_ada, b_sh)
    mod_g = _allgather8("gather_mod", mod_sh.reshape(DEPTH * 24, Wa)).reshape(4, 2, DEPTH, 24, Wa)[:, 0]
    mods = mod_g.transpose(1, 2, 0, 3).reshape(DEPTH, 24, 4 * Wa)
    mine = lax.dynamic_slice_in_dim(mods, B * dev, B, axis=1)
    mod8 = jnp.stack([jnp.broadcast_to(mods[:, 16:17], (DEPTH, 8, 6 * D)), jnp.repeat(mine, 4, axis=1)], axis=1)
    SH1, SC1, G1, SH2, SC2, G2 = (k * D for k in range(6))

    small_parts = [ln_gain.reshape(-1, Ds), ln_bias.reshape(-1, Ds), cv_b_pw1.reshape(-1, Ds), cv_w_dw.reshape(-1, Ds),
                   cv_b_dw, cv_ln_g, cv_ln_b, cv_b_pw2]
    small_rows = [p.shape[0] for p in small_parts]
    sm = jnp.concatenate(small_parts, axis=0)
    pad_r = -sm.shape[0] % 8
    sm = jnp.pad(sm, ((0, pad_r), (0, 0)))
    sm_g = _allgather8("gather_small", sm).reshape(4, 2, sm.shape[0], Ds)[:, 0]
    pieces, o = [], 0
    for nr in small_rows:
        pieces.append(sm_g[:, o:o + nr])
        o += nr

    def unshard(p, lead):
        return p.reshape((4,) + lead + (Ds,)).transpose(tuple(range(1, len(lead) + 1)) + (0, len(lead) + 1)).reshape(lead + (4 * Ds,))

    ln_gain_f = unshard(pieces[0], (DEPTH, 2))
    ln_bias_f = unshard(pieces[1], (DEPTH, 2))
    nconv = cv_w_dw.shape[0]
    b_pw1_f = pieces[2].reshape(4, nconv, 2 * D // 4).transpose(1, 0, 2).reshape(nconv, 2 * D)
    w_dw_f = unshard(pieces[3], (nconv, cfg.KW))
    b_dw_f, cvg_f, cvb_f, b_pw2_f = (unshard(p, (nconv,)) for p in pieces[4:8])

    ns5 = s5_w_glu.shape[0]
    assert mlp_w1.shape[1:] == mlp_w2.shape[1:]
    fam_a = _place_shard("place_w1", mlp_w1, None, 0, 2 * DEPTH)
    fam_a = _place_shard("place_w2", mlp_w2, fam_a, DEPTH, 2 * DEPTH)
    fam_b = _place_shard("place_wglu", s5_w_glu, None, 0, ns5 + nconv)
    fam_b = _place_shard("place_wpw1", cv_w_pw1, fam_b, ns5, ns5 + nconv)
    fam_c = _place_shard("place_wpw2", cv_w_pw2, None, 0, nconv)
    (wb_full,) = _gather_weights([fam_b])
    ov = _Overlap()
    gather_tokens = ov.add("gather", _gather_gen("gatherw", [fam_a, fam_c]))

    pos = jnp.broadcast_to(_pos_embed(cfg.L // GRID_W, D)[None], (B, cfg.L, D))
    tok_in = jnp.concatenate([_to_perm(ctx), _to_perm(x)], axis=0)
    pos_in = jnp.concatenate([jnp.zeros((Tc, D), F32), _to_perm(pos)], axis=0)
    tok0 = _ew("add_pos", lambda a, b: a + b, [tok_in, pos_in], [_sds((T, D))])[0]
    tok0 = _tie(tok0, gather_tokens)
    tgt = _to_perm(loss_target)

    s5p = []
    for j in range(ns5):
        lay = _s5_layouts(cfg, s5_lam_re[j], s5_lam_im[j], s5_log_dt[j], s5_b_re[j], s5_b_im[j])
        abr, abi, bbr, bbi = _disc_fwd(*lay)
        s5p.append(dict(lay=lay, acoef=_coef_rows(cfg, abr, abi, False), acoef_adj=_coef_rows(cfg, abr, abi, True),
                        bf=_blockdiag_b(cfg, bbr, bbi), cf=_blockdiag_c(cfg, s5_c_re[j], s5_c_im[j])))

    kinds = ["s5" if i % 2 == 0 else "conv" for i in range(DEPTH)]
    tok = tok0
    saved = []
    s5_j = cv_j = 0
    for i in range(DEPTH):
        later_s5 = any(k == "s5" for k in kinds[i + 1:])
        rows = cfg.rows(later_s5)
        m8 = mod8[i]
        sv = dict(tok=tok, rows=rows, kind=kinds[i])
        g0, b0 = ln_gain_f[i, 0][None], ln_bias_f[i, 0][None]
        g1, b1 = ln_gain_f[i, 1][None], ln_bias_f[i, 1][None]
        if kinds[i] == "s5":
            j = s5_j
            s5_j += 1
            p = s5p[j]
            y, ck = _s5_forward(cfg, f"l{i}", tok, m8, SH1, SC1, p["bf"], p["cf"], p["acoef"], ov.point if i == 0 else None)
            wg = (wb_full, j)
            x1, r1, mix, zz, zb, yy = _glu_ln(cfg, f"l{i}_glu", rows, tok, y, m8, (SH1, SC1, G1), s5_d[j][None], wg,
                                              s5_b_glu[j][None], g0, b0)
            sv.update(j=j, ck=ck, zz=zz, zb=zb, yy=yy, wg=wg)
            if i == 0:
                x1 = ov.point(x1)
                wa_full, wc_full = ov.finish("gather", x1)
        else:
            j = cv_j
            cv_j += 1
            w1c, w2c = (wb_full, ns5 + j), (wc_full, j)
            aa, ag, hb = _pw1_glu(cfg, f"l{i}_pw1", rows, tok, m8, (SH1, SC1), w1c, b_pw1_f[j][None])
            cvv, sb = _dwconv_ln(cfg, f"l{i}_dw", rows, ag, w_dw_f[j], b_dw_f[j][None], cvg_f[j][None], cvb_f[j][None])
            x1, r1, mix = _pw2_ln(cfg, f"l{i}_pw2", rows, sb, tok, m8, G1, w2c, b_pw2_f[j][None], g0, b0)
            sv.update(j=j, aa=aa, ag=ag, hb=hb, cvv=cvv, sb=sb, w1c=w1c, w2c=w2c)
        w1m, w2m = (wa_full, i), (wa_full, DEPTH + i)
        x2, r2, mout, am, hm = _mlp_ln(cfg, f"l{i}_mlp", rows, x1, m8, (SH2, SC2, G2), w1m, w2m, g1, b1)
        sv.update(r1=r1, mix=mix, x1=x1, r2=r2, mout=mout, am=am, hm=hm, w1m=w1m, w2m=w2m, g0=g0, b0=b0, g1=g1, b1=b1)
        saved.append(sv)
        tok = x2

    loss8, dxf = _loss(cfg, tok, tgt)
    loss = lax.psum(loss8[0, 0], ("x", "y", "c"))

    dmod8 = [None] * DEPTH
    g_ln_gain = [[None, None] for _ in range(DEPTH)]
    g_ln_bias = [[None, None] for _ in range(DEPTH)]
    g_s5 = [None] * ns5
    g_cv = [None] * nconv
    dres, dh = dxf, None
    for i in reversed(range(DEPTH)):
        sv = saved[i]
        rows = sv["rows"]
        m8 = mod8[i]
        nxt_m8 = mod8[i + 1] if i + 1 < DEPTH else None
        ctx_ok = True if i + 1 >= DEPTH else (saved[i + 1]["rows"][0] == 0)
        if rows[0] != 0:
            ctx_ok = True
        dprev, dbr, dgn, dbs, dg2, dsc_n, dsh_n = _lnb(
            cfg, f"l{i}_lnb2", rows, dres, ctx_ok, dh, sv["r2"], sv["mout"], jnp.concatenate([sv["g1"], sv["b1"]], 0),
            m8, G2, nxt_m8, SC1)
        if rows[0] != 0:
            dg2, dsc_n, dsh_n = (_x_only(t) for t in (dg2, dsc_n, dsh_n))
        g_ln_gain[i][1], g_ln_bias[i][1] = dgn[0], dbs[0]
        if i + 1 < DEPTH:
            dmod8[i + 1]["sc1"], dmod8[i + 1]["sh1"] = dsc_n, dsh_n
        dmod8[i] = dict(g2=dg2)
        dh2, dhid, dout = _mlp_bwd(cfg, f"l{i}_mlpb", rows, dbr, sv["am"], sv["w1m"], sv["w2m"])
        dh2 = ov.point(dh2)
        ga = _wgrad(cfg, f"l{i}_gw1", rows, sv["hm"], dhid, "col", (4, 2, D, cfg.F // 4), 0)
        ga = _wgrad(cfg, f"l{i}_gw2", rows, sv["am"], dout, "row", ga, 1)
        dprev1, dbr1, dgn, dbs, dg1, dsc2, dsh2 = _lnb(
            cfg, f"l{i}_lnb1", rows, dprev, True, dh2, sv["r1"], sv["mix"], jnp.concatenate([sv["g0"], sv["b0"]], 0),
            m8, G1, m8, SC2)
        if rows[0] != 0:
            dg1, dsc2, dsh2 = (_x_only(t) for t in (dg1, dsc2, dsh2))
        g_ln_gain[i][0], g_ln_bias[i][0] = dgn[0], dbs[0]
        dmod8[i].update(g1=dg1, sc2=dsc2, sh2=dsh2)
        dbr1 = ov.point(dbr1)
        j = sv["j"]
        if sv["kind"] == "s5":
            p = s5p[j]
            dyy, dzz, dbglu = _glu_bwd(cfg, f"l{i}_glub", rows, dbr1, sv["zz"], sv["wg"], sv["yy"])
            dyy = ov.point(dyy)
            gb = _wgrad(cfg, f"l{i}_gwg", rows, sv["zb"], dzz, "col", (4, 1, D, D // 2), 0)
            du, (da, dbf, dcf) = _s5_backward(cfg, f"l{i}", dyy, rows[0] == 0, sv["tok"], m8, SH1, SC1, p["bf"], p["cf"],
                                              p["acoef"], p["acoef_adj"], sv["ck"])
            dh, dds = _s5_du(cfg, f"l{i}_du", cfg.rows(True), du, dyy, rows[0] // cfg.TM, sv["tok"], m8, (SH1, SC1),
                             s5_d[j][None])
            g_s5[j] = dict(da=da, dbf=dbf, dcf=dcf, dd=dds[0], dbglu=dbglu[0])
            layer_grads = [ga, gb]
        else:
            dcv, dmb, sums = _pw2_bwd(cfg, f"l{i}_pw2b", rows, dbr1, sv["cvv"], sv["w2c"], cvg_f[j][None], cvb_f[j][None])
            dcv = ov.point(dcv)
            gc = _wgrad(cfg, f"l{i}_gwp2", rows, sv["sb"], dmb, "row", (4, 1, D // 4, D), 0)
            dag, dwdw = _dwconv_bwd(cfg, f"l{i}_dwb", rows, dcv, sv["ag"], w_dw_f[j])
            dh, daa, dbpw1 = _glu_bwd(cfg, f"l{i}_pw1b", rows, dag, sv["aa"], sv["w1c"])
            gb = _wgrad(cfg, f"l{i}_gwp1", rows, sv["hb"], daa, "col", (4, 1, D, D // 2), 0)
            g_cv[j] = dict(ln_g=sums[0], ln_b=sums[1], b_dw=sums[2], b_pw2=sums[3], w_dw=dwdw, b_pw1=dbpw1[0])
            layer_grads = [ga, gb, gc]
        dres = ov.point(dprev1)
        dres = _tie(dres, ov.add(f"rs{i}", _reduce_scatter_gen(f"gw{i}", layer_grads)))
    gx_perm, dsc0, dsh0 = _input_bwd(cfg, dres, dh, tok0, mod8[0], SC1)
    dmod8[0]["sc1"], dmod8[0]["sh1"] = dsc0, dsh0
    grad_x = _from_perm(gx_perm[Tc:], B, cfg.L)
    reduced = [ov.finish(f"rs{i}", gx_perm) for i in range(DEPTH)]

    zero28 = jnp.zeros((2, 8, D), F32)
    dm8 = jnp.stack([jnp.concatenate([dmod8[i].get(k, zero28) for k in ("sh1", "sc1", "g1", "sh2", "sc2", "g2")], axis=-1)
                     for i in range(DEPTH)])
    dm_rows = _dmod_rows(dm8)
    dm_tab = jnp.zeros((DEPTH, 24, 6 * D), F32)
    dm_tab = lax.dynamic_update_slice_in_dim(dm_tab, dm_rows[:, 0:B], B * dev, axis=1)
    dm_tab = lax.dynamic_update_slice_in_dim(dm_tab, dm_rows[:, 2:3], 16, axis=1)

    eye_parts = []
    for j in range(ns5):
        g = g_s5[j]
        dbbr, dbbi = _diag_b(cfg, g["dbf"])
        dcr, dci = _diag_c(cfg, g["dcf"])
        eye_parts += [g["da"], dbbr, dbbi, dcr, dci, g["dd"], g["dbglu"]]
    for j in range(nconv):
        g = g_cv[j]
        eye_parts += [g["ln_g"], g["ln_b"], g["b_dw"], g["b_pw2"], g["w_dw"], g["b_pw1"]]
    eye_parts += [jnp.stack([jnp.stack(r) for r in g_ln_gain]), jnp.stack([jnp.stack(r) for r in g_ln_bias]), dm_tab]
    buf, meta = _pack(eye_parts)
    red = _unpack(_allreduce8("small", buf), meta)

    grads = {}
    k = 0
    lam_re_g, lam_im_g, ldt_g, bre_g, bim_g, cre_g, cim_g, dd_g, bglu_g = ([] for _ in range(9))
    for j in range(ns5):
        da, dbbr, dbbi, dcr, dci, dd, dbglu = red[k:k + 7]
        k += 7
        da_s = _sublane_sum(f"s5_dasum_{j}", da.reshape(4, 8, cfg.NS)).reshape(2, 2, cfg.NS)
        g_abr = da_s[:, 0].reshape(2, cfg.G, cfg.P).transpose(2, 0, 1).reshape(cfg.P, 2 * cfg.G)
        g_abi = da_s[:, 1].reshape(2, cfg.G, cfg.P).transpose(2, 0, 1).reshape(cfg.P, 2 * cfg.G)
        glr, gli, gldt, gbr, gbi = _disc_bwd(*s5p[j]["lay"], g_abr, g_abi, dbbr, dbbi)
        lam_re_g.append(glr.reshape(cfg.P, 2, cfg.G).transpose(1, 2, 0))
        lam_im_g.append(gli.reshape(cfg.P, 2, cfg.G).transpose(1, 2, 0))
        ldt_g.append(gldt.reshape(2, cfg.G))
        bre_g.append(gbr.reshape(S5_GROUP, cfg.P, 2, cfg.G).transpose(2, 3, 1, 0))
        bim_g.append(gbi.reshape(S5_GROUP, cfg.P, 2, cfg.G).transpose(2, 3, 1, 0))
        cre_g.append(dcr)
        cim_g.append(dci)
        dd_g.append(dd)
        bglu_g.append(dbglu)
    grads.update(s5_lam_re=jnp.stack(lam_re_g), s5_lam_im=jnp.stack(lam_im_g), s5_log_dt=jnp.stack(ldt_g),
                 s5_b_re=jnp.stack(bre_g), s5_b_im=jnp.stack(bim_g), s5_c_re=jnp.stack(cre_g), s5_c_im=jnp.stack(cim_g),
                 s5_d=jnp.stack(dd_g), s5_b_glu=jnp.stack(bglu_g))

    def my_cols(full, width):
        return lax.dynamic_slice_in_dim(full, shard * width, width, axis=full.ndim - 1)

    cvs = {n: [] for n in ("ln_g", "ln_b", "b_dw", "b_pw2", "w_dw", "b_pw1")}
    for j in range(nconv):
        for n, val in zip(("ln_g", "ln_b", "b_dw", "b_pw2", "w_dw", "b_pw1"), red[k:k + 6]):
            cvs[n].append(val)
        k += 6
    grads.update(cv_ln_g=my_cols(jnp.stack(cvs["ln_g"]), Ds), cv_ln_b=my_cols(jnp.stack(cvs["ln_b"]), Ds),
                 cv_b_dw=my_cols(jnp.stack(cvs["b_dw"]), Ds), cv_b_pw2=my_cols(jnp.stack(cvs["b_pw2"]), Ds),
                 cv_w_dw=my_cols(jnp.stack(cvs["w_dw"]), Ds), cv_b_pw1=my_cols(jnp.stack(cvs["b_pw1"]), 2 * D // 4))
    grads.update(ln_gain=my_cols(red[k], Ds), ln_bias=my_cols(red[k + 1], Ds))
    dm_all = red[k + 2]

    dm_sh = lax.dynamic_slice_in_dim(dm_all, shard * Wa, Wa, axis=2)
    gw_ada, dcond = _ada_bwd(c_all, dm_sh, w_ada)
    grads["w_ada"] = gw_ada
    grads["b_ada"] = _colsum_groups("ada_bsum", dm_all)
    dc_part = dcond[0:1]
    dc_buf = jnp.concatenate([jnp.where(ac == 0, dc_part, 0.0), jnp.zeros((7, D), F32)], axis=0)
    dc_tot = _allreduce8("cctx", dc_buf.reshape(8 * D // LANES, LANES)).reshape(8, D)[0:1]
    grads["c_ctx"] = _ew("cctx_grad", lambda g, cv: g * (_sigmoid(cv) * (1.0 + cv * (1.0 - _sigmoid(cv)))),
                         [jnp.broadcast_to(dc_tot, (8, D)), jnp.broadcast_to(c_ctx[None], (8, D))], [_sds((8, D))])[0][0]

    s5_layers = [i for i in range(DEPTH) if kinds[i] == "s5"]
    cv_layers = [i for i in range(DEPTH) if kinds[i] == "conv"]
    grads.update(mlp_w1=jnp.stack([reduced[i][0][0] for i in range(DEPTH)]),
                 mlp_w2=jnp.stack([reduced[i][0][1] for i in range(DEPTH)]),
                 s5_w_glu=jnp.stack([reduced[i][1][0] for i in s5_layers]),
                 cv_w_pw1=jnp.stack([reduced[i][1][0] for i in cv_layers]),
                 cv_w_pw2=jnp.stack([reduced[i][2][0] for i in cv_layers]))

    weights = dict(c_ctx=c_ctx, w_ada=w_ada, b_ada=b_ada, ln_gain=ln_gain, ln_bias=ln_bias, s5_lam_re=s5_lam_re,
                   s5_lam_im=s5_lam_im, s5_log_dt=s5_log_dt, s5_b_re=s5_b_re, s5_b_im=s5_b_im, s5_c_re=s5_c_re,
                   s5_c_im=s5_c_im, s5_d=s5_d, s5_w_glu=s5_w_glu, s5_b_glu=s5_b_glu, cv_w_pw1=cv_w_pw1, cv_b_pw1=cv_b_pw1,
                   cv_w_dw=cv_w_dw, cv_b_dw=cv_b_dw, cv_ln_g=cv_ln_g, cv_ln_b=cv_ln_b, cv_w_pw2=cv_w_pw2, cv_b_pw2=cv_b_pw2,
                   mlp_w1=mlp_w1, mlp_w2=mlp_w2)
    ms = dict(c_ctx=m_c_ctx, w_ada=m_w_ada, b_ada=m_b_ada, ln_gain=m_ln_gain, ln_bias=m_ln_bias, s5_lam_re=m_s5_lam_re,
              s5_lam_im=m_s5_lam_im, s5_log_dt=m_s5_log_dt, s5_b_re=m_s5_b_re, s5_b_im=m_s5_b_im, s5_c_re=m_s5_c_re,
              s5_c_im=m_s5_c_im, s5_d=m_s5_d, s5_w_glu=m_s5_w_glu, s5_b_glu=m_s5_b_glu, cv_w_pw1=m_cv_w_pw1,
              cv_b_pw1=m_cv_b_pw1, cv_w_dw=m_cv_w_dw, cv_b_dw=m_cv_b_dw, cv_ln_g=m_cv_ln_g, cv_ln_b=m_cv_ln_b,
              cv_w_pw2=m_cv_w_pw2, cv_b_pw2=m_cv_b_pw2, mlp_w1=m_mlp_w1, mlp_w2=m_mlp_w2)
    vs = dict(c_ctx=v_c_ctx, w_ada=v_w_ada, b_ada=v_b_ada, ln_gain=v_ln_gain, ln_bias=v_ln_bias, s5_lam_re=v_s5_lam_re,
              s5_lam_im=v_s5_lam_im, s5_log_dt=v_s5_log_dt, s5_b_re=v_s5_b_re, s5_b_im=v_s5_b_im, s5_c_re=v_s5_c_re,
              s5_c_im=v_s5_c_im, s5_d=v_s5_d, s5_w_glu=v_s5_w_glu, s5_b_glu=v_s5_b_glu, cv_w_pw1=v_cv_w_pw1,
              cv_b_pw1=v_cv_b_pw1, cv_w_dw=v_cv_w_dw, cv_b_dw=v_cv_b_dw, cv_ln_g=v_cv_ln_g, cv_ln_b=v_cv_ln_b,
              cv_w_pw2=v_cv_w_pw2, cv_b_pw2=v_cv_b_pw2, mlp_w1=v_mlp_w1, mlp_w2=v_mlp_w2)
    names = list(weights)
    deltas, new_m, new_v = {}, {}, {}
    for n in names:
        g = grads[n].reshape(weights[n].shape)
        grads[n] = g
        deltas[n], new_m[n], new_v[n] = _adamw("adamw_" + n, weights[n], g, ms[n], vs[n])
    return (loss, grad_x, *[grads[n] for n in names], *[deltas[n] for n in names], *[new_m[n] for n in names],
            *[new_v[n] for n in names])


def _sublane_sum(name, a):
    n, _, w = a.shape

    def body(a_ref, o_ref):
        for q in range(n):
            o_ref[q:q + 1, :] = jnp.sum(a_ref[q], axis=0, keepdims=True)

    return pl.pallas_call(body, name=name, out_shape=_sds((n, w)))(a)


def _colsum_groups(name, dm_all):
    nl, nr, w = dm_all.shape

    def body(d_ref, o_ref):
        o_ref[...] = jnp.zeros((8, w), F32) + jnp.sum(d_ref[...], axis=0, keepdims=True)

    out = pl.pallas_call(body, name=name, grid=(nl,), in_specs=[pl.BlockSpec((None, nr, w), lambda l: (l, 0, 0))],
                         out_specs=pl.BlockSpec((None, 8, w), lambda l: (l, 0, 0)), out_shape=_sds((nl, 8, w)),
                         compiler_params=_cp(1))(dm_all)
    return out[:, 0]
```

```python
import functools
import math

import jax
import jax.numpy as jnp
from jax import lax
from jax.experimental import pallas as pl
from jax.experimental.pallas import tpu as pltpu

F32 = jnp.float32
BF16 = jnp.bfloat16
MESH = pl.DeviceIdType.MESH
ANY = pl.BlockSpec(memory_space=pl.ANY)

DEPTH = 4
S5_GROUP = 16
S5_STATE = 64
GRID_W = 64
POS_TEMP = 10000.0
LAMBDA_RE_MAX = -1e-4
LN_EPS = 1e-5
DN_ALPHA = (2.0 * DEPTH) ** 0.25
ADAM_LR, ADAM_B1, ADAM_B2, ADAM_EPS, ADAM_WD, ADAM_STEP = 0.001, 0.9, 0.999, 1e-08, 0.01, 10

SUBLANES = 8
LANES = 128
OCT_CH = 128
OCT_ST = 512
CONV_ROWS = 64
VMEM_LIMIT = 56 * 1024 * 1024


def _cp(n_axes):
    return pltpu.CompilerParams(dimension_semantics=("arbitrary",) * n_axes, vmem_limit_bytes=VMEM_LIMIT)


def _full(shape, single=False):
    nd = len(shape)
    if single:
        return pl.BlockSpec(shape, lambda *i: (0,) * nd, pipeline_mode=pl.Buffered(1))
    return pl.BlockSpec(shape, lambda *i: (0,) * nd)


def _sds(shape, dtype=F32):
    return jax.ShapeDtypeStruct(tuple(shape), dtype)


def _mod(x, sh8, sc8):
    r, d = x.shape
    return (x.reshape(r // 8, 8, d) * (1.0 + sc8[None]) + sh8[None]).reshape(r, d)


def _rowscale(x, g8):
    r, d = x.shape
    return (x.reshape(r // 8, 8, d) * g8[None]).reshape(r, d)


def _sum8(x):
    r, w = x.shape
    return jnp.sum(x.reshape(r // 8, 8, w), axis=0)


def _ln_stats(r):
    mu = jnp.mean(r, axis=-1, keepdims=True)
    xc = r - mu
    var = jnp.mean(xc * xc, axis=-1, keepdims=True)
    rstd = lax.rsqrt(var + LN_EPS)
    return xc * rstd, rstd


def _ln_bwd(dxh, xhat, rstd):
    m1 = jnp.mean(dxh, axis=-1, keepdims=True)
    m2 = jnp.mean(dxh * xhat, axis=-1, keepdims=True)
    return rstd * (dxh - m1 - xhat * m2)


def _sigmoid(x):
    return 1.0 / (1.0 + jnp.exp(-x))


def _gelu(y):
    return 0.5 * y * (1.0 + lax.erf(y * (1.0 / math.sqrt(2.0))))


def _gelu_grad(y):
    return 0.5 * (1.0 + lax.erf(y * (1.0 / math.sqrt(2.0)))) + y * jnp.exp(-0.5 * y * y) * (1.0 / math.sqrt(2.0 * math.pi))


def _dot(a, b):
    return jnp.dot(a, b, preferred_element_type=F32)


def _dot_nt(a, b):
    return lax.dot_general(a, b, (((1,), (1,)), ((), ())), preferred_element_type=F32)


def _dot_tn(a, b):
    return lax.dot_general(a, b, (((0,), (0,)), ((), ())), preferred_element_type=F32)


class _Cfg:
    def __init__(self, x, ctx, mlp_w1, cv_w_dw):
        self.B, self.L, self.D = x.shape
        self.Lc = ctx.shape[1]
        assert self.B * 4 == SUBLANES, "two examples per device, four chunks each"
        self.F = mlp_w1.shape[2] * 4
        self.KW = cv_w_dw.shape[1]
        self.half = self.KW // 2
        self.G = self.D // S5_GROUP
        self.P = S5_STATE
        self.NS = self.G * self.P
        self.NO = self.D // OCT_CH
        assert self.NO % 2 == 0
        self.nx = self.L // 4
        self.nc = self.Lc // 4
        self.Tc = self.B * self.Lc
        self.Tx = self.B * self.L
        self.T = self.Tc + self.Tx
        self.TM = 256 if self.Tc % 256 == 0 else self.Tc
        assert self.Tx % self.TM == 0 and self.TM % 16 == 0
        self.HB = self.TM // 2
        assert SUBLANES * self.half <= self.HB
        self.TW = 512 if (self.Tc % 512 == 0 and self.Tx % 512 == 0) else self.TM

    def ti(self, n):
        t = 32 if self.nc % 32 == 0 else self.nc
        assert n % t == 0 and self.Tc % (8 * t) == 0
        return t

    def rows(self, ctx_too):
        return (0, self.T) if ctx_too else (self.Tc, self.Tx)


def _allgather8(name, x_shard):
    m_per, n = x_shard.shape
    assert m_per % 8 == 0

    def body(x_ref, out_ref, send_sems, recv_sems, local_sem):
        x, y, c = lax.axis_index("x"), lax.axis_index("y"), lax.axis_index("c")
        me, sibling = (x, y, c), (x, y, 1 - c)
        chips = [(1 - x, y), (x, 1 - y), (1 - x, 1 - y)]

        def rows(px, py, pc):
            return out_ref.at[pl.ds((4 * px + 2 * py + pc) * m_per, m_per), :]

        def copy(k, block, to, src=None):
            return pltpu.make_async_remote_copy(
                src_ref=rows(*block) if src is None else src, dst_ref=rows(*block),
                send_sem=send_sems.at[k], recv_sem=recv_sems.at[k], device_id=to, device_id_type=MESH)

        mine = pltpu.make_async_copy(x_ref, rows(*me), local_sem)
        mine.start()
        first = [copy(0, me, sibling, src=x_ref)]
        first += [copy(1 + j, me, (*chip, c), src=x_ref) for j, chip in enumerate(chips)]
        for cp in first:
            cp.start()
        passed = [copy(4 + j, (*chip, c), sibling) for j, chip in enumerate(chips)]
        for j, chip in enumerate(chips):
            copy(1 + j, (*chip, c), me).wait_recv()
            passed[j].start()
        copy(0, sibling, me).wait_recv()
        for j, chip in enumerate(chips):
            copy(4 + j, (*chip, 1 - c), me).wait_recv()
        for cp in first + passed:
            cp.wait_send()
        mine.wait()

    return pl.pallas_call(
        body, name=name, out_shape=_sds((8 * m_per, n), x_shard.dtype),
        in_specs=[pl.BlockSpec(memory_space=pltpu.VMEM)], out_specs=pl.BlockSpec(memory_space=pltpu.VMEM),
        scratch_shapes=[pltpu.SemaphoreType.DMA((7,)), pltpu.SemaphoreType.DMA((7,)), pltpu.SemaphoreType.DMA],
        compiler_params=pltpu.CompilerParams(vmem_limit_bytes=VMEM_LIMIT),
    )(x_shard)


def _flip(v, m):
    return v + m - 2 * v * m


def _peer(axis):
    x, y, c = lax.axis_index("x"), lax.axis_index("y"), lax.axis_index("c")
    if axis == "c":
        return (x, y, 1 - c)
    if axis == "xy":
        return (_flip(x, 1 - c), _flip(y, c), c)
    if axis == "yx":
        return (_flip(x, c), _flip(y, 1 - c), c)
    raise ValueError(axis)


def _pair_exchange(name, axis, inputs, out_shapes, aliases, plan):
    n_in = len(inputs)
    n_out = len(out_shapes)

    def body(*refs):
        ins, outs = refs[:n_in], refs[n_in:n_in + n_out]
        send_sems, recv_sems, local_sems = refs[n_in + n_out:]
        x, y, c = lax.axis_index("x"), lax.axis_index("y"), lax.axis_index("c")
        remote, local = plan(x, y, c, ins, outs)
        lcs = [pltpu.make_async_copy(s, d, local_sems.at[k]) for k, (s, d) in enumerate(local)]
        for cp in lcs:
            cp.start()
        rcs = [pltpu.make_async_remote_copy(src_ref=s, dst_ref=d, send_sem=send_sems.at[k], recv_sem=recv_sems.at[k],
                                            device_id=_peer(axis), device_id_type=MESH) for k, (s, d) in enumerate(remote)]
        for cp in rcs:
            cp.start()
        for cp in rcs:
            cp.wait()
        for cp in lcs:
            cp.wait()

    n_remote, n_local = plan.counts
    return pl.pallas_call(
        body, name=name, out_shape=tuple(out_shapes),
        in_specs=[ANY] * n_in, out_specs=tuple([ANY] * n_out),
        input_output_aliases=dict(aliases),
        scratch_shapes=[pltpu.SemaphoreType.DMA((n_remote,)), pltpu.SemaphoreType.DMA((n_remote,)),
                        pltpu.SemaphoreType.DMA((max(n_local, 1),))],
    )(*inputs)


def _plan(n_remote, n_local=0):
    def deco(fn):
        fn.counts = (n_remote, n_local)
        return fn
    return deco


HBM = pl.BlockSpec(memory_space=pltpu.HBM)
SEM = pl.BlockSpec(memory_space=pltpu.SEMAPHORE)


def _split_start(name, axis, bufs, plan):
    nb = len(bufs)
    n = plan.counts[0]

    def body(*refs):
        ins, send_sem, recv_sem, token = refs[:nb], refs[nb], refs[nb + 1], refs[-1]
        x, y, c = lax.axis_index("x"), lax.axis_index("y"), lax.axis_index("c")
        for k, (s, d) in enumerate(plan(x, y, c, ins)):
            pltpu.make_async_remote_copy(src_ref=s, dst_ref=d, send_sem=send_sem.at[k], recv_sem=recv_sem.at[k],
                                         device_id=_peer(axis), device_id_type=MESH).start()
        token[...] = jnp.zeros_like(token)

    outs = pl.pallas_call(
        body, name=name,
        out_shape=(pltpu.SemaphoreType.DMA((n,)), pltpu.SemaphoreType.DMA((n,)),
                   *[pltpu.HBM(b.shape, b.dtype) for b in bufs], _sds((8, LANES))),
        in_specs=[HBM] * nb, out_specs=(SEM, SEM, *([HBM] * nb), pl.BlockSpec(memory_space=pltpu.VMEM)),
        input_output_aliases={i: 2 + i for i in range(nb)},
        compiler_params=pltpu.CompilerParams(has_side_effects=pltpu.SideEffectType.DATAFLOW_SIDE_EFFECTING),
    )(*[pltpu.with_memory_space_constraint(b, pltpu.HBM) for b in bufs])
    return dict(name=name, axis=axis, plan=plan, send=outs[0], recv=outs[1], bufs=list(outs[2:2 + nb]), token=outs[-1])


def _split_wait(h, after):
    bufs, plan, axis = h["bufs"], h["plan"], h["axis"]
    nb = len(bufs)

    def body(*refs):
        ins, send_sem, recv_sem = refs[:nb], refs[nb], refs[nb + 1]
        x, y, c = lax.axis_index("x"), lax.axis_index("y"), lax.axis_index("c")
        for k, (s, d) in enumerate(plan(x, y, c, ins)):
            cp = pltpu.make_async_remote_copy(src_ref=s, dst_ref=d, send_sem=send_sem.at[k], recv_sem=recv_sem.at[k],
                                              device_id=_peer(axis), device_id_type=MESH)
            cp.wait_send()
            cp.wait_recv()

    outs = pl.pallas_call(
        body, name=h["name"] + "_wait", out_shape=tuple(pltpu.HBM(b.shape, b.dtype) for b in bufs),
        in_specs=[HBM] * nb + [SEM, SEM, ANY], out_specs=tuple([HBM] * nb),
        input_output_aliases={i: i for i in range(nb)},
        compiler_params=pltpu.CompilerParams(has_side_effects=pltpu.SideEffectType.DATAFLOW_SIDE_EFFECTING),
    )(*bufs, h["send"], h["recv"], after)
    return list(outs)


def _tie(small, tokens):
    for t in tokens:
        small = small + t[0, 0]
    return small


class _Overlap:
    def __init__(self):
        self.live = {}
        self.done = {}

    def add(self, key, gen):
        self.live[key] = gen
        return [next(gen)]

    def point(self, arr):
        tokens = []
        for key in list(self.live):
            try:
                tokens.append(self.live[key].send(arr))
            except StopIteration as e:
                self.done[key] = e.value
                del self.live[key]
        return tokens

    def finish(self, key, arr):
        while key in self.live:
            self.point(arr)
        return self.done.pop(key)


def _gather_gen(tag, fams):
    nf = len(fams)
    shapes = [f.shape for f in fams]
    views = [f.reshape(4, 2, -1, f.shape[-1]) for f in fams]

    @_plan(nf)
    def plan1(x, y, c, refs):
        s = 2 * x + y
        return [(refs[k].at[s, c], refs[k].at[s, c]) for k in range(nf)]

    @_plan(2 * nf)
    def plan2(x, y, c, refs):
        shards = [2 * x + y, 2 * _flip(x, 1 - c) + _flip(y, c)]
        return [(refs[k].at[s, c], refs[k].at[s, c]) for k in range(nf) for s in shards]

    @_plan(3 * nf)
    def plan3(x, y, c, refs):
        shards = [2 * (1 - x) + y, 2 * x + (1 - y), 2 * (1 - x) + (1 - y)]
        return [(refs[k].at[s, c], refs[k].at[s, c]) for k in range(nf) for s in shards]

    for rnd, (axis, plan) in enumerate((("xy", plan1), ("yx", plan2), ("c", plan3))):
        h = _split_start(f"{tag}_g{rnd}", axis, views, plan)
        after = yield h["token"]
        views = _split_wait(h, after)
    return [v.reshape(sh) for v, sh in zip(views, shapes)]


def _reduce_scatter_gen(tag, grads):
    ng = len(grads)
    flat = [g.reshape(4, 2, -1, g.shape[-1]) for g in grads]

    def empty(shape, dtype):
        return lax.empty(tuple(shape), dtype)

    @_plan(ng)
    def plan1(x, y, c, refs):
        return [(refs[k].at[:, 1 - c], refs[ng + k]) for k in range(ng)]

    h = _split_start(tag + "_r0", "c", flat + [empty((4,) + f.shape[2:], F32) for f in flat], plan1)
    after = yield h["token"]
    bufs = _split_wait(h, after)
    p1 = [_sel_add(f"{tag}_add1_{k}", bufs[k], lambda j, sc: (j, sc[2]), bufs[ng + k], True) for k in range(ng)]

    def sent1(kk, x, y, c):
        return ((1 - c) * kk + c * (1 - x), (1 - c) * (1 - y) + c * kk)

    def kept1(j, sc):
        x, y, c = sc[0], sc[1], sc[2]
        return ((1 - c) * j + c * x, (1 - c) * y + c * j)

    @_plan(2 * ng)
    def plan2(x, y, c, refs):
        return [(refs[k].at[sent1(kk, x, y, c)], refs[ng + k].at[kk]) for k in range(ng) for kk in range(2)]

    v1 = [pb.reshape(2, 2, pb.shape[1], pb.shape[2]) for p, pb in p1]
    h = _split_start(tag + "_r1", "yx", v1 + [empty((2,) + v.shape[2:], BF16) for v in v1], plan2)
    after = yield h["token"]
    bufs = _split_wait(h, after)
    p2 = [_sel_add(f"{tag}_add2_{k}", p1[k][0].reshape(2, 2, p1[k][0].shape[1], p1[k][0].shape[2]), kept1, bufs[ng + k], True)
          for k in range(ng)]

    @_plan(ng)
    def plan3(x, y, c, refs):
        return [(refs[k].at[(1 - c) * (1 - x) + c * (1 - y)], refs[ng + k]) for k in range(ng)]

    h = _split_start(tag + "_r2", "xy", [qb for q, qb in p2] + [empty(qb.shape[1:], BF16) for q, qb in p2], plan3)
    after = yield h["token"]
    bufs = _split_wait(h, after)
    fin = [_sel_add(f"{tag}_add3_{k}", p2[k][0], lambda j, sc: ((1 - sc[2]) * sc[0] + sc[2] * sc[1],), bufs[ng + k][None],
                    False, out_slots=(2, lambda j, sc: sc[2]))[0] for k in range(ng)]

    @_plan(ng)
    def plan4(x, y, c, refs):
        return [(refs[k].at[c], refs[k].at[c]) for k in range(ng)]

    h = _split_start(tag + "_r3", "c", fin, plan4)
    after = yield h["token"]
    full = _split_wait(h, after)
    return [full[k].reshape(grads[k].shape[1:]) for k in range(ng)]


def _xyc():
    return jnp.stack([lax.axis_index("x"), lax.axis_index("y"), lax.axis_index("c")]).astype(jnp.int32)


def _place_shard(name, w, fam, slot0, n_slots):
    n, kk, nn = w.shape
    kt = 256 if kk % 256 == 0 else kk

    def body(scal, w_ref, *rest):
        rest[-1][...] = w_ref[...].astype(BF16)

    in_specs = [pl.BlockSpec((None, kt, nn), lambda t, i, sc: (t, i, 0))]
    args = [_xyc(), w]
    aliases = {}
    if fam is not None:
        in_specs.append(ANY)
        args.append(fam)
        aliases = {2: 0}
    gs = pltpu.PrefetchScalarGridSpec(
        num_scalar_prefetch=1, grid=(n, kk // kt), in_specs=in_specs,
        out_specs=pl.BlockSpec((None, None, kt, nn), lambda t, i, sc: (2 * sc[0] + sc[1], slot0 + t, i, 0)))
    return pl.pallas_call(body, name=name, grid_spec=gs, out_shape=_sds((4, n_slots, kk, nn), BF16),
                          input_output_aliases=aliases, compiler_params=_cp(2))(*args)


def _gather_weights(fams):
    nf = len(fams)
    shapes = [f.shape for f in fams]
    views = [f.reshape(4, 2, -1, f.shape[-1]) for f in fams]
    outs = [_sds(v.shape, v.dtype) for v in views]
    alias = {k: k for k in range(nf)}

    @_plan(nf)
    def plan1(x, y, c, ins, outs_):
        s = 2 * x + y
        return ([(ins[k].at[s, c], outs_[k].at[s, c]) for k in range(nf)], [])

    views = _pair_exchange("gatherw_1", "xy", list(views), outs, alias, plan1)

    @_plan(2 * nf)
    def plan2(x, y, c, ins, outs_):
        shards = [2 * x + y, 2 * _flip(x, 1 - c) + _flip(y, c)]
        return ([(ins[k].at[s, c], outs_[k].at[s, c]) for k in range(nf) for s in shards], [])

    views = _pair_exchange("gatherw_2", "yx", list(views), outs, alias, plan2)

    @_plan(3 * nf)
    def plan3(x, y, c, ins, outs_):
        shards = [2 * (1 - x) + y, 2 * x + (1 - y), 2 * (1 - x) + (1 - y)]
        return ([(ins[k].at[s, c], outs_[k].at[s, c]) for k in range(nf) for s in shards], [])

    views = _pair_exchange("gatherw_c", "c", list(views), outs, alias, plan3)
    return [v.reshape(sh) for v, sh in zip(views, shapes)]


def _sel_add(name, a, a_sel, r, emit_bf16, out_slots=None):
    nr, rows, w = r.shape
    tr = 256 if rows % 256 == 0 else rows

    def body(scal, a_ref, r_ref, *outs):
        s = a_ref[...] + r_ref[...].astype(F32)
        outs[0][...] = s
        if emit_bf16:
            outs[1][...] = s.astype(BF16)

    lead = a.ndim - 2
    a_block = (None,) * lead + (tr, w)
    n_out, o_fn = out_slots if out_slots is not None else (nr, lambda j, sc: j)
    out_shape = [_sds((n_out, rows, w), F32)] + ([_sds((nr, rows, w), BF16)] if emit_bf16 else [])
    out_specs = [pl.BlockSpec((None, tr, w), lambda j, t, sc: (o_fn(j, sc), t, 0))]
    if emit_bf16:
        out_specs.append(pl.BlockSpec((None, tr, w), lambda j, t, sc: (j, t, 0)))
    gs = pltpu.PrefetchScalarGridSpec(
        num_scalar_prefetch=1, grid=(nr, rows // tr),
        in_specs=[pl.BlockSpec(a_block, lambda j, t, sc: tuple(a_sel(j, sc)) + (t, 0)),
                  pl.BlockSpec((None, tr, w), lambda j, t, sc: (j, t, 0))],
        out_specs=out_specs)
    return pl.pallas_call(body, name=name, grid_spec=gs, out_shape=out_shape, compiler_params=_cp(2))(_xyc(), a, r)


def _reduce_scatter(tag, grads):
    ng = len(grads)
    flat = [g.reshape(4, 2, -1, g.shape[-1]) for g in grads]

    @_plan(ng)
    def plan1(x, y, c, ins, outs_):
        return ([(ins[k].at[:, 1 - c], outs_[k]) for k in range(ng)], [])

    r1 = _pair_exchange(tag + "_rs_c", "c", flat, [_sds((4,) + f.shape[2:], F32) for f in flat], {}, plan1)
    p1 = [_sel_add(f"{tag}_add1_{k}", flat[k], lambda j, sc: (j, sc[2]), r1[k], True) for k in range(ng)]

    def sent1(kk, x, y, c):
        return ((1 - c) * kk + c * (1 - x), (1 - c) * (1 - y) + c * kk)

    def kept1(j, sc):
        x, y, c = sc[0], sc[1], sc[2]
        return ((1 - c) * j + c * x, (1 - c) * y + c * j)

    @_plan(2 * ng)
    def plan2(x, y, c, ins, outs_):
        return ([(ins[k].at[sent1(kk, x, y, c)], outs_[k].at[kk]) for k in range(ng) for kk in range(2)], [])

    v1 = [pb.reshape(2, 2, pb.shape[1], pb.shape[2]) for p, pb in p1]
    r2 = _pair_exchange(tag + "_rs_1", "yx", v1, [_sds((2,) + v.shape[2:], BF16) for v in v1], {}, plan2)
    p2 = [_sel_add(f"{tag}_add2_{k}", p1[k][0].reshape(2, 2, p1[k][0].shape[1], p1[k][0].shape[2]), kept1, r2[k], True)
          for k in range(ng)]

    @_plan(ng)
    def plan3(x, y, c, ins, outs_):
        return ([(ins[k].at[(1 - c) * (1 - x) + c * (1 - y)], outs_[k]) for k in range(ng)], [])

    r3 = _pair_exchange(tag + "_rs_2", "xy", [qb for q, qb in p2], [_sds(qb.shape[1:], BF16) for q, qb in p2], {}, plan3)
    fin = [_sel_add(f"{tag}_add3_{k}", p2[k][0], lambda j, sc: ((1 - sc[2]) * sc[0] + sc[2] * sc[1],), r3[k][None], False,
                    out_slots=(2, lambda j, sc: sc[2]))[0] for k in range(ng)]

    @_plan(ng)
    def plan4(x, y, c, ins, outs_):
        return ([(ins[k].at[c], outs_[k].at[c]) for k in range(ng)], [])

    full = _pair_exchange(tag + "_rs_c2", "c", fin, [_sds(f.shape, F32) for f in fin], {k: k for k in range(ng)}, plan4)
    return [full[k].reshape(grads[k].shape[1:]) for k in range(ng)]


def _allreduce8(tag, buf):
    rows, w = buf.shape
    assert rows % 16 == 0
    one = lambda: _plan(1)(lambda x, y, c, ins, outs_: ([(ins[0], outs_[0])], []))
    (got,) = _pair_exchange(f"{tag}_ar_c", "c", [buf], [_sds(buf.shape, F32)], {}, one())
    cur = _ew(f"{tag}_aradd_c", lambda a, b: a + b, [buf, got], [_sds(buf.shape, F32)])[0].reshape(2, rows // 2, w)
    mine = _plan(1)(lambda x, y, c, ins, outs_: ([(ins[0].at[c], outs_[0])], []))
    (got,) = _pair_exchange(f"{tag}_ar_1", "xy", [cur], [_sds(cur.shape[1:], F32)], {}, mine)
    (h1,) = _sel_add(f"{tag}_aradd_1", cur, lambda j, sc: (sc[2],), got[None], False)
    (got,) = _pair_exchange(f"{tag}_ar_2", "yx", [h1[0]], [_sds(h1.shape[1:], F32)], {}, one())
    (h2,) = _sel_add(f"{tag}_aradd_2", h1, lambda j, sc: (0,), got[None], False, out_slots=(2, lambda j, sc: sc[2]))
    swap = _plan(1)(lambda x, y, c, ins, outs_: ([(ins[0].at[c], outs_[0].at[c])], []))
    (full,) = _pair_exchange(f"{tag}_ar_c2", "c", [h2], [_sds(h2.shape, F32)], {0: 0}, swap)
    return full.reshape(rows, w)


def _ew(name, fn, ins, outs):
    rows, w = ins[0].shape
    tr = rows
    for cand in (512, 256, 128, 64, 32, 16, 8):
        if rows % cand == 0 and rows > cand and cand * w * 4 <= (1 << 20):
            tr = cand
            break
    n_in = len(ins)

    def body(*refs):
        vals = fn(*[r[...] for r in refs[:n_in]])
        if not isinstance(vals, (tuple, list)):
            vals = (vals,)
        for o, v in zip(refs[n_in:], vals):
            o[...] = v.astype(o.dtype)

    spec = pl.BlockSpec((tr, w), lambda i: (i, 0))
    return pl.pallas_call(body, name=name, grid=(rows // tr,), in_specs=[spec] * n_in,
                          out_specs=[spec] * len(outs), out_shape=list(outs), compiler_params=_cp(1))(*ins)


def _ew3(name, fn, ins, n_out):
    aa, bb, cc = ins[0].shape
    pad_bytes = (-(-bb // SUBLANES) * SUBLANES) * (-(-cc // LANES) * LANES) * 4
    ta = 1
    for cand in range(aa, 0, -1):
        if aa % cand == 0 and cand * pad_bytes <= (1 << 20):
            ta = cand
            break
    n_in = len(ins)

    def body(*refs):
        vals = fn(*[r[...] for r in refs[:n_in]])
        for o, v in zip(refs[n_in:], vals):
            o[...] = v

    spec = pl.BlockSpec((ta, bb, cc), lambda i: (i, 0, 0))
    return pl.pallas_call(body, name=name, grid=(aa // ta,), in_specs=[spec] * n_in, out_specs=[spec] * n_out,
                          out_shape=[_sds((aa, bb, cc))] * n_out, compiler_params=_cp(1))(*ins)


def _view_for_ew(a):
    if a.ndim == 1:
        return a.reshape(1, -1)
    if a.ndim == 2:
        return a
    if a.shape[-1] % LANES == 0 and a.shape[-2] % SUBLANES == 0:
        return a.reshape(-1, a.shape[-1])
    return a.reshape(-1, a.shape[-2], a.shape[-1])


def _adamw(name, w, g, m, v):
    def fn(w, g, m, v):
        m = ADAM_B1 * m + (1.0 - ADAM_B1) * g
        v = ADAM_B2 * v + (1.0 - ADAM_B2) * (g * g)
        m_hat = m / (1.0 - ADAM_B1 ** ADAM_STEP)
        v_hat = v / (1.0 - ADAM_B2 ** ADAM_STEP)
        delta = -ADAM_LR * (m_hat / (jnp.sqrt(v_hat) + ADAM_EPS) + ADAM_WD * w)
        return delta, m, v

    shp = w.shape
    a = [_view_for_ew(t) for t in (w, g, m, v)]
    if a[0].ndim == 3:
        o = _ew3(name, fn, a, 3)
    else:
        o = _ew(name, fn, a, [_sds(a[0].shape)] * 3)
    return tuple(t.reshape(shp) for t in o)


def _to_perm(a):
    b, ls, d = a.shape
    n = ls // 4
    return a.reshape(b * 4, n, d).swapaxes(0, 1).reshape(n * 8, d)


def _from_perm(p, b, ls):
    n = ls // 4
    return p.reshape(n, b * 4, p.shape[-1]).swapaxes(0, 1).reshape(b, ls, p.shape[-1])


def _pos_embed(rows, dim):
    def sincos(pos, d):
        quarter = d // 2
        omega = POS_TEMP ** (-jnp.arange(quarter, dtype=F32) / quarter)
        ang = pos[:, None] * omega[None, :]
        return jnp.concatenate([jnp.sin(ang), jnp.cos(ang)], axis=-1)

    row_idx = jnp.repeat(jnp.arange(rows), GRID_W).astype(F32)
    col_idx = jnp.tile(jnp.arange(GRID_W), rows).astype(F32)
    return jnp.concatenate([sincos(row_idx, dim // 2), sincos(col_idx, dim // 2)], axis=-1)


def _stream_of(cfg, off_tiles, tile_rows):
    nct = cfg.Tc // tile_rows
    return lambda i: jnp.where(i + off_tiles >= nct, 1, 0)


def _ada_fwd(c_all, w_ada, b_shard):
    nl, d, w = w_ada.shape
    tn = 512 if w % 512 == 0 else w

    def body(c_ref, w_ref, b_ref, o_ref):
        cv = c_ref[...]
        cond = (cv * _sigmoid(cv)).astype(BF16)
        o_ref[...] = _dot(cond, w_ref[...].astype(BF16)) + b_ref[...]

    return pl.pallas_call(
        body, name="ada_fwd", grid=(nl, w // tn),
        in_specs=[_full(c_all.shape), pl.BlockSpec((None, d, tn), lambda l, j: (l, 0, j)),
                  pl.BlockSpec((None, 1, tn), lambda l, j: (l, 0, j))],
        out_specs=pl.BlockSpec((None, c_all.shape[0], tn), lambda l, j: (l, 0, j)),
        out_shape=_sds((nl, c_all.shape[0], w)), compiler_params=_cp(2))(c_all, w_ada, b_shard)


def _ada_bwd(c_all, dmod_shard, w_ada):
    nl, d, w = w_ada.shape
    tn = 512 if w % 512 == 0 else w
    nr = c_all.shape[0]

    def body(c_ref, dm_ref, w_ref, gw_ref, dc_ref):
        j = pl.program_id(0) * (w // tn) + pl.program_id(1)
        cv = c_ref[...]
        cond = (cv * _sigmoid(cv)).astype(BF16)
        dm = dm_ref[...].astype(BF16)
        gw_ref[...] = _dot_tn(cond, dm)
        part = _dot_nt(dm[16:24], w_ref[...].astype(BF16))

        @pl.when(j == 0)
        def _():
            dc_ref[...] = part

        @pl.when(j > 0)
        def _():
            dc_ref[...] += part

    return pl.pallas_call(
        body, name="ada_bwd", grid=(nl, w // tn),
        in_specs=[_full(c_all.shape), pl.BlockSpec((None, nr, tn), lambda l, j: (l, 0, j)),
                  pl.BlockSpec((None, d, tn), lambda l, j: (l, 0, j))],
        out_specs=[pl.BlockSpec((None, d, tn), lambda l, j: (l, 0, j)), _full((8, d))],
        out_shape=[_sds((nl, d, w)), _sds((8, d))], compiler_params=_cp(2))(c_all, dmod_shard, w_ada)


def _disc(lr, li, ldt, br, bi):
    lr = jnp.minimum(lr, LAMBDA_RE_MAX)
    dt = jnp.exp(ldt)
    mag = jnp.exp(lr * dt)
    abr = mag * jnp.cos(li * dt)
    abi = mag * jnp.sin(li * dt)
    den = lr * lr + li * li
    nr = abr - 1.0
    ni = abi
    cr = (nr * lr + ni * li) / den
    ci = (ni * lr - nr * li) / den
    return abr, abi, cr[None] * br - ci[None] * bi, cr[None] * bi + ci[None] * br


def _disc_fwd(lr, li, ldt, br, bi):
    def body(a, b, c, d, e, o1, o2, o3, o4):
        r = _disc(a[...], b[...], c[...], d[...], e[...])
        o1[...], o2[...], o3[...], o4[...] = r

    return pl.pallas_call(body, name="s5_disc_fwd", out_shape=[_sds(lr.shape), _sds(lr.shape), _sds(br.shape), _sds(br.shape)])(
        lr, li, ldt, br, bi)


def _disc_bwd(lr, li, ldt, br, bi, g_abr, g_abi, g_bbr, g_bbi):
    def body(a, b, c, d, e, g1, g2, g3, g4, o1, o2, o3, o4, o5):
        _, vjp = jax.vjp(_disc, a[...], b[...], c[...], d[...], e[...])
        r = vjp((g1[...], g2[...], g3[...], g4[...]))
        o1[...], o2[...], o3[...], o4[...], o5[...] = r

    return pl.pallas_call(
        body, name="s5_disc_bwd",
        out_shape=[_sds(lr.shape), _sds(li.shape), _sds(ldt.shape), _sds(br.shape), _sds(bi.shape)])(
        lr, li, ldt, br, bi, g_abr, g_abi, g_bbr, g_bbi)


def _s5_layouts(cfg, lam_re, lam_im, log_dt, b_re, b_im):
    P, G = cfg.P, cfg.G
    lr = lam_re.transpose(2, 0, 1).reshape(P, 2 * G)
    li = lam_im.transpose(2, 0, 1).reshape(P, 2 * G)
    ldt = log_dt.reshape(1, 2 * G)
    br = b_re.transpose(3, 2, 0, 1).reshape(S5_GROUP, P, 2 * G)
    bi = b_im.transpose(3, 2, 0, 1).reshape(S5_GROUP, P, 2 * G)
    return lr, li, ldt, br, bi


def _coef_rows(cfg, abr, abi, conj):
    def one(t):
        return t.reshape(cfg.P, 2, cfg.G).transpose(1, 2, 0).reshape(2, cfg.NS)
    a = jnp.stack([one(abr), -one(abi) if conj else one(abi)], axis=1)
    return jnp.broadcast_to(a[:, :, None, :], (2, 2, SUBLANES, cfg.NS))


def _blockdiag_b(cfg, bbr, bbi):
    eye = jnp.eye(8, dtype=F32)

    def one(t):
        t = t.reshape(S5_GROUP, cfg.P, 2, cfg.G).transpose(2, 3, 0, 1)
        t = t.reshape(2, cfg.NO, 8, S5_GROUP, cfg.P)
        return jnp.einsum("dogcp,gh->dogchp", t, eye).reshape(2, cfg.NO, OCT_CH, OCT_ST)

    return jnp.concatenate([one(bbr), one(bbi)], axis=-1).astype(BF16)


def _blockdiag_c(cfg, c_re, c_im):
    eye = jnp.eye(8, dtype=F32)

    def one(t):
        t = t.transpose(0, 1, 3, 2).reshape(2, cfg.NO, 8, cfg.P, S5_GROUP)
        return jnp.einsum("dogpc,gh->dogphc", t, eye).reshape(2, cfg.NO, OCT_ST, OCT_CH)

    return jnp.concatenate([one(c_re), -one(c_im)], axis=2).astype(BF16)


def _diag_b(cfg, dbf):
    eye = jnp.eye(8, dtype=F32)

    def one(t):
        t = t.reshape(2, cfg.NO, 8, S5_GROUP, 8, cfg.P)
        t = jnp.einsum("dogchp,gh->dogcp", t, eye).reshape(2, cfg.G, S5_GROUP, cfg.P)
        return t.transpose(2, 3, 0, 1).reshape(S5_GROUP, cfg.P, 2 * cfg.G)

    return one(dbf[..., :OCT_ST]), one(dbf[..., OCT_ST:])


def _diag_c(cfg, dcft):
    eye = jnp.eye(8, dtype=F32)

    def one(t):
        t = t.reshape(2, cfg.NO, 8, S5_GROUP, 8, cfg.P)
        return jnp.einsum("dohcgp,gh->dogcp", t, eye).reshape(2, cfg.G, S5_GROUP, cfg.P)

    return one(dcft[..., :OCT_ST]), -one(dcft[..., OCT_ST:])


def _recur(buf, st, a_ref, n_oct, ti, rev, store):
    for o in range(0, n_oct, 2):
        cols = [(pl.ds(oo * 2 * OCT_ST, OCT_ST), pl.ds(oo * 2 * OCT_ST + OCT_ST, OCT_ST)) for oo in (o, o + 1)]
        scol = [pl.ds(oo * OCT_ST, OCT_ST) for oo in (o, o + 1)]
        coef = [(a_ref[0, :, sc], a_ref[1, :, sc]) for sc in scol]
        init = (st[0, :, scol[0]], st[1, :, scol[0]], st[0, :, scol[1]], st[1, :, scol[1]])

        def step(i4, carry, cols=cols, coef=coef):
            carry = list(carry)
            for q in range(unroll):
                i = i4 * unroll + q
                r = pl.multiple_of((i + rev * (ti - 1 - 2 * i)) * 8, 8)
                for s in range(2):
                    sr, si = carry[2 * s], carry[2 * s + 1]
                    ar, ai = coef[s]
                    zr = buf[pl.ds(r, 8), cols[s][0]]
                    zi = buf[pl.ds(r, 8), cols[s][1]]
                    nr = ar * sr - ai * si + zr
                    ni = ar * si + ai * sr + zi
                    if store:
                        buf[pl.ds(r, 8), cols[s][0]] = nr
                        buf[pl.ds(r, 8), cols[s][1]] = ni
                    carry[2 * s], carry[2 * s + 1] = nr, ni
            return tuple(carry)

        unroll = 4 if ti % 4 == 0 else 1
        fin = lax.fori_loop(0, ti // unroll, step, init)
        st[0, :, scol[0]] = fin[0]
        st[1, :, scol[0]] = fin[1]
        st[0, :, scol[1]] = fin[2]
        st[1, :, scol[1]] = fin[3]


def _s5_fwd_pass(cfg, name, tok, mod8, col_sh, col_sc, bf, acoef, r0, n, s_init=None, cf=None, y_prev=None):
    D, NO, NS = cfg.D, cfg.NO, cfg.NS
    ti = cfg.ti(n)
    nb = n // ti
    R = 8 * ti
    ob = r0 // R
    second = s_init is not None
    blk = lambda d, j: ob + j + d * (nb - 1 - 2 * j)

    def body(*refs):
        if second:
            tok_ref, mod_ref, bf_ref, a_ref, si_ref, cf_ref, yp_ref, y_ref, ck_ref, fin_ref, zbuf, st = refs
        else:
            tok_ref, mod_ref, bf_ref, a_ref, fin_ref, zbuf, st = refs
        d = pl.program_id(0)
        j = pl.program_id(1)

        @pl.when(j == 0)
        def _():
            if second:
                st[...] = si_ref[...]
            else:
                st[...] = jnp.zeros_like(st)

        if second:
            ck_ref[...] = st[...]
        u = _mod(tok_ref[...], mod_ref[:, col_sh:col_sh + D], mod_ref[:, col_sc:col_sc + D]).astype(BF16)
        for o in range(NO):
            zbuf[:, o * 1024:(o + 1) * 1024] = _dot(u[:, o * OCT_CH:(o + 1) * OCT_CH], bf_ref[o])
        _recur(zbuf, st, a_ref, NO, ti, d, second)
        if second:
            for o in range(NO):
                y_ref[:, o * OCT_CH:(o + 1) * OCT_CH] = _dot(zbuf[:, o * 1024:(o + 1) * 1024].astype(BF16), cf_ref[o])

        @pl.when(j == nb - 1)
        def _():
            fin_ref[...] = st[...]

    st_spec = pl.BlockSpec((None, 2, 8, NS), lambda d, j: (d, 0, 0, 0))
    in_specs = [pl.BlockSpec((R, D), lambda d, j: (blk(d, j), 0)), _full(mod8.shape),
                pl.BlockSpec((None, NO, OCT_CH, 1024), lambda d, j: (d, 0, 0, 0)), st_spec]
    args = [tok, mod8, bf, acoef]
    scratch = [pltpu.VMEM((R, NO * 1024), F32), pltpu.VMEM((2, 8, NS), F32)]
    if not second:
        return pl.pallas_call(body, name=name, grid=(2, nb), in_specs=in_specs, out_specs=st_spec,
                              out_shape=_sds((2, 2, 8, NS)), scratch_shapes=scratch, compiler_params=_cp(2))(*args)
    in_specs += [st_spec, pl.BlockSpec((None, NO, 1024, OCT_CH), lambda d, j: (d, 0, 0, 0))]
    args += [s_init, cf]
    aliases = {}
    if y_prev is not None:
        in_specs.append(ANY)
        args.append(y_prev)
        aliases = {6: 0}
    else:
        in_specs.append(_full((8, LANES)))
        args.append(jnp.zeros((8, LANES), F32))
    out_specs = [pl.BlockSpec((None, R, D), lambda d, j: (d, blk(d, j), 0)),
                 pl.BlockSpec((None, None, 2, 8, NS), lambda d, j: (d, j + d * (nb - 1 - 2 * j), 0, 0, 0)), st_spec]
    out_shape = [_sds((2, cfg.T, D)), _sds((2, nb, 2, 8, NS)), _sds((2, 2, 8, NS))]
    return pl.pallas_call(body, name=name, grid=(2, nb), in_specs=in_specs, out_specs=out_specs, out_shape=out_shape,
                          input_output_aliases=aliases, scratch_shapes=scratch, compiler_params=_cp(2))(*args)


def _s5_chain(cfg, name, fin_local, acoef, n, inc, prev_fin=None):
    NS = cfg.NS
    nsq = int(round(math.log2(n)))
    assert 2 ** nsq == n

    def body(*refs):
        if prev_fin is not None:
            f_ref, a_ref, p_ref, o_ref = refs
        else:
            f_ref, a_ref, o_ref = refs
        for d in range(2):
            pr, pi = a_ref[d, 0, 0:1, :], a_ref[d, 1, 0:1, :]
            for _ in range(nsq):
                pr, pi = pr * pr - pi * pi, 2.0 * pr * pi
            for b in range(2):
                order = [4 * b + k for k in range(4)]
                if not inc[d]:
                    order = order[::-1]
                if prev_fin is not None:
                    last = order[-1]
                    sr, si = p_ref[d, 0, last:last + 1, :], p_ref[d, 1, last:last + 1, :]
                else:
                    sr = jnp.zeros((1, NS), F32)
                    si = jnp.zeros((1, NS), F32)
                for k in order:
                    o_ref[d, 0, k:k + 1, :] = sr
                    o_ref[d, 1, k:k + 1, :] = si
                    fr, fi = f_ref[d, 0, k:k + 1, :], f_ref[d, 1, k:k + 1, :]
                    sr, si = pr * sr - pi * si + fr, pr * si + pi * sr + fi

    args = [fin_local, acoef] + ([prev_fin] if prev_fin is not None else [])
    return pl.pallas_call(body, name=name, out_shape=_sds((2, 2, 8, NS)))(*args)


def _s5_forward(cfg, tag, tok, mod8, col_sh, col_sc, bf, cf, acoef, point=None):
    saved = {}
    fin_prev = None
    y = None
    for ph, (r0, n) in (("c", (0, cfg.nc)), ("x", (cfg.Tc, cfg.nx))):
        m8 = mod8[0 if ph == "c" else 1]
        loc = _s5_fwd_pass(cfg, f"{tag}_scan1{ph}", tok, m8, col_sh, col_sc, bf, acoef, r0, n)
        if point is not None and ph == "x":
            m8 = _tie(m8, point(loc))
        s_in = _s5_chain(cfg, f"{tag}_chain{ph}", loc, acoef, n, (True, False), fin_prev)
        y, ck, fin_prev = _s5_fwd_pass(cfg, f"{tag}_scan2{ph}", tok, m8, col_sh, col_sc, bf, acoef, r0, n, s_in, cf, y)
        saved[ph] = ck
    return y, saved


def _s5_bwd_pass(cfg, name, dy, tok, mod8, col_sh, col_sc, bf, cf, acoef, acoef_adj, r0, n, g_init=None, ck=None,
                 du_prev=None):
    D, NO, NS = cfg.D, cfg.NO, cfg.NS
    ti = cfg.ti(n)
    nb = n // ti
    R = 8 * ti
    ob = r0 // R
    second = g_init is not None
    has_dy = dy is not None
    blk = lambda d, j: ob + j + (1 - d) * (nb - 1 - 2 * j)

    def body(*refs):
        refs = list(refs)
        dy_ref = refs.pop(0) if has_dy else None
        if second:
            (tok_ref, mod_ref, bf_ref, cf_ref, a_ref, aa_ref, gi_ref, ck_ref, dup_ref,
             du_ref, da_ref, dbf_ref, dcf_ref, gfin_ref, qbuf, zbuf, gst, hst) = refs
        else:
            cf_ref, aa_ref, gfin_ref, qbuf, gst = refs
        d = pl.program_id(0)
        j = pl.program_id(1)

        @pl.when(j == 0)
        def _():
            if second:
                gst[...] = gi_ref[...]
                da_ref[...] = jnp.zeros_like(da_ref)
                dbf_ref[...] = jnp.zeros_like(dbf_ref)
                dcf_ref[...] = jnp.zeros_like(dcf_ref)
            else:
                gst[...] = jnp.zeros_like(gst)

        if has_dy:
            dyb = dy_ref[...].astype(BF16)
            for o in range(NO):
                qbuf[:, o * 1024:(o + 1) * 1024] = _dot_nt(dyb[:, o * OCT_CH:(o + 1) * OCT_CH], cf_ref[o])
        else:
            qbuf[...] = jnp.zeros_like(qbuf)
        _recur(qbuf, gst, aa_ref, NO, ti, 1 - d, second)

        if second:
            u = _mod(tok_ref[...], mod_ref[:, col_sh:col_sh + D], mod_ref[:, col_sc:col_sc + D]).astype(BF16)
            for o in range(NO):
                zbuf[:, o * 1024:(o + 1) * 1024] = _dot(u[:, o * OCT_CH:(o + 1) * OCT_CH], bf_ref[o])
            hst[...] = ck_ref[...]
            _recur(zbuf, hst, a_ref, NO, ti, d, True)

            g_off, h_off = (1 - d) * 8, d * 8
            edge = pl.multiple_of(d * (R - 8), 8)
            for o in range(0, NO, 2):
                cols = [(pl.ds(oo * 1024, OCT_ST), pl.ds(oo * 1024 + OCT_ST, OCT_ST)) for oo in (o, o + 1)]
                scol = [pl.ds(oo * OCT_ST, OCT_ST) for oo in (o, o + 1)]
                init = []
                for s in range(2):
                    er, ei = qbuf[pl.ds(edge, 8), cols[s][0]], qbuf[pl.ds(edge, 8), cols[s][1]]
                    kr, ki = ck_ref[0, :, scol[s]], ck_ref[1, :, scol[s]]
                    init += [er * kr + ei * ki, ei * kr - er * ki]

                def stp(i, carry, cols=cols):
                    rg = pl.multiple_of(i * 8 + g_off, 8)
                    rh = pl.multiple_of(i * 8 + h_off, 8)
                    out = []
                    for s in range(2):
                        gr, gi = qbuf[pl.ds(rg, 8), cols[s][0]], qbuf[pl.ds(rg, 8), cols[s][1]]
                        hr, hi = zbuf[pl.ds(rh, 8), cols[s][0]], zbuf[pl.ds(rh, 8), cols[s][1]]
                        out += [carry[2 * s] + (gr * hr + gi * hi), carry[2 * s + 1] + (gi * hr - gr * hi)]
                    return tuple(out)

                fin = lax.fori_loop(0, ti - 1, stp, tuple(init))
                for s in range(2):
                    da_ref[0, :, scol[s]] += fin[2 * s]
                    da_ref[1, :, scol[s]] += fin[2 * s + 1]

            for o in range(NO):
                gb = qbuf[:, o * 1024:(o + 1) * 1024].astype(BF16)
                uo = u[:, o * OCT_CH:(o + 1) * OCT_CH]
                dbf_ref[o] += _dot_tn(uo, gb)
                if has_dy:
                    dcf_ref[o] += _dot_tn(dyb[:, o * OCT_CH:(o + 1) * OCT_CH], zbuf[:, o * 1024:(o + 1) * 1024].astype(BF16))
                du_ref[:, o * OCT_CH:(o + 1) * OCT_CH] = _dot_nt(gb, bf_ref[o])

        @pl.when(j == nb - 1)
        def _():
            gfin_ref[...] = gst[...]

    st_spec = pl.BlockSpec((None, 2, 8, NS), lambda d, j: (d, 0, 0, 0))
    row_spec = pl.BlockSpec((R, D), lambda d, j: (blk(d, j), 0))
    bf_spec = pl.BlockSpec((None, NO, OCT_CH, 1024), lambda d, j: (d, 0, 0, 0))
    cf_spec = pl.BlockSpec((None, NO, 1024, OCT_CH), lambda d, j: (d, 0, 0, 0))
    in_specs, args = [], []
    if has_dy:
        in_specs.append(row_spec)
        args.append(dy)
    if not second:
        in_specs += [cf_spec, st_spec]
        args += [cf, acoef_adj]
        return pl.pallas_call(body, name=name, grid=(2, nb), in_specs=in_specs, out_specs=st_spec,
                              out_shape=_sds((2, 2, 8, NS)),
                              scratch_shapes=[pltpu.VMEM((R, NO * 1024), F32), pltpu.VMEM((2, 8, NS), F32)],
                              compiler_params=_cp(2))(*args)
    ck_spec = pl.BlockSpec((None, None, 2, 8, NS), lambda d, j: (d, j + (1 - d) * (nb - 1 - 2 * j), 0, 0, 0))
    in_specs += [row_spec, _full(mod8.shape), bf_spec, cf_spec, st_spec, st_spec, st_spec, ck_spec]
    args += [tok, mod8, bf, cf, acoef, acoef_adj, g_init, ck]
    n_before = len(args)
    aliases = {}
    if du_prev is not None:
        in_specs.append(ANY)
        args.append(du_prev)
        aliases = {n_before: 0}
    else:
        in_specs.append(_full((8, LANES)))
        args.append(jnp.zeros((8, LANES), F32))
    out_specs = [pl.BlockSpec((None, R, D), lambda d, j: (d, blk(d, j), 0)), st_spec, bf_spec, bf_spec, st_spec]
    out_shape = [_sds((2, cfg.T, D)), _sds((2, 2, 8, NS)), _sds((2, NO, OCT_CH, 1024)), _sds((2, NO, OCT_CH, 1024)),
                 _sds((2, 2, 8, NS))]
    scratch = [pltpu.VMEM((R, NO * 1024), F32), pltpu.VMEM((R, NO * 1024), F32), pltpu.VMEM((2, 8, NS), F32),
               pltpu.VMEM((2, 8, NS), F32)]
    return pl.pallas_call(body, name=name, grid=(2, nb), in_specs=in_specs, out_specs=out_specs, out_shape=out_shape,
                          input_output_aliases=aliases, scratch_shapes=scratch, compiler_params=_cp(2))(*args)


def _s5_backward(cfg, tag, dy, dy_ctx, tok, mod8, col_sh, col_sc, bf, cf, acoef, acoef_adj, saved):
    g_prev = None
    acc = None
    du = None
    for ph, (r0, n) in (("x", (cfg.Tc, cfg.nx)), ("c", (0, cfg.nc))):
        m8 = mod8[0 if ph == "c" else 1]
        dyp = dy if (ph == "x" or dy_ctx) else None
        loc = _s5_bwd_pass(cfg, f"{tag}_adjA{ph}", dyp, tok, m8, col_sh, col_sc, bf, cf, acoef, acoef_adj, r0, n)
        g_in = _s5_chain(cfg, f"{tag}_adjchain{ph}", loc, acoef_adj, n, (False, True), g_prev)
        du, da, dbf, dcf, g_prev = _s5_bwd_pass(cfg, f"{tag}_adjB{ph}", dyp, tok, m8, col_sh, col_sc, bf, cf, acoef,
                                                acoef_adj, r0, n, g_in, saved[ph], du)
        new = (da, dbf, dcf)
        if acc is None:
            acc = new
        else:
            acc = tuple(_ew(f"{tag}_accsum{q}", lambda a, b: a + b, [a.reshape(-1, a.shape[-1]), b.reshape(-1, b.shape[-1])],
                            [_sds((a.size // a.shape[-1], a.shape[-1]))])[0].reshape(a.shape)
                        for q, (a, b) in enumerate(zip(acc, new)))
    return du, acc


def _tok_specs(cfg, rows, width, tile=None):
    tm = tile or cfg.TM
    off = rows[0] // tm
    return pl.BlockSpec((tm, width), lambda i: (i + off, 0)), rows[1] // tm, off


def _mod_spec(cfg, mod8, off):
    st = _stream_of(cfg, off, cfg.TM)
    return pl.BlockSpec((None, 8, mod8.shape[-1]), lambda i: (st(i), 0, 0))


def _wspec(w):
    fam, slot = w
    _, _, kk, nn = fam.shape
    return pl.BlockSpec((4, None, kk, nn), lambda *i: (0, slot, 0, 0), pipeline_mode=pl.Buffered(1))


def _glu_ln(cfg, name, rows, tok, y, mod8, cols, dskip, w, b, gain, bias):
    D, TM = cfg.D, cfg.TM
    csh, csc, cg = cols
    spec, nt, off = _tok_specs(cfg, rows, D)
    spec2, _, _ = _tok_specs(cfg, rows, 2 * D)

    def body(tok_ref, y_ref, mod_ref, ds_ref, w_ref, b_ref, g_ref, bi_ref, x1_ref, r1_ref, mix_ref, zz_ref, zb_ref, yy_ref):
        t = tok_ref[...]
        u = _mod(t, mod_ref[:, csh:csh + D], mod_ref[:, csc:csc + D])
        yy = ds_ref[...] * u + y_ref[0] + y_ref[1]
        zb = _gelu(yy).astype(BF16)
        zz = jnp.concatenate([_dot(zb, w_ref[s]) for s in range(4)], axis=-1) + b_ref[...]
        mix = zz[:, :D] * _sigmoid(zz[:, D:])
        r1 = DN_ALPHA * t + _rowscale(mix, mod_ref[:, cg:cg + D])
        xhat, _ = _ln_stats(r1)
        x1_ref[...] = xhat * g_ref[...] + bi_ref[...]
        r1_ref[...] = r1
        mix_ref[...] = mix
        zz_ref[...] = zz
        zb_ref[...] = zb
        yy_ref[...] = yy

    T = cfg.T
    return pl.pallas_call(
        body, name=name, grid=(nt,),
        in_specs=[spec, pl.BlockSpec((2, TM, D), lambda i: (0, i + off, 0)), _mod_spec(cfg, mod8, off), _full((1, D)),
                  _wspec(w), _full((1, 2 * D)), _full((1, D)), _full((1, D))],
        out_specs=[spec, spec, spec, spec2, spec, spec],
        out_shape=[_sds((T, D)), _sds((T, D)), _sds((T, D)), _sds((T, 2 * D)), _sds((T, D), BF16), _sds((T, D))],
        compiler_params=_cp(1))(tok, y, mod8, dskip, w[0], b, gain, bias)


def _mlp_ln(cfg, name, rows, x1, mod8, cols, w1, w2, gain, bias):
    D, TM = cfg.D, cfg.TM
    csh, csc, cg = cols
    spec, nt, off = _tok_specs(cfg, rows, D)
    specf, _, _ = _tok_specs(cfg, rows, cfg.F)
    fb = cfg.F // 4

    def body(x_ref, mod_ref, w1_ref, w2_ref, g_ref, bi_ref, x2_ref, r2_ref, out_ref, a_ref, h_ref):
        t = x_ref[...]
        h = _mod(t, mod_ref[:, csh:csh + D], mod_ref[:, csc:csc + D]).astype(BF16)
        out = jnp.zeros((TM, D), F32)
        for s in range(4):
            hid = jnp.maximum(_dot(h, w1_ref[s]), 0.0)
            a = (hid * hid).astype(BF16)
            a_ref[:, s * fb:(s + 1) * fb] = a
            out = out + _dot(a, w2_ref[s])
        r2 = DN_ALPHA * t + _rowscale(out, mod_ref[:, cg:cg + D])
        xhat, _ = _ln_stats(r2)
        x2_ref[...] = xhat * g_ref[...] + bi_ref[...]
        r2_ref[...] = r2
        out_ref[...] = out
        h_ref[...] = h

    T = cfg.T
    return pl.pallas_call(
        body, name=name, grid=(nt,),
        in_specs=[spec, _mod_spec(cfg, mod8, off), _wspec(w1), _wspec(w2), _full((1, D)), _full((1, D))],
        out_specs=[spec, spec, spec, specf, spec],
        out_shape=[_sds((T, D)), _sds((T, D)), _sds((T, D)), _sds((T, cfg.F), BF16), _sds((T, D), BF16)],
        compiler_params=_cp(1))(x1, mod8, w1[0], w2[0], gain, bias)


def _pw1_glu(cfg, name, rows, tok, mod8, cols, w, b):
    D, TM = cfg.D, cfg.TM
    csh, csc = cols
    spec, nt, off = _tok_specs(cfg, rows, D)
    spec2, _, _ = _tok_specs(cfg, rows, 2 * D)

    def body(tok_ref, mod_ref, w_ref, b_ref, aa_ref, ag_ref, h_ref):
        h = _mod(tok_ref[...], mod_ref[:, csh:csh + D], mod_ref[:, csc:csc + D]).astype(BF16)
        aa = jnp.concatenate([_dot(h, w_ref[s]) for s in range(4)], axis=-1) + b_ref[...]
        aa_ref[...] = aa
        ag_ref[...] = aa[:, :D] * _sigmoid(aa[:, D:])
        h_ref[...] = h

    T = cfg.T
    return pl.pallas_call(
        body, name=name, grid=(nt,),
        in_specs=[spec, _mod_spec(cfg, mod8, off), _wspec(w), _full((1, 2 * D))],
        out_specs=[spec2, spec, spec],
        out_shape=[_sds((T, 2 * D)), _sds((T, D)), _sds((T, D), BF16)], compiler_params=_cp(1))(tok, mod8, w[0], b)


def _halo_maps(cfg, rows):
    TM, HB = cfg.TM, cfg.HB
    off = rows[0] // TM
    nct = cfg.Tc // TM
    ntx = cfg.Tx // TM

    def phase(i):
        t = i + off
        is_x = t >= nct
        first = jnp.where(is_x, nct, 0)
        cnt = jnp.where(is_x, ntx, nct)
        return t, first, cnt

    def prev(i):
        t, first, cnt = phase(i)
        return jnp.where(t == first, 2 * (first + cnt) - 1, 2 * t - 1), 0

    def nxt(i):
        t, first, cnt = phase(i)
        return jnp.where(t == first + cnt - 1, 2 * first, 2 * t + 2), 0

    def edge(i):
        t, first, cnt = phase(i)
        return t == first, t == first + cnt - 1

    return prev, nxt, edge, off


def _halo_fix(prev, nxt, is_first, is_last):
    hb, d = prev.shape
    k = lax.broadcasted_iota(jnp.int32, (hb // 8, 8, d), 1)
    p3 = prev.reshape(hb // 8, 8, d)
    n3 = nxt.reshape(hb // 8, 8, d)
    p_roll = jnp.where((k % 4) == 0, 0.0, pltpu.roll(p3, 1, 1))
    n_roll = jnp.where((k % 4) == 3, 0.0, pltpu.roll(n3, 7, 1))
    p3 = jnp.where(is_first, p_roll, p3)
    n3 = jnp.where(is_last, n_roll, n3)
    return p3.reshape(hb, d), n3.reshape(hb, d)


def _dwconv_ln(cfg, name, rows, ag, w_dw, b_dw, ln_g, ln_b):
    D, TM, HB, KW, half = cfg.D, cfg.TM, cfg.HB, cfg.KW, cfg.half
    prev_map, next_map, edge, off = _halo_maps(cfg, rows)
    spec, nt, _ = _tok_specs(cfg, rows, D)

    def body(cur_ref, prev_ref, next_ref, w_ref, b_ref, g_ref, bi_ref, cv_ref, s_ref, ext):
        i = pl.program_id(0)
        is_first, is_last = edge(i)

        @pl.when(i >= 0)
        def _():
            p, n = _halo_fix(prev_ref[...], next_ref[...], is_first, is_last)
            ext[0:HB, :] = p
            ext[HB:HB + TM, :] = cur_ref[...]
            ext[HB + TM:, :] = n

        acc = jnp.zeros((TM, D), F32)
        for k in range(KW):
            lo = HB + 8 * (k - half)
            acc = acc + w_ref[k:k + 1, :] * ext[lo:lo + TM, :]
        cv_ref[...] = acc + b_ref[...]
        xhat, _ = _ln_stats(cv_ref[...])
        nn = xhat * g_ref[...] + bi_ref[...]
        s_ref[...] = (nn * _sigmoid(nn)).astype(BF16)

    T = cfg.T
    return pl.pallas_call(
        body, name=name, grid=(nt,),
        in_specs=[spec, pl.BlockSpec((HB, D), prev_map), pl.BlockSpec((HB, D), next_map), _full((KW, D)),
                  _full((1, D)), _full((1, D)), _full((1, D))],
        out_specs=[spec, spec], out_shape=[_sds((T, D)), _sds((T, D), BF16)],
        scratch_shapes=[pltpu.VMEM((TM + 2 * HB, D), F32)], compiler_params=_cp(1))(ag, ag, ag, w_dw, b_dw, ln_g, ln_b)


def _pw2_ln(cfg, name, rows, s, tok, mod8, cg, w, b, gain, bias):
    D, TM = cfg.D, cfg.TM
    spec, nt, off = _tok_specs(cfg, rows, D)
    kb = D // 4

    def body(s_ref, tok_ref, mod_ref, w_ref, b_ref, g_ref, bi_ref, x1_ref, r1_ref, mix_ref):
        sv = s_ref[...]
        mix = b_ref[...] + jnp.zeros((TM, D), F32)
        for q in range(4):
            mix = mix + _dot(sv[:, q * kb:(q + 1) * kb], w_ref[q])
        r1 = DN_ALPHA * tok_ref[...] + _rowscale(mix, mod_ref[:, cg:cg + D])
        xhat, _ = _ln_stats(r1)
        x1_ref[...] = xhat * g_ref[...] + bi_ref[...]
        r1_ref[...] = r1
        mix_ref[...] = mix

    T = cfg.T
    return pl.pallas_call(
        body, name=name, grid=(nt,),
        in_specs=[spec, spec, _mod_spec(cfg, mod8, off), _wspec(w), _full((1, D)), _full((1, D)), _full((1, D))],
        out_specs=[spec, spec, spec], out_shape=[_sds((T, D))] * 3, compiler_params=_cp(1))(s, tok, mod8, w[0], b, gain, bias)


def _loss(cfg, xf, tgt):
    D, TM = cfg.D, cfg.TM
    spec, nt, off = _tok_specs(cfg, cfg.rows(False), D)

    def body(x_ref, t_ref, l_ref, dx_ref, acc):
        i = pl.program_id(0)
        dlt = x_ref[...] - t_ref[...]

        @pl.when(i == 0)
        def _():
            acc[...] = jnp.zeros_like(acc)

        acc[...] += _sum8(dlt * dlt)
        dx_ref[...] = dlt * (1.0 / D)

        @pl.when(i == nt - 1)
        def _():
            l_ref[...] = jnp.zeros((8, LANES), F32) + jnp.sum(acc[...]) * (0.5 / D)

    return pl.pallas_call(
        body, name="loss", grid=(nt,),
        in_specs=[spec, pl.BlockSpec((TM, D), lambda i: (i, 0))],
        out_specs=[_full((8, LANES)), spec], out_shape=[_sds((8, LANES)), _sds((cfg.T, D))],
        scratch_shapes=[pltpu.VMEM((8, D), F32)], compiler_params=_cp(1))(xf, tgt)


def _masked_spec(cfg, rows, width, valid_from_tile):
    tm = cfg.TM
    off = rows[0] // tm
    return pl.BlockSpec((tm, width), lambda i: (jnp.maximum(i + off, valid_from_tile), 0))


def _lnb(cfg, name, rows, dres, dres_ctx_ok, dh, r, aux, gain, mod_gate, cg, mod_next, csc):
    D, TM = cfg.D, cfg.TM
    spec, nt, off = _tok_specs(cfg, rows, D)
    nct = cfg.Tc // TM
    has_dres, has_dh = dres is not None, dh is not None

    def body(*refs):
        refs = list(refs)
        dres_ref = refs.pop(0) if has_dres else None
        dh_ref = refs.pop(0) if has_dh else None
        r_ref, aux_ref, g_ref, mg_ref = refs[:4]
        refs = refs[4:]
        mn_ref = refs.pop(0) if has_dh else None
        dprev_ref, dbr_ref, dgain_ref, dbias_ref, dg_ref, dsc_ref, dsh_ref, acc_g, acc_b = refs
        i = pl.program_id(0)
        t = i + off
        first_of_stream = (i == 0) | (t == nct)
        xhat, rstd = _ln_stats(r_ref[...])
        dy = jnp.zeros((TM, D), F32)
        if has_dres:
            dv = dres_ref[...]
            if not dres_ctx_ok:
                dv = jnp.where(t >= nct, dv, 0.0)
            dy = dy + dv
        if has_dh:
            dhv = dh_ref[...]
            dy = dy + _rowscale(dhv, 1.0 + mn_ref[:, csc:csc + D])
            x_out = xhat * g_ref[0:1, :] + g_ref[1:2, :]
            s_sc, s_sh = _sum8(dhv * x_out), _sum8(dhv)
        else:
            s_sc = s_sh = jnp.zeros((8, D), F32)
        dr = _ln_bwd(dy * g_ref[0:1, :], xhat, rstd)
        s_g = _sum8(dr * aux_ref[...])

        @pl.when(i == 0)
        def _():
            acc_g[...] = jnp.zeros_like(acc_g)
            acc_b[...] = jnp.zeros_like(acc_b)

        acc_g[...] += _sum8(dy * xhat)
        acc_b[...] += _sum8(dy)

        @pl.when(first_of_stream)
        def _():
            dg_ref[...] = s_g
            dsc_ref[...] = s_sc
            dsh_ref[...] = s_sh

        @pl.when(jnp.logical_not(first_of_stream))
        def _():
            dg_ref[...] += s_g
            dsc_ref[...] += s_sc
            dsh_ref[...] += s_sh

        dprev_ref[...] = DN_ALPHA * dr
        dbr_ref[...] = _rowscale(dr, mg_ref[:, cg:cg + D])

        @pl.when(i == nt - 1)
        def _():
            dgain_ref[...] = jnp.sum(acc_g[...], axis=0, keepdims=True)
            dbias_ref[...] = jnp.sum(acc_b[...], axis=0, keepdims=True)

    st = _stream_of(cfg, off, TM)
    in_specs, args = [], []
    if has_dres:
        in_specs.append(spec if dres_ctx_ok else _masked_spec(cfg, rows, D, nct))
        args.append(dres)
    if has_dh:
        in_specs.append(spec)
        args.append(dh)
    in_specs += [spec, spec, _full((2, D)), _mod_spec(cfg, mod_gate, off)]
    args += [r, aux, gain, mod_gate]
    if has_dh:
        in_specs.append(_mod_spec(cfg, mod_next, off))
        args.append(mod_next)
    acc_spec = pl.BlockSpec((None, 8, D), lambda i: (st(i), 0, 0))
    T = cfg.T
    return pl.pallas_call(
        body, name=name, grid=(nt,), in_specs=in_specs,
        out_specs=[spec, spec, _full((1, D)), _full((1, D)), acc_spec, acc_spec, acc_spec],
        out_shape=[_sds((T, D)), _sds((T, D)), _sds((1, D)), _sds((1, D)), _sds((2, 8, D)), _sds((2, 8, D)), _sds((2, 8, D))],
        scratch_shapes=[pltpu.VMEM((8, D), F32), pltpu.VMEM((8, D), F32)], compiler_params=_cp(1))(*args)


def _mlp_bwd(cfg, name, rows, dbr, a, w1, w2):
    D, TM = cfg.D, cfg.TM
    spec, nt, off = _tok_specs(cfg, rows, D)
    specf, _, _ = _tok_specs(cfg, rows, cfg.F)
    fb = cfg.F // 4

    def body(d_ref, a_ref, w1_ref, w2_ref, dh_ref, dhid_ref, dout_ref):
        dout = d_ref[...].astype(BF16)
        dh = jnp.zeros((TM, D), F32)
        for s in range(4):
            da = _dot_nt(dout, w2_ref[s])
            dhid = (da * (2.0 * jnp.sqrt(a_ref[:, s * fb:(s + 1) * fb].astype(F32)))).astype(BF16)
            dhid_ref[:, s * fb:(s + 1) * fb] = dhid
            dh = dh + _dot_nt(dhid, w1_ref[s])
        dh_ref[...] = dh
        dout_ref[...] = dout

    T = cfg.T
    return pl.pallas_call(
        body, name=name, grid=(nt,),
        in_specs=[spec, specf, _wspec(w1), _wspec(w2)],
        out_specs=[spec, specf, spec],
        out_shape=[_sds((T, D)), _sds((T, cfg.F), BF16), _sds((T, D), BF16)], compiler_params=_cp(1))(dbr, a, w1[0], w2[0])


def _wgrad(cfg, name, rows, a, b, mode, fam, slot):
    tw = cfg.TW
    off = rows[0] // tw
    nt = rows[1] // tw
    fresh = not hasattr(fam, "dtype")
    fam_shape = tuple(fam) if fresh else fam.shape
    _, n, kk, nn = fam_shape

    def body(a_ref, b_ref, *rest):
        o_ref = rest[-1]
        t = pl.program_id(1)
        part = _dot_tn(a_ref[...], b_ref[...])

        @pl.when(t == 0)
        def _():
            o_ref[...] = part

        @pl.when(t > 0)
        def _():
            o_ref[...] += part

    if mode == "col":
        a_spec = pl.BlockSpec((tw, kk), lambda s, t: (t + off, 0))
        b_spec = pl.BlockSpec((tw, nn), lambda s, t: (t + off, s))
    else:
        a_spec = pl.BlockSpec((tw, kk), lambda s, t: (t + off, s))
        b_spec = pl.BlockSpec((tw, nn), lambda s, t: (t + off, 0))
    out_spec = pl.BlockSpec((None, None, kk, nn), lambda s, t: (s, slot, 0, 0))
    if fresh:
        return pl.pallas_call(body, name=name, grid=(4, nt), in_specs=[a_spec, b_spec], out_specs=out_spec,
                              out_shape=_sds(fam_shape), compiler_params=_cp(2))(a, b)
    return pl.pallas_call(body, name=name, grid=(4, nt), in_specs=[a_spec, b_spec, ANY], out_specs=out_spec,
                          out_shape=_sds(fam_shape), input_output_aliases={2: 0}, compiler_params=_cp(2))(a, b, fam)


def _glu_bwd(cfg, name, rows, dmix, pre, w, yy=None):
    D, TM = cfg.D, cfg.TM
    spec, nt, off = _tok_specs(cfg, rows, D)
    spec2, _, _ = _tok_specs(cfg, rows, 2 * D)
    hw = w[0].shape[-1]
    has_y = yy is not None

    def body(*refs):
        refs = list(refs)
        d_ref, p_ref, w_ref = refs[:3]
        y_ref = refs[3] if has_y else None
        dz_ref, dp_ref, db_ref, acc = refs[-4:]
        i = pl.program_id(0)
        dm = d_ref[...]
        po, pg = p_ref[:, :D], p_ref[:, D:]
        sg = _sigmoid(pg)
        dpre = jnp.concatenate([dm * sg, dm * po * sg * (1.0 - sg)], axis=-1)

        @pl.when(i == 0)
        def _():
            acc[...] = jnp.zeros_like(acc)

        acc[...] += _sum8(dpre)
        dpb = dpre.astype(BF16)
        dz = jnp.zeros((TM, D), F32)
        for s in range(4):
            dz = dz + _dot_nt(dpb[:, s * hw:(s + 1) * hw], w_ref[s])
        if has_y:
            dz = dz * _gelu_grad(y_ref[...])
        dz_ref[...] = dz
        dp_ref[...] = dpb

        @pl.when(i == nt - 1)
        def _():
            db_ref[...] = jnp.sum(acc[...], axis=0, keepdims=True)

    T = cfg.T
    in_specs = [spec, spec2, _wspec(w)] + ([spec] if has_y else [])
    args = [dmix, pre, w[0]] + ([yy] if has_y else [])
    return pl.pallas_call(
        body, name=name, grid=(nt,), in_specs=in_specs, out_specs=[spec, spec2, _full((1, 2 * D))],
        out_shape=[_sds((T, D)), _sds((T, 2 * D), BF16), _sds((1, 2 * D))],
        scratch_shapes=[pltpu.VMEM((8, 2 * D), F32)], compiler_params=_cp(1))(*args)


def _s5_du(cfg, name, rows, du, dy, dy_from_tile, tok, mod8, cols, dskip):
    D, TM = cfg.D, cfg.TM
    csh, csc = cols
    spec, nt, off = _tok_specs(cfg, rows, D)

    def body(du_ref, dy_ref, tok_ref, mod_ref, ds_ref, dh_ref, dd_ref, acc):
        i = pl.program_id(0)
        dyv = jnp.where(i + off >= dy_from_tile, dy_ref[...], 0.0)
        u = _mod(tok_ref[...], mod_ref[:, csh:csh + D], mod_ref[:, csc:csc + D])
        dh_ref[...] = du_ref[0] + du_ref[1] + ds_ref[...] * dyv

        @pl.when(i == 0)
        def _():
            acc[...] = jnp.zeros_like(acc)

        acc[...] += _sum8(dyv * u)

        @pl.when(i == nt - 1)
        def _():
            dd_ref[...] = jnp.sum(acc[...], axis=0, keepdims=True)

    T = cfg.T
    return pl.pallas_call(
        body, name=name, grid=(nt,),
        in_specs=[pl.BlockSpec((2, TM, D), lambda i: (0, i + off, 0)), _masked_spec(cfg, rows, D, dy_from_tile), spec,
                  _mod_spec(cfg, mod8, off), _full((1, D))],
        out_specs=[spec, _full((1, D))], out_shape=[_sds((T, D)), _sds((1, D))],
        scratch_shapes=[pltpu.VMEM((8, D), F32)], compiler_params=_cp(1))(du, dy, tok, mod8, dskip)


def _pw2_bwd(cfg, name, rows, dmix, cv, w, ln_g, ln_b):
    D, TM = cfg.D, cfg.TM
    spec, nt, off = _tok_specs(cfg, rows, D)
    kb = D // 4

    def body(d_ref, cv_ref, w_ref, g_ref, b_ref, dcv_ref, dmb_ref, sums_ref, acc):
        i = pl.program_id(0)
        dm = d_ref[...]
        dmb = dm.astype(BF16)
        ds = jnp.concatenate([_dot_nt(dmb, w_ref[q]) for q in range(4)], axis=-1)
        xhat, rstd = _ln_stats(cv_ref[...])
        nn = xhat * g_ref[...] + b_ref[...]
        sg = _sigmoid(nn)
        dn = ds * (sg * (1.0 + nn * (1.0 - sg)))
        dcv = _ln_bwd(dn * g_ref[...], xhat, rstd)

        @pl.when(i == 0)
        def _():
            acc[...] = jnp.zeros_like(acc)

        acc[0] += _sum8(dn * xhat)
        acc[1] += _sum8(dn)
        acc[2] += _sum8(dcv)
        acc[3] += _sum8(dm)
        dcv_ref[...] = dcv
        dmb_ref[...] = dmb

        @pl.when(i == nt - 1)
        def _():
            for q in range(4):
                sums_ref[q:q + 1, :] = jnp.sum(acc[q], axis=0, keepdims=True)

    T = cfg.T
    return pl.pallas_call(
        body, name=name, grid=(nt,),
        in_specs=[spec, spec, _wspec(w), _full((1, D)), _full((1, D))],
        out_specs=[spec, spec, _full((4, D))], out_shape=[_sds((T, D)), _sds((T, D), BF16), _sds((4, D))],
        scratch_shapes=[pltpu.VMEM((4, 8, D), F32)], compiler_params=_cp(1))(dmix, cv, w[0], ln_g, ln_b)


def _dwconv_bwd(cfg, name, rows, dcv, ag, w_dw):
    D, TM, HB, KW, half = cfg.D, cfg.TM, cfg.HB, cfg.KW, cfg.half
    prev_map, next_map, edge, off = _halo_maps(cfg, rows)
    spec, nt, _ = _tok_specs(cfg, rows, D)

    def body(dc_ref, dp_ref, dn_ref, ac_ref, ap_ref, an_ref, w_ref, dag_ref, dw_ref, extd, exta, acc):
        i = pl.program_id(0)
        is_first, is_last = edge(i)

        @pl.when(i >= 0)
        def _():
            p, n = _halo_fix(dp_ref[...], dn_ref[...], is_first, is_last)
            extd[0:HB, :] = p
            extd[HB:HB + TM, :] = dc_ref[...]
            extd[HB + TM:, :] = n
            p, n = _halo_fix(ap_ref[...], an_ref[...], is_first, is_last)
            exta[0:HB, :] = p
            exta[HB:HB + TM, :] = ac_ref[...]
            exta[HB + TM:, :] = n

        @pl.when(i == 0)
        def _():
            acc[...] = jnp.zeros_like(acc)

        cr = min(CONV_ROWS, TM)
        for r0 in range(0, TM, cr):
            for lc in range(D // LANES):
                ls = pl.ds(lc * LANES, LANES)
                dcur = dc_ref[r0:r0 + cr, ls]
                dag = jnp.zeros((cr, LANES), F32)
                for k in range(KW):
                    lo = r0 + HB + 8 * (half - k)
                    la = r0 + HB + 8 * (k - half)
                    dag = dag + w_ref[k:k + 1, ls] * extd[lo:lo + cr, ls]
                    acc[k, :, ls] += _sum8(dcur * exta[la:la + cr, ls])
                dag_ref[r0:r0 + cr, ls] = dag

        @pl.when(i == nt - 1)
        def _():
            for k in range(KW):
                dw_ref[k:k + 1, :] = jnp.sum(acc[k], axis=0, keepdims=True)

    T = cfg.T
    hp, hn = pl.BlockSpec((HB, D), prev_map), pl.BlockSpec((HB, D), next_map)
    return pl.pallas_call(
        body, name=name, grid=(nt,), in_specs=[spec, hp, hn, spec, hp, hn, _full((KW, D))],
        out_specs=[spec, _full((KW, D))], out_shape=[_sds((T, D)), _sds((KW, D))],
        scratch_shapes=[pltpu.VMEM((TM + 2 * HB, D), F32), pltpu.VMEM((TM + 2 * HB, D), F32), pltpu.VMEM((KW, 8, D), F32)],
        compiler_params=_cp(1))(dcv, dcv, dcv, ag, ag, ag, w_dw)


def _input_bwd(cfg, dres, dh, tok0, mod8, csc):
    D, TM = cfg.D, cfg.TM
    rows = cfg.rows(True)
    spec, nt, off = _tok_specs(cfg, rows, D)
    nct = cfg.Tc // TM
    st = _stream_of(cfg, off, TM)

    def body(dr_ref, dh_ref, t_ref, mod_ref, gx_ref, dsc_ref, dsh_ref):
        i = pl.program_id(0)
        dhv = dh_ref[...]
        gx_ref[...] = dr_ref[...] + _rowscale(dhv, 1.0 + mod_ref[:, csc:csc + D])
        first = (i == 0) | (i == nct)
        s_sc, s_sh = _sum8(dhv * t_ref[...]), _sum8(dhv)

        @pl.when(first)
        def _():
            dsc_ref[...] = s_sc
            dsh_ref[...] = s_sh

        @pl.when(jnp.logical_not(first))
        def _():
            dsc_ref[...] += s_sc
            dsh_ref[...] += s_sh

    acc_spec = pl.BlockSpec((None, 8, D), lambda i: (st(i), 0, 0))
    return pl.pallas_call(
        body, name="input_bwd", grid=(nt,), in_specs=[spec, spec, spec, _mod_spec(cfg, mod8, off)],
        out_specs=[spec, acc_spec, acc_spec], out_shape=[_sds((cfg.T, D)), _sds((2, 8, D)), _sds((2, 8, D))],
        compiler_params=_cp(1))(dres, dh, tok0, mod8)


def _dmod_rows(dmod8):
    nl, _, _, w = dmod8.shape

    def body(d_ref, o_ref):
        xs = d_ref[1]
        cs = d_ref[0]
        o_ref[...] = jnp.zeros((8, w), F32)
        o_ref[0:1, :] = jnp.sum(xs[0:4], axis=0, keepdims=True)
        o_ref[1:2, :] = jnp.sum(xs[4:8], axis=0, keepdims=True)
        o_ref[2:3, :] = jnp.sum(cs, axis=0, keepdims=True)

    return pl.pallas_call(body, name="dmod_rows", grid=(nl,),
                          in_specs=[pl.BlockSpec((None, 2, 8, w), lambda l: (l, 0, 0, 0))],
                          out_specs=pl.BlockSpec((None, 8, w), lambda l: (l, 0, 0)), out_shape=_sds((nl, 8, w)),
                          compiler_params=_cp(1))(dmod8)


def _x_only(acc):
    return jnp.concatenate([jnp.zeros_like(acc[:1]), acc[1:]], axis=0)


def _pack(parts):
    bufs, meta, off = [], [], 0
    for p in parts:
        n = p.size
        rows = -(-n // (8 * LANES)) * 8
        flat = jnp.pad(p.reshape(-1).astype(F32), (0, rows * LANES - n)).reshape(rows, LANES)
        bufs.append(flat)
        meta.append((off, rows, p.shape))
        off += rows
    if off % 16:
        bufs.append(jnp.zeros((8, LANES), F32))
    return jnp.concatenate(bufs, axis=0), meta


def _unpack(buf, meta):
    out = []
    for off, rows, shape in meta:
        n = 1
        for s in shape:
            n *= s
        out.append(buf[off:off + rows].reshape(-1)[:n].reshape(shape))
    return out


def kernel(x, c, ctx, c_ctx, w_ada, b_ada, ln_gain, ln_bias, s5_lam_re, s5_lam_im, s5_log_dt, s5_b_re, s5_b_im, s5_c_re, s5_c_im, s5_d, s5_w_glu, s5_b_glu, cv_w_pw1, cv_b_pw1, cv_w_dw, cv_b_dw, cv_ln_g, cv_ln_b, cv_w_pw2, cv_b_pw2, mlp_w1, mlp_w2, loss_target, m_c_ctx, m_w_ada, m_b_ada, m_ln_gain, m_ln_bias, m_s5_lam_re, m_s5_lam_im, m_s5_log_dt, m_s5_b_re, m_s5_b_im, m_s5_c_re, m_s5_c_im, m_s5_d, m_s5_w_glu, m_s5_b_glu, m_cv_w_pw1, m_cv_b_pw1, m_cv_w_dw, m_cv_b_dw, m_cv_ln_g, m_cv_ln_b, m_cv_w_pw2, m_cv_b_pw2, m_mlp_w1, m_mlp_w2, v_c_ctx, v_w_ada, v_b_ada, v_ln_gain, v_ln_bias, v_s5_lam_re, v_s5_lam_im, v_s5_log_dt, v_s5_b_re, v_s5_b_im, v_s5_c_re, v_s5_c_im, v_s5_d, v_s5_w_glu, v_s5_b_glu, v_cv_w_pw1, v_cv_b_pw1, v_cv_w_dw, v_cv_b_dw, v_cv_ln_g, v_cv_ln_b, v_cv_w_pw2, v_cv_b_pw2, v_mlp_w1, v_mlp_w2):
    cfg = _Cfg(x, ctx, mlp_w1, cv_w_dw)
    D, T, Tc, Tx, B = cfg.D, cfg.T, cfg.Tc, cfg.Tx, cfg.B
    ax, ay, ac = lax.axis_index("x"), lax.axis_index("y"), lax.axis_index("c")
    shard = 2 * ax + ay
    dev = 4 * ax + 2 * ay + ac
    Ds = D // 4
    Wa = w_ada.shape[2]

    c_pad = jnp.concatenate([c, jnp.zeros((8 - B, D), F32)], axis=0)
    c_gath = _allgather8("gather_c", c_pad).reshape(8, 8, D)[:, :B].reshape(8 * B, D)
    c_all = jnp.concatenate([c_gath, c_ctx[None], jnp.zeros((7, D), F32)], axis=0)
    b_sh = lax.dynamic_slice_in_dim(b_ada, shard * Wa, Wa, axis=1)[:, None, :]
    mod_sh = _ada_fwd(c_all, w_ada, b_sh)
    mod_g = _allgather8("gather_mod", mod_sh.reshape(DEPTH * 24, Wa)).reshape(4, 2, DEPTH, 24, Wa)[:, 0]
    mods = mod_g.transpose(1, 2, 0, 3).reshape(DEPTH, 24, 4 * Wa)
    mine = lax.dynamic_slice_in_dim(mods, B * dev, B, axis=1)
    mod8 = jnp.stack([jnp.broadcast_to(mods[:, 16:17], (DEPTH, 8, 6 * D)), jnp.repeat(mine, 4, axis=1)], axis=1)
    SH1, SC1, G1, SH2, SC2, G2 = (k * D for k in range(6))

    small_parts = [ln_gain.reshape(-1, Ds), ln_bias.reshape(-1, Ds), cv_b_pw1.reshape(-1, Ds), cv_w_dw.reshape(-1, Ds),
                   cv_b_dw, cv_ln_g, cv_ln_b, cv_b_pw2]
    small_rows = [p.shape[0] for p in small_parts]
    sm = jnp.concatenate(small_parts, axis=0)
    pad_r = -sm.shape[0] % 8
    sm = jnp.pad(sm, ((0, pad_r), (0, 0)))
    sm_g = _allgather8("gather_small", sm).reshape(4, 2, sm.shape[0], Ds)[:, 0]
    pieces, o = [], 0
    for nr in small_rows:
        pieces.append(sm_g[:, o:o + nr])
        o += nr

    def unshard(p, lead):
        return p.reshape((4,) + lead + (Ds,)).transpose(tuple(range(1, len(lead) + 1)) + (0, len(lead) + 1)).reshape(lead + (4 * Ds,))

    ln_gain_f = unshard(pieces[0], (DEPTH, 2))
    ln_bias_f = unshard(pieces[1], (DEPTH, 2))
    nconv = cv_w_dw.shape[0]
    b_pw1_f = pieces[2].reshape(4, nconv, 2 * D // 4).transpose(1, 0, 2).reshape(nconv, 2 * D)
    w_dw_f = unshard(pieces[3], (nconv, cfg.KW))
    b_dw_f, cvg_f, cvb_f, b_pw2_f = (unshard(p, (nconv,)) for p in pieces[4:8])

    ns5 = s5_w_glu.shape[0]
    assert mlp_w1.shape[1:] == mlp_w2.shape[1:]
    fam_a = _place_shard("place_w1", mlp_w1, None, 0, 2 * DEPTH)
    fam_a = _place_shard("place_w2", mlp_w2, fam_a, DEPTH, 2 * DEPTH)
    fam_b = _place_shard("place_wglu", s5_w_glu, None, 0, ns5 + nconv)
    fam_b = _place_shard("place_wpw1", cv_w_pw1, fam_b, ns5, ns5 + nconv)
    fam_c = _place_shard("place_wpw2", cv_w_pw2, None, 0, nconv)
    (wb_full,) = _gather_weights([fam_b])
    ov = _Overlap()
    gather_tokens = ov.add("gather", _gather_gen("gatherw", [fam_a, fam_c]))

    pos = jnp.broadcast_to(_pos_embed(cfg.L // GRID_W, D)[None], (B, cfg.L, D))
    tok_in = jnp.concatenate([_to_perm(ctx), _to_perm(x)], axis=0)
    pos_in = jnp.concatenate([jnp.zeros((Tc, D), F32), _to_perm(pos)], axis=0)
    tok0 = _ew("add_pos", lambda a, b: a + b, [tok_in, pos_in], [_sds((T, D))])[0]
    mod8 = _tie(mod8, gather_tokens)
    tgt = _to_perm(loss_target)

    s5p = []
    for j in range(ns5):
        lay = _s5_layouts(cfg, s5_lam_re[j], s5_lam_im[j], s5_log_dt[j], s5_b_re[j], s5_b_im[j])
        abr, abi, bbr, bbi = _disc_fwd(*lay)
        s5p.append(dict(lay=lay, acoef=_coef_rows(cfg, abr, abi, False), acoef_adj=_coef_rows(cfg, abr, abi, True),
                        bf=_blockdiag_b(cfg, bbr, bbi), cf=_blockdiag_c(cfg, s5_c_re[j], s5_c_im[j])))

    kinds = ["s5" if i % 2 == 0 else "conv" for i in range(DEPTH)]
    tok = tok0
    saved = []
    s5_j = cv_j = 0
    for i in range(DEPTH):
        later_s5 = any(k == "s5" for k in kinds[i + 1:])
        rows = cfg.rows(later_s5)
        m8 = mod8[i]
        sv = dict(tok=tok, rows=rows, kind=kinds[i])
        g0, b0 = ln_gain_f[i, 0][None], ln_bias_f[i, 0][None]
        g1, b1 = ln_gain_f[i, 1][None], ln_bias_f[i, 1][None]
        if kinds[i] == "s5":
            j = s5_j
            s5_j += 1
            p = s5p[j]
            y, ck = _s5_forward(cfg, f"l{i}", tok, m8, SH1, SC1, p["bf"], p["cf"], p["acoef"], ov.point if i == 0 else None)
            wg = (wb_full, j)
            m8g = _tie(m8, ov.point(y)) if i == 0 else m8
            x1, r1, mix, zz, zb, yy = _glu_ln(cfg, f"l{i}_glu", rows, tok, y, m8g, (SH1, SC1, G1), s5_d[j][None], wg,
                                              s5_b_glu[j][None], g0, b0)
            sv.update(j=j, ck=ck, zz=zz, zb=zb, yy=yy, wg=wg)
            if i == 0:
                wa_full, wc_full = ov.finish("gather", x1)
        else:
            j = cv_j
            cv_j += 1
            w1c, w2c = (wb_full, ns5 + j), (wc_full, j)
            aa, ag, hb = _pw1_glu(cfg, f"l{i}_pw1", rows, tok, m8, (SH1, SC1), w1c, b_pw1_f[j][None])
            cvv, sb = _dwconv_ln(cfg, f"l{i}_dw", rows, ag, w_dw_f[j], b_dw_f[j][None], cvg_f[j][None], cvb_f[j][None])
            x1, r1, mix = _pw2_ln(cfg, f"l{i}_pw2", rows, sb, tok, m8, G1, w2c, b_pw2_f[j][None], g0, b0)
            sv.update(j=j, aa=aa, ag=ag, hb=hb, cvv=cvv, sb=sb, w1c=w1c, w2c=w2c)
        w1m, w2m = (wa_full, i), (wa_full, DEPTH + i)
        x2, r2, mout, am, hm = _mlp_ln(cfg, f"l{i}_mlp", rows, x1, m8, (SH2, SC2, G2), w1m, w2m, g1, b1)
        sv.update(r1=r1, mix=mix, x1=x1, r2=r2, mout=mout, am=am, hm=hm, w1m=w1m, w2m=w2m, g0=g0, b0=b0, g1=g1, b1=b1)
        saved.append(sv)
        tok = x2

    loss8, dxf = _loss(cfg, tok, tgt)
    loss = lax.psum(loss8[0, 0], ("x", "y", "c"))

    dmod8 = [None] * DEPTH
    g_ln_gain = [[None, None] for _ in range(DEPTH)]
    g_ln_bias = [[None, None] for _ in range(DEPTH)]
    g_s5 = [None] * ns5
    g_cv = [None] * nconv
    dres, dh = dxf, None
    pend = []
    for i in reversed(range(DEPTH)):
        sv = saved[i]
        rows = sv["rows"]
        m8 = mod8[i]
        nxt_m8 = mod8[i + 1] if i + 1 < DEPTH else None
        ctx_ok = True if i + 1 >= DEPTH else (saved[i + 1]["rows"][0] == 0)
        if rows[0] != 0:
            ctx_ok = True
        dprev, dbr, dgn, dbs, dg2, dsc_n, dsh_n = _lnb(
            cfg, f"l{i}_lnb2", rows, dres, ctx_ok, dh, sv["r2"], sv["mout"],
            _tie(jnp.concatenate([sv["g1"], sv["b1"]], 0), pend), m8, G2, nxt_m8, SC1)
        if rows[0] != 0:
            dg2, dsc_n, dsh_n = (_x_only(t) for t in (dg2, dsc_n, dsh_n))
        g_ln_gain[i][1], g_ln_bias[i][1] = dgn[0], dbs[0]
        if i + 1 < DEPTH:
            dmod8[i + 1]["sc1"], dmod8[i + 1]["sh1"] = dsc_n, dsh_n
        dmod8[i] = dict(g2=dg2)
        dh2, dhid, dout = _mlp_bwd(cfg, f"l{i}_mlpb", rows, dbr, sv["am"], sv["w1m"], sv["w2m"])
        pend = ov.point(dh2)
        ga = _wgrad(cfg, f"l{i}_gw1", rows, sv["hm"], dhid, "col", (4, 2, D, cfg.F // 4), 0)
        ga = _wgrad(cfg, f"l{i}_gw2", rows, sv["am"], dout, "row", ga, 1)
        dprev1, dbr1, dgn, dbs, dg1, dsc2, dsh2 = _lnb(
            cfg, f"l{i}_lnb1", rows, dprev, True, dh2, sv["r1"], sv["mix"],
            _tie(jnp.concatenate([sv["g0"], sv["b0"]], 0), pend), m8, G1, m8, SC2)
        if rows[0] != 0:
            dg1, dsc2, dsh2 = (_x_only(t) for t in (dg1, dsc2, dsh2))
        g_ln_gain[i][0], g_ln_bias[i][0] = dgn[0], dbs[0]
        dmod8[i].update(g1=dg1, sc2=dsc2, sh2=dsh2)
        j = sv["j"]
        if sv["kind"] == "s5":
            p = s5p[j]
            dyy, dzz, dbglu = _glu_bwd(cfg, f"l{i}_glub", rows, dbr1, sv["zz"], sv["wg"], sv["yy"])
            pend = ov.point(dyy)
            gb = _wgrad(cfg, f"l{i}_gwg", rows, sv["zb"], dzz, "col", (4, 1, D, D // 2), 0)
            du, (da, dbf, dcf) = _s5_backward(cfg, f"l{i}", dyy, rows[0] == 0, sv["tok"], m8, SH1, SC1, p["bf"], p["cf"],
                                              p["acoef"], _tie(p["acoef_adj"], pend), sv["ck"])
            dh, dds = _s5_du(cfg, f"l{i}_du", cfg.rows(True), du, dyy, rows[0] // cfg.TM, sv["tok"], m8, (SH1, SC1),
                             s5_d[j][None])
            g_s5[j] = dict(da=da, dbf=dbf, dcf=dcf, dd=dds[0], dbglu=dbglu[0])
            layer_grads = [ga, gb]
        else:
            dcv, dmb, sums = _pw2_bwd(cfg, f"l{i}_pw2b", rows, dbr1, sv["cvv"], sv["w2c"], cvg_f[j][None], cvb_f[j][None])
            pend = ov.point(dcv)
            gc = _wgrad(cfg, f"l{i}_gwp2", rows, sv["sb"], dmb, "row", (4, 1, D // 4, D), 0)
            dag, dwdw = _dwconv_bwd(cfg, f"l{i}_dwb", rows, dcv, sv["ag"], _tie(w_dw_f[j], pend))
            dh, daa, dbpw1 = _glu_bwd(cfg, f"l{i}_pw1b", rows, dag, sv["aa"], sv["w1c"])
            gb = _wgrad(cfg, f"l{i}_gwp1", rows, sv["hb"], daa, "col", (4, 1, D, D // 2), 0)
            g_cv[j] = dict(ln_g=sums[0], ln_b=sums[1], b_dw=sums[2], b_pw2=sums[3], w_dw=dwdw, b_pw1=dbpw1[0])
            layer_grads = [ga, gb, gc]
        dres = dprev1
        pend = ov.point(dh) + ov.add(f"rs{i}", _reduce_scatter_gen(f"gw{i}", layer_grads))
    gx_perm, dsc0, dsh0 = _input_bwd(cfg, dres, dh, tok0, _tie(mod8[0], pend), SC1)
    dmod8[0]["sc1"], dmod8[0]["sh1"] = dsc0, dsh0
    grad_x = _from_perm(gx_perm[Tc:], B, cfg.L)
    reduced = [ov.finish(f"rs{i}", gx_perm) for i in range(DEPTH)]

    zero28 = jnp.zeros((2, 8, D), F32)
    dm8 = jnp.stack([jnp.concatenate([dmod8[i].get(k, zero28) for k in ("sh1", "sc1", "g1", "sh2", "sc2", "g2")], axis=-1)
                     for i in range(DEPTH)])
    dm_rows = _dmod_rows(dm8)
    dm_tab = jnp.zeros((DEPTH, 24, 6 * D), F32)
    dm_tab = lax.dynamic_update_slice_in_dim(dm_tab, dm_rows[:, 0:B], B * dev, axis=1)
    dm_tab = lax.dynamic_update_slice_in_dim(dm_tab, dm_rows[:, 2:3], 16, axis=1)

    eye_parts = []
    for j in range(ns5):
        g = g_s5[j]
        dbbr, dbbi = _diag_b(cfg, g["dbf"])
        dcr, dci = _diag_c(cfg, g["dcf"])
        eye_parts += [g["da"], dbbr, dbbi, dcr, dci, g["dd"], g["dbglu"]]
    for j in range(nconv):
        g = g_cv[j]
        eye_parts += [g["ln_g"], g["ln_b"], g["b_dw"], g["b_pw2"], g["w_dw"], g["b_pw1"]]
    eye_parts += [jnp.stack([jnp.stack(r) for r in g_ln_gain]), jnp.stack([jnp.stack(r) for r in g_ln_bias]), dm_tab]
    buf, meta = _pack(eye_parts)
    red = _unpack(_allreduce8("small", buf), meta)

    grads = {}
    k = 0
    lam_re_g, lam_im_g, ldt_g, bre_g, bim_g, cre_g, cim_g, dd_g, bglu_g = ([] for _ in range(9))
    for j in range(ns5):
        da, dbbr, dbbi, dcr, dci, dd, dbglu = red[k:k + 7]
        k += 7
        da_s = _sublane_sum(f"s5_dasum_{j}", da.reshape(4, 8, cfg.NS)).reshape(2, 2, cfg.NS)
        g_abr = da_s[:, 0].reshape(2, cfg.G, cfg.P).transpose(2, 0, 1).reshape(cfg.P, 2 * cfg.G)
        g_abi = da_s[:, 1].reshape(2, cfg.G, cfg.P).transpose(2, 0, 1).reshape(cfg.P, 2 * cfg.G)
        glr, gli, gldt, gbr, gbi = _disc_bwd(*s5p[j]["lay"], g_abr, g_abi, dbbr, dbbi)
        lam_re_g.append(glr.reshape(cfg.P, 2, cfg.G).transpose(1, 2, 0))
        lam_im_g.append(gli.reshape(cfg.P, 2, cfg.G).transpose(1, 2, 0))
        ldt_g.append(gldt.reshape(2, cfg.G))
        bre_g.append(gbr.reshape(S5_GROUP, cfg.P, 2, cfg.G).transpose(2, 3, 1, 0))
        bim_g.append(gbi.reshape(S5_GROUP, cfg.P, 2, cfg.G).transpose(2, 3, 1, 0))
        cre_g.append(dcr)
        cim_g.append(dci)
        dd_g.append(dd)
        bglu_g.append(dbglu)
    grads.update(s5_lam_re=jnp.stack(lam_re_g), s5_lam_im=jnp.stack(lam_im_g), s5_log_dt=jnp.stack(ldt_g),
                 s5_b_re=jnp.stack(bre_g), s5_b_im=jnp.stack(bim_g), s5_c_re=jnp.stack(cre_g), s5_c_im=jnp.stack(cim_g),
                 s5_d=jnp.stack(dd_g), s5_b_glu=jnp.stack(bglu_g))

    def my_cols(full, width):
        return lax.dynamic_slice_in_dim(full, shard * width, width, axis=full.ndim - 1)

    cvs = {n: [] for n in ("ln_g", "ln_b", "b_dw", "b_pw2", "w_dw", "b_pw1")}
    for j in range(nconv):
        for n, val in zip(("ln_g", "ln_b", "b_dw", "b_pw2", "w_dw", "b_pw1"), red[k:k + 6]):
            cvs[n].append(val)
        k += 6
    grads.update(cv_ln_g=my_cols(jnp.stack(cvs["ln_g"]), Ds), cv_ln_b=my_cols(jnp.stack(cvs["ln_b"]), Ds),
                 cv_b_dw=my_cols(jnp.stack(cvs["b_dw"]), Ds), cv_b_pw2=my_cols(jnp.stack(cvs["b_pw2"]), Ds),
                 cv_w_dw=my_cols(jnp.stack(cvs["w_dw"]), Ds), cv_b_pw1=my_cols(jnp.stack(cvs["b_pw1"]), 2 * D // 4))
    grads.update(ln_gain=my_cols(red[k], Ds), ln_bias=my_cols(red[k + 1], Ds))
    dm_all = red[k + 2]

    dm_sh = lax.dynamic_slice_in_dim(dm_all, shard * Wa, Wa, axis=2)
    gw_ada, dcond = _ada_bwd(c_all, dm_sh, w_ada)
    grads["w_ada"] = gw_ada
    grads["b_ada"] = _colsum_groups("ada_bsum", dm_all)
    dc_part = dcond[0:1]
    dc_buf = jnp.concatenate([jnp.where(ac == 0, dc_part, 0.0), jnp.zeros((7, D), F32)], axis=0)
    dc_tot = _allreduce8("cctx", dc_buf.reshape(8 * D // LANES, LANES)).reshape(8, D)[0:1]
    grads["c_ctx"] = _ew("cctx_grad", lambda g, cv: g * (_sigmoid(cv) * (1.0 + cv * (1.0 - _sigmoid(cv)))),
                         [jnp.broadcast_to(dc_tot, (8, D)), jnp.broadcast_to(c_ctx[None], (8, D))], [_sds((8, D))])[0][0]

    s5_layers = [i for i in range(DEPTH) if kinds[i] == "s5"]
    cv_layers = [i for i in range(DEPTH) if kinds[i] == "conv"]
    grads.update(mlp_w1=jnp.stack([reduced[i][0][0] for i in range(DEPTH)]),
                 mlp_w2=jnp.stack([reduced[i][0][1] for i in range(DEPTH)]),
                 s5_w_glu=jnp.stack([reduced[i][1][0] for i in s5_layers]),
                 cv_w_pw1=jnp.stack([reduced[i][1][0] for i in cv_layers]),
                 cv_w_pw2=jnp.stack([reduced[i][2][0] for i in cv_layers]))

    weights = dict(c_ctx=c_ctx, w_ada=w_ada, b_ada=b_ada, ln_gain=ln_gain, ln_bias=ln_bias, s5_lam_re=s5_lam_re,
                   s5_lam_im=s5_lam_im, s5_log_dt=s5_log_dt, s5_b_re=s5_b_re, s5_b_im=s5_b_im, s5_c_re=s5_c_re,
                   s5_c_im=s5_c_im, s5_d=s5_d, s5_w_glu=s5_w_glu, s5_b_glu=s5_b_glu, cv_w_pw1=cv_w_pw1, cv_b_pw1=cv_b_pw1,
                   cv_w_dw=cv_w_dw, cv_b_dw=cv_b_dw, cv_ln_g=cv_ln_g, cv_ln_b=cv_ln_b, cv_w_pw2=cv_w_pw2, cv_b_pw2=cv_b_pw2,
                   mlp_w1=mlp_w1, mlp_w2=mlp_w2)
    ms = dict(c_ctx=m_c_ctx, w_ada=m_w_ada, b_ada=m_b_ada, ln_gain=m_ln_gain, ln_bias=m_ln_bias, s5_lam_re=m_s5_lam_re,
              s5_lam_im=m_s5_lam_im, s5_log_dt=m_s5_log_dt, s5_b_re=m_s5_b_re, s5_b_im=m_s5_b_im, s5_c_re=m_s5_c_re,
              s5_c_im=m_s5_c_im, s5_d=m_s5_d, s5_w_glu=m_s5_w_glu, s5_b_glu=m_s5_b_glu, cv_w_pw1=m_cv_w_pw1,
              cv_b_pw1=m_cv_b_pw1, cv_w_dw=m_cv_w_dw, cv_b_dw=m_cv_b_dw, cv_ln_g=m_cv_ln_g, cv_ln_b=m_cv_ln_b,
              cv_w_pw2=m_cv_w_pw2, cv_b_pw2=m_cv_b_pw2, mlp_w1=m_mlp_w1, mlp_w2=m_mlp_w2)
    vs = dict(c_ctx=v_c_ctx, w_ada=v_w_ada, b_ada=v_b_ada, ln_gain=v_ln_gain, ln_bias=v_ln_bias, s5_lam_re=v_s5_lam_re,
              s5_lam_im=v_s5_lam_im, s5_log_dt=v_s5_log_dt, s5_b_re=v_s5_b_re, s5_b_im=v_s5_b_im, s5_c_re=v_s5_c_re,
              s5_c_im=v_s5_c_im, s5_d=v_s5_d, s5_w_glu=v_s5_w_glu, s5_b_glu=v_s5_b_glu, cv_w_pw1=v_cv_w_pw1,
              cv_b_pw1=v_cv_b_pw1, cv_w_dw=v_cv_w_dw, cv_b_dw=v_cv_b_dw, cv_ln_g=v_cv_ln_g, cv_ln_b=v_cv_ln_b,
              cv_w_pw2=v_cv_w_pw2, cv_b_pw2=v_cv_b_pw2, mlp_w1=v_mlp_w1, mlp_w2=v_mlp_w2)
    names = list(weights)
    deltas, new_m, new_v = {}, {}, {}
    for n in names:
        g = grads[n].reshape(weights[n].shape)
        grads[n] = g
        deltas[n], new_m[n], new_v[n] = _adamw("adamw_" + n, weights[n], g, ms[n], vs[n])
    return (loss, grad_x, *[grads[n] for n in names], *[deltas[n] for n in names], *[new_m[n] for n in names],
            *[new_v[n] for n in names])


def _sublane_sum(name, a):
    n, _, w = a.shape

    def body(a_ref, o_ref):
        for q in range(n):
            o_ref[q:q + 1, :] = jnp.sum(a_ref[q], axis=0, keepdims=True)

    return pl.pallas_call(body, name=name, out_shape=_sds((n, w)))(a)


def _colsum_groups(name, dm_all):
    nl, nr, w = dm_all.shape

    def body(d_ref, o_ref):
        o_ref[...] = jnp.zeros((8, w), F32) + jnp.sum(d_ref[...], axis=0, keepdims=True)

    out = pl.pallas_call(body, name=name, grid=(nl,), in_specs=[pl.BlockSpec((None, nr, w), lambda l: (l, 0, 0))],
                         out_specs=pl.BlockSpec((None, 8, w), lambda l: (l, 0, 0)), out_shape=_sds((nl, 8, w)),
                         compiler_params=_cp(1))(dm_all)
    return out[:, 0]
```

```python
import functools
import math

import jax
import jax.numpy as jnp
from jax import lax
from jax.experimental import pallas as pl
from jax.experimental.pallas import tpu as pltpu

F32 = jnp.float32
BF16 = jnp.bfloat16
MESH = pl.DeviceIdType.MESH
ANY = pl.BlockSpec(memory_space=pl.ANY)

DEPTH = 4
S5_GROUP = 16
S5_STATE = 64
GRID_W = 64
POS_TEMP = 10000.0
LAMBDA_RE_MAX = -1e-4
LN_EPS = 1e-5
DN_ALPHA = (2.0 * DEPTH) ** 0.25
ADAM_LR, ADAM_B1, ADAM_B2, ADAM_EPS, ADAM_WD, ADAM_STEP = 0.001, 0.9, 0.999, 1e-08, 0.01, 10

SUBLANES = 8
LANES = 128
OCT_CH = 128
OCT_ST = 512
CONV_ROWS = 64
VMEM_LIMIT = 56 * 1024 * 1024


def _cp(n_axes):
    return pltpu.CompilerParams(dimension_semantics=("arbitrary",) * n_axes, vmem_limit_bytes=VMEM_LIMIT)


def _full(shape, single=False):
    nd = len(shape)
    if single:
        return pl.BlockSpec(shape, lambda *i: (0,) * nd, pipeline_mode=pl.Buffered(1))
    return pl.BlockSpec(shape, lambda *i: (0,) * nd)


def _sds(shape, dtype=F32):
    return jax.ShapeDtypeStruct(tuple(shape), dtype)


def _mod(x, sh8, sc8):
    r, d = x.shape
    return (x.reshape(r // 8, 8, d) * (1.0 + sc8[None]) + sh8[None]).reshape(r, d)


def _rowscale(x, g8):
    r, d = x.shape
    return (x.reshape(r // 8, 8, d) * g8[None]).reshape(r, d)


def _sum8(x):
    r, w = x.shape
    return jnp.sum(x.reshape(r // 8, 8, w), axis=0)


def _ln_stats(r):
    mu = jnp.mean(r, axis=-1, keepdims=True)
    xc = r - mu
    var = jnp.mean(xc * xc, axis=-1, keepdims=True)
    rstd = lax.rsqrt(var + LN_EPS)
    return xc * rstd, rstd


def _ln_bwd(dxh, xhat, rstd):
    m1 = jnp.mean(dxh, axis=-1, keepdims=True)
    m2 = jnp.mean(dxh * xhat, axis=-1, keepdims=True)
    return rstd * (dxh - m1 - xhat * m2)


def _sigmoid(x):
    return 1.0 / (1.0 + jnp.exp(-x))


def _gelu(y):
    return 0.5 * y * (1.0 + lax.erf(y * (1.0 / math.sqrt(2.0))))


def _gelu_grad(y):
    return 0.5 * (1.0 + lax.erf(y * (1.0 / math.sqrt(2.0)))) + y * jnp.exp(-0.5 * y * y) * (1.0 / math.sqrt(2.0 * math.pi))


def _dot(a, b):
    return jnp.dot(a, b, preferred_element_type=F32)


def _dot_nt(a, b):
    return lax.dot_general(a, b, (((1,), (1,)), ((), ())), preferred_element_type=F32)


def _dot_tn(a, b):
    return lax.dot_general(a, b, (((0,), (0,)), ((), ())), preferred_element_type=F32)


class _Cfg:
    def __init__(self, x, ctx, mlp_w1, cv_w_dw):
        self.B, self.L, self.D = x.shape
        self.Lc = ctx.shape[1]
        assert self.B * 4 == SUBLANES, "two examples per device, four chunks each"
        self.F = mlp_w1.shape[2] * 4
        self.KW = cv_w_dw.shape[1]
        self.half = self.KW // 2
        self.G = self.D // S5_GROUP
        self.P = S5_STATE
        self.NS = self.G * self.P
        self.NO = self.D // OCT_CH
        assert self.NO % 2 == 0
        self.nx = self.L // 4
        self.nc = self.Lc // 4
        self.Tc = self.B * self.Lc
        self.Tx = self.B * self.L
        self.T = self.Tc + self.Tx
        self.TM = 256 if self.Tc % 256 == 0 else self.Tc
        assert self.Tx % self.TM == 0 and self.TM % 16 == 0
        self.HB = self.TM // 2
        assert SUBLANES * self.half <= self.HB
        self.TW = 512 if (self.Tc % 512 == 0 and self.Tx % 512 == 0) else self.TM

    def ti(self, n):
        t = 32 if self.nc % 32 == 0 else self.nc
        assert n % t == 0 and self.Tc % (8 * t) == 0
        return t

    def rows(self, ctx_too):
        return (0, self.T) if ctx_too else (self.Tc, self.Tx)


def _allgather8(name, x_shard):
    m_per, n = x_shard.shape
    assert m_per % 8 == 0

    def body(x_ref, out_ref, send_sems, recv_sems, local_sem):
        x, y, c = lax.axis_index("x"), lax.axis_index("y"), lax.axis_index("c")
        me, sibling = (x, y, c), (x, y, 1 - c)
        chips = [(1 - x, y), (x, 1 - y), (1 - x, 1 - y)]

        def rows(px, py, pc):
            return out_ref.at[pl.ds((4 * px + 2 * py + pc) * m_per, m_per), :]

        def copy(k, block, to, src=None):
            return pltpu.make_async_remote_copy(
                src_ref=rows(*block) if src is None else src, dst_ref=rows(*block),
                send_sem=send_sems.at[k], recv_sem=recv_sems.at[k], device_id=to, device_id_type=MESH)

        mine = pltpu.make_async_copy(x_ref, rows(*me), local_sem)
        mine.start()
        first = [copy(0, me, sibling, src=x_ref)]
        first += [copy(1 + j, me, (*chip, c), src=x_ref) for j, chip in enumerate(chips)]
        for cp in first:
            cp.start()
        passed = [copy(4 + j, (*chip, c), sibling) for j, chip in enumerate(chips)]
        for j, chip in enumerate(chips):
            copy(1 + j, (*chip, c), me).wait_recv()
            passed[j].start()
        copy(0, sibling, me).wait_recv()
        for j, chip in enumerate(chips):
            copy(4 + j, (*chip, 1 - c), me).wait_recv()
        for cp in first + passed:
            cp.wait_send()
        mine.wait()

    return pl.pallas_call(
        body, name=name, out_shape=_sds((8 * m_per, n), x_shard.dtype),
        in_specs=[pl.BlockSpec(memory_space=pltpu.VMEM)], out_specs=pl.BlockSpec(memory_space=pltpu.VMEM),
        scratch_shapes=[pltpu.SemaphoreType.DMA((7,)), pltpu.SemaphoreType.DMA((7,)), pltpu.SemaphoreType.DMA],
        compiler_params=pltpu.CompilerParams(vmem_limit_bytes=VMEM_LIMIT),
    )(x_shard)


def _flip(v, m):
    return v + m - 2 * v * m


def _peer(axis):
    x, y, c = lax.axis_index("x"), lax.axis_index("y"), lax.axis_index("c")
    if axis == "c":
        return (x, y, 1 - c)
    if axis == "xy":
        return (_flip(x, 1 - c), _flip(y, c), c)
    if axis == "yx":
        return (_flip(x, c), _flip(y, 1 - c), c)
    raise ValueError(axis)


def _pair_exchange(name, axis, inputs, out_shapes, aliases, plan):
    n_in = len(inputs)
    n_out = len(out_shapes)

    def body(*refs):
        ins, outs = refs[:n_in], refs[n_in:n_in + n_out]
        send_sems, recv_sems, local_sems = refs[n_in + n_out:]
        x, y, c = lax.axis_index("x"), lax.axis_index("y"), lax.axis_index("c")
        remote, local = plan(x, y, c, ins, outs)
        lcs = [pltpu.make_async_copy(s, d, local_sems.at[k]) for k, (s, d) in enumerate(local)]
        for cp in lcs:
            cp.start()
        rcs = [pltpu.make_async_remote_copy(src_ref=s, dst_ref=d, send_sem=send_sems.at[k], recv_sem=recv_sems.at[k],
                                            device_id=_peer(axis), device_id_type=MESH) for k, (s, d) in enumerate(remote)]
        for cp in rcs:
            cp.start()
        for cp in rcs:
            cp.wait()
        for cp in lcs:
            cp.wait()

    n_remote, n_local = plan.counts
    return pl.pallas_call(
        body, name=name, out_shape=tuple(out_shapes),
        in_specs=[ANY] * n_in, out_specs=tuple([ANY] * n_out),
        input_output_aliases=dict(aliases),
        scratch_shapes=[pltpu.SemaphoreType.DMA((n_remote,)), pltpu.SemaphoreType.DMA((n_remote,)),
                        pltpu.SemaphoreType.DMA((max(n_local, 1),))],
    )(*inputs)


def _plan(n_remote, n_local=0):
    def deco(fn):
        fn.counts = (n_remote, n_local)
        return fn
    return deco


HBM = pl.BlockSpec(memory_space=pltpu.HBM)
SEM = pl.BlockSpec(memory_space=pltpu.SEMAPHORE)


def _split_start(name, axis, bufs, plan):
    nb = len(bufs)
    n = plan.counts[0]

    def body(*refs):
        ins, send_sem, recv_sem, token = refs[:nb], refs[nb], refs[nb + 1], refs[-1]
        x, y, c = lax.axis_index("x"), lax.axis_index("y"), lax.axis_index("c")
        for k, (s, d) in enumerate(plan(x, y, c, ins)):
            pltpu.make_async_remote_copy(src_ref=s, dst_ref=d, send_sem=send_sem.at[k], recv_sem=recv_sem.at[k],
                                         device_id=_peer(axis), device_id_type=MESH).start()
        token[...] = jnp.zeros_like(token)

    outs = pl.pallas_call(
        body, name=name,
        out_shape=(pltpu.SemaphoreType.DMA((n,)), pltpu.SemaphoreType.DMA((n,)),
                   *[pltpu.HBM(b.shape, b.dtype) for b in bufs], _sds((8, LANES))),
        in_specs=[HBM] * nb, out_specs=(SEM, SEM, *([HBM] * nb), pl.BlockSpec(memory_space=pltpu.VMEM)),
        input_output_aliases={i: 2 + i for i in range(nb)},
        compiler_params=pltpu.CompilerParams(has_side_effects=pltpu.SideEffectType.DATAFLOW_SIDE_EFFECTING),
    )(*[pltpu.with_memory_space_constraint(b, pltpu.HBM) for b in bufs])
    return dict(name=name, axis=axis, plan=plan, send=outs[0], recv=outs[1], bufs=list(outs[2:2 + nb]), token=outs[-1])


def _split_wait(h, after):
    bufs, plan, axis = h["bufs"], h["plan"], h["axis"]
    nb = len(bufs)

    def body(*refs):
        ins, send_sem, recv_sem = refs[:nb], refs[nb], refs[nb + 1]
        x, y, c = lax.axis_index("x"), lax.axis_index("y"), lax.axis_index("c")
        for k, (s, d) in enumerate(plan(x, y, c, ins)):
            cp = pltpu.make_async_remote_copy(src_ref=s, dst_ref=d, send_sem=send_sem.at[k], recv_sem=recv_sem.at[k],
                                              device_id=_peer(axis), device_id_type=MESH)
            cp.wait_send()
            cp.wait_recv()

    outs = pl.pallas_call(
        body, name=h["name"] + "_wait", out_shape=tuple(pltpu.HBM(b.shape, b.dtype) for b in bufs),
        in_specs=[HBM] * nb + [SEM, SEM, ANY], out_specs=tuple([HBM] * nb),
        input_output_aliases={i: i for i in range(nb)},
        compiler_params=pltpu.CompilerParams(has_side_effects=pltpu.SideEffectType.DATAFLOW_SIDE_EFFECTING),
    )(*bufs, h["send"], h["recv"], after)
    return list(outs)


def _tie(small, tokens):
    for t in tokens:
        small = small + t[0, 0]
    return small


class _Overlap:
    def __init__(self):
        self.live = {}
        self.done = {}

    def add(self, key, gen):
        self.live[key] = gen
        return [next(gen)]

    def point(self, arr):
        tokens = []
        for key in list(self.live):
            try:
                tokens.append(self.live[key].send(arr))
            except StopIteration as e:
                self.done[key] = e.value
                del self.live[key]
        return tokens

    def finish(self, key, arr):
        while key in self.live:
            try:
                self.live[key].send(arr)
            except StopIteration as e:
                self.done[key] = e.value
                del self.live[key]
        return self.done.pop(key)


def _gather_gen(tag, fams):
    nf = len(fams)
    shapes = [f.shape for f in fams]
    views = [f.reshape(4, 2, -1, f.shape[-1]) for f in fams]

    @_plan(nf)
    def plan1(x, y, c, refs):
        s = 2 * x + y
        return [(refs[k].at[s, c], refs[k].at[s, c]) for k in range(nf)]

    @_plan(2 * nf)
    def plan2(x, y, c, refs):
        shards = [2 * x + y, 2 * _flip(x, 1 - c) + _flip(y, c)]
        return [(refs[k].at[s, c], refs[k].at[s, c]) for k in range(nf) for s in shards]

    @_plan(3 * nf)
    def plan3(x, y, c, refs):
        shards = [2 * (1 - x) + y, 2 * x + (1 - y), 2 * (1 - x) + (1 - y)]
        return [(refs[k].at[s, c], refs[k].at[s, c]) for k in range(nf) for s in shards]

    for rnd, (axis, plan) in enumerate((("xy", plan1), ("yx", plan2), ("c", plan3))):
        h = _split_start(f"{tag}_g{rnd}", axis, views, plan)
        after = yield h["token"]
        views = _split_wait(h, after)
    return [v.reshape(sh) for v, sh in zip(views, shapes)]


def _reduce_scatter_gen(tag, grads):
    ng = len(grads)
    flat = [g.reshape(4, 2, -1, g.shape[-1]) for g in grads]

    def empty(shape, dtype):
        return lax.empty(tuple(shape), dtype)

    @_plan(ng)
    def plan1(x, y, c, refs):
        return [(refs[k].at[:, 1 - c], refs[ng + k]) for k in range(ng)]

    h = _split_start(tag + "_r0", "c", flat + [empty((4,) + f.shape[2:], F32) for f in flat], plan1)
    after = yield h["token"]
    bufs = _split_wait(h, after)
    p1 = [_sel_add(f"{tag}_add1_{k}", bufs[k], lambda j, sc: (j, sc[2]), bufs[ng + k], True) for k in range(ng)]

    def sent1(kk, x, y, c):
        return ((1 - c) * kk + c * (1 - x), (1 - c) * (1 - y) + c * kk)

    def kept1(j, sc):
        x, y, c = sc[0], sc[1], sc[2]
        return ((1 - c) * j + c * x, (1 - c) * y + c * j)

    @_plan(2 * ng)
    def plan2(x, y, c, refs):
        return [(refs[k].at[sent1(kk, x, y, c)], refs[ng + k].at[kk]) for k in range(ng) for kk in range(2)]

    v1 = [pb.reshape(2, 2, pb.shape[1], pb.shape[2]) for p, pb in p1]
    h = _split_start(tag + "_r1", "yx", v1 + [empty((2,) + v.shape[2:], BF16) for v in v1], plan2)
    after = yield h["token"]
    bufs = _split_wait(h, after)
    p2 = [_sel_add(f"{tag}_add2_{k}", p1[k][0].reshape(2, 2, p1[k][0].shape[1], p1[k][0].shape[2]), kept1, bufs[ng + k], True)
          for k in range(ng)]

    @_plan(ng)
    def plan3(x, y, c, refs):
        return [(refs[k].at[(1 - c) * (1 - x) + c * (1 - y)], refs[ng + k]) for k in range(ng)]

    h = _split_start(tag + "_r2", "xy", [qb for q, qb in p2] + [empty(qb.shape[1:], BF16) for q, qb in p2], plan3)
    after = yield h["token"]
    bufs = _split_wait(h, after)
    fin = [_sel_add(f"{tag}_add3_{k}", p2[k][0], lambda j, sc: ((1 - sc[2]) * sc[0] + sc[2] * sc[1],), bufs[ng + k][None],
                    False, out_slots=(2, lambda j, sc: sc[2]))[0] for k in range(ng)]

    @_plan(ng)
    def plan4(x, y, c, refs):
        return [(refs[k].at[c], refs[k].at[c]) for k in range(ng)]

    h = _split_start(tag + "_r3", "c", fin, plan4)
    after = yield h["token"]
    full = _split_wait(h, after)
    return [full[k].reshape(grads[k].shape[1:]) for k in range(ng)]


def _xyc():
    return jnp.stack([lax.axis_index("x"), lax.axis_index("y"), lax.axis_index("c")]).astype(jnp.int32)


def _place_shard(name, w, fam, slot0, n_slots):
    n, kk, nn = w.shape
    kt = 256 if kk % 256 == 0 else kk

    def body(scal, w_ref, *rest):
        rest[-1][...] = w_ref[...].astype(BF16)

    in_specs = [pl.BlockSpec((None, kt, nn), lambda t, i, sc: (t, i, 0))]
    args = [_xyc(), w]
    aliases = {}
    if fam is not None:
        in_specs.append(ANY)
        args.append(fam)
        aliases = {2: 0}
    gs = pltpu.PrefetchScalarGridSpec(
        num_scalar_prefetch=1, grid=(n, kk // kt), in_specs=in_specs,
        out_specs=pl.BlockSpec((None, None, kt, nn), lambda t, i, sc: (2 * sc[0] + sc[1], slot0 + t, i, 0)))
    return pl.pallas_call(body, name=name, grid_spec=gs, out_shape=_sds((4, n_slots, kk, nn), BF16),
                          input_output_aliases=aliases, compiler_params=_cp(2))(*args)


def _gather_weights(fams):
    nf = len(fams)
    shapes = [f.shape for f in fams]
    views = [f.reshape(4, 2, -1, f.shape[-1]) for f in fams]
    outs = [_sds(v.shape, v.dtype) for v in views]
    alias = {k: k for k in range(nf)}

    @_plan(nf)
    def plan1(x, y, c, ins, outs_):
        s = 2 * x + y
        return ([(ins[k].at[s, c], outs_[k].at[s, c]) for k in range(nf)], [])

    views = _pair_exchange("gatherw_1", "xy", list(views), outs, alias, plan1)

    @_plan(2 * nf)
    def plan2(x, y, c, ins, outs_):
        shards = [2 * x + y, 2 * _flip(x, 1 - c) + _flip(y, c)]
        return ([(ins[k].at[s, c], outs_[k].at[s, c]) for k in range(nf) for s in shards], [])

    views = _pair_exchange("gatherw_2", "yx", list(views), outs, alias, plan2)

    @_plan(3 * nf)
    def plan3(x, y, c, ins, outs_):
        shards = [2 * (1 - x) + y, 2 * x + (1 - y), 2 * (1 - x) + (1 - y)]
        return ([(ins[k].at[s, c], outs_[k].at[s, c]) for k in range(nf) for s in shards], [])

    views = _pair_exchange("gatherw_c", "c", list(views), outs, alias, plan3)
    return [v.reshape(sh) for v, sh in zip(views, shapes)]


def _sel_add(name, a, a_sel, r, emit_bf16, out_slots=None):
    nr, rows, w = r.shape
    tr = 256 if rows % 256 == 0 else rows

    def body(scal, a_ref, r_ref, *outs):
        s = a_ref[...] + r_ref[...].astype(F32)
        outs[0][...] = s
        if emit_bf16:
            outs[1][...] = s.astype(BF16)

    lead = a.ndim - 2
    a_block = (None,) * lead + (tr, w)
    n_out, o_fn = out_slots if out_slots is not None else (nr, lambda j, sc: j)
    out_shape = [_sds((n_out, rows, w), F32)] + ([_sds((nr, rows, w), BF16)] if emit_bf16 else [])
    out_specs = [pl.BlockSpec((None, tr, w), lambda j, t, sc: (o_fn(j, sc), t, 0))]
    if emit_bf16:
        out_specs.append(pl.BlockSpec((None, tr, w), lambda j, t, sc: (j, t, 0)))
    gs = pltpu.PrefetchScalarGridSpec(
        num_scalar_prefetch=1, grid=(nr, rows // tr),
        in_specs=[pl.BlockSpec(a_block, lambda j, t, sc: tuple(a_sel(j, sc)) + (t, 0)),
                  pl.BlockSpec((None, tr, w), lambda j, t, sc: (j, t, 0))],
        out_specs=out_specs)
    return pl.pallas_call(body, name=name, grid_spec=gs, out_shape=out_shape, compiler_params=_cp(2))(_xyc(), a, r)


def _reduce_scatter(tag, grads):
    ng = len(grads)
    flat = [g.reshape(4, 2, -1, g.shape[-1]) for g in grads]

    @_plan(ng)
    def plan1(x, y, c, ins, outs_):
        return ([(ins[k].at[:, 1 - c], outs_[k]) for k in range(ng)], [])

    r1 = _pair_exchange(tag + "_rs_c", "c", flat, [_sds((4,) + f.shape[2:], F32) for f in flat], {}, plan1)
    p1 = [_sel_add(f"{tag}_add1_{k}", flat[k], lambda j, sc: (j, sc[2]), r1[k], True) for k in range(ng)]

    def sent1(kk, x, y, c):
        return ((1 - c) * kk + c * (1 - x), (1 - c) * (1 - y) + c * kk)

    def kept1(j, sc):
        x, y, c = sc[0], sc[1], sc[2]
        return ((1 - c) * j + c * x, (1 - c) * y + c * j)

    @_plan(2 * ng)
    def plan2(x, y, c, ins, outs_):
        return ([(ins[k].at[sent1(kk, x, y, c)], outs_[k].at[kk]) for k in range(ng) for kk in range(2)], [])

    v1 = [pb.reshape(2, 2, pb.shape[1], pb.shape[2]) for p, pb in p1]
    r2 = _pair_exchange(tag + "_rs_1", "yx", v1, [_sds((2,) + v.shape[2:], BF16) for v in v1], {}, plan2)
    p2 = [_sel_add(f"{tag}_add2_{k}", p1[k][0].reshape(2, 2, p1[k][0].shape[1], p1[k][0].shape[2]), kept1, r2[k], True)
          for k in range(ng)]

    @_plan(ng)
    def plan3(x, y, c, ins, outs_):
        return ([(ins[k].at[(1 - c) * (1 - x) + c * (1 - y)], outs_[k]) for k in range(ng)], [])

    r3 = _pair_exchange(tag + "_rs_2", "xy", [qb for q, qb in p2], [_sds(qb.shape[1:], BF16) for q, qb in p2], {}, plan3)
    fin = [_sel_add(f"{tag}_add3_{k}", p2[k][0], lambda j, sc: ((1 - sc[2]) * sc[0] + sc[2] * sc[1],), r3[k][None], False,
                    out_slots=(2, lambda j, sc: sc[2]))[0] for k in range(ng)]

    @_plan(ng)
    def plan4(x, y, c, ins, outs_):
        return ([(ins[k].at[c], outs_[k].at[c]) for k in range(ng)], [])

    full = _pair_exchange(tag + "_rs_c2", "c", fin, [_sds(f.shape, F32) for f in fin], {k: k for k in range(ng)}, plan4)
    return [full[k].reshape(grads[k].shape[1:]) for k in range(ng)]


def _allreduce8(tag, buf, point=None):
    rows, w = buf.shape
    assert rows % 16 == 0
    step = (lambda a: _tie(a, point(a))) if point is not None else (lambda a: a)
    one = lambda: _plan(1)(lambda x, y, c, ins, outs_: ([(ins[0], outs_[0])], []))
    (got,) = _pair_exchange(f"{tag}_ar_c", "c", [buf], [_sds(buf.shape, F32)], {}, one())
    cur = _ew(f"{tag}_aradd_c", lambda a, b: a + b, [buf, got], [_sds(buf.shape, F32)])[0]
    cur = step(cur).reshape(2, rows // 2, w)
    mine = _plan(1)(lambda x, y, c, ins, outs_: ([(ins[0].at[c], outs_[0])], []))
    (got,) = _pair_exchange(f"{tag}_ar_1", "xy", [cur], [_sds(cur.shape[1:], F32)], {}, mine)
    (h1,) = _sel_add(f"{tag}_aradd_1", cur, lambda j, sc: (sc[2],), got[None], False)
    h1 = step(h1)
    (got,) = _pair_exchange(f"{tag}_ar_2", "yx", [h1[0]], [_sds(h1.shape[1:], F32)], {}, one())
    (h2,) = _sel_add(f"{tag}_aradd_2", h1, lambda j, sc: (0,), got[None], False, out_slots=(2, lambda j, sc: sc[2]))
    swap = _plan(1)(lambda x, y, c, ins, outs_: ([(ins[0].at[c], outs_[0].at[c])], []))
    (full,) = _pair_exchange(f"{tag}_ar_c2", "c", [h2], [_sds(h2.shape, F32)], {0: 0}, swap)
    return full.reshape(rows, w)


def _ew(name, fn, ins, outs):
    rows, w = ins[0].shape
    tr = rows
    for cand in (512, 256, 128, 64, 32, 16, 8):
        if rows % cand == 0 and rows > cand and cand * w * 4 <= (1 << 20):
            tr = cand
            break
    n_in = len(ins)

    def body(*refs):
        vals = fn(*[r[...] for r in refs[:n_in]])
        if not isinstance(vals, (tuple, list)):
            vals = (vals,)
        for o, v in zip(refs[n_in:], vals):
            o[...] = v.astype(o.dtype)

    spec = pl.BlockSpec((tr, w), lambda i: (i, 0))
    return pl.pallas_call(body, name=name, grid=(rows // tr,), in_specs=[spec] * n_in,
                          out_specs=[spec] * len(outs), out_shape=list(outs), compiler_params=_cp(1))(*ins)


def _ew3(name, fn, ins, n_out):
    aa, bb, cc = ins[0].shape
    pad_bytes = (-(-bb // SUBLANES) * SUBLANES) * (-(-cc // LANES) * LANES) * 4
    ta = 1
    for cand in range(aa, 0, -1):
        if aa % cand == 0 and cand * pad_bytes <= (1 << 20):
            ta = cand
            break
    n_in = len(ins)

    def body(*refs):
        vals = fn(*[r[...] for r in refs[:n_in]])
        for o, v in zip(refs[n_in:], vals):
            o[...] = v

    spec = pl.BlockSpec((ta, bb, cc), lambda i: (i, 0, 0))
    return pl.pallas_call(body, name=name, grid=(aa // ta,), in_specs=[spec] * n_in, out_specs=[spec] * n_out,
                          out_shape=[_sds((aa, bb, cc))] * n_out, compiler_params=_cp(1))(*ins)


def _view_for_ew(a):
    if a.ndim == 1:
        return a.reshape(1, -1)
    if a.ndim == 2:
        return a
    if a.shape[-1] % LANES == 0 and a.shape[-2] % SUBLANES == 0:
        return a.reshape(-1, a.shape[-1])
    return a.reshape(-1, a.shape[-2], a.shape[-1])


def _adamw(name, w, g, m, v):
    def fn(w, g, m, v):
        m = ADAM_B1 * m + (1.0 - ADAM_B1) * g
        v = ADAM_B2 * v + (1.0 - ADAM_B2) * (g * g)
        m_hat = m / (1.0 - ADAM_B1 ** ADAM_STEP)
        v_hat = v / (1.0 - ADAM_B2 ** ADAM_STEP)
        delta = -ADAM_LR * (m_hat / (jnp.sqrt(v_hat) + ADAM_EPS) + ADAM_WD * w)
        return delta, m, v

    shp = w.shape
    a = [_view_for_ew(t) for t in (w, g, m, v)]
    if a[0].ndim == 3:
        o = _ew3(name, fn, a, 3)
    else:
        o = _ew(name, fn, a, [_sds(a[0].shape)] * 3)
    return tuple(t.reshape(shp) for t in o)


def _to_perm(a):
    b, ls, d = a.shape
    n = ls // 4
    return a.reshape(b * 4, n, d).swapaxes(0, 1).reshape(n * 8, d)


def _from_perm(p, b, ls):
    n = ls // 4
    return p.reshape(n, b * 4, p.shape[-1]).swapaxes(0, 1).reshape(b, ls, p.shape[-1])


def _pos_embed(rows, dim):
    def sincos(pos, d):
        quarter = d // 2
        omega = POS_TEMP ** (-jnp.arange(quarter, dtype=F32) / quarter)
        ang = pos[:, None] * omega[None, :]
        return jnp.concatenate([jnp.sin(ang), jnp.cos(ang)], axis=-1)

    row_idx = jnp.repeat(jnp.arange(rows), GRID_W).astype(F32)
    col_idx = jnp.tile(jnp.arange(GRID_W), rows).astype(F32)
    return jnp.concatenate([sincos(row_idx, dim // 2), sincos(col_idx, dim // 2)], axis=-1)


def _stream_of(cfg, off_tiles, tile_rows):
    nct = cfg.Tc // tile_rows
    return lambda i: jnp.where(i + off_tiles >= nct, 1, 0)


def _ada_fwd(c_all, w_ada, b_shard):
    nl, d, w = w_ada.shape
    tn = 512 if w % 512 == 0 else w

    def body(c_ref, w_ref, b_ref, o_ref):
        cv = c_ref[...]
        cond = (cv * _sigmoid(cv)).astype(BF16)
        o_ref[...] = _dot(cond, w_ref[...].astype(BF16)) + b_ref[...]

    return pl.pallas_call(
        body, name="ada_fwd", grid=(nl, w // tn),
        in_specs=[_full(c_all.shape), pl.BlockSpec((None, d, tn), lambda l, j: (l, 0, j)),
                  pl.BlockSpec((None, 1, tn), lambda l, j: (l, 0, j))],
        out_specs=pl.BlockSpec((None, c_all.shape[0], tn), lambda l, j: (l, 0, j)),
        out_shape=_sds((nl, c_all.shape[0], w)), compiler_params=_cp(2))(c_all, w_ada, b_shard)


def _ada_bwd(c_all, dmod_shard, w_ada):
    nl, d, w = w_ada.shape
    tn = 512 if w % 512 == 0 else w
    nr = c_all.shape[0]

    def body(c_ref, dm_ref, w_ref, gw_ref, dc_ref):
        j = pl.program_id(0) * (w // tn) + pl.program_id(1)
        cv = c_ref[...]
        cond = (cv * _sigmoid(cv)).astype(BF16)
        dm = dm_ref[...].astype(BF16)
        gw_ref[...] = _dot_tn(cond, dm)
        part = _dot_nt(dm[16:24], w_ref[...].astype(BF16))

        @pl.when(j == 0)
        def _():
            dc_ref[...] = part

        @pl.when(j > 0)
        def _():
            dc_ref[...] += part

    return pl.pallas_call(
        body, name="ada_bwd", grid=(nl, w // tn),
        in_specs=[_full(c_all.shape), pl.BlockSpec((None, nr, tn), lambda l, j: (l, 0, j)),
                  pl.BlockSpec((None, d, tn), lambda l, j: (l, 0, j))],
        out_specs=[pl.BlockSpec((None, d, tn), lambda l, j: (l, 0, j)), _full((8, d))],
        out_shape=[_sds((nl, d, w)), _sds((8, d))], compiler_params=_cp(2))(c_all, dmod_shard, w_ada)


def _disc(lr, li, ldt, br, bi):
    lr = jnp.minimum(lr, LAMBDA_RE_MAX)
    dt = jnp.exp(ldt)
    mag = jnp.exp(lr * dt)
    abr = mag * jnp.cos(li * dt)
    abi = mag * jnp.sin(li * dt)
    den = lr * lr + li * li
    nr = abr - 1.0
    ni = abi
    cr = (nr * lr + ni * li) / den
    ci = (ni * lr - nr * li) / den
    return abr, abi, cr[None] * br - ci[None] * bi, cr[None] * bi + ci[None] * br


def _disc_fwd(lr, li, ldt, br, bi):
    def body(a, b, c, d, e, o1, o2, o3, o4):
        r = _disc(a[...], b[...], c[...], d[...], e[...])
        o1[...], o2[...], o3[...], o4[...] = r

    return pl.pallas_call(body, name="s5_disc_fwd", out_shape=[_sds(lr.shape), _sds(lr.shape), _sds(br.shape), _sds(br.shape)])(
        lr, li, ldt, br, bi)


def _disc_bwd(lr, li, ldt, br, bi, g_abr, g_abi, g_bbr, g_bbi):
    def body(a, b, c, d, e, g1, g2, g3, g4, o1, o2, o3, o4, o5):
        _, vjp = jax.vjp(_disc, a[...], b[...], c[...], d[...], e[...])
        r = vjp((g1[...], g2[...], g3[...], g4[...]))
        o1[...], o2[...], o3[...], o4[...], o5[...] = r

    return pl.pallas_call(
        body, name="s5_disc_bwd",
        out_shape=[_sds(lr.shape), _sds(li.shape), _sds(ldt.shape), _sds(br.shape), _sds(bi.shape)])(
        lr, li, ldt, br, bi, g_abr, g_abi, g_bbr, g_bbi)


def _s5_layouts(cfg, lam_re, lam_im, log_dt, b_re, b_im):
    P, G = cfg.P, cfg.G
    lr = lam_re.transpose(2, 0, 1).reshape(P, 2 * G)
    li = lam_im.transpose(2, 0, 1).reshape(P, 2 * G)
    ldt = log_dt.reshape(1, 2 * G)
    br = b_re.transpose(3, 2, 0, 1).reshape(S5_GROUP, P, 2 * G)
    bi = b_im.transpose(3, 2, 0, 1).reshape(S5_GROUP, P, 2 * G)
    return lr, li, ldt, br, bi


def _coef_rows(cfg, abr, abi, conj):
    def one(t):
        return t.reshape(cfg.P, 2, cfg.G).transpose(1, 2, 0).reshape(2, cfg.NS)
    a = jnp.stack([one(abr), -one(abi) if conj else one(abi)], axis=1)
    return jnp.broadcast_to(a[:, :, None, :], (2, 2, SUBLANES, cfg.NS))


def _blockdiag_b(cfg, bbr, bbi):
    eye = jnp.eye(8, dtype=F32)

    def one(t):
        t = t.reshape(S5_GROUP, cfg.P, 2, cfg.G).transpose(2, 3, 0, 1)
        t = t.reshape(2, cfg.NO, 8, S5_GROUP, cfg.P)
        return jnp.einsum("dogcp,gh->dogchp", t, eye).reshape(2, cfg.NO, OCT_CH, OCT_ST)

    return jnp.concatenate([one(bbr), one(bbi)], axis=-1).astype(BF16)


def _blockdiag_c(cfg, c_re, c_im):
    eye = jnp.eye(8, dtype=F32)

    def one(t):
        t = t.transpose(0, 1, 3, 2).reshape(2, cfg.NO, 8, cfg.P, S5_GROUP)
        return jnp.einsum("dogpc,gh->dogphc", t, eye).reshape(2, cfg.NO, OCT_ST, OCT_CH)

    return jnp.concatenate([one(c_re), -one(c_im)], axis=2).astype(BF16)


def _diag_b(cfg, dbf):
    eye = jnp.eye(8, dtype=F32)

    def one(t):
        t = t.reshape(2, cfg.NO, 8, S5_GROUP, 8, cfg.P)
        t = jnp.einsum("dogchp,gh->dogcp", t, eye).reshape(2, cfg.G, S5_GROUP, cfg.P)
        return t.transpose(2, 3, 0, 1).reshape(S5_GROUP, cfg.P, 2 * cfg.G)

    return one(dbf[..., :OCT_ST]), one(dbf[..., OCT_ST:])


def _diag_c(cfg, dcft):
    eye = jnp.eye(8, dtype=F32)

    def one(t):
        t = t.reshape(2, cfg.NO, 8, S5_GROUP, 8, cfg.P)
        return jnp.einsum("dohcgp,gh->dogcp", t, eye).reshape(2, cfg.G, S5_GROUP, cfg.P)

    return one(dcft[..., :OCT_ST]), -one(dcft[..., OCT_ST:])


def _recur(buf, st, a_ref, n_oct, ti, rev, store):
    for o in range(0, n_oct, 2):
        cols = [(pl.ds(oo * 2 * OCT_ST, OCT_ST), pl.ds(oo * 2 * OCT_ST + OCT_ST, OCT_ST)) for oo in (o, o + 1)]
        scol = [pl.ds(oo * OCT_ST, OCT_ST) for oo in (o, o + 1)]
        coef = [(a_ref[0, :, sc], a_ref[1, :, sc]) for sc in scol]
        init = (st[0, :, scol[0]], st[1, :, scol[0]], st[0, :, scol[1]], st[1, :, scol[1]])

        def step(i4, carry, cols=cols, coef=coef):
            carry = list(carry)
            for q in range(unroll):
                i = i4 * unroll + q
                r = pl.multiple_of((i + rev * (ti - 1 - 2 * i)) * 8, 8)
                for s in range(2):
                    sr, si = carry[2 * s], carry[2 * s + 1]
                    ar, ai = coef[s]
                    zr = buf[pl.ds(r, 8), cols[s][0]]
                    zi = buf[pl.ds(r, 8), cols[s][1]]
                    nr = ar * sr - ai * si + zr
                    ni = ar * si + ai * sr + zi
                    if store:
                        buf[pl.ds(r, 8), cols[s][0]] = nr
                        buf[pl.ds(r, 8), cols[s][1]] = ni
                    carry[2 * s], carry[2 * s + 1] = nr, ni
            return tuple(carry)

        unroll = 4 if ti % 4 == 0 else 1
        fin = lax.fori_loop(0, ti // unroll, step, init)
        st[0, :, scol[0]] = fin[0]
        st[1, :, scol[0]] = fin[1]
        st[0, :, scol[1]] = fin[2]
        st[1, :, scol[1]] = fin[3]


def _s5_fwd_pass(cfg, name, tok, mod8, col_sh, col_sc, bf, acoef, r0, n, s_init=None, cf=None, y_prev=None):
    D, NO, NS = cfg.D, cfg.NO, cfg.NS
    ti = cfg.ti(n)
    nb = n // ti
    R = 8 * ti
    ob = r0 // R
    second = s_init is not None
    blk = lambda d, j: ob + j + d * (nb - 1 - 2 * j)

    def body(*refs):
        if second:
            tok_ref, mod_ref, bf_ref, a_ref, si_ref, cf_ref, yp_ref, y_ref, ck_ref, fin_ref, zbuf, st = refs
        else:
            tok_ref, mod_ref, bf_ref, a_ref, fin_ref, zbuf, st = refs
        d = pl.program_id(0)
        j = pl.program_id(1)

        @pl.when(j == 0)
        def _():
            if second:
                st[...] = si_ref[...]
            else:
                st[...] = jnp.zeros_like(st)

        if second:
            ck_ref[...] = st[...]
        u = _mod(tok_ref[...], mod_ref[:, col_sh:col_sh + D], mod_ref[:, col_sc:col_sc + D]).astype(BF16)
        for o in range(NO):
            zbuf[:, o * 1024:(o + 1) * 1024] = _dot(u[:, o * OCT_CH:(o + 1) * OCT_CH], bf_ref[o])
        _recur(zbuf, st, a_ref, NO, ti, d, second)
        if second:
            for o in range(NO):
                y_ref[:, o * OCT_CH:(o + 1) * OCT_CH] = _dot(zbuf[:, o * 1024:(o + 1) * 1024].astype(BF16), cf_ref[o])

        @pl.when(j == nb - 1)
        def _():
            fin_ref[...] = st[...]

    st_spec = pl.BlockSpec((None, 2, 8, NS), lambda d, j: (d, 0, 0, 0))
    in_specs = [pl.BlockSpec((R, D), lambda d, j: (blk(d, j), 0)), _full(mod8.shape),
                pl.BlockSpec((None, NO, OCT_CH, 1024), lambda d, j: (d, 0, 0, 0)), st_spec]
    args = [tok, mod8, bf, acoef]
    scratch = [pltpu.VMEM((R, NO * 1024), F32), pltpu.VMEM((2, 8, NS), F32)]
    if not second:
        return pl.pallas_call(body, name=name, grid=(2, nb), in_specs=in_specs, out_specs=st_spec,
                              out_shape=_sds((2, 2, 8, NS)), scratch_shapes=scratch, compiler_params=_cp(2))(*args)
    in_specs += [st_spec, pl.BlockSpec((None, NO, 1024, OCT_CH), lambda d, j: (d, 0, 0, 0))]
    args += [s_init, cf]
    aliases = {}
    if y_prev is not None:
        in_specs.append(ANY)
        args.append(y_prev)
        aliases = {6: 0}
    else:
        in_specs.append(_full((8, LANES)))
        args.append(jnp.zeros((8, LANES), F32))
    out_specs = [pl.BlockSpec((None, R, D), lambda d, j: (d, blk(d, j), 0)),
                 pl.BlockSpec((None, None, 2, 8, NS), lambda d, j: (d, j + d * (nb - 1 - 2 * j), 0, 0, 0)), st_spec]
    out_shape = [_sds((2, cfg.T, D)), _sds((2, nb, 2, 8, NS)), _sds((2, 2, 8, NS))]
    return pl.pallas_call(body, name=name, grid=(2, nb), in_specs=in_specs, out_specs=out_specs, out_shape=out_shape,
                          input_output_aliases=aliases, scratch_shapes=scratch, compiler_params=_cp(2))(*args)


def _s5_chain(cfg, name, fin_local, acoef, n, inc, prev_fin=None):
    NS = cfg.NS
    nsq = int(round(math.log2(n)))
    assert 2 ** nsq == n

    def body(*refs):
        if prev_fin is not None:
            f_ref, a_ref, p_ref, o_ref = refs
        else:
            f_ref, a_ref, o_ref = refs
        for d in range(2):
            pr, pi = a_ref[d, 0, 0:1, :], a_ref[d, 1, 0:1, :]
            for _ in range(nsq):
                pr, pi = pr * pr - pi * pi, 2.0 * pr * pi
            for b in range(2):
                order = [4 * b + k for k in range(4)]
                if not inc[d]:
                    order = order[::-1]
                if prev_fin is not None:
                    last = order[-1]
                    sr, si = p_ref[d, 0, last:last + 1, :], p_ref[d, 1, last:last + 1, :]
                else:
                    sr = jnp.zeros((1, NS), F32)
                    si = jnp.zeros((1, NS), F32)
                for k in order:
                    o_ref[d, 0, k:k + 1, :] = sr
                    o_ref[d, 1, k:k + 1, :] = si
                    fr, fi = f_ref[d, 0, k:k + 1, :], f_ref[d, 1, k:k + 1, :]
                    sr, si = pr * sr - pi * si + fr, pr * si + pi * sr + fi

    args = [fin_local, acoef] + ([prev_fin] if prev_fin is not None else [])
    return pl.pallas_call(body, name=name, out_shape=_sds((2, 2, 8, NS)))(*args)


def _s5_forward(cfg, tag, tok, mod8, col_sh, col_sc, bf, cf, acoef, point=None):
    saved = {}
    fin_prev = None
    y = None
    for ph, (r0, n) in (("c", (0, cfg.nc)), ("x", (cfg.Tc, cfg.nx))):
        m8 = mod8[0 if ph == "c" else 1]
        loc = _s5_fwd_pass(cfg, f"{tag}_scan1{ph}", tok, m8, col_sh, col_sc, bf, acoef, r0, n)
        if point is not None and ph == "x":
            m8 = _tie(m8, point(loc))
        s_in = _s5_chain(cfg, f"{tag}_chain{ph}", loc, acoef, n, (True, False), fin_prev)
        y, ck, fin_prev = _s5_fwd_pass(cfg, f"{tag}_scan2{ph}", tok, m8, col_sh, col_sc, bf, acoef, r0, n, s_in, cf, y)
        saved[ph] = ck
    return y, saved


def _s5_bwd_pass(cfg, name, dy, tok, mod8, col_sh, col_sc, bf, cf, acoef, acoef_adj, r0, n, g_init=None, ck=None,
                 du_prev=None):
    D, NO, NS = cfg.D, cfg.NO, cfg.NS
    ti = cfg.ti(n)
    nb = n // ti
    R = 8 * ti
    ob = r0 // R
    second = g_init is not None
    has_dy = dy is not None
    blk = lambda d, j: ob + j + (1 - d) * (nb - 1 - 2 * j)

    def body(*refs):
        refs = list(refs)
        dy_ref = refs.pop(0) if has_dy else None
        if second:
            (tok_ref, mod_ref, bf_ref, cf_ref, a_ref, aa_ref, gi_ref, ck_ref, dup_ref,
             du_ref, da_ref, dbf_ref, dcf_ref, gfin_ref, qbuf, zbuf, gst, hst) = refs
        else:
            cf_ref, aa_ref, gfin_ref, qbuf, gst = refs
        d = pl.program_id(0)
        j = pl.program_id(1)

        @pl.when(j == 0)
        def _():
            if second:
                gst[...] = gi_ref[...]
                da_ref[...] = jnp.zeros_like(da_ref)
                dbf_ref[...] = jnp.zeros_like(dbf_ref)
                dcf_ref[...] = jnp.zeros_like(dcf_ref)
            else:
                gst[...] = jnp.zeros_like(gst)

        if has_dy:
            dyb = dy_ref[...].astype(BF16)
            for o in range(NO):
                qbuf[:, o * 1024:(o + 1) * 1024] = _dot_nt(dyb[:, o * OCT_CH:(o + 1) * OCT_CH], cf_ref[o])
        else:
            qbuf[...] = jnp.zeros_like(qbuf)
        _recur(qbuf, gst, aa_ref, NO, ti, 1 - d, second)

        if second:
            u = _mod(tok_ref[...], mod_ref[:, col_sh:col_sh + D], mod_ref[:, col_sc:col_sc + D]).astype(BF16)
            for o in range(NO):
                zbuf[:, o * 1024:(o + 1) * 1024] = _dot(u[:, o * OCT_CH:(o + 1) * OCT_CH], bf_ref[o])
            hst[...] = ck_ref[...]
            _recur(zbuf, hst, a_ref, NO, ti, d, True)

            g_off, h_off = (1 - d) * 8, d * 8
            edge = pl.multiple_of(d * (R - 8), 8)
            for o in range(0, NO, 2):
                cols = [(pl.ds(oo * 1024, OCT_ST), pl.ds(oo * 1024 + OCT_ST, OCT_ST)) for oo in (o, o + 1)]
                scol = [pl.ds(oo * OCT_ST, OCT_ST) for oo in (o, o + 1)]
                init = []
                for s in range(2):
                    er, ei = qbuf[pl.ds(edge, 8), cols[s][0]], qbuf[pl.ds(edge, 8), cols[s][1]]
                    kr, ki = ck_ref[0, :, scol[s]], ck_ref[1, :, scol[s]]
                    init += [er * kr + ei * ki, ei * kr - er * ki]

                def stp(i, carry, cols=cols):
                    rg = pl.multiple_of(i * 8 + g_off, 8)
                    rh = pl.multiple_of(i * 8 + h_off, 8)
                    out = []
                    for s in range(2):
                        gr, gi = qbuf[pl.ds(rg, 8), cols[s][0]], qbuf[pl.ds(rg, 8), cols[s][1]]
                        hr, hi = zbuf[pl.ds(rh, 8), cols[s][0]], zbuf[pl.ds(rh, 8), cols[s][1]]
                        out += [carry[2 * s] + (gr * hr + gi * hi), carry[2 * s + 1] + (gi * hr - gr * hi)]
                    return tuple(out)

                fin = lax.fori_loop(0, (ti - 1) // 2, lambda i2, cy, stp=stp: stp(2 * i2 + 1, stp(2 * i2, cy)), tuple(init))
                if (ti - 1) % 2:
                    fin = stp(ti - 2, fin)
                for s in range(2):
                    da_ref[0, :, scol[s]] += fin[2 * s]
                    da_ref[1, :, scol[s]] += fin[2 * s + 1]

            for o in range(NO):
                gb = qbuf[:, o * 1024:(o + 1) * 1024].astype(BF16)
                uo = u[:, o * OCT_CH:(o + 1) * OCT_CH]
                dbf_ref[o] += _dot_tn(uo, gb)
                if has_dy:
                    dcf_ref[o] += _dot_tn(dyb[:, o * OCT_CH:(o + 1) * OCT_CH], zbuf[:, o * 1024:(o + 1) * 1024].astype(BF16))
                du_ref[:, o * OCT_CH:(o + 1) * OCT_CH] = _dot_nt(gb, bf_ref[o])

        @pl.when(j == nb - 1)
        def _():
            gfin_ref[...] = gst[...]

    st_spec = pl.BlockSpec((None, 2, 8, NS), lambda d, j: (d, 0, 0, 0))
    row_spec = pl.BlockSpec((R, D), lambda d, j: (blk(d, j), 0))
    bf_spec = pl.BlockSpec((None, NO, OCT_CH, 1024), lambda d, j: (d, 0, 0, 0))
    cf_spec = pl.BlockSpec((None, NO, 1024, OCT_CH), lambda d, j: (d, 0, 0, 0))
    in_specs, args = [], []
    if has_dy:
        in_specs.append(row_spec)
        args.append(dy)
    if not second:
        in_specs += [cf_spec, st_spec]
        args += [cf, acoef_adj]
        return pl.pallas_call(body, name=name, grid=(2, nb), in_specs=in_specs, out_specs=st_spec,
                              out_shape=_sds((2, 2, 8, NS)),
                              scratch_shapes=[pltpu.VMEM((R, NO * 1024), F32), pltpu.VMEM((2, 8, NS), F32)],
                              compiler_params=_cp(2))(*args)
    ck_spec = pl.BlockSpec((None, None, 2, 8, NS), lambda d, j: (d, j + (1 - d) * (nb - 1 - 2 * j), 0, 0, 0))
    in_specs += [row_spec, _full(mod8.shape), bf_spec, cf_spec, st_spec, st_spec, st_spec, ck_spec]
    args += [tok, mod8, bf, cf, acoef, acoef_adj, g_init, ck]
    n_before = len(args)
    aliases = {}
    if du_prev is not None:
        in_specs.append(ANY)
        args.append(du_prev)
        aliases = {n_before: 0}
    else:
        in_specs.append(_full((8, LANES)))
        args.append(jnp.zeros((8, LANES), F32))
    out_specs = [pl.BlockSpec((None, R, D), lambda d, j: (d, blk(d, j), 0)), st_spec, bf_spec, bf_spec, st_spec]
    out_shape = [_sds((2, cfg.T, D)), _sds((2, 2, 8, NS)), _sds((2, NO, OCT_CH, 1024)), _sds((2, NO, OCT_CH, 1024)),
                 _sds((2, 2, 8, NS))]
    scratch = [pltpu.VMEM((R, NO * 1024), F32), pltpu.VMEM((R, NO * 1024), F32), pltpu.VMEM((2, 8, NS), F32),
               pltpu.VMEM((2, 8, NS), F32)]
    return pl.pallas_call(body, name=name, grid=(2, nb), in_specs=in_specs, out_specs=out_specs, out_shape=out_shape,
                          input_output_aliases=aliases, scratch_shapes=scratch, compiler_params=_cp(2))(*args)


def _s5_backward(cfg, tag, dy, dy_ctx, tok, mod8, col_sh, col_sc, bf, cf, acoef, acoef_adj, saved):
    g_prev = None
    acc = None
    du = None
    for ph, (r0, n) in (("x", (cfg.Tc, cfg.nx)), ("c", (0, cfg.nc))):
        m8 = mod8[0 if ph == "c" else 1]
        dyp = dy if (ph == "x" or dy_ctx) else None
        loc = _s5_bwd_pass(cfg, f"{tag}_adjA{ph}", dyp, tok, m8, col_sh, col_sc, bf, cf, acoef, acoef_adj, r0, n)
        g_in = _s5_chain(cfg, f"{tag}_adjchain{ph}", loc, acoef_adj, n, (False, True), g_prev)
        du, da, dbf, dcf, g_prev = _s5_bwd_pass(cfg, f"{tag}_adjB{ph}", dyp, tok, m8, col_sh, col_sc, bf, cf, acoef,
                                                acoef_adj, r0, n, g_in, saved[ph], du)
        new = (da, dbf, dcf)
        if acc is None:
            acc = new
        else:
            acc = tuple(_ew(f"{tag}_accsum{q}", lambda a, b: a + b, [a.reshape(-1, a.shape[-1]), b.reshape(-1, b.shape[-1])],
                            [_sds((a.size // a.shape[-1], a.shape[-1]))])[0].reshape(a.shape)
                        for q, (a, b) in enumerate(zip(acc, new)))
    return du, acc


def _tok_specs(cfg, rows, width, tile=None):
    tm = tile or cfg.TM
    off = rows[0] // tm
    return pl.BlockSpec((tm, width), lambda i: (i + off, 0)), rows[1] // tm, off


def _mod_spec(cfg, mod8, off):
    st = _stream_of(cfg, off, cfg.TM)
    return pl.BlockSpec((None, 8, mod8.shape[-1]), lambda i: (st(i), 0, 0))


def _wspec(w):
    fam, slot = w
    _, _, kk, nn = fam.shape
    return pl.BlockSpec((4, None, kk, nn), lambda *i: (0, slot, 0, 0), pipeline_mode=pl.Buffered(1))


def _glu_ln(cfg, name, rows, tok, y, mod8, cols, dskip, w, b, gain, bias):
    D, TM = cfg.D, cfg.TM
    csh, csc, cg = cols
    spec, nt, off = _tok_specs(cfg, rows, D)
    spec2, _, _ = _tok_specs(cfg, rows, 2 * D)

    def body(tok_ref, y_ref, mod_ref, ds_ref, w_ref, b_ref, g_ref, bi_ref, x1_ref, r1_ref, mix_ref, zz_ref, zb_ref, yy_ref):
        t = tok_ref[...]
        u = _mod(t, mod_ref[:, csh:csh + D], mod_ref[:, csc:csc + D])
        yy = ds_ref[...] * u + y_ref[0] + y_ref[1]
        zb = _gelu(yy).astype(BF16)
        zz = jnp.concatenate([_dot(zb, w_ref[s]) for s in range(4)], axis=-1) + b_ref[...]
        mix = zz[:, :D] * _sigmoid(zz[:, D:])
        r1 = DN_ALPHA * t + _rowscale(mix, mod_ref[:, cg:cg + D])
        xhat, _ = _ln_stats(r1)
        x1_ref[...] = xhat * g_ref[...] + bi_ref[...]
        r1_ref[...] = r1
        mix_ref[...] = mix
        zz_ref[...] = zz
        zb_ref[...] = zb
        yy_ref[...] = yy

    T = cfg.T
    return pl.pallas_call(
        body, name=name, grid=(nt,),
        in_specs=[spec, pl.BlockSpec((2, TM, D), lambda i: (0, i + off, 0)), _mod_spec(cfg, mod8, off), _full((1, D)),
                  _wspec(w), _full((1, 2 * D)), _full((1, D)), _full((1, D))],
        out_specs=[spec, spec, spec, spec2, spec, spec],
        out_shape=[_sds((T, D)), _sds((T, D)), _sds((T, D)), _sds((T, 2 * D)), _sds((T, D), BF16), _sds((T, D))],
        compiler_params=_cp(1))(tok, y, mod8, dskip, w[0], b, gain, bias)


def _mlp_ln(cfg, name, rows, x1, mod8, cols, w1, w2, gain, bias):
    D, TM = cfg.D, cfg.TM
    csh, csc, cg = cols
    spec, nt, off = _tok_specs(cfg, rows, D)
    specf, _, _ = _tok_specs(cfg, rows, cfg.F)
    fb = cfg.F // 4

    def body(x_ref, mod_ref, w1_ref, w2_ref, g_ref, bi_ref, x2_ref, r2_ref, out_ref, a_ref, h_ref):
        t = x_ref[...]
        h = _mod(t, mod_ref[:, csh:csh + D], mod_ref[:, csc:csc + D]).astype(BF16)
        out = jnp.zeros((TM, D), F32)
        for s in range(4):
            hid = jnp.maximum(_dot(h, w1_ref[s]), 0.0)
            a = (hid * hid).astype(BF16)
            a_ref[:, s * fb:(s + 1) * fb] = a
            out = out + _dot(a, w2_ref[s])
        r2 = DN_ALPHA * t + _rowscale(out, mod_ref[:, cg:cg + D])
        xhat, _ = _ln_stats(r2)
        x2_ref[...] = xhat * g_ref[...] + bi_ref[...]
        r2_ref[...] = r2
        out_ref[...] = out
        h_ref[...] = h

    T = cfg.T
    return pl.pallas_call(
        body, name=name, grid=(nt,),
        in_specs=[spec, _mod_spec(cfg, mod8, off), _wspec(w1), _wspec(w2), _full((1, D)), _full((1, D))],
        out_specs=[spec, spec, spec, specf, spec],
        out_shape=[_sds((T, D)), _sds((T, D)), _sds((T, D)), _sds((T, cfg.F), BF16), _sds((T, D), BF16)],
        compiler_params=_cp(1))(x1, mod8, w1[0], w2[0], gain, bias)


def _pw1_glu(cfg, name, rows, tok, mod8, cols, w, b):
    D, TM = cfg.D, cfg.TM
    csh, csc = cols
    spec, nt, off = _tok_specs(cfg, rows, D)
    spec2, _, _ = _tok_specs(cfg, rows, 2 * D)

    def body(tok_ref, mod_ref, w_ref, b_ref, aa_ref, ag_ref, h_ref):
        h = _mod(tok_ref[...], mod_ref[:, csh:csh + D], mod_ref[:, csc:csc + D]).astype(BF16)
        aa = jnp.concatenate([_dot(h, w_ref[s]) for s in range(4)], axis=-1) + b_ref[...]
        aa_ref[...] = aa
        ag_ref[...] = aa[:, :D] * _sigmoid(aa[:, D:])
        h_ref[...] = h

    T = cfg.T
    return pl.pallas_call(
        body, name=name, grid=(nt,),
        in_specs=[spec, _mod_spec(cfg, mod8, off), _wspec(w), _full((1, 2 * D))],
        out_specs=[spec2, spec, spec],
        out_shape=[_sds((T, 2 * D)), _sds((T, D)), _sds((T, D), BF16)], compiler_params=_cp(1))(tok, mod8, w[0], b)


def _halo_maps(cfg, rows):
    TM, HB = cfg.TM, cfg.HB
    off = rows[0] // TM
    nct = cfg.Tc // TM
    ntx = cfg.Tx // TM

    def phase(i):
        t = i + off
        is_x = t >= nct
        first = jnp.where(is_x, nct, 0)
        cnt = jnp.where(is_x, ntx, nct)
        return t, first, cnt

    def prev(i):
        t, first, cnt = phase(i)
        return jnp.where(t == first, 2 * (first + cnt) - 1, 2 * t - 1), 0

    def nxt(i):
        t, first, cnt = phase(i)
        return jnp.where(t == first + cnt - 1, 2 * first, 2 * t + 2), 0

    def edge(i):
        t, first, cnt = phase(i)
        return t == first, t == first + cnt - 1

    return prev, nxt, edge, off


def _halo_fix(prev, nxt, is_first, is_last):
    hb, d = prev.shape
    k = lax.broadcasted_iota(jnp.int32, (hb // 8, 8, d), 1)
    p3 = prev.reshape(hb // 8, 8, d)
    n3 = nxt.reshape(hb // 8, 8, d)
    p_roll = jnp.where((k % 4) == 0, 0.0, pltpu.roll(p3, 1, 1))
    n_roll = jnp.where((k % 4) == 3, 0.0, pltpu.roll(n3, 7, 1))
    p3 = jnp.where(is_first, p_roll, p3)
    n3 = jnp.where(is_last, n_roll, n3)
    return p3.reshape(hb, d), n3.reshape(hb, d)


def _dwconv_ln(cfg, name, rows, ag, w_dw, b_dw, ln_g, ln_b):
    D, TM, HB, KW, half = cfg.D, cfg.TM, cfg.HB, cfg.KW, cfg.half
    prev_map, next_map, edge, off = _halo_maps(cfg, rows)
    spec, nt, _ = _tok_specs(cfg, rows, D)

    def body(cur_ref, prev_ref, next_ref, w_ref, b_ref, g_ref, bi_ref, cv_ref, s_ref, ext):
        i = pl.program_id(0)
        is_first, is_last = edge(i)

        @pl.when(i >= 0)
        def _():
            p, n = _halo_fix(prev_ref[...], next_ref[...], is_first, is_last)
            ext[0:HB, :] = p
            ext[HB:HB + TM, :] = cur_ref[...]
            ext[HB + TM:, :] = n

        acc = jnp.zeros((TM, D), F32)
        for k in range(KW):
            lo = HB + 8 * (k - half)
            acc = acc + w_ref[k:k + 1, :] * ext[lo:lo + TM, :]
        cv_ref[...] = acc + b_ref[...]
        xhat, _ = _ln_stats(cv_ref[...])
        nn = xhat * g_ref[...] + bi_ref[...]
        s_ref[...] = (nn * _sigmoid(nn)).astype(BF16)

    T = cfg.T
    return pl.pallas_call(
        body, name=name, grid=(nt,),
        in_specs=[spec, pl.BlockSpec((HB, D), prev_map), pl.BlockSpec((HB, D), next_map), _full((KW, D)),
                  _full((1, D)), _full((1, D)), _full((1, D))],
        out_specs=[spec, spec], out_shape=[_sds((T, D)), _sds((T, D), BF16)],
        scratch_shapes=[pltpu.VMEM((TM + 2 * HB, D), F32)], compiler_params=_cp(1))(ag, ag, ag, w_dw, b_dw, ln_g, ln_b)


def _pw2_ln(cfg, name, rows, s, tok, mod8, cg, w, b, gain, bias):
    D, TM = cfg.D, cfg.TM
    spec, nt, off = _tok_specs(cfg, rows, D)
    kb = D // 4

    def body(s_ref, tok_ref, mod_ref, w_ref, b_ref, g_ref, bi_ref, x1_ref, r1_ref, mix_ref):
        sv = s_ref[...]
        mix = b_ref[...] + jnp.zeros((TM, D), F32)
        for q in range(4):
            mix = mix + _dot(sv[:, q * kb:(q + 1) * kb], w_ref[q])
        r1 = DN_ALPHA * tok_ref[...] + _rowscale(mix, mod_ref[:, cg:cg + D])
        xhat, _ = _ln_stats(r1)
        x1_ref[...] = xhat * g_ref[...] + bi_ref[...]
        r1_ref[...] = r1
        mix_ref[...] = mix

    T = cfg.T
    return pl.pallas_call(
        body, name=name, grid=(nt,),
        in_specs=[spec, spec, _mod_spec(cfg, mod8, off), _wspec(w), _full((1, D)), _full((1, D)), _full((1, D))],
        out_specs=[spec, spec, spec], out_shape=[_sds((T, D))] * 3, compiler_params=_cp(1))(s, tok, mod8, w[0], b, gain, bias)


def _loss(cfg, xf, tgt):
    D, TM = cfg.D, cfg.TM
    spec, nt, off = _tok_specs(cfg, cfg.rows(False), D)

    def body(x_ref, t_ref, l_ref, dx_ref, acc):
        i = pl.program_id(0)
        dlt = x_ref[...] - t_ref[...]

        @pl.when(i == 0)
        def _():
            acc[...] = jnp.zeros_like(acc)

        acc[...] += _sum8(dlt * dlt)
        dx_ref[...] = dlt * (1.0 / D)

        @pl.when(i == nt - 1)
        def _():
            l_ref[...] = jnp.zeros((8, LANES), F32) + jnp.sum(acc[...]) * (0.5 / D)

    return pl.pallas_call(
        body, name="loss", grid=(nt,),
        in_specs=[spec, pl.BlockSpec((TM, D), lambda i: (i, 0))],
        out_specs=[_full((8, LANES)), spec], out_shape=[_sds((8, LANES)), _sds((cfg.T, D))],
        scratch_shapes=[pltpu.VMEM((8, D), F32)], compiler_params=_cp(1))(xf, tgt)


def _masked_spec(cfg, rows, width, valid_from_tile):
    tm = cfg.TM
    off = rows[0] // tm
    return pl.BlockSpec((tm, width), lambda i: (jnp.maximum(i + off, valid_from_tile), 0))


def _lnb(cfg, name, rows, dres, dres_ctx_ok, dh, r, aux, gain, mod_gate, cg, mod_next, csc):
    D, TM = cfg.D, cfg.TM
    spec, nt, off = _tok_specs(cfg, rows, D)
    nct = cfg.Tc // TM
    has_dres, has_dh = dres is not None, dh is not None

    def body(*refs):
        refs = list(refs)
        dres_ref = refs.pop(0) if has_dres else None
        dh_ref = refs.pop(0) if has_dh else None
        r_ref, aux_ref, g_ref, mg_ref = refs[:4]
        refs = refs[4:]
        mn_ref = refs.pop(0) if has_dh else None
        dprev_ref, dbr_ref, dgain_ref, dbias_ref, dg_ref, dsc_ref, dsh_ref, acc_g, acc_b = refs
        i = pl.program_id(0)
        t = i + off
        first_of_stream = (i == 0) | (t == nct)
        xhat, rstd = _ln_stats(r_ref[...])
        dy = jnp.zeros((TM, D), F32)
        if has_dres:
            dv = dres_ref[...]
            if not dres_ctx_ok:
                dv = jnp.where(t >= nct, dv, 0.0)
            dy = dy + dv
        if has_dh:
            dhv = dh_ref[...]
            dy = dy + _rowscale(dhv, 1.0 + mn_ref[:, csc:csc + D])
            x_out = xhat * g_ref[0:1, :] + g_ref[1:2, :]
            s_sc, s_sh = _sum8(dhv * x_out), _sum8(dhv)
        else:
            s_sc = s_sh = jnp.zeros((8, D), F32)
        dr = _ln_bwd(dy * g_ref[0:1, :], xhat, rstd)
        s_g = _sum8(dr * aux_ref[...])

        @pl.when(i == 0)
        def _():
            acc_g[...] = jnp.zeros_like(acc_g)
            acc_b[...] = jnp.zeros_like(acc_b)

        acc_g[...] += _sum8(dy * xhat)
        acc_b[...] += _sum8(dy)

        @pl.when(first_of_stream)
        def _():
            dg_ref[...] = s_g
            dsc_ref[...] = s_sc
            dsh_ref[...] = s_sh

        @pl.when(jnp.logical_not(first_of_stream))
        def _():
            dg_ref[...] += s_g
            dsc_ref[...] += s_sc
            dsh_ref[...] += s_sh

        dprev_ref[...] = DN_ALPHA * dr
        dbr_ref[...] = _rowscale(dr, mg_ref[:, cg:cg + D])

        @pl.when(i == nt - 1)
        def _():
            dgain_ref[...] = jnp.sum(acc_g[...], axis=0, keepdims=True)
            dbias_ref[...] = jnp.sum(acc_b[...], axis=0, keepdims=True)

    st = _stream_of(cfg, off, TM)
    in_specs, args = [], []
    if has_dres:
        in_specs.append(spec if dres_ctx_ok else _masked_spec(cfg, rows, D, nct))
        args.append(dres)
    if has_dh:
        in_specs.append(spec)
        args.append(dh)
    in_specs += [spec, spec, _full((2, D)), _mod_spec(cfg, mod_gate, off)]
    args += [r, aux, gain, mod_gate]
    if has_dh:
        in_specs.append(_mod_spec(cfg, mod_next, off))
        args.append(mod_next)
    acc_spec = pl.BlockSpec((None, 8, D), lambda i: (st(i), 0, 0))
    T = cfg.T
    return pl.pallas_call(
        body, name=name, grid=(nt,), in_specs=in_specs,
        out_specs=[spec, spec, _full((1, D)), _full((1, D)), acc_spec, acc_spec, acc_spec],
        out_shape=[_sds((T, D)), _sds((T, D)), _sds((1, D)), _sds((1, D)), _sds((2, 8, D)), _sds((2, 8, D)), _sds((2, 8, D))],
        scratch_shapes=[pltpu.VMEM((8, D), F32), pltpu.VMEM((8, D), F32)], compiler_params=_cp(1))(*args)


def _mlp_bwd(cfg, name, rows, dbr, a, w1, w2):
    D, TM = cfg.D, cfg.TM
    spec, nt, off = _tok_specs(cfg, rows, D)
    specf, _, _ = _tok_specs(cfg, rows, cfg.F)
    fb = cfg.F // 4

    def body(d_ref, a_ref, w1_ref, w2_ref, dh_ref, dhid_ref, dout_ref):
        dout = d_ref[...].astype(BF16)
        dh = jnp.zeros((TM, D), F32)
        for s in range(4):
            da = _dot_nt(dout, w2_ref[s])
            dhid = (da * (2.0 * jnp.sqrt(a_ref[:, s * fb:(s + 1) * fb].astype(F32)))).astype(BF16)
            dhid_ref[:, s * fb:(s + 1) * fb] = dhid
            dh = dh + _dot_nt(dhid, w1_ref[s])
        dh_ref[...] = dh
        dout_ref[...] = dout

    T = cfg.T
    return pl.pallas_call(
        body, name=name, grid=(nt,),
        in_specs=[spec, specf, _wspec(w1), _wspec(w2)],
        out_specs=[spec, specf, spec],
        out_shape=[_sds((T, D)), _sds((T, cfg.F), BF16), _sds((T, D), BF16)], compiler_params=_cp(1))(dbr, a, w1[0], w2[0])


def _wgrad(cfg, name, rows, a, b, mode, fam, slot):
    tw = cfg.TW
    off = rows[0] // tw
    ntile = rows[1] // tw
    tps = next(q for q in (4, 3, 2, 1) if ntile % q == 0)
    nt = ntile // tps
    fresh = not hasattr(fam, "dtype")
    fam_shape = tuple(fam) if fresh else fam.shape
    _, n, kk, nn = fam_shape

    def body(*refs):
        a_refs, b_refs, o_ref = refs[:tps], refs[tps:2 * tps], refs[-1]
        t = pl.program_id(1)
        part = _dot_tn(a_refs[0][...], b_refs[0][...])
        for q in range(1, tps):
            part = part + _dot_tn(a_refs[q][...], b_refs[q][...])

        @pl.when(t == 0)
        def _():
            o_ref[...] = part

        @pl.when(t > 0)
        def _():
            o_ref[...] += part

    def row(q):
        return lambda s, t: t * tps + q + off

    if mode == "col":
        a_specs = [pl.BlockSpec((tw, kk), lambda s, t, r=row(q): (r(s, t), 0)) for q in range(tps)]
        b_specs = [pl.BlockSpec((tw, nn), lambda s, t, r=row(q): (r(s, t), s)) for q in range(tps)]
    else:
        a_specs = [pl.BlockSpec((tw, kk), lambda s, t, r=row(q): (r(s, t), s)) for q in range(tps)]
        b_specs = [pl.BlockSpec((tw, nn), lambda s, t, r=row(q): (r(s, t), 0)) for q in range(tps)]
    out_spec = pl.BlockSpec((None, None, kk, nn), lambda s, t: (s, slot, 0, 0))
    ins = [a] * tps + [b] * tps
    if fresh:
        return pl.pallas_call(body, name=name, grid=(4, nt), in_specs=a_specs + b_specs, out_specs=out_spec,
                              out_shape=_sds(fam_shape), compiler_params=_cp(2))(*ins)
    return pl.pallas_call(body, name=name, grid=(4, nt), in_specs=a_specs + b_specs + [ANY], out_specs=out_spec,
                          out_shape=_sds(fam_shape), input_output_aliases={2 * tps: 0}, compiler_params=_cp(2))(*ins, fam)


def _glu_bwd(cfg, name, rows, dmix, pre, w, yy=None):
    D, TM = cfg.D, cfg.TM
    spec, nt, off = _tok_specs(cfg, rows, D)
    spec2, _, _ = _tok_specs(cfg, rows, 2 * D)
    hw = w[0].shape[-1]
    has_y = yy is not None

    def body(*refs):
        refs = list(refs)
        d_ref, p_ref, w_ref = refs[:3]
        y_ref = refs[3] if has_y else None
        dz_ref, dp_ref, db_ref, acc = refs[-4:]
        i = pl.program_id(0)
        dm = d_ref[...]
        po, pg = p_ref[:, :D], p_ref[:, D:]
        sg = _sigmoid(pg)
        dpre = jnp.concatenate([dm * sg, dm * po * sg * (1.0 - sg)], axis=-1)

        @pl.when(i == 0)
        def _():
            acc[...] = jnp.zeros_like(acc)

        acc[...] += _sum8(dpre)
        dpb = dpre.astype(BF16)
        dz = jnp.zeros((TM, D), F32)
        for s in range(4):
            dz = dz + _dot_nt(dpb[:, s * hw:(s + 1) * hw], w_ref[s])
        if has_y:
            dz = dz * _gelu_grad(y_ref[...])
        dz_ref[...] = dz
        dp_ref[...] = dpb

        @pl.when(i == nt - 1)
        def _():
            db_ref[...] = jnp.sum(acc[...], axis=0, keepdims=True)

    T = cfg.T
    in_specs = [spec, spec2, _wspec(w)] + ([spec] if has_y else [])
    args = [dmix, pre, w[0]] + ([yy] if has_y else [])
    return pl.pallas_call(
        body, name=name, grid=(nt,), in_specs=in_specs, out_specs=[spec, spec2, _full((1, 2 * D))],
        out_shape=[_sds((T, D)), _sds((T, 2 * D), BF16), _sds((1, 2 * D))],
        scratch_shapes=[pltpu.VMEM((8, 2 * D), F32)], compiler_params=_cp(1))(*args)


def _s5_du(cfg, name, rows, du, dy, dy_from_tile, tok, mod8, cols, dskip):
    D, TM = cfg.D, cfg.TM
    csh, csc = cols
    spec, nt, off = _tok_specs(cfg, rows, D)

    def body(du_ref, dy_ref, tok_ref, mod_ref, ds_ref, dh_ref, dd_ref, acc):
        i = pl.program_id(0)
        dyv = jnp.where(i + off >= dy_from_tile, dy_ref[...], 0.0)
        u = _mod(tok_ref[...], mod_ref[:, csh:csh + D], mod_ref[:, csc:csc + D])
        dh_ref[...] = du_ref[0] + du_ref[1] + ds_ref[...] * dyv

        @pl.when(i == 0)
        def _():
            acc[...] = jnp.zeros_like(acc)

        acc[...] += _sum8(dyv * u)

        @pl.when(i == nt - 1)
        def _():
            dd_ref[...] = jnp.sum(acc[...], axis=0, keepdims=True)

    T = cfg.T
    return pl.pallas_call(
        body, name=name, grid=(nt,),
        in_specs=[pl.BlockSpec((2, TM, D), lambda i: (0, i + off, 0)), _masked_spec(cfg, rows, D, dy_from_tile), spec,
                  _mod_spec(cfg, mod8, off), _full((1, D))],
        out_specs=[spec, _full((1, D))], out_shape=[_sds((T, D)), _sds((1, D))],
        scratch_shapes=[pltpu.VMEM((8, D), F32)], compiler_params=_cp(1))(du, dy, tok, mod8, dskip)


def _pw2_bwd(cfg, name, rows, dmix, cv, w, ln_g, ln_b):
    D, TM = cfg.D, cfg.TM
    spec, nt, off = _tok_specs(cfg, rows, D)
    kb = D // 4

    def body(d_ref, cv_ref, w_ref, g_ref, b_ref, dcv_ref, dmb_ref, sums_ref, acc):
        i = pl.program_id(0)
        dm = d_ref[...]
        dmb = dm.astype(BF16)
        ds = jnp.concatenate([_dot_nt(dmb, w_ref[q]) for q in range(4)], axis=-1)
        xhat, rstd = _ln_stats(cv_ref[...])
        nn = xhat * g_ref[...] + b_ref[...]
        sg = _sigmoid(nn)
        dn = ds * (sg * (1.0 + nn * (1.0 - sg)))
        dcv = _ln_bwd(dn * g_ref[...], xhat, rstd)

        @pl.when(i == 0)
        def _():
            acc[...] = jnp.zeros_like(acc)

        acc[0] += _sum8(dn * xhat)
        acc[1] += _sum8(dn)
        acc[2] += _sum8(dcv)
        acc[3] += _sum8(dm)
        dcv_ref[...] = dcv
        dmb_ref[...] = dmb

        @pl.when(i == nt - 1)
        def _():
            for q in range(4):
                sums_ref[q:q + 1, :] = jnp.sum(acc[q], axis=0, keepdims=True)

    T = cfg.T
    return pl.pallas_call(
        body, name=name, grid=(nt,),
        in_specs=[spec, spec, _wspec(w), _full((1, D)), _full((1, D))],
        out_specs=[spec, spec, _full((4, D))], out_shape=[_sds((T, D)), _sds((T, D), BF16), _sds((4, D))],
        scratch_shapes=[pltpu.VMEM((4, 8, D), F32)], compiler_params=_cp(1))(dmix, cv, w[0], ln_g, ln_b)


def _dwconv_bwd(cfg, name, rows, dcv, ag, w_dw):
    D, TM, HB, KW, half = cfg.D, cfg.TM, cfg.HB, cfg.KW, cfg.half
    prev_map, next_map, edge, off = _halo_maps(cfg, rows)
    spec, nt, _ = _tok_specs(cfg, rows, D)

    def body(dc_ref, dp_ref, dn_ref, ac_ref, ap_ref, an_ref, w_ref, dag_ref, dw_ref, extd, exta, acc):
        i = pl.program_id(0)
        is_first, is_last = edge(i)

        @pl.when(i >= 0)
        def _():
            p, n = _halo_fix(dp_ref[...], dn_ref[...], is_first, is_last)
            extd[0:HB, :] = p
            extd[HB:HB + TM, :] = dc_ref[...]
            extd[HB + TM:, :] = n
            p, n = _halo_fix(ap_ref[...], an_ref[...], is_first, is_last)
            exta[0:HB, :] = p
            exta[HB:HB + TM, :] = ac_ref[...]
            exta[HB + TM:, :] = n

        @pl.when(i == 0)
        def _():
            acc[...] = jnp.zeros_like(acc)

        cr = min(CONV_ROWS, TM)
        for r0 in range(0, TM, cr):
            for lc in range(D // LANES):
                ls = pl.ds(lc * LANES, LANES)
                dcur = dc_ref[r0:r0 + cr, ls]
                dag = jnp.zeros((cr, LANES), F32)
                for k in range(KW):
                    lo = r0 + HB + 8 * (half - k)
                    la = r0 + HB + 8 * (k - half)
                    dag = dag + w_ref[k:k + 1, ls] * extd[lo:lo + cr, ls]
                    acc[k, :, ls] += _sum8(dcur * exta[la:la + cr, ls])
                dag_ref[r0:r0 + cr, ls] = dag

        @pl.when(i == nt - 1)
        def _():
            for k in range(KW):
                dw_ref[k:k + 1, :] = jnp.sum(acc[k], axis=0, keepdims=True)

    T = cfg.T
    hp, hn = pl.BlockSpec((HB, D), prev_map), pl.BlockSpec((HB, D), next_map)
    return pl.pallas_call(
        body, name=name, grid=(nt,), in_specs=[spec, hp, hn, spec, hp, hn, _full((KW, D))],
        out_specs=[spec, _full((KW, D))], out_shape=[_sds((T, D)), _sds((KW, D))],
        scratch_shapes=[pltpu.VMEM((TM + 2 * HB, D), F32), pltpu.VMEM((TM + 2 * HB, D), F32), pltpu.VMEM((KW, 8, D), F32)],
        compiler_params=_cp(1))(dcv, dcv, dcv, ag, ag, ag, w_dw)


def _input_bwd(cfg, dres, dh, tok0, mod8, csc):
    D, TM = cfg.D, cfg.TM
    rows = cfg.rows(True)
    spec, nt, off = _tok_specs(cfg, rows, D)
    nct = cfg.Tc // TM
    st = _stream_of(cfg, off, TM)

    def body(dr_ref, dh_ref, t_ref, mod_ref, gx_ref, dsc_ref, dsh_ref):
        i = pl.program_id(0)
        dhv = dh_ref[...]
        gx_ref[...] = dr_ref[...] + _rowscale(dhv, 1.0 + mod_ref[:, csc:csc + D])
        first = (i == 0) | (i == nct)
        s_sc, s_sh = _sum8(dhv * t_ref[...]), _sum8(dhv)

        @pl.when(first)
        def _():
            dsc_ref[...] = s_sc
            dsh_ref[...] = s_sh

        @pl.when(jnp.logical_not(first))
        def _():
            dsc_ref[...] += s_sc
            dsh_ref[...] += s_sh

    acc_spec = pl.BlockSpec((None, 8, D), lambda i: (st(i), 0, 0))
    return pl.pallas_call(
        body, name="input_bwd", grid=(nt,), in_specs=[spec, spec, spec, _mod_spec(cfg, mod8, off)],
        out_specs=[spec, acc_spec, acc_spec], out_shape=[_sds((cfg.T, D)), _sds((2, 8, D)), _sds((2, 8, D))],
        compiler_params=_cp(1))(dres, dh, tok0, mod8)


def _dmod_rows(dmod8):
    nl, _, _, w = dmod8.shape

    def body(d_ref, o_ref):
        xs = d_ref[1]
        cs = d_ref[0]
        o_ref[...] = jnp.zeros((8, w), F32)
        o_ref[0:1, :] = jnp.sum(xs[0:4], axis=0, keepdims=True)
        o_ref[1:2, :] = jnp.sum(xs[4:8], axis=0, keepdims=True)
        o_ref[2:3, :] = jnp.sum(cs, axis=0, keepdims=True)

    return pl.pallas_call(body, name="dmod_rows", grid=(nl,),
                          in_specs=[pl.BlockSpec((None, 2, 8, w), lambda l: (l, 0, 0, 0))],
                          out_specs=pl.BlockSpec((None, 8, w), lambda l: (l, 0, 0)), out_shape=_sds((nl, 8, w)),
                          compiler_params=_cp(1))(dmod8)


def _x_only(acc):
    return jnp.concatenate([jnp.zeros_like(acc[:1]), acc[1:]], axis=0)


def _pack(parts):
    bufs, meta, off = [], [], 0
    for p in parts:
        n = p.size
        rows = -(-n // (8 * LANES)) * 8
        flat = jnp.pad(p.reshape(-1).astype(F32), (0, rows * LANES - n)).reshape(rows, LANES)
        bufs.append(flat)
        meta.append((off, rows, p.shape))
        off += rows
    if off % 16:
        bufs.append(jnp.zeros((8, LANES), F32))
    return jnp.concatenate(bufs, axis=0), meta


def _unpack(buf, meta):
    out = []
    for off, rows, shape in meta:
        n = 1
        for s in shape:
            n *= s
        out.append(buf[off:off + rows].reshape(-1)[:n].reshape(shape))
    return out


def kernel(x, c, ctx, c_ctx, w_ada, b_ada, ln_gain, ln_bias, s5_lam_re, s5_lam_im, s5_log_dt, s5_b_re, s5_b_im, s5_c_re, s5_c_im, s5_d, s5_w_glu, s5_b_glu, cv_w_pw1, cv_b_pw1, cv_w_dw, cv_b_dw, cv_ln_g, cv_ln_b, cv_w_pw2, cv_b_pw2, mlp_w1, mlp_w2, loss_target, m_c_ctx, m_w_ada, m_b_ada, m_ln_gain, m_ln_bias, m_s5_lam_re, m_s5_lam_im, m_s5_log_dt, m_s5_b_re, m_s5_b_im, m_s5_c_re, m_s5_c_im, m_s5_d, m_s5_w_glu, m_s5_b_glu, m_cv_w_pw1, m_cv_b_pw1, m_cv_w_dw, m_cv_b_dw, m_cv_ln_g, m_cv_ln_b, m_cv_w_pw2, m_cv_b_pw2, m_mlp_w1, m_mlp_w2, v_c_ctx, v_w_ada, v_b_ada, v_ln_gain, v_ln_bias, v_s5_lam_re, v_s5_lam_im, v_s5_log_dt, v_s5_b_re, v_s5_b_im, v_s5_c_re, v_s5_c_im, v_s5_d, v_s5_w_glu, v_s5_b_glu, v_cv_w_pw1, v_cv_b_pw1, v_cv_w_dw, v_cv_b_dw, v_cv_ln_g, v_cv_ln_b, v_cv_w_pw2, v_cv_b_pw2, v_mlp_w1, v_mlp_w2):
    cfg = _Cfg(x, ctx, mlp_w1, cv_w_dw)
    D, T, Tc, Tx, B = cfg.D, cfg.T, cfg.Tc, cfg.Tx, cfg.B
    ax, ay, ac = lax.axis_index("x"), lax.axis_index("y"), lax.axis_index("c")
    shard = 2 * ax + ay
    dev = 4 * ax + 2 * ay + ac
    Ds = D // 4
    Wa = w_ada.shape[2]

    c_pad = jnp.concatenate([c, jnp.zeros((8 - B, D), F32)], axis=0)
    c_gath = _allgather8("gather_c", c_pad).reshape(8, 8, D)[:, :B].reshape(8 * B, D)
    c_all = jnp.concatenate([c_gath, c_ctx[None], jnp.zeros((7, D), F32)], axis=0)
    b_sh = lax.dynamic_slice_in_dim(b_ada, shard * Wa, Wa, axis=1)[:, None, :]
    mod_sh = _ada_fwd(c_all, w_ada, b_sh)
    mod_g = _allgather8("gather_mod", mod_sh.reshape(DEPTH * 24, Wa)).reshape(4, 2, DEPTH, 24, Wa)[:, 0]
    mods = mod_g.transpose(1, 2, 0, 3).reshape(DEPTH, 24, 4 * Wa)
    mine = lax.dynamic_slice_in_dim(mods, B * dev, B, axis=1)
    mod8 = jnp.stack([jnp.broadcast_to(mods[:, 16:17], (DEPTH, 8, 6 * D)), jnp.repeat(mine, 4, axis=1)], axis=1)
    SH1, SC1, G1, SH2, SC2, G2 = (k * D for k in range(6))

    small_parts = [ln_gain.reshape(-1, Ds), ln_bias.reshape(-1, Ds), cv_b_pw1.reshape(-1, Ds), cv_w_dw.reshape(-1, Ds),
                   cv_b_dw, cv_ln_g, cv_ln_b, cv_b_pw2]
    small_rows = [p.shape[0] for p in small_parts]
    sm = jnp.concatenate(small_parts, axis=0)
    pad_r = -sm.shape[0] % 8
    sm = jnp.pad(sm, ((0, pad_r), (0, 0)))
    sm_g = _allgather8("gather_small", sm).reshape(4, 2, sm.shape[0], Ds)[:, 0]
    pieces, o = [], 0
    for nr in small_rows:
        pieces.append(sm_g[:, o:o + nr])
        o += nr

    def unshard(p, lead):
        return p.reshape((4,) + lead + (Ds,)).transpose(tuple(range(1, len(lead) + 1)) + (0, len(lead) + 1)).reshape(lead + (4 * Ds,))

    ln_gain_f = unshard(pieces[0], (DEPTH, 2))
    ln_bias_f = unshard(pieces[1], (DEPTH, 2))
    nconv = cv_w_dw.shape[0]
    b_pw1_f = pieces[2].reshape(4, nconv, 2 * D // 4).transpose(1, 0, 2).reshape(nconv, 2 * D)
    w_dw_f = unshard(pieces[3], (nconv, cfg.KW))
    b_dw_f, cvg_f, cvb_f, b_pw2_f = (unshard(p, (nconv,)) for p in pieces[4:8])

    ns5 = s5_w_glu.shape[0]
    assert mlp_w1.shape[1:] == mlp_w2.shape[1:]
    fam_a = _place_shard("place_w1", mlp_w1, None, 0, 2 * DEPTH)
    fam_a = _place_shard("place_w2", mlp_w2, fam_a, DEPTH, 2 * DEPTH)
    fam_b = _place_shard("place_wglu", s5_w_glu, None, 0, ns5 + nconv)
    fam_b = _place_shard("place_wpw1", cv_w_pw1, fam_b, ns5, ns5 + nconv)
    fam_c = _place_shard("place_wpw2", cv_w_pw2, None, 0, nconv)
    ov = _Overlap()
    gather_tokens = ov.add("gather_b", _gather_gen("gatherb", [fam_b])) + ov.add("gather", _gather_gen("gatherw", [fam_a, fam_c]))

    pos = jnp.broadcast_to(_pos_embed(cfg.L // GRID_W, D)[None], (B, cfg.L, D))
    tok_in = jnp.concatenate([_to_perm(ctx), _to_perm(x)], axis=0)
    pos_in = jnp.concatenate([jnp.zeros((Tc, D), F32), _to_perm(pos)], axis=0)
    tok0 = _ew("add_pos", lambda a, b: a + b, [tok_in, pos_in], [_sds((T, D))])[0]
    mod8 = _tie(mod8, gather_tokens)
    tgt = _to_perm(loss_target)

    s5p = []
    for j in range(ns5):
        lay = _s5_layouts(cfg, s5_lam_re[j], s5_lam_im[j], s5_log_dt[j], s5_b_re[j], s5_b_im[j])
        abr, abi, bbr, bbi = _disc_fwd(*lay)
        s5p.append(dict(lay=lay, acoef=_coef_rows(cfg, abr, abi, False), acoef_adj=_coef_rows(cfg, abr, abi, True),
                        bf=_blockdiag_b(cfg, bbr, bbi), cf=_blockdiag_c(cfg, s5_c_re[j], s5_c_im[j])))

    kinds = ["s5" if i % 2 == 0 else "conv" for i in range(DEPTH)]
    tok = tok0
    saved = []
    s5_j = cv_j = 0
    for i in range(DEPTH):
        later_s5 = any(k == "s5" for k in kinds[i + 1:])
        rows = cfg.rows(later_s5)
        m8 = mod8[i]
        sv = dict(tok=tok, rows=rows, kind=kinds[i])
        g0, b0 = ln_gain_f[i, 0][None], ln_bias_f[i, 0][None]
        g1, b1 = ln_gain_f[i, 1][None], ln_bias_f[i, 1][None]
        if kinds[i] == "s5":
            j = s5_j
            s5_j += 1
            p = s5p[j]
            y, ck = _s5_forward(cfg, f"l{i}", tok, m8, SH1, SC1, p["bf"], p["cf"], p["acoef"], ov.point if i == 0 else None)
            m8g = m8
            if i == 0:
                m8g = _tie(m8, ov.point(y))
                (wb_full,) = ov.finish("gather_b", y)
            wg = (wb_full, j)
            x1, r1, mix, zz, zb, yy = _glu_ln(cfg, f"l{i}_glu", rows, tok, y, m8g, (SH1, SC1, G1), s5_d[j][None], wg,
                                              s5_b_glu[j][None], g0, b0)
            sv.update(j=j, ck=ck, zz=zz, zb=zb, yy=yy, wg=wg)
            if i == 0:
                wa_full, wc_full = ov.finish("gather", x1)
        else:
            j = cv_j
            cv_j += 1
            w1c, w2c = (wb_full, ns5 + j), (wc_full, j)
            aa, ag, hb = _pw1_glu(cfg, f"l{i}_pw1", rows, tok, m8, (SH1, SC1), w1c, b_pw1_f[j][None])
            cvv, sb = _dwconv_ln(cfg, f"l{i}_dw", rows, ag, w_dw_f[j], b_dw_f[j][None], cvg_f[j][None], cvb_f[j][None])
            x1, r1, mix = _pw2_ln(cfg, f"l{i}_pw2", rows, sb, tok, m8, G1, w2c, b_pw2_f[j][None], g0, b0)
            sv.update(j=j, aa=aa, ag=ag, hb=hb, cvv=cvv, sb=sb, w1c=w1c, w2c=w2c)
        w1m, w2m = (wa_full, i), (wa_full, DEPTH + i)
        x2, r2, mout, am, hm = _mlp_ln(cfg, f"l{i}_mlp", rows, x1, m8, (SH2, SC2, G2), w1m, w2m, g1, b1)
        sv.update(r1=r1, mix=mix, x1=x1, r2=r2, mout=mout, am=am, hm=hm, w1m=w1m, w2m=w2m, g0=g0, b0=b0, g1=g1, b1=b1)
        saved.append(sv)
        tok = x2

    loss8, dxf = _loss(cfg, tok, tgt)
    loss = lax.psum(loss8[0, 0], ("x", "y", "c"))

    dmod8 = [None] * DEPTH
    g_ln_gain = [[None, None] for _ in range(DEPTH)]
    g_ln_bias = [[None, None] for _ in range(DEPTH)]
    g_s5 = [None] * ns5
    g_cv = [None] * nconv
    dres, dh = dxf, None
    pend = []
    for i in reversed(range(DEPTH)):
        sv = saved[i]
        rows = sv["rows"]
        m8 = mod8[i]
        nxt_m8 = mod8[i + 1] if i + 1 < DEPTH else None
        ctx_ok = True if i + 1 >= DEPTH else (saved[i + 1]["rows"][0] == 0)
        if rows[0] != 0:
            ctx_ok = True
        dprev, dbr, dgn, dbs, dg2, dsc_n, dsh_n = _lnb(
            cfg, f"l{i}_lnb2", rows, dres, ctx_ok, dh, sv["r2"], sv["mout"],
            _tie(jnp.concatenate([sv["g1"], sv["b1"]], 0), pend), m8, G2, nxt_m8, SC1)
        if rows[0] != 0:
            dg2, dsc_n, dsh_n = (_x_only(t) for t in (dg2, dsc_n, dsh_n))
        g_ln_gain[i][1], g_ln_bias[i][1] = dgn[0], dbs[0]
        if i + 1 < DEPTH:
            dmod8[i + 1]["sc1"], dmod8[i + 1]["sh1"] = dsc_n, dsh_n
        dmod8[i] = dict(g2=dg2)
        dh2, dhid, dout = _mlp_bwd(cfg, f"l{i}_mlpb", rows, dbr, sv["am"], sv["w1m"], sv["w2m"])
        pend = ov.point(dh2)
        ga = _wgrad(cfg, f"l{i}_gw1", rows, sv["hm"], dhid, "col", (4, 2, D, cfg.F // 4), 0)
        ga = _wgrad(cfg, f"l{i}_gw2", rows, sv["am"], dout, "row", ga, 1)
        dprev1, dbr1, dgn, dbs, dg1, dsc2, dsh2 = _lnb(
            cfg, f"l{i}_lnb1", rows, dprev, True, dh2, sv["r1"], sv["mix"],
            _tie(jnp.concatenate([sv["g0"], sv["b0"]], 0), pend), m8, G1, m8, SC2)
        if rows[0] != 0:
            dg1, dsc2, dsh2 = (_x_only(t) for t in (dg1, dsc2, dsh2))
        g_ln_gain[i][0], g_ln_bias[i][0] = dgn[0], dbs[0]
        dmod8[i].update(g1=dg1, sc2=dsc2, sh2=dsh2)
        j = sv["j"]
        if sv["kind"] == "s5":
            p = s5p[j]
            dyy, dzz, dbglu = _glu_bwd(cfg, f"l{i}_glub", rows, dbr1, sv["zz"], sv["wg"], sv["yy"])
            pend = ov.point(dyy)
            gb = _wgrad(cfg, f"l{i}_gwg", rows, sv["zb"], dzz, "col", (4, 1, D, D // 2), 0)
            du, (da, dbf, dcf) = _s5_backward(cfg, f"l{i}", dyy, rows[0] == 0, sv["tok"], m8, SH1, SC1, p["bf"], p["cf"],
                                              p["acoef"], _tie(p["acoef_adj"], pend), sv["ck"])
            dh, dds = _s5_du(cfg, f"l{i}_du", cfg.rows(True), du, dyy, rows[0] // cfg.TM, sv["tok"], m8, (SH1, SC1),
                             s5_d[j][None])
            g_s5[j] = dict(da=da, dbf=dbf, dcf=dcf, dd=dds[0], dbglu=dbglu[0])
            layer_grads = [ga, gb]
        else:
            dcv, dmb, sums = _pw2_bwd(cfg, f"l{i}_pw2b", rows, dbr1, sv["cvv"], sv["w2c"], cvg_f[j][None], cvb_f[j][None])
            pend = ov.point(dcv)
            gc = _wgrad(cfg, f"l{i}_gwp2", rows, sv["sb"], dmb, "row", (4, 1, D // 4, D), 0)
            dag, dwdw = _dwconv_bwd(cfg, f"l{i}_dwb", rows, dcv, sv["ag"], _tie(w_dw_f[j], pend))
            dh, daa, dbpw1 = _glu_bwd(cfg, f"l{i}_pw1b", rows, dag, sv["aa"], sv["w1c"])
            gb = _wgrad(cfg, f"l{i}_gwp1", rows, sv["hb"], daa, "col", (4, 1, D, D // 2), 0)
            g_cv[j] = dict(ln_g=sums[0], ln_b=sums[1], b_dw=sums[2], b_pw2=sums[3], w_dw=dwdw, b_pw1=dbpw1[0])
            layer_grads = [ga, gb, gc]
        dres = dprev1
        pend = ov.point(dh) + ov.add(f"rs{i}", _reduce_scatter_gen(f"gw{i}", layer_grads))
    gx_perm, dsc0, dsh0 = _input_bwd(cfg, dres, dh, tok0, _tie(mod8[0], pend), SC1)
    dmod8[0]["sc1"], dmod8[0]["sh1"] = dsc0, dsh0
    grad_x = _from_perm(gx_perm[Tc:], B, cfg.L)

    zero28 = jnp.zeros((2, 8, D), F32)
    dm8 = jnp.stack([jnp.concatenate([dmod8[i].get(k, zero28) for k in ("sh1", "sc1", "g1", "sh2", "sc2", "g2")], axis=-1)
                     for i in range(DEPTH)])
    dm_rows = _dmod_rows(dm8)
    dm_tab = jnp.zeros((DEPTH, 24, 6 * D), F32)
    dm_tab = lax.dynamic_update_slice_in_dim(dm_tab, dm_rows[:, 0:B], B * dev, axis=1)
    dm_tab = lax.dynamic_update_slice_in_dim(dm_tab, dm_rows[:, 2:3], 16, axis=1)

    eye_parts = []
    for j in range(ns5):
        g = g_s5[j]
        dbbr, dbbi = _diag_b(cfg, g["dbf"])
        dcr, dci = _diag_c(cfg, g["dcf"])
        eye_parts += [g["da"], dbbr, dbbi, dcr, dci, g["dd"], g["dbglu"]]
    for j in range(nconv):
        g = g_cv[j]
        eye_parts += [g["ln_g"], g["ln_b"], g["b_dw"], g["b_pw2"], g["w_dw"], g["b_pw1"]]
    eye_parts += [jnp.stack([jnp.stack(r) for r in g_ln_gain]), jnp.stack([jnp.stack(r) for r in g_ln_bias]), dm_tab]
    buf, meta = _pack(eye_parts)
    buf = _tie(buf, ov.point(gx_perm))
    red_buf = _allreduce8("small", buf, ov.point)
    reduced = [ov.finish(f"rs{i}", red_buf) for i in range(DEPTH)]
    red = _unpack(red_buf, meta)

    grads = {}
    k = 0
    lam_re_g, lam_im_g, ldt_g, bre_g, bim_g, cre_g, cim_g, dd_g, bglu_g = ([] for _ in range(9))
    for j in range(ns5):
        da, dbbr, dbbi, dcr, dci, dd, dbglu = red[k:k + 7]
        k += 7
        da_s = _sublane_sum(f"s5_dasum_{j}", da.reshape(4, 8, cfg.NS)).reshape(2, 2, cfg.NS)
        g_abr = da_s[:, 0].reshape(2, cfg.G, cfg.P).transpose(2, 0, 1).reshape(cfg.P, 2 * cfg.G)
        g_abi = da_s[:, 1].reshape(2, cfg.G, cfg.P).transpose(2, 0, 1).reshape(cfg.P, 2 * cfg.G)
        glr, gli, gldt, gbr, gbi = _disc_bwd(*s5p[j]["lay"], g_abr, g_abi, dbbr, dbbi)
        lam_re_g.append(glr.reshape(cfg.P, 2, cfg.G).transpose(1, 2, 0))
        lam_im_g.append(gli.reshape(cfg.P, 2, cfg.G).transpose(1, 2, 0))
        ldt_g.append(gldt.reshape(2, cfg.G))
        bre_g.append(gbr.reshape(S5_GROUP, cfg.P, 2, cfg.G).transpose(2, 3, 1, 0))
        bim_g.append(gbi.reshape(S5_GROUP, cfg.P, 2, cfg.G).transpose(2, 3, 1, 0))
        cre_g.append(dcr)
        cim_g.append(dci)
        dd_g.append(dd)
        bglu_g.append(dbglu)
    grads.update(s5_lam_re=jnp.stack(lam_re_g), s5_lam_im=jnp.stack(lam_im_g), s5_log_dt=jnp.stack(ldt_g),
                 s5_b_re=jnp.stack(bre_g), s5_b_im=jnp.stack(bim_g), s5_c_re=jnp.stack(cre_g), s5_c_im=jnp.stack(cim_g),
                 s5_d=jnp.stack(dd_g), s5_b_glu=jnp.stack(bglu_g))

    def my_cols(full, width):
        return lax.dynamic_slice_in_dim(full, shard * width, width, axis=full.ndim - 1)

    cvs = {n: [] for n in ("ln_g", "ln_b", "b_dw", "b_pw2", "w_dw", "b_pw1")}
    for j in range(nconv):
        for n, val in zip(("ln_g", "ln_b", "b_dw", "b_pw2", "w_dw", "b_pw1"), red[k:k + 6]):
            cvs[n].append(val)
        k += 6
    grads.update(cv_ln_g=my_cols(jnp.stack(cvs["ln_g"]), Ds), cv_ln_b=my_cols(jnp.stack(cvs["ln_b"]), Ds),
                 cv_b_dw=my_cols(jnp.stack(cvs["b_dw"]), Ds), cv_b_pw2=my_cols(jnp.stack(cvs["b_pw2"]), Ds),
                 cv_w_dw=my_cols(jnp.stack(cvs["w_dw"]), Ds), cv_b_pw1=my_cols(jnp.stack(cvs["b_pw1"]), 2 * D // 4))
    grads.update(ln_gain=my_cols(red[k], Ds), ln_bias=my_cols(red[k + 1], Ds))
    dm_all = red[k + 2]

    dm_sh = lax.dynamic_slice_in_dim(dm_all, shard * Wa, Wa, axis=2)
    gw_ada, dcond = _ada_bwd(c_all, dm_sh, w_ada)
    grads["w_ada"] = gw_ada
    grads["b_ada"] = _colsum_groups("ada_bsum", dm_all)
    dc_part = dcond[0:1]
    dc_buf = jnp.concatenate([jnp.where(ac == 0, dc_part, 0.0), jnp.zeros((7, D), F32)], axis=0)
    dc_tot = _allreduce8("cctx", dc_buf.reshape(8 * D // LANES, LANES)).reshape(8, D)[0:1]
    grads["c_ctx"] = _ew("cctx_grad", lambda g, cv: g * (_sigmoid(cv) * (1.0 + cv * (1.0 - _sigmoid(cv)))),
                         [jnp.broadcast_to(dc_tot, (8, D)), jnp.broadcast_to(c_ctx[None], (8, D))], [_sds((8, D))])[0][0]

    s5_layers = [i for i in range(DEPTH) if kinds[i] == "s5"]
    cv_layers = [i for i in range(DEPTH) if kinds[i] == "conv"]
    grads.update(mlp_w1=jnp.stack([reduced[i][0][0] for i in range(DEPTH)]),
                 mlp_w2=jnp.stack([reduced[i][0][1] for i in range(DEPTH)]),
                 s5_w_glu=jnp.stack([reduced[i][1][0] for i in s5_layers]),
                 cv_w_pw1=jnp.stack([reduced[i][1][0] for i in cv_layers]),
                 cv_w_pw2=jnp.stack([reduced[i][2][0] for i in cv_layers]))

    weights = dict(c_ctx=c_ctx, w_ada=w_ada, b_ada=b_ada, ln_gain=ln_gain, ln_bias=ln_bias, s5_lam_re=s5_lam_re,
                   s5_lam_im=s5_lam_im, s5_log_dt=s5_log_dt, s5_b_re=s5_b_re, s5_b_im=s5_b_im, s5_c_re=s5_c_re,
                   s5_c_im=s5_c_im, s5_d=s5_d, s5_w_glu=s5_w_glu, s5_b_glu=s5_b_glu, cv_w_pw1=cv_w_pw1, cv_b_pw1=cv_b_pw1,
                   cv_w_dw=cv_w_dw, cv_b_dw=cv_b_dw, cv_ln_g=cv_ln_g, cv_ln_b=cv_ln_b, cv_w_pw2=cv_w_pw2, cv_b_pw2=cv_b_pw2,
                   mlp_w1=mlp_w1, mlp_w2=mlp_w2)
    ms = dict(c_ctx=m_c_ctx, w_ada=m_w_ada, b_ada=m_b_ada, ln_gain=m_ln_gain, ln_bias=m_ln_bias, s5_lam_re=m_s5_lam_re,
              s5_lam_im=m_s5_lam_im, s5_log_dt=m_s5_log_dt, s5_b_re=m_s5_b_re, s5_b_im=m_s5_b_im, s5_c_re=m_s5_c_re,
              s5_c_im=m_s5_c_im, s5_d=m_s5_d, s5_w_glu=m_s5_w_glu, s5_b_glu=m_s5_b_glu, cv_w_pw1=m_cv_w_pw1,
              cv_b_pw1=m_cv_b_pw1, cv_w_dw=m_cv_w_dw, cv_b_dw=m_cv_b_dw, cv_ln_g=m_cv_ln_g, cv_ln_b=m_cv_ln_b,
              cv_w_pw2=m_cv_w_pw2, cv_b_pw2=m_cv_b_pw2, mlp_w1=m_mlp_w1, mlp_w2=m_mlp_w2)
    vs = dict(c_ctx=v_c_ctx, w_ada=v_w_ada, b_ada=v_b_ada, ln_gain=v_ln_gain, ln_bias=v_ln_bias, s5_lam_re=v_s5_lam_re,
              s5_lam_im=v_s5_lam_im, s5_log_dt=v_s5_log_dt, s5_b_re=v_s5_b_re, s5_b_im=v_s5_b_im, s5_c_re=v_s5_c_re,
              s5_c_im=v_s5_c_im, s5_d=v_s5_d, s5_w_glu=v_s5_w_glu, s5_b_glu=v_s5_b_glu, cv_w_pw1=v_cv_w_pw1,
              cv_b_pw1=v_cv_b_pw1, cv_w_dw=v_cv_w_dw, cv_b_dw=v_cv_b_dw, cv_ln_g=v_cv_ln_g, cv_ln_b=v_cv_ln_b,
              cv_w_pw2=v_cv_w_pw2, cv_b_pw2=v_cv_b_pw2, mlp_w1=v_mlp_w1, mlp_w2=v_mlp_w2)
    names = list(weights)
    deltas, new_m, new_v = {}, {}, {}
    for n in names:
        g = grads[n].reshape(weights[n].shape)
        grads[n] = g
        deltas[n], new_m[n], new_v[n] = _adamw("adamw_" + n, weights[n], g, ms[n], vs[n])
    return (loss, grad_x, *[grads[n] for n in names], *[deltas[n] for n in names], *[new_m[n] for n in names],
            *[new_v[n] for n in names])


def _sublane_sum(name, a):
    n, _, w = a.shape

    def body(a_ref, o_ref):
        for q in range(n):
            o_ref[q:q + 1, :] = jnp.sum(a_ref[q], axis=0, keepdims=True)

    return pl.pallas_call(body, name=name, out_shape=_sds((n, w)))(a)


def _colsum_groups(name, dm_all):
    nl, nr, w = dm_all.shape

    def body(d_ref, o_ref):
        o_ref[...] = jnp.zeros((8, w), F32) + jnp.sum(d_ref[...], axis=0, keepdims=True)

    out = pl.pallas_call(body, name=name, grid=(nl,), in_specs=[pl.BlockSpec((None, nr, w), lambda l: (l, 0, 0))],
                         out_specs=pl.BlockSpec((None, 8, w), lambda l: (l, 0, 0)), out_shape=_sds((nl, 8, w)),
                         compiler_params=_cp(1))(dm_all)
    return out[:, 0]
```

```python
import functools
import math

import jax
import jax.numpy as jnp
from jax import lax
from jax.experimental import pallas as pl
from jax.experimental.pallas import tpu as pltpu

F32 = jnp.float32
BF16 = jnp.bfloat16
MESH = pl.DeviceIdType.MESH
ANY = pl.BlockSpec(memory_space=pl.ANY)

DEPTH = 4
S5_GROUP = 16
S5_STATE = 64
GRID_W = 64
POS_TEMP = 10000.0
LAMBDA_RE_MAX = -1e-4
LN_EPS = 1e-5
DN_ALPHA = (2.0 * DEPTH) ** 0.25
ADAM_LR, ADAM_B1, ADAM_B2, ADAM_EPS, ADAM_WD, ADAM_STEP = 0.001, 0.9, 0.999, 1e-08, 0.01, 10

SUBLANES = 8
LANES = 128
OCT_CH = 128
OCT_ST = 512
CONV_ROWS = 64
VMEM_LIMIT = 56 * 1024 * 1024


def _cp(n_axes):
    return pltpu.CompilerParams(dimension_semantics=("arbitrary",) * n_axes, vmem_limit_bytes=VMEM_LIMIT)


def _full(shape, single=False):
    nd = len(shape)
    if single:
        return pl.BlockSpec(shape, lambda *i: (0,) * nd, pipeline_mode=pl.Buffered(1))
    return pl.BlockSpec(shape, lambda *i: (0,) * nd)


def _sds(shape, dtype=F32):
    return jax.ShapeDtypeStruct(tuple(shape), dtype)


def _mod(x, sh8, sc8):
    r, d = x.shape
    return (x.reshape(r // 8, 8, d) * (1.0 + sc8[None]) + sh8[None]).reshape(r, d)


def _rowscale(x, g8):
    r, d = x.shape
    return (x.reshape(r // 8, 8, d) * g8[None]).reshape(r, d)


def _sum8(x):
    r, w = x.shape
    return jnp.sum(x.reshape(r // 8, 8, w), axis=0)


def _ln_stats(r):
    mu = jnp.mean(r, axis=-1, keepdims=True)
    xc = r - mu
    var = jnp.mean(xc * xc, axis=-1, keepdims=True)
    rstd = lax.rsqrt(var + LN_EPS)
    return xc * rstd, rstd


def _ln_bwd(dxh, xhat, rstd):
    m1 = jnp.mean(dxh, axis=-1, keepdims=True)
    m2 = jnp.mean(dxh * xhat, axis=-1, keepdims=True)
    return rstd * (dxh - m1 - xhat * m2)


def _sigmoid(x):
    return 1.0 / (1.0 + jnp.exp(-x))


def _gelu(y):
    return 0.5 * y * (1.0 + lax.erf(y * (1.0 / math.sqrt(2.0))))


def _gelu_grad(y):
    return 0.5 * (1.0 + lax.erf(y * (1.0 / math.sqrt(2.0)))) + y * jnp.exp(-0.5 * y * y) * (1.0 / math.sqrt(2.0 * math.pi))


def _dot(a, b):
    return jnp.dot(a, b, preferred_element_type=F32)


def _dot_nt(a, b):
    return lax.dot_general(a, b, (((1,), (1,)), ((), ())), preferred_element_type=F32)


def _dot_tn(a, b):
    return lax.dot_general(a, b, (((0,), (0,)), ((), ())), preferred_element_type=F32)


class _Cfg:
    def __init__(self, x, ctx, mlp_w1, cv_w_dw):
        self.B, self.L, self.D = x.shape
        self.Lc = ctx.shape[1]
        assert self.B * 4 == SUBLANES, "two examples per device, four chunks each"
        self.F = mlp_w1.shape[2] * 4
        self.KW = cv_w_dw.shape[1]
        self.half = self.KW // 2
        self.G = self.D // S5_GROUP
        self.P = S5_STATE
        self.NS = self.G * self.P
        self.NO = self.D // OCT_CH
        assert self.NO % 2 == 0
        self.nx = self.L // 4
        self.nc = self.Lc // 4
        self.Tc = self.B * self.Lc
        self.Tx = self.B * self.L
        self.T = self.Tc + self.Tx
        self.TM = 256 if self.Tc % 256 == 0 else self.Tc
        assert self.Tx % self.TM == 0 and self.TM % 16 == 0
        self.HB = self.TM // 2
        assert SUBLANES * self.half <= self.HB
        self.TW = 512 if (self.Tc % 512 == 0 and self.Tx % 512 == 0) else self.TM

    def ti(self, n):
        t = 32 if self.nc % 32 == 0 else self.nc
        assert n % t == 0 and self.Tc % (8 * t) == 0
        return t

    def rows(self, ctx_too):
        return (0, self.T) if ctx_too else (self.Tc, self.Tx)


def _allgather8(name, x_shard):
    m_per, n = x_shard.shape
    assert m_per % 8 == 0

    def body(x_ref, out_ref, send_sems, recv_sems, local_sem):
        x, y, c = lax.axis_index("x"), lax.axis_index("y"), lax.axis_index("c")
        me, sibling = (x, y, c), (x, y, 1 - c)
        chips = [(1 - x, y), (x, 1 - y), (1 - x, 1 - y)]

        def rows(px, py, pc):
            return out_ref.at[pl.ds((4 * px + 2 * py + pc) * m_per, m_per), :]

        def copy(k, block, to, src=None):
            return pltpu.make_async_remote_copy(
                src_ref=rows(*block) if src is None else src, dst_ref=rows(*block),
                send_sem=send_sems.at[k], recv_sem=recv_sems.at[k], device_id=to, device_id_type=MESH)

        mine = pltpu.make_async_copy(x_ref, rows(*me), local_sem)
        mine.start()
        first = [copy(0, me, sibling, src=x_ref)]
        first += [copy(1 + j, me, (*chip, c), src=x_ref) for j, chip in enumerate(chips)]
        for cp in first:
            cp.start()
        passed = [copy(4 + j, (*chip, c), sibling) for j, chip in enumerate(chips)]
        for j, chip in enumerate(chips):
            copy(1 + j, (*chip, c), me).wait_recv()
            passed[j].start()
        copy(0, sibling, me).wait_recv()
        for j, chip in enumerate(chips):
            copy(4 + j, (*chip, 1 - c), me).wait_recv()
        for cp in first + passed:
            cp.wait_send()
        mine.wait()

    return pl.pallas_call(
        body, name=name, out_shape=_sds((8 * m_per, n), x_shard.dtype),
        in_specs=[pl.BlockSpec(memory_space=pltpu.VMEM)], out_specs=pl.BlockSpec(memory_space=pltpu.VMEM),
        scratch_shapes=[pltpu.SemaphoreType.DMA((7,)), pltpu.SemaphoreType.DMA((7,)), pltpu.SemaphoreType.DMA],
        compiler_params=pltpu.CompilerParams(vmem_limit_bytes=VMEM_LIMIT),
    )(x_shard)


def _flip(v, m):
    return v + m - 2 * v * m


def _peer(axis):
    x, y, c = lax.axis_index("x"), lax.axis_index("y"), lax.axis_index("c")
    if axis == "c":
        return (x, y, 1 - c)
    if axis == "xy":
        return (_flip(x, 1 - c), _flip(y, c), c)
    if axis == "yx":
        return (_flip(x, c), _flip(y, 1 - c), c)
    raise ValueError(axis)


def _pair_exchange(name, axis, inputs, out_shapes, aliases, plan):
    n_in = len(inputs)
    n_out = len(out_shapes)

    def body(*refs):
        ins, outs = refs[:n_in], refs[n_in:n_in + n_out]
        send_sems, recv_sems, local_sems = refs[n_in + n_out:]
        x, y, c = lax.axis_index("x"), lax.axis_index("y"), lax.axis_index("c")
        remote, local = plan(x, y, c, ins, outs)
        lcs = [pltpu.make_async_copy(s, d, local_sems.at[k]) for k, (s, d) in enumerate(local)]
        for cp in lcs:
            cp.start()
        rcs = [pltpu.make_async_remote_copy(src_ref=s, dst_ref=d, send_sem=send_sems.at[k], recv_sem=recv_sems.at[k],
                                            device_id=_peer(axis), device_id_type=MESH) for k, (s, d) in enumerate(remote)]
        for cp in rcs:
            cp.start()
        for cp in rcs:
            cp.wait()
        for cp in lcs:
            cp.wait()

    n_remote, n_local = plan.counts
    return pl.pallas_call(
        body, name=name, out_shape=tuple(out_shapes),
        in_specs=[ANY] * n_in, out_specs=tuple([ANY] * n_out),
        input_output_aliases=dict(aliases),
        scratch_shapes=[pltpu.SemaphoreType.DMA((n_remote,)), pltpu.SemaphoreType.DMA((n_remote,)),
                        pltpu.SemaphoreType.DMA((max(n_local, 1),))],
    )(*inputs)


def _plan(n_remote, n_local=0):
    def deco(fn):
        fn.counts = (n_remote, n_local)
        return fn
    return deco


HBM = pl.BlockSpec(memory_space=pltpu.HBM)
SEM = pl.BlockSpec(memory_space=pltpu.SEMAPHORE)


def _split_start(name, axis, bufs, plan):
    nb = len(bufs)
    n = plan.counts[0]

    def body(*refs):
        ins, send_sem, recv_sem, token = refs[:nb], refs[nb], refs[nb + 1], refs[-1]
        x, y, c = lax.axis_index("x"), lax.axis_index("y"), lax.axis_index("c")
        for k, (s, d) in enumerate(plan(x, y, c, ins)):
            pltpu.make_async_remote_copy(src_ref=s, dst_ref=d, send_sem=send_sem.at[k], recv_sem=recv_sem.at[k],
                                         device_id=_peer(axis), device_id_type=MESH).start()
        token[...] = jnp.zeros_like(token)

    outs = pl.pallas_call(
        body, name=name,
        out_shape=(pltpu.SemaphoreType.DMA((n,)), pltpu.SemaphoreType.DMA((n,)),
                   *[pltpu.HBM(b.shape, b.dtype) for b in bufs], _sds((8, LANES))),
        in_specs=[HBM] * nb, out_specs=(SEM, SEM, *([HBM] * nb), pl.BlockSpec(memory_space=pltpu.VMEM)),
        input_output_aliases={i: 2 + i for i in range(nb)},
        compiler_params=pltpu.CompilerParams(has_side_effects=pltpu.SideEffectType.DATAFLOW_SIDE_EFFECTING),
    )(*[pltpu.with_memory_space_constraint(b, pltpu.HBM) for b in bufs])
    return dict(name=name, axis=axis, plan=plan, send=outs[0], recv=outs[1], bufs=list(outs[2:2 + nb]), token=outs[-1])


def _split_wait(h, after):
    bufs, plan, axis = h["bufs"], h["plan"], h["axis"]
    nb = len(bufs)

    def body(*refs):
        ins, send_sem, recv_sem = refs[:nb], refs[nb], refs[nb + 1]
        x, y, c = lax.axis_index("x"), lax.axis_index("y"), lax.axis_index("c")
        for k, (s, d) in enumerate(plan(x, y, c, ins)):
            cp = pltpu.make_async_remote_copy(src_ref=s, dst_ref=d, send_sem=send_sem.at[k], recv_sem=recv_sem.at[k],
                                              device_id=_peer(axis), device_id_type=MESH)
            cp.wait_send()
            cp.wait_recv()

    outs = pl.pallas_call(
        body, name=h["name"] + "_wait", out_shape=tuple(pltpu.HBM(b.shape, b.dtype) for b in bufs),
        in_specs=[HBM] * nb + [SEM, SEM, ANY], out_specs=tuple([HBM] * nb),
        input_output_aliases={i: i for i in range(nb)},
        compiler_params=pltpu.CompilerParams(has_side_effects=pltpu.SideEffectType.DATAFLOW_SIDE_EFFECTING),
    )(*bufs, h["send"], h["recv"], after)
    return list(outs)


def _tie(small, tokens):
    for t in tokens:
        small = small + t[0, 0]
    return small


class _Overlap:
    def __init__(self):
        self.live = {}
        self.done = {}

    def add(self, key, gen):
        self.live[key] = gen
        return [next(gen)]

    def point(self, arr):
        tokens = []
        for key in list(self.live):
            try:
                tokens.append(self.live[key].send(arr))
            except StopIteration as e:
                self.done[key] = e.value
                del self.live[key]
        return tokens

    def finish(self, key, arr):
        while key in self.live:
            try:
                self.live[key].send(arr)
            except StopIteration as e:
                self.done[key] = e.value
                del self.live[key]
        return self.done.pop(key)


def _gather_gen(tag, fams):
    nf = len(fams)
    shapes = [f.shape for f in fams]
    views = [f.reshape(4, 2, -1, f.shape[-1]) for f in fams]

    @_plan(nf)
    def plan1(x, y, c, refs):
        s = 2 * x + y
        return [(refs[k].at[s, c], refs[k].at[s, c]) for k in range(nf)]

    @_plan(2 * nf)
    def plan2(x, y, c, refs):
        shards = [2 * x + y, 2 * _flip(x, 1 - c) + _flip(y, c)]
        return [(refs[k].at[s, c], refs[k].at[s, c]) for k in range(nf) for s in shards]

    @_plan(3 * nf)
    def plan3(x, y, c, refs):
        shards = [2 * (1 - x) + y, 2 * x + (1 - y), 2 * (1 - x) + (1 - y)]
        return [(refs[k].at[s, c], refs[k].at[s, c]) for k in range(nf) for s in shards]

    for rnd, (axis, plan) in enumerate((("xy", plan1), ("yx", plan2), ("c", plan3))):
        h = _split_start(f"{tag}_g{rnd}", axis, views, plan)
        after = yield h["token"]
        views = _split_wait(h, after)
    return [v.reshape(sh) for v, sh in zip(views, shapes)]


def _reduce_scatter_gen(tag, grads):
    ng = len(grads)
    flat = [g.reshape(4, 2, -1, g.shape[-1]) for g in grads]

    def empty(shape, dtype):
        return lax.empty(tuple(shape), dtype)

    @_plan(ng)
    def plan1(x, y, c, refs):
        return [(refs[k].at[:, 1 - c], refs[ng + k]) for k in range(ng)]

    h = _split_start(tag + "_r0", "c", flat + [empty((4,) + f.shape[2:], F32) for f in flat], plan1)
    after = yield h["token"]
    bufs = _split_wait(h, after)
    p1 = [_sel_add(f"{tag}_add1_{k}", bufs[k], lambda j, sc: (j, sc[2]), bufs[ng + k], True) for k in range(ng)]

    def sent1(kk, x, y, c):
        return ((1 - c) * kk + c * (1 - x), (1 - c) * (1 - y) + c * kk)

    def kept1(j, sc):
        x, y, c = sc[0], sc[1], sc[2]
        return ((1 - c) * j + c * x, (1 - c) * y + c * j)

    @_plan(2 * ng)
    def plan2(x, y, c, refs):
        return [(refs[k].at[sent1(kk, x, y, c)], refs[ng + k].at[kk]) for k in range(ng) for kk in range(2)]

    v1 = [pb.reshape(2, 2, pb.shape[1], pb.shape[2]) for p, pb in p1]
    h = _split_start(tag + "_r1", "yx", v1 + [empty((2,) + v.shape[2:], BF16) for v in v1], plan2)
    after = yield h["token"]
    bufs = _split_wait(h, after)
    p2 = [_sel_add(f"{tag}_add2_{k}", p1[k][0].reshape(2, 2, p1[k][0].shape[1], p1[k][0].shape[2]), kept1, bufs[ng + k], True)
          for k in range(ng)]

    @_plan(ng)
    def plan3(x, y, c, refs):
        return [(refs[k].at[(1 - c) * (1 - x) + c * (1 - y)], refs[ng + k]) for k in range(ng)]

    h = _split_start(tag + "_r2", "xy", [qb for q, qb in p2] + [empty(qb.shape[1:], BF16) for q, qb in p2], plan3)
    after = yield h["token"]
    bufs = _split_wait(h, after)
    fin = [_sel_add(f"{tag}_add3_{k}", p2[k][0], lambda j, sc: ((1 - sc[2]) * sc[0] + sc[2] * sc[1],), bufs[ng + k][None],
                    False, out_slots=(2, lambda j, sc: sc[2]))[0] for k in range(ng)]

    @_plan(ng)
    def plan4(x, y, c, refs):
        return [(refs[k].at[c], refs[k].at[c]) for k in range(ng)]

    h = _split_start(tag + "_r3", "c", fin, plan4)
    after = yield h["token"]
    full = _split_wait(h, after)
    return [full[k].reshape(grads[k].shape[1:]) for k in range(ng)]


def _xyc():
    return jnp.stack([lax.axis_index("x"), lax.axis_index("y"), lax.axis_index("c")]).astype(jnp.int32)


def _place_shard(name, w, fam, slot0, n_slots):
    n, kk, nn = w.shape
    kt = 256 if kk % 256 == 0 else kk

    def body(scal, w_ref, *rest):
        rest[-1][...] = w_ref[...].astype(BF16)

    in_specs = [pl.BlockSpec((None, kt, nn), lambda t, i, sc: (t, i, 0))]
    args = [_xyc(), w]
    aliases = {}
    if fam is not None:
        in_specs.append(ANY)
        args.append(fam)
        aliases = {2: 0}
    gs = pltpu.PrefetchScalarGridSpec(
        num_scalar_prefetch=1, grid=(n, kk // kt), in_specs=in_specs,
        out_specs=pl.BlockSpec((None, None, kt, nn), lambda t, i, sc: (2 * sc[0] + sc[1], slot0 + t, i, 0)))
    return pl.pallas_call(body, name=name, grid_spec=gs, out_shape=_sds((4, n_slots, kk, nn), BF16),
                          input_output_aliases=aliases, compiler_params=_cp(2))(*args)


def _gather_weights(fams):
    nf = len(fams)
    shapes = [f.shape for f in fams]
    views = [f.reshape(4, 2, -1, f.shape[-1]) for f in fams]
    outs = [_sds(v.shape, v.dtype) for v in views]
    alias = {k: k for k in range(nf)}

    @_plan(nf)
    def plan1(x, y, c, ins, outs_):
        s = 2 * x + y
        return ([(ins[k].at[s, c], outs_[k].at[s, c]) for k in range(nf)], [])

    views = _pair_exchange("gatherw_1", "xy", list(views), outs, alias, plan1)

    @_plan(2 * nf)
    def plan2(x, y, c, ins, outs_):
        shards = [2 * x + y, 2 * _flip(x, 1 - c) + _flip(y, c)]
        return ([(ins[k].at[s, c], outs_[k].at[s, c]) for k in range(nf) for s in shards], [])

    views = _pair_exchange("gatherw_2", "yx", list(views), outs, alias, plan2)

    @_plan(3 * nf)
    def plan3(x, y, c, ins, outs_):
        shards = [2 * (1 - x) + y, 2 * x + (1 - y), 2 * (1 - x) + (1 - y)]
        return ([(ins[k].at[s, c], outs_[k].at[s, c]) for k in range(nf) for s in shards], [])

    views = _pair_exchange("gatherw_c", "c", list(views), outs, alias, plan3)
    return [v.reshape(sh) for v, sh in zip(views, shapes)]


def _sel_add(name, a, a_sel, r, emit_bf16, out_slots=None):
    nr, rows, w = r.shape
    tr = 256 if rows % 256 == 0 else rows

    def body(scal, a_ref, r_ref, *outs):
        s = a_ref[...] + r_ref[...].astype(F32)
        outs[0][...] = s
        if emit_bf16:
            outs[1][...] = s.astype(BF16)

    lead = a.ndim - 2
    a_block = (None,) * lead + (tr, w)
    n_out, o_fn = out_slots if out_slots is not None else (nr, lambda j, sc: j)
    out_shape = [_sds((n_out, rows, w), F32)] + ([_sds((nr, rows, w), BF16)] if emit_bf16 else [])
    out_specs = [pl.BlockSpec((None, tr, w), lambda j, t, sc: (o_fn(j, sc), t, 0))]
    if emit_bf16:
        out_specs.append(pl.BlockSpec((None, tr, w), lambda j, t, sc: (j, t, 0)))
    gs = pltpu.PrefetchScalarGridSpec(
        num_scalar_prefetch=1, grid=(nr, rows // tr),
        in_specs=[pl.BlockSpec(a_block, lambda j, t, sc: tuple(a_sel(j, sc)) + (t, 0)),
                  pl.BlockSpec((None, tr, w), lambda j, t, sc: (j, t, 0))],
        out_specs=out_specs)
    return pl.pallas_call(body, name=name, grid_spec=gs, out_shape=out_shape, compiler_params=_cp(2))(_xyc(), a, r)


def _reduce_scatter(tag, grads):
    ng = len(grads)
    flat = [g.reshape(4, 2, -1, g.shape[-1]) for g in grads]

    @_plan(ng)
    def plan1(x, y, c, ins, outs_):
        return ([(ins[k].at[:, 1 - c], outs_[k]) for k in range(ng)], [])

    r1 = _pair_exchange(tag + "_rs_c", "c", flat, [_sds((4,) + f.shape[2:], F32) for f in flat], {}, plan1)
    p1 = [_sel_add(f"{tag}_add1_{k}", flat[k], lambda j, sc: (j, sc[2]), r1[k], True) for k in range(ng)]

    def sent1(kk, x, y, c):
        return ((1 - c) * kk + c * (1 - x), (1 - c) * (1 - y) + c * kk)

    def kept1(j, sc):
        x, y, c = sc[0], sc[1], sc[2]
        return ((1 - c) * j + c * x, (1 - c) * y + c * j)

    @_plan(2 * ng)
    def plan2(x, y, c, ins, outs_):
        return ([(ins[k].at[sent1(kk, x, y, c)], outs_[k].at[kk]) for k in range(ng) for kk in range(2)], [])

    v1 = [pb.reshape(2, 2, pb.shape[1], pb.shape[2]) for p, pb in p1]
    r2 = _pair_exchange(tag + "_rs_1", "yx", v1, [_sds((2,) + v.shape[2:], BF16) for v in v1], {}, plan2)
    p2 = [_sel_add(f"{tag}_add2_{k}", p1[k][0].reshape(2, 2, p1[k][0].shape[1], p1[k][0].shape[2]), kept1, r2[k], True)
          for k in range(ng)]

    @_plan(ng)
    def plan3(x, y, c, ins, outs_):
        return ([(ins[k].at[(1 - c) * (1 - x) + c * (1 - y)], outs_[k]) for k in range(ng)], [])

    r3 = _pair_exchange(tag + "_rs_2", "xy", [qb for q, qb in p2], [_sds(qb.shape[1:], BF16) for q, qb in p2], {}, plan3)
    fin = [_sel_add(f"{tag}_add3_{k}", p2[k][0], lambda j, sc: ((1 - sc[2]) * sc[0] + sc[2] * sc[1],), r3[k][None], False,
                    out_slots=(2, lambda j, sc: sc[2]))[0] for k in range(ng)]

    @_plan(ng)
    def plan4(x, y, c, ins, outs_):
        return ([(ins[k].at[c], outs_[k].at[c]) for k in range(ng)], [])

    full = _pair_exchange(tag + "_rs_c2", "c", fin, [_sds(f.shape, F32) for f in fin], {k: k for k in range(ng)}, plan4)
    return [full[k].reshape(grads[k].shape[1:]) for k in range(ng)]


def _allreduce8(tag, buf, point=None):
    rows, w = buf.shape
    assert rows % 16 == 0
    step = (lambda a: _tie(a, point(a))) if point is not None else (lambda a: a)
    one = lambda: _plan(1)(lambda x, y, c, ins, outs_: ([(ins[0], outs_[0])], []))
    (got,) = _pair_exchange(f"{tag}_ar_c", "c", [buf], [_sds(buf.shape, F32)], {}, one())
    cur = _ew(f"{tag}_aradd_c", lambda a, b: a + b, [buf, got], [_sds(buf.shape, F32)])[0]
    cur = step(cur).reshape(2, rows // 2, w)
    mine = _plan(1)(lambda x, y, c, ins, outs_: ([(ins[0].at[c], outs_[0])], []))
    (got,) = _pair_exchange(f"{tag}_ar_1", "xy", [cur], [_sds(cur.shape[1:], F32)], {}, mine)
    (h1,) = _sel_add(f"{tag}_aradd_1", cur, lambda j, sc: (sc[2],), got[None], False)
    h1 = step(h1)
    (got,) = _pair_exchange(f"{tag}_ar_2", "yx", [h1[0]], [_sds(h1.shape[1:], F32)], {}, one())
    (h2,) = _sel_add(f"{tag}_aradd_2", h1, lambda j, sc: (0,), got[None], False, out_slots=(2, lambda j, sc: sc[2]))
    swap = _plan(1)(lambda x, y, c, ins, outs_: ([(ins[0].at[c], outs_[0].at[c])], []))
    (full,) = _pair_exchange(f"{tag}_ar_c2", "c", [h2], [_sds(h2.shape, F32)], {0: 0}, swap)
    return full.reshape(rows, w)


def _ew(name, fn, ins, outs):
    rows, w = ins[0].shape
    tr = rows
    for cand in (512, 256, 128, 64, 32, 16, 8):
        if rows % cand == 0 and rows > cand and cand * w * 4 <= (1 << 20):
            tr = cand
            break
    n_in = len(ins)

    def body(*refs):
        vals = fn(*[r[...] for r in refs[:n_in]])
        if not isinstance(vals, (tuple, list)):
            vals = (vals,)
        for o, v in zip(refs[n_in:], vals):
            o[...] = v.astype(o.dtype)

    spec = pl.BlockSpec((tr, w), lambda i: (i, 0))
    return pl.pallas_call(body, name=name, grid=(rows // tr,), in_specs=[spec] * n_in,
                          out_specs=[spec] * len(outs), out_shape=list(outs), compiler_params=_cp(1))(*ins)


def _ew3(name, fn, ins, n_out):
    aa, bb, cc = ins[0].shape
    pad_bytes = (-(-bb // SUBLANES) * SUBLANES) * (-(-cc // LANES) * LANES) * 4
    ta = 1
    for cand in range(aa, 0, -1):
        if aa % cand == 0 and cand * pad_bytes <= (1 << 20):
            ta = cand
            break
    n_in = len(ins)

    def body(*refs):
        vals = fn(*[r[...] for r in refs[:n_in]])
        for o, v in zip(refs[n_in:], vals):
            o[...] = v

    spec = pl.BlockSpec((ta, bb, cc), lambda i: (i, 0, 0))
    return pl.pallas_call(body, name=name, grid=(aa // ta,), in_specs=[spec] * n_in, out_specs=[spec] * n_out,
                          out_shape=[_sds((aa, bb, cc))] * n_out, compiler_params=_cp(1))(*ins)


def _view_for_ew(a):
    if a.ndim == 1:
        return a.reshape(1, -1)
    if a.ndim == 2:
        return a
    if a.shape[-1] % LANES == 0 and a.shape[-2] % SUBLANES == 0:
        return a.reshape(-1, a.shape[-1])
    return a.reshape(-1, a.shape[-2], a.shape[-1])


def _adamw(name, w, g, m, v):
    def fn(w, g, m, v):
        m = ADAM_B1 * m + (1.0 - ADAM_B1) * g
        v = ADAM_B2 * v + (1.0 - ADAM_B2) * (g * g)
        m_hat = m / (1.0 - ADAM_B1 ** ADAM_STEP)
        v_hat = v / (1.0 - ADAM_B2 ** ADAM_STEP)
        delta = -ADAM_LR * (m_hat / (jnp.sqrt(v_hat) + ADAM_EPS) + ADAM_WD * w)
        return delta, m, v

    shp = w.shape
    a = [_view_for_ew(t) for t in (w, g, m, v)]
    if a[0].ndim == 3:
        o = _ew3(name, fn, a, 3)
    else:
        o = _ew(name, fn, a, [_sds(a[0].shape)] * 3)
    return tuple(t.reshape(shp) for t in o)


def _to_perm(a):
    b, ls, d = a.shape
    n = ls // 4
    return a.reshape(b * 4, n, d).swapaxes(0, 1).reshape(n * 8, d)


def _from_perm(p, b, ls):
    n = ls // 4
    return p.reshape(n, b * 4, p.shape[-1]).swapaxes(0, 1).reshape(b, ls, p.shape[-1])


def _pos_embed(rows, dim):
    def sincos(pos, d):
        quarter = d // 2
        omega = POS_TEMP ** (-jnp.arange(quarter, dtype=F32) / quarter)
        ang = pos[:, None] * omega[None, :]
        return jnp.concatenate([jnp.sin(ang), jnp.cos(ang)], axis=-1)

    row_idx = jnp.repeat(jnp.arange(rows), GRID_W).astype(F32)
    col_idx = jnp.tile(jnp.arange(GRID_W), rows).astype(F32)
    return jnp.concatenate([sincos(row_idx, dim // 2), sincos(col_idx, dim // 2)], axis=-1)


def _stream_of(cfg, off_tiles, tile_rows):
    nct = cfg.Tc // tile_rows
    return lambda i: jnp.where(i + off_tiles >= nct, 1, 0)


def _ada_fwd(c_all, w_ada, b_shard):
    nl, d, w = w_ada.shape
    tn = 512 if w % 512 == 0 else w

    def body(c_ref, w_ref, b_ref, o_ref):
        cv = c_ref[...]
        cond = (cv * _sigmoid(cv)).astype(BF16)
        o_ref[...] = _dot(cond, w_ref[...].astype(BF16)) + b_ref[...]

    return pl.pallas_call(
        body, name="ada_fwd", grid=(nl, w // tn),
        in_specs=[_full(c_all.shape), pl.BlockSpec((None, d, tn), lambda l, j: (l, 0, j)),
                  pl.BlockSpec((None, 1, tn), lambda l, j: (l, 0, j))],
        out_specs=pl.BlockSpec((None, c_all.shape[0], tn), lambda l, j: (l, 0, j)),
        out_shape=_sds((nl, c_all.shape[0], w)), compiler_params=_cp(2))(c_all, w_ada, b_shard)


def _ada_bwd(c_all, dmod_shard, w_ada):
    nl, d, w = w_ada.shape
    tn = 512 if w % 512 == 0 else w
    nr = c_all.shape[0]

    def body(c_ref, dm_ref, w_ref, gw_ref, dc_ref):
        j = pl.program_id(0) * (w // tn) + pl.program_id(1)
        cv = c_ref[...]
        cond = (cv * _sigmoid(cv)).astype(BF16)
        dm = dm_ref[...].astype(BF16)
        gw_ref[...] = _dot_tn(cond, dm)
        part = _dot_nt(dm[16:24], w_ref[...].astype(BF16))

        @pl.when(j == 0)
        def _():
            dc_ref[...] = part

        @pl.when(j > 0)
        def _():
            dc_ref[...] += part

    return pl.pallas_call(
        body, name="ada_bwd", grid=(nl, w // tn),
        in_specs=[_full(c_all.shape), pl.BlockSpec((None, nr, tn), lambda l, j: (l, 0, j)),
                  pl.BlockSpec((None, d, tn), lambda l, j: (l, 0, j))],
        out_specs=[pl.BlockSpec((None, d, tn), lambda l, j: (l, 0, j)), _full((8, d))],
        out_shape=[_sds((nl, d, w)), _sds((8, d))], compiler_params=_cp(2))(c_all, dmod_shard, w_ada)


def _disc(lr, li, ldt, br, bi):
    lr = jnp.minimum(lr, LAMBDA_RE_MAX)
    dt = jnp.exp(ldt)
    mag = jnp.exp(lr * dt)
    abr = mag * jnp.cos(li * dt)
    abi = mag * jnp.sin(li * dt)
    den = lr * lr + li * li
    nr = abr - 1.0
    ni = abi
    cr = (nr * lr + ni * li) / den
    ci = (ni * lr - nr * li) / den
    return abr, abi, cr[None] * br - ci[None] * bi, cr[None] * bi + ci[None] * br


def _disc_fwd(lr, li, ldt, br, bi):
    def body(a, b, c, d, e, o1, o2, o3, o4):
        r = _disc(a[...], b[...], c[...], d[...], e[...])
        o1[...], o2[...], o3[...], o4[...] = r

    return pl.pallas_call(body, name="s5_disc_fwd", out_shape=[_sds(lr.shape), _sds(lr.shape), _sds(br.shape), _sds(br.shape)])(
        lr, li, ldt, br, bi)


def _disc_bwd(lr, li, ldt, br, bi, g_abr, g_abi, g_bbr, g_bbi):
    def body(a, b, c, d, e, g1, g2, g3, g4, o1, o2, o3, o4, o5):
        _, vjp = jax.vjp(_disc, a[...], b[...], c[...], d[...], e[...])
        r = vjp((g1[...], g2[...], g3[...], g4[...]))
        o1[...], o2[...], o3[...], o4[...], o5[...] = r

    return pl.pallas_call(
        body, name="s5_disc_bwd",
        out_shape=[_sds(lr.shape), _sds(li.shape), _sds(ldt.shape), _sds(br.shape), _sds(bi.shape)])(
        lr, li, ldt, br, bi, g_abr, g_abi, g_bbr, g_bbi)


def _s5_layouts(cfg, lam_re, lam_im, log_dt, b_re, b_im):
    P, G = cfg.P, cfg.G
    nd = lam_re.shape[0]
    lr = lam_re.transpose(2, 0, 1).reshape(P, nd * G)
    li = lam_im.transpose(2, 0, 1).reshape(P, nd * G)
    ldt = log_dt.reshape(1, nd * G)
    br = b_re.transpose(3, 2, 0, 1).reshape(S5_GROUP, P, nd * G)
    bi = b_im.transpose(3, 2, 0, 1).reshape(S5_GROUP, P, nd * G)
    return lr, li, ldt, br, bi


def _coef_rows(cfg, abr, abi, conj):
    nd = abr.shape[1] // cfg.G

    def one(t):
        return t.reshape(cfg.P, nd, cfg.G).transpose(1, 2, 0).reshape(nd, cfg.NS)
    a = jnp.stack([one(abr), -one(abi) if conj else one(abi)], axis=1)
    return jnp.broadcast_to(a[:, :, None, :], (nd, 2, SUBLANES, cfg.NS))


def _blockdiag_b(cfg, bbr, bbi):
    eye = jnp.eye(8, dtype=F32)
    nd = bbr.shape[2] // cfg.G

    def one(t):
        t = t.reshape(S5_GROUP, cfg.P, nd, cfg.G).transpose(2, 3, 0, 1)
        t = t.reshape(nd, cfg.NO, 8, S5_GROUP, cfg.P)
        return jnp.einsum("dogcp,gh->dogchp", t, eye).reshape(nd, cfg.NO, OCT_CH, OCT_ST)

    return jnp.concatenate([one(bbr), one(bbi)], axis=-1).astype(BF16)


def _blockdiag_c(cfg, c_re, c_im):
    eye = jnp.eye(8, dtype=F32)
    nd = c_re.shape[0]

    def one(t):
        t = t.transpose(0, 1, 3, 2).reshape(nd, cfg.NO, 8, cfg.P, S5_GROUP)
        return jnp.einsum("dogpc,gh->dogphc", t, eye).reshape(nd, cfg.NO, OCT_ST, OCT_CH)

    return jnp.concatenate([one(c_re), -one(c_im)], axis=2).astype(BF16)


def _diag_b(cfg, dbf):
    eye = jnp.eye(8, dtype=F32)
    nd = dbf.shape[0]

    def one(t):
        t = t.reshape(nd, cfg.NO, 8, S5_GROUP, 8, cfg.P)
        t = jnp.einsum("dogchp,gh->dogcp", t, eye).reshape(nd, cfg.G, S5_GROUP, cfg.P)
        return t.transpose(2, 3, 0, 1).reshape(S5_GROUP, cfg.P, nd * cfg.G)

    return one(dbf[..., :OCT_ST]), one(dbf[..., OCT_ST:])


def _diag_c(cfg, dcft):
    eye = jnp.eye(8, dtype=F32)
    nd = dcft.shape[0]

    def one(t):
        t = t.reshape(nd, cfg.NO, 8, S5_GROUP, 8, cfg.P)
        return jnp.einsum("dohcgp,gh->dogcp", t, eye).reshape(nd, cfg.G, S5_GROUP, cfg.P)

    return one(dcft[..., :OCT_ST]), -one(dcft[..., OCT_ST:])


def _recur(buf, st, a_ref, n_oct, ti, rev, store):
    for o in range(0, n_oct, 2):
        cols = [(pl.ds(oo * 2 * OCT_ST, OCT_ST), pl.ds(oo * 2 * OCT_ST + OCT_ST, OCT_ST)) for oo in (o, o + 1)]
        scol = [pl.ds(oo * OCT_ST, OCT_ST) for oo in (o, o + 1)]
        coef = [(a_ref[0, :, sc], a_ref[1, :, sc]) for sc in scol]
        init = (st[0, :, scol[0]], st[1, :, scol[0]], st[0, :, scol[1]], st[1, :, scol[1]])

        def step(i4, carry, cols=cols, coef=coef):
            carry = list(carry)
            for q in range(unroll):
                i = i4 * unroll + q
                r = pl.multiple_of((i + rev * (ti - 1 - 2 * i)) * 8, 8)
                for s in range(2):
                    sr, si = carry[2 * s], carry[2 * s + 1]
                    ar, ai = coef[s]
                    zr = buf[pl.ds(r, 8), cols[s][0]]
                    zi = buf[pl.ds(r, 8), cols[s][1]]
                    nr = ar * sr - ai * si + zr
                    ni = ar * si + ai * sr + zi
                    if store:
                        buf[pl.ds(r, 8), cols[s][0]] = nr
                        buf[pl.ds(r, 8), cols[s][1]] = ni
                    carry[2 * s], carry[2 * s + 1] = nr, ni
            return tuple(carry)

        unroll = 4 if ti % 4 == 0 else 1
        fin = lax.fori_loop(0, ti // unroll, step, init)
        st[0, :, scol[0]] = fin[0]
        st[1, :, scol[0]] = fin[1]
        st[0, :, scol[1]] = fin[2]
        st[1, :, scol[1]] = fin[3]


def _s5_fwd_pass(cfg, name, tok, mod8, col_sh, col_sc, bf, acoef, r0, n, s_init=None, cf=None, y_prev=None):
    D, NO, NS = cfg.D, cfg.NO, cfg.NS
    ti = cfg.ti(n)
    nb = n // ti
    R = 8 * ti
    ob = r0 // R
    second = s_init is not None
    blk = lambda d, j: ob + j + d * (nb - 1 - 2 * j)

    def body(*refs):
        if second:
            tok_ref, mod_ref, bf_ref, a_ref, si_ref, cf_ref, yp_ref, y_ref, ck_ref, fin_ref, zbuf, st = refs
        else:
            tok_ref, mod_ref, bf_ref, a_ref, fin_ref, zbuf, st = refs
        d = pl.program_id(0)
        j = pl.program_id(1)

        @pl.when(j == 0)
        def _():
            if second:
                st[...] = si_ref[...]
            else:
                st[...] = jnp.zeros_like(st)

        if second:
            ck_ref[...] = st[...]
        u = _mod(tok_ref[...], mod_ref[:, col_sh:col_sh + D], mod_ref[:, col_sc:col_sc + D]).astype(BF16)
        for o in range(NO):
            zbuf[:, o * 1024:(o + 1) * 1024] = _dot(u[:, o * OCT_CH:(o + 1) * OCT_CH], bf_ref[o])
        _recur(zbuf, st, a_ref, NO, ti, d, second)
        if second:
            for o in range(NO):
                y_ref[:, o * OCT_CH:(o + 1) * OCT_CH] = _dot(zbuf[:, o * 1024:(o + 1) * 1024].astype(BF16), cf_ref[o])

        @pl.when(j == nb - 1)
        def _():
            fin_ref[...] = st[...]

    st_spec = pl.BlockSpec((None, 2, 8, NS), lambda d, j: (d, 0, 0, 0))
    in_specs = [pl.BlockSpec((R, D), lambda d, j: (blk(d, j), 0)), _full(mod8.shape),
                pl.BlockSpec((None, NO, OCT_CH, 1024), lambda d, j: (d, 0, 0, 0)), st_spec]
    args = [tok, mod8, bf, acoef]
    scratch = [pltpu.VMEM((R, NO * 1024), F32), pltpu.VMEM((2, 8, NS), F32)]
    if not second:
        return pl.pallas_call(body, name=name, grid=(2, nb), in_specs=in_specs, out_specs=st_spec,
                              out_shape=_sds((2, 2, 8, NS)), scratch_shapes=scratch, compiler_params=_cp(2))(*args)
    in_specs += [st_spec, pl.BlockSpec((None, NO, 1024, OCT_CH), lambda d, j: (d, 0, 0, 0))]
    args += [s_init, cf]
    aliases = {}
    if y_prev is not None:
        in_specs.append(ANY)
        args.append(y_prev)
        aliases = {6: 0}
    else:
        in_specs.append(_full((8, LANES)))
        args.append(jnp.zeros((8, LANES), F32))
    out_specs = [pl.BlockSpec((None, R, D), lambda d, j: (d, blk(d, j), 0)),
                 pl.BlockSpec((None, None, 2, 8, NS), lambda d, j: (d, j + d * (nb - 1 - 2 * j), 0, 0, 0)), st_spec]
    out_shape = [_sds((2, cfg.T, D)), _sds((2, nb, 2, 8, NS)), _sds((2, 2, 8, NS))]
    return pl.pallas_call(body, name=name, grid=(2, nb), in_specs=in_specs, out_specs=out_specs, out_shape=out_shape,
                          input_output_aliases=aliases, scratch_shapes=scratch, compiler_params=_cp(2))(*args)


def _s5_chain(cfg, name, fin_local, acoef, n, inc, prev_fin=None):
    NS = cfg.NS
    nsq = int(round(math.log2(n)))
    assert 2 ** nsq == n

    def body(*refs):
        if prev_fin is not None:
            f_ref, a_ref, p_ref, o_ref = refs
        else:
            f_ref, a_ref, o_ref = refs
        for d in range(2):
            pr, pi = a_ref[d, 0, 0:1, :], a_ref[d, 1, 0:1, :]
            for _ in range(nsq):
                pr, pi = pr * pr - pi * pi, 2.0 * pr * pi
            for b in range(2):
                order = [4 * b + k for k in range(4)]
                if not inc[d]:
                    order = order[::-1]
                if prev_fin is not None:
                    last = order[-1]
                    sr, si = p_ref[d, 0, last:last + 1, :], p_ref[d, 1, last:last + 1, :]
                else:
                    sr = jnp.zeros((1, NS), F32)
                    si = jnp.zeros((1, NS), F32)
                for k in order:
                    o_ref[d, 0, k:k + 1, :] = sr
                    o_ref[d, 1, k:k + 1, :] = si
                    fr, fi = f_ref[d, 0, k:k + 1, :], f_ref[d, 1, k:k + 1, :]
                    sr, si = pr * sr - pi * si + fr, pr * si + pi * sr + fi

    args = [fin_local, acoef] + ([prev_fin] if prev_fin is not None else [])
    return pl.pallas_call(body, name=name, out_shape=_sds((2, 2, 8, NS)))(*args)


def _s5_forward(cfg, tag, tok, mod8, col_sh, col_sc, bf, cf, acoef, point=None):
    saved = {}
    fin_prev = None
    y = None
    for ph, (r0, n) in (("c", (0, cfg.nc)), ("x", (cfg.Tc, cfg.nx))):
        m8 = mod8[0 if ph == "c" else 1]
        loc = _s5_fwd_pass(cfg, f"{tag}_scan1{ph}", tok, m8, col_sh, col_sc, bf, acoef, r0, n)
        if point is not None and ph == "x":
            m8 = _tie(m8, point(loc))
        s_in = _s5_chain(cfg, f"{tag}_chain{ph}", loc, acoef, n, (True, False), fin_prev)
        y, ck, fin_prev = _s5_fwd_pass(cfg, f"{tag}_scan2{ph}", tok, m8, col_sh, col_sc, bf, acoef, r0, n, s_in, cf, y)
        saved[ph] = ck
    return y, saved


def _s5_bwd_pass(cfg, name, dy, tok, mod8, col_sh, col_sc, bf, cf, acoef, acoef_adj, r0, n, g_init=None, ck=None,
                 du_prev=None):
    D, NO, NS = cfg.D, cfg.NO, cfg.NS
    ti = cfg.ti(n)
    nb = n // ti
    R = 8 * ti
    ob = r0 // R
    second = g_init is not None
    has_dy = dy is not None
    blk = lambda d, j: ob + j + (1 - d) * (nb - 1 - 2 * j)

    def body(*refs):
        refs = list(refs)
        dy_ref = refs.pop(0) if has_dy else None
        if second:
            (tok_ref, mod_ref, bf_ref, cf_ref, a_ref, aa_ref, gi_ref, ck_ref, dup_ref,
             du_ref, da_ref, dbf_ref, dcf_ref, gfin_ref, qbuf, zbuf, gst, hst) = refs
        else:
            cf_ref, aa_ref, gfin_ref, qbuf, gst = refs
        d = pl.program_id(0)
        j = pl.program_id(1)

        @pl.when(j == 0)
        def _():
            if second:
                gst[...] = gi_ref[...]
                da_ref[...] = jnp.zeros_like(da_ref)
                dbf_ref[...] = jnp.zeros_like(dbf_ref)
                dcf_ref[...] = jnp.zeros_like(dcf_ref)
            else:
                gst[...] = jnp.zeros_like(gst)

        if has_dy:
            dyb = dy_ref[...].astype(BF16)
            for o in range(NO):
                qbuf[:, o * 1024:(o + 1) * 1024] = _dot_nt(dyb[:, o * OCT_CH:(o + 1) * OCT_CH], cf_ref[o])
        else:
            qbuf[...] = jnp.zeros_like(qbuf)
        _recur(qbuf, gst, aa_ref, NO, ti, 1 - d, second)

        if second:
            u = _mod(tok_ref[...], mod_ref[:, col_sh:col_sh + D], mod_ref[:, col_sc:col_sc + D]).astype(BF16)
            for o in range(NO):
                zbuf[:, o * 1024:(o + 1) * 1024] = _dot(u[:, o * OCT_CH:(o + 1) * OCT_CH], bf_ref[o])
            hst[...] = ck_ref[...]
            _recur(zbuf, hst, a_ref, NO, ti, d, True)

            g_off, h_off = (1 - d) * 8, d * 8
            edge = pl.multiple_of(d * (R - 8), 8)
            for o in range(0, NO, 2):
                cols = [(pl.ds(oo * 1024, OCT_ST), pl.ds(oo * 1024 + OCT_ST, OCT_ST)) for oo in (o, o + 1)]
                scol = [pl.ds(oo * OCT_ST, OCT_ST) for oo in (o, o + 1)]
                init = []
                for s in range(2):
                    er, ei = qbuf[pl.ds(edge, 8), cols[s][0]], qbuf[pl.ds(edge, 8), cols[s][1]]
                    kr, ki = ck_ref[0, :, scol[s]], ck_ref[1, :, scol[s]]
                    init += [er * kr + ei * ki, ei * kr - er * ki]

                def stp(i, carry, cols=cols):
                    rg = pl.multiple_of(i * 8 + g_off, 8)
                    rh = pl.multiple_of(i * 8 + h_off, 8)
                    out = []
                    for s in range(2):
                        gr, gi = qbuf[pl.ds(rg, 8), cols[s][0]], qbuf[pl.ds(rg, 8), cols[s][1]]
                        hr, hi = zbuf[pl.ds(rh, 8), cols[s][0]], zbuf[pl.ds(rh, 8), cols[s][1]]
                        out += [carry[2 * s] + (gr * hr + gi * hi), carry[2 * s + 1] + (gi * hr - gr * hi)]
                    return tuple(out)

                fin = lax.fori_loop(0, (ti - 1) // 2, lambda i2, cy, stp=stp: stp(2 * i2 + 1, stp(2 * i2, cy)), tuple(init))
                if (ti - 1) % 2:
                    fin = stp(ti - 2, fin)
                for s in range(2):
                    da_ref[0, :, scol[s]] += fin[2 * s]
                    da_ref[1, :, scol[s]] += fin[2 * s + 1]

            for o in range(NO):
                gb = qbuf[:, o * 1024:(o + 1) * 1024].astype(BF16)
                uo = u[:, o * OCT_CH:(o + 1) * OCT_CH]
                dbf_ref[o] += _dot_tn(uo, gb)
                if has_dy:
                    dcf_ref[o] += _dot_tn(dyb[:, o * OCT_CH:(o + 1) * OCT_CH], zbuf[:, o * 1024:(o + 1) * 1024].astype(BF16))
                du_ref[:, o * OCT_CH:(o + 1) * OCT_CH] = _dot_nt(gb, bf_ref[o])

        @pl.when(j == nb - 1)
        def _():
            gfin_ref[...] = gst[...]

    st_spec = pl.BlockSpec((None, 2, 8, NS), lambda d, j: (d, 0, 0, 0))
    row_spec = pl.BlockSpec((R, D), lambda d, j: (blk(d, j), 0))
    bf_spec = pl.BlockSpec((None, NO, OCT_CH, 1024), lambda d, j: (d, 0, 0, 0))
    cf_spec = pl.BlockSpec((None, NO, 1024, OCT_CH), lambda d, j: (d, 0, 0, 0))
    in_specs, args = [], []
    if has_dy:
        in_specs.append(row_spec)
        args.append(dy)
    if not second:
        in_specs += [cf_spec, st_spec]
        args += [cf, acoef_adj]
        return pl.pallas_call(body, name=name, grid=(2, nb), in_specs=in_specs, out_specs=st_spec,
                              out_shape=_sds((2, 2, 8, NS)),
                              scratch_shapes=[pltpu.VMEM((R, NO * 1024), F32), pltpu.VMEM((2, 8, NS), F32)],
                              compiler_params=_cp(2))(*args)
    ck_spec = pl.BlockSpec((None, None, 2, 8, NS), lambda d, j: (d, j + (1 - d) * (nb - 1 - 2 * j), 0, 0, 0))
    in_specs += [row_spec, _full(mod8.shape), bf_spec, cf_spec, st_spec, st_spec, st_spec, ck_spec]
    args += [tok, mod8, bf, cf, acoef, acoef_adj, g_init, ck]
    n_before = len(args)
    aliases = {}
    if du_prev is not None:
        in_specs.append(ANY)
        args.append(du_prev)
        aliases = {n_before: 0}
    else:
        in_specs.append(_full((8, LANES)))
        args.append(jnp.zeros((8, LANES), F32))
    out_specs = [pl.BlockSpec((None, R, D), lambda d, j: (d, blk(d, j), 0)), st_spec, bf_spec, bf_spec, st_spec]
    out_shape = [_sds((2, cfg.T, D)), _sds((2, 2, 8, NS)), _sds((2, NO, OCT_CH, 1024)), _sds((2, NO, OCT_CH, 1024)),
                 _sds((2, 2, 8, NS))]
    scratch = [pltpu.VMEM((R, NO * 1024), F32), pltpu.VMEM((R, NO * 1024), F32), pltpu.VMEM((2, 8, NS), F32),
               pltpu.VMEM((2, 8, NS), F32)]
    return pl.pallas_call(body, name=name, grid=(2, nb), in_specs=in_specs, out_specs=out_specs, out_shape=out_shape,
                          input_output_aliases=aliases, scratch_shapes=scratch, compiler_params=_cp(2))(*args)


def _s5_backward(cfg, tag, dy, dy_ctx, tok, mod8, col_sh, col_sc, bf, cf, acoef, acoef_adj, saved):
    g_prev = None
    acc = None
    du = None
    for ph, (r0, n) in (("x", (cfg.Tc, cfg.nx)), ("c", (0, cfg.nc))):
        m8 = mod8[0 if ph == "c" else 1]
        dyp = dy if (ph == "x" or dy_ctx) else None
        loc = _s5_bwd_pass(cfg, f"{tag}_adjA{ph}", dyp, tok, m8, col_sh, col_sc, bf, cf, acoef, acoef_adj, r0, n)
        g_in = _s5_chain(cfg, f"{tag}_adjchain{ph}", loc, acoef_adj, n, (False, True), g_prev)
        du, da, dbf, dcf, g_prev = _s5_bwd_pass(cfg, f"{tag}_adjB{ph}", dyp, tok, m8, col_sh, col_sc, bf, cf, acoef,
                                                acoef_adj, r0, n, g_in, saved[ph], du)
        new = (da, dbf, dcf)
        if acc is None:
            acc = new
        else:
            acc = tuple(_ew(f"{tag}_accsum{q}", lambda a, b: a + b, [a.reshape(-1, a.shape[-1]), b.reshape(-1, b.shape[-1])],
                            [_sds((a.size // a.shape[-1], a.shape[-1]))])[0].reshape(a.shape)
                        for q, (a, b) in enumerate(zip(acc, new)))
    return du, acc


def _tok_specs(cfg, rows, width, tile=None):
    tm = tile or cfg.TM
    off = rows[0] // tm
    return pl.BlockSpec((tm, width), lambda i: (i + off, 0)), rows[1] // tm, off


def _mod_spec(cfg, mod8, off):
    st = _stream_of(cfg, off, cfg.TM)
    return pl.BlockSpec((None, 8, mod8.shape[-1]), lambda i: (st(i), 0, 0))


def _wspec(w):
    fam, slot = w
    _, _, kk, nn = fam.shape
    return pl.BlockSpec((4, None, kk, nn), lambda *i: (0, slot, 0, 0), pipeline_mode=pl.Buffered(1))


def _glu_ln(cfg, name, rows, tok, y, mod8, cols, dskip, w, b, gain, bias):
    D, TM = cfg.D, cfg.TM
    csh, csc, cg = cols
    spec, nt, off = _tok_specs(cfg, rows, D)
    spec2, _, _ = _tok_specs(cfg, rows, 2 * D)

    def body(tok_ref, y_ref, mod_ref, ds_ref, w_ref, b_ref, g_ref, bi_ref, x1_ref, r1_ref, mix_ref, zz_ref, zb_ref, yy_ref):
        t = tok_ref[...]
        u = _mod(t, mod_ref[:, csh:csh + D], mod_ref[:, csc:csc + D])
        yy = ds_ref[...] * u + y_ref[0] + y_ref[1]
        zb = _gelu(yy).astype(BF16)
        zz = jnp.concatenate([_dot(zb, w_ref[s]) for s in range(4)], axis=-1) + b_ref[...]
        mix = zz[:, :D] * _sigmoid(zz[:, D:])
        r1 = DN_ALPHA * t + _rowscale(mix, mod_ref[:, cg:cg + D])
        xhat, _ = _ln_stats(r1)
        x1_ref[...] = xhat * g_ref[...] + bi_ref[...]
        r1_ref[...] = r1
        mix_ref[...] = mix
        zz_ref[...] = zz
        zb_ref[...] = zb
        yy_ref[...] = yy

    T = cfg.T
    return pl.pallas_call(
        body, name=name, grid=(nt,),
        in_specs=[spec, pl.BlockSpec((2, TM, D), lambda i: (0, i + off, 0)), _mod_spec(cfg, mod8, off), _full((1, D)),
                  _wspec(w), _full((1, 2 * D)), _full((1, D)), _full((1, D))],
        out_specs=[spec, spec, spec, spec2, spec, spec],
        out_shape=[_sds((T, D)), _sds((T, D)), _sds((T, D)), _sds((T, 2 * D)), _sds((T, D), BF16), _sds((T, D))],
        compiler_params=_cp(1))(tok, y, mod8, dskip, w[0], b, gain, bias)


def _mlp_ln(cfg, name, rows, x1, mod8, cols, w1, w2, gain, bias):
    D, TM = cfg.D, cfg.TM
    csh, csc, cg = cols
    spec, nt, off = _tok_specs(cfg, rows, D)
    specf, _, _ = _tok_specs(cfg, rows, cfg.F)
    fb = cfg.F // 4

    def body(x_ref, mod_ref, w1_ref, w2_ref, g_ref, bi_ref, x2_ref, r2_ref, out_ref, a_ref, h_ref):
        t = x_ref[...]
        h = _mod(t, mod_ref[:, csh:csh + D], mod_ref[:, csc:csc + D]).astype(BF16)
        out = jnp.zeros((TM, D), F32)
        for s in range(4):
            hid = jnp.maximum(_dot(h, w1_ref[s]), 0.0)
            a = (hid * hid).astype(BF16)
            a_ref[:, s * fb:(s + 1) * fb] = a
            out = out + _dot(a, w2_ref[s])
        r2 = DN_ALPHA * t + _rowscale(out, mod_ref[:, cg:cg + D])
        xhat, _ = _ln_stats(r2)
        x2_ref[...] = xhat * g_ref[...] + bi_ref[...]
        r2_ref[...] = r2
        out_ref[...] = out
        h_ref[...] = h

    T = cfg.T
    return pl.pallas_call(
        body, name=name, grid=(nt,),
        in_specs=[spec, _mod_spec(cfg, mod8, off), _wspec(w1), _wspec(w2), _full((1, D)), _full((1, D))],
        out_specs=[spec, spec, spec, specf, spec],
        out_shape=[_sds((T, D)), _sds((T, D)), _sds((T, D)), _sds((T, cfg.F), BF16), _sds((T, D), BF16)],
        compiler_params=_cp(1))(x1, mod8, w1[0], w2[0], gain, bias)


def _pw1_glu(cfg, name, rows, tok, mod8, cols, w, b):
    D, TM = cfg.D, cfg.TM
    csh, csc = cols
    spec, nt, off = _tok_specs(cfg, rows, D)
    spec2, _, _ = _tok_specs(cfg, rows, 2 * D)

    def body(tok_ref, mod_ref, w_ref, b_ref, aa_ref, ag_ref, h_ref):
        h = _mod(tok_ref[...], mod_ref[:, csh:csh + D], mod_ref[:, csc:csc + D]).astype(BF16)
        aa = jnp.concatenate([_dot(h, w_ref[s]) for s in range(4)], axis=-1) + b_ref[...]
        aa_ref[...] = aa
        ag_ref[...] = aa[:, :D] * _sigmoid(aa[:, D:])
        h_ref[...] = h

    T = cfg.T
    return pl.pallas_call(
        body, name=name, grid=(nt,),
        in_specs=[spec, _mod_spec(cfg, mod8, off), _wspec(w), _full((1, 2 * D))],
        out_specs=[spec2, spec, spec],
        out_shape=[_sds((T, 2 * D)), _sds((T, D)), _sds((T, D), BF16)], compiler_params=_cp(1))(tok, mod8, w[0], b)


def _halo_maps(cfg, rows):
    TM, HB = cfg.TM, cfg.HB
    off = rows[0] // TM
    nct = cfg.Tc // TM
    ntx = cfg.Tx // TM

    def phase(i):
        t = i + off
        is_x = t >= nct
        first = jnp.where(is_x, nct, 0)
        cnt = jnp.where(is_x, ntx, nct)
        return t, first, cnt

    def prev(i):
        t, first, cnt = phase(i)
        return jnp.where(t == first, 2 * (first + cnt) - 1, 2 * t - 1), 0

    def nxt(i):
        t, first, cnt = phase(i)
        return jnp.where(t == first + cnt - 1, 2 * first, 2 * t + 2), 0

    def edge(i):
        t, first, cnt = phase(i)
        return t == first, t == first + cnt - 1

    return prev, nxt, edge, off


def _halo_fix(prev, nxt, is_first, is_last):
    hb, d = prev.shape
    k = lax.broadcasted_iota(jnp.int32, (hb // 8, 8, d), 1)
    p3 = prev.reshape(hb // 8, 8, d)
    n3 = nxt.reshape(hb // 8, 8, d)
    p_roll = jnp.where((k % 4) == 0, 0.0, pltpu.roll(p3, 1, 1))
    n_roll = jnp.where((k % 4) == 3, 0.0, pltpu.roll(n3, 7, 1))
    p3 = jnp.where(is_first, p_roll, p3)
    n3 = jnp.where(is_last, n_roll, n3)
    return p3.reshape(hb, d), n3.reshape(hb, d)


def _dwconv_ln(cfg, name, rows, ag, w_dw, b_dw, ln_g, ln_b):
    D, TM, HB, KW, half = cfg.D, cfg.TM, cfg.HB, cfg.KW, cfg.half
    prev_map, next_map, edge, off = _halo_maps(cfg, rows)
    spec, nt, _ = _tok_specs(cfg, rows, D)

    def body(cur_ref, prev_ref, next_ref, w_ref, b_ref, g_ref, bi_ref, cv_ref, s_ref, ext):
        i = pl.program_id(0)
        is_first, is_last = edge(i)

        @pl.when(i >= 0)
        def _():
            p, n = _halo_fix(prev_ref[...], next_ref[...], is_first, is_last)
            ext[0:HB, :] = p
            ext[HB:HB + TM, :] = cur_ref[...]
            ext[HB + TM:, :] = n

        acc = jnp.zeros((TM, D), F32)
        for k in range(KW):
            lo = HB + 8 * (k - half)
            acc = acc + w_ref[k:k + 1, :] * ext[lo:lo + TM, :]
        cv_ref[...] = acc + b_ref[...]
        xhat, _ = _ln_stats(cv_ref[...])
        nn = xhat * g_ref[...] + bi_ref[...]
        s_ref[...] = (nn * _sigmoid(nn)).astype(BF16)

    T = cfg.T
    return pl.pallas_call(
        body, name=name, grid=(nt,),
        in_specs=[spec, pl.BlockSpec((HB, D), prev_map), pl.BlockSpec((HB, D), next_map), _full((KW, D)),
                  _full((1, D)), _full((1, D)), _full((1, D))],
        out_specs=[spec, spec], out_shape=[_sds((T, D)), _sds((T, D), BF16)],
        scratch_shapes=[pltpu.VMEM((TM + 2 * HB, D), F32)], compiler_params=_cp(1))(ag, ag, ag, w_dw, b_dw, ln_g, ln_b)


def _pw2_ln(cfg, name, rows, s, tok, mod8, cg, w, b, gain, bias):
    D, TM = cfg.D, cfg.TM
    spec, nt, off = _tok_specs(cfg, rows, D)
    kb = D // 4

    def body(s_ref, tok_ref, mod_ref, w_ref, b_ref, g_ref, bi_ref, x1_ref, r1_ref, mix_ref):
        sv = s_ref[...]
        mix = b_ref[...] + jnp.zeros((TM, D), F32)
        for q in range(4):
            mix = mix + _dot(sv[:, q * kb:(q + 1) * kb], w_ref[q])
        r1 = DN_ALPHA * tok_ref[...] + _rowscale(mix, mod_ref[:, cg:cg + D])
        xhat, _ = _ln_stats(r1)
        x1_ref[...] = xhat * g_ref[...] + bi_ref[...]
        r1_ref[...] = r1
        mix_ref[...] = mix

    T = cfg.T
    return pl.pallas_call(
        body, name=name, grid=(nt,),
        in_specs=[spec, spec, _mod_spec(cfg, mod8, off), _wspec(w), _full((1, D)), _full((1, D)), _full((1, D))],
        out_specs=[spec, spec, spec], out_shape=[_sds((T, D))] * 3, compiler_params=_cp(1))(s, tok, mod8, w[0], b, gain, bias)


def _loss(cfg, xf, tgt):
    D, TM = cfg.D, cfg.TM
    spec, nt, off = _tok_specs(cfg, cfg.rows(False), D)

    def body(x_ref, t_ref, l_ref, dx_ref, acc):
        i = pl.program_id(0)
        dlt = x_ref[...] - t_ref[...]

        @pl.when(i == 0)
        def _():
            acc[...] = jnp.zeros_like(acc)

        acc[...] += _sum8(dlt * dlt)
        dx_ref[...] = dlt * (1.0 / D)

        @pl.when(i == nt - 1)
        def _():
            l_ref[...] = jnp.zeros((8, LANES), F32) + jnp.sum(acc[...]) * (0.5 / D)

    return pl.pallas_call(
        body, name="loss", grid=(nt,),
        in_specs=[spec, pl.BlockSpec((TM, D), lambda i: (i, 0))],
        out_specs=[_full((8, LANES)), spec], out_shape=[_sds((8, LANES)), _sds((cfg.T, D))],
        scratch_shapes=[pltpu.VMEM((8, D), F32)], compiler_params=_cp(1))(xf, tgt)


def _masked_spec(cfg, rows, width, valid_from_tile):
    tm = cfg.TM
    off = rows[0] // tm
    return pl.BlockSpec((tm, width), lambda i: (jnp.maximum(i + off, valid_from_tile), 0))


def _lnb(cfg, name, rows, dres, dres_ctx_ok, dh, r, aux, gain, mod_gate, cg, mod_next, csc):
    D, TM = cfg.D, cfg.TM
    spec, nt, off = _tok_specs(cfg, rows, D)
    nct = cfg.Tc // TM
    has_dres, has_dh = dres is not None, dh is not None

    def body(*refs):
        refs = list(refs)
        dres_ref = refs.pop(0) if has_dres else None
        dh_ref = refs.pop(0) if has_dh else None
        r_ref, aux_ref, g_ref, mg_ref = refs[:4]
        refs = refs[4:]
        mn_ref = refs.pop(0) if has_dh else None
        dprev_ref, dbr_ref, dgain_ref, dbias_ref, dg_ref, dsc_ref, dsh_ref, acc_g, acc_b = refs
        i = pl.program_id(0)
        t = i + off
        first_of_stream = (i == 0) | (t == nct)
        xhat, rstd = _ln_stats(r_ref[...])
        dy = jnp.zeros((TM, D), F32)
        if has_dres:
            dv = dres_ref[...]
            if not dres_ctx_ok:
                dv = jnp.where(t >= nct, dv, 0.0)
            dy = dy + dv
        if has_dh:
            dhv = dh_ref[...]
            dy = dy + _rowscale(dhv, 1.0 + mn_ref[:, csc:csc + D])
            x_out = xhat * g_ref[0:1, :] + g_ref[1:2, :]
            s_sc, s_sh = _sum8(dhv * x_out), _sum8(dhv)
        else:
            s_sc = s_sh = jnp.zeros((8, D), F32)
        dr = _ln_bwd(dy * g_ref[0:1, :], xhat, rstd)
        s_g = _sum8(dr * aux_ref[...])

        @pl.when(i == 0)
        def _():
            acc_g[...] = jnp.zeros_like(acc_g)
            acc_b[...] = jnp.zeros_like(acc_b)

        acc_g[...] += _sum8(dy * xhat)
        acc_b[...] += _sum8(dy)

        @pl.when(first_of_stream)
        def _():
            dg_ref[...] = s_g
            dsc_ref[...] = s_sc
            dsh_ref[...] = s_sh

        @pl.when(jnp.logical_not(first_of_stream))
        def _():
            dg_ref[...] += s_g
            dsc_ref[...] += s_sc
            dsh_ref[...] += s_sh

        dprev_ref[...] = DN_ALPHA * dr
        dbr_ref[...] = _rowscale(dr, mg_ref[:, cg:cg + D])

        @pl.when(i == nt - 1)
        def _():
            dgain_ref[...] = jnp.sum(acc_g[...], axis=0, keepdims=True)
            dbias_ref[...] = jnp.sum(acc_b[...], axis=0, keepdims=True)

    st = _stream_of(cfg, off, TM)
    in_specs, args = [], []
    if has_dres:
        in_specs.append(spec if dres_ctx_ok else _masked_spec(cfg, rows, D, nct))
        args.append(dres)
    if has_dh:
        in_specs.append(spec)
        args.append(dh)
    in_specs += [spec, spec, _full((2, D)), _mod_spec(cfg, mod_gate, off)]
    args += [r, aux, gain, mod_gate]
    if has_dh:
        in_specs.append(_mod_spec(cfg, mod_next, off))
        args.append(mod_next)
    acc_spec = pl.BlockSpec((None, 8, D), lambda i: (st(i), 0, 0))
    T = cfg.T
    return pl.pallas_call(
        body, name=name, grid=(nt,), in_specs=in_specs,
        out_specs=[spec, spec, _full((1, D)), _full((1, D)), acc_spec, acc_spec, acc_spec],
        out_shape=[_sds((T, D)), _sds((T, D)), _sds((1, D)), _sds((1, D)), _sds((2, 8, D)), _sds((2, 8, D)), _sds((2, 8, D))],
        scratch_shapes=[pltpu.VMEM((8, D), F32), pltpu.VMEM((8, D), F32)], compiler_params=_cp(1))(*args)


def _mlp_bwd(cfg, name, rows, dbr, a, w1, w2):
    D, TM = cfg.D, cfg.TM
    spec, nt, off = _tok_specs(cfg, rows, D)
    specf, _, _ = _tok_specs(cfg, rows, cfg.F)
    fb = cfg.F // 4

    def body(d_ref, a_ref, w1_ref, w2_ref, dh_ref, dhid_ref, dout_ref):
        dout = d_ref[...].astype(BF16)
        dh = jnp.zeros((TM, D), F32)
        for s in range(4):
            da = _dot_nt(dout, w2_ref[s])
            dhid = (da * (2.0 * jnp.sqrt(a_ref[:, s * fb:(s + 1) * fb].astype(F32)))).astype(BF16)
            dhid_ref[:, s * fb:(s + 1) * fb] = dhid
            dh = dh + _dot_nt(dhid, w1_ref[s])
        dh_ref[...] = dh
        dout_ref[...] = dout

    T = cfg.T
    return pl.pallas_call(
        body, name=name, grid=(nt,),
        in_specs=[spec, specf, _wspec(w1), _wspec(w2)],
        out_specs=[spec, specf, spec],
        out_shape=[_sds((T, D)), _sds((T, cfg.F), BF16), _sds((T, D), BF16)], compiler_params=_cp(1))(dbr, a, w1[0], w2[0])


def _wgrad(cfg, name, rows, a, b, mode, fam, slot):
    tw = cfg.TW
    off = rows[0] // tw
    ntile = rows[1] // tw
    tps = next(q for q in (4, 3, 2, 1) if ntile % q == 0)
    nt = ntile // tps
    fresh = not hasattr(fam, "dtype")
    fam_shape = tuple(fam) if fresh else fam.shape
    _, n, kk, nn = fam_shape

    def body(*refs):
        a_refs, b_refs, o_ref = refs[:tps], refs[tps:2 * tps], refs[-1]
        t = pl.program_id(1)
        part = _dot_tn(a_refs[0][...], b_refs[0][...])
        for q in range(1, tps):
            part = part + _dot_tn(a_refs[q][...], b_refs[q][...])

        @pl.when(t == 0)
        def _():
            o_ref[...] = part

        @pl.when(t > 0)
        def _():
            o_ref[...] += part

    def row(q):
        return lambda s, t: t * tps + q + off

    if mode == "col":
        a_specs = [pl.BlockSpec((tw, kk), lambda s, t, r=row(q): (r(s, t), 0)) for q in range(tps)]
        b_specs = [pl.BlockSpec((tw, nn), lambda s, t, r=row(q): (r(s, t), s)) for q in range(tps)]
    else:
        a_specs = [pl.BlockSpec((tw, kk), lambda s, t, r=row(q): (r(s, t), s)) for q in range(tps)]
        b_specs = [pl.BlockSpec((tw, nn), lambda s, t, r=row(q): (r(s, t), 0)) for q in range(tps)]
    out_spec = pl.BlockSpec((None, None, kk, nn), lambda s, t: (s, slot, 0, 0))
    ins = [a] * tps + [b] * tps
    if fresh:
        return pl.pallas_call(body, name=name, grid=(4, nt), in_specs=a_specs + b_specs, out_specs=out_spec,
                              out_shape=_sds(fam_shape), compiler_params=_cp(2))(*ins)
    return pl.pallas_call(body, name=name, grid=(4, nt), in_specs=a_specs + b_specs + [ANY], out_specs=out_spec,
                          out_shape=_sds(fam_shape), input_output_aliases={2 * tps: 0}, compiler_params=_cp(2))(*ins, fam)


def _glu_bwd(cfg, name, rows, dmix, pre, w, yy=None):
    D, TM = cfg.D, cfg.TM
    spec, nt, off = _tok_specs(cfg, rows, D)
    spec2, _, _ = _tok_specs(cfg, rows, 2 * D)
    hw = w[0].shape[-1]
    has_y = yy is not None

    def body(*refs):
        refs = list(refs)
        d_ref, p_ref, w_ref = refs[:3]
        y_ref = refs[3] if has_y else None
        dz_ref, dp_ref, db_ref, acc = refs[-4:]
        i = pl.program_id(0)
        dm = d_ref[...]
        po, pg = p_ref[:, :D], p_ref[:, D:]
        sg = _sigmoid(pg)
        dpre = jnp.concatenate([dm * sg, dm * po * sg * (1.0 - sg)], axis=-1)

        @pl.when(i == 0)
        def _():
            acc[...] = jnp.zeros_like(acc)

        acc[...] += _sum8(dpre)
        dpb = dpre.astype(BF16)
        dz = jnp.zeros((TM, D), F32)
        for s in range(4):
            dz = dz + _dot_nt(dpb[:, s * hw:(s + 1) * hw], w_ref[s])
        if has_y:
            dz = dz * _gelu_grad(y_ref[...])
        dz_ref[...] = dz
        dp_ref[...] = dpb

        @pl.when(i == nt - 1)
        def _():
            db_ref[...] = jnp.sum(acc[...], axis=0, keepdims=True)

    T = cfg.T
    in_specs = [spec, spec2, _wspec(w)] + ([spec] if has_y else [])
    args = [dmix, pre, w[0]] + ([yy] if has_y else [])
    return pl.pallas_call(
        body, name=name, grid=(nt,), in_specs=in_specs, out_specs=[spec, spec2, _full((1, 2 * D))],
        out_shape=[_sds((T, D)), _sds((T, 2 * D), BF16), _sds((1, 2 * D))],
        scratch_shapes=[pltpu.VMEM((8, 2 * D), F32)], compiler_params=_cp(1))(*args)


def _s5_du(cfg, name, rows, du, dy, dy_from_tile, tok, mod8, cols, dskip):
    D, TM = cfg.D, cfg.TM
    csh, csc = cols
    spec, nt, off = _tok_specs(cfg, rows, D)

    def body(du_ref, dy_ref, tok_ref, mod_ref, ds_ref, dh_ref, dd_ref, acc):
        i = pl.program_id(0)
        dyv = jnp.where(i + off >= dy_from_tile, dy_ref[...], 0.0)
        u = _mod(tok_ref[...], mod_ref[:, csh:csh + D], mod_ref[:, csc:csc + D])
        dh_ref[...] = du_ref[0] + du_ref[1] + ds_ref[...] * dyv

        @pl.when(i == 0)
        def _():
            acc[...] = jnp.zeros_like(acc)

        acc[...] += _sum8(dyv * u)

        @pl.when(i == nt - 1)
        def _():
            dd_ref[...] = jnp.sum(acc[...], axis=0, keepdims=True)

    T = cfg.T
    return pl.pallas_call(
        body, name=name, grid=(nt,),
        in_specs=[pl.BlockSpec((2, TM, D), lambda i: (0, i + off, 0)), _masked_spec(cfg, rows, D, dy_from_tile), spec,
                  _mod_spec(cfg, mod8, off), _full((1, D))],
        out_specs=[spec, _full((1, D))], out_shape=[_sds((T, D)), _sds((1, D))],
        scratch_shapes=[pltpu.VMEM((8, D), F32)], compiler_params=_cp(1))(du, dy, tok, mod8, dskip)


def _pw2_bwd(cfg, name, rows, dmix, cv, w, ln_g, ln_b):
    D, TM = cfg.D, cfg.TM
    spec, nt, off = _tok_specs(cfg, rows, D)
    kb = D // 4

    def body(d_ref, cv_ref, w_ref, g_ref, b_ref, dcv_ref, dmb_ref, sums_ref, acc):
        i = pl.program_id(0)
        dm = d_ref[...]
        dmb = dm.astype(BF16)
        ds = jnp.concatenate([_dot_nt(dmb, w_ref[q]) for q in range(4)], axis=-1)
        xhat, rstd = _ln_stats(cv_ref[...])
        nn = xhat * g_ref[...] + b_ref[...]
        sg = _sigmoid(nn)
        dn = ds * (sg * (1.0 + nn * (1.0 - sg)))
        dcv = _ln_bwd(dn * g_ref[...], xhat, rstd)

        @pl.when(i == 0)
        def _():
            acc[...] = jnp.zeros_like(acc)

        acc[0] += _sum8(dn * xhat)
        acc[1] += _sum8(dn)
        acc[2] += _sum8(dcv)
        acc[3] += _sum8(dm)
        dcv_ref[...] = dcv
        dmb_ref[...] = dmb

        @pl.when(i == nt - 1)
        def _():
            for q in range(4):
                sums_ref[q:q + 1, :] = jnp.sum(acc[q], axis=0, keepdims=True)

    T = cfg.T
    return pl.pallas_call(
        body, name=name, grid=(nt,),
        in_specs=[spec, spec, _wspec(w), _full((1, D)), _full((1, D))],
        out_specs=[spec, spec, _full((4, D))], out_shape=[_sds((T, D)), _sds((T, D), BF16), _sds((4, D))],
        scratch_shapes=[pltpu.VMEM((4, 8, D), F32)], compiler_params=_cp(1))(dmix, cv, w[0], ln_g, ln_b)


def _dwconv_bwd(cfg, name, rows, dcv, ag, w_dw):
    D, TM, HB, KW, half = cfg.D, cfg.TM, cfg.HB, cfg.KW, cfg.half
    prev_map, next_map, edge, off = _halo_maps(cfg, rows)
    spec, nt, _ = _tok_specs(cfg, rows, D)

    def body(dc_ref, dp_ref, dn_ref, ac_ref, ap_ref, an_ref, w_ref, dag_ref, dw_ref, extd, exta, acc):
        i = pl.program_id(0)
        is_first, is_last = edge(i)

        @pl.when(i >= 0)
        def _():
            p, n = _halo_fix(dp_ref[...], dn_ref[...], is_first, is_last)
            extd[0:HB, :] = p
            extd[HB:HB + TM, :] = dc_ref[...]
            extd[HB + TM:, :] = n
            p, n = _halo_fix(ap_ref[...], an_ref[...], is_first, is_last)
            exta[0:HB, :] = p
            exta[HB:HB + TM, :] = ac_ref[...]
            exta[HB + TM:, :] = n

        @pl.when(i == 0)
        def _():
            acc[...] = jnp.zeros_like(acc)

        cr = min(CONV_ROWS, TM)
        for r0 in range(0, TM, cr):
            for lc in range(D // LANES):
                ls = pl.ds(lc * LANES, LANES)
                dcur = dc_ref[r0:r0 + cr, ls]
                dag = jnp.zeros((cr, LANES), F32)
                for k in range(KW):
                    lo = r0 + HB + 8 * (half - k)
                    la = r0 + HB + 8 * (k - half)
                    dag = dag + w_ref[k:k + 1, ls] * extd[lo:lo + cr, ls]
                    acc[k, :, ls] += _sum8(dcur * exta[la:la + cr, ls])
                dag_ref[r0:r0 + cr, ls] = dag

        @pl.when(i == nt - 1)
        def _():
            for k in range(KW):
                dw_ref[k:k + 1, :] = jnp.sum(acc[k], axis=0, keepdims=True)

    T = cfg.T
    hp, hn = pl.BlockSpec((HB, D), prev_map), pl.BlockSpec((HB, D), next_map)
    return pl.pallas_call(
        body, name=name, grid=(nt,), in_specs=[spec, hp, hn, spec, hp, hn, _full((KW, D))],
        out_specs=[spec, _full((KW, D))], out_shape=[_sds((T, D)), _sds((KW, D))],
        scratch_shapes=[pltpu.VMEM((TM + 2 * HB, D), F32), pltpu.VMEM((TM + 2 * HB, D), F32), pltpu.VMEM((KW, 8, D), F32)],
        compiler_params=_cp(1))(dcv, dcv, dcv, ag, ag, ag, w_dw)


def _input_bwd(cfg, dres, dh, tok0, mod8, csc):
    D, TM = cfg.D, cfg.TM
    rows = cfg.rows(True)
    spec, nt, off = _tok_specs(cfg, rows, D)
    nct = cfg.Tc // TM
    st = _stream_of(cfg, off, TM)

    def body(dr_ref, dh_ref, t_ref, mod_ref, gx_ref, dsc_ref, dsh_ref):
        i = pl.program_id(0)
        dhv = dh_ref[...]
        gx_ref[...] = dr_ref[...] + _rowscale(dhv, 1.0 + mod_ref[:, csc:csc + D])
        first = (i == 0) | (i == nct)
        s_sc, s_sh = _sum8(dhv * t_ref[...]), _sum8(dhv)

        @pl.when(first)
        def _():
            dsc_ref[...] = s_sc
            dsh_ref[...] = s_sh

        @pl.when(jnp.logical_not(first))
        def _():
            dsc_ref[...] += s_sc
            dsh_ref[...] += s_sh

    acc_spec = pl.BlockSpec((None, 8, D), lambda i: (st(i), 0, 0))
    return pl.pallas_call(
        body, name="input_bwd", grid=(nt,), in_specs=[spec, spec, spec, _mod_spec(cfg, mod8, off)],
        out_specs=[spec, acc_spec, acc_spec], out_shape=[_sds((cfg.T, D)), _sds((2, 8, D)), _sds((2, 8, D))],
        compiler_params=_cp(1))(dres, dh, tok0, mod8)


def _dmod_rows(dmod8):
    nl, _, _, w = dmod8.shape

    def body(d_ref, o_ref):
        xs = d_ref[1]
        cs = d_ref[0]
        o_ref[...] = jnp.zeros((8, w), F32)
        o_ref[0:1, :] = jnp.sum(xs[0:4], axis=0, keepdims=True)
        o_ref[1:2, :] = jnp.sum(xs[4:8], axis=0, keepdims=True)
        o_ref[2:3, :] = jnp.sum(cs, axis=0, keepdims=True)

    return pl.pallas_call(body, name="dmod_rows", grid=(nl,),
                          in_specs=[pl.BlockSpec((None, 2, 8, w), lambda l: (l, 0, 0, 0))],
                          out_specs=pl.BlockSpec((None, 8, w), lambda l: (l, 0, 0)), out_shape=_sds((nl, 8, w)),
                          compiler_params=_cp(1))(dmod8)


def _x_only(acc):
    return jnp.concatenate([jnp.zeros_like(acc[:1]), acc[1:]], axis=0)


def _pack(parts):
    bufs, meta, off = [], [], 0
    for p in parts:
        n = p.size
        rows = -(-n // (8 * LANES)) * 8
        flat = jnp.pad(p.reshape(-1).astype(F32), (0, rows * LANES - n)).reshape(rows, LANES)
        bufs.append(flat)
        meta.append((off, rows, p.shape))
        off += rows
    if off % 16:
        bufs.append(jnp.zeros((8, LANES), F32))
    return jnp.concatenate(bufs, axis=0), meta


def _unpack(buf, meta):
    out = []
    for off, rows, shape in meta:
        n = 1
        for s in shape:
            n *= s
        out.append(buf[off:off + rows].reshape(-1)[:n].reshape(shape))
    return out


def kernel(x, c, ctx, c_ctx, w_ada, b_ada, ln_gain, ln_bias, s5_lam_re, s5_lam_im, s5_log_dt, s5_b_re, s5_b_im, s5_c_re, s5_c_im, s5_d, s5_w_glu, s5_b_glu, cv_w_pw1, cv_b_pw1, cv_w_dw, cv_b_dw, cv_ln_g, cv_ln_b, cv_w_pw2, cv_b_pw2, mlp_w1, mlp_w2, loss_target, m_c_ctx, m_w_ada, m_b_ada, m_ln_gain, m_ln_bias, m_s5_lam_re, m_s5_lam_im, m_s5_log_dt, m_s5_b_re, m_s5_b_im, m_s5_c_re, m_s5_c_im, m_s5_d, m_s5_w_glu, m_s5_b_glu, m_cv_w_pw1, m_cv_b_pw1, m_cv_w_dw, m_cv_b_dw, m_cv_ln_g, m_cv_ln_b, m_cv_w_pw2, m_cv_b_pw2, m_mlp_w1, m_mlp_w2, v_c_ctx, v_w_ada, v_b_ada, v_ln_gain, v_ln_bias, v_s5_lam_re, v_s5_lam_im, v_s5_log_dt, v_s5_b_re, v_s5_b_im, v_s5_c_re, v_s5_c_im, v_s5_d, v_s5_w_glu, v_s5_b_glu, v_cv_w_pw1, v_cv_b_pw1, v_cv_w_dw, v_cv_b_dw, v_cv_ln_g, v_cv_ln_b, v_cv_w_pw2, v_cv_b_pw2, v_mlp_w1, v_mlp_w2):
    cfg = _Cfg(x, ctx, mlp_w1, cv_w_dw)
    D, T, Tc, Tx, B = cfg.D, cfg.T, cfg.Tc, cfg.Tx, cfg.B
    ax, ay, ac = lax.axis_index("x"), lax.axis_index("y"), lax.axis_index("c")
    shard = 2 * ax + ay
    dev = 4 * ax + 2 * ay + ac
    Ds = D // 4
    Wa = w_ada.shape[2]

    c_pad = jnp.concatenate([c, jnp.zeros((8 - B, D), F32)], axis=0)
    c_gath = _allgather8("gather_c", c_pad).reshape(8, 8, D)[:, :B].reshape(8 * B, D)
    c_all = jnp.concatenate([c_gath, c_ctx[None], jnp.zeros((7, D), F32)], axis=0)
    b_sh = lax.dynamic_slice_in_dim(b_ada, shard * Wa, Wa, axis=1)[:, None, :]
    mod_sh = _ada_fwd(c_all, w_ada, b_sh)
    mod_g = _allgather8("gather_mod", mod_sh.reshape(DEPTH * 24, Wa)).reshape(4, 2, DEPTH, 24, Wa)[:, 0]
    mods = mod_g.transpose(1, 2, 0, 3).reshape(DEPTH, 24, 4 * Wa)
    mine = lax.dynamic_slice_in_dim(mods, B * dev, B, axis=1)
    mod8 = jnp.stack([jnp.broadcast_to(mods[:, 16:17], (DEPTH, 8, 6 * D)), jnp.repeat(mine, 4, axis=1)], axis=1)
    SH1, SC1, G1, SH2, SC2, G2 = (k * D for k in range(6))

    small_parts = [ln_gain.reshape(-1, Ds), ln_bias.reshape(-1, Ds), cv_b_pw1.reshape(-1, Ds), cv_w_dw.reshape(-1, Ds),
                   cv_b_dw, cv_ln_g, cv_ln_b, cv_b_pw2]
    small_rows = [p.shape[0] for p in small_parts]
    sm = jnp.concatenate(small_parts, axis=0)
    pad_r = -sm.shape[0] % 8
    sm = jnp.pad(sm, ((0, pad_r), (0, 0)))
    sm_g = _allgather8("gather_small", sm).reshape(4, 2, sm.shape[0], Ds)[:, 0]
    pieces, o = [], 0
    for nr in small_rows:
        pieces.append(sm_g[:, o:o + nr])
        o += nr

    def unshard(p, lead):
        return p.reshape((4,) + lead + (Ds,)).transpose(tuple(range(1, len(lead) + 1)) + (0, len(lead) + 1)).reshape(lead + (4 * Ds,))

    ln_gain_f = unshard(pieces[0], (DEPTH, 2))
    ln_bias_f = unshard(pieces[1], (DEPTH, 2))
    nconv = cv_w_dw.shape[0]
    b_pw1_f = pieces[2].reshape(4, nconv, 2 * D // 4).transpose(1, 0, 2).reshape(nconv, 2 * D)
    w_dw_f = unshard(pieces[3], (nconv, cfg.KW))
    b_dw_f, cvg_f, cvb_f, b_pw2_f = (unshard(p, (nconv,)) for p in pieces[4:8])

    ns5 = s5_w_glu.shape[0]
    assert mlp_w1.shape[1:] == mlp_w2.shape[1:]
    fam_a = _place_shard("place_w1", mlp_w1, None, 0, 2 * DEPTH)
    fam_a = _place_shard("place_w2", mlp_w2, fam_a, DEPTH, 2 * DEPTH)
    fam_b = _place_shard("place_wglu", s5_w_glu, None, 0, ns5 + nconv)
    fam_b = _place_shard("place_wpw1", cv_w_pw1, fam_b, ns5, ns5 + nconv)
    fam_c = _place_shard("place_wpw2", cv_w_pw2, None, 0, nconv)
    ov = _Overlap()
    gather_tokens = ov.add("gather_b", _gather_gen("gatherb", [fam_b])) + ov.add("gather", _gather_gen("gatherw", [fam_a, fam_c]))

    pos = jnp.broadcast_to(_pos_embed(cfg.L // GRID_W, D)[None], (B, cfg.L, D))
    tok_in = jnp.concatenate([_to_perm(ctx), _to_perm(x)], axis=0)
    pos_in = jnp.concatenate([jnp.zeros((Tc, D), F32), _to_perm(pos)], axis=0)
    tok0 = _ew("add_pos", lambda a, b: a + b, [tok_in, pos_in], [_sds((T, D))])[0]
    mod8 = _tie(mod8, gather_tokens)
    tgt = _to_perm(loss_target)

    def lead(t):
        return t.reshape((2 * ns5,) + t.shape[2:])

    s5_lay = _s5_layouts(cfg, lead(s5_lam_re), lead(s5_lam_im), lead(s5_log_dt), lead(s5_b_re), lead(s5_b_im))
    abr, abi, bbr, bbi = _disc_fwd(*s5_lay)
    acoef_all, acoef_adj_all = _coef_rows(cfg, abr, abi, False), _coef_rows(cfg, abr, abi, True)
    bf_all, cf_all = _blockdiag_b(cfg, bbr, bbi), _blockdiag_c(cfg, lead(s5_c_re), lead(s5_c_im))
    s5p = [dict(acoef=acoef_all[2 * j:2 * j + 2], acoef_adj=acoef_adj_all[2 * j:2 * j + 2], bf=bf_all[2 * j:2 * j + 2],
                cf=cf_all[2 * j:2 * j + 2]) for j in range(ns5)]

    kinds = ["s5" if i % 2 == 0 else "conv" for i in range(DEPTH)]
    tok = tok0
    saved = []
    s5_j = cv_j = 0
    for i in range(DEPTH):
        later_s5 = any(k == "s5" for k in kinds[i + 1:])
        rows = cfg.rows(later_s5)
        m8 = mod8[i]
        sv = dict(tok=tok, rows=rows, kind=kinds[i])
        g0, b0 = ln_gain_f[i, 0][None], ln_bias_f[i, 0][None]
        g1, b1 = ln_gain_f[i, 1][None], ln_bias_f[i, 1][None]
        if kinds[i] == "s5":
            j = s5_j
            s5_j += 1
            p = s5p[j]
            y, ck = _s5_forward(cfg, f"l{i}", tok, m8, SH1, SC1, p["bf"], p["cf"], p["acoef"], ov.point if i == 0 else None)
            m8g = m8
            if i == 0:
                m8g = _tie(m8, ov.point(y))
                (wb_full,) = ov.finish("gather_b", y)
            wg = (wb_full, j)
            x1, r1, mix, zz, zb, yy = _glu_ln(cfg, f"l{i}_glu", rows, tok, y, m8g, (SH1, SC1, G1), s5_d[j][None], wg,
                                              s5_b_glu[j][None], g0, b0)
            sv.update(j=j, ck=ck, zz=zz, zb=zb, yy=yy, wg=wg)
            if i == 0:
                wa_full, wc_full = ov.finish("gather", x1)
        else:
            j = cv_j
            cv_j += 1
            w1c, w2c = (wb_full, ns5 + j), (wc_full, j)
            aa, ag, hb = _pw1_glu(cfg, f"l{i}_pw1", rows, tok, m8, (SH1, SC1), w1c, b_pw1_f[j][None])
            cvv, sb = _dwconv_ln(cfg, f"l{i}_dw", rows, ag, w_dw_f[j], b_dw_f[j][None], cvg_f[j][None], cvb_f[j][None])
            x1, r1, mix = _pw2_ln(cfg, f"l{i}_pw2", rows, sb, tok, m8, G1, w2c, b_pw2_f[j][None], g0, b0)
            sv.update(j=j, aa=aa, ag=ag, hb=hb, cvv=cvv, sb=sb, w1c=w1c, w2c=w2c)
        w1m, w2m = (wa_full, i), (wa_full, DEPTH + i)
        x2, r2, mout, am, hm = _mlp_ln(cfg, f"l{i}_mlp", rows, x1, m8, (SH2, SC2, G2), w1m, w2m, g1, b1)
        sv.update(r1=r1, mix=mix, x1=x1, r2=r2, mout=mout, am=am, hm=hm, w1m=w1m, w2m=w2m, g0=g0, b0=b0, g1=g1, b1=b1)
        saved.append(sv)
        tok = x2

    loss8, dxf = _loss(cfg, tok, tgt)
    loss = lax.psum(loss8[0, 0], ("x", "y", "c"))

    dmod8 = [None] * DEPTH
    g_ln_gain = [[None, None] for _ in range(DEPTH)]
    g_ln_bias = [[None, None] for _ in range(DEPTH)]
    g_s5 = [None] * ns5
    g_cv = [None] * nconv
    dres, dh = dxf, None
    pend = []
    for i in reversed(range(DEPTH)):
        sv = saved[i]
        rows = sv["rows"]
        m8 = mod8[i]
        nxt_m8 = mod8[i + 1] if i + 1 < DEPTH else None
        ctx_ok = True if i + 1 >= DEPTH else (saved[i + 1]["rows"][0] == 0)
        if rows[0] != 0:
            ctx_ok = True
        dprev, dbr, dgn, dbs, dg2, dsc_n, dsh_n = _lnb(
            cfg, f"l{i}_lnb2", rows, dres, ctx_ok, dh, sv["r2"], sv["mout"],
            _tie(jnp.concatenate([sv["g1"], sv["b1"]], 0), pend), m8, G2, nxt_m8, SC1)
        if rows[0] != 0:
            dg2, dsc_n, dsh_n = (_x_only(t) for t in (dg2, dsc_n, dsh_n))
        g_ln_gain[i][1], g_ln_bias[i][1] = dgn[0], dbs[0]
        if i + 1 < DEPTH:
            dmod8[i + 1]["sc1"], dmod8[i + 1]["sh1"] = dsc_n, dsh_n
        dmod8[i] = dict(g2=dg2)
        dh2, dhid, dout = _mlp_bwd(cfg, f"l{i}_mlpb", rows, dbr, sv["am"], sv["w1m"], sv["w2m"])
        pend = ov.point(dh2)
        ga = _wgrad(cfg, f"l{i}_gw1", rows, sv["hm"], dhid, "col", (4, 2, D, cfg.F // 4), 0)
        ga = _wgrad(cfg, f"l{i}_gw2", rows, sv["am"], dout, "row", ga, 1)
        dprev1, dbr1, dgn, dbs, dg1, dsc2, dsh2 = _lnb(
            cfg, f"l{i}_lnb1", rows, dprev, True, dh2, sv["r1"], sv["mix"],
            _tie(jnp.concatenate([sv["g0"], sv["b0"]], 0), pend), m8, G1, m8, SC2)
        if rows[0] != 0:
            dg1, dsc2, dsh2 = (_x_only(t) for t in (dg1, dsc2, dsh2))
        g_ln_gain[i][0], g_ln_bias[i][0] = dgn[0], dbs[0]
        dmod8[i].update(g1=dg1, sc2=dsc2, sh2=dsh2)
        j = sv["j"]
        if sv["kind"] == "s5":
            p = s5p[j]
            dyy, dzz, dbglu = _glu_bwd(cfg, f"l{i}_glub", rows, dbr1, sv["zz"], sv["wg"], sv["yy"])
            pend = ov.point(dyy)
            gb = _wgrad(cfg, f"l{i}_gwg", rows, sv["zb"], dzz, "col", (4, 1, D, D // 2), 0)
            du, (da, dbf, dcf) = _s5_backward(cfg, f"l{i}", dyy, rows[0] == 0, sv["tok"], m8, SH1, SC1, p["bf"], p["cf"],
                                              p["acoef"], _tie(p["acoef_adj"], pend), sv["ck"])
            dh, dds = _s5_du(cfg, f"l{i}_du", cfg.rows(True), du, dyy, rows[0] // cfg.TM, sv["tok"], m8, (SH1, SC1),
                             s5_d[j][None])
            g_s5[j] = dict(da=da, dbf=dbf, dcf=dcf, dd=dds[0], dbglu=dbglu[0])
            layer_grads = [ga, gb]
        else:
            dcv, dmb, sums = _pw2_bwd(cfg, f"l{i}_pw2b", rows, dbr1, sv["cvv"], sv["w2c"], cvg_f[j][None], cvb_f[j][None])
            pend = ov.point(dcv)
            gc = _wgrad(cfg, f"l{i}_gwp2", rows, sv["sb"], dmb, "row", (4, 1, D // 4, D), 0)
            dag, dwdw = _dwconv_bwd(cfg, f"l{i}_dwb", rows, dcv, sv["ag"], _tie(w_dw_f[j], pend))
            dh, daa, dbpw1 = _glu_bwd(cfg, f"l{i}_pw1b", rows, dag, sv["aa"], sv["w1c"])
            gb = _wgrad(cfg, f"l{i}_gwp1", rows, sv["hb"], daa, "col", (4, 1, D, D // 2), 0)
            g_cv[j] = dict(ln_g=sums[0], ln_b=sums[1], b_dw=sums[2], b_pw2=sums[3], w_dw=dwdw, b_pw1=dbpw1[0])
            layer_grads = [ga, gb, gc]
        dres = dprev1
        pend = ov.point(dh) + ov.add(f"rs{i}", _reduce_scatter_gen(f"gw{i}", layer_grads))
    gx_perm, dsc0, dsh0 = _input_bwd(cfg, dres, dh, tok0, _tie(mod8[0], pend), SC1)
    dmod8[0]["sc1"], dmod8[0]["sh1"] = dsc0, dsh0
    grad_x = _from_perm(gx_perm[Tc:], B, cfg.L)

    zero28 = jnp.zeros((2, 8, D), F32)
    dm8 = jnp.stack([jnp.concatenate([dmod8[i].get(k, zero28) for k in ("sh1", "sc1", "g1", "sh2", "sc2", "g2")], axis=-1)
                     for i in range(DEPTH)])
    dm_rows = _dmod_rows(dm8)
    dm_tab = jnp.zeros((DEPTH, 24, 6 * D), F32)
    dm_tab = lax.dynamic_update_slice_in_dim(dm_tab, dm_rows[:, 0:B], B * dev, axis=1)
    dm_tab = lax.dynamic_update_slice_in_dim(dm_tab, dm_rows[:, 2:3], 16, axis=1)

    dbbr, dbbi = _diag_b(cfg, jnp.concatenate([g["dbf"] for g in g_s5], axis=0))
    dcr, dci = _diag_c(cfg, jnp.concatenate([g["dcf"] for g in g_s5], axis=0))
    eye_parts = [jnp.concatenate([g["da"] for g in g_s5], axis=0), dbbr, dbbi, dcr, dci,
                 jnp.stack([g["dd"] for g in g_s5]), jnp.stack([g["dbglu"] for g in g_s5])]
    for j in range(nconv):
        g = g_cv[j]
        eye_parts += [g["ln_g"], g["ln_b"], g["b_dw"], g["b_pw2"], g["w_dw"], g["b_pw1"]]
    eye_parts += [jnp.stack([jnp.stack(r) for r in g_ln_gain]), jnp.stack([jnp.stack(r) for r in g_ln_bias]), dm_tab]
    buf, meta = _pack(eye_parts)
    buf = _tie(buf, ov.point(gx_perm))
    red_buf = _allreduce8("small", buf, ov.point)
    reduced = [ov.finish(f"rs{i}", red_buf) for i in range(DEPTH)]
    red = _unpack(red_buf, meta)

    grads = {}
    nd = 2 * ns5
    da, dbbr, dbbi, dcr, dci, dd, dbglu = red[0:7]
    k = 7
    da_s = _sublane_sum("s5_dasum", da.reshape(2 * nd, 8, cfg.NS)).reshape(nd, 2, cfg.NS)
    g_abr = da_s[:, 0].reshape(nd, cfg.G, cfg.P).transpose(2, 0, 1).reshape(cfg.P, nd * cfg.G)
    g_abi = da_s[:, 1].reshape(nd, cfg.G, cfg.P).transpose(2, 0, 1).reshape(cfg.P, nd * cfg.G)
    glr, gli, gldt, gbr, gbi = _disc_bwd(*s5_lay, g_abr, g_abi, dbbr, dbbi)
    grads.update(s5_lam_re=glr.reshape(cfg.P, nd, cfg.G).transpose(1, 2, 0), s5_lam_im=gli.reshape(cfg.P, nd, cfg.G).transpose(1, 2, 0),
                 s5_log_dt=gldt, s5_b_re=gbr.reshape(S5_GROUP, cfg.P, nd, cfg.G).transpose(2, 3, 1, 0),
                 s5_b_im=gbi.reshape(S5_GROUP, cfg.P, nd, cfg.G).transpose(2, 3, 1, 0), s5_c_re=dcr, s5_c_im=dci,
                 s5_d=dd, s5_b_glu=dbglu)

    def my_cols(full, width):
        return lax.dynamic_slice_in_dim(full, shard * width, width, axis=full.ndim - 1)

    cvs = {n: [] for n in ("ln_g", "ln_b", "b_dw", "b_pw2", "w_dw", "b_pw1")}
    for j in range(nconv):
        for n, val in zip(("ln_g", "ln_b", "b_dw", "b_pw2", "w_dw", "b_pw1"), red[k:k + 6]):
            cvs[n].append(val)
        k += 6
    grads.update(cv_ln_g=my_cols(jnp.stack(cvs["ln_g"]), Ds), cv_ln_b=my_cols(jnp.stack(cvs["ln_b"]), Ds),
                 cv_b_dw=my_cols(jnp.stack(cvs["b_dw"]), Ds), cv_b_pw2=my_cols(jnp.stack(cvs["b_pw2"]), Ds),
                 cv_w_dw=my_cols(jnp.stack(cvs["w_dw"]), Ds), cv_b_pw1=my_cols(jnp.stack(cvs["b_pw1"]), 2 * D // 4))
    grads.update(ln_gain=my_cols(red[k], Ds), ln_bias=my_cols(red[k + 1], Ds))
    dm_all = red[k + 2]

    dm_sh = lax.dynamic_slice_in_dim(dm_all, shard * Wa, Wa, axis=2)
    gw_ada, dcond = _ada_bwd(c_all, dm_sh, w_ada)
    grads["w_ada"] = gw_ada
    grads["b_ada"] = _colsum_groups("ada_bsum", dm_all)
    dc_part = dcond[0:1]
    dc_buf = jnp.concatenate([jnp.where(ac == 0, dc_part, 0.0), jnp.zeros((7, D), F32)], axis=0)
    dc_tot = _allreduce8("cctx", dc_buf.reshape(8 * D // LANES, LANES)).reshape(8, D)[0:1]
    grads["c_ctx"] = _ew("cctx_grad", lambda g, cv: g * (_sigmoid(cv) * (1.0 + cv * (1.0 - _sigmoid(cv)))),
                         [jnp.broadcast_to(dc_tot, (8, D)), jnp.broadcast_to(c_ctx[None], (8, D))], [_sds((8, D))])[0][0]

    s5_layers = [i for i in range(DEPTH) if kinds[i] == "s5"]
    cv_layers = [i for i in range(DEPTH) if kinds[i] == "conv"]
    grads.update(mlp_w1=jnp.stack([reduced[i][0][0] for i in range(DEPTH)]),
                 mlp_w2=jnp.stack([reduced[i][0][1] for i in range(DEPTH)]),
                 s5_w_glu=jnp.stack([reduced[i][1][0] for i in s5_layers]),
                 cv_w_pw1=jnp.stack([reduced[i][1][0] for i in cv_layers]),
                 cv_w_pw2=jnp.stack([reduced[i][2][0] for i in cv_layers]))

    weights = dict(c_ctx=c_ctx, w_ada=w_ada, b_ada=b_ada, ln_gain=ln_gain, ln_bias=ln_bias, s5_lam_re=s5_lam_re,
                   s5_lam_im=s5_lam_im, s5_log_dt=s5_log_dt, s5_b_re=s5_b_re, s5_b_im=s5_b_im, s5_c_re=s5_c_re,
                   s5_c_im=s5_c_im, s5_d=s5_d, s5_w_glu=s5_w_glu, s5_b_glu=s5_b_glu, cv_w_pw1=cv_w_pw1, cv_b_pw1=cv_b_pw1,
                   cv_w_dw=cv_w_dw, cv_b_dw=cv_b_dw, cv_ln_g=cv_ln_g, cv_ln_b=cv_ln_b, cv_w_pw2=cv_w_pw2, cv_b_pw2=cv_b_pw2,
                   mlp_w1=mlp_w1, mlp_w2=mlp_w2)
    ms = dict(c_ctx=m_c_ctx, w_ada=m_w_ada, b_ada=m_b_ada, ln_gain=m_ln_gain, ln_bias=m_ln_bias, s5_lam_re=m_s5_lam_re,
              s5_lam_im=m_s5_lam_im, s5_log_dt=m_s5_log_dt, s5_b_re=m_s5_b_re, s5_b_im=m_s5_b_im, s5_c_re=m_s5_c_re,
              s5_c_im=m_s5_c_im, s5_d=m_s5_d, s5_w_glu=m_s5_w_glu, s5_b_glu=m_s5_b_glu, cv_w_pw1=m_cv_w_pw1,
              cv_b_pw1=m_cv_b_pw1, cv_w_dw=m_cv_w_dw, cv_b_dw=m_cv_b_dw, cv_ln_g=m_cv_ln_g, cv_ln_b=m_cv_ln_b,
              cv_w_pw2=m_cv_w_pw2, cv_b_pw2=m_cv_b_pw2, mlp_w1=m_mlp_w1, mlp_w2=m_mlp_w2)
    vs = dict(c_ctx=v_c_ctx, w_ada=v_w_ada, b_ada=v_b_ada, ln_gain=v_ln_gain, ln_bias=v_ln_bias, s5_lam_re=v_s5_lam_re,
              s5_lam_im=v_s5_lam_im, s5_log_dt=v_s5_log_dt, s5_b_re=v_s5_b_re, s5_b_im=v_s5_b_im, s5_c_re=v_s5_c_re,
              s5_c_im=v_s5_c_im, s5_d=v_s5_d, s5_w_glu=v_s5_w_glu, s5_b_glu=v_s5_b_glu, cv_w_pw1=v_cv_w_pw1,
              cv_b_pw1=v_cv_b_pw1, cv_w_dw=v_cv_w_dw, cv_b_dw=v_cv_b_dw, cv_ln_g=v_cv_ln_g, cv_ln_b=v_cv_ln_b,
              cv_w_pw2=v_cv_w_pw2, cv_b_pw2=v_cv_b_pw2, mlp_w1=v_mlp_w1, mlp_w2=v_mlp_w2)
    names = list(weights)
    deltas, new_m, new_v = {}, {}, {}
    for n in names:
        g = grads[n].reshape(weights[n].shape)
        grads[n] = g
        deltas[n], new_m[n], new_v[n] = _adamw("adamw_" + n, weights[n], g, ms[n], vs[n])
    return (loss, grad_x, *[grads[n] for n in names], *[deltas[n] for n in names], *[new_m[n] for n in names],
            *[new_v[n] for n in names])


def _sublane_sum(name, a):
    n, _, w = a.shape

    def body(a_ref, o_ref):
        for q in range(n):
            o_ref[q:q + 1, :] = jnp.sum(a_ref[q], axis=0, keepdims=True)

    return pl.pallas_call(body, name=name, out_shape=_sds((n, w)))(a)


def _colsum_groups(name, dm_all):
    nl, nr, w = dm_all.shape

    def body(d_ref, o_ref):
        o_ref[...] = jnp.zeros((8, w), F32) + jnp.sum(d_ref[...], axis=0, keepdims=True)

    out = pl.pallas_call(body, name=name, grid=(nl,), in_specs=[pl.BlockSpec((None, nr, w), lambda l: (l, 0, 0))],
                         out_specs=pl.BlockSpec((None, 8, w), lambda l: (l, 0, 0)), out_shape=_sds((nl, 8, w)),
                         compiler_params=_cp(1))(dm_all)
    return out[:, 0]
```

```python
import functools
import math

import jax
import jax.numpy as jnp
from jax import lax
from jax.experimental import pallas as pl
from jax.experimental.pallas import tpu as pltpu

F32 = jnp.float32
BF16 = jnp.bfloat16
MESH = pl.DeviceIdType.MESH
ANY = pl.BlockSpec(memory_space=pl.ANY)

DEPTH = 4
S5_GROUP = 16
S5_STATE = 64
GRID_W = 64
POS_TEMP = 10000.0
LAMBDA_RE_MAX = -1e-4
LN_EPS = 1e-5
DN_ALPHA = (2.0 * DEPTH) ** 0.25
ADAM_LR, ADAM_B1, ADAM_B2, ADAM_EPS, ADAM_WD, ADAM_STEP = 0.001, 0.9, 0.999, 1e-08, 0.01, 10

SUBLANES = 8
LANES = 128
OCT_CH = 128
OCT_ST = 512
CONV_ROWS = 64
VMEM_LIMIT = 56 * 1024 * 1024


def _cp(n_axes):
    return pltpu.CompilerParams(dimension_semantics=("arbitrary",) * n_axes, vmem_limit_bytes=VMEM_LIMIT)


def _full(shape, single=False):
    nd = len(shape)
    if single:
        return pl.BlockSpec(shape, lambda *i: (0,) * nd, pipeline_mode=pl.Buffered(1))
    return pl.BlockSpec(shape, lambda *i: (0,) * nd)


def _sds(shape, dtype=F32):
    return jax.ShapeDtypeStruct(tuple(shape), dtype)


def _mod(x, sh8, sc8):
    r, d = x.shape
    return (x.reshape(r // 8, 8, d) * (1.0 + sc8[None]) + sh8[None]).reshape(r, d)


def _rowscale(x, g8):
    r, d = x.shape
    return (x.reshape(r // 8, 8, d) * g8[None]).reshape(r, d)


def _sum8(x):
    r, w = x.shape
    return jnp.sum(x.reshape(r // 8, 8, w), axis=0)


def _ln_stats(r):
    mu = jnp.mean(r, axis=-1, keepdims=True)
    xc = r - mu
    var = jnp.mean(xc * xc, axis=-1, keepdims=True)
    rstd = lax.rsqrt(var + LN_EPS)
    return xc * rstd, rstd


def _ln_bwd(dxh, xhat, rstd):
    m1 = jnp.mean(dxh, axis=-1, keepdims=True)
    m2 = jnp.mean(dxh * xhat, axis=-1, keepdims=True)
    return rstd * (dxh - m1 - xhat * m2)


def _sigmoid(x):
    return 1.0 / (1.0 + jnp.exp(-x))


def _gelu(y):
    return 0.5 * y * (1.0 + lax.erf(y * (1.0 / math.sqrt(2.0))))


def _gelu_grad(y):
    return 0.5 * (1.0 + lax.erf(y * (1.0 / math.sqrt(2.0)))) + y * jnp.exp(-0.5 * y * y) * (1.0 / math.sqrt(2.0 * math.pi))


def _dot(a, b):
    return jnp.dot(a, b, preferred_element_type=F32)


def _dot_nt(a, b):
    return lax.dot_general(a, b, (((1,), (1,)), ((), ())), preferred_element_type=F32)


def _dot_tn(a, b):
    return lax.dot_general(a, b, (((0,), (0,)), ((), ())), preferred_element_type=F32)


class _Cfg:
    def __init__(self, x, ctx, mlp_w1, cv_w_dw):
        self.B, self.L, self.D = x.shape
        self.Lc = ctx.shape[1]
        assert self.B * 4 == SUBLANES, "two examples per device, four chunks each"
        self.F = mlp_w1.shape[2] * 4
        self.KW = cv_w_dw.shape[1]
        self.half = self.KW // 2
        self.G = self.D // S5_GROUP
        self.P = S5_STATE
        self.NS = self.G * self.P
        self.NO = self.D // OCT_CH
        assert self.NO % 2 == 0
        self.nx = self.L // 4
        self.nc = self.Lc // 4
        self.Tc = self.B * self.Lc
        self.Tx = self.B * self.L
        self.T = self.Tc + self.Tx
        self.TM = 512 if self.Tc % 512 == 0 else self.Tc
        assert self.Tx % self.TM == 0 and self.TM % 16 == 0
        self.HB = self.TM // 2
        assert SUBLANES * self.half <= self.HB
        self.TW = 512 if (self.Tc % 512 == 0 and self.Tx % 512 == 0) else self.TM

    def ti(self, n):
        t = 32 if self.nc % 32 == 0 else self.nc
        assert n % t == 0 and self.Tc % (8 * t) == 0
        return t

    def rows(self, ctx_too):
        return (0, self.T) if ctx_too else (self.Tc, self.Tx)


def _allgather8(name, x_shard):
    m_per, n = x_shard.shape
    assert m_per % 8 == 0

    def body(x_ref, out_ref, send_sems, recv_sems, local_sem):
        x, y, c = lax.axis_index("x"), lax.axis_index("y"), lax.axis_index("c")
        me, sibling = (x, y, c), (x, y, 1 - c)
        chips = [(1 - x, y), (x, 1 - y), (1 - x, 1 - y)]

        def rows(px, py, pc):
            return out_ref.at[pl.ds((4 * px + 2 * py + pc) * m_per, m_per), :]

        def copy(k, block, to, src=None):
            return pltpu.make_async_remote_copy(
                src_ref=rows(*block) if src is None else src, dst_ref=rows(*block),
                send_sem=send_sems.at[k], recv_sem=recv_sems.at[k], device_id=to, device_id_type=MESH)

        mine = pltpu.make_async_copy(x_ref, rows(*me), local_sem)
        mine.start()
        first = [copy(0, me, sibling, src=x_ref)]
        first += [copy(1 + j, me, (*chip, c), src=x_ref) for j, chip in enumerate(chips)]
        for cp in first:
            cp.start()
        passed = [copy(4 + j, (*chip, c), sibling) for j, chip in enumerate(chips)]
        for j, chip in enumerate(chips):
            copy(1 + j, (*chip, c), me).wait_recv()
            passed[j].start()
        copy(0, sibling, me).wait_recv()
        for j, chip in enumerate(chips):
            copy(4 + j, (*chip, 1 - c), me).wait_recv()
        for cp in first + passed:
            cp.wait_send()
        mine.wait()

    return pl.pallas_call(
        body, name=name, out_shape=_sds((8 * m_per, n), x_shard.dtype),
        in_specs=[pl.BlockSpec(memory_space=pltpu.VMEM)], out_specs=pl.BlockSpec(memory_space=pltpu.VMEM),
        scratch_shapes=[pltpu.SemaphoreType.DMA((7,)), pltpu.SemaphoreType.DMA((7,)), pltpu.SemaphoreType.DMA],
        compiler_params=pltpu.CompilerParams(vmem_limit_bytes=VMEM_LIMIT),
    )(x_shard)


def _flip(v, m):
    return v + m - 2 * v * m


def _peer(axis):
    x, y, c = lax.axis_index("x"), lax.axis_index("y"), lax.axis_index("c")
    if axis == "c":
        return (x, y, 1 - c)
    if axis == "xy":
        return (_flip(x, 1 - c), _flip(y, c), c)
    if axis == "yx":
        return (_flip(x, c), _flip(y, 1 - c), c)
    raise ValueError(axis)


def _pair_exchange(name, axis, inputs, out_shapes, aliases, plan):
    n_in = len(inputs)
    n_out = len(out_shapes)

    def body(*refs):
        ins, outs = refs[:n_in], refs[n_in:n_in + n_out]
        send_sems, recv_sems, local_sems = refs[n_in + n_out:]
        x, y, c = lax.axis_index("x"), lax.axis_index("y"), lax.axis_index("c")
        remote, local = plan(x, y, c, ins, outs)
        lcs = [pltpu.make_async_copy(s, d, local_sems.at[k]) for k, (s, d) in enumerate(local)]
        for cp in lcs:
            cp.start()
        rcs = [pltpu.make_async_remote_copy(src_ref=s, dst_ref=d, send_sem=send_sems.at[k], recv_sem=recv_sems.at[k],
                                            device_id=_peer(axis), device_id_type=MESH) for k, (s, d) in enumerate(remote)]
        for cp in rcs:
            cp.start()
        for cp in rcs:
            cp.wait()
        for cp in lcs:
            cp.wait()

    n_remote, n_local = plan.counts
    return pl.pallas_call(
        body, name=name, out_shape=tuple(out_shapes),
        in_specs=[ANY] * n_in, out_specs=tuple([ANY] * n_out),
        input_output_aliases=dict(aliases),
        scratch_shapes=[pltpu.SemaphoreType.DMA((n_remote,)), pltpu.SemaphoreType.DMA((n_remote,)),
                        pltpu.SemaphoreType.DMA((max(n_local, 1),))],
    )(*inputs)


def _plan(n_remote, n_local=0):
    def deco(fn):
        fn.counts = (n_remote, n_local)
        return fn
    return deco


HBM = pl.BlockSpec(memory_space=pltpu.HBM)
SEM = pl.BlockSpec(memory_space=pltpu.SEMAPHORE)


def _split_start(name, axis, bufs, plan):
    nb = len(bufs)
    n = plan.counts[0]

    def body(*refs):
        ins, send_sem, recv_sem, token = refs[:nb], refs[nb], refs[nb + 1], refs[-1]
        x, y, c = lax.axis_index("x"), lax.axis_index("y"), lax.axis_index("c")
        for k, (s, d) in enumerate(plan(x, y, c, ins)):
            pltpu.make_async_remote_copy(src_ref=s, dst_ref=d, send_sem=send_sem.at[k], recv_sem=recv_sem.at[k],
                                         device_id=_peer(axis), device_id_type=MESH).start()
        token[...] = jnp.zeros_like(token)

    outs = pl.pallas_call(
        body, name=name,
        out_shape=(pltpu.SemaphoreType.DMA((n,)), pltpu.SemaphoreType.DMA((n,)),
                   *[pltpu.HBM(b.shape, b.dtype) for b in bufs], _sds((8, LANES))),
        in_specs=[HBM] * nb, out_specs=(SEM, SEM, *([HBM] * nb), pl.BlockSpec(memory_space=pltpu.VMEM)),
        input_output_aliases={i: 2 + i for i in range(nb)},
        compiler_params=pltpu.CompilerParams(has_side_effects=pltpu.SideEffectType.DATAFLOW_SIDE_EFFECTING),
    )(*[pltpu.with_memory_space_constraint(b, pltpu.HBM) for b in bufs])
    return dict(name=name, axis=axis, plan=plan, send=outs[0], recv=outs[1], bufs=list(outs[2:2 + nb]), token=outs[-1])


def _split_wait(h, after):
    bufs, plan, axis = h["bufs"], h["plan"], h["axis"]
    nb = len(bufs)

    def body(*refs):
        ins, send_sem, recv_sem = refs[:nb], refs[nb], refs[nb + 1]
        x, y, c = lax.axis_index("x"), lax.axis_index("y"), lax.axis_index("c")
        for k, (s, d) in enumerate(plan(x, y, c, ins)):
            cp = pltpu.make_async_remote_copy(src_ref=s, dst_ref=d, send_sem=send_sem.at[k], recv_sem=recv_sem.at[k],
                                              device_id=_peer(axis), device_id_type=MESH)
            cp.wait_send()
            cp.wait_recv()

    outs = pl.pallas_call(
        body, name=h["name"] + "_wait", out_shape=tuple(pltpu.HBM(b.shape, b.dtype) for b in bufs),
        in_specs=[HBM] * nb + [SEM, SEM, ANY], out_specs=tuple([HBM] * nb),
        input_output_aliases={i: i for i in range(nb)},
        compiler_params=pltpu.CompilerParams(has_side_effects=pltpu.SideEffectType.DATAFLOW_SIDE_EFFECTING),
    )(*bufs, h["send"], h["recv"], after)
    return list(outs)


def _tie(small, tokens):
    for t in tokens:
        small = small + t[0, 0]
    return small


class _Overlap:
    def __init__(self):
        self.live = {}
        self.done = {}

    def add(self, key, gen):
        self.live[key] = gen
        return [next(gen)]

    def point(self, arr):
        tokens = []
        for key in list(self.live):
            try:
                tokens.append(self.live[key].send(arr))
            except StopIteration as e:
                self.done[key] = e.value
                del self.live[key]
        return tokens

    def finish(self, key, arr):
        while key in self.live:
            try:
                self.live[key].send(arr)
            except StopIteration as e:
                self.done[key] = e.value
                del self.live[key]
        return self.done.pop(key)


def _gather_gen(tag, fams):
    nf = len(fams)
    shapes = [f.shape for f in fams]
    views = [f.reshape(4, 2, -1, f.shape[-1]) for f in fams]

    @_plan(nf)
    def plan1(x, y, c, refs):
        s = 2 * x + y
        return [(refs[k].at[s, c], refs[k].at[s, c]) for k in range(nf)]

    @_plan(2 * nf)
    def plan2(x, y, c, refs):
        shards = [2 * x + y, 2 * _flip(x, 1 - c) + _flip(y, c)]
        return [(refs[k].at[s, c], refs[k].at[s, c]) for k in range(nf) for s in shards]

    @_plan(3 * nf)
    def plan3(x, y, c, refs):
        shards = [2 * (1 - x) + y, 2 * x + (1 - y), 2 * (1 - x) + (1 - y)]
        return [(refs[k].at[s, c], refs[k].at[s, c]) for k in range(nf) for s in shards]

    for rnd, (axis, plan) in enumerate((("xy", plan1), ("yx", plan2), ("c", plan3))):
        h = _split_start(f"{tag}_g{rnd}", axis, views, plan)
        after = yield h["token"]
        views = _split_wait(h, after)
    return [v.reshape(sh) for v, sh in zip(views, shapes)]


def _reduce_scatter_gen(tag, grads):
    ng = len(grads)
    flat = [g.reshape(4, 2, -1, g.shape[-1]) for g in grads]

    def empty(shape, dtype):
        return lax.empty(tuple(shape), dtype)

    @_plan(ng)
    def plan1(x, y, c, refs):
        return [(refs[k].at[:, 1 - c], refs[ng + k]) for k in range(ng)]

    h = _split_start(tag + "_r0", "c", flat + [empty((4,) + f.shape[2:], F32) for f in flat], plan1)
    after = yield h["token"]
    bufs = _split_wait(h, after)
    p1 = [_sel_add(f"{tag}_add1_{k}", bufs[k], lambda j, sc: (j, sc[2]), bufs[ng + k], True) for k in range(ng)]

    def sent1(kk, x, y, c):
        return ((1 - c) * kk + c * (1 - x), (1 - c) * (1 - y) + c * kk)

    def kept1(j, sc):
        x, y, c = sc[0], sc[1], sc[2]
        return ((1 - c) * j + c * x, (1 - c) * y + c * j)

    @_plan(2 * ng)
    def plan2(x, y, c, refs):
        return [(refs[k].at[sent1(kk, x, y, c)], refs[ng + k].at[kk]) for k in range(ng) for kk in range(2)]

    v1 = [pb.reshape(2, 2, pb.shape[1], pb.shape[2]) for p, pb in p1]
    h = _split_start(tag + "_r1", "yx", v1 + [empty((2,) + v.shape[2:], BF16) for v in v1], plan2)
    after = yield h["token"]
    bufs = _split_wait(h, after)
    p2 = [_sel_add(f"{tag}_add2_{k}", p1[k][0].reshape(2, 2, p1[k][0].shape[1], p1[k][0].shape[2]), kept1, bufs[ng + k], True)
          for k in range(ng)]

    @_plan(ng)
    def plan3(x, y, c, refs):
        return [(refs[k].at[(1 - c) * (1 - x) + c * (1 - y)], refs[ng + k]) for k in range(ng)]

    h = _split_start(tag + "_r2", "xy", [qb for q, qb in p2] + [empty(qb.shape[1:], BF16) for q, qb in p2], plan3)
    after = yield h["token"]
    bufs = _split_wait(h, after)
    fin = [_sel_add(f"{tag}_add3_{k}", p2[k][0], lambda j, sc: ((1 - sc[2]) * sc[0] + sc[2] * sc[1],), bufs[ng + k][None],
                    False, out_slots=(2, lambda j, sc: sc[2]))[0] for k in range(ng)]

    @_plan(ng)
    def plan4(x, y, c, refs):
        return [(refs[k].at[c], refs[k].at[c]) for k in range(ng)]

    h = _split_start(tag + "_r3", "c", fin, plan4)
    after = yield h["token"]
    full = _split_wait(h, after)
    return [full[k].reshape(grads[k].shape[1:]) for k in range(ng)]


def _xyc():
    return jnp.stack([lax.axis_index("x"), lax.axis_index("y"), lax.axis_index("c")]).astype(jnp.int32)


def _place_shard(name, w, fam, slot0, n_slots):
    n, kk, nn = w.shape
    kt = 256 if kk % 256 == 0 else kk

    def body(scal, w_ref, *rest):
        rest[-1][...] = w_ref[...].astype(BF16)

    in_specs = [pl.BlockSpec((None, kt, nn), lambda t, i, sc: (t, i, 0))]
    args = [_xyc(), w]
    aliases = {}
    if fam is not None:
        in_specs.append(ANY)
        args.append(fam)
        aliases = {2: 0}
    gs = pltpu.PrefetchScalarGridSpec(
        num_scalar_prefetch=1, grid=(n, kk // kt), in_specs=in_specs,
        out_specs=pl.BlockSpec((None, None, kt, nn), lambda t, i, sc: (2 * sc[0] + sc[1], slot0 + t, i, 0)))
    return pl.pallas_call(body, name=name, grid_spec=gs, out_shape=_sds((4, n_slots, kk, nn), BF16),
                          input_output_aliases=aliases, compiler_params=_cp(2))(*args)


def _gather_weights(fams):
    nf = len(fams)
    shapes = [f.shape for f in fams]
    views = [f.reshape(4, 2, -1, f.shape[-1]) for f in fams]
    outs = [_sds(v.shape, v.dtype) for v in views]
    alias = {k: k for k in range(nf)}

    @_plan(nf)
    def plan1(x, y, c, ins, outs_):
        s = 2 * x + y
        return ([(ins[k].at[s, c], outs_[k].at[s, c]) for k in range(nf)], [])

    views = _pair_exchange("gatherw_1", "xy", list(views), outs, alias, plan1)

    @_plan(2 * nf)
    def plan2(x, y, c, ins, outs_):
        shards = [2 * x + y, 2 * _flip(x, 1 - c) + _flip(y, c)]
        return ([(ins[k].at[s, c], outs_[k].at[s, c]) for k in range(nf) for s in shards], [])

    views = _pair_exchange("gatherw_2", "yx", list(views), outs, alias, plan2)

    @_plan(3 * nf)
    def plan3(x, y, c, ins, outs_):
        shards = [2 * (1 - x) + y, 2 * x + (1 - y), 2 * (1 - x) + (1 - y)]
        return ([(ins[k].at[s, c], outs_[k].at[s, c]) for k in range(nf) for s in shards], [])

    views = _pair_exchange("gatherw_c", "c", list(views), outs, alias, plan3)
    return [v.reshape(sh) for v, sh in zip(views, shapes)]


def _sel_add(name, a, a_sel, r, emit_bf16, out_slots=None):
    nr, rows, w = r.shape
    tr = 256 if rows % 256 == 0 else rows

    def body(scal, a_ref, r_ref, *outs):
        s = a_ref[...] + r_ref[...].astype(F32)
        outs[0][...] = s
        if emit_bf16:
            outs[1][...] = s.astype(BF16)

    lead = a.ndim - 2
    a_block = (None,) * lead + (tr, w)
    n_out, o_fn = out_slots if out_slots is not None else (nr, lambda j, sc: j)
    out_shape = [_sds((n_out, rows, w), F32)] + ([_sds((nr, rows, w), BF16)] if emit_bf16 else [])
    out_specs = [pl.BlockSpec((None, tr, w), lambda j, t, sc: (o_fn(j, sc), t, 0))]
    if emit_bf16:
        out_specs.append(pl.BlockSpec((None, tr, w), lambda j, t, sc: (j, t, 0)))
    gs = pltpu.PrefetchScalarGridSpec(
        num_scalar_prefetch=1, grid=(nr, rows // tr),
        in_specs=[pl.BlockSpec(a_block, lambda j, t, sc: tuple(a_sel(j, sc)) + (t, 0)),
                  pl.BlockSpec((None, tr, w), lambda j, t, sc: (j, t, 0))],
        out_specs=out_specs)
    return pl.pallas_call(body, name=name, grid_spec=gs, out_shape=out_shape, compiler_params=_cp(2))(_xyc(), a, r)


def _reduce_scatter(tag, grads):
    ng = len(grads)
    flat = [g.reshape(4, 2, -1, g.shape[-1]) for g in grads]

    @_plan(ng)
    def plan1(x, y, c, ins, outs_):
        return ([(ins[k].at[:, 1 - c], outs_[k]) for k in range(ng)], [])

    r1 = _pair_exchange(tag + "_rs_c", "c", flat, [_sds((4,) + f.shape[2:], F32) for f in flat], {}, plan1)
    p1 = [_sel_add(f"{tag}_add1_{k}", flat[k], lambda j, sc: (j, sc[2]), r1[k], True) for k in range(ng)]

    def sent1(kk, x, y, c):
        return ((1 - c) * kk + c * (1 - x), (1 - c) * (1 - y) + c * kk)

    def kept1(j, sc):
        x, y, c = sc[0], sc[1], sc[2]
        return ((1 - c) * j + c * x, (1 - c) * y + c * j)

    @_plan(2 * ng)
    def plan2(x, y, c, ins, outs_):
        return ([(ins[k].at[sent1(kk, x, y, c)], outs_[k].at[kk]) for k in range(ng) for kk in range(2)], [])

    v1 = [pb.reshape(2, 2, pb.shape[1], pb.shape[2]) for p, pb in p1]
    r2 = _pair_exchange(tag + "_rs_1", "yx", v1, [_sds((2,) + v.shape[2:], BF16) for v in v1], {}, plan2)
    p2 = [_sel_add(f"{tag}_add2_{k}", p1[k][0].reshape(2, 2, p1[k][0].shape[1], p1[k][0].shape[2]), kept1, r2[k], True)
          for k in range(ng)]

    @_plan(ng)
    def plan3(x, y, c, ins, outs_):
        return ([(ins[k].at[(1 - c) * (1 - x) + c * (1 - y)], outs_[k]) for k in range(ng)], [])

    r3 = _pair_exchange(tag + "_rs_2", "xy", [qb for q, qb in p2], [_sds(qb.shape[1:], BF16) for q, qb in p2], {}, plan3)
    fin = [_sel_add(f"{tag}_add3_{k}", p2[k][0], lambda j, sc: ((1 - sc[2]) * sc[0] + sc[2] * sc[1],), r3[k][None], False,
                    out_slots=(2, lambda j, sc: sc[2]))[0] for k in range(ng)]

    @_plan(ng)
    def plan4(x, y, c, ins, outs_):
        return ([(ins[k].at[c], outs_[k].at[c]) for k in range(ng)], [])

    full = _pair_exchange(tag + "_rs_c2", "c", fin, [_sds(f.shape, F32) for f in fin], {k: k for k in range(ng)}, plan4)
    return [full[k].reshape(grads[k].shape[1:]) for k in range(ng)]


def _allreduce8(tag, buf, point=None):
    rows, w = buf.shape
    assert rows % 16 == 0
    step = (lambda a: _tie(a, point(a))) if point is not None else (lambda a: a)
    one = lambda: _plan(1)(lambda x, y, c, ins, outs_: ([(ins[0], outs_[0])], []))
    (got,) = _pair_exchange(f"{tag}_ar_c", "c", [buf], [_sds(buf.shape, F32)], {}, one())
    cur = _ew(f"{tag}_aradd_c", lambda a, b: a + b, [buf, got], [_sds(buf.shape, F32)])[0]
    cur = step(cur).reshape(2, rows // 2, w)
    mine = _plan(1)(lambda x, y, c, ins, outs_: ([(ins[0].at[c], outs_[0])], []))
    (got,) = _pair_exchange(f"{tag}_ar_1", "xy", [cur], [_sds(cur.shape[1:], F32)], {}, mine)
    (h1,) = _sel_add(f"{tag}_aradd_1", cur, lambda j, sc: (sc[2],), got[None], False)
    h1 = step(h1)
    (got,) = _pair_exchange(f"{tag}_ar_2", "yx", [h1[0]], [_sds(h1.shape[1:], F32)], {}, one())
    (h2,) = _sel_add(f"{tag}_aradd_2", h1, lambda j, sc: (0,), got[None], False, out_slots=(2, lambda j, sc: sc[2]))
    swap = _plan(1)(lambda x, y, c, ins, outs_: ([(ins[0].at[c], outs_[0].at[c])], []))
    (full,) = _pair_exchange(f"{tag}_ar_c2", "c", [h2], [_sds(h2.shape, F32)], {0: 0}, swap)
    return full.reshape(rows, w)


def _ew(name, fn, ins, outs):
    rows, w = ins[0].shape
    tr = rows
    for cand in (512, 256, 128, 64, 32, 16, 8):
        if rows % cand == 0 and rows > cand and cand * w * 4 <= (1 << 20):
            tr = cand
            break
    n_in = len(ins)

    def body(*refs):
        vals = fn(*[r[...] for r in refs[:n_in]])
        if not isinstance(vals, (tuple, list)):
            vals = (vals,)
        for o, v in zip(refs[n_in:], vals):
            o[...] = v.astype(o.dtype)

    spec = pl.BlockSpec((tr, w), lambda i: (i, 0))
    return pl.pallas_call(body, name=name, grid=(rows // tr,), in_specs=[spec] * n_in,
                          out_specs=[spec] * len(outs), out_shape=list(outs), compiler_params=_cp(1))(*ins)


def _ew3(name, fn, ins, n_out):
    aa, bb, cc = ins[0].shape
    pad_bytes = (-(-bb // SUBLANES) * SUBLANES) * (-(-cc // LANES) * LANES) * 4
    ta = 1
    for cand in range(aa, 0, -1):
        if aa % cand == 0 and cand * pad_bytes <= (1 << 20):
            ta = cand
            break
    n_in = len(ins)

    def body(*refs):
        vals = fn(*[r[...] for r in refs[:n_in]])
        for o, v in zip(refs[n_in:], vals):
            o[...] = v

    spec = pl.BlockSpec((ta, bb, cc), lambda i: (i, 0, 0))
    return pl.pallas_call(body, name=name, grid=(aa // ta,), in_specs=[spec] * n_in, out_specs=[spec] * n_out,
                          out_shape=[_sds((aa, bb, cc))] * n_out, compiler_params=_cp(1))(*ins)


def _view_for_ew(a):
    if a.ndim == 1:
        return a.reshape(1, -1)
    if a.ndim == 2:
        return a
    if a.shape[-1] % LANES == 0 and a.shape[-2] % SUBLANES == 0:
        return a.reshape(-1, a.shape[-1])
    return a.reshape(-1, a.shape[-2], a.shape[-1])


def _adamw(name, w, g, m, v):
    def fn(w, g, m, v):
        m = ADAM_B1 * m + (1.0 - ADAM_B1) * g
        v = ADAM_B2 * v + (1.0 - ADAM_B2) * (g * g)
        m_hat = m / (1.0 - ADAM_B1 ** ADAM_STEP)
        v_hat = v / (1.0 - ADAM_B2 ** ADAM_STEP)
        delta = -ADAM_LR * (m_hat / (jnp.sqrt(v_hat) + ADAM_EPS) + ADAM_WD * w)
        return delta, m, v

    shp = w.shape
    a = [_view_for_ew(t) for t in (w, g, m, v)]
    if a[0].ndim == 3:
        o = _ew3(name, fn, a, 3)
    else:
        o = _ew(name, fn, a, [_sds(a[0].shape)] * 3)
    return tuple(t.reshape(shp) for t in o)


def _to_perm(a):
    b, ls, d = a.shape
    n = ls // 4
    return a.reshape(b * 4, n, d).swapaxes(0, 1).reshape(n * 8, d)


def _from_perm(p, b, ls):
    n = ls // 4
    return p.reshape(n, b * 4, p.shape[-1]).swapaxes(0, 1).reshape(b, ls, p.shape[-1])


def _pos_embed(rows, dim):
    def sincos(pos, d):
        quarter = d // 2
        omega = POS_TEMP ** (-jnp.arange(quarter, dtype=F32) / quarter)
        ang = pos[:, None] * omega[None, :]
        return jnp.concatenate([jnp.sin(ang), jnp.cos(ang)], axis=-1)

    row_idx = jnp.repeat(jnp.arange(rows), GRID_W).astype(F32)
    col_idx = jnp.tile(jnp.arange(GRID_W), rows).astype(F32)
    return jnp.concatenate([sincos(row_idx, dim // 2), sincos(col_idx, dim // 2)], axis=-1)


def _stream_of(cfg, off_tiles, tile_rows):
    nct = cfg.Tc // tile_rows
    return lambda i: jnp.where(i + off_tiles >= nct, 1, 0)


def _ada_fwd(c_all, w_ada, b_shard):
    nl, d, w = w_ada.shape
    tn = 512 if w % 512 == 0 else w

    def body(c_ref, w_ref, b_ref, o_ref):
        cv = c_ref[...]
        cond = (cv * _sigmoid(cv)).astype(BF16)
        o_ref[...] = _dot(cond, w_ref[...].astype(BF16)) + b_ref[...]

    return pl.pallas_call(
        body, name="ada_fwd", grid=(nl, w // tn),
        in_specs=[_full(c_all.shape), pl.BlockSpec((None, d, tn), lambda l, j: (l, 0, j)),
                  pl.BlockSpec((None, 1, tn), lambda l, j: (l, 0, j))],
        out_specs=pl.BlockSpec((None, c_all.shape[0], tn), lambda l, j: (l, 0, j)),
        out_shape=_sds((nl, c_all.shape[0], w)), compiler_params=_cp(2))(c_all, w_ada, b_shard)


def _ada_bwd(c_all, dmod_shard, w_ada):
    nl, d, w = w_ada.shape
    tn = 512 if w % 512 == 0 else w
    nr = c_all.shape[0]

    def body(c_ref, dm_ref, w_ref, gw_ref, dc_ref):
        j = pl.program_id(0) * (w // tn) + pl.program_id(1)
        cv = c_ref[...]
        cond = (cv * _sigmoid(cv)).astype(BF16)
        dm = dm_ref[...].astype(BF16)
        gw_ref[...] = _dot_tn(cond, dm)
        part = _dot_nt(dm[16:24], w_ref[...].astype(BF16))

        @pl.when(j == 0)
        def _():
            dc_ref[...] = part

        @pl.when(j > 0)
        def _():
            dc_ref[...] += part

    return pl.pallas_call(
        body, name="ada_bwd", grid=(nl, w // tn),
        in_specs=[_full(c_all.shape), pl.BlockSpec((None, nr, tn), lambda l, j: (l, 0, j)),
                  pl.BlockSpec((None, d, tn), lambda l, j: (l, 0, j))],
        out_specs=[pl.BlockSpec((None, d, tn), lambda l, j: (l, 0, j)), _full((8, d))],
        out_shape=[_sds((nl, d, w)), _sds((8, d))], compiler_params=_cp(2))(c_all, dmod_shard, w_ada)


def _disc(lr, li, ldt, br, bi):
    lr = jnp.minimum(lr, LAMBDA_RE_MAX)
    dt = jnp.exp(ldt)
    mag = jnp.exp(lr * dt)
    abr = mag * jnp.cos(li * dt)
    abi = mag * jnp.sin(li * dt)
    den = lr * lr + li * li
    nr = abr - 1.0
    ni = abi
    cr = (nr * lr + ni * li) / den
    ci = (ni * lr - nr * li) / den
    return abr, abi, cr[None] * br - ci[None] * bi, cr[None] * bi + ci[None] * br


def _disc_fwd(lr, li, ldt, br, bi):
    def body(a, b, c, d, e, o1, o2, o3, o4):
        r = _disc(a[...], b[...], c[...], d[...], e[...])
        o1[...], o2[...], o3[...], o4[...] = r

    return pl.pallas_call(body, name="s5_disc_fwd", out_shape=[_sds(lr.shape), _sds(lr.shape), _sds(br.shape), _sds(br.shape)])(
        lr, li, ldt, br, bi)


def _disc_bwd(lr, li, ldt, br, bi, g_abr, g_abi, g_bbr, g_bbi):
    def body(a, b, c, d, e, g1, g2, g3, g4, o1, o2, o3, o4, o5):
        _, vjp = jax.vjp(_disc, a[...], b[...], c[...], d[...], e[...])
        r = vjp((g1[...], g2[...], g3[...], g4[...]))
        o1[...], o2[...], o3[...], o4[...], o5[...] = r

    return pl.pallas_call(
        body, name="s5_disc_bwd",
        out_shape=[_sds(lr.shape), _sds(li.shape), _sds(ldt.shape), _sds(br.shape), _sds(bi.shape)])(
        lr, li, ldt, br, bi, g_abr, g_abi, g_bbr, g_bbi)


def _s5_layouts(cfg, lam_re, lam_im, log_dt, b_re, b_im):
    P, G = cfg.P, cfg.G
    nd = lam_re.shape[0]
    lr = lam_re.transpose(2, 0, 1).reshape(P, nd * G)
    li = lam_im.transpose(2, 0, 1).reshape(P, nd * G)
    ldt = log_dt.reshape(1, nd * G)
    br = b_re.transpose(3, 2, 0, 1).reshape(S5_GROUP, P, nd * G)
    bi = b_im.transpose(3, 2, 0, 1).reshape(S5_GROUP, P, nd * G)
    return lr, li, ldt, br, bi


def _coef_rows(cfg, abr, abi, conj):
    nd = abr.shape[1] // cfg.G

    def one(t):
        return t.reshape(cfg.P, nd, cfg.G).transpose(1, 2, 0).reshape(nd, cfg.NS)
    a = jnp.stack([one(abr), -one(abi) if conj else one(abi)], axis=1)
    return jnp.broadcast_to(a[:, :, None, :], (nd, 2, SUBLANES, cfg.NS))


def _blockdiag_b(cfg, bbr, bbi):
    eye = jnp.eye(8, dtype=F32)
    nd = bbr.shape[2] // cfg.G

    def one(t):
        t = t.reshape(S5_GROUP, cfg.P, nd, cfg.G).transpose(2, 3, 0, 1)
        t = t.reshape(nd, cfg.NO, 8, S5_GROUP, cfg.P)
        return jnp.einsum("dogcp,gh->dogchp", t, eye).reshape(nd, cfg.NO, OCT_CH, OCT_ST)

    return jnp.concatenate([one(bbr), one(bbi)], axis=-1).astype(BF16)


def _blockdiag_c(cfg, c_re, c_im):
    eye = jnp.eye(8, dtype=F32)
    nd = c_re.shape[0]

    def one(t):
        t = t.transpose(0, 1, 3, 2).reshape(nd, cfg.NO, 8, cfg.P, S5_GROUP)
        return jnp.einsum("dogpc,gh->dogphc", t, eye).reshape(nd, cfg.NO, OCT_ST, OCT_CH)

    return jnp.concatenate([one(c_re), -one(c_im)], axis=2).astype(BF16)


def _diag_b(cfg, dbf):
    eye = jnp.eye(8, dtype=F32)
    nd = dbf.shape[0]

    def one(t):
        t = t.reshape(nd, cfg.NO, 8, S5_GROUP, 8, cfg.P)
        t = jnp.einsum("dogchp,gh->dogcp", t, eye).reshape(nd, cfg.G, S5_GROUP, cfg.P)
        return t.transpose(2, 3, 0, 1).reshape(S5_GROUP, cfg.P, nd * cfg.G)

    return one(dbf[..., :OCT_ST]), one(dbf[..., OCT_ST:])


def _diag_c(cfg, dcft):
    eye = jnp.eye(8, dtype=F32)
    nd = dcft.shape[0]

    def one(t):
        t = t.reshape(nd, cfg.NO, 8, S5_GROUP, 8, cfg.P)
        return jnp.einsum("dohcgp,gh->dogcp", t, eye).reshape(nd, cfg.G, S5_GROUP, cfg.P)

    return one(dcft[..., :OCT_ST]), -one(dcft[..., OCT_ST:])


def _recur(buf, st, a_ref, n_oct, ti, rev, store):
    for o in range(0, n_oct, 2):
        cols = [(pl.ds(oo * 2 * OCT_ST, OCT_ST), pl.ds(oo * 2 * OCT_ST + OCT_ST, OCT_ST)) for oo in (o, o + 1)]
        scol = [pl.ds(oo * OCT_ST, OCT_ST) for oo in (o, o + 1)]
        coef = [(a_ref[0, :, sc], a_ref[1, :, sc]) for sc in scol]
        init = (st[0, :, scol[0]], st[1, :, scol[0]], st[0, :, scol[1]], st[1, :, scol[1]])

        def step(i4, carry, cols=cols, coef=coef):
            carry = list(carry)
            for q in range(unroll):
                i = i4 * unroll + q
                r = pl.multiple_of((i + rev * (ti - 1 - 2 * i)) * 8, 8)
                for s in range(2):
                    sr, si = carry[2 * s], carry[2 * s + 1]
                    ar, ai = coef[s]
                    zr = buf[pl.ds(r, 8), cols[s][0]]
                    zi = buf[pl.ds(r, 8), cols[s][1]]
                    nr = ar * sr - ai * si + zr
                    ni = ar * si + ai * sr + zi
                    if store:
                        buf[pl.ds(r, 8), cols[s][0]] = nr
                        buf[pl.ds(r, 8), cols[s][1]] = ni
                    carry[2 * s], carry[2 * s + 1] = nr, ni
            return tuple(carry)

        unroll = 4 if ti % 4 == 0 else 1
        fin = lax.fori_loop(0, ti // unroll, step, init)
        st[0, :, scol[0]] = fin[0]
        st[1, :, scol[0]] = fin[1]
        st[0, :, scol[1]] = fin[2]
        st[1, :, scol[1]] = fin[3]


def _s5_fwd_pass(cfg, name, tok, mod8, col_sh, col_sc, bf, acoef, r0, n, s_init=None, cf=None, y_prev=None):
    D, NO, NS = cfg.D, cfg.NO, cfg.NS
    ti = cfg.ti(n)
    nb = n // ti
    R = 8 * ti
    ob = r0 // R
    second = s_init is not None
    blk = lambda d, j: ob + j + d * (nb - 1 - 2 * j)

    def body(*refs):
        if second:
            tok_ref, mod_ref, bf_ref, a_ref, si_ref, cf_ref, yp_ref, y_ref, ck_ref, fin_ref, zbuf, st = refs
        else:
            tok_ref, mod_ref, bf_ref, a_ref, fin_ref, zbuf, st = refs
        d = pl.program_id(0)
        j = pl.program_id(1)

        @pl.when(j == 0)
        def _():
            if second:
                st[...] = si_ref[...]
            else:
                st[...] = jnp.zeros_like(st)

        if second:
            ck_ref[...] = st[...]
        u = _mod(tok_ref[...], mod_ref[:, col_sh:col_sh + D], mod_ref[:, col_sc:col_sc + D]).astype(BF16)
        for o in range(NO):
            zbuf[:, o * 1024:(o + 1) * 1024] = _dot(u[:, o * OCT_CH:(o + 1) * OCT_CH], bf_ref[o])
        _recur(zbuf, st, a_ref, NO, ti, d, second)
        if second:
            for o in range(NO):
                y_ref[:, o * OCT_CH:(o + 1) * OCT_CH] = _dot(zbuf[:, o * 1024:(o + 1) * 1024].astype(BF16), cf_ref[o])

        @pl.when(j == nb - 1)
        def _():
            fin_ref[...] = st[...]

    st_spec = pl.BlockSpec((None, 2, 8, NS), lambda d, j: (d, 0, 0, 0))
    in_specs = [pl.BlockSpec((R, D), lambda d, j: (blk(d, j), 0)), _full(mod8.shape),
                pl.BlockSpec((None, NO, OCT_CH, 1024), lambda d, j: (d, 0, 0, 0)), st_spec]
    args = [tok, mod8, bf, acoef]
    scratch = [pltpu.VMEM((R, NO * 1024), F32), pltpu.VMEM((2, 8, NS), F32)]
    if not second:
        return pl.pallas_call(body, name=name, grid=(2, nb), in_specs=in_specs, out_specs=st_spec,
                              out_shape=_sds((2, 2, 8, NS)), scratch_shapes=scratch, compiler_params=_cp(2))(*args)
    in_specs += [st_spec, pl.BlockSpec((None, NO, 1024, OCT_CH), lambda d, j: (d, 0, 0, 0))]
    args += [s_init, cf]
    aliases = {}
    if y_prev is not None:
        in_specs.append(ANY)
        args.append(y_prev)
        aliases = {6: 0}
    else:
        in_specs.append(_full((8, LANES)))
        args.append(jnp.zeros((8, LANES), F32))
    out_specs = [pl.BlockSpec((None, R, D), lambda d, j: (d, blk(d, j), 0)),
                 pl.BlockSpec((None, None, 2, 8, NS), lambda d, j: (d, j + d * (nb - 1 - 2 * j), 0, 0, 0)), st_spec]
    out_shape = [_sds((2, cfg.T, D)), _sds((2, nb, 2, 8, NS)), _sds((2, 2, 8, NS))]
    return pl.pallas_call(body, name=name, grid=(2, nb), in_specs=in_specs, out_specs=out_specs, out_shape=out_shape,
                          input_output_aliases=aliases, scratch_shapes=scratch, compiler_params=_cp(2))(*args)


def _s5_chain(cfg, name, fin_local, acoef, n, inc, prev_fin=None):
    NS = cfg.NS
    nsq = int(round(math.log2(n)))
    assert 2 ** nsq == n

    def body(*refs):
        if prev_fin is not None:
            f_ref, a_ref, p_ref, o_ref = refs
        else:
            f_ref, a_ref, o_ref = refs
        for d in range(2):
            pr, pi = a_ref[d, 0, 0:1, :], a_ref[d, 1, 0:1, :]
            for _ in range(nsq):
                pr, pi = pr * pr - pi * pi, 2.0 * pr * pi
            for b in range(2):
                order = [4 * b + k for k in range(4)]
                if not inc[d]:
                    order = order[::-1]
                if prev_fin is not None:
                    last = order[-1]
                    sr, si = p_ref[d, 0, last:last + 1, :], p_ref[d, 1, last:last + 1, :]
                else:
                    sr = jnp.zeros((1, NS), F32)
                    si = jnp.zeros((1, NS), F32)
                for k in order:
                    o_ref[d, 0, k:k + 1, :] = sr
                    o_ref[d, 1, k:k + 1, :] = si
                    fr, fi = f_ref[d, 0, k:k + 1, :], f_ref[d, 1, k:k + 1, :]
                    sr, si = pr * sr - pi * si + fr, pr * si + pi * sr + fi

    args = [fin_local, acoef] + ([prev_fin] if prev_fin is not None else [])
    return pl.pallas_call(body, name=name, out_shape=_sds((2, 2, 8, NS)))(*args)


def _s5_forward(cfg, tag, tok, mod8, col_sh, col_sc, bf, cf, acoef, point=None):
    saved = {}
    fin_prev = None
    y = None
    for ph, (r0, n) in (("c", (0, cfg.nc)), ("x", (cfg.Tc, cfg.nx))):
        m8 = mod8[0 if ph == "c" else 1]
        loc = _s5_fwd_pass(cfg, f"{tag}_scan1{ph}", tok, m8, col_sh, col_sc, bf, acoef, r0, n)
        if point is not None and ph == "x":
            m8 = _tie(m8, point(loc))
        s_in = _s5_chain(cfg, f"{tag}_chain{ph}", loc, acoef, n, (True, False), fin_prev)
        y, ck, fin_prev = _s5_fwd_pass(cfg, f"{tag}_scan2{ph}", tok, m8, col_sh, col_sc, bf, acoef, r0, n, s_in, cf, y)
        saved[ph] = ck
    return y, saved


def _s5_bwd_pass(cfg, name, dy, tok, mod8, col_sh, col_sc, bf, cf, acoef, acoef_adj, r0, n, g_init=None, ck=None,
                 du_prev=None):
    D, NO, NS = cfg.D, cfg.NO, cfg.NS
    ti = cfg.ti(n)
    nb = n // ti
    R = 8 * ti
    ob = r0 // R
    second = g_init is not None
    has_dy = dy is not None
    blk = lambda d, j: ob + j + (1 - d) * (nb - 1 - 2 * j)

    def body(*refs):
        refs = list(refs)
        dy_ref = refs.pop(0) if has_dy else None
        if second:
            (tok_ref, mod_ref, bf_ref, cf_ref, a_ref, aa_ref, gi_ref, ck_ref, dup_ref,
             du_ref, da_ref, dbf_ref, dcf_ref, gfin_ref, qbuf, zbuf, gst, hst) = refs
        else:
            cf_ref, aa_ref, gfin_ref, qbuf, gst = refs
        d = pl.program_id(0)
        j = pl.program_id(1)

        @pl.when(j == 0)
        def _():
            if second:
                gst[...] = gi_ref[...]
                da_ref[...] = jnp.zeros_like(da_ref)
                dbf_ref[...] = jnp.zeros_like(dbf_ref)
                dcf_ref[...] = jnp.zeros_like(dcf_ref)
            else:
                gst[...] = jnp.zeros_like(gst)

        if has_dy:
            dyb = dy_ref[...].astype(BF16)
            for o in range(NO):
                qbuf[:, o * 1024:(o + 1) * 1024] = _dot_nt(dyb[:, o * OCT_CH:(o + 1) * OCT_CH], cf_ref[o])
        else:
            qbuf[...] = jnp.zeros_like(qbuf)
        _recur(qbuf, gst, aa_ref, NO, ti, 1 - d, second)

        if second:
            u = _mod(tok_ref[...], mod_ref[:, col_sh:col_sh + D], mod_ref[:, col_sc:col_sc + D]).astype(BF16)
            for o in range(NO):
                zbuf[:, o * 1024:(o + 1) * 1024] = _dot(u[:, o * OCT_CH:(o + 1) * OCT_CH], bf_ref[o])
            hst[...] = ck_ref[...]
            _recur(zbuf, hst, a_ref, NO, ti, d, True)

            g_off, h_off = (1 - d) * 8, d * 8
            edge = pl.multiple_of(d * (R - 8), 8)
            for o in range(0, NO, 2):
                cols = [(pl.ds(oo * 1024, OCT_ST), pl.ds(oo * 1024 + OCT_ST, OCT_ST)) for oo in (o, o + 1)]
                scol = [pl.ds(oo * OCT_ST, OCT_ST) for oo in (o, o + 1)]
                init = []
                for s in range(2):
                    er, ei = qbuf[pl.ds(edge, 8), cols[s][0]], qbuf[pl.ds(edge, 8), cols[s][1]]
                    kr, ki = ck_ref[0, :, scol[s]], ck_ref[1, :, scol[s]]
                    init += [er * kr + ei * ki, ei * kr - er * ki]

                def stp(i, carry, cols=cols):
                    rg = pl.multiple_of(i * 8 + g_off, 8)
                    rh = pl.multiple_of(i * 8 + h_off, 8)
                    out = []
                    for s in range(2):
                        gr, gi = qbuf[pl.ds(rg, 8), cols[s][0]], qbuf[pl.ds(rg, 8), cols[s][1]]
                        hr, hi = zbuf[pl.ds(rh, 8), cols[s][0]], zbuf[pl.ds(rh, 8), cols[s][1]]
                        out += [carry[2 * s] + (gr * hr + gi * hi), carry[2 * s + 1] + (gi * hr - gr * hi)]
                    return tuple(out)

                fin = lax.fori_loop(0, (ti - 1) // 2, lambda i2, cy, stp=stp: stp(2 * i2 + 1, stp(2 * i2, cy)), tuple(init))
                if (ti - 1) % 2:
                    fin = stp(ti - 2, fin)
                for s in range(2):
                    da_ref[0, :, scol[s]] += fin[2 * s]
                    da_ref[1, :, scol[s]] += fin[2 * s + 1]

            for o in range(NO):
                gb = qbuf[:, o * 1024:(o + 1) * 1024].astype(BF16)
                uo = u[:, o * OCT_CH:(o + 1) * OCT_CH]
                dbf_ref[o] += _dot_tn(uo, gb)
                if has_dy:
                    dcf_ref[o] += _dot_tn(dyb[:, o * OCT_CH:(o + 1) * OCT_CH], zbuf[:, o * 1024:(o + 1) * 1024].astype(BF16))
                du_ref[:, o * OCT_CH:(o + 1) * OCT_CH] = _dot_nt(gb, bf_ref[o])

        @pl.when(j == nb - 1)
        def _():
            gfin_ref[...] = gst[...]

    st_spec = pl.BlockSpec((None, 2, 8, NS), lambda d, j: (d, 0, 0, 0))
    row_spec = pl.BlockSpec((R, D), lambda d, j: (blk(d, j), 0))
    bf_spec = pl.BlockSpec((None, NO, OCT_CH, 1024), lambda d, j: (d, 0, 0, 0))
    cf_spec = pl.BlockSpec((None, NO, 1024, OCT_CH), lambda d, j: (d, 0, 0, 0))
    in_specs, args = [], []
    if has_dy:
        in_specs.append(row_spec)
        args.append(dy)
    if not second:
        in_specs += [cf_spec, st_spec]
        args += [cf, acoef_adj]
        return pl.pallas_call(body, name=name, grid=(2, nb), in_specs=in_specs, out_specs=st_spec,
                              out_shape=_sds((2, 2, 8, NS)),
                              scratch_shapes=[pltpu.VMEM((R, NO * 1024), F32), pltpu.VMEM((2, 8, NS), F32)],
                              compiler_params=_cp(2))(*args)
    ck_spec = pl.BlockSpec((None, None, 2, 8, NS), lambda d, j: (d, j + (1 - d) * (nb - 1 - 2 * j), 0, 0, 0))
    in_specs += [row_spec, _full(mod8.shape), bf_spec, cf_spec, st_spec, st_spec, st_spec, ck_spec]
    args += [tok, mod8, bf, cf, acoef, acoef_adj, g_init, ck]
    n_before = len(args)
    aliases = {}
    if du_prev is not None:
        in_specs.append(ANY)
        args.append(du_prev)
        aliases = {n_before: 0}
    else:
        in_specs.append(_full((8, LANES)))
        args.append(jnp.zeros((8, LANES), F32))
    out_specs = [pl.BlockSpec((None, R, D), lambda d, j: (d, blk(d, j), 0)), st_spec, bf_spec, bf_spec, st_spec]
    out_shape = [_sds((2, cfg.T, D)), _sds((2, 2, 8, NS)), _sds((2, NO, OCT_CH, 1024)), _sds((2, NO, OCT_CH, 1024)),
                 _sds((2, 2, 8, NS))]
    scratch = [pltpu.VMEM((R, NO * 1024), F32), pltpu.VMEM((R, NO * 1024), F32), pltpu.VMEM((2, 8, NS), F32),
               pltpu.VMEM((2, 8, NS), F32)]
    return pl.pallas_call(body, name=name, grid=(2, nb), in_specs=in_specs, out_specs=out_specs, out_shape=out_shape,
                          input_output_aliases=aliases, scratch_shapes=scratch, compiler_params=_cp(2))(*args)


def _s5_backward(cfg, tag, dy, dy_ctx, tok, mod8, col_sh, col_sc, bf, cf, acoef, acoef_adj, saved):
    g_prev = None
    acc = None
    du = None
    for ph, (r0, n) in (("x", (cfg.Tc, cfg.nx)), ("c", (0, cfg.nc))):
        m8 = mod8[0 if ph == "c" else 1]
        dyp = dy if (ph == "x" or dy_ctx) else None
        loc = _s5_bwd_pass(cfg, f"{tag}_adjA{ph}", dyp, tok, m8, col_sh, col_sc, bf, cf, acoef, acoef_adj, r0, n)
        g_in = _s5_chain(cfg, f"{tag}_adjchain{ph}", loc, acoef_adj, n, (False, True), g_prev)
        du, da, dbf, dcf, g_prev = _s5_bwd_pass(cfg, f"{tag}_adjB{ph}", dyp, tok, m8, col_sh, col_sc, bf, cf, acoef,
                                                acoef_adj, r0, n, g_in, saved[ph], du)
        new = (da, dbf, dcf)
        if acc is None:
            acc = new
        else:
            acc = tuple(_ew(f"{tag}_accsum{q}", lambda a, b: a + b, [a.reshape(-1, a.shape[-1]), b.reshape(-1, b.shape[-1])],
                            [_sds((a.size // a.shape[-1], a.shape[-1]))])[0].reshape(a.shape)
                        for q, (a, b) in enumerate(zip(acc, new)))
    return du, acc


def _tok_specs(cfg, rows, width, tile=None):
    tm = tile or cfg.TM
    off = rows[0] // tm
    return pl.BlockSpec((tm, width), lambda i: (i + off, 0)), rows[1] // tm, off


def _mod_spec(cfg, mod8, off):
    st = _stream_of(cfg, off, cfg.TM)
    return pl.BlockSpec((None, 8, mod8.shape[-1]), lambda i: (st(i), 0, 0))


def _wspec(w):
    fam, slot = w
    _, _, kk, nn = fam.shape
    return pl.BlockSpec((4, None, kk, nn), lambda *i: (0, slot, 0, 0), pipeline_mode=pl.Buffered(1))


def _glu_ln(cfg, name, rows, tok, y, mod8, cols, dskip, w, b, gain, bias):
    D, TM = cfg.D, cfg.TM
    csh, csc, cg = cols
    spec, nt, off = _tok_specs(cfg, rows, D)
    spec2, _, _ = _tok_specs(cfg, rows, 2 * D)

    def body(tok_ref, y_ref, mod_ref, ds_ref, w_ref, b_ref, g_ref, bi_ref, x1_ref, r1_ref, mix_ref, zz_ref, zb_ref, yy_ref):
        t = tok_ref[...]
        u = _mod(t, mod_ref[:, csh:csh + D], mod_ref[:, csc:csc + D])
        yy = ds_ref[...] * u + y_ref[0] + y_ref[1]
        zb = _gelu(yy).astype(BF16)
        zz = jnp.concatenate([_dot(zb, w_ref[s]) for s in range(4)], axis=-1) + b_ref[...]
        mix = zz[:, :D] * _sigmoid(zz[:, D:])
        r1 = DN_ALPHA * t + _rowscale(mix, mod_ref[:, cg:cg + D])
        xhat, _ = _ln_stats(r1)
        x1_ref[...] = xhat * g_ref[...] + bi_ref[...]
        r1_ref[...] = r1
        mix_ref[...] = mix.astype(BF16)
        zz_ref[...] = zz.astype(BF16)
        zb_ref[...] = zb
        yy_ref[...] = yy.astype(BF16)

    T = cfg.T
    return pl.pallas_call(
        body, name=name, grid=(nt,),
        in_specs=[spec, pl.BlockSpec((2, TM, D), lambda i: (0, i + off, 0)), _mod_spec(cfg, mod8, off), _full((1, D)),
                  _wspec(w), _full((1, 2 * D)), _full((1, D)), _full((1, D))],
        out_specs=[spec, spec, spec, spec2, spec, spec],
        out_shape=[_sds((T, D)), _sds((T, D)), _sds((T, D), BF16), _sds((T, 2 * D), BF16), _sds((T, D), BF16),
                   _sds((T, D), BF16)],
        compiler_params=_cp(1))(tok, y, mod8, dskip, w[0], b, gain, bias)


def _mlp_ln(cfg, name, rows, x1, mod8, cols, w1, w2, gain, bias):
    D, TM = cfg.D, cfg.TM
    csh, csc, cg = cols
    spec, nt, off = _tok_specs(cfg, rows, D)
    specf, _, _ = _tok_specs(cfg, rows, cfg.F)
    fb = cfg.F // 4

    def body(x_ref, mod_ref, w1_ref, w2_ref, g_ref, bi_ref, x2_ref, r2_ref, out_ref, a_ref, h_ref):
        t = x_ref[...]
        h = _mod(t, mod_ref[:, csh:csh + D], mod_ref[:, csc:csc + D]).astype(BF16)
        out = jnp.zeros((TM, D), F32)
        for s in range(4):
            hid = jnp.maximum(_dot(h, w1_ref[s]), 0.0)
            a = (hid * hid).astype(BF16)
            a_ref[:, s * fb:(s + 1) * fb] = a
            out = out + _dot(a, w2_ref[s])
        r2 = DN_ALPHA * t + _rowscale(out, mod_ref[:, cg:cg + D])
        xhat, _ = _ln_stats(r2)
        x2_ref[...] = xhat * g_ref[...] + bi_ref[...]
        r2_ref[...] = r2
        out_ref[...] = out.astype(BF16)
        h_ref[...] = h

    T = cfg.T
    return pl.pallas_call(
        body, name=name, grid=(nt,),
        in_specs=[spec, _mod_spec(cfg, mod8, off), _wspec(w1), _wspec(w2), _full((1, D)), _full((1, D))],
        out_specs=[spec, spec, spec, specf, spec],
        out_shape=[_sds((T, D)), _sds((T, D)), _sds((T, D), BF16), _sds((T, cfg.F), BF16), _sds((T, D), BF16)],
        compiler_params=_cp(1))(x1, mod8, w1[0], w2[0], gain, bias)


def _pw1_glu(cfg, name, rows, tok, mod8, cols, w, b):
    D, TM = cfg.D, cfg.TM
    csh, csc = cols
    spec, nt, off = _tok_specs(cfg, rows, D)
    spec2, _, _ = _tok_specs(cfg, rows, 2 * D)

    def body(tok_ref, mod_ref, w_ref, b_ref, aa_ref, ag_ref, h_ref):
        h = _mod(tok_ref[...], mod_ref[:, csh:csh + D], mod_ref[:, csc:csc + D]).astype(BF16)
        aa = jnp.concatenate([_dot(h, w_ref[s]) for s in range(4)], axis=-1) + b_ref[...]
        aa_ref[...] = aa.astype(BF16)
        ag_ref[...] = aa[:, :D] * _sigmoid(aa[:, D:])
        h_ref[...] = h

    T = cfg.T
    return pl.pallas_call(
        body, name=name, grid=(nt,),
        in_specs=[spec, _mod_spec(cfg, mod8, off), _wspec(w), _full((1, 2 * D))],
        out_specs=[spec2, spec, spec],
        out_shape=[_sds((T, 2 * D), BF16), _sds((T, D)), _sds((T, D), BF16)], compiler_params=_cp(1))(tok, mod8, w[0], b)


def _halo_maps(cfg, rows):
    TM, HB = cfg.TM, cfg.HB
    off = rows[0] // TM
    nct = cfg.Tc // TM
    ntx = cfg.Tx // TM

    def phase(i):
        t = i + off
        is_x = t >= nct
        first = jnp.where(is_x, nct, 0)
        cnt = jnp.where(is_x, ntx, nct)
        return t, first, cnt

    def prev(i):
        t, first, cnt = phase(i)
        return jnp.where(t == first, 2 * (first + cnt) - 1, 2 * t - 1), 0

    def nxt(i):
        t, first, cnt = phase(i)
        return jnp.where(t == first + cnt - 1, 2 * first, 2 * t + 2), 0

    def edge(i):
        t, first, cnt = phase(i)
        return t == first, t == first + cnt - 1

    return prev, nxt, edge, off


def _halo_fix(prev, nxt, is_first, is_last):
    hb, d = prev.shape
    k = lax.broadcasted_iota(jnp.int32, (hb // 8, 8, d), 1)
    p3 = prev.reshape(hb // 8, 8, d)
    n3 = nxt.reshape(hb // 8, 8, d)
    p_roll = jnp.where((k % 4) == 0, 0.0, pltpu.roll(p3, 1, 1))
    n_roll = jnp.where((k % 4) == 3, 0.0, pltpu.roll(n3, 7, 1))
    p3 = jnp.where(is_first, p_roll, p3)
    n3 = jnp.where(is_last, n_roll, n3)
    return p3.reshape(hb, d), n3.reshape(hb, d)


def _dwconv_ln(cfg, name, rows, ag, w_dw, b_dw, ln_g, ln_b):
    D, TM, HB, KW, half = cfg.D, cfg.TM, cfg.HB, cfg.KW, cfg.half
    prev_map, next_map, edge, off = _halo_maps(cfg, rows)
    spec, nt, _ = _tok_specs(cfg, rows, D)

    def body(cur_ref, prev_ref, next_ref, w_ref, b_ref, g_ref, bi_ref, cv_ref, s_ref, ext):
        i = pl.program_id(0)
        is_first, is_last = edge(i)

        @pl.when(i >= 0)
        def _():
            p, n = _halo_fix(prev_ref[...], next_ref[...], is_first, is_last)
            ext[0:HB, :] = p
            ext[HB:HB + TM, :] = cur_ref[...]
            ext[HB + TM:, :] = n

        acc = jnp.zeros((TM, D), F32)
        for k in range(KW):
            lo = HB + 8 * (k - half)
            acc = acc + w_ref[k:k + 1, :] * ext[lo:lo + TM, :]
        cv_ref[...] = acc + b_ref[...]
        xhat, _ = _ln_stats(cv_ref[...])
        nn = xhat * g_ref[...] + bi_ref[...]
        s_ref[...] = (nn * _sigmoid(nn)).astype(BF16)

    T = cfg.T
    return pl.pallas_call(
        body, name=name, grid=(nt,),
        in_specs=[spec, pl.BlockSpec((HB, D), prev_map), pl.BlockSpec((HB, D), next_map), _full((KW, D)),
                  _full((1, D)), _full((1, D)), _full((1, D))],
        out_specs=[spec, spec], out_shape=[_sds((T, D)), _sds((T, D), BF16)],
        scratch_shapes=[pltpu.VMEM((TM + 2 * HB, D), F32)], compiler_params=_cp(1))(ag, ag, ag, w_dw, b_dw, ln_g, ln_b)


def _pw2_ln(cfg, name, rows, s, tok, mod8, cg, w, b, gain, bias):
    D, TM = cfg.D, cfg.TM
    spec, nt, off = _tok_specs(cfg, rows, D)
    kb = D // 4

    def body(s_ref, tok_ref, mod_ref, w_ref, b_ref, g_ref, bi_ref, x1_ref, r1_ref, mix_ref):
        sv = s_ref[...]
        mix = b_ref[...] + jnp.zeros((TM, D), F32)
        for q in range(4):
            mix = mix + _dot(sv[:, q * kb:(q + 1) * kb], w_ref[q])
        r1 = DN_ALPHA * tok_ref[...] + _rowscale(mix, mod_ref[:, cg:cg + D])
        xhat, _ = _ln_stats(r1)
        x1_ref[...] = xhat * g_ref[...] + bi_ref[...]
        r1_ref[...] = r1
        mix_ref[...] = mix.astype(BF16)

    T = cfg.T
    return pl.pallas_call(
        body, name=name, grid=(nt,),
        in_specs=[spec, spec, _mod_spec(cfg, mod8, off), _wspec(w), _full((1, D)), _full((1, D)), _full((1, D))],
        out_specs=[spec, spec, spec], out_shape=[_sds((T, D)), _sds((T, D)), _sds((T, D), BF16)],
        compiler_params=_cp(1))(s, tok, mod8, w[0], b, gain, bias)


def _loss(cfg, xf, tgt):
    D, TM = cfg.D, cfg.TM
    spec, nt, off = _tok_specs(cfg, cfg.rows(False), D)

    def body(x_ref, t_ref, l_ref, dx_ref, acc):
        i = pl.program_id(0)
        dlt = x_ref[...] - t_ref[...]

        @pl.when(i == 0)
        def _():
            acc[...] = jnp.zeros_like(acc)

        acc[...] += _sum8(dlt * dlt)
        dx_ref[...] = dlt * (1.0 / D)

        @pl.when(i == nt - 1)
        def _():
            l_ref[...] = jnp.zeros((8, LANES), F32) + jnp.sum(acc[...]) * (0.5 / D)

    return pl.pallas_call(
        body, name="loss", grid=(nt,),
        in_specs=[spec, pl.BlockSpec((TM, D), lambda i: (i, 0))],
        out_specs=[_full((8, LANES)), spec], out_shape=[_sds((8, LANES)), _sds((cfg.T, D))],
        scratch_shapes=[pltpu.VMEM((8, D), F32)], compiler_params=_cp(1))(xf, tgt)


def _masked_spec(cfg, rows, width, valid_from_tile):
    tm = cfg.TM
    off = rows[0] // tm
    return pl.BlockSpec((tm, width), lambda i: (jnp.maximum(i + off, valid_from_tile), 0))


def _lnb(cfg, name, rows, dres, dres_ctx_ok, dh, r, aux, gain, mod_gate, cg, mod_next, csc):
    D, TM = cfg.D, cfg.TM
    spec, nt, off = _tok_specs(cfg, rows, D)
    nct = cfg.Tc // TM
    has_dres, has_dh = dres is not None, dh is not None

    def body(*refs):
        refs = list(refs)
        dres_ref = refs.pop(0) if has_dres else None
        dh_ref = refs.pop(0) if has_dh else None
        r_ref, aux_ref, g_ref, mg_ref = refs[:4]
        refs = refs[4:]
        mn_ref = refs.pop(0) if has_dh else None
        dprev_ref, dbr_ref, dgain_ref, dbias_ref, dg_ref, dsc_ref, dsh_ref, acc_g, acc_b = refs
        i = pl.program_id(0)
        t = i + off
        first_of_stream = (i == 0) | (t == nct)
        xhat, rstd = _ln_stats(r_ref[...])
        dy = jnp.zeros((TM, D), F32)
        if has_dres:
            dv = dres_ref[...]
            if not dres_ctx_ok:
                dv = jnp.where(t >= nct, dv, 0.0)
            dy = dy + dv
        if has_dh:
            dhv = dh_ref[...]
            dy = dy + _rowscale(dhv, 1.0 + mn_ref[:, csc:csc + D])
            x_out = xhat * g_ref[0:1, :] + g_ref[1:2, :]
            s_sc, s_sh = _sum8(dhv * x_out), _sum8(dhv)
        else:
            s_sc = s_sh = jnp.zeros((8, D), F32)
        dr = _ln_bwd(dy * g_ref[0:1, :], xhat, rstd)
        s_g = _sum8(dr * aux_ref[...].astype(F32))

        @pl.when(i == 0)
        def _():
            acc_g[...] = jnp.zeros_like(acc_g)
            acc_b[...] = jnp.zeros_like(acc_b)

        acc_g[...] += _sum8(dy * xhat)
        acc_b[...] += _sum8(dy)

        @pl.when(first_of_stream)
        def _():
            dg_ref[...] = s_g
            dsc_ref[...] = s_sc
            dsh_ref[...] = s_sh

        @pl.when(jnp.logical_not(first_of_stream))
        def _():
            dg_ref[...] += s_g
            dsc_ref[...] += s_sc
            dsh_ref[...] += s_sh

        dprev_ref[...] = DN_ALPHA * dr
        dbr_ref[...] = _rowscale(dr, mg_ref[:, cg:cg + D])

        @pl.when(i == nt - 1)
        def _():
            dgain_ref[...] = jnp.sum(acc_g[...], axis=0, keepdims=True)
            dbias_ref[...] = jnp.sum(acc_b[...], axis=0, keepdims=True)

    st = _stream_of(cfg, off, TM)
    in_specs, args = [], []
    if has_dres:
        in_specs.append(spec if dres_ctx_ok else _masked_spec(cfg, rows, D, nct))
        args.append(dres)
    if has_dh:
        in_specs.append(spec)
        args.append(dh)
    in_specs += [spec, spec, _full((2, D)), _mod_spec(cfg, mod_gate, off)]
    args += [r, aux, gain, mod_gate]
    if has_dh:
        in_specs.append(_mod_spec(cfg, mod_next, off))
        args.append(mod_next)
    acc_spec = pl.BlockSpec((None, 8, D), lambda i: (st(i), 0, 0))
    T = cfg.T
    return pl.pallas_call(
        body, name=name, grid=(nt,), in_specs=in_specs,
        out_specs=[spec, spec, _full((1, D)), _full((1, D)), acc_spec, acc_spec, acc_spec],
        out_shape=[_sds((T, D)), _sds((T, D)), _sds((1, D)), _sds((1, D)), _sds((2, 8, D)), _sds((2, 8, D)), _sds((2, 8, D))],
        scratch_shapes=[pltpu.VMEM((8, D), F32), pltpu.VMEM((8, D), F32)], compiler_params=_cp(1))(*args)


def _mlp_bwd(cfg, name, rows, dbr, a, w1, w2):
    D, TM = cfg.D, cfg.TM
    spec, nt, off = _tok_specs(cfg, rows, D)
    specf, _, _ = _tok_specs(cfg, rows, cfg.F)
    fb = cfg.F // 4

    def body(d_ref, a_ref, w1_ref, w2_ref, dh_ref, dhid_ref, dout_ref):
        dout = d_ref[...].astype(BF16)
        dh = jnp.zeros((TM, D), F32)
        for s in range(4):
            da = _dot_nt(dout, w2_ref[s])
            dhid = (da * (2.0 * jnp.sqrt(a_ref[:, s * fb:(s + 1) * fb].astype(F32)))).astype(BF16)
            dhid_ref[:, s * fb:(s + 1) * fb] = dhid
            dh = dh + _dot_nt(dhid, w1_ref[s])
        dh_ref[...] = dh
        dout_ref[...] = dout

    T = cfg.T
    return pl.pallas_call(
        body, name=name, grid=(nt,),
        in_specs=[spec, specf, _wspec(w1), _wspec(w2)],
        out_specs=[spec, specf, spec],
        out_shape=[_sds((T, D)), _sds((T, cfg.F), BF16), _sds((T, D), BF16)], compiler_params=_cp(1))(dbr, a, w1[0], w2[0])


def _wgrad(cfg, name, rows, a, b, mode, fam, slot):
    tw = cfg.TW
    off = rows[0] // tw
    ntile = rows[1] // tw
    tps = next(q for q in (4, 3, 2, 1) if ntile % q == 0)
    nt = ntile // tps
    fresh = not hasattr(fam, "dtype")
    fam_shape = tuple(fam) if fresh else fam.shape
    _, n, kk, nn = fam_shape

    def body(*refs):
        a_refs, b_refs, o_ref = refs[:tps], refs[tps:2 * tps], refs[-1]
        t = pl.program_id(1)
        part = _dot_tn(a_refs[0][...], b_refs[0][...])
        for q in range(1, tps):
            part = part + _dot_tn(a_refs[q][...], b_refs[q][...])

        @pl.when(t == 0)
        def _():
            o_ref[...] = part

        @pl.when(t > 0)
        def _():
            o_ref[...] += part

    def row(q):
        return lambda s, t: t * tps + q + off

    if mode == "col":
        a_specs = [pl.BlockSpec((tw, kk), lambda s, t, r=row(q): (r(s, t), 0)) for q in range(tps)]
        b_specs = [pl.BlockSpec((tw, nn), lambda s, t, r=row(q): (r(s, t), s)) for q in range(tps)]
    else:
        a_specs = [pl.BlockSpec((tw, kk), lambda s, t, r=row(q): (r(s, t), s)) for q in range(tps)]
        b_specs = [pl.BlockSpec((tw, nn), lambda s, t, r=row(q): (r(s, t), 0)) for q in range(tps)]
    out_spec = pl.BlockSpec((None, None, kk, nn), lambda s, t: (s, slot, 0, 0))
    ins = [a] * tps + [b] * tps
    if fresh:
        return pl.pallas_call(body, name=name, grid=(4, nt), in_specs=a_specs + b_specs, out_specs=out_spec,
                              out_shape=_sds(fam_shape), compiler_params=_cp(2))(*ins)
    return pl.pallas_call(body, name=name, grid=(4, nt), in_specs=a_specs + b_specs + [ANY], out_specs=out_spec,
                          out_shape=_sds(fam_shape), input_output_aliases={2 * tps: 0}, compiler_params=_cp(2))(*ins, fam)


def _glu_bwd(cfg, name, rows, dmix, pre, w, yy=None):
    D, TM = cfg.D, cfg.TM
    spec, nt, off = _tok_specs(cfg, rows, D)
    spec2, _, _ = _tok_specs(cfg, rows, 2 * D)
    hw = w[0].shape[-1]
    has_y = yy is not None

    def body(*refs):
        refs = list(refs)
        d_ref, p_ref, w_ref = refs[:3]
        y_ref = refs[3] if has_y else None
        dz_ref, dp_ref, db_ref, acc = refs[-4:]
        i = pl.program_id(0)
        dm = d_ref[...]
        po, pg = p_ref[:, :D].astype(F32), p_ref[:, D:].astype(F32)
        sg = _sigmoid(pg)
        dpre = jnp.concatenate([dm * sg, dm * po * sg * (1.0 - sg)], axis=-1)

        @pl.when(i == 0)
        def _():
            acc[...] = jnp.zeros_like(acc)

        acc[...] += _sum8(dpre)
        dpb = dpre.astype(BF16)
        dz = jnp.zeros((TM, D), F32)
        for s in range(4):
            dz = dz + _dot_nt(dpb[:, s * hw:(s + 1) * hw], w_ref[s])
        if has_y:
            dz = dz * _gelu_grad(y_ref[...].astype(F32))
        dz_ref[...] = dz
        dp_ref[...] = dpb

        @pl.when(i == nt - 1)
        def _():
            db_ref[...] = jnp.sum(acc[...], axis=0, keepdims=True)

    T = cfg.T
    in_specs = [spec, spec2, _wspec(w)] + ([spec] if has_y else [])
    args = [dmix, pre, w[0]] + ([yy] if has_y else [])
    return pl.pallas_call(
        body, name=name, grid=(nt,), in_specs=in_specs, out_specs=[spec, spec2, _full((1, 2 * D))],
        out_shape=[_sds((T, D)), _sds((T, 2 * D), BF16), _sds((1, 2 * D))],
        scratch_shapes=[pltpu.VMEM((8, 2 * D), F32)], compiler_params=_cp(1))(*args)


def _s5_du(cfg, name, rows, du, dy, dy_from_tile, tok, mod8, cols, dskip):
    D, TM = cfg.D, cfg.TM
    csh, csc = cols
    spec, nt, off = _tok_specs(cfg, rows, D)

    def body(du_ref, dy_ref, tok_ref, mod_ref, ds_ref, dh_ref, dd_ref, acc):
        i = pl.program_id(0)
        dyv = jnp.where(i + off >= dy_from_tile, dy_ref[...], 0.0)
        u = _mod(tok_ref[...], mod_ref[:, csh:csh + D], mod_ref[:, csc:csc + D])
        dh_ref[...] = du_ref[0] + du_ref[1] + ds_ref[...] * dyv

        @pl.when(i == 0)
        def _():
            acc[...] = jnp.zeros_like(acc)

        acc[...] += _sum8(dyv * u)

        @pl.when(i == nt - 1)
        def _():
            dd_ref[...] = jnp.sum(acc[...], axis=0, keepdims=True)

    T = cfg.T
    return pl.pallas_call(
        body, name=name, grid=(nt,),
        in_specs=[pl.BlockSpec((2, TM, D), lambda i: (0, i + off, 0)), _masked_spec(cfg, rows, D, dy_from_tile), spec,
                  _mod_spec(cfg, mod8, off), _full((1, D))],
        out_specs=[spec, _full((1, D))], out_shape=[_sds((T, D)), _sds((1, D))],
        scratch_shapes=[pltpu.VMEM((8, D), F32)], compiler_params=_cp(1))(du, dy, tok, mod8, dskip)


def _pw2_bwd(cfg, name, rows, dmix, cv, w, ln_g, ln_b):
    D, TM = cfg.D, cfg.TM
    spec, nt, off = _tok_specs(cfg, rows, D)
    kb = D // 4

    def body(d_ref, cv_ref, w_ref, g_ref, b_ref, dcv_ref, dmb_ref, sums_ref, acc):
        i = pl.program_id(0)
        dm = d_ref[...]
        dmb = dm.astype(BF16)
        ds = jnp.concatenate([_dot_nt(dmb, w_ref[q]) for q in range(4)], axis=-1)
        xhat, rstd = _ln_stats(cv_ref[...])
        nn = xhat * g_ref[...] + b_ref[...]
        sg = _sigmoid(nn)
        dn = ds * (sg * (1.0 + nn * (1.0 - sg)))
        dcv = _ln_bwd(dn * g_ref[...], xhat, rstd)

        @pl.when(i == 0)
        def _():
            acc[...] = jnp.zeros_like(acc)

        acc[0] += _sum8(dn * xhat)
        acc[1] += _sum8(dn)
        acc[2] += _sum8(dcv)
        acc[3] += _sum8(dm)
        dcv_ref[...] = dcv
        dmb_ref[...] = dmb

        @pl.when(i == nt - 1)
        def _():
            for q in range(4):
                sums_ref[q:q + 1, :] = jnp.sum(acc[q], axis=0, keepdims=True)

    T = cfg.T
    return pl.pallas_call(
        body, name=name, grid=(nt,),
        in_specs=[spec, spec, _wspec(w), _full((1, D)), _full((1, D))],
        out_specs=[spec, spec, _full((4, D))], out_shape=[_sds((T, D)), _sds((T, D), BF16), _sds((4, D))],
        scratch_shapes=[pltpu.VMEM((4, 8, D), F32)], compiler_params=_cp(1))(dmix, cv, w[0], ln_g, ln_b)


def _dwconv_bwd(cfg, name, rows, dcv, ag, w_dw):
    D, TM, HB, KW, half = cfg.D, cfg.TM, cfg.HB, cfg.KW, cfg.half
    prev_map, next_map, edge, off = _halo_maps(cfg, rows)
    spec, nt, _ = _tok_specs(cfg, rows, D)

    def body(dc_ref, dp_ref, dn_ref, ac_ref, ap_ref, an_ref, w_ref, dag_ref, dw_ref, extd, exta, acc):
        i = pl.program_id(0)
        is_first, is_last = edge(i)

        @pl.when(i >= 0)
        def _():
            p, n = _halo_fix(dp_ref[...], dn_ref[...], is_first, is_last)
            extd[0:HB, :] = p
            extd[HB:HB + TM, :] = dc_ref[...]
            extd[HB + TM:, :] = n
            p, n = _halo_fix(ap_ref[...], an_ref[...], is_first, is_last)
            exta[0:HB, :] = p
            exta[HB:HB + TM, :] = ac_ref[...]
            exta[HB + TM:, :] = n

        @pl.when(i == 0)
        def _():
            acc[...] = jnp.zeros_like(acc)

        cr = min(CONV_ROWS, TM)
        for r0 in range(0, TM, cr):
            for lc in range(D // LANES):
                ls = pl.ds(lc * LANES, LANES)
                dcur = dc_ref[r0:r0 + cr, ls]
                dag = jnp.zeros((cr, LANES), F32)
                for k in range(KW):
                    lo = r0 + HB + 8 * (half - k)
                    la = r0 + HB + 8 * (k - half)
                    dag = dag + w_ref[k:k + 1, ls] * extd[lo:lo + cr, ls]
                    acc[k, :, ls] += _sum8(dcur * exta[la:la + cr, ls])
                dag_ref[r0:r0 + cr, ls] = dag

        @pl.when(i == nt - 1)
        def _():
            for k in range(KW):
                dw_ref[k:k + 1, :] = jnp.sum(acc[k], axis=0, keepdims=True)

    T = cfg.T
    hp, hn = pl.BlockSpec((HB, D), prev_map), pl.BlockSpec((HB, D), next_map)
    return pl.pallas_call(
        body, name=name, grid=(nt,), in_specs=[spec, hp, hn, spec, hp, hn, _full((KW, D))],
        out_specs=[spec, _full((KW, D))], out_shape=[_sds((T, D)), _sds((KW, D))],
        scratch_shapes=[pltpu.VMEM((TM + 2 * HB, D), F32), pltpu.VMEM((TM + 2 * HB, D), F32), pltpu.VMEM((KW, 8, D), F32)],
        compiler_params=_cp(1))(dcv, dcv, dcv, ag, ag, ag, w_dw)


def _input_bwd(cfg, dres, dh, tok0, mod8, csc):
    D, TM = cfg.D, cfg.TM
    rows = cfg.rows(True)
    spec, nt, off = _tok_specs(cfg, rows, D)
    nct = cfg.Tc // TM
    st = _stream_of(cfg, off, TM)

    def body(dr_ref, dh_ref, t_ref, mod_ref, gx_ref, dsc_ref, dsh_ref):
        i = pl.program_id(0)
        dhv = dh_ref[...]
        gx_ref[...] = dr_ref[...] + _rowscale(dhv, 1.0 + mod_ref[:, csc:csc + D])
        first = (i == 0) | (i == nct)
        s_sc, s_sh = _sum8(dhv * t_ref[...]), _sum8(dhv)

        @pl.when(first)
        def _():
            dsc_ref[...] = s_sc
            dsh_ref[...] = s_sh

        @pl.when(jnp.logical_not(first))
        def _():
            dsc_ref[...] += s_sc
            dsh_ref[...] += s_sh

    acc_spec = pl.BlockSpec((None, 8, D), lambda i: (st(i), 0, 0))
    return pl.pallas_call(
        body, name="input_bwd", grid=(nt,), in_specs=[spec, spec, spec, _mod_spec(cfg, mod8, off)],
        out_specs=[spec, acc_spec, acc_spec], out_shape=[_sds((cfg.T, D)), _sds((2, 8, D)), _sds((2, 8, D))],
        compiler_params=_cp(1))(dres, dh, tok0, mod8)


def _dmod_rows(dmod8):
    nl, _, _, w = dmod8.shape

    def body(d_ref, o_ref):
        xs = d_ref[1]
        cs = d_ref[0]
        o_ref[...] = jnp.zeros((8, w), F32)
        o_ref[0:1, :] = jnp.sum(xs[0:4], axis=0, keepdims=True)
        o_ref[1:2, :] = jnp.sum(xs[4:8], axis=0, keepdims=True)
        o_ref[2:3, :] = jnp.sum(cs, axis=0, keepdims=True)

    return pl.pallas_call(body, name="dmod_rows", grid=(nl,),
                          in_specs=[pl.BlockSpec((None, 2, 8, w), lambda l: (l, 0, 0, 0))],
                          out_specs=pl.BlockSpec((None, 8, w), lambda l: (l, 0, 0)), out_shape=_sds((nl, 8, w)),
                          compiler_params=_cp(1))(dmod8)


def _x_only(acc):
    return jnp.concatenate([jnp.zeros_like(acc[:1]), acc[1:]], axis=0)


def _pack(parts):
    bufs, meta, off = [], [], 0
    for p in parts:
        n = p.size
        rows = -(-n // (8 * LANES)) * 8
        flat = p.reshape(-1).astype(F32)
        if rows * LANES != n:
            flat = jnp.pad(flat, (0, rows * LANES - n))
        flat = flat.reshape(rows, LANES)
        bufs.append(flat)
        meta.append((off, rows, p.shape))
        off += rows
    if off % 16:
        bufs.append(jnp.zeros((8, LANES), F32))
    return jnp.concatenate(bufs, axis=0), meta


def _unpack(buf, meta):
    out = []
    for off, rows, shape in meta:
        n = 1
        for s in shape:
            n *= s
        out.append(buf[off:off + rows].reshape(-1)[:n].reshape(shape))
    return out


def kernel(x, c, ctx, c_ctx, w_ada, b_ada, ln_gain, ln_bias, s5_lam_re, s5_lam_im, s5_log_dt, s5_b_re, s5_b_im, s5_c_re, s5_c_im, s5_d, s5_w_glu, s5_b_glu, cv_w_pw1, cv_b_pw1, cv_w_dw, cv_b_dw, cv_ln_g, cv_ln_b, cv_w_pw2, cv_b_pw2, mlp_w1, mlp_w2, loss_target, m_c_ctx, m_w_ada, m_b_ada, m_ln_gain, m_ln_bias, m_s5_lam_re, m_s5_lam_im, m_s5_log_dt, m_s5_b_re, m_s5_b_im, m_s5_c_re, m_s5_c_im, m_s5_d, m_s5_w_glu, m_s5_b_glu, m_cv_w_pw1, m_cv_b_pw1, m_cv_w_dw, m_cv_b_dw, m_cv_ln_g, m_cv_ln_b, m_cv_w_pw2, m_cv_b_pw2, m_mlp_w1, m_mlp_w2, v_c_ctx, v_w_ada, v_b_ada, v_ln_gain, v_ln_bias, v_s5_lam_re, v_s5_lam_im, v_s5_log_dt, v_s5_b_re, v_s5_b_im, v_s5_c_re, v_s5_c_im, v_s5_d, v_s5_w_glu, v_s5_b_glu, v_cv_w_pw1, v_cv_b_pw1, v_cv_w_dw, v_cv_b_dw, v_cv_ln_g, v_cv_ln_b, v_cv_w_pw2, v_cv_b_pw2, v_mlp_w1, v_mlp_w2):
    cfg = _Cfg(x, ctx, mlp_w1, cv_w_dw)
    D, T, Tc, Tx, B = cfg.D, cfg.T, cfg.Tc, cfg.Tx, cfg.B
    ax, ay, ac = lax.axis_index("x"), lax.axis_index("y"), lax.axis_index("c")
    shard = 2 * ax + ay
    dev = 4 * ax + 2 * ay + ac
    Ds = D // 4
    Wa = w_ada.shape[2]

    c_pad = jnp.concatenate([c, jnp.zeros((8 - B, D), F32)], axis=0)
    c_gath = _allgather8("gather_c", c_pad).reshape(8, 8, D)[:, :B].reshape(8 * B, D)
    c_all = jnp.concatenate([c_gath, c_ctx[None], jnp.zeros((7, D), F32)], axis=0)
    b_sh = lax.dynamic_slice_in_dim(b_ada, shard * Wa, Wa, axis=1)[:, None, :]
    mod_sh = _ada_fwd(c_all, w_ada, b_sh)
    mod_g = _allgather8("gather_mod", mod_sh.reshape(DEPTH * 24, Wa)).reshape(4, 2, DEPTH, 24, Wa)[:, 0]
    mods = mod_g.transpose(1, 2, 0, 3).reshape(DEPTH, 24, 4 * Wa)
    mine = lax.dynamic_slice_in_dim(mods, B * dev, B, axis=1)
    mod8 = jnp.stack([jnp.broadcast_to(mods[:, 16:17], (DEPTH, 8, 6 * D)), jnp.repeat(mine, 4, axis=1)], axis=1)
    SH1, SC1, G1, SH2, SC2, G2 = (k * D for k in range(6))

    small_parts = [ln_gain.reshape(-1, Ds), ln_bias.reshape(-1, Ds), cv_b_pw1.reshape(-1, Ds), cv_w_dw.reshape(-1, Ds),
                   cv_b_dw, cv_ln_g, cv_ln_b, cv_b_pw2]
    small_rows = [p.shape[0] for p in small_parts]
    sm = jnp.concatenate(small_parts, axis=0)
    pad_r = -sm.shape[0] % 8
    sm = jnp.pad(sm, ((0, pad_r), (0, 0)))
    sm_g = _allgather8("gather_small", sm).reshape(4, 2, sm.shape[0], Ds)[:, 0]
    pieces, o = [], 0
    for nr in small_rows:
        pieces.append(sm_g[:, o:o + nr])
        o += nr

    def unshard(p, lead):
        return p.reshape((4,) + lead + (Ds,)).transpose(tuple(range(1, len(lead) + 1)) + (0, len(lead) + 1)).reshape(lead + (4 * Ds,))

    ln_gain_f = unshard(pieces[0], (DEPTH, 2))
    ln_bias_f = unshard(pieces[1], (DEPTH, 2))
    nconv = cv_w_dw.shape[0]
    b_pw1_f = pieces[2].reshape(4, nconv, 2 * D // 4).transpose(1, 0, 2).reshape(nconv, 2 * D)
    w_dw_f = unshard(pieces[3], (nconv, cfg.KW))
    b_dw_f, cvg_f, cvb_f, b_pw2_f = (unshard(p, (nconv,)) for p in pieces[4:8])

    ns5 = s5_w_glu.shape[0]
    assert mlp_w1.shape[1:] == mlp_w2.shape[1:]
    fam_a = _place_shard("place_w1", mlp_w1, None, 0, 2 * DEPTH)
    fam_a = _place_shard("place_w2", mlp_w2, fam_a, DEPTH, 2 * DEPTH)
    fam_b = _place_shard("place_wglu", s5_w_glu, None, 0, ns5 + nconv)
    fam_b = _place_shard("place_wpw1", cv_w_pw1, fam_b, ns5, ns5 + nconv)
    fam_c = _place_shard("place_wpw2", cv_w_pw2, None, 0, nconv)
    ov = _Overlap()
    gather_tokens = ov.add("gather_b", _gather_gen("gatherb", [fam_b])) + ov.add("gather", _gather_gen("gatherw", [fam_a, fam_c]))

    pos = jnp.broadcast_to(_pos_embed(cfg.L // GRID_W, D)[None], (B, cfg.L, D))
    tok_in = jnp.concatenate([_to_perm(ctx), _to_perm(x)], axis=0)
    pos_in = jnp.concatenate([jnp.zeros((Tc, D), F32), _to_perm(pos)], axis=0)
    tok0 = _ew("add_pos", lambda a, b: a + b, [tok_in, pos_in], [_sds((T, D))])[0]
    mod8 = _tie(mod8, gather_tokens)
    tgt = _to_perm(loss_target)

    def lead(t):
        return t.reshape((2 * ns5,) + t.shape[2:])

    s5_lay = _s5_layouts(cfg, lead(s5_lam_re), lead(s5_lam_im), lead(s5_log_dt), lead(s5_b_re), lead(s5_b_im))
    abr, abi, bbr, bbi = _disc_fwd(*s5_lay)
    acoef_all, acoef_adj_all = _coef_rows(cfg, abr, abi, False), _coef_rows(cfg, abr, abi, True)
    bf_all, cf_all = _blockdiag_b(cfg, bbr, bbi), _blockdiag_c(cfg, lead(s5_c_re), lead(s5_c_im))
    s5p = [dict(acoef=acoef_all[2 * j:2 * j + 2], acoef_adj=acoef_adj_all[2 * j:2 * j + 2], bf=bf_all[2 * j:2 * j + 2],
                cf=cf_all[2 * j:2 * j + 2]) for j in range(ns5)]

    kinds = ["s5" if i % 2 == 0 else "conv" for i in range(DEPTH)]
    tok = tok0
    saved = []
    s5_j = cv_j = 0
    for i in range(DEPTH):
        later_s5 = any(k == "s5" for k in kinds[i + 1:])
        rows = cfg.rows(later_s5)
        m8 = mod8[i]
        sv = dict(tok=tok, rows=rows, kind=kinds[i])
        g0, b0 = ln_gain_f[i, 0][None], ln_bias_f[i, 0][None]
        g1, b1 = ln_gain_f[i, 1][None], ln_bias_f[i, 1][None]
        if kinds[i] == "s5":
            j = s5_j
            s5_j += 1
            p = s5p[j]
            y, ck = _s5_forward(cfg, f"l{i}", tok, m8, SH1, SC1, p["bf"], p["cf"], p["acoef"], ov.point if i == 0 else None)
            m8g = m8
            if i == 0:
                m8g = _tie(m8, ov.point(y))
                (wb_full,) = ov.finish("gather_b", y)
            wg = (wb_full, j)
            x1, r1, mix, zz, zb, yy = _glu_ln(cfg, f"l{i}_glu", rows, tok, y, m8g, (SH1, SC1, G1), s5_d[j][None], wg,
                                              s5_b_glu[j][None], g0, b0)
            sv.update(j=j, ck=ck, zz=zz, zb=zb, yy=yy, wg=wg)
            if i == 0:
                wa_full, wc_full = ov.finish("gather", x1)
        else:
            j = cv_j
            cv_j += 1
            w1c, w2c = (wb_full, ns5 + j), (wc_full, j)
            aa, ag, hb = _pw1_glu(cfg, f"l{i}_pw1", rows, tok, m8, (SH1, SC1), w1c, b_pw1_f[j][None])
            cvv, sb = _dwconv_ln(cfg, f"l{i}_dw", rows, ag, w_dw_f[j], b_dw_f[j][None], cvg_f[j][None], cvb_f[j][None])
            x1, r1, mix = _pw2_ln(cfg, f"l{i}_pw2", rows, sb, tok, m8, G1, w2c, b_pw2_f[j][None], g0, b0)
            sv.update(j=j, aa=aa, ag=ag, hb=hb, cvv=cvv, sb=sb, w1c=w1c, w2c=w2c)
        w1m, w2m = (wa_full, i), (wa_full, DEPTH + i)
        x2, r2, mout, am, hm = _mlp_ln(cfg, f"l{i}_mlp", rows, x1, m8, (SH2, SC2, G2), w1m, w2m, g1, b1)
        sv.update(r1=r1, mix=mix, x1=x1, r2=r2, mout=mout, am=am, hm=hm, w1m=w1m, w2m=w2m, g0=g0, b0=b0, g1=g1, b1=b1)
        saved.append(sv)
        tok = x2

    loss8, dxf = _loss(cfg, tok, tgt)
    loss = lax.psum(loss8[0, 0], ("x", "y", "c"))

    dmod8 = [None] * DEPTH
    g_ln_gain = [[None, None] for _ in range(DEPTH)]
    g_ln_bias = [[None, None] for _ in range(DEPTH)]
    g_s5 = [None] * ns5
    g_cv = [None] * nconv
    dres, dh = dxf, None
    pend = []
    for i in reversed(range(DEPTH)):
        sv = saved[i]
        rows = sv["rows"]
        m8 = mod8[i]
        nxt_m8 = mod8[i + 1] if i + 1 < DEPTH else None
        ctx_ok = True if i + 1 >= DEPTH else (saved[i + 1]["rows"][0] == 0)
        if rows[0] != 0:
            ctx_ok = True
        dprev, dbr, dgn, dbs, dg2, dsc_n, dsh_n = _lnb(
            cfg, f"l{i}_lnb2", rows, dres, ctx_ok, dh, sv["r2"], sv["mout"],
            _tie(jnp.concatenate([sv["g1"], sv["b1"]], 0), pend), m8, G2, nxt_m8, SC1)
        if rows[0] != 0:
            dg2, dsc_n, dsh_n = (_x_only(t) for t in (dg2, dsc_n, dsh_n))
        g_ln_gain[i][1], g_ln_bias[i][1] = dgn[0], dbs[0]
        if i + 1 < DEPTH:
            dmod8[i + 1]["sc1"], dmod8[i + 1]["sh1"] = dsc_n, dsh_n
        dmod8[i] = dict(g2=dg2)
        dh2, dhid, dout = _mlp_bwd(cfg, f"l{i}_mlpb", rows, dbr, sv["am"], sv["w1m"], sv["w2m"])
        pend = ov.point(dh2)
        ga = _wgrad(cfg, f"l{i}_gw1", rows, sv["hm"], dhid, "col", (4, 2, D, cfg.F // 4), 0)
        ga = _wgrad(cfg, f"l{i}_gw2", rows, sv["am"], dout, "row", ga, 1)
        dprev1, dbr1, dgn, dbs, dg1, dsc2, dsh2 = _lnb(
            cfg, f"l{i}_lnb1", rows, dprev, True, dh2, sv["r1"], sv["mix"],
            _tie(jnp.concatenate([sv["g0"], sv["b0"]], 0), pend), m8, G1, m8, SC2)
        if rows[0] != 0:
            dg1, dsc2, dsh2 = (_x_only(t) for t in (dg1, dsc2, dsh2))
        g_ln_gain[i][0], g_ln_bias[i][0] = dgn[0], dbs[0]
        dmod8[i].update(g1=dg1, sc2=dsc2, sh2=dsh2)
        j = sv["j"]
        if sv["kind"] == "s5":
            p = s5p[j]
            dyy, dzz, dbglu = _glu_bwd(cfg, f"l{i}_glub", rows, dbr1, sv["zz"], sv["wg"], sv["yy"])
            pend = ov.point(dyy)
            gb = _wgrad(cfg, f"l{i}_gwg", rows, sv["zb"], dzz, "col", (4, 1, D, D // 2), 0)
            du, (da, dbf, dcf) = _s5_backward(cfg, f"l{i}", dyy, rows[0] == 0, sv["tok"], m8, SH1, SC1, p["bf"], p["cf"],
                                              p["acoef"], _tie(p["acoef_adj"], pend), sv["ck"])
            dh, dds = _s5_du(cfg, f"l{i}_du", cfg.rows(True), du, dyy, rows[0] // cfg.TM, sv["tok"], m8, (SH1, SC1),
                             s5_d[j][None])
            g_s5[j] = dict(da=da, dbf=dbf, dcf=dcf, dd=dds[0], dbglu=dbglu[0])
            layer_grads = [ga, gb]
        else:
            dcv, dmb, sums = _pw2_bwd(cfg, f"l{i}_pw2b", rows, dbr1, sv["cvv"], sv["w2c"], cvg_f[j][None], cvb_f[j][None])
            pend = ov.point(dcv)
            gc = _wgrad(cfg, f"l{i}_gwp2", rows, sv["sb"], dmb, "row", (4, 1, D // 4, D), 0)
            dag, dwdw = _dwconv_bwd(cfg, f"l{i}_dwb", rows, dcv, sv["ag"], _tie(w_dw_f[j], pend))
            dh, daa, dbpw1 = _glu_bwd(cfg, f"l{i}_pw1b", rows, dag, sv["aa"], sv["w1c"])
            gb = _wgrad(cfg, f"l{i}_gwp1", rows, sv["hb"], daa, "col", (4, 1, D, D // 2), 0)
            g_cv[j] = dict(ln_g=sums[0], ln_b=sums[1], b_dw=sums[2], b_pw2=sums[3], w_dw=dwdw, b_pw1=dbpw1[0])
            layer_grads = [ga, gb, gc]
        dres = dprev1
        pend = ov.point(dh) + ov.add(f"rs{i}", _reduce_scatter_gen(f"gw{i}", layer_grads))
    gx_perm, dsc0, dsh0 = _input_bwd(cfg, dres, dh, tok0, _tie(mod8[0], pend), SC1)
    dmod8[0]["sc1"], dmod8[0]["sh1"] = dsc0, dsh0
    grad_x = _from_perm(gx_perm[Tc:], B, cfg.L)

    zero28 = jnp.zeros((2, 8, D), F32)
    dm8 = jnp.stack([jnp.concatenate([dmod8[i].get(k, zero28) for k in ("sh1", "sc1", "g1", "sh2", "sc2", "g2")], axis=-1)
                     for i in range(DEPTH)])
    dm_rows = _dmod_rows(dm8)
    dm_tab = jnp.zeros((DEPTH, 24, 6 * D), F32)
    dm_tab = lax.dynamic_update_slice_in_dim(dm_tab, dm_rows[:, 0:B], B * dev, axis=1)
    dm_tab = lax.dynamic_update_slice_in_dim(dm_tab, dm_rows[:, 2:3], 16, axis=1)

    dbbr, dbbi = _diag_b(cfg, jnp.concatenate([g["dbf"] for g in g_s5], axis=0))
    dcr, dci = _diag_c(cfg, jnp.concatenate([g["dcf"] for g in g_s5], axis=0))
    eye_parts = [jnp.concatenate([g["da"] for g in g_s5], axis=0), dbbr, dbbi, dcr, dci,
                 jnp.stack([g["dd"] for g in g_s5]), jnp.stack([g["dbglu"] for g in g_s5])]
    for j in range(nconv):
        g = g_cv[j]
        eye_parts += [g["ln_g"], g["ln_b"], g["b_dw"], g["b_pw2"], g["w_dw"], g["b_pw1"]]
    eye_parts += [jnp.stack([jnp.stack(r) for r in g_ln_gain]), jnp.stack([jnp.stack(r) for r in g_ln_bias]), dm_tab]
    buf, meta = _pack(eye_parts)
    buf = _tie(buf, ov.point(gx_perm))
    red_buf = _allreduce8("small", buf, ov.point)
    reduced = [ov.finish(f"rs{i}", red_buf) for i in range(DEPTH)]
    red = _unpack(red_buf, meta)

    grads = {}
    nd = 2 * ns5
    da, dbbr, dbbi, dcr, dci, dd, dbglu = red[0:7]
    k = 7
    da_s = _sublane_sum("s5_dasum", da.reshape(2 * nd, 8, cfg.NS)).reshape(nd, 2, cfg.NS)
    g_abr = da_s[:, 0].reshape(nd, cfg.G, cfg.P).transpose(2, 0, 1).reshape(cfg.P, nd * cfg.G)
    g_abi = da_s[:, 1].reshape(nd, cfg.G, cfg.P).transpose(2, 0, 1).reshape(cfg.P, nd * cfg.G)
    glr, gli, gldt, gbr, gbi = _disc_bwd(*s5_lay, g_abr, g_abi, dbbr, dbbi)
    grads.update(s5_lam_re=glr.reshape(cfg.P, nd, cfg.G).transpose(1, 2, 0), s5_lam_im=gli.reshape(cfg.P, nd, cfg.G).transpose(1, 2, 0),
                 s5_log_dt=gldt, s5_b_re=gbr.reshape(S5_GROUP, cfg.P, nd, cfg.G).transpose(2, 3, 1, 0),
                 s5_b_im=gbi.reshape(S5_GROUP, cfg.P, nd, cfg.G).transpose(2, 3, 1, 0), s5_c_re=dcr, s5_c_im=dci,
                 s5_d=dd, s5_b_glu=dbglu)

    def my_cols(full, width):
        return lax.dynamic_slice_in_dim(full, shard * width, width, axis=full.ndim - 1)

    cvs = {n: [] for n in ("ln_g", "ln_b", "b_dw", "b_pw2", "w_dw", "b_pw1")}
    for j in range(nconv):
        for n, val in zip(("ln_g", "ln_b", "b_dw", "b_pw2", "w_dw", "b_pw1"), red[k:k + 6]):
            cvs[n].append(val)
        k += 6
    grads.update(cv_ln_g=my_cols(jnp.stack(cvs["ln_g"]), Ds), cv_ln_b=my_cols(jnp.stack(cvs["ln_b"]), Ds),
                 cv_b_dw=my_cols(jnp.stack(cvs["b_dw"]), Ds), cv_b_pw2=my_cols(jnp.stack(cvs["b_pw2"]), Ds),
                 cv_w_dw=my_cols(jnp.stack(cvs["w_dw"]), Ds), cv_b_pw1=my_cols(jnp.stack(cvs["b_pw1"]), 2 * D // 4))
    grads.update(ln_gain=my_cols(red[k], Ds), ln_bias=my_cols(red[k + 1], Ds))
    dm_all = red[k + 2]

    dm_sh = lax.dynamic_slice_in_dim(dm_all, shard * Wa, Wa, axis=2)
    gw_ada, dcond = _ada_bwd(c_all, dm_sh, w_ada)
    grads["w_ada"] = gw_ada
    grads["b_ada"] = _colsum_groups("ada_bsum", dm_all)
    dc_part = dcond[0:1]
    dc_buf = jnp.concatenate([jnp.where(ac == 0, dc_part, 0.0), jnp.zeros((7, D), F32)], axis=0)
    dc_tot = _allreduce8("cctx", dc_buf.reshape(8 * D // LANES, LANES)).reshape(8, D)[0:1]
    grads["c_ctx"] = _ew("cctx_grad", lambda g, cv: g * (_sigmoid(cv) * (1.0 + cv * (1.0 - _sigmoid(cv)))),
                         [jnp.broadcast_to(dc_tot, (8, D)), jnp.broadcast_to(c_ctx[None], (8, D))], [_sds((8, D))])[0][0]

    s5_layers = [i for i in range(DEPTH) if kinds[i] == "s5"]
    cv_layers = [i for i in range(DEPTH) if kinds[i] == "conv"]
    grads.update(mlp_w1=jnp.stack([reduced[i][0][0] for i in range(DEPTH)]),
                 mlp_w2=jnp.stack([reduced[i][0][1] for i in range(DEPTH)]),
                 s5_w_glu=jnp.stack([reduced[i][1][0] for i in s5_layers]),
                 cv_w_pw1=jnp.stack([reduced[i][1][0] for i in cv_layers]),
                 cv_w_pw2=jnp.stack([reduced[i][2][0] for i in cv_layers]))

    weights = dict(c_ctx=c_ctx, w_ada=w_ada, b_ada=b_ada, ln_gain=ln_gain, ln_bias=ln_bias, s5_lam_re=s5_lam_re,
                   s5_lam_im=s5_lam_im, s5_log_dt=s5_log_dt, s5_b_re=s5_b_re, s5_b_im=s5_b_im, s5_c_re=s5_c_re,
                   s5_c_im=s5_c_im, s5_d=s5_d, s5_w_glu=s5_w_glu, s5_b_glu=s5_b_glu, cv_w_pw1=cv_w_pw1, cv_b_pw1=cv_b_pw1,
                   cv_w_dw=cv_w_dw, cv_b_dw=cv_b_dw, cv_ln_g=cv_ln_g, cv_ln_b=cv_ln_b, cv_w_pw2=cv_w_pw2, cv_b_pw2=cv_b_pw2,
                   mlp_w1=mlp_w1, mlp_w2=mlp_w2)
    ms = dict(c_ctx=m_c_ctx, w_ada=m_w_ada, b_ada=m_b_ada, ln_gain=m_ln_gain, ln_bias=m_ln_bias, s5_lam_re=m_s5_lam_re,
              s5_lam_im=m_s5_lam_im, s5_log_dt=m_s5_log_dt, s5_b_re=m_s5_b_re, s5_b_im=m_s5_b_im, s5_c_re=m_s5_c_re,
              s5_c_im=m_s5_c_im, s5_d=m_s5_d, s5_w_glu=m_s5_w_glu, s5_b_glu=m_s5_b_glu, cv_w_pw1=m_cv_w_pw1,
              cv_b_pw1=m_cv_b_pw1, cv_w_dw=m_cv_w_dw, cv_b_dw=m_cv_b_dw, cv_ln_g=m_cv_ln_g, cv_ln_b=m_cv_ln_b,
              cv_w_pw2=m_cv_w_pw2, cv_b_pw2=m_cv_b_pw2, mlp_w1=m_mlp_w1, mlp_w2=m_mlp_w2)
    vs = dict(c_ctx=v_c_ctx, w_ada=v_w_ada, b_ada=v_b_ada, ln_gain=v_ln_gain, ln_bias=v_ln_bias, s5_lam_re=v_s5_lam_re,
              s5_lam_im=v_s5_lam_im, s5_log_dt=v_s5_log_dt, s5_b_re=v_s5_b_re, s5_b_im=v_s5_b_im, s5_c_re=v_s5_c_re,
              s5_c_im=v_s5_c_im, s5_d=v_s5_d, s5_w_glu=v_s5_w_glu, s5_b_glu=v_s5_b_glu, cv_w_pw1=v_cv_w_pw1,
              cv_b_pw1=v_cv_b_pw1, cv_w_dw=v_cv_w_dw, cv_b_dw=v_cv_b_dw, cv_ln_g=v_cv_ln_g, cv_ln_b=v_cv_ln_b,
              cv_w_pw2=v_cv_w_pw2, cv_b_pw2=v_cv_b_pw2, mlp_w1=v_mlp_w1, mlp_w2=v_mlp_w2)
    names = list(weights)
    deltas, new_m, new_v = {}, {}, {}
    for n in names:
        g = grads[n].reshape(weights[n].shape)
        grads[n] = g
        deltas[n], new_m[n], new_v[n] = _adamw("adamw_" + n, weights[n], g, ms[n], vs[n])
    return (loss, grad_x, *[grads[n] for n in names], *[deltas[n] for n in names], *[new_m[n] for n in names],
            *[new_v[n] for n in names])


def _sublane_sum(name, a):
    n, _, w = a.shape

    def body(a_ref, o_ref):
        for q in range(n):
            o_ref[q:q + 1, :] = jnp.sum(a_ref[q], axis=0, keepdims=True)

    return pl.pallas_call(body, name=name, out_shape=_sds((n, w)))(a)


def _colsum_groups(name, dm_all):
    nl, nr, w = dm_all.shape

    def body(d_ref, o_ref):
        o_ref[...] = jnp.zeros((8, w), F32) + jnp.sum(d_ref[...], axis=0, keepdims=True)

    out = pl.pallas_call(body, name=name, grid=(nl,), in_specs=[pl.BlockSpec((None, nr, w), lambda l: (l, 0, 0))],
                         out_specs=pl.BlockSpec((None, 8, w), lambda l: (l, 0, 0)), out_shape=_sds((nl, 8, w)),
                         compiler_params=_cp(1))(dm_all)
    return out[:, 0]
```

```python
import functools
import math

import jax
import jax.numpy as jnp
from jax import lax
from jax.experimental import pallas as pl
from jax.experimental.pallas import tpu as pltpu

F32 = jnp.float32
BF16 = jnp.bfloat16
MESH = pl.DeviceIdType.MESH
ANY = pl.BlockSpec(memory_space=pl.ANY)

DEPTH = 4
S5_GROUP = 16
S5_STATE = 64
GRID_W = 64
POS_TEMP = 10000.0
LAMBDA_RE_MAX = -1e-4
LN_EPS = 1e-5
DN_ALPHA = (2.0 * DEPTH) ** 0.25
ADAM_LR, ADAM_B1, ADAM_B2, ADAM_EPS, ADAM_WD, ADAM_STEP = 0.001, 0.9, 0.999, 1e-08, 0.01, 10

SUBLANES = 8
LANES = 128
OCT_CH = 128
OCT_ST = 512
CONV_ROWS = 64
VMEM_LIMIT = 56 * 1024 * 1024


def _cp(n_axes):
    return pltpu.CompilerParams(dimension_semantics=("arbitrary",) * n_axes, vmem_limit_bytes=VMEM_LIMIT)


def _full(shape, single=False):
    nd = len(shape)
    if single:
        return pl.BlockSpec(shape, lambda *i: (0,) * nd, pipeline_mode=pl.Buffered(1))
    return pl.BlockSpec(shape, lambda *i: (0,) * nd)


def _sds(shape, dtype=F32):
    return jax.ShapeDtypeStruct(tuple(shape), dtype)


def _mod(x, sh8, sc8):
    r, d = x.shape
    return (x.reshape(r // 8, 8, d) * (1.0 + sc8[None]) + sh8[None]).reshape(r, d)


def _rowscale(x, g8):
    r, d = x.shape
    return (x.reshape(r // 8, 8, d) * g8[None]).reshape(r, d)


def _sum8(x):
    r, w = x.shape
    return jnp.sum(x.reshape(r // 8, 8, w), axis=0)


def _ln_stats(r):
    mu = jnp.mean(r, axis=-1, keepdims=True)
    xc = r - mu
    var = jnp.mean(xc * xc, axis=-1, keepdims=True)
    rstd = lax.rsqrt(var + LN_EPS)
    return xc * rstd, rstd


def _ln_bwd(dxh, xhat, rstd):
    m1 = jnp.mean(dxh, axis=-1, keepdims=True)
    m2 = jnp.mean(dxh * xhat, axis=-1, keepdims=True)
    return rstd * (dxh - m1 - xhat * m2)


def _sigmoid(x):
    return 1.0 / (1.0 + jnp.exp(-x))


def _gelu(y):
    return 0.5 * y * (1.0 + lax.erf(y * (1.0 / math.sqrt(2.0))))


def _gelu_grad(y):
    return 0.5 * (1.0 + lax.erf(y * (1.0 / math.sqrt(2.0)))) + y * jnp.exp(-0.5 * y * y) * (1.0 / math.sqrt(2.0 * math.pi))


def _dot(a, b):
    return jnp.dot(a, b, preferred_element_type=F32)


def _dot_nt(a, b):
    return lax.dot_general(a, b, (((1,), (1,)), ((), ())), preferred_element_type=F32)


def _dot_tn(a, b):
    return lax.dot_general(a, b, (((0,), (0,)), ((), ())), preferred_element_type=F32)


class _Cfg:
    def __init__(self, x, ctx, mlp_w1, cv_w_dw):
        self.B, self.L, self.D = x.shape
        self.Lc = ctx.shape[1]
        assert self.B * 4 == SUBLANES, "two examples per device, four chunks each"
        self.F = mlp_w1.shape[2] * 4
        self.KW = cv_w_dw.shape[1]
        self.half = self.KW // 2
        self.G = self.D // S5_GROUP
        self.P = S5_STATE
        self.NS = self.G * self.P
        self.NO = self.D // OCT_CH
        assert self.NO % 2 == 0
        self.nx = self.L // 4
        self.nc = self.Lc // 4
        self.Tc = self.B * self.Lc
        self.Tx = self.B * self.L
        self.T = self.Tc + self.Tx
        self.TM = 512 if self.Tc % 512 == 0 else self.Tc
        assert self.Tx % self.TM == 0 and self.TM % 16 == 0
        self.HB = self.TM // 2
        assert SUBLANES * self.half <= self.HB
        self.TW = 512 if (self.Tc % 512 == 0 and self.Tx % 512 == 0) else self.TM

    def ti(self, n):
        t = 32 if self.nc % 32 == 0 else self.nc
        assert n % t == 0 and self.Tc % (8 * t) == 0
        return t

    def rows(self, ctx_too):
        return (0, self.T) if ctx_too else (self.Tc, self.Tx)


def _allgather8(name, x_shard):
    m_per, n = x_shard.shape
    assert m_per % 8 == 0

    def body(x_ref, out_ref, send_sems, recv_sems, local_sem):
        x, y, c = lax.axis_index("x"), lax.axis_index("y"), lax.axis_index("c")
        me, sibling = (x, y, c), (x, y, 1 - c)
        chips = [(1 - x, y), (x, 1 - y), (1 - x, 1 - y)]

        def rows(px, py, pc):
            return out_ref.at[pl.ds((4 * px + 2 * py + pc) * m_per, m_per), :]

        def copy(k, block, to, src=None):
            return pltpu.make_async_remote_copy(
                src_ref=rows(*block) if src is None else src, dst_ref=rows(*block),
                send_sem=send_sems.at[k], recv_sem=recv_sems.at[k], device_id=to, device_id_type=MESH)

        mine = pltpu.make_async_copy(x_ref, rows(*me), local_sem)
        mine.start()
        first = [copy(0, me, sibling, src=x_ref)]
        first += [copy(1 + j, me, (*chip, c), src=x_ref) for j, chip in enumerate(chips)]
        for cp in first:
            cp.start()
        passed = [copy(4 + j, (*chip, c), sibling) for j, chip in enumerate(chips)]
        for j, chip in enumerate(chips):
            copy(1 + j, (*chip, c), me).wait_recv()
            passed[j].start()
        copy(0, sibling, me).wait_recv()
        for j, chip in enumerate(chips):
            copy(4 + j, (*chip, 1 - c), me).wait_recv()
        for cp in first + passed:
            cp.wait_send()
        mine.wait()

    return pl.pallas_call(
        body, name=name, out_shape=_sds((8 * m_per, n), x_shard.dtype),
        in_specs=[pl.BlockSpec(memory_space=pltpu.VMEM)], out_specs=pl.BlockSpec(memory_space=pltpu.VMEM),
        scratch_shapes=[pltpu.SemaphoreType.DMA((7,)), pltpu.SemaphoreType.DMA((7,)), pltpu.SemaphoreType.DMA],
        compiler_params=pltpu.CompilerParams(vmem_limit_bytes=VMEM_LIMIT),
    )(x_shard)


def _flip(v, m):
    return v + m - 2 * v * m


def _peer(axis):
    x, y, c = lax.axis_index("x"), lax.axis_index("y"), lax.axis_index("c")
    if axis == "c":
        return (x, y, 1 - c)
    if axis == "xy":
        return (_flip(x, 1 - c), _flip(y, c), c)
    if axis == "yx":
        return (_flip(x, c), _flip(y, 1 - c), c)
    raise ValueError(axis)


def _pair_exchange(name, axis, inputs, out_shapes, aliases, plan):
    n_in = len(inputs)
    n_out = len(out_shapes)

    def body(*refs):
        ins, outs = refs[:n_in], refs[n_in:n_in + n_out]
        send_sems, recv_sems, local_sems = refs[n_in + n_out:]
        x, y, c = lax.axis_index("x"), lax.axis_index("y"), lax.axis_index("c")
        remote, local = plan(x, y, c, ins, outs)
        lcs = [pltpu.make_async_copy(s, d, local_sems.at[k]) for k, (s, d) in enumerate(local)]
        for cp in lcs:
            cp.start()
        rcs = [pltpu.make_async_remote_copy(src_ref=s, dst_ref=d, send_sem=send_sems.at[k], recv_sem=recv_sems.at[k],
                                            device_id=_peer(axis), device_id_type=MESH) for k, (s, d) in enumerate(remote)]
        for cp in rcs:
            cp.start()
        for cp in rcs:
            cp.wait()
        for cp in lcs:
            cp.wait()

    n_remote, n_local = plan.counts
    return pl.pallas_call(
        body, name=name, out_shape=tuple(out_shapes),
        in_specs=[ANY] * n_in, out_specs=tuple([ANY] * n_out),
        input_output_aliases=dict(aliases),
        scratch_shapes=[pltpu.SemaphoreType.DMA((n_remote,)), pltpu.SemaphoreType.DMA((n_remote,)),
                        pltpu.SemaphoreType.DMA((max(n_local, 1),))],
    )(*inputs)


def _plan(n_remote, n_local=0):
    def deco(fn):
        fn.counts = (n_remote, n_local)
        return fn
    return deco


HBM = pl.BlockSpec(memory_space=pltpu.HBM)
SEM = pl.BlockSpec(memory_space=pltpu.SEMAPHORE)


def _split_start(name, axis, bufs, plan):
    nb = len(bufs)
    n = plan.counts[0]

    def body(*refs):
        ins, send_sem, recv_sem, token = refs[:nb], refs[nb], refs[nb + 1], refs[-1]
        x, y, c = lax.axis_index("x"), lax.axis_index("y"), lax.axis_index("c")
        for k, (s, d) in enumerate(plan(x, y, c, ins)):
            pltpu.make_async_remote_copy(src_ref=s, dst_ref=d, send_sem=send_sem.at[k], recv_sem=recv_sem.at[k],
                                         device_id=_peer(axis), device_id_type=MESH).start()
        token[...] = jnp.zeros_like(token)

    outs = pl.pallas_call(
        body, name=name,
        out_shape=(pltpu.SemaphoreType.DMA((n,)), pltpu.SemaphoreType.DMA((n,)),
                   *[pltpu.HBM(b.shape, b.dtype) for b in bufs], _sds((8, LANES))),
        in_specs=[HBM] * nb, out_specs=(SEM, SEM, *([HBM] * nb), pl.BlockSpec(memory_space=pltpu.VMEM)),
        input_output_aliases={i: 2 + i for i in range(nb)},
        compiler_params=pltpu.CompilerParams(has_side_effects=pltpu.SideEffectType.DATAFLOW_SIDE_EFFECTING),
    )(*[pltpu.with_memory_space_constraint(b, pltpu.HBM) for b in bufs])
    return dict(name=name, axis=axis, plan=plan, send=outs[0], recv=outs[1], bufs=list(outs[2:2 + nb]), token=outs[-1])


def _split_wait(h, after):
    bufs, plan, axis = h["bufs"], h["plan"], h["axis"]
    nb = len(bufs)

    def body(*refs):
        ins, send_sem, recv_sem = refs[:nb], refs[nb], refs[nb + 1]
        x, y, c = lax.axis_index("x"), lax.axis_index("y"), lax.axis_index("c")
        for k, (s, d) in enumerate(plan(x, y, c, ins)):
            cp = pltpu.make_async_remote_copy(src_ref=s, dst_ref=d, send_sem=send_sem.at[k], recv_sem=recv_sem.at[k],
                                              device_id=_peer(axis), device_id_type=MESH)
            cp.wait_send()
            cp.wait_recv()

    outs = pl.pallas_call(
        body, name=h["name"] + "_wait", out_shape=tuple(pltpu.HBM(b.shape, b.dtype) for b in bufs),
        in_specs=[HBM] * nb + [SEM, SEM, ANY], out_specs=tuple([HBM] * nb),
        input_output_aliases={i: i for i in range(nb)},
        compiler_params=pltpu.CompilerParams(has_side_effects=pltpu.SideEffectType.DATAFLOW_SIDE_EFFECTING),
    )(*bufs, h["send"], h["recv"], after)
    return list(outs)


def _tie(small, tokens):
    for t in tokens:
        small = small + t[0, 0]
    return small


class _Overlap:
    def __init__(self):
        self.live = {}
        self.done = {}

    def add(self, key, gen):
        self.live[key] = gen
        return [next(gen)]

    def point(self, arr):
        tokens = []
        for key in list(self.live):
            try:
                tokens.append(self.live[key].send(arr))
            except StopIteration as e:
                self.done[key] = e.value
                del self.live[key]
        return tokens

    def finish(self, key, arr):
        while key in self.live:
            try:
                self.live[key].send(arr)
            except StopIteration as e:
                self.done[key] = e.value
                del self.live[key]
        return self.done.pop(key)


def _gather_gen(tag, fams):
    nf = len(fams)
    shapes = [f.shape for f in fams]
    views = [f.reshape(4, 2, -1, f.shape[-1]) for f in fams]

    @_plan(nf)
    def plan1(x, y, c, refs):
        s = 2 * x + y
        return [(refs[k].at[s, c], refs[k].at[s, c]) for k in range(nf)]

    @_plan(2 * nf)
    def plan2(x, y, c, refs):
        shards = [2 * x + y, 2 * _flip(x, 1 - c) + _flip(y, c)]
        return [(refs[k].at[s, c], refs[k].at[s, c]) for k in range(nf) for s in shards]

    @_plan(3 * nf)
    def plan3(x, y, c, refs):
        shards = [2 * (1 - x) + y, 2 * x + (1 - y), 2 * (1 - x) + (1 - y)]
        return [(refs[k].at[s, c], refs[k].at[s, c]) for k in range(nf) for s in shards]

    for rnd, (axis, plan) in enumerate((("xy", plan1), ("yx", plan2), ("c", plan3))):
        h = _split_start(f"{tag}_g{rnd}", axis, views, plan)
        after = yield h["token"]
        views = _split_wait(h, after)
    return [v.reshape(sh) for v, sh in zip(views, shapes)]


def _reduce_scatter_gen(tag, grads):
    ng = len(grads)
    flat = [g.reshape(4, 2, -1, g.shape[-1]) for g in grads]

    def empty(shape, dtype):
        return lax.empty(tuple(shape), dtype)

    @_plan(ng)
    def plan1(x, y, c, refs):
        return [(refs[k].at[:, 1 - c], refs[ng + k]) for k in range(ng)]

    h = _split_start(tag + "_r0", "c", flat + [empty((4,) + f.shape[2:], F32) for f in flat], plan1)
    after = yield h["token"]
    bufs = _split_wait(h, after)
    p1 = [_sel_add(f"{tag}_add1_{k}", bufs[k], lambda j, sc: (j, sc[2]), bufs[ng + k], True) for k in range(ng)]

    def sent1(kk, x, y, c):
        return ((1 - c) * kk + c * (1 - x), (1 - c) * (1 - y) + c * kk)

    def kept1(j, sc):
        x, y, c = sc[0], sc[1], sc[2]
        return ((1 - c) * j + c * x, (1 - c) * y + c * j)

    @_plan(2 * ng)
    def plan2(x, y, c, refs):
        return [(refs[k].at[sent1(kk, x, y, c)], refs[ng + k].at[kk]) for k in range(ng) for kk in range(2)]

    v1 = [pb.reshape(2, 2, pb.shape[1], pb.shape[2]) for p, pb in p1]
    h = _split_start(tag + "_r1", "yx", v1 + [empty((2,) + v.shape[2:], BF16) for v in v1], plan2)
    after = yield h["token"]
    bufs = _split_wait(h, after)
    p2 = [_sel_add(f"{tag}_add2_{k}", p1[k][0].reshape(2, 2, p1[k][0].shape[1], p1[k][0].shape[2]), kept1, bufs[ng + k], True)
          for k in range(ng)]

    @_plan(ng)
    def plan3(x, y, c, refs):
        return [(refs[k].at[(1 - c) * (1 - x) + c * (1 - y)], refs[ng + k]) for k in range(ng)]

    h = _split_start(tag + "_r2", "xy", [qb for q, qb in p2] + [empty(qb.shape[1:], BF16) for q, qb in p2], plan3)
    after = yield h["token"]
    bufs = _split_wait(h, after)
    fin = [_sel_add(f"{tag}_add3_{k}", p2[k][0], lambda j, sc: ((1 - sc[2]) * sc[0] + sc[2] * sc[1],), bufs[ng + k][None],
                    False, out_slots=(2, lambda j, sc: sc[2]))[0] for k in range(ng)]

    @_plan(ng)
    def plan4(x, y, c, refs):
        return [(refs[k].at[c], refs[k].at[c]) for k in range(ng)]

    h = _split_start(tag + "_r3", "c", fin, plan4)
    after = yield h["token"]
    full = _split_wait(h, after)
    return [full[k].reshape(grads[k].shape[1:]) for k in range(ng)]


def _xyc():
    return jnp.stack([lax.axis_index("x"), lax.axis_index("y"), lax.axis_index("c")]).astype(jnp.int32)


def _place_shard(name, w, fam, slot0, n_slots):
    n, kk, nn = w.shape
    kt = 256 if kk % 256 == 0 else kk

    def body(scal, w_ref, *rest):
        rest[-1][...] = w_ref[...].astype(BF16)

    in_specs = [pl.BlockSpec((None, kt, nn), lambda t, i, sc: (t, i, 0))]
    args = [_xyc(), w]
    aliases = {}
    if fam is not None:
        in_specs.append(ANY)
        args.append(fam)
        aliases = {2: 0}
    gs = pltpu.PrefetchScalarGridSpec(
        num_scalar_prefetch=1, grid=(n, kk // kt), in_specs=in_specs,
        out_specs=pl.BlockSpec((None, None, kt, nn), lambda t, i, sc: (2 * sc[0] + sc[1], slot0 + t, i, 0)))
    return pl.pallas_call(body, name=name, grid_spec=gs, out_shape=_sds((4, n_slots, kk, nn), BF16),
                          input_output_aliases=aliases, compiler_params=_cp(2))(*args)


def _gather_weights(fams):
    nf = len(fams)
    shapes = [f.shape for f in fams]
    views = [f.reshape(4, 2, -1, f.shape[-1]) for f in fams]
    outs = [_sds(v.shape, v.dtype) for v in views]
    alias = {k: k for k in range(nf)}

    @_plan(nf)
    def plan1(x, y, c, ins, outs_):
        s = 2 * x + y
        return ([(ins[k].at[s, c], outs_[k].at[s, c]) for k in range(nf)], [])

    views = _pair_exchange("gatherw_1", "xy", list(views), outs, alias, plan1)

    @_plan(2 * nf)
    def plan2(x, y, c, ins, outs_):
        shards = [2 * x + y, 2 * _flip(x, 1 - c) + _flip(y, c)]
        return ([(ins[k].at[s, c], outs_[k].at[s, c]) for k in range(nf) for s in shards], [])

    views = _pair_exchange("gatherw_2", "yx", list(views), outs, alias, plan2)

    @_plan(3 * nf)
    def plan3(x, y, c, ins, outs_):
        shards = [2 * (1 - x) + y, 2 * x + (1 - y), 2 * (1 - x) + (1 - y)]
        return ([(ins[k].at[s, c], outs_[k].at[s, c]) for k in range(nf) for s in shards], [])

    views = _pair_exchange("gatherw_c", "c", list(views), outs, alias, plan3)
    return [v.reshape(sh) for v, sh in zip(views, shapes)]


def _sel_add(name, a, a_sel, r, emit_bf16, out_slots=None):
    nr, rows, w = r.shape
    tr = 256 if rows % 256 == 0 else rows

    def body(scal, a_ref, r_ref, *outs):
        s = a_ref[...] + r_ref[...].astype(F32)
        outs[0][...] = s
        if emit_bf16:
            outs[1][...] = s.astype(BF16)

    lead = a.ndim - 2
    a_block = (None,) * lead + (tr, w)
    n_out, o_fn = out_slots if out_slots is not None else (nr, lambda j, sc: j)
    out_shape = [_sds((n_out, rows, w), F32)] + ([_sds((nr, rows, w), BF16)] if emit_bf16 else [])
    out_specs = [pl.BlockSpec((None, tr, w), lambda j, t, sc: (o_fn(j, sc), t, 0))]
    if emit_bf16:
        out_specs.append(pl.BlockSpec((None, tr, w), lambda j, t, sc: (j, t, 0)))
    gs = pltpu.PrefetchScalarGridSpec(
        num_scalar_prefetch=1, grid=(nr, rows // tr),
        in_specs=[pl.BlockSpec(a_block, lambda j, t, sc: tuple(a_sel(j, sc)) + (t, 0)),
                  pl.BlockSpec((None, tr, w), lambda j, t, sc: (j, t, 0))],
        out_specs=out_specs)
    return pl.pallas_call(body, name=name, grid_spec=gs, out_shape=out_shape, compiler_params=_cp(2))(_xyc(), a, r)


def _reduce_scatter(tag, grads):
    ng = len(grads)
    flat = [g.reshape(4, 2, -1, g.shape[-1]) for g in grads]

    @_plan(ng)
    def plan1(x, y, c, ins, outs_):
        return ([(ins[k].at[:, 1 - c], outs_[k]) for k in range(ng)], [])

    r1 = _pair_exchange(tag + "_rs_c", "c", flat, [_sds((4,) + f.shape[2:], F32) for f in flat], {}, plan1)
    p1 = [_sel_add(f"{tag}_add1_{k}", flat[k], lambda j, sc: (j, sc[2]), r1[k], True) for k in range(ng)]

    def sent1(kk, x, y, c):
        return ((1 - c) * kk + c * (1 - x), (1 - c) * (1 - y) + c * kk)

    def kept1(j, sc):
        x, y, c = sc[0], sc[1], sc[2]
        return ((1 - c) * j + c * x, (1 - c) * y + c * j)

    @_plan(2 * ng)
    def plan2(x, y, c, ins, outs_):
        return ([(ins[k].at[sent1(kk, x, y, c)], outs_[k].at[kk]) for k in range(ng) for kk in range(2)], [])

    v1 = [pb.reshape(2, 2, pb.shape[1], pb.shape[2]) for p, pb in p1]
    r2 = _pair_exchange(tag + "_rs_1", "yx", v1, [_sds((2,) + v.shape[2:], BF16) for v in v1], {}, plan2)
    p2 = [_sel_add(f"{tag}_add2_{k}", p1[k][0].reshape(2, 2, p1[k][0].shape[1], p1[k][0].shape[2]), kept1, r2[k], True)
          for k in range(ng)]

    @_plan(ng)
    def plan3(x, y, c, ins, outs_):
        return ([(ins[k].at[(1 - c) * (1 - x) + c * (1 - y)], outs_[k]) for k in range(ng)], [])

    r3 = _pair_exchange(tag + "_rs_2", "xy", [qb for q, qb in p2], [_sds(qb.shape[1:], BF16) for q, qb in p2], {}, plan3)
    fin = [_sel_add(f"{tag}_add3_{k}", p2[k][0], lambda j, sc: ((1 - sc[2]) * sc[0] + sc[2] * sc[1],), r3[k][None], False,
                    out_slots=(2, lambda j, sc: sc[2]))[0] for k in range(ng)]

    @_plan(ng)
    def plan4(x, y, c, ins, outs_):
        return ([(ins[k].at[c], outs_[k].at[c]) for k in range(ng)], [])

    full = _pair_exchange(tag + "_rs_c2", "c", fin, [_sds(f.shape, F32) for f in fin], {k: k for k in range(ng)}, plan4)
    return [full[k].reshape(grads[k].shape[1:]) for k in range(ng)]


def _allreduce8(tag, buf, point=None):
    rows, w = buf.shape
    assert rows % 16 == 0
    step = (lambda a: _tie(a, point(a))) if point is not None else (lambda a: a)
    one = lambda: _plan(1)(lambda x, y, c, ins, outs_: ([(ins[0], outs_[0])], []))
    (got,) = _pair_exchange(f"{tag}_ar_c", "c", [buf], [_sds(buf.shape, F32)], {}, one())
    cur = _ew(f"{tag}_aradd_c", lambda a, b: a + b, [buf, got], [_sds(buf.shape, F32)])[0]
    cur = step(cur).reshape(2, rows // 2, w)
    mine = _plan(1)(lambda x, y, c, ins, outs_: ([(ins[0].at[c], outs_[0])], []))
    (got,) = _pair_exchange(f"{tag}_ar_1", "xy", [cur], [_sds(cur.shape[1:], F32)], {}, mine)
    (h1,) = _sel_add(f"{tag}_aradd_1", cur, lambda j, sc: (sc[2],), got[None], False)
    h1 = step(h1)
    (got,) = _pair_exchange(f"{tag}_ar_2", "yx", [h1[0]], [_sds(h1.shape[1:], F32)], {}, one())
    (h2,) = _sel_add(f"{tag}_aradd_2", h1, lambda j, sc: (0,), got[None], False, out_slots=(2, lambda j, sc: sc[2]))
    swap = _plan(1)(lambda x, y, c, ins, outs_: ([(ins[0].at[c], outs_[0].at[c])], []))
    (full,) = _pair_exchange(f"{tag}_ar_c2", "c", [h2], [_sds(h2.shape, F32)], {0: 0}, swap)
    return full.reshape(rows, w)


def _ew(name, fn, ins, outs):
    rows, w = ins[0].shape
    tr = rows
    for cand in (512, 256, 128, 64, 32, 16, 8):
        if rows % cand == 0 and rows > cand and cand * w * 4 <= (1 << 20):
            tr = cand
            break
    n_in = len(ins)

    def body(*refs):
        vals = fn(*[r[...] for r in refs[:n_in]])
        if not isinstance(vals, (tuple, list)):
            vals = (vals,)
        for o, v in zip(refs[n_in:], vals):
            o[...] = v.astype(o.dtype)

    spec = pl.BlockSpec((tr, w), lambda i: (i, 0))
    return pl.pallas_call(body, name=name, grid=(rows // tr,), in_specs=[spec] * n_in,
                          out_specs=[spec] * len(outs), out_shape=list(outs), compiler_params=_cp(1))(*ins)


def _ew3(name, fn, ins, n_out):
    aa, bb, cc = ins[0].shape
    pad_bytes = (-(-bb // SUBLANES) * SUBLANES) * (-(-cc // LANES) * LANES) * 4
    ta = 1
    for cand in range(aa, 0, -1):
        if aa % cand == 0 and cand * pad_bytes <= (1 << 20):
            ta = cand
            break
    n_in = len(ins)

    def body(*refs):
        vals = fn(*[r[...] for r in refs[:n_in]])
        for o, v in zip(refs[n_in:], vals):
            o[...] = v

    spec = pl.BlockSpec((ta, bb, cc), lambda i: (i, 0, 0))
    return pl.pallas_call(body, name=name, grid=(aa // ta,), in_specs=[spec] * n_in, out_specs=[spec] * n_out,
                          out_shape=[_sds((aa, bb, cc))] * n_out, compiler_params=_cp(1))(*ins)


def _view_for_ew(a):
    if a.ndim == 1:
        return a.reshape(1, -1)
    if a.ndim == 2:
        return a
    if a.shape[-1] % LANES == 0 and a.shape[-2] % SUBLANES == 0:
        return a.reshape(-1, a.shape[-1])
    return a.reshape(-1, a.shape[-2], a.shape[-1])


def _adamw(name, w, g, m, v):
    def fn(w, g, m, v):
        m = ADAM_B1 * m + (1.0 - ADAM_B1) * g
        v = ADAM_B2 * v + (1.0 - ADAM_B2) * (g * g)
        m_hat = m / (1.0 - ADAM_B1 ** ADAM_STEP)
        v_hat = v / (1.0 - ADAM_B2 ** ADAM_STEP)
        delta = -ADAM_LR * (m_hat / (jnp.sqrt(v_hat) + ADAM_EPS) + ADAM_WD * w)
        return delta, m, v

    shp = w.shape
    a = [_view_for_ew(t) for t in (w, g, m, v)]
    if a[0].ndim == 3:
        o = _ew3(name, fn, a, 3)
    else:
        o = _ew(name, fn, a, [_sds(a[0].shape)] * 3)
    return tuple(t.reshape(shp) for t in o)


def _to_perm(a):
    b, ls, d = a.shape
    n = ls // 4
    return a.reshape(b * 4, n, d).swapaxes(0, 1).reshape(n * 8, d)


def _from_perm(p, b, ls):
    n = ls // 4
    return p.reshape(n, b * 4, p.shape[-1]).swapaxes(0, 1).reshape(b, ls, p.shape[-1])


def _pos_embed(rows, dim):
    def sincos(pos, d):
        quarter = d // 2
        omega = POS_TEMP ** (-jnp.arange(quarter, dtype=F32) / quarter)
        ang = pos[:, None] * omega[None, :]
        return jnp.concatenate([jnp.sin(ang), jnp.cos(ang)], axis=-1)

    row_idx = jnp.repeat(jnp.arange(rows), GRID_W).astype(F32)
    col_idx = jnp.tile(jnp.arange(GRID_W), rows).astype(F32)
    return jnp.concatenate([sincos(row_idx, dim // 2), sincos(col_idx, dim // 2)], axis=-1)


def _stream_of(cfg, off_tiles, tile_rows):
    nct = cfg.Tc // tile_rows
    return lambda i: jnp.where(i + off_tiles >= nct, 1, 0)


def _ada_fwd(c_all, w_ada, b_shard):
    nl, d, w = w_ada.shape
    tn = 512 if w % 512 == 0 else w

    def body(c_ref, w_ref, b_ref, o_ref):
        cv = c_ref[...]
        cond = (cv * _sigmoid(cv)).astype(BF16)
        o_ref[...] = _dot(cond, w_ref[...].astype(BF16)) + b_ref[...]

    return pl.pallas_call(
        body, name="ada_fwd", grid=(nl, w // tn),
        in_specs=[_full(c_all.shape), pl.BlockSpec((None, d, tn), lambda l, j: (l, 0, j)),
                  pl.BlockSpec((None, 1, tn), lambda l, j: (l, 0, j))],
        out_specs=pl.BlockSpec((None, c_all.shape[0], tn), lambda l, j: (l, 0, j)),
        out_shape=_sds((nl, c_all.shape[0], w)), compiler_params=_cp(2))(c_all, w_ada, b_shard)


def _ada_bwd(c_all, dmod_shard, w_ada):
    nl, d, w = w_ada.shape
    tn = 512 if w % 512 == 0 else w
    nr = c_all.shape[0]

    def body(c_ref, dm_ref, w_ref, gw_ref, dc_ref):
        j = pl.program_id(0) * (w // tn) + pl.program_id(1)
        cv = c_ref[...]
        cond = (cv * _sigmoid(cv)).astype(BF16)
        dm = dm_ref[...].astype(BF16)
        gw_ref[...] = _dot_tn(cond, dm)
        part = _dot_nt(dm[16:24], w_ref[...].astype(BF16))

        @pl.when(j == 0)
        def _():
            dc_ref[...] = part

        @pl.when(j > 0)
        def _():
            dc_ref[...] += part

    return pl.pallas_call(
        body, name="ada_bwd", grid=(nl, w // tn),
        in_specs=[_full(c_all.shape), pl.BlockSpec((None, nr, tn), lambda l, j: (l, 0, j)),
                  pl.BlockSpec((None, d, tn), lambda l, j: (l, 0, j))],
        out_specs=[pl.BlockSpec((None, d, tn), lambda l, j: (l, 0, j)), _full((8, d))],
        out_shape=[_sds((nl, d, w)), _sds((8, d))], compiler_params=_cp(2))(c_all, dmod_shard, w_ada)


def _disc(lr, li, ldt, br, bi):
    lr = jnp.minimum(lr, LAMBDA_RE_MAX)
    dt = jnp.exp(ldt)
    mag = jnp.exp(lr * dt)
    abr = mag * jnp.cos(li * dt)
    abi = mag * jnp.sin(li * dt)
    den = lr * lr + li * li
    nr = abr - 1.0
    ni = abi
    cr = (nr * lr + ni * li) / den
    ci = (ni * lr - nr * li) / den
    return abr, abi, cr[None] * br - ci[None] * bi, cr[None] * bi + ci[None] * br


def _disc_fwd(lr, li, ldt, br, bi):
    def body(a, b, c, d, e, o1, o2, o3, o4):
        r = _disc(a[...], b[...], c[...], d[...], e[...])
        o1[...], o2[...], o3[...], o4[...] = r

    return pl.pallas_call(body, name="s5_disc_fwd", out_shape=[_sds(lr.shape), _sds(lr.shape), _sds(br.shape), _sds(br.shape)])(
        lr, li, ldt, br, bi)


def _disc_bwd(lr, li, ldt, br, bi, g_abr, g_abi, g_bbr, g_bbi):
    def body(a, b, c, d, e, g1, g2, g3, g4, o1, o2, o3, o4, o5):
        _, vjp = jax.vjp(_disc, a[...], b[...], c[...], d[...], e[...])
        r = vjp((g1[...], g2[...], g3[...], g4[...]))
        o1[...], o2[...], o3[...], o4[...], o5[...] = r

    return pl.pallas_call(
        body, name="s5_disc_bwd",
        out_shape=[_sds(lr.shape), _sds(li.shape), _sds(ldt.shape), _sds(br.shape), _sds(bi.shape)])(
        lr, li, ldt, br, bi, g_abr, g_abi, g_bbr, g_bbi)


def _s5_layouts(cfg, lam_re, lam_im, log_dt, b_re, b_im):
    P, G = cfg.P, cfg.G
    nd = lam_re.shape[0]
    lr = lam_re.transpose(2, 0, 1).reshape(P, nd * G)
    li = lam_im.transpose(2, 0, 1).reshape(P, nd * G)
    ldt = log_dt.reshape(1, nd * G)
    br = b_re.transpose(3, 2, 0, 1).reshape(S5_GROUP, P, nd * G)
    bi = b_im.transpose(3, 2, 0, 1).reshape(S5_GROUP, P, nd * G)
    return lr, li, ldt, br, bi


def _coef_rows(cfg, abr, abi, conj):
    nd = abr.shape[1] // cfg.G

    def one(t):
        return t.reshape(cfg.P, nd, cfg.G).transpose(1, 2, 0).reshape(nd, cfg.NS)
    a = jnp.stack([one(abr), -one(abi) if conj else one(abi)], axis=1)
    return jnp.broadcast_to(a[:, :, None, :], (nd, 2, SUBLANES, cfg.NS))


def _blockdiag_b(cfg, bbr, bbi):
    eye = jnp.eye(8, dtype=F32)
    nd = bbr.shape[2] // cfg.G

    def one(t):
        t = t.reshape(S5_GROUP, cfg.P, nd, cfg.G).transpose(2, 3, 0, 1)
        t = t.reshape(nd, cfg.NO, 8, S5_GROUP, cfg.P)
        return jnp.einsum("dogcp,gh->dogchp", t, eye).reshape(nd, cfg.NO, OCT_CH, OCT_ST)

    return jnp.concatenate([one(bbr), one(bbi)], axis=-1).astype(BF16)


def _blockdiag_c(cfg, c_re, c_im):
    eye = jnp.eye(8, dtype=F32)
    nd = c_re.shape[0]

    def one(t):
        t = t.transpose(0, 1, 3, 2).reshape(nd, cfg.NO, 8, cfg.P, S5_GROUP)
        return jnp.einsum("dogpc,gh->dogphc", t, eye).reshape(nd, cfg.NO, OCT_ST, OCT_CH)

    return jnp.concatenate([one(c_re), -one(c_im)], axis=2).astype(BF16)


def _diag_b(cfg, dbf):
    eye = jnp.eye(8, dtype=F32)
    nd = dbf.shape[0]

    def one(t):
        t = t.reshape(nd, cfg.NO, 8, S5_GROUP, 8, cfg.P)
        t = jnp.einsum("dogchp,gh->dogcp", t, eye).reshape(nd, cfg.G, S5_GROUP, cfg.P)
        return t.transpose(2, 3, 0, 1).reshape(S5_GROUP, cfg.P, nd * cfg.G)

    return one(dbf[..., :OCT_ST]), one(dbf[..., OCT_ST:])


def _diag_c(cfg, dcft):
    eye = jnp.eye(8, dtype=F32)
    nd = dcft.shape[0]

    def one(t):
        t = t.reshape(nd, cfg.NO, 8, S5_GROUP, 8, cfg.P)
        return jnp.einsum("dohcgp,gh->dogcp", t, eye).reshape(nd, cfg.G, S5_GROUP, cfg.P)

    return one(dcft[..., :OCT_ST]), -one(dcft[..., OCT_ST:])


def _recur(buf, st, a_ref, n_oct, ti, rev, store, flat=False):
    for o in range(0, n_oct, 2):
        cols = [(pl.ds(oo * 2 * OCT_ST, OCT_ST), pl.ds(oo * 2 * OCT_ST + OCT_ST, OCT_ST)) for oo in (o, o + 1)]
        scol = [pl.ds(oo * OCT_ST, OCT_ST) for oo in (o, o + 1)]
        coef = [(a_ref[0, :, sc], a_ref[1, :, sc]) for sc in scol]
        init = (st[0, :, scol[0]], st[1, :, scol[0]], st[0, :, scol[1]], st[1, :, scol[1]])

        def step(i4, carry, cols=cols, coef=coef):
            carry = list(carry)
            for q in range(unroll):
                i = i4 * unroll + q
                r = pl.multiple_of((i + rev * (ti - 1 - 2 * i)) * 8, 8)
                for s in range(2):
                    sr, si = carry[2 * s], carry[2 * s + 1]
                    ar, ai = coef[s]
                    zr = buf[pl.ds(r, 8), cols[s][0]]
                    zi = buf[pl.ds(r, 8), cols[s][1]]
                    nr = ar * sr - ai * si + zr
                    ni = ar * si + ai * sr + zi
                    if store:
                        buf[pl.ds(r, 8), cols[s][0]] = nr
                        buf[pl.ds(r, 8), cols[s][1]] = ni
                    carry[2 * s], carry[2 * s + 1] = nr, ni
            return tuple(carry)

        unroll = 4 if ti % 4 == 0 else 1
        if flat:
            fin = init
            for i4 in range(ti // unroll):
                fin = step(i4, fin)
        else:
            fin = lax.fori_loop(0, ti // unroll, step, init)
        st[0, :, scol[0]] = fin[0]
        st[1, :, scol[0]] = fin[1]
        st[0, :, scol[1]] = fin[2]
        st[1, :, scol[1]] = fin[3]


def _s5_fwd_pass(cfg, name, tok, mod8, col_sh, col_sc, bf, acoef, r0, n, s_init=None, cf=None, y_prev=None):
    D, NO, NS = cfg.D, cfg.NO, cfg.NS
    ti = cfg.ti(n)
    nb = n // ti
    R = 8 * ti
    ob = r0 // R
    second = s_init is not None
    blk = lambda d, j: ob + j + d * (nb - 1 - 2 * j)

    def body(*refs):
        if second:
            tok_ref, mod_ref, bf_ref, a_ref, si_ref, cf_ref, yp_ref, y_ref, ck_ref, fin_ref, zbuf, st = refs
        else:
            tok_ref, mod_ref, bf_ref, a_ref, fin_ref, zbuf, st = refs
        d = pl.program_id(0)
        j = pl.program_id(1)

        @pl.when(j == 0)
        def _():
            if second:
                st[...] = si_ref[...]
            else:
                st[...] = jnp.zeros_like(st)

        if second:
            ck_ref[...] = st[...]
        u = _mod(tok_ref[...], mod_ref[:, col_sh:col_sh + D], mod_ref[:, col_sc:col_sc + D]).astype(BF16)
        for o in range(NO):
            zbuf[:, o * 1024:(o + 1) * 1024] = _dot(u[:, o * OCT_CH:(o + 1) * OCT_CH], bf_ref[o])
        _recur(zbuf, st, a_ref, NO, ti, d, second, flat=True)
        if second:
            for o in range(NO):
                y_ref[:, o * OCT_CH:(o + 1) * OCT_CH] = _dot(zbuf[:, o * 1024:(o + 1) * 1024].astype(BF16), cf_ref[o])

        @pl.when(j == nb - 1)
        def _():
            fin_ref[...] = st[...]

    st_spec = pl.BlockSpec((None, 2, 8, NS), lambda d, j: (d, 0, 0, 0))
    in_specs = [pl.BlockSpec((R, D), lambda d, j: (blk(d, j), 0)), _full(mod8.shape),
                pl.BlockSpec((None, NO, OCT_CH, 1024), lambda d, j: (d, 0, 0, 0)), st_spec]
    args = [tok, mod8, bf, acoef]
    scratch = [pltpu.VMEM((R, NO * 1024), F32), pltpu.VMEM((2, 8, NS), F32)]
    if not second:
        return pl.pallas_call(body, name=name, grid=(2, nb), in_specs=in_specs, out_specs=st_spec,
                              out_shape=_sds((2, 2, 8, NS)), scratch_shapes=scratch, compiler_params=_cp(2))(*args)
    in_specs += [st_spec, pl.BlockSpec((None, NO, 1024, OCT_CH), lambda d, j: (d, 0, 0, 0))]
    args += [s_init, cf]
    aliases = {}
    if y_prev is not None:
        in_specs.append(ANY)
        args.append(y_prev)
        aliases = {6: 0}
    else:
        in_specs.append(_full((8, LANES)))
        args.append(jnp.zeros((8, LANES), F32))
    out_specs = [pl.BlockSpec((None, R, D), lambda d, j: (d, blk(d, j), 0)),
                 pl.BlockSpec((None, None, 2, 8, NS), lambda d, j: (d, j + d * (nb - 1 - 2 * j), 0, 0, 0)), st_spec]
    out_shape = [_sds((2, cfg.T, D)), _sds((2, nb, 2, 8, NS)), _sds((2, 2, 8, NS))]
    return pl.pallas_call(body, name=name, grid=(2, nb), in_specs=in_specs, out_specs=out_specs, out_shape=out_shape,
                          input_output_aliases=aliases, scratch_shapes=scratch, compiler_params=_cp(2))(*args)


def _s5_chain(cfg, name, fin_local, acoef, n, inc, prev_fin=None):
    NS = cfg.NS
    nsq = int(round(math.log2(n)))
    assert 2 ** nsq == n

    def body(*refs):
        if prev_fin is not None:
            f_ref, a_ref, p_ref, o_ref = refs
        else:
            f_ref, a_ref, o_ref = refs
        for d in range(2):
            pr, pi = a_ref[d, 0, 0:1, :], a_ref[d, 1, 0:1, :]
            for _ in range(nsq):
                pr, pi = pr * pr - pi * pi, 2.0 * pr * pi
            for b in range(2):
                order = [4 * b + k for k in range(4)]
                if not inc[d]:
                    order = order[::-1]
                if prev_fin is not None:
                    last = order[-1]
                    sr, si = p_ref[d, 0, last:last + 1, :], p_ref[d, 1, last:last + 1, :]
                else:
                    sr = jnp.zeros((1, NS), F32)
                    si = jnp.zeros((1, NS), F32)
                for k in order:
                    o_ref[d, 0, k:k + 1, :] = sr
                    o_ref[d, 1, k:k + 1, :] = si
                    fr, fi = f_ref[d, 0, k:k + 1, :], f_ref[d, 1, k:k + 1, :]
                    sr, si = pr * sr - pi * si + fr, pr * si + pi * sr + fi

    args = [fin_local, acoef] + ([prev_fin] if prev_fin is not None else [])
    return pl.pallas_call(body, name=name, out_shape=_sds((2, 2, 8, NS)))(*args)


def _s5_forward(cfg, tag, tok, mod8, col_sh, col_sc, bf, cf, acoef, point=None):
    saved = {}
    fin_prev = None
    y = None
    for ph, (r0, n) in (("c", (0, cfg.nc)), ("x", (cfg.Tc, cfg.nx))):
        m8 = mod8[0 if ph == "c" else 1]
        loc = _s5_fwd_pass(cfg, f"{tag}_scan1{ph}", tok, m8, col_sh, col_sc, bf, acoef, r0, n)
        if point is not None and ph == "x":
            m8 = _tie(m8, point(loc))
        s_in = _s5_chain(cfg, f"{tag}_chain{ph}", loc, acoef, n, (True, False), fin_prev)
        y, ck, fin_prev = _s5_fwd_pass(cfg, f"{tag}_scan2{ph}", tok, m8, col_sh, col_sc, bf, acoef, r0, n, s_in, cf, y)
        saved[ph] = ck
    return y, saved


def _s5_bwd_pass(cfg, name, dy, tok, mod8, col_sh, col_sc, bf, cf, acoef, acoef_adj, r0, n, g_init=None, ck=None,
                 du_prev=None):
    D, NO, NS = cfg.D, cfg.NO, cfg.NS
    ti = cfg.ti(n)
    nb = n // ti
    R = 8 * ti
    ob = r0 // R
    second = g_init is not None
    has_dy = dy is not None
    blk = lambda d, j: ob + j + (1 - d) * (nb - 1 - 2 * j)

    def body(*refs):
        refs = list(refs)
        dy_ref = refs.pop(0) if has_dy else None
        if second:
            (tok_ref, mod_ref, bf_ref, cf_ref, a_ref, aa_ref, gi_ref, ck_ref, dup_ref,
             du_ref, da_ref, dbf_ref, dcf_ref, gfin_ref, qbuf, zbuf, gst, hst) = refs
        else:
            cf_ref, aa_ref, gfin_ref, qbuf, gst = refs
        d = pl.program_id(0)
        j = pl.program_id(1)

        @pl.when(j == 0)
        def _():
            if second:
                gst[...] = gi_ref[...]
                da_ref[...] = jnp.zeros_like(da_ref)
                dbf_ref[...] = jnp.zeros_like(dbf_ref)
                dcf_ref[...] = jnp.zeros_like(dcf_ref)
            else:
                gst[...] = jnp.zeros_like(gst)

        if has_dy:
            dyb = dy_ref[...].astype(BF16)
            for o in range(NO):
                qbuf[:, o * 1024:(o + 1) * 1024] = _dot_nt(dyb[:, o * OCT_CH:(o + 1) * OCT_CH], cf_ref[o])
        else:
            qbuf[...] = jnp.zeros_like(qbuf)
        _recur(qbuf, gst, aa_ref, NO, ti, 1 - d, second, flat=True)

        if second:
            u = _mod(tok_ref[...], mod_ref[:, col_sh:col_sh + D], mod_ref[:, col_sc:col_sc + D]).astype(BF16)
            for o in range(NO):
                zbuf[:, o * 1024:(o + 1) * 1024] = _dot(u[:, o * OCT_CH:(o + 1) * OCT_CH], bf_ref[o])
            hst[...] = ck_ref[...]
            _recur(zbuf, hst, a_ref, NO, ti, d, True, flat=True)

            g_off, h_off = (1 - d) * 8, d * 8
            edge = pl.multiple_of(d * (R - 8), 8)
            for o in range(0, NO, 2):
                cols = [(pl.ds(oo * 1024, OCT_ST), pl.ds(oo * 1024 + OCT_ST, OCT_ST)) for oo in (o, o + 1)]
                scol = [pl.ds(oo * OCT_ST, OCT_ST) for oo in (o, o + 1)]
                init = []
                for s in range(2):
                    er, ei = qbuf[pl.ds(edge, 8), cols[s][0]], qbuf[pl.ds(edge, 8), cols[s][1]]
                    kr, ki = ck_ref[0, :, scol[s]], ck_ref[1, :, scol[s]]
                    init += [er * kr + ei * ki, ei * kr - er * ki]

                def stp(i, carry, cols=cols):
                    rg = pl.multiple_of(i * 8 + g_off, 8)
                    rh = pl.multiple_of(i * 8 + h_off, 8)
                    out = []
                    for s in range(2):
                        gr, gi = qbuf[pl.ds(rg, 8), cols[s][0]], qbuf[pl.ds(rg, 8), cols[s][1]]
                        hr, hi = zbuf[pl.ds(rh, 8), cols[s][0]], zbuf[pl.ds(rh, 8), cols[s][1]]
                        out += [carry[2 * s] + (gr * hr + gi * hi), carry[2 * s + 1] + (gi * hr - gr * hi)]
                    return tuple(out)

                fin = tuple(init)
                for i in range(ti - 1):
                    fin = stp(i, fin)
                for s in range(2):
                    da_ref[0, :, scol[s]] += fin[2 * s]
                    da_ref[1, :, scol[s]] += fin[2 * s + 1]

            for o in range(NO):
                gb = qbuf[:, o * 1024:(o + 1) * 1024].astype(BF16)
                uo = u[:, o * OCT_CH:(o + 1) * OCT_CH]
                dbf_ref[o] += _dot_tn(uo, gb)
                if has_dy:
                    dcf_ref[o] += _dot_tn(dyb[:, o * OCT_CH:(o + 1) * OCT_CH], zbuf[:, o * 1024:(o + 1) * 1024].astype(BF16))
                du_ref[:, o * OCT_CH:(o + 1) * OCT_CH] = _dot_nt(gb, bf_ref[o])

        @pl.when(j == nb - 1)
        def _():
            gfin_ref[...] = gst[...]

    st_spec = pl.BlockSpec((None, 2, 8, NS), lambda d, j: (d, 0, 0, 0))
    row_spec = pl.BlockSpec((R, D), lambda d, j: (blk(d, j), 0))
    bf_spec = pl.BlockSpec((None, NO, OCT_CH, 1024), lambda d, j: (d, 0, 0, 0))
    cf_spec = pl.BlockSpec((None, NO, 1024, OCT_CH), lambda d, j: (d, 0, 0, 0))
    in_specs, args = [], []
    if has_dy:
        in_specs.append(row_spec)
        args.append(dy)
    if not second:
        in_specs += [cf_spec, st_spec]
        args += [cf, acoef_adj]
        return pl.pallas_call(body, name=name, grid=(2, nb), in_specs=in_specs, out_specs=st_spec,
                              out_shape=_sds((2, 2, 8, NS)),
                              scratch_shapes=[pltpu.VMEM((R, NO * 1024), F32), pltpu.VMEM((2, 8, NS), F32)],
                              compiler_params=_cp(2))(*args)
    ck_spec = pl.BlockSpec((None, None, 2, 8, NS), lambda d, j: (d, j + (1 - d) * (nb - 1 - 2 * j), 0, 0, 0))
    in_specs += [row_spec, _full(mod8.shape), bf_spec, cf_spec, st_spec, st_spec, st_spec, ck_spec]
    args += [tok, mod8, bf, cf, acoef, acoef_adj, g_init, ck]
    n_before = len(args)
    aliases = {}
    if du_prev is not None:
        in_specs.append(ANY)
        args.append(du_prev)
        aliases = {n_before: 0}
    else:
        in_specs.append(_full((8, LANES)))
        args.append(jnp.zeros((8, LANES), F32))
    out_specs = [pl.BlockSpec((None, R, D), lambda d, j: (d, blk(d, j), 0)), st_spec, bf_spec, bf_spec, st_spec]
    out_shape = [_sds((2, cfg.T, D)), _sds((2, 2, 8, NS)), _sds((2, NO, OCT_CH, 1024)), _sds((2, NO, OCT_CH, 1024)),
                 _sds((2, 2, 8, NS))]
    scratch = [pltpu.VMEM((R, NO * 1024), F32), pltpu.VMEM((R, NO * 1024), F32), pltpu.VMEM((2, 8, NS), F32),
               pltpu.VMEM((2, 8, NS), F32)]
    return pl.pallas_call(body, name=name, grid=(2, nb), in_specs=in_specs, out_specs=out_specs, out_shape=out_shape,
                          input_output_aliases=aliases, scratch_shapes=scratch, compiler_params=_cp(2))(*args)


def _s5_backward(cfg, tag, dy, dy_ctx, tok, mod8, col_sh, col_sc, bf, cf, acoef, acoef_adj, saved):
    g_prev = None
    acc = None
    du = None
    for ph, (r0, n) in (("x", (cfg.Tc, cfg.nx)), ("c", (0, cfg.nc))):
        m8 = mod8[0 if ph == "c" else 1]
        dyp = dy if (ph == "x" or dy_ctx) else None
        loc = _s5_bwd_pass(cfg, f"{tag}_adjA{ph}", dyp, tok, m8, col_sh, col_sc, bf, cf, acoef, acoef_adj, r0, n)
        g_in = _s5_chain(cfg, f"{tag}_adjchain{ph}", loc, acoef_adj, n, (False, True), g_prev)
        du, da, dbf, dcf, g_prev = _s5_bwd_pass(cfg, f"{tag}_adjB{ph}", dyp, tok, m8, col_sh, col_sc, bf, cf, acoef,
                                                acoef_adj, r0, n, g_in, saved[ph], du)
        new = (da, dbf, dcf)
        if acc is None:
            acc = new
        else:
            acc = tuple(_ew(f"{tag}_accsum{q}", lambda a, b: a + b, [a.reshape(-1, a.shape[-1]), b.reshape(-1, b.shape[-1])],
                            [_sds((a.size // a.shape[-1], a.shape[-1]))])[0].reshape(a.shape)
                        for q, (a, b) in enumerate(zip(acc, new)))
    return du, acc


def _tok_specs(cfg, rows, width, tile=None):
    tm = tile or cfg.TM
    off = rows[0] // tm
    return pl.BlockSpec((tm, width), lambda i: (i + off, 0)), rows[1] // tm, off


def _mod_spec(cfg, mod8, off):
    st = _stream_of(cfg, off, cfg.TM)
    return pl.BlockSpec((None, 8, mod8.shape[-1]), lambda i: (st(i), 0, 0))


def _wspec(w):
    fam, slot = w
    _, _, kk, nn = fam.shape
    return pl.BlockSpec((4, None, kk, nn), lambda *i: (0, slot, 0, 0), pipeline_mode=pl.Buffered(1))


def _glu_ln(cfg, name, rows, tok, y, mod8, cols, dskip, w, b, gain, bias):
    D, TM = cfg.D, cfg.TM
    csh, csc, cg = cols
    spec, nt, off = _tok_specs(cfg, rows, D)
    spec2, _, _ = _tok_specs(cfg, rows, 2 * D)

    def body(tok_ref, y_ref, mod_ref, ds_ref, w_ref, b_ref, g_ref, bi_ref, x1_ref, r1_ref, mix_ref, zz_ref, zb_ref, yy_ref):
        t = tok_ref[...]
        u = _mod(t, mod_ref[:, csh:csh + D], mod_ref[:, csc:csc + D])
        yy = ds_ref[...] * u + y_ref[0] + y_ref[1]
        zb = _gelu(yy).astype(BF16)
        zz = jnp.concatenate([_dot(zb, w_ref[s]) for s in range(4)], axis=-1) + b_ref[...]
        mix = zz[:, :D] * _sigmoid(zz[:, D:])
        r1 = DN_ALPHA * t + _rowscale(mix, mod_ref[:, cg:cg + D])
        xhat, _ = _ln_stats(r1)
        x1_ref[...] = xhat * g_ref[...] + bi_ref[...]
        r1_ref[...] = r1
        mix_ref[...] = mix.astype(BF16)
        zz_ref[...] = zz.astype(BF16)
        zb_ref[...] = zb
        yy_ref[...] = yy.astype(BF16)

    T = cfg.T
    return pl.pallas_call(
        body, name=name, grid=(nt,),
        in_specs=[spec, pl.BlockSpec((2, TM, D), lambda i: (0, i + off, 0)), _mod_spec(cfg, mod8, off), _full((1, D)),
                  _wspec(w), _full((1, 2 * D)), _full((1, D)), _full((1, D))],
        out_specs=[spec, spec, spec, spec2, spec, spec],
        out_shape=[_sds((T, D)), _sds((T, D)), _sds((T, D), BF16), _sds((T, 2 * D), BF16), _sds((T, D), BF16),
                   _sds((T, D), BF16)],
        compiler_params=_cp(1))(tok, y, mod8, dskip, w[0], b, gain, bias)


def _mlp_ln(cfg, name, rows, x1, mod8, cols, w1, w2, gain, bias):
    D, TM = cfg.D, cfg.TM
    csh, csc, cg = cols
    spec, nt, off = _tok_specs(cfg, rows, D)
    specf, _, _ = _tok_specs(cfg, rows, cfg.F)
    fb = cfg.F // 4

    def body(x_ref, mod_ref, w1_ref, w2_ref, g_ref, bi_ref, x2_ref, r2_ref, out_ref, a_ref, h_ref):
        t = x_ref[...]
        h = _mod(t, mod_ref[:, csh:csh + D], mod_ref[:, csc:csc + D]).astype(BF16)
        out = jnp.zeros((TM, D), F32)
        for s in range(4):
            hid = jnp.maximum(_dot(h, w1_ref[s]), 0.0)
            a = (hid * hid).astype(BF16)
            a_ref[:, s * fb:(s + 1) * fb] = a
            out = out + _dot(a, w2_ref[s])
        r2 = DN_ALPHA * t + _rowscale(out, mod_ref[:, cg:cg + D])
        xhat, _ = _ln_stats(r2)
        x2_ref[...] = xhat * g_ref[...] + bi_ref[...]
        r2_ref[...] = r2
        out_ref[...] = out.astype(BF16)
        h_ref[...] = h

    T = cfg.T
    return pl.pallas_call(
        body, name=name, grid=(nt,),
        in_specs=[spec, _mod_spec(cfg, mod8, off), _wspec(w1), _wspec(w2), _full((1, D)), _full((1, D))],
        out_specs=[spec, spec, spec, specf, spec],
        out_shape=[_sds((T, D)), _sds((T, D)), _sds((T, D), BF16), _sds((T, cfg.F), BF16), _sds((T, D), BF16)],
        compiler_params=_cp(1))(x1, mod8, w1[0], w2[0], gain, bias)


def _pw1_glu(cfg, name, rows, tok, mod8, cols, w, b):
    D, TM = cfg.D, cfg.TM
    csh, csc = cols
    spec, nt, off = _tok_specs(cfg, rows, D)
    spec2, _, _ = _tok_specs(cfg, rows, 2 * D)

    def body(tok_ref, mod_ref, w_ref, b_ref, aa_ref, ag_ref, h_ref):
        h = _mod(tok_ref[...], mod_ref[:, csh:csh + D], mod_ref[:, csc:csc + D]).astype(BF16)
        aa = jnp.concatenate([_dot(h, w_ref[s]) for s in range(4)], axis=-1) + b_ref[...]
        aa_ref[...] = aa.astype(BF16)
        ag_ref[...] = aa[:, :D] * _sigmoid(aa[:, D:])
        h_ref[...] = h

    T = cfg.T
    return pl.pallas_call(
        body, name=name, grid=(nt,),
        in_specs=[spec, _mod_spec(cfg, mod8, off), _wspec(w), _full((1, 2 * D))],
        out_specs=[spec2, spec, spec],
        out_shape=[_sds((T, 2 * D), BF16), _sds((T, D)), _sds((T, D), BF16)], compiler_params=_cp(1))(tok, mod8, w[0], b)


def _halo_maps(cfg, rows):
    TM, HB = cfg.TM, cfg.HB
    off = rows[0] // TM
    nct = cfg.Tc // TM
    ntx = cfg.Tx // TM

    def phase(i):
        t = i + off
        is_x = t >= nct
        first = jnp.where(is_x, nct, 0)
        cnt = jnp.where(is_x, ntx, nct)
        return t, first, cnt

    def prev(i):
        t, first, cnt = phase(i)
        return jnp.where(t == first, 2 * (first + cnt) - 1, 2 * t - 1), 0

    def nxt(i):
        t, first, cnt = phase(i)
        return jnp.where(t == first + cnt - 1, 2 * first, 2 * t + 2), 0

    def edge(i):
        t, first, cnt = phase(i)
        return t == first, t == first + cnt - 1

    return prev, nxt, edge, off


def _halo_fix(prev, nxt, is_first, is_last):
    hb, d = prev.shape
    k = lax.broadcasted_iota(jnp.int32, (hb // 8, 8, d), 1)
    p3 = prev.reshape(hb // 8, 8, d)
    n3 = nxt.reshape(hb // 8, 8, d)
    p_roll = jnp.where((k % 4) == 0, 0.0, pltpu.roll(p3, 1, 1))
    n_roll = jnp.where((k % 4) == 3, 0.0, pltpu.roll(n3, 7, 1))
    p3 = jnp.where(is_first, p_roll, p3)
    n3 = jnp.where(is_last, n_roll, n3)
    return p3.reshape(hb, d), n3.reshape(hb, d)


def _dwconv_ln(cfg, name, rows, ag, w_dw, b_dw, ln_g, ln_b):
    D, TM, HB, KW, half = cfg.D, cfg.TM, cfg.HB, cfg.KW, cfg.half
    prev_map, next_map, edge, off = _halo_maps(cfg, rows)
    spec, nt, _ = _tok_specs(cfg, rows, D)

    def body(cur_ref, prev_ref, next_ref, w_ref, b_ref, g_ref, bi_ref, cv_ref, s_ref, ext):
        i = pl.program_id(0)
        is_first, is_last = edge(i)

        @pl.when(i >= 0)
        def _():
            p, n = _halo_fix(prev_ref[...], next_ref[...], is_first, is_last)
            ext[0:HB, :] = p
            ext[HB:HB + TM, :] = cur_ref[...]
            ext[HB + TM:, :] = n

        acc = jnp.zeros((TM, D), F32)
        for k in range(KW):
            lo = HB + 8 * (k - half)
            acc = acc + w_ref[k:k + 1, :] * ext[lo:lo + TM, :]
        cv_ref[...] = acc + b_ref[...]
        xhat, _ = _ln_stats(cv_ref[...])
        nn = xhat * g_ref[...] + bi_ref[...]
        s_ref[...] = (nn * _sigmoid(nn)).astype(BF16)

    T = cfg.T
    return pl.pallas_call(
        body, name=name, grid=(nt,),
        in_specs=[spec, pl.BlockSpec((HB, D), prev_map), pl.BlockSpec((HB, D), next_map), _full((KW, D)),
                  _full((1, D)), _full((1, D)), _full((1, D))],
        out_specs=[spec, spec], out_shape=[_sds((T, D)), _sds((T, D), BF16)],
        scratch_shapes=[pltpu.VMEM((TM + 2 * HB, D), F32)], compiler_params=_cp(1))(ag, ag, ag, w_dw, b_dw, ln_g, ln_b)


def _pw2_ln(cfg, name, rows, s, tok, mod8, cg, w, b, gain, bias):
    D, TM = cfg.D, cfg.TM
    spec, nt, off = _tok_specs(cfg, rows, D)
    kb = D // 4

    def body(s_ref, tok_ref, mod_ref, w_ref, b_ref, g_ref, bi_ref, x1_ref, r1_ref, mix_ref):
        sv = s_ref[...]
        mix = b_ref[...] + jnp.zeros((TM, D), F32)
        for q in range(4):
            mix = mix + _dot(sv[:, q * kb:(q + 1) * kb], w_ref[q])
        r1 = DN_ALPHA * tok_ref[...] + _rowscale(mix, mod_ref[:, cg:cg + D])
        xhat, _ = _ln_stats(r1)
        x1_ref[...] = xhat * g_ref[...] + bi_ref[...]
        r1_ref[...] = r1
        mix_ref[...] = mix.astype(BF16)

    T = cfg.T
    return pl.pallas_call(
        body, name=name, grid=(nt,),
        in_specs=[spec, spec, _mod_spec(cfg, mod8, off), _wspec(w), _full((1, D)), _full((1, D)), _full((1, D))],
        out_specs=[spec, spec, spec], out_shape=[_sds((T, D)), _sds((T, D)), _sds((T, D), BF16)],
        compiler_params=_cp(1))(s, tok, mod8, w[0], b, gain, bias)


def _loss(cfg, xf, tgt):
    D, TM = cfg.D, cfg.TM
    spec, nt, off = _tok_specs(cfg, cfg.rows(False), D)

    def body(x_ref, t_ref, l_ref, dx_ref, acc):
        i = pl.program_id(0)
        dlt = x_ref[...] - t_ref[...]

        @pl.when(i == 0)
        def _():
            acc[...] = jnp.zeros_like(acc)

        acc[...] += _sum8(dlt * dlt)
        dx_ref[...] = dlt * (1.0 / D)

        @pl.when(i == nt - 1)
        def _():
            l_ref[...] = jnp.zeros((8, LANES), F32) + jnp.sum(acc[...]) * (0.5 / D)

    return pl.pallas_call(
        body, name="loss", grid=(nt,),
        in_specs=[spec, pl.BlockSpec((TM, D), lambda i: (i, 0))],
        out_specs=[_full((8, LANES)), spec], out_shape=[_sds((8, LANES)), _sds((cfg.T, D))],
        scratch_shapes=[pltpu.VMEM((8, D), F32)], compiler_params=_cp(1))(xf, tgt)


def _masked_spec(cfg, rows, width, valid_from_tile):
    tm = cfg.TM
    off = rows[0] // tm
    return pl.BlockSpec((tm, width), lambda i: (jnp.maximum(i + off, valid_from_tile), 0))


def _lnb(cfg, name, rows, dres, dres_ctx_ok, dh, r, aux, gain, mod_gate, cg, mod_next, csc):
    D, TM = cfg.D, cfg.TM
    spec, nt, off = _tok_specs(cfg, rows, D)
    nct = cfg.Tc // TM
    has_dres, has_dh = dres is not None, dh is not None

    def body(*refs):
        refs = list(refs)
        dres_ref = refs.pop(0) if has_dres else None
        dh_ref = refs.pop(0) if has_dh else None
        r_ref, aux_ref, g_ref, mg_ref = refs[:4]
        refs = refs[4:]
        mn_ref = refs.pop(0) if has_dh else None
        dprev_ref, dbr_ref, dgain_ref, dbias_ref, dg_ref, dsc_ref, dsh_ref, acc_g, acc_b = refs
        i = pl.program_id(0)
        t = i + off
        first_of_stream = (i == 0) | (t == nct)
        xhat, rstd = _ln_stats(r_ref[...])
        dy = jnp.zeros((TM, D), F32)
        if has_dres:
            dv = dres_ref[...]
            if not dres_ctx_ok:
                dv = jnp.where(t >= nct, dv, 0.0)
            dy = dy + dv
        if has_dh:
            dhv = dh_ref[...]
            dy = dy + _rowscale(dhv, 1.0 + mn_ref[:, csc:csc + D])
            x_out = xhat * g_ref[0:1, :] + g_ref[1:2, :]
            s_sc, s_sh = _sum8(dhv * x_out), _sum8(dhv)
        else:
            s_sc = s_sh = jnp.zeros((8, D), F32)
        dr = _ln_bwd(dy * g_ref[0:1, :], xhat, rstd)
        s_g = _sum8(dr * aux_ref[...].astype(F32))

        @pl.when(i == 0)
        def _():
            acc_g[...] = jnp.zeros_like(acc_g)
            acc_b[...] = jnp.zeros_like(acc_b)

        acc_g[...] += _sum8(dy * xhat)
        acc_b[...] += _sum8(dy)

        @pl.when(first_of_stream)
        def _():
            dg_ref[...] = s_g
            dsc_ref[...] = s_sc
            dsh_ref[...] = s_sh

        @pl.when(jnp.logical_not(first_of_stream))
        def _():
            dg_ref[...] += s_g
            dsc_ref[...] += s_sc
            dsh_ref[...] += s_sh

        dprev_ref[...] = DN_ALPHA * dr
        dbr_ref[...] = _rowscale(dr, mg_ref[:, cg:cg + D])

        @pl.when(i == nt - 1)
        def _():
            dgain_ref[...] = jnp.sum(acc_g[...], axis=0, keepdims=True)
            dbias_ref[...] = jnp.sum(acc_b[...], axis=0, keepdims=True)

    st = _stream_of(cfg, off, TM)
    in_specs, args = [], []
    if has_dres:
        in_specs.append(spec if dres_ctx_ok else _masked_spec(cfg, rows, D, nct))
        args.append(dres)
    if has_dh:
        in_specs.append(spec)
        args.append(dh)
    in_specs += [spec, spec, _full((2, D)), _mod_spec(cfg, mod_gate, off)]
    args += [r, aux, gain, mod_gate]
    if has_dh:
        in_specs.append(_mod_spec(cfg, mod_next, off))
        args.append(mod_next)
    acc_spec = pl.BlockSpec((None, 8, D), lambda i: (st(i), 0, 0))
    T = cfg.T
    return pl.pallas_call(
        body, name=name, grid=(nt,), in_specs=in_specs,
        out_specs=[spec, spec, _full((1, D)), _full((1, D)), acc_spec, acc_spec, acc_spec],
        out_shape=[_sds((T, D)), _sds((T, D)), _sds((1, D)), _sds((1, D)), _sds((2, 8, D)), _sds((2, 8, D)), _sds((2, 8, D))],
        scratch_shapes=[pltpu.VMEM((8, D), F32), pltpu.VMEM((8, D), F32)], compiler_params=_cp(1))(*args)


def _mlp_bwd(cfg, name, rows, dbr, a, w1, w2):
    D, TM = cfg.D, cfg.TM
    spec, nt, off = _tok_specs(cfg, rows, D)
    specf, _, _ = _tok_specs(cfg, rows, cfg.F)
    fb = cfg.F // 4

    def body(d_ref, a_ref, w1_ref, w2_ref, dh_ref, dhid_ref, dout_ref):
        dout = d_ref[...].astype(BF16)
        dh = jnp.zeros((TM, D), F32)
        for s in range(4):
            da = _dot_nt(dout, w2_ref[s])
            dhid = (da * (2.0 * jnp.sqrt(a_ref[:, s * fb:(s + 1) * fb].astype(F32)))).astype(BF16)
            dhid_ref[:, s * fb:(s + 1) * fb] = dhid
            dh = dh + _dot_nt(dhid, w1_ref[s])
        dh_ref[...] = dh
        dout_ref[...] = dout

    T = cfg.T
    return pl.pallas_call(
        body, name=name, grid=(nt,),
        in_specs=[spec, specf, _wspec(w1), _wspec(w2)],
        out_specs=[spec, specf, spec],
        out_shape=[_sds((T, D)), _sds((T, cfg.F), BF16), _sds((T, D), BF16)], compiler_params=_cp(1))(dbr, a, w1[0], w2[0])


def _wgrad(cfg, name, rows, a, b, mode, fam, slot):
    tw = cfg.TW
    off = rows[0] // tw
    ntile = rows[1] // tw
    tps = next(q for q in (4, 3, 2, 1) if ntile % q == 0)
    nt = ntile // tps
    fresh = not hasattr(fam, "dtype")
    fam_shape = tuple(fam) if fresh else fam.shape
    _, n, kk, nn = fam_shape

    def body(*refs):
        a_refs, b_refs, o_ref = refs[:tps], refs[tps:2 * tps], refs[-1]
        t = pl.program_id(1)
        part = _dot_tn(a_refs[0][...], b_refs[0][...])
        for q in range(1, tps):
            part = part + _dot_tn(a_refs[q][...], b_refs[q][...])

        @pl.when(t == 0)
        def _():
            o_ref[...] = part

        @pl.when(t > 0)
        def _():
            o_ref[...] += part

    def row(q):
        return lambda s, t: t * tps + q + off

    if mode == "col":
        a_specs = [pl.BlockSpec((tw, kk), lambda s, t, r=row(q): (r(s, t), 0)) for q in range(tps)]
        b_specs = [pl.BlockSpec((tw, nn), lambda s, t, r=row(q): (r(s, t), s)) for q in range(tps)]
    else:
        a_specs = [pl.BlockSpec((tw, kk), lambda s, t, r=row(q): (r(s, t), s)) for q in range(tps)]
        b_specs = [pl.BlockSpec((tw, nn), lambda s, t, r=row(q): (r(s, t), 0)) for q in range(tps)]
    out_spec = pl.BlockSpec((None, None, kk, nn), lambda s, t: (s, slot, 0, 0))
    ins = [a] * tps + [b] * tps
    if fresh:
        return pl.pallas_call(body, name=name, grid=(4, nt), in_specs=a_specs + b_specs, out_specs=out_spec,
                              out_shape=_sds(fam_shape), compiler_params=_cp(2))(*ins)
    return pl.pallas_call(body, name=name, grid=(4, nt), in_specs=a_specs + b_specs + [ANY], out_specs=out_spec,
                          out_shape=_sds(fam_shape), input_output_aliases={2 * tps: 0}, compiler_params=_cp(2))(*ins, fam)


def _glu_bwd(cfg, name, rows, dmix, pre, w, yy=None):
    D, TM = cfg.D, cfg.TM
    spec, nt, off = _tok_specs(cfg, rows, D)
    spec2, _, _ = _tok_specs(cfg, rows, 2 * D)
    hw = w[0].shape[-1]
    has_y = yy is not None

    def body(*refs):
        refs = list(refs)
        d_ref, p_ref, w_ref = refs[:3]
        y_ref = refs[3] if has_y else None
        dz_ref, dp_ref, db_ref, acc = refs[-4:]
        i = pl.program_id(0)
        dm = d_ref[...]
        po, pg = p_ref[:, :D].astype(F32), p_ref[:, D:].astype(F32)
        sg = _sigmoid(pg)
        dpre = jnp.concatenate([dm * sg, dm * po * sg * (1.0 - sg)], axis=-1)

        @pl.when(i == 0)
        def _():
            acc[...] = jnp.zeros_like(acc)

        acc[...] += _sum8(dpre)
        dpb = dpre.astype(BF16)
        dz = jnp.zeros((TM, D), F32)
        for s in range(4):
            dz = dz + _dot_nt(dpb[:, s * hw:(s + 1) * hw], w_ref[s])
        if has_y:
            dz = dz * _gelu_grad(y_ref[...].astype(F32))
        dz_ref[...] = dz
        dp_ref[...] = dpb

        @pl.when(i == nt - 1)
        def _():
            db_ref[...] = jnp.sum(acc[...], axis=0, keepdims=True)

    T = cfg.T
    in_specs = [spec, spec2, _wspec(w)] + ([spec] if has_y else [])
    args = [dmix, pre, w[0]] + ([yy] if has_y else [])
    return pl.pallas_call(
        body, name=name, grid=(nt,), in_specs=in_specs, out_specs=[spec, spec2, _full((1, 2 * D))],
        out_shape=[_sds((T, D)), _sds((T, 2 * D), BF16), _sds((1, 2 * D))],
        scratch_shapes=[pltpu.VMEM((8, 2 * D), F32)], compiler_params=_cp(1))(*args)


def _s5_du(cfg, name, rows, du, dy, dy_from_tile, tok, mod8, cols, dskip):
    D, TM = cfg.D, cfg.TM
    csh, csc = cols
    spec, nt, off = _tok_specs(cfg, rows, D)

    def body(du_ref, dy_ref, tok_ref, mod_ref, ds_ref, dh_ref, dd_ref, acc):
        i = pl.program_id(0)
        dyv = jnp.where(i + off >= dy_from_tile, dy_ref[...], 0.0)
        u = _mod(tok_ref[...], mod_ref[:, csh:csh + D], mod_ref[:, csc:csc + D])
        dh_ref[...] = du_ref[0] + du_ref[1] + ds_ref[...] * dyv

        @pl.when(i == 0)
        def _():
            acc[...] = jnp.zeros_like(acc)

        acc[...] += _sum8(dyv * u)

        @pl.when(i == nt - 1)
        def _():
            dd_ref[...] = jnp.sum(acc[...], axis=0, keepdims=True)

    T = cfg.T
    return pl.pallas_call(
        body, name=name, grid=(nt,),
        in_specs=[pl.BlockSpec((2, TM, D), lambda i: (0, i + off, 0)), _masked_spec(cfg, rows, D, dy_from_tile), spec,
                  _mod_spec(cfg, mod8, off), _full((1, D))],
        out_specs=[spec, _full((1, D))], out_shape=[_sds((T, D)), _sds((1, D))],
        scratch_shapes=[pltpu.VMEM((8, D), F32)], compiler_params=_cp(1))(du, dy, tok, mod8, dskip)


def _pw2_bwd(cfg, name, rows, dmix, cv, w, ln_g, ln_b):
    D, TM = cfg.D, cfg.TM
    spec, nt, off = _tok_specs(cfg, rows, D)
    kb = D // 4

    def body(d_ref, cv_ref, w_ref, g_ref, b_ref, dcv_ref, dmb_ref, sums_ref, acc):
        i = pl.program_id(0)
        dm = d_ref[...]
        dmb = dm.astype(BF16)
        ds = jnp.concatenate([_dot_nt(dmb, w_ref[q]) for q in range(4)], axis=-1)
        xhat, rstd = _ln_stats(cv_ref[...])
        nn = xhat * g_ref[...] + b_ref[...]
        sg = _sigmoid(nn)
        dn = ds * (sg * (1.0 + nn * (1.0 - sg)))
        dcv = _ln_bwd(dn * g_ref[...], xhat, rstd)

        @pl.when(i == 0)
        def _():
            acc[...] = jnp.zeros_like(acc)

        acc[0] += _sum8(dn * xhat)
        acc[1] += _sum8(dn)
        acc[2] += _sum8(dcv)
        acc[3] += _sum8(dm)
        dcv_ref[...] = dcv
        dmb_ref[...] = dmb

        @pl.when(i == nt - 1)
        def _():
            for q in range(4):
                sums_ref[q:q + 1, :] = jnp.sum(acc[q], axis=0, keepdims=True)

    T = cfg.T
    return pl.pallas_call(
        body, name=name, grid=(nt,),
        in_specs=[spec, spec, _wspec(w), _full((1, D)), _full((1, D))],
        out_specs=[spec, spec, _full((4, D))], out_shape=[_sds((T, D)), _sds((T, D), BF16), _sds((4, D))],
        scratch_shapes=[pltpu.VMEM((4, 8, D), F32)], compiler_params=_cp(1))(dmix, cv, w[0], ln_g, ln_b)


def _dwconv_bwd(cfg, name, rows, dcv, ag, w_dw):
    D, TM, HB, KW, half = cfg.D, cfg.TM, cfg.HB, cfg.KW, cfg.half
    prev_map, next_map, edge, off = _halo_maps(cfg, rows)
    spec, nt, _ = _tok_specs(cfg, rows, D)

    def body(dc_ref, dp_ref, dn_ref, ac_ref, ap_ref, an_ref, w_ref, dag_ref, dw_ref, extd, exta, acc):
        i = pl.program_id(0)
        is_first, is_last = edge(i)

        @pl.when(i >= 0)
        def _():
            p, n = _halo_fix(dp_ref[...], dn_ref[...], is_first, is_last)
            extd[0:HB, :] = p
            extd[HB:HB + TM, :] = dc_ref[...]
            extd[HB + TM:, :] = n
            p, n = _halo_fix(ap_ref[...], an_ref[...], is_first, is_last)
            exta[0:HB, :] = p
            exta[HB:HB + TM, :] = ac_ref[...]
            exta[HB + TM:, :] = n

        @pl.when(i == 0)
        def _():
            acc[...] = jnp.zeros_like(acc)

        cr = min(CONV_ROWS, TM)
        for r0 in range(0, TM, cr):
            for lc in range(D // LANES):
                ls = pl.ds(lc * LANES, LANES)
                dcur = dc_ref[r0:r0 + cr, ls]
                dag = jnp.zeros((cr, LANES), F32)
                for k in range(KW):
                    lo = r0 + HB + 8 * (half - k)
                    la = r0 + HB + 8 * (k - half)
                    dag = dag + w_ref[k:k + 1, ls] * extd[lo:lo + cr, ls]
                    acc[k, :, ls] += _sum8(dcur * exta[la:la + cr, ls])
                dag_ref[r0:r0 + cr, ls] = dag

        @pl.when(i == nt - 1)
        def _():
            for k in range(KW):
                dw_ref[k:k + 1, :] = jnp.sum(acc[k], axis=0, keepdims=True)

    T = cfg.T
    hp, hn = pl.BlockSpec((HB, D), prev_map), pl.BlockSpec((HB, D), next_map)
    return pl.pallas_call(
        body, name=name, grid=(nt,), in_specs=[spec, hp, hn, spec, hp, hn, _full((KW, D))],
        out_specs=[spec, _full((KW, D))], out_shape=[_sds((T, D)), _sds((KW, D))],
        scratch_shapes=[pltpu.VMEM((TM + 2 * HB, D), F32), pltpu.VMEM((TM + 2 * HB, D), F32), pltpu.VMEM((KW, 8, D), F32)],
        compiler_params=_cp(1))(dcv, dcv, dcv, ag, ag, ag, w_dw)


def _input_bwd(cfg, dres, dh, tok0, mod8, csc):
    D, TM = cfg.D, cfg.TM
    rows = cfg.rows(True)
    spec, nt, off = _tok_specs(cfg, rows, D)
    nct = cfg.Tc // TM
    st = _stream_of(cfg, off, TM)

    def body(dr_ref, dh_ref, t_ref, mod_ref, gx_ref, dsc_ref, dsh_ref):
        i = pl.program_id(0)
        dhv = dh_ref[...]
        gx_ref[...] = dr_ref[...] + _rowscale(dhv, 1.0 + mod_ref[:, csc:csc + D])
        first = (i == 0) | (i == nct)
        s_sc, s_sh = _sum8(dhv * t_ref[...]), _sum8(dhv)

        @pl.when(first)
        def _():
            dsc_ref[...] = s_sc
            dsh_ref[...] = s_sh

        @pl.when(jnp.logical_not(first))
        def _():
            dsc_ref[...] += s_sc
            dsh_ref[...] += s_sh

    acc_spec = pl.BlockSpec((None, 8, D), lambda i: (st(i), 0, 0))
    return pl.pallas_call(
        body, name="input_bwd", grid=(nt,), in_specs=[spec, spec, spec, _mod_spec(cfg, mod8, off)],
        out_specs=[spec, acc_spec, acc_spec], out_shape=[_sds((cfg.T, D)), _sds((2, 8, D)), _sds((2, 8, D))],
        compiler_params=_cp(1))(dres, dh, tok0, mod8)


def _dmod_rows(dmod8):
    nl, _, _, w = dmod8.shape

    def body(d_ref, o_ref):
        xs = d_ref[1]
        cs = d_ref[0]
        o_ref[...] = jnp.zeros((8, w), F32)
        o_ref[0:1, :] = jnp.sum(xs[0:4], axis=0, keepdims=True)
        o_ref[1:2, :] = jnp.sum(xs[4:8], axis=0, keepdims=True)
        o_ref[2:3, :] = jnp.sum(cs, axis=0, keepdims=True)

    return pl.pallas_call(body, name="dmod_rows", grid=(nl,),
                          in_specs=[pl.BlockSpec((None, 2, 8, w), lambda l: (l, 0, 0, 0))],
                          out_specs=pl.BlockSpec((None, 8, w), lambda l: (l, 0, 0)), out_shape=_sds((nl, 8, w)),
                          compiler_params=_cp(1))(dmod8)


def _x_only(acc):
    return jnp.concatenate([jnp.zeros_like(acc[:1]), acc[1:]], axis=0)


def _pack(parts):
    bufs, meta, off = [], [], 0
    for p in parts:
        n = p.size
        rows = -(-n // (8 * LANES)) * 8
        flat = p.reshape(-1).astype(F32)
        if rows * LANES != n:
            flat = jnp.pad(flat, (0, rows * LANES - n))
        flat = flat.reshape(rows, LANES)
        bufs.append(flat)
        meta.append((off, rows, p.shape))
        off += rows
    if off % 16:
        bufs.append(jnp.zeros((8, LANES), F32))
    return jnp.concatenate(bufs, axis=0), meta


def _unpack(buf, meta):
    out = []
    for off, rows, shape in meta:
        n = 1
        for s in shape:
            n *= s
        out.append(buf[off:off + rows].reshape(-1)[:n].reshape(shape))
    return out


def kernel(x, c, ctx, c_ctx, w_ada, b_ada, ln_gain, ln_bias, s5_lam_re, s5_lam_im, s5_log_dt, s5_b_re, s5_b_im, s5_c_re, s5_c_im, s5_d, s5_w_glu, s5_b_glu, cv_w_pw1, cv_b_pw1, cv_w_dw, cv_b_dw, cv_ln_g, cv_ln_b, cv_w_pw2, cv_b_pw2, mlp_w1, mlp_w2, loss_target, m_c_ctx, m_w_ada, m_b_ada, m_ln_gain, m_ln_bias, m_s5_lam_re, m_s5_lam_im, m_s5_log_dt, m_s5_b_re, m_s5_b_im, m_s5_c_re, m_s5_c_im, m_s5_d, m_s5_w_glu, m_s5_b_glu, m_cv_w_pw1, m_cv_b_pw1, m_cv_w_dw, m_cv_b_dw, m_cv_ln_g, m_cv_ln_b, m_cv_w_pw2, m_cv_b_pw2, m_mlp_w1, m_mlp_w2, v_c_ctx, v_w_ada, v_b_ada, v_ln_gain, v_ln_bias, v_s5_lam_re, v_s5_lam_im, v_s5_log_dt, v_s5_b_re, v_s5_b_im, v_s5_c_re, v_s5_c_im, v_s5_d, v_s5_w_glu, v_s5_b_glu, v_cv_w_pw1, v_cv_b_pw1, v_cv_w_dw, v_cv_b_dw, v_cv_ln_g, v_cv_ln_b, v_cv_w_pw2, v_cv_b_pw2, v_mlp_w1, v_mlp_w2):
    cfg = _Cfg(x, ctx, mlp_w1, cv_w_dw)
    D, T, Tc, Tx, B = cfg.D, cfg.T, cfg.Tc, cfg.Tx, cfg.B
    ax, ay, ac = lax.axis_index("x"), lax.axis_index("y"), lax.axis_index("c")
    shard = 2 * ax + ay
    dev = 4 * ax + 2 * ay + ac
    Ds = D // 4
    Wa = w_ada.shape[2]

    c_pad = jnp.concatenate([c, jnp.zeros((8 - B, D), F32)], axis=0)
    c_gath = _allgather8("gather_c", c_pad).reshape(8, 8, D)[:, :B].reshape(8 * B, D)
    c_all = jnp.concatenate([c_gath, c_ctx[None], jnp.zeros((7, D), F32)], axis=0)
    b_sh = lax.dynamic_slice_in_dim(b_ada, shard * Wa, Wa, axis=1)[:, None, :]
    mod_sh = _ada_fwd(c_all, w_ada, b_sh)
    mod_g = _allgather8("gather_mod", mod_sh.reshape(DEPTH * 24, Wa)).reshape(4, 2, DEPTH, 24, Wa)[:, 0]
    mods = mod_g.transpose(1, 2, 0, 3).reshape(DEPTH, 24, 4 * Wa)
    mine = lax.dynamic_slice_in_dim(mods, B * dev, B, axis=1)
    mod8 = jnp.stack([jnp.broadcast_to(mods[:, 16:17], (DEPTH, 8, 6 * D)), jnp.repeat(mine, 4, axis=1)], axis=1)
    SH1, SC1, G1, SH2, SC2, G2 = (k * D for k in range(6))

    small_parts = [ln_gain.reshape(-1, Ds), ln_bias.reshape(-1, Ds), cv_b_pw1.reshape(-1, Ds), cv_w_dw.reshape(-1, Ds),
                   cv_b_dw, cv_ln_g, cv_ln_b, cv_b_pw2]
    small_rows = [p.shape[0] for p in small_parts]
    sm = jnp.concatenate(small_parts, axis=0)
    pad_r = -sm.shape[0] % 8
    sm = jnp.pad(sm, ((0, pad_r), (0, 0)))
    sm_g = _allgather8("gather_small", sm).reshape(4, 2, sm.shape[0], Ds)[:, 0]
    pieces, o = [], 0
    for nr in small_rows:
        pieces.append(sm_g[:, o:o + nr])
        o += nr

    def unshard(p, lead):
        return p.reshape((4,) + lead + (Ds,)).transpose(tuple(range(1, len(lead) + 1)) + (0, len(lead) + 1)).reshape(lead + (4 * Ds,))

    ln_gain_f = unshard(pieces[0], (DEPTH, 2))
    ln_bias_f = unshard(pieces[1], (DEPTH, 2))
    nconv = cv_w_dw.shape[0]
    b_pw1_f = pieces[2].reshape(4, nconv, 2 * D // 4).transpose(1, 0, 2).reshape(nconv, 2 * D)
    w_dw_f = unshard(pieces[3], (nconv, cfg.KW))
    b_dw_f, cvg_f, cvb_f, b_pw2_f = (unshard(p, (nconv,)) for p in pieces[4:8])

    ns5 = s5_w_glu.shape[0]
    assert mlp_w1.shape[1:] == mlp_w2.shape[1:]
    fam_a = _place_shard("place_w1", mlp_w1, None, 0, 2 * DEPTH)
    fam_a = _place_shard("place_w2", mlp_w2, fam_a, DEPTH, 2 * DEPTH)
    fam_b = _place_shard("place_wglu", s5_w_glu, None, 0, ns5 + nconv)
    fam_b = _place_shard("place_wpw1", cv_w_pw1, fam_b, ns5, ns5 + nconv)
    fam_c = _place_shard("place_wpw2", cv_w_pw2, None, 0, nconv)
    ov = _Overlap()
    gather_tokens = ov.add("gather_b", _gather_gen("gatherb", [fam_b])) + ov.add("gather", _gather_gen("gatherw", [fam_a, fam_c]))

    pos = jnp.broadcast_to(_pos_embed(cfg.L // GRID_W, D)[None], (B, cfg.L, D))
    tok_in = jnp.concatenate([_to_perm(ctx), _to_perm(x)], axis=0)
    pos_in = jnp.concatenate([jnp.zeros((Tc, D), F32), _to_perm(pos)], axis=0)
    tok0 = _ew("add_pos", lambda a, b: a + b, [tok_in, pos_in], [_sds((T, D))])[0]
    mod8 = _tie(mod8, gather_tokens)
    tgt = _to_perm(loss_target)

    def lead(t):
        return t.reshape((2 * ns5,) + t.shape[2:])

    s5_lay = _s5_layouts(cfg, lead(s5_lam_re), lead(s5_lam_im), lead(s5_log_dt), lead(s5_b_re), lead(s5_b_im))
    abr, abi, bbr, bbi = _disc_fwd(*s5_lay)
    acoef_all, acoef_adj_all = _coef_rows(cfg, abr, abi, False), _coef_rows(cfg, abr, abi, True)
    bf_all, cf_all = _blockdiag_b(cfg, bbr, bbi), _blockdiag_c(cfg, lead(s5_c_re), lead(s5_c_im))
    s5p = [dict(acoef=acoef_all[2 * j:2 * j + 2], acoef_adj=acoef_adj_all[2 * j:2 * j + 2], bf=bf_all[2 * j:2 * j + 2],
                cf=cf_all[2 * j:2 * j + 2]) for j in range(ns5)]

    kinds = ["s5" if i % 2 == 0 else "conv" for i in range(DEPTH)]
    tok = tok0
    saved = []
    s5_j = cv_j = 0
    for i in range(DEPTH):
        later_s5 = any(k == "s5" for k in kinds[i + 1:])
        rows = cfg.rows(later_s5)
        m8 = mod8[i]
        sv = dict(tok=tok, rows=rows, kind=kinds[i])
        g0, b0 = ln_gain_f[i, 0][None], ln_bias_f[i, 0][None]
        g1, b1 = ln_gain_f[i, 1][None], ln_bias_f[i, 1][None]
        if kinds[i] == "s5":
            j = s5_j
            s5_j += 1
            p = s5p[j]
            y, ck = _s5_forward(cfg, f"l{i}", tok, m8, SH1, SC1, p["bf"], p["cf"], p["acoef"], ov.point if i == 0 else None)
            m8g = m8
            if i == 0:
                m8g = _tie(m8, ov.point(y))
                (wb_full,) = ov.finish("gather_b", y)
            wg = (wb_full, j)
            x1, r1, mix, zz, zb, yy = _glu_ln(cfg, f"l{i}_glu", rows, tok, y, m8g, (SH1, SC1, G1), s5_d[j][None], wg,
                                              s5_b_glu[j][None], g0, b0)
            sv.update(j=j, ck=ck, zz=zz, zb=zb, yy=yy, wg=wg)
            if i == 0:
                wa_full, wc_full = ov.finish("gather", x1)
        else:
            j = cv_j
            cv_j += 1
            w1c, w2c = (wb_full, ns5 + j), (wc_full, j)
            aa, ag, hb = _pw1_glu(cfg, f"l{i}_pw1", rows, tok, m8, (SH1, SC1), w1c, b_pw1_f[j][None])
            cvv, sb = _dwconv_ln(cfg, f"l{i}_dw", rows, ag, w_dw_f[j], b_dw_f[j][None], cvg_f[j][None], cvb_f[j][None])
            x1, r1, mix = _pw2_ln(cfg, f"l{i}_pw2", rows, sb, tok, m8, G1, w2c, b_pw2_f[j][None], g0, b0)
            sv.update(j=j, aa=aa, ag=ag, hb=hb, cvv=cvv, sb=sb, w1c=w1c, w2c=w2c)
        w1m, w2m = (wa_full, i), (wa_full, DEPTH + i)
        x2, r2, mout, am, hm = _mlp_ln(cfg, f"l{i}_mlp", rows, x1, m8, (SH2, SC2, G2), w1m, w2m, g1, b1)
        sv.update(r1=r1, mix=mix, x1=x1, r2=r2, mout=mout, am=am, hm=hm, w1m=w1m, w2m=w2m, g0=g0, b0=b0, g1=g1, b1=b1)
        saved.append(sv)
        tok = x2

    loss8, dxf = _loss(cfg, tok, tgt)
    loss = lax.psum(loss8[0, 0], ("x", "y", "c"))

    dmod8 = [None] * DEPTH
    g_ln_gain = [[None, None] for _ in range(DEPTH)]
    g_ln_bias = [[None, None] for _ in range(DEPTH)]
    g_s5 = [None] * ns5
    g_cv = [None] * nconv
    dres, dh = dxf, None
    pend = []
    for i in reversed(range(DEPTH)):
        sv = saved[i]
        rows = sv["rows"]
        m8 = mod8[i]
        nxt_m8 = mod8[i + 1] if i + 1 < DEPTH else None
        ctx_ok = True if i + 1 >= DEPTH else (saved[i + 1]["rows"][0] == 0)
        if rows[0] != 0:
            ctx_ok = True
        dprev, dbr, dgn, dbs, dg2, dsc_n, dsh_n = _lnb(
            cfg, f"l{i}_lnb2", rows, dres, ctx_ok, dh, sv["r2"], sv["mout"],
            _tie(jnp.concatenate([sv["g1"], sv["b1"]], 0), pend), m8, G2, nxt_m8, SC1)
        if rows[0] != 0:
            dg2, dsc_n, dsh_n = (_x_only(t) for t in (dg2, dsc_n, dsh_n))
        g_ln_gain[i][1], g_ln_bias[i][1] = dgn[0], dbs[0]
        if i + 1 < DEPTH:
            dmod8[i + 1]["sc1"], dmod8[i + 1]["sh1"] = dsc_n, dsh_n
        dmod8[i] = dict(g2=dg2)
        dh2, dhid, dout = _mlp_bwd(cfg, f"l{i}_mlpb", rows, dbr, sv["am"], sv["w1m"], sv["w2m"])
        pend = ov.point(dh2)
        ga = _wgrad(cfg, f"l{i}_gw1", rows, sv["hm"], dhid, "col", (4, 2, D, cfg.F // 4), 0)
        ga = _wgrad(cfg, f"l{i}_gw2", rows, sv["am"], dout, "row", ga, 1)
        dprev1, dbr1, dgn, dbs, dg1, dsc2, dsh2 = _lnb(
            cfg, f"l{i}_lnb1", rows, dprev, True, dh2, sv["r1"], sv["mix"],
            _tie(jnp.concatenate([sv["g0"], sv["b0"]], 0), pend), m8, G1, m8, SC2)
        if rows[0] != 0:
            dg1, dsc2, dsh2 = (_x_only(t) for t in (dg1, dsc2, dsh2))
        g_ln_gain[i][0], g_ln_bias[i][0] = dgn[0], dbs[0]
        dmod8[i].update(g1=dg1, sc2=dsc2, sh2=dsh2)
        j = sv["j"]
        if sv["kind"] == "s5":
            p = s5p[j]
            dyy, dzz, dbglu = _glu_bwd(cfg, f"l{i}_glub", rows, dbr1, sv["zz"], sv["wg"], sv["yy"])
            pend = ov.point(dyy)
            gb = _wgrad(cfg, f"l{i}_gwg", rows, sv["zb"], dzz, "col", (4, 1, D, D // 2), 0)
            du, (da, dbf, dcf) = _s5_backward(cfg, f"l{i}", dyy, rows[0] == 0, sv["tok"], m8, SH1, SC1, p["bf"], p["cf"],
                                              p["acoef"], _tie(p["acoef_adj"], pend), sv["ck"])
            dh, dds = _s5_du(cfg, f"l{i}_du", cfg.rows(True), du, dyy, rows[0] // cfg.TM, sv["tok"], m8, (SH1, SC1),
                             s5_d[j][None])
            g_s5[j] = dict(da=da, dbf=dbf, dcf=dcf, dd=dds[0], dbglu=dbglu[0])
            layer_grads = [ga, gb]
        else:
            dcv, dmb, sums = _pw2_bwd(cfg, f"l{i}_pw2b", rows, dbr1, sv["cvv"], sv["w2c"], cvg_f[j][None], cvb_f[j][None])
            pend = ov.point(dcv)
            gc = _wgrad(cfg, f"l{i}_gwp2", rows, sv["sb"], dmb, "row", (4, 1, D // 4, D), 0)
            dag, dwdw = _dwconv_bwd(cfg, f"l{i}_dwb", rows, dcv, sv["ag"], _tie(w_dw_f[j], pend))
            dh, daa, dbpw1 = _glu_bwd(cfg, f"l{i}_pw1b", rows, dag, sv["aa"], sv["w1c"])
            gb = _wgrad(cfg, f"l{i}_gwp1", rows, sv["hb"], daa, "col", (4, 1, D, D // 2), 0)
            g_cv[j] = dict(ln_g=sums[0], ln_b=sums[1], b_dw=sums[2], b_pw2=sums[3], w_dw=dwdw, b_pw1=dbpw1[0])
            layer_grads = [ga, gb, gc]
        dres = dprev1
        pend = ov.point(dh) + ov.add(f"rs{i}", _reduce_scatter_gen(f"gw{i}", layer_grads))
    gx_perm, dsc0, dsh0 = _input_bwd(cfg, dres, dh, tok0, _tie(mod8[0], pend), SC1)
    dmod8[0]["sc1"], dmod8[0]["sh1"] = dsc0, dsh0
    grad_x = _from_perm(gx_perm[Tc:], B, cfg.L)

    zero28 = jnp.zeros((2, 8, D), F32)
    dm8 = jnp.stack([jnp.concatenate([dmod8[i].get(k, zero28) for k in ("sh1", "sc1", "g1", "sh2", "sc2", "g2")], axis=-1)
                     for i in range(DEPTH)])
    dm_rows = _dmod_rows(dm8)
    dm_tab = jnp.zeros((DEPTH, 24, 6 * D), F32)
    dm_tab = lax.dynamic_update_slice_in_dim(dm_tab, dm_rows[:, 0:B], B * dev, axis=1)
    dm_tab = lax.dynamic_update_slice_in_dim(dm_tab, dm_rows[:, 2:3], 16, axis=1)

    dbbr, dbbi = _diag_b(cfg, jnp.concatenate([g["dbf"] for g in g_s5], axis=0))
    dcr, dci = _diag_c(cfg, jnp.concatenate([g["dcf"] for g in g_s5], axis=0))
    eye_parts = [jnp.concatenate([g["da"] for g in g_s5], axis=0), dbbr, dbbi, dcr, dci,
                 jnp.stack([g["dd"] for g in g_s5]), jnp.stack([g["dbglu"] for g in g_s5])]
    for j in range(nconv):
        g = g_cv[j]
        eye_parts += [g["ln_g"], g["ln_b"], g["b_dw"], g["b_pw2"], g["w_dw"], g["b_pw1"]]
    eye_parts += [jnp.stack([jnp.stack(r) for r in g_ln_gain]), jnp.stack([jnp.stack(r) for r in g_ln_bias]), dm_tab]
    buf, meta = _pack(eye_parts)
    buf = _tie(buf, ov.point(gx_perm))
    red_buf = _allreduce8("small", buf, ov.point)
    reduced = [ov.finish(f"rs{i}", red_buf) for i in range(DEPTH)]
    red = _unpack(red_buf, meta)

    grads = {}
    nd = 2 * ns5
    da, dbbr, dbbi, dcr, dci, dd, dbglu = red[0:7]
    k = 7
    da_s = _sublane_sum("s5_dasum", da.reshape(2 * nd, 8, cfg.NS)).reshape(nd, 2, cfg.NS)
    g_abr = da_s[:, 0].reshape(nd, cfg.G, cfg.P).transpose(2, 0, 1).reshape(cfg.P, nd * cfg.G)
    g_abi = da_s[:, 1].reshape(nd, cfg.G, cfg.P).transpose(2, 0, 1).reshape(cfg.P, nd * cfg.G)
    glr, gli, gldt, gbr, gbi = _disc_bwd(*s5_lay, g_abr, g_abi, dbbr, dbbi)
    grads.update(s5_lam_re=glr.reshape(cfg.P, nd, cfg.G).transpose(1, 2, 0), s5_lam_im=gli.reshape(cfg.P, nd, cfg.G).transpose(1, 2, 0),
                 s5_log_dt=gldt, s5_b_re=gbr.reshape(S5_GROUP, cfg.P, nd, cfg.G).transpose(2, 3, 1, 0),
                 s5_b_im=gbi.reshape(S5_GROUP, cfg.P, nd, cfg.G).transpose(2, 3, 1, 0), s5_c_re=dcr, s5_c_im=dci,
                 s5_d=dd, s5_b_glu=dbglu)

    def my_cols(full, width):
        return lax.dynamic_slice_in_dim(full, shard * width, width, axis=full.ndim - 1)

    cvs = {n: [] for n in ("ln_g", "ln_b", "b_dw", "b_pw2", "w_dw", "b_pw1")}
    for j in range(nconv):
        for n, val in zip(("ln_g", "ln_b", "b_dw", "b_pw2", "w_dw", "b_pw1"), red[k:k + 6]):
            cvs[n].append(val)
        k += 6
    grads.update(cv_ln_g=my_cols(jnp.stack(cvs["ln_g"]), Ds), cv_ln_b=my_cols(jnp.stack(cvs["ln_b"]), Ds),
                 cv_b_dw=my_cols(jnp.stack(cvs["b_dw"]), Ds), cv_b_pw2=my_cols(jnp.stack(cvs["b_pw2"]), Ds),
                 cv_w_dw=my_cols(jnp.stack(cvs["w_dw"]), Ds), cv_b_pw1=my_cols(jnp.stack(cvs["b_pw1"]), 2 * D // 4))
    grads.update(ln_gain=my_cols(red[k], Ds), ln_bias=my_cols(red[k + 1], Ds))
    dm_all = red[k + 2]

    dm_sh = lax.dynamic_slice_in_dim(dm_all, shard * Wa, Wa, axis=2)
    gw_ada, dcond = _ada_bwd(c_all, dm_sh, w_ada)
    grads["w_ada"] = gw_ada
    grads["b_ada"] = _colsum_groups("ada_bsum", dm_all)
    dc_part = dcond[0:1]
    dc_buf = jnp.concatenate([jnp.where(ac == 0, dc_part, 0.0), jnp.zeros((7, D), F32)], axis=0)
    dc_tot = _allreduce8("cctx", dc_buf.reshape(8 * D // LANES, LANES)).reshape(8, D)[0:1]
    grads["c_ctx"] = _ew("cctx_grad", lambda g, cv: g * (_sigmoid(cv) * (1.0 + cv * (1.0 - _sigmoid(cv)))),
                         [jnp.broadcast_to(dc_tot, (8, D)), jnp.broadcast_to(c_ctx[None], (8, D))], [_sds((8, D))])[0][0]

    s5_layers = [i for i in range(DEPTH) if kinds[i] == "s5"]
    cv_layers = [i for i in range(DEPTH) if kinds[i] == "conv"]
    grads.update(mlp_w1=jnp.stack([reduced[i][0][0] for i in range(DEPTH)]),
                 mlp_w2=jnp.stack([reduced[i][0][1] for i in range(DEPTH)]),
                 s5_w_glu=jnp.stack([reduced[i][1][0] for i in s5_layers]),
                 cv_w_pw1=jnp.stack([reduced[i][1][0] for i in cv_layers]),
                 cv_w_pw2=jnp.stack([reduced[i][2][0] for i in cv_layers]))

    weights = dict(c_ctx=c_ctx, w_ada=w_ada, b_ada=b_ada, ln_gain=ln_gain, ln_bias=ln_bias, s5_lam_re=s5_lam_re,
                   s5_lam_im=s5_lam_im, s5_log_dt=s5_log_dt, s5_b_re=s5_b_re, s5_b_im=s5_b_im, s5_c_re=s5_c_re,
                   s5_c_im=s5_c_im, s5_d=s5_d, s5_w_glu=s5_w_glu, s5_b_glu=s5_b_glu, cv_w_pw1=cv_w_pw1, cv_b_pw1=cv_b_pw1,
                   cv_w_dw=cv_w_dw, cv_b_dw=cv_b_dw, cv_ln_g=cv_ln_g, cv_ln_b=cv_ln_b, cv_w_pw2=cv_w_pw2, cv_b_pw2=cv_b_pw2,
                   mlp_w1=mlp_w1, mlp_w2=mlp_w2)
    ms = dict(c_ctx=m_c_ctx, w_ada=m_w_ada, b_ada=m_b_ada, ln_gain=m_ln_gain, ln_bias=m_ln_bias, s5_lam_re=m_s5_lam_re,
              s5_lam_im=m_s5_lam_im, s5_log_dt=m_s5_log_dt, s5_b_re=m_s5_b_re, s5_b_im=m_s5_b_im, s5_c_re=m_s5_c_re,
              s5_c_im=m_s5_c_im, s5_d=m_s5_d, s5_w_glu=m_s5_w_glu, s5_b_glu=m_s5_b_glu, cv_w_pw1=m_cv_w_pw1,
              cv_b_pw1=m_cv_b_pw1, cv_w_dw=m_cv_w_dw, cv_b_dw=m_cv_b_dw, cv_ln_g=m_cv_ln_g, cv_ln_b=m_cv_ln_b,
              cv_w_pw2=m_cv_w_pw2, cv_b_pw2=m_cv_b_pw2, mlp_w1=m_mlp_w1, mlp_w2=m_mlp_w2)
    vs = dict(c_ctx=v_c_ctx, w_ada=v_w_ada, b_ada=v_b_ada, ln_gain=v_ln_gain, ln_bias=v_ln_bias, s5_lam_re=v_s5_lam_re,
              s5_lam_im=v_s5_lam_im, s5_log_dt=v_s5_log_dt, s5_b_re=v_s5_b_re, s5_b_im=v_s5_b_im, s5_c_re=v_s5_c_re,
              s5_c_im=v_s5_c_im, s5_d=v_s5_d, s5_w_glu=v_s5_w_glu, s5_b_glu=v_s5_b_glu, cv_w_pw1=v_cv_w_pw1,
              cv_b_pw1=v_cv_b_pw1, cv_w_dw=v_cv_w_dw, cv_b_dw=v_cv_b_dw, cv_ln_g=v_cv_ln_g, cv_ln_b=v_cv_ln_b,
              cv_w_pw2=v_cv_w_pw2, cv_b_pw2=v_cv_b_pw2, mlp_w1=v_mlp_w1, mlp_w2=v_mlp_w2)
    names = list(weights)
    deltas, new_m, new_v = {}, {}, {}
    for n in names:
        g = grads[n].reshape(weights[n].shape)
        grads[n] = g
        deltas[n], new_m[n], new_v[n] = _adamw("adamw_" + n, weights[n], g, ms[n], vs[n])
    return (loss, grad_x, *[grads[n] for n in names], *[deltas[n] for n in names], *[new_m[n] for n in names],
            *[new_v[n] for n in names])


def _sublane_sum(name, a):
    n, _, w = a.shape

    def body(a_ref, o_ref):
        for q in range(n):
            o_ref[q:q + 1, :] = jnp.sum(a_ref[q], axis=0, keepdims=True)

    return pl.pallas_call(body, name=name, out_shape=_sds((n, w)))(a)


def _colsum_groups(name, dm_all):
    nl, nr, w = dm_all.shape

    def body(d_ref, o_ref):
        o_ref[...] = jnp.zeros((8, w), F32) + jnp.sum(d_ref[...], axis=0, keepdims=True)

    out = pl.pallas_call(body, name=name, grid=(nl,), in_specs=[pl.BlockSpec((None, nr, w), lambda l: (l, 0, 0))],
                         out_specs=pl.BlockSpec((None, 8, w), lambda l: (l, 0, 0)), out_shape=_sds((nl, 8, w)),
                         compiler_params=_cp(1))(dm_all)
    return out[:, 0]
```

```python
import math

import jax
import jax.numpy as jnp
from jax import lax
from jax.experimental import pallas as pl
from jax.experimental.pallas import tpu as pltpu

F32 = jnp.float32
BF16 = jnp.bfloat16
MESH = pl.DeviceIdType.MESH
ANY = pl.BlockSpec(memory_space=pl.ANY)

DEPTH = 4
S5_GROUP = 16
S5_STATE = 64
GRID_W = 64
POS_TEMP = 10000.0
LAMBDA_RE_MAX = -1e-4
LN_EPS = 1e-5
DN_ALPHA = (2.0 * DEPTH) ** 0.25
ADAM_LR, ADAM_B1, ADAM_B2, ADAM_EPS, ADAM_WD, ADAM_STEP = 0.001, 0.9, 0.999, 1e-08, 0.01, 10

SUBLANES = 8
LANES = 128
OCT_CH = 128
OCT_ST = 512
CONV_ROWS = 64
VMEM_LIMIT = 56 * 1024 * 1024


def _cp(n_axes):
    return pltpu.CompilerParams(dimension_semantics=("arbitrary",) * n_axes, vmem_limit_bytes=VMEM_LIMIT)


def _full(shape, single=False):
    nd = len(shape)
    if single:
        return pl.BlockSpec(shape, lambda *i: (0,) * nd, pipeline_mode=pl.Buffered(1))
    return pl.BlockSpec(shape, lambda *i: (0,) * nd)


def _sds(shape, dtype=F32):
    return jax.ShapeDtypeStruct(tuple(shape), dtype)


def _mod(x, sh8, sc8):
    r, d = x.shape
    return (x.reshape(r // 8, 8, d) * (1.0 + sc8[None]) + sh8[None]).reshape(r, d)


def _rowscale(x, g8):
    r, d = x.shape
    return (x.reshape(r // 8, 8, d) * g8[None]).reshape(r, d)


def _sum8(x):
    r, w = x.shape
    return jnp.sum(x.reshape(r // 8, 8, w), axis=0)


def _ln_stats(r):
    mu = jnp.mean(r, axis=-1, keepdims=True)
    xc = r - mu
    var = jnp.mean(xc * xc, axis=-1, keepdims=True)
    rstd = lax.rsqrt(var + LN_EPS)
    return xc * rstd, rstd


def _ln_bwd(dxh, xhat, rstd):
    m1 = jnp.mean(dxh, axis=-1, keepdims=True)
    m2 = jnp.mean(dxh * xhat, axis=-1, keepdims=True)
    return rstd * (dxh - m1 - xhat * m2)


def _sigmoid(x):
    return 1.0 / (1.0 + jnp.exp(-x))


def _gelu(y):
    return 0.5 * y * (1.0 + lax.erf(y * (1.0 / math.sqrt(2.0))))


def _gelu_grad(y):
    return 0.5 * (1.0 + lax.erf(y * (1.0 / math.sqrt(2.0)))) + y * jnp.exp(-0.5 * y * y) * (1.0 / math.sqrt(2.0 * math.pi))


def _dot(a, b):
    return jnp.dot(a, b, preferred_element_type=F32)


def _dot_nt(a, b):
    return lax.dot_general(a, b, (((1,), (1,)), ((), ())), preferred_element_type=F32)


def _dot_tn(a, b):
    return lax.dot_general(a, b, (((0,), (0,)), ((), ())), preferred_element_type=F32)


class _Cfg:
    def __init__(self, x, ctx, mlp_w1, cv_w_dw):
        self.B, self.L, self.D = x.shape
        self.Lc = ctx.shape[1]
        assert self.B * 4 == SUBLANES, "two examples per device, four chunks each"
        self.F = mlp_w1.shape[2] * 4
        self.KW = cv_w_dw.shape[1]
        self.half = self.KW // 2
        self.G = self.D // S5_GROUP
        self.P = S5_STATE
        self.NS = self.G * self.P
        self.NO = self.D // OCT_CH
        assert self.NO % 2 == 0
        self.nx = self.L // 4
        self.nc = self.Lc // 4
        self.Tc = self.B * self.Lc
        self.Tx = self.B * self.L
        self.T = self.Tc + self.Tx
        self.TM = 512 if self.Tc % 512 == 0 else self.Tc
        assert self.Tx % self.TM == 0 and self.TM % 16 == 0
        self.HB = self.TM // 2
        assert SUBLANES * self.half <= self.HB
        self.TW = 512 if (self.Tc % 512 == 0 and self.Tx % 512 == 0) else self.TM

    def ti(self, n):
        t = 32 if self.nc % 32 == 0 else self.nc
        assert n % t == 0 and self.Tc % (8 * t) == 0
        return t

    def rows(self, ctx_too):
        return (0, self.T) if ctx_too else (self.Tc, self.Tx)


def _allgather8(name, x_shard):
    m_per, n = x_shard.shape
    assert m_per % 8 == 0

    def body(x_ref, out_ref, send_sems, recv_sems, local_sem):
        x, y, c = lax.axis_index("x"), lax.axis_index("y"), lax.axis_index("c")
        me, sibling = (x, y, c), (x, y, 1 - c)
        chips = [(1 - x, y), (x, 1 - y), (1 - x, 1 - y)]

        def rows(px, py, pc):
            return out_ref.at[pl.ds((4 * px + 2 * py + pc) * m_per, m_per), :]

        def copy(k, block, to, src=None):
            return pltpu.make_async_remote_copy(
                src_ref=rows(*block) if src is None else src, dst_ref=rows(*block),
                send_sem=send_sems.at[k], recv_sem=recv_sems.at[k], device_id=to, device_id_type=MESH)

        mine = pltpu.make_async_copy(x_ref, rows(*me), local_sem)
        mine.start()
        first = [copy(0, me, sibling, src=x_ref)]
        first += [copy(1 + j, me, (*chip, c), src=x_ref) for j, chip in enumerate(chips)]
        for cp in first:
            cp.start()
        passed = [copy(4 + j, (*chip, c), sibling) for j, chip in enumerate(chips)]
        for j, chip in enumerate(chips):
            copy(1 + j, (*chip, c), me).wait_recv()
            passed[j].start()
        copy(0, sibling, me).wait_recv()
        for j, chip in enumerate(chips):
            copy(4 + j, (*chip, 1 - c), me).wait_recv()
        for cp in first + passed:
            cp.wait_send()
        mine.wait()

    return pl.pallas_call(
        body, name=name, out_shape=_sds((8 * m_per, n), x_shard.dtype),
        in_specs=[pl.BlockSpec(memory_space=pltpu.VMEM)], out_specs=pl.BlockSpec(memory_space=pltpu.VMEM),
        scratch_shapes=[pltpu.SemaphoreType.DMA((7,)), pltpu.SemaphoreType.DMA((7,)), pltpu.SemaphoreType.DMA],
        compiler_params=pltpu.CompilerParams(vmem_limit_bytes=VMEM_LIMIT),
    )(x_shard)


def _flip(v, m):
    return v + m - 2 * v * m


def _peer(axis):
    x, y, c = lax.axis_index("x"), lax.axis_index("y"), lax.axis_index("c")
    if axis == "c":
        return (x, y, 1 - c)
    if axis == "xy":
        return (_flip(x, 1 - c), _flip(y, c), c)
    if axis == "yx":
        return (_flip(x, c), _flip(y, 1 - c), c)
    raise ValueError(axis)


def _pair_exchange(name, axis, inputs, out_shapes, aliases, plan):
    n_in = len(inputs)
    n_out = len(out_shapes)

    def body(*refs):
        ins, outs = refs[:n_in], refs[n_in:n_in + n_out]
        send_sems, recv_sems, local_sems = refs[n_in + n_out:]
        x, y, c = lax.axis_index("x"), lax.axis_index("y"), lax.axis_index("c")
        remote, local = plan(x, y, c, ins, outs)
        lcs = [pltpu.make_async_copy(s, d, local_sems.at[k]) for k, (s, d) in enumerate(local)]
        for cp in lcs:
            cp.start()
        rcs = [pltpu.make_async_remote_copy(src_ref=s, dst_ref=d, send_sem=send_sems.at[k], recv_sem=recv_sems.at[k],
                                            device_id=_peer(axis), device_id_type=MESH) for k, (s, d) in enumerate(remote)]
        for cp in rcs:
            cp.start()
        for cp in rcs:
            cp.wait()
        for cp in lcs:
            cp.wait()

    n_remote, n_local = plan.counts
    return pl.pallas_call(
        body, name=name, out_shape=tuple(out_shapes),
        in_specs=[ANY] * n_in, out_specs=tuple([ANY] * n_out),
        input_output_aliases=dict(aliases),
        scratch_shapes=[pltpu.SemaphoreType.DMA((n_remote,)), pltpu.SemaphoreType.DMA((n_remote,)),
                        pltpu.SemaphoreType.DMA((max(n_local, 1),))],
    )(*inputs)


def _plan(n_remote, n_local=0):
    def deco(fn):
        fn.counts = (n_remote, n_local)
        return fn
    return deco


HBM = pl.BlockSpec(memory_space=pltpu.HBM)
SEM = pl.BlockSpec(memory_space=pltpu.SEMAPHORE)


def _split_start(name, axis, bufs, plan):
    nb = len(bufs)
    n = plan.counts[0]

    def body(*refs):
        ins, send_sem, recv_sem, token = refs[:nb], refs[nb], refs[nb + 1], refs[-1]
        x, y, c = lax.axis_index("x"), lax.axis_index("y"), lax.axis_index("c")
        for k, (s, d) in enumerate(plan(x, y, c, ins)):
            pltpu.make_async_remote_copy(src_ref=s, dst_ref=d, send_sem=send_sem.at[k], recv_sem=recv_sem.at[k],
                                         device_id=_peer(axis), device_id_type=MESH).start()
        token[...] = jnp.zeros_like(token)

    outs = pl.pallas_call(
        body, name=name,
        out_shape=(pltpu.SemaphoreType.DMA((n,)), pltpu.SemaphoreType.DMA((n,)),
                   *[pltpu.HBM(b.shape, b.dtype) for b in bufs], _sds((8, LANES))),
        in_specs=[HBM] * nb, out_specs=(SEM, SEM, *([HBM] * nb), pl.BlockSpec(memory_space=pltpu.VMEM)),
        input_output_aliases={i: 2 + i for i in range(nb)},
        compiler_params=pltpu.CompilerParams(has_side_effects=pltpu.SideEffectType.DATAFLOW_SIDE_EFFECTING),
    )(*[pltpu.with_memory_space_constraint(b, pltpu.HBM) for b in bufs])
    return dict(name=name, axis=axis, plan=plan, send=outs[0], recv=outs[1], bufs=list(outs[2:2 + nb]), token=outs[-1])


def _split_wait(h, after):
    bufs, plan, axis = h["bufs"], h["plan"], h["axis"]
    nb = len(bufs)

    def body(*refs):
        ins, send_sem, recv_sem = refs[:nb], refs[nb], refs[nb + 1]
        x, y, c = lax.axis_index("x"), lax.axis_index("y"), lax.axis_index("c")
        for k, (s, d) in enumerate(plan(x, y, c, ins)):
            cp = pltpu.make_async_remote_copy(src_ref=s, dst_ref=d, send_sem=send_sem.at[k], recv_sem=recv_sem.at[k],
                                              device_id=_peer(axis), device_id_type=MESH)
            cp.wait_send()
            cp.wait_recv()

    outs = pl.pallas_call(
        body, name=h["name"] + "_wait", out_shape=tuple(pltpu.HBM(b.shape, b.dtype) for b in bufs),
        in_specs=[HBM] * nb + [SEM, SEM, ANY], out_specs=tuple([HBM] * nb),
        input_output_aliases={i: i for i in range(nb)},
        compiler_params=pltpu.CompilerParams(has_side_effects=pltpu.SideEffectType.DATAFLOW_SIDE_EFFECTING),
    )(*bufs, h["send"], h["recv"], after)
    return list(outs)


def _tie(small, tokens):
    for t in tokens:
        small = small + t[0, 0]
    return small


class _Overlap:
    def __init__(self):
        self.live = {}
        self.done = {}

    def add(self, key, gen):
        self.live[key] = gen
        return [next(gen)]

    def point(self, arr):
        tokens = []
        for key in list(self.live):
            try:
                tokens.append(self.live[key].send(arr))
            except StopIteration as e:
                self.done[key] = e.value
                del self.live[key]
        return tokens

    def finish(self, key, arr):
        while key in self.live:
            try:
                self.live[key].send(arr)
            except StopIteration as e:
                self.done[key] = e.value
                del self.live[key]
        return self.done.pop(key)


def _gather_gen(tag, fams):
    nf = len(fams)
    shapes = [f.shape for f in fams]
    views = [f.reshape(4, 2, -1, f.shape[-1]) for f in fams]

    @_plan(nf)
    def plan1(x, y, c, refs):
        s = 2 * x + y
        return [(refs[k].at[s, c], refs[k].at[s, c]) for k in range(nf)]

    @_plan(2 * nf)
    def plan2(x, y, c, refs):
        shards = [2 * x + y, 2 * _flip(x, 1 - c) + _flip(y, c)]
        return [(refs[k].at[s, c], refs[k].at[s, c]) for k in range(nf) for s in shards]

    @_plan(3 * nf)
    def plan3(x, y, c, refs):
        shards = [2 * (1 - x) + y, 2 * x + (1 - y), 2 * (1 - x) + (1 - y)]
        return [(refs[k].at[s, c], refs[k].at[s, c]) for k in range(nf) for s in shards]

    for rnd, (axis, plan) in enumerate((("xy", plan1), ("yx", plan2), ("c", plan3))):
        h = _split_start(f"{tag}_g{rnd}", axis, views, plan)
        after = yield h["token"]
        views = _split_wait(h, after)
    return [v.reshape(sh) for v, sh in zip(views, shapes)]


def _reduce_scatter_gen(tag, grads):
    ng = len(grads)
    flat = [g.reshape(4, 2, -1, g.shape[-1]) for g in grads]

    def empty(shape, dtype):
        return lax.empty(tuple(shape), dtype)

    @_plan(ng)
    def plan1(x, y, c, refs):
        return [(refs[k].at[:, 1 - c], refs[ng + k]) for k in range(ng)]

    h = _split_start(tag + "_r0", "c", flat + [empty((4,) + f.shape[2:], F32) for f in flat], plan1)
    after = yield h["token"]
    bufs = _split_wait(h, after)
    p1 = [_sel_add(f"{tag}_add1_{k}", bufs[k], lambda j, sc: (j, sc[2]), bufs[ng + k], True) for k in range(ng)]

    def sent1(kk, x, y, c):
        return ((1 - c) * kk + c * (1 - x), (1 - c) * (1 - y) + c * kk)

    def kept1(j, sc):
        x, y, c = sc[0], sc[1], sc[2]
        return ((1 - c) * j + c * x, (1 - c) * y + c * j)

    @_plan(2 * ng)
    def plan2(x, y, c, refs):
        return [(refs[k].at[sent1(kk, x, y, c)], refs[ng + k].at[kk]) for k in range(ng) for kk in range(2)]

    v1 = [pb.reshape(2, 2, pb.shape[1], pb.shape[2]) for p, pb in p1]
    h = _split_start(tag + "_r1", "yx", v1 + [empty((2,) + v.shape[2:], BF16) for v in v1], plan2)
    after = yield h["token"]
    bufs = _split_wait(h, after)
    p2 = [_sel_add(f"{tag}_add2_{k}", p1[k][0].reshape(2, 2, p1[k][0].shape[1], p1[k][0].shape[2]), kept1, bufs[ng + k], True)
          for k in range(ng)]

    @_plan(ng)
    def plan3(x, y, c, refs):
        return [(refs[k].at[(1 - c) * (1 - x) + c * (1 - y)], refs[ng + k]) for k in range(ng)]

    h = _split_start(tag + "_r2", "xy", [qb for q, qb in p2] + [empty(qb.shape[1:], BF16) for q, qb in p2], plan3)
    after = yield h["token"]
    bufs = _split_wait(h, after)
    fin = [_sel_add(f"{tag}_add3_{k}", p2[k][0], lambda j, sc: ((1 - sc[2]) * sc[0] + sc[2] * sc[1],), bufs[ng + k][None],
                    False, out_slots=(2, lambda j, sc: sc[2]))[0] for k in range(ng)]

    @_plan(ng)
    def plan4(x, y, c, refs):
        return [(refs[k].at[c], refs[k].at[c]) for k in range(ng)]

    h = _split_start(tag + "_r3", "c", fin, plan4)
    after = yield h["token"]
    full = _split_wait(h, after)
    return [full[k].reshape(grads[k].shape[1:]) for k in range(ng)]


def _xyc():
    return jnp.stack([lax.axis_index("x"), lax.axis_index("y"), lax.axis_index("c")]).astype(jnp.int32)


def _place_shard(name, w, fam, slot0, n_slots):
    n, kk, nn = w.shape
    kt = 256 if kk % 256 == 0 else kk

    def body(scal, w_ref, *rest):
        rest[-1][...] = w_ref[...].astype(BF16)

    in_specs = [pl.BlockSpec((None, kt, nn), lambda t, i, sc: (t, i, 0))]
    args = [_xyc(), w]
    aliases = {}
    if fam is not None:
        in_specs.append(ANY)
        args.append(fam)
        aliases = {2: 0}
    gs = pltpu.PrefetchScalarGridSpec(
        num_scalar_prefetch=1, grid=(n, kk // kt), in_specs=in_specs,
        out_specs=pl.BlockSpec((None, None, kt, nn), lambda t, i, sc: (2 * sc[0] + sc[1], slot0 + t, i, 0)))
    return pl.pallas_call(body, name=name, grid_spec=gs, out_shape=_sds((4, n_slots, kk, nn), BF16),
                          input_output_aliases=aliases, compiler_params=_cp(2))(*args)


def _sel_add(name, a, a_sel, r, emit_bf16, out_slots=None):
    nr, rows, w = r.shape
    tr = 256 if rows % 256 == 0 else rows

    def body(scal, a_ref, r_ref, *outs):
        s = a_ref[...] + r_ref[...].astype(F32)
        outs[0][...] = s
        if emit_bf16:
            outs[1][...] = s.astype(BF16)

    lead = a.ndim - 2
    a_block = (None,) * lead + (tr, w)
    n_out, o_fn = out_slots if out_slots is not None else (nr, lambda j, sc: j)
    out_shape = [_sds((n_out, rows, w), F32)] + ([_sds((nr, rows, w), BF16)] if emit_bf16 else [])
    out_specs = [pl.BlockSpec((None, tr, w), lambda j, t, sc: (o_fn(j, sc), t, 0))]
    if emit_bf16:
        out_specs.append(pl.BlockSpec((None, tr, w), lambda j, t, sc: (j, t, 0)))
    gs = pltpu.PrefetchScalarGridSpec(
        num_scalar_prefetch=1, grid=(nr, rows // tr),
        in_specs=[pl.BlockSpec(a_block, lambda j, t, sc: tuple(a_sel(j, sc)) + (t, 0)),
                  pl.BlockSpec((None, tr, w), lambda j, t, sc: (j, t, 0))],
        out_specs=out_specs)
    return pl.pallas_call(body, name=name, grid_spec=gs, out_shape=out_shape, compiler_params=_cp(2))(_xyc(), a, r)


def _allreduce8(tag, buf, point=None):
    rows, w = buf.shape
    assert rows % 16 == 0
    step = (lambda a: _tie(a, point(a))) if point is not None else (lambda a: a)
    one = lambda: _plan(1)(lambda x, y, c, ins, outs_: ([(ins[0], outs_[0])], []))
    (got,) = _pair_exchange(f"{tag}_ar_c", "c", [buf], [_sds(buf.shape, F32)], {}, one())
    cur = _ew(f"{tag}_aradd_c", lambda a, b: a + b, [buf, got], [_sds(buf.shape, F32)])[0]
    cur = step(cur).reshape(2, rows // 2, w)
    mine = _plan(1)(lambda x, y, c, ins, outs_: ([(ins[0].at[c], outs_[0])], []))
    (got,) = _pair_exchange(f"{tag}_ar_1", "xy", [cur], [_sds(cur.shape[1:], F32)], {}, mine)
    (h1,) = _sel_add(f"{tag}_aradd_1", cur, lambda j, sc: (sc[2],), got[None], False)
    h1 = step(h1)
    (got,) = _pair_exchange(f"{tag}_ar_2", "yx", [h1[0]], [_sds(h1.shape[1:], F32)], {}, one())
    (h2,) = _sel_add(f"{tag}_aradd_2", h1, lambda j, sc: (0,), got[None], False, out_slots=(2, lambda j, sc: sc[2]))
    swap = _plan(1)(lambda x, y, c, ins, outs_: ([(ins[0].at[c], outs_[0].at[c])], []))
    (full,) = _pair_exchange(f"{tag}_ar_c2", "c", [h2], [_sds(h2.shape, F32)], {0: 0}, swap)
    return full.reshape(rows, w)


def _ew(name, fn, ins, outs):
    rows, w = ins[0].shape
    tr = rows
    for cand in (512, 256, 128, 64, 32, 16, 8):
        if rows % cand == 0 and rows > cand and cand * w * 4 <= (1 << 20):
            tr = cand
            break
    n_in = len(ins)

    def body(*refs):
        vals = fn(*[r[...] for r in refs[:n_in]])
        if not isinstance(vals, (tuple, list)):
            vals = (vals,)
        for o, v in zip(refs[n_in:], vals):
            o[...] = v.astype(o.dtype)

    spec = pl.BlockSpec((tr, w), lambda i: (i, 0))
    return pl.pallas_call(body, name=name, grid=(rows // tr,), in_specs=[spec] * n_in,
                          out_specs=[spec] * len(outs), out_shape=list(outs), compiler_params=_cp(1))(*ins)


def _ew3(name, fn, ins, n_out):
    aa, bb, cc = ins[0].shape
    pad_bytes = (-(-bb // SUBLANES) * SUBLANES) * (-(-cc // LANES) * LANES) * 4
    ta = 1
    for cand in range(aa, 0, -1):
        if aa % cand == 0 and cand * pad_bytes <= (1 << 20):
            ta = cand
            break
    n_in = len(ins)

    def body(*refs):
        vals = fn(*[r[...] for r in refs[:n_in]])
        for o, v in zip(refs[n_in:], vals):
            o[...] = v

    spec = pl.BlockSpec((ta, bb, cc), lambda i: (i, 0, 0))
    return pl.pallas_call(body, name=name, grid=(aa // ta,), in_specs=[spec] * n_in, out_specs=[spec] * n_out,
                          out_shape=[_sds((aa, bb, cc))] * n_out, compiler_params=_cp(1))(*ins)


def _view_for_ew(a):
    if a.ndim == 1:
        return a.reshape(1, -1)
    if a.ndim == 2:
        return a
    if a.shape[-1] % LANES == 0 and a.shape[-2] % SUBLANES == 0:
        return a.reshape(-1, a.shape[-1])
    return a.reshape(-1, a.shape[-2], a.shape[-1])


def _adamw(name, w, g, m, v):
    def fn(w, g, m, v):
        m = ADAM_B1 * m + (1.0 - ADAM_B1) * g
        v = ADAM_B2 * v + (1.0 - ADAM_B2) * (g * g)
        m_hat = m / (1.0 - ADAM_B1 ** ADAM_STEP)
        v_hat = v / (1.0 - ADAM_B2 ** ADAM_STEP)
        delta = -ADAM_LR * (m_hat / (jnp.sqrt(v_hat) + ADAM_EPS) + ADAM_WD * w)
        return delta, m, v

    shp = w.shape
    a = [_view_for_ew(t) for t in (w, g, m, v)]
    if a[0].ndim == 3:
        o = _ew3(name, fn, a, 3)
    else:
        o = _ew(name, fn, a, [_sds(a[0].shape)] * 3)
    return tuple(t.reshape(shp) for t in o)


def _to_perm(a):
    b, ls, d = a.shape
    n = ls // 4
    return a.reshape(b * 4, n, d).swapaxes(0, 1).reshape(n * 8, d)


def _from_perm(p, b, ls):
    n = ls // 4
    return p.reshape(n, b * 4, p.shape[-1]).swapaxes(0, 1).reshape(b, ls, p.shape[-1])


def _pos_embed(rows, dim):
    def sincos(pos, d):
        quarter = d // 2
        omega = POS_TEMP ** (-jnp.arange(quarter, dtype=F32) / quarter)
        ang = pos[:, None] * omega[None, :]
        return jnp.concatenate([jnp.sin(ang), jnp.cos(ang)], axis=-1)

    row_idx = jnp.repeat(jnp.arange(rows), GRID_W).astype(F32)
    col_idx = jnp.tile(jnp.arange(GRID_W), rows).astype(F32)
    return jnp.concatenate([sincos(row_idx, dim // 2), sincos(col_idx, dim // 2)], axis=-1)


def _stream_of(cfg, off_tiles, tile_rows):
    nct = cfg.Tc // tile_rows
    return lambda i: jnp.where(i + off_tiles >= nct, 1, 0)


def _ada_fwd(c_all, w_ada, b_shard):
    nl, d, w = w_ada.shape
    tn = 512 if w % 512 == 0 else w

    def body(c_ref, w_ref, b_ref, o_ref):
        cv = c_ref[...]
        cond = (cv * _sigmoid(cv)).astype(BF16)
        o_ref[...] = _dot(cond, w_ref[...].astype(BF16)) + b_ref[...]

    return pl.pallas_call(
        body, name="ada_fwd", grid=(nl, w // tn),
        in_specs=[_full(c_all.shape), pl.BlockSpec((None, d, tn), lambda l, j: (l, 0, j)),
                  pl.BlockSpec((None, 1, tn), lambda l, j: (l, 0, j))],
        out_specs=pl.BlockSpec((None, c_all.shape[0], tn), lambda l, j: (l, 0, j)),
        out_shape=_sds((nl, c_all.shape[0], w)), compiler_params=_cp(2))(c_all, w_ada, b_shard)


def _ada_bwd(c_all, dmod_shard, w_ada):
    nl, d, w = w_ada.shape
    tn = 512 if w % 512 == 0 else w
    nr = c_all.shape[0]

    def body(c_ref, dm_ref, w_ref, gw_ref, dc_ref):
        j = pl.program_id(0) * (w // tn) + pl.program_id(1)
        cv = c_ref[...]
        cond = (cv * _sigmoid(cv)).astype(BF16)
        dm = dm_ref[...].astype(BF16)
        gw_ref[...] = _dot_tn(cond, dm)
        part = _dot_nt(dm[16:24], w_ref[...].astype(BF16))

        @pl.when(j == 0)
        def _():
            dc_ref[...] = part

        @pl.when(j > 0)
        def _():
            dc_ref[...] += part

    return pl.pallas_call(
        body, name="ada_bwd", grid=(nl, w // tn),
        in_specs=[_full(c_all.shape), pl.BlockSpec((None, nr, tn), lambda l, j: (l, 0, j)),
                  pl.BlockSpec((None, d, tn), lambda l, j: (l, 0, j))],
        out_specs=[pl.BlockSpec((None, d, tn), lambda l, j: (l, 0, j)), _full((8, d))],
        out_shape=[_sds((nl, d, w)), _sds((8, d))], compiler_params=_cp(2))(c_all, dmod_shard, w_ada)


def _disc(lr, li, ldt, br, bi):
    lr = jnp.minimum(lr, LAMBDA_RE_MAX)
    dt = jnp.exp(ldt)
    mag = jnp.exp(lr * dt)
    abr = mag * jnp.cos(li * dt)
    abi = mag * jnp.sin(li * dt)
    den = lr * lr + li * li
    nr = abr - 1.0
    ni = abi
    cr = (nr * lr + ni * li) / den
    ci = (ni * lr - nr * li) / den
    return abr, abi, cr[None] * br - ci[None] * bi, cr[None] * bi + ci[None] * br


def _disc_fwd(lr, li, ldt, br, bi):
    def body(a, b, c, d, e, o1, o2, o3, o4):
        r = _disc(a[...], b[...], c[...], d[...], e[...])
        o1[...], o2[...], o3[...], o4[...] = r

    return pl.pallas_call(body, name="s5_disc_fwd", out_shape=[_sds(lr.shape), _sds(lr.shape), _sds(br.shape), _sds(br.shape)])(
        lr, li, ldt, br, bi)


def _disc_bwd(lr, li, ldt, br, bi, g_abr, g_abi, g_bbr, g_bbi):
    def body(a, b, c, d, e, g1, g2, g3, g4, o1, o2, o3, o4, o5):
        _, vjp = jax.vjp(_disc, a[...], b[...], c[...], d[...], e[...])
        r = vjp((g1[...], g2[...], g3[...], g4[...]))
        o1[...], o2[...], o3[...], o4[...], o5[...] = r

    return pl.pallas_call(
        body, name="s5_disc_bwd",
        out_shape=[_sds(lr.shape), _sds(li.shape), _sds(ldt.shape), _sds(br.shape), _sds(bi.shape)])(
        lr, li, ldt, br, bi, g_abr, g_abi, g_bbr, g_bbi)


def _s5_layouts(cfg, lam_re, lam_im, log_dt, b_re, b_im):
    P, G = cfg.P, cfg.G
    nd = lam_re.shape[0]
    lr = lam_re.transpose(2, 0, 1).reshape(P, nd * G)
    li = lam_im.transpose(2, 0, 1).reshape(P, nd * G)
    ldt = log_dt.reshape(1, nd * G)
    br = b_re.transpose(3, 2, 0, 1).reshape(S5_GROUP, P, nd * G)
    bi = b_im.transpose(3, 2, 0, 1).reshape(S5_GROUP, P, nd * G)
    return lr, li, ldt, br, bi


def _coef_rows(cfg, abr, abi, conj):
    nd = abr.shape[1] // cfg.G

    def one(t):
        return t.reshape(cfg.P, nd, cfg.G).transpose(1, 2, 0).reshape(nd, cfg.NS)
    a = jnp.stack([one(abr), -one(abi) if conj else one(abi)], axis=1)
    return jnp.broadcast_to(a[:, :, None, :], (nd, 2, SUBLANES, cfg.NS))


def _blockdiag_b(cfg, bbr, bbi):
    eye = jnp.eye(8, dtype=F32)
    nd = bbr.shape[2] // cfg.G

    def one(t):
        t = t.reshape(S5_GROUP, cfg.P, nd, cfg.G).transpose(2, 3, 0, 1)
        t = t.reshape(nd, cfg.NO, 8, S5_GROUP, cfg.P)
        return jnp.einsum("dogcp,gh->dogchp", t, eye).reshape(nd, cfg.NO, OCT_CH, OCT_ST)

    return jnp.concatenate([one(bbr), one(bbi)], axis=-1).astype(BF16)


def _blockdiag_c(cfg, c_re, c_im):
    eye = jnp.eye(8, dtype=F32)
    nd = c_re.shape[0]

    def one(t):
        t = t.transpose(0, 1, 3, 2).reshape(nd, cfg.NO, 8, cfg.P, S5_GROUP)
        return jnp.einsum("dogpc,gh->dogphc", t, eye).reshape(nd, cfg.NO, OCT_ST, OCT_CH)

    return jnp.concatenate([one(c_re), -one(c_im)], axis=2).astype(BF16)


def _diag_b(cfg, dbf):
    eye = jnp.eye(8, dtype=F32)
    nd = dbf.shape[0]

    def one(t):
        t = t.reshape(nd, cfg.NO, 8, S5_GROUP, 8, cfg.P)
        t = jnp.einsum("dogchp,gh->dogcp", t, eye).reshape(nd, cfg.G, S5_GROUP, cfg.P)
        return t.transpose(2, 3, 0, 1).reshape(S5_GROUP, cfg.P, nd * cfg.G)

    return one(dbf[..., :OCT_ST]), one(dbf[..., OCT_ST:])


def _diag_c(cfg, dcft):
    eye = jnp.eye(8, dtype=F32)
    nd = dcft.shape[0]

    def one(t):
        t = t.reshape(nd, cfg.NO, 8, S5_GROUP, 8, cfg.P)
        return jnp.einsum("dohcgp,gh->dogcp", t, eye).reshape(nd, cfg.G, S5_GROUP, cfg.P)

    return one(dcft[..., :OCT_ST]), -one(dcft[..., OCT_ST:])


def _recur(buf, st, a_ref, n_oct, ti, rev, store, flat=False):
    for o in range(0, n_oct, 2):
        cols = [(pl.ds(oo * 2 * OCT_ST, OCT_ST), pl.ds(oo * 2 * OCT_ST + OCT_ST, OCT_ST)) for oo in (o, o + 1)]
        scol = [pl.ds(oo * OCT_ST, OCT_ST) for oo in (o, o + 1)]
        coef = [(a_ref[0, :, sc], a_ref[1, :, sc]) for sc in scol]
        init = (st[0, :, scol[0]], st[1, :, scol[0]], st[0, :, scol[1]], st[1, :, scol[1]])

        def step(i4, carry, cols=cols, coef=coef):
            carry = list(carry)
            for q in range(unroll):
                i = i4 * unroll + q
                r = pl.multiple_of((i + rev * (ti - 1 - 2 * i)) * 8, 8)
                for s in range(2):
                    sr, si = carry[2 * s], carry[2 * s + 1]
                    ar, ai = coef[s]
                    zr = buf[pl.ds(r, 8), cols[s][0]]
                    zi = buf[pl.ds(r, 8), cols[s][1]]
                    nr = ar * sr - ai * si + zr
                    ni = ar * si + ai * sr + zi
                    if store:
                        buf[pl.ds(r, 8), cols[s][0]] = nr
                        buf[pl.ds(r, 8), cols[s][1]] = ni
                    carry[2 * s], carry[2 * s + 1] = nr, ni
            return tuple(carry)

        unroll = 4 if ti % 4 == 0 else 1
        if flat:
            fin = init
            for i4 in range(ti // unroll):
                fin = step(i4, fin)
        else:
            fin = lax.fori_loop(0, ti // unroll, step, init)
        st[0, :, scol[0]] = fin[0]
        st[1, :, scol[0]] = fin[1]
        st[0, :, scol[1]] = fin[2]
        st[1, :, scol[1]] = fin[3]


def _s5_fwd_pass(cfg, name, tok, mod8, col_sh, col_sc, bf, acoef, r0, n, s_init=None, cf=None, y_prev=None):
    D, NO, NS = cfg.D, cfg.NO, cfg.NS
    ti = cfg.ti(n)
    nb = n // ti
    R = 8 * ti
    ob = r0 // R
    second = s_init is not None
    blk = lambda d, j: ob + j + d * (nb - 1 - 2 * j)

    def body(*refs):
        if second:
            tok_ref, mod_ref, bf_ref, a_ref, si_ref, cf_ref, yp_ref, y_ref, ck_ref, fin_ref, zbuf, st = refs
        else:
            tok_ref, mod_ref, bf_ref, a_ref, fin_ref, zbuf, st = refs
        d = pl.program_id(0)
        j = pl.program_id(1)

        @pl.when(j == 0)
        def _():
            if second:
                st[...] = si_ref[...]
            else:
                st[...] = jnp.zeros_like(st)

        if second:
            ck_ref[...] = st[...]
        u = _mod(tok_ref[...], mod_ref[:, col_sh:col_sh + D], mod_ref[:, col_sc:col_sc + D]).astype(BF16)
        for o in range(NO):
            zbuf[:, o * 1024:(o + 1) * 1024] = _dot(u[:, o * OCT_CH:(o + 1) * OCT_CH], bf_ref[o])
        _recur(zbuf, st, a_ref, NO, ti, d, second, flat=True)
        if second:
            for o in range(NO):
                y_ref[:, o * OCT_CH:(o + 1) * OCT_CH] = _dot(zbuf[:, o * 1024:(o + 1) * 1024].astype(BF16), cf_ref[o])

        @pl.when(j == nb - 1)
        def _():
            fin_ref[...] = st[...]

    st_spec = pl.BlockSpec((None, 2, 8, NS), lambda d, j: (d, 0, 0, 0))
    in_specs = [pl.BlockSpec((R, D), lambda d, j: (blk(d, j), 0)), _full(mod8.shape),
                pl.BlockSpec((None, NO, OCT_CH, 1024), lambda d, j: (d, 0, 0, 0)), st_spec]
    args = [tok, mod8, bf, acoef]
    scratch = [pltpu.VMEM((R, NO * 1024), F32), pltpu.VMEM((2, 8, NS), F32)]
    if not second:
        return pl.pallas_call(body, name=name, grid=(2, nb), in_specs=in_specs, out_specs=st_spec,
                              out_shape=_sds((2, 2, 8, NS)), scratch_shapes=scratch, compiler_params=_cp(2))(*args)
    in_specs += [st_spec, pl.BlockSpec((None, NO, 1024, OCT_CH), lambda d, j: (d, 0, 0, 0))]
    args += [s_init, cf]
    aliases = {}
    if y_prev is not None:
        in_specs.append(ANY)
        args.append(y_prev)
        aliases = {6: 0}
    else:
        in_specs.append(_full((8, LANES)))
        args.append(jnp.zeros((8, LANES), F32))
    out_specs = [pl.BlockSpec((None, R, D), lambda d, j: (d, blk(d, j), 0)),
                 pl.BlockSpec((None, None, 2, 8, NS), lambda d, j: (d, j + d * (nb - 1 - 2 * j), 0, 0, 0)), st_spec]
    out_shape = [_sds((2, cfg.T, D)), _sds((2, nb, 2, 8, NS)), _sds((2, 2, 8, NS))]
    return pl.pallas_call(body, name=name, grid=(2, nb), in_specs=in_specs, out_specs=out_specs, out_shape=out_shape,
                          input_output_aliases=aliases, scratch_shapes=scratch, compiler_params=_cp(2))(*args)


def _s5_chain(cfg, name, fin_local, acoef, n, inc, prev_fin=None):
    NS = cfg.NS
    nsq = int(round(math.log2(n)))
    assert 2 ** nsq == n

    def body(*refs):
        if prev_fin is not None:
            f_ref, a_ref, p_ref, o_ref = refs
        else:
            f_ref, a_ref, o_ref = refs
        for d in range(2):
            pr, pi = a_ref[d, 0, 0:1, :], a_ref[d, 1, 0:1, :]
            for _ in range(nsq):
                pr, pi = pr * pr - pi * pi, 2.0 * pr * pi
            for b in range(2):
                order = [4 * b + k for k in range(4)]
                if not inc[d]:
                    order = order[::-1]
                if prev_fin is not None:
                    last = order[-1]
                    sr, si = p_ref[d, 0, last:last + 1, :], p_ref[d, 1, last:last + 1, :]
                else:
                    sr = jnp.zeros((1, NS), F32)
                    si = jnp.zeros((1, NS), F32)
                for k in order:
                    o_ref[d, 0, k:k + 1, :] = sr
                    o_ref[d, 1, k:k + 1, :] = si
                    fr, fi = f_ref[d, 0, k:k + 1, :], f_ref[d, 1, k:k + 1, :]
                    sr, si = pr * sr - pi * si + fr, pr * si + pi * sr + fi

    args = [fin_local, acoef] + ([prev_fin] if prev_fin is not None else [])
    return pl.pallas_call(body, name=name, out_shape=_sds((2, 2, 8, NS)))(*args)


def _s5_forward(cfg, tag, tok, mod8, col_sh, col_sc, bf, cf, acoef, point=None):
    saved = {}
    fin_prev = None
    y = None
    for ph, (r0, n) in (("c", (0, cfg.nc)), ("x", (cfg.Tc, cfg.nx))):
        m8 = mod8[0 if ph == "c" else 1]
        loc = _s5_fwd_pass(cfg, f"{tag}_scan1{ph}", tok, m8, col_sh, col_sc, bf, acoef, r0, n)
        if point is not None and ph == "x":
            m8 = _tie(m8, point(loc))
        s_in = _s5_chain(cfg, f"{tag}_chain{ph}", loc, acoef, n, (True, False), fin_prev)
        y, ck, fin_prev = _s5_fwd_pass(cfg, f"{tag}_scan2{ph}", tok, m8, col_sh, col_sc, bf, acoef, r0, n, s_in, cf, y)
        saved[ph] = ck
    return y, saved


def _s5_bwd_pass(cfg, name, dy, tok, mod8, col_sh, col_sc, bf, cf, acoef, acoef_adj, r0, n, g_init=None, ck=None,
                 du_prev=None):
    D, NO, NS = cfg.D, cfg.NO, cfg.NS
    ti = cfg.ti(n)
    nb = n // ti
    R = 8 * ti
    ob = r0 // R
    second = g_init is not None
    has_dy = dy is not None
    blk = lambda d, j: ob + j + (1 - d) * (nb - 1 - 2 * j)

    def body(*refs):
        refs = list(refs)
        dy_ref = refs.pop(0) if has_dy else None
        if second:
            (tok_ref, mod_ref, bf_ref, cf_ref, a_ref, aa_ref, gi_ref, ck_ref, dup_ref,
             du_ref, da_ref, dbf_ref, dcf_ref, gfin_ref, qbuf, zbuf, gst, hst) = refs
        else:
            cf_ref, aa_ref, gfin_ref, qbuf, gst = refs
        d = pl.program_id(0)
        j = pl.program_id(1)

        @pl.when(j == 0)
        def _():
            if second:
                gst[...] = gi_ref[...]
                da_ref[...] = jnp.zeros_like(da_ref)
                dbf_ref[...] = jnp.zeros_like(dbf_ref)
                dcf_ref[...] = jnp.zeros_like(dcf_ref)
            else:
                gst[...] = jnp.zeros_like(gst)

        if has_dy:
            dyb = dy_ref[...].astype(BF16)
            for o in range(NO):
                qbuf[:, o * 1024:(o + 1) * 1024] = _dot_nt(dyb[:, o * OCT_CH:(o + 1) * OCT_CH], cf_ref[o])
        else:
            qbuf[...] = jnp.zeros_like(qbuf)
        _recur(qbuf, gst, aa_ref, NO, ti, 1 - d, second, flat=True)

        if second:
            u = _mod(tok_ref[...], mod_ref[:, col_sh:col_sh + D], mod_ref[:, col_sc:col_sc + D]).astype(BF16)
            for o in range(NO):
                zbuf[:, o * 1024:(o + 1) * 1024] = _dot(u[:, o * OCT_CH:(o + 1) * OCT_CH], bf_ref[o])
            hst[...] = ck_ref[...]
            _recur(zbuf, hst, a_ref, NO, ti, d, True, flat=True)

            g_off, h_off = (1 - d) * 8, d * 8
            edge = pl.multiple_of(d * (R - 8), 8)
            for o in range(0, NO, 2):
                cols = [(pl.ds(oo * 1024, OCT_ST), pl.ds(oo * 1024 + OCT_ST, OCT_ST)) for oo in (o, o + 1)]
                scol = [pl.ds(oo * OCT_ST, OCT_ST) for oo in (o, o + 1)]
                init = []
                for s in range(2):
                    er, ei = qbuf[pl.ds(edge, 8), cols[s][0]], qbuf[pl.ds(edge, 8), cols[s][1]]
                    kr, ki = ck_ref[0, :, scol[s]], ck_ref[1, :, scol[s]]
                    init += [er * kr + ei * ki, ei * kr - er * ki]

                def stp(i, carry, cols=cols):
                    rg = pl.multiple_of(i * 8 + g_off, 8)
                    rh = pl.multiple_of(i * 8 + h_off, 8)
                    out = []
                    for s in range(2):
                        gr, gi = qbuf[pl.ds(rg, 8), cols[s][0]], qbuf[pl.ds(rg, 8), cols[s][1]]
                        hr, hi = zbuf[pl.ds(rh, 8), cols[s][0]], zbuf[pl.ds(rh, 8), cols[s][1]]
                        out += [carry[2 * s] + (gr * hr + gi * hi), carry[2 * s + 1] + (gi * hr - gr * hi)]
                    return tuple(out)

                fin = tuple(init)
                for i in range(ti - 1):
                    fin = stp(i, fin)
                for s in range(2):
                    da_ref[0, :, scol[s]] += fin[2 * s]
                    da_ref[1, :, scol[s]] += fin[2 * s + 1]

            for o in range(NO):
                gb = qbuf[:, o * 1024:(o + 1) * 1024].astype(BF16)
                uo = u[:, o * OCT_CH:(o + 1) * OCT_CH]
                dbf_ref[o] += _dot_tn(uo, gb)
                if has_dy:
                    dcf_ref[o] += _dot_tn(dyb[:, o * OCT_CH:(o + 1) * OCT_CH], zbuf[:, o * 1024:(o + 1) * 1024].astype(BF16))
                du_ref[:, o * OCT_CH:(o + 1) * OCT_CH] = _dot_nt(gb, bf_ref[o])

        @pl.when(j == nb - 1)
        def _():
            gfin_ref[...] = gst[...]

    st_spec = pl.BlockSpec((None, 2, 8, NS), lambda d, j: (d, 0, 0, 0))
    row_spec = pl.BlockSpec((R, D), lambda d, j: (blk(d, j), 0))
    bf_spec = pl.BlockSpec((None, NO, OCT_CH, 1024), lambda d, j: (d, 0, 0, 0))
    cf_spec = pl.BlockSpec((None, NO, 1024, OCT_CH), lambda d, j: (d, 0, 0, 0))
    in_specs, args = [], []
    if has_dy:
        in_specs.append(row_spec)
        args.append(dy)
    if not second:
        in_specs += [cf_spec, st_spec]
        args += [cf, acoef_adj]
        return pl.pallas_call(body, name=name, grid=(2, nb), in_specs=in_specs, out_specs=st_spec,
                              out_shape=_sds((2, 2, 8, NS)),
                              scratch_shapes=[pltpu.VMEM((R, NO * 1024), F32), pltpu.VMEM((2, 8, NS), F32)],
                              compiler_params=_cp(2))(*args)
    ck_spec = pl.BlockSpec((None, None, 2, 8, NS), lambda d, j: (d, j + (1 - d) * (nb - 1 - 2 * j), 0, 0, 0))
    in_specs += [row_spec, _full(mod8.shape), bf_spec, cf_spec, st_spec, st_spec, st_spec, ck_spec]
    args += [tok, mod8, bf, cf, acoef, acoef_adj, g_init, ck]
    n_before = len(args)
    aliases = {}
    if du_prev is not None:
        in_specs.append(ANY)
        args.append(du_prev)
        aliases = {n_before: 0}
    else:
        in_specs.append(_full((8, LANES)))
        args.append(jnp.zeros((8, LANES), F32))
    out_specs = [pl.BlockSpec((None, R, D), lambda d, j: (d, blk(d, j), 0)), st_spec, bf_spec, bf_spec, st_spec]
    out_shape = [_sds((2, cfg.T, D)), _sds((2, 2, 8, NS)), _sds((2, NO, OCT_CH, 1024)), _sds((2, NO, OCT_CH, 1024)),
                 _sds((2, 2, 8, NS))]
    scratch = [pltpu.VMEM((R, NO * 1024), F32), pltpu.VMEM((R, NO * 1024), F32), pltpu.VMEM((2, 8, NS), F32),
               pltpu.VMEM((2, 8, NS), F32)]
    return pl.pallas_call(body, name=name, grid=(2, nb), in_specs=in_specs, out_specs=out_specs, out_shape=out_shape,
                          input_output_aliases=aliases, scratch_shapes=scratch, compiler_params=_cp(2))(*args)


def _s5_backward(cfg, tag, dy, dy_ctx, tok, mod8, col_sh, col_sc, bf, cf, acoef, acoef_adj, saved):
    g_prev = None
    acc = None
    du = None
    for ph, (r0, n) in (("x", (cfg.Tc, cfg.nx)), ("c", (0, cfg.nc))):
        m8 = mod8[0 if ph == "c" else 1]
        dyp = dy if (ph == "x" or dy_ctx) else None
        loc = _s5_bwd_pass(cfg, f"{tag}_adjA{ph}", dyp, tok, m8, col_sh, col_sc, bf, cf, acoef, acoef_adj, r0, n)
        g_in = _s5_chain(cfg, f"{tag}_adjchain{ph}", loc, acoef_adj, n, (False, True), g_prev)
        du, da, dbf, dcf, g_prev = _s5_bwd_pass(cfg, f"{tag}_adjB{ph}", dyp, tok, m8, col_sh, col_sc, bf, cf, acoef,
                                                acoef_adj, r0, n, g_in, saved[ph], du)
        new = (da, dbf, dcf)
        if acc is None:
            acc = new
        else:
            acc = tuple(_ew(f"{tag}_accsum{q}", lambda a, b: a + b, [a.reshape(-1, a.shape[-1]), b.reshape(-1, b.shape[-1])],
                            [_sds((a.size // a.shape[-1], a.shape[-1]))])[0].reshape(a.shape)
                        for q, (a, b) in enumerate(zip(acc, new)))
    return du, acc


def _tok_specs(cfg, rows, width, tile=None):
    tm = tile or cfg.TM
    off = rows[0] // tm
    return pl.BlockSpec((tm, width), lambda i: (i + off, 0)), rows[1] // tm, off


def _mod_spec(cfg, mod8, off):
    st = _stream_of(cfg, off, cfg.TM)
    return pl.BlockSpec((None, 8, mod8.shape[-1]), lambda i: (st(i), 0, 0))


def _wspec(w):
    fam, slot = w
    _, _, kk, nn = fam.shape
    return pl.BlockSpec((4, None, kk, nn), lambda *i: (0, slot, 0, 0), pipeline_mode=pl.Buffered(1))


def _glu_ln(cfg, name, rows, tok, y, mod8, cols, dskip, w, b, gain, bias):
    D, TM = cfg.D, cfg.TM
    csh, csc, cg = cols
    spec, nt, off = _tok_specs(cfg, rows, D)
    spec2, _, _ = _tok_specs(cfg, rows, 2 * D)

    def body(tok_ref, y_ref, mod_ref, ds_ref, w_ref, b_ref, g_ref, bi_ref, x1_ref, r1_ref, mix_ref, zz_ref, zb_ref, yy_ref):
        t = tok_ref[...]
        u = _mod(t, mod_ref[:, csh:csh + D], mod_ref[:, csc:csc + D])
        yy = ds_ref[...] * u + y_ref[0] + y_ref[1]
        zb = _gelu(yy).astype(BF16)
        zz = jnp.concatenate([_dot(zb, w_ref[s]) for s in range(4)], axis=-1) + b_ref[...]
        mix = zz[:, :D] * _sigmoid(zz[:, D:])
        r1 = DN_ALPHA * t + _rowscale(mix, mod_ref[:, cg:cg + D])
        xhat, _ = _ln_stats(r1)
        x1_ref[...] = xhat * g_ref[...] + bi_ref[...]
        r1_ref[...] = r1
        mix_ref[...] = mix.astype(BF16)
        zz_ref[...] = zz.astype(BF16)
        zb_ref[...] = zb
        yy_ref[...] = yy.astype(BF16)

    T = cfg.T
    return pl.pallas_call(
        body, name=name, grid=(nt,),
        in_specs=[spec, pl.BlockSpec((2, TM, D), lambda i: (0, i + off, 0)), _mod_spec(cfg, mod8, off), _full((1, D)),
                  _wspec(w), _full((1, 2 * D)), _full((1, D)), _full((1, D))],
        out_specs=[spec, spec, spec, spec2, spec, spec],
        out_shape=[_sds((T, D)), _sds((T, D)), _sds((T, D), BF16), _sds((T, 2 * D), BF16), _sds((T, D), BF16),
                   _sds((T, D), BF16)],
        compiler_params=_cp(1))(tok, y, mod8, dskip, w[0], b, gain, bias)


def _mlp_ln(cfg, name, rows, x1, mod8, cols, w1, w2, gain, bias):
    D, TM = cfg.D, cfg.TM
    csh, csc, cg = cols
    spec, nt, off = _tok_specs(cfg, rows, D)
    specf, _, _ = _tok_specs(cfg, rows, cfg.F)
    fb = cfg.F // 4

    def body(x_ref, mod_ref, w1_ref, w2_ref, g_ref, bi_ref, x2_ref, r2_ref, out_ref, a_ref, h_ref):
        t = x_ref[...]
        h = _mod(t, mod_ref[:, csh:csh + D], mod_ref[:, csc:csc + D]).astype(BF16)
        out = jnp.zeros((TM, D), F32)
        for s in range(4):
            hid = jnp.maximum(_dot(h, w1_ref[s]), 0.0)
            a = (hid * hid).astype(BF16)
            a_ref[:, s * fb:(s + 1) * fb] = a
            out = out + _dot(a, w2_ref[s])
        r2 = DN_ALPHA * t + _rowscale(out, mod_ref[:, cg:cg + D])
        xhat, _ = _ln_stats(r2)
        x2_ref[...] = xhat * g_ref[...] + bi_ref[...]
        r2_ref[...] = r2
        out_ref[...] = out.astype(BF16)
        h_ref[...] = h

    T = cfg.T
    return pl.pallas_call(
        body, name=name, grid=(nt,),
        in_specs=[spec, _mod_spec(cfg, mod8, off), _wspec(w1), _wspec(w2), _full((1, D)), _full((1, D))],
        out_specs=[spec, spec, spec, specf, spec],
        out_shape=[_sds((T, D)), _sds((T, D)), _sds((T, D), BF16), _sds((T, cfg.F), BF16), _sds((T, D), BF16)],
        compiler_params=_cp(1))(x1, mod8, w1[0], w2[0], gain, bias)


def _pw1_glu(cfg, name, rows, tok, mod8, cols, w, b):
    D, TM = cfg.D, cfg.TM
    csh, csc = cols
    spec, nt, off = _tok_specs(cfg, rows, D)
    spec2, _, _ = _tok_specs(cfg, rows, 2 * D)

    def body(tok_ref, mod_ref, w_ref, b_ref, aa_ref, ag_ref, h_ref):
        h = _mod(tok_ref[...], mod_ref[:, csh:csh + D], mod_ref[:, csc:csc + D]).astype(BF16)
        aa = jnp.concatenate([_dot(h, w_ref[s]) for s in range(4)], axis=-1) + b_ref[...]
        aa_ref[...] = aa.astype(BF16)
        ag_ref[...] = aa[:, :D] * _sigmoid(aa[:, D:])
        h_ref[...] = h

    T = cfg.T
    return pl.pallas_call(
        body, name=name, grid=(nt,),
        in_specs=[spec, _mod_spec(cfg, mod8, off), _wspec(w), _full((1, 2 * D))],
        out_specs=[spec2, spec, spec],
        out_shape=[_sds((T, 2 * D), BF16), _sds((T, D)), _sds((T, D), BF16)], compiler_params=_cp(1))(tok, mod8, w[0], b)


def _halo_maps(cfg, rows):
    TM, HB = cfg.TM, cfg.HB
    off = rows[0] // TM
    nct = cfg.Tc // TM
    ntx = cfg.Tx // TM

    def phase(i):
        t = i + off
        is_x = t >= nct
        first = jnp.where(is_x, nct, 0)
        cnt = jnp.where(is_x, ntx, nct)
        return t, first, cnt

    def prev(i):
        t, first, cnt = phase(i)
        return jnp.where(t == first, 2 * (first + cnt) - 1, 2 * t - 1), 0

    def nxt(i):
        t, first, cnt = phase(i)
        return jnp.where(t == first + cnt - 1, 2 * first, 2 * t + 2), 0

    def edge(i):
        t, first, cnt = phase(i)
        return t == first, t == first + cnt - 1

    return prev, nxt, edge, off


def _halo_fix(prev, nxt, is_first, is_last):
    hb, d = prev.shape
    k = lax.broadcasted_iota(jnp.int32, (hb // 8, 8, d), 1)
    p3 = prev.reshape(hb // 8, 8, d)
    n3 = nxt.reshape(hb // 8, 8, d)
    p_roll = jnp.where((k % 4) == 0, 0.0, pltpu.roll(p3, 1, 1))
    n_roll = jnp.where((k % 4) == 3, 0.0, pltpu.roll(n3, 7, 1))
    p3 = jnp.where(is_first, p_roll, p3)
    n3 = jnp.where(is_last, n_roll, n3)
    return p3.reshape(hb, d), n3.reshape(hb, d)


def _dwconv_ln(cfg, name, rows, ag, w_dw, b_dw, ln_g, ln_b):
    D, TM, HB, KW, half = cfg.D, cfg.TM, cfg.HB, cfg.KW, cfg.half
    prev_map, next_map, edge, off = _halo_maps(cfg, rows)
    spec, nt, _ = _tok_specs(cfg, rows, D)

    def body(cur_ref, prev_ref, next_ref, w_ref, b_ref, g_ref, bi_ref, cv_ref, s_ref, ext):
        i = pl.program_id(0)
        is_first, is_last = edge(i)

        @pl.when(i >= 0)
        def _():
            p, n = _halo_fix(prev_ref[...], next_ref[...], is_first, is_last)
            ext[0:HB, :] = p
            ext[HB:HB + TM, :] = cur_ref[...]
            ext[HB + TM:, :] = n

        acc = jnp.zeros((TM, D), F32)
        for k in range(KW):
            lo = HB + 8 * (k - half)
            acc = acc + w_ref[k:k + 1, :] * ext[lo:lo + TM, :]
        cv_ref[...] = acc + b_ref[...]
        xhat, _ = _ln_stats(cv_ref[...])
        nn = xhat * g_ref[...] + bi_ref[...]
        s_ref[...] = (nn * _sigmoid(nn)).astype(BF16)

    T = cfg.T
    return pl.pallas_call(
        body, name=name, grid=(nt,),
        in_specs=[spec, pl.BlockSpec((HB, D), prev_map), pl.BlockSpec((HB, D), next_map), _full((KW, D)),
                  _full((1, D)), _full((1, D)), _full((1, D))],
        out_specs=[spec, spec], out_shape=[_sds((T, D)), _sds((T, D), BF16)],
        scratch_shapes=[pltpu.VMEM((TM + 2 * HB, D), F32)], compiler_params=_cp(1))(ag, ag, ag, w_dw, b_dw, ln_g, ln_b)


def _pw2_ln(cfg, name, rows, s, tok, mod8, cg, w, b, gain, bias):
    D, TM = cfg.D, cfg.TM
    spec, nt, off = _tok_specs(cfg, rows, D)
    kb = D // 4

    def body(s_ref, tok_ref, mod_ref, w_ref, b_ref, g_ref, bi_ref, x1_ref, r1_ref, mix_ref):
        sv = s_ref[...]
        mix = b_ref[...] + jnp.zeros((TM, D), F32)
        for q in range(4):
            mix = mix + _dot(sv[:, q * kb:(q + 1) * kb], w_ref[q])
        r1 = DN_ALPHA * tok_ref[...] + _rowscale(mix, mod_ref[:, cg:cg + D])
        xhat, _ = _ln_stats(r1)
        x1_ref[...] = xhat * g_ref[...] + bi_ref[...]
        r1_ref[...] = r1
        mix_ref[...] = mix.astype(BF16)

    T = cfg.T
    return pl.pallas_call(
        body, name=name, grid=(nt,),
        in_specs=[spec, spec, _mod_spec(cfg, mod8, off), _wspec(w), _full((1, D)), _full((1, D)), _full((1, D))],
        out_specs=[spec, spec, spec], out_shape=[_sds((T, D)), _sds((T, D)), _sds((T, D), BF16)],
        compiler_params=_cp(1))(s, tok, mod8, w[0], b, gain, bias)


def _loss(cfg, xf, tgt):
    D, TM = cfg.D, cfg.TM
    spec, nt, off = _tok_specs(cfg, cfg.rows(False), D)

    def body(x_ref, t_ref, l_ref, dx_ref, acc):
        i = pl.program_id(0)
        dlt = x_ref[...] - t_ref[...]

        @pl.when(i == 0)
        def _():
            acc[...] = jnp.zeros_like(acc)

        acc[...] += _sum8(dlt * dlt)
        dx_ref[...] = dlt * (1.0 / D)

        @pl.when(i == nt - 1)
        def _():
            l_ref[...] = jnp.zeros((8, LANES), F32) + jnp.sum(acc[...]) * (0.5 / D)

    return pl.pallas_call(
        body, name="loss", grid=(nt,),
        in_specs=[spec, pl.BlockSpec((TM, D), lambda i: (i, 0))],
        out_specs=[_full((8, LANES)), spec], out_shape=[_sds((8, LANES)), _sds((cfg.T, D))],
        scratch_shapes=[pltpu.VMEM((8, D), F32)], compiler_params=_cp(1))(xf, tgt)


def _masked_spec(cfg, rows, width, valid_from_tile):
    tm = cfg.TM
    off = rows[0] // tm
    return pl.BlockSpec((tm, width), lambda i: (jnp.maximum(i + off, valid_from_tile), 0))


def _lnb(cfg, name, rows, dres, dres_ctx_ok, dh, r, aux, gain, mod_gate, cg, mod_next, csc, dbr_dtype=F32):
    D, TM = cfg.D, cfg.TM
    spec, nt, off = _tok_specs(cfg, rows, D)
    nct = cfg.Tc // TM
    has_dres, has_dh = dres is not None, dh is not None

    def body(*refs):
        refs = list(refs)
        dres_ref = refs.pop(0) if has_dres else None
        dh_ref = refs.pop(0) if has_dh else None
        r_ref, aux_ref, g_ref, mg_ref = refs[:4]
        refs = refs[4:]
        mn_ref = refs.pop(0) if has_dh else None
        dprev_ref, dbr_ref, dgain_ref, dbias_ref, dg_ref, dsc_ref, dsh_ref, acc_g, acc_b = refs
        i = pl.program_id(0)
        t = i + off
        first_of_stream = (i == 0) | (t == nct)
        xhat, rstd = _ln_stats(r_ref[...])
        dy = jnp.zeros((TM, D), F32)
        if has_dres:
            dv = dres_ref[...]
            if not dres_ctx_ok:
                dv = jnp.where(t >= nct, dv, 0.0)
            dy = dy + dv
        if has_dh:
            dhv = dh_ref[...]
            dy = dy + _rowscale(dhv, 1.0 + mn_ref[:, csc:csc + D])
            x_out = xhat * g_ref[0:1, :] + g_ref[1:2, :]
            s_sc, s_sh = _sum8(dhv * x_out), _sum8(dhv)
        else:
            s_sc = s_sh = jnp.zeros((8, D), F32)
        dr = _ln_bwd(dy * g_ref[0:1, :], xhat, rstd)
        s_g = _sum8(dr * aux_ref[...].astype(F32))

        @pl.when(i == 0)
        def _():
            acc_g[...] = jnp.zeros_like(acc_g)
            acc_b[...] = jnp.zeros_like(acc_b)

        acc_g[...] += _sum8(dy * xhat)
        acc_b[...] += _sum8(dy)

        @pl.when(first_of_stream)
        def _():
            dg_ref[...] = s_g
            dsc_ref[...] = s_sc
            dsh_ref[...] = s_sh

        @pl.when(jnp.logical_not(first_of_stream))
        def _():
            dg_ref[...] += s_g
            dsc_ref[...] += s_sc
            dsh_ref[...] += s_sh

        dprev_ref[...] = DN_ALPHA * dr
        dbr_ref[...] = _rowscale(dr, mg_ref[:, cg:cg + D]).astype(dbr_dtype)

        @pl.when(i == nt - 1)
        def _():
            dgain_ref[...] = jnp.sum(acc_g[...], axis=0, keepdims=True)
            dbias_ref[...] = jnp.sum(acc_b[...], axis=0, keepdims=True)

    st = _stream_of(cfg, off, TM)
    in_specs, args = [], []
    if has_dres:
        in_specs.append(spec if dres_ctx_ok else _masked_spec(cfg, rows, D, nct))
        args.append(dres)
    if has_dh:
        in_specs.append(spec)
        args.append(dh)
    in_specs += [spec, spec, _full((2, D)), _mod_spec(cfg, mod_gate, off)]
    args += [r, aux, gain, mod_gate]
    if has_dh:
        in_specs.append(_mod_spec(cfg, mod_next, off))
        args.append(mod_next)
    acc_spec = pl.BlockSpec((None, 8, D), lambda i: (st(i), 0, 0))
    T = cfg.T
    return pl.pallas_call(
        body, name=name, grid=(nt,), in_specs=in_specs,
        out_specs=[spec, spec, _full((1, D)), _full((1, D)), acc_spec, acc_spec, acc_spec],
        out_shape=[_sds((T, D)), _sds((T, D), dbr_dtype), _sds((1, D)), _sds((1, D)), _sds((2, 8, D)), _sds((2, 8, D)),
                   _sds((2, 8, D))],
        scratch_shapes=[pltpu.VMEM((8, D), F32), pltpu.VMEM((8, D), F32)], compiler_params=_cp(1))(*args)


def _mlp_bwd(cfg, name, rows, dout, a, w1, w2):
    D, TM = cfg.D, cfg.TM
    spec, nt, off = _tok_specs(cfg, rows, D)
    specf, _, _ = _tok_specs(cfg, rows, cfg.F)
    fb = cfg.F // 4

    def body(d_ref, a_ref, w1_ref, w2_ref, dh_ref, dhid_ref):
        dout = d_ref[...]
        dh = jnp.zeros((TM, D), F32)
        for s in range(4):
            da = _dot_nt(dout, w2_ref[s])
            dhid = (da * (2.0 * jnp.sqrt(a_ref[:, s * fb:(s + 1) * fb].astype(F32)))).astype(BF16)
            dhid_ref[:, s * fb:(s + 1) * fb] = dhid
            dh = dh + _dot_nt(dhid, w1_ref[s])
        dh_ref[...] = dh

    T = cfg.T
    return pl.pallas_call(
        body, name=name, grid=(nt,),
        in_specs=[spec, specf, _wspec(w1), _wspec(w2)],
        out_specs=[spec, specf],
        out_shape=[_sds((T, D)), _sds((T, cfg.F), BF16)], compiler_params=_cp(1))(dout, a, w1[0], w2[0])


def _wgrad(cfg, name, rows, a, b, mode, fam, slot):
    tw = cfg.TW
    off = rows[0] // tw
    ntile = rows[1] // tw
    tps = next(q for q in (4, 3, 2, 1) if ntile % q == 0)
    nt = ntile // tps
    fresh = not hasattr(fam, "dtype")
    fam_shape = tuple(fam) if fresh else fam.shape
    _, n, kk, nn = fam_shape

    def body(*refs):
        a_refs, b_refs, o_ref = refs[:tps], refs[tps:2 * tps], refs[-1]
        t = pl.program_id(1)
        part = _dot_tn(a_refs[0][...], b_refs[0][...])
        for q in range(1, tps):
            part = part + _dot_tn(a_refs[q][...], b_refs[q][...])

        @pl.when(t == 0)
        def _():
            o_ref[...] = part

        @pl.when(t > 0)
        def _():
            o_ref[...] += part

    def row(q):
        return lambda s, t: t * tps + q + off

    if mode == "col":
        a_specs = [pl.BlockSpec((tw, kk), lambda s, t, r=row(q): (r(s, t), 0)) for q in range(tps)]
        b_specs = [pl.BlockSpec((tw, nn), lambda s, t, r=row(q): (r(s, t), s)) for q in range(tps)]
    else:
        a_specs = [pl.BlockSpec((tw, kk), lambda s, t, r=row(q): (r(s, t), s)) for q in range(tps)]
        b_specs = [pl.BlockSpec((tw, nn), lambda s, t, r=row(q): (r(s, t), 0)) for q in range(tps)]
    out_spec = pl.BlockSpec((None, None, kk, nn), lambda s, t: (s, slot, 0, 0))
    ins = [a] * tps + [b] * tps
    if fresh:
        return pl.pallas_call(body, name=name, grid=(4, nt), in_specs=a_specs + b_specs, out_specs=out_spec,
                              out_shape=_sds(fam_shape), compiler_params=_cp(2))(*ins)
    return pl.pallas_call(body, name=name, grid=(4, nt), in_specs=a_specs + b_specs + [ANY], out_specs=out_spec,
                          out_shape=_sds(fam_shape), input_output_aliases={2 * tps: 0}, compiler_params=_cp(2))(*ins, fam)


def _glu_bwd(cfg, name, rows, dmix, pre, w, yy=None):
    D, TM = cfg.D, cfg.TM
    spec, nt, off = _tok_specs(cfg, rows, D)
    spec2, _, _ = _tok_specs(cfg, rows, 2 * D)
    hw = w[0].shape[-1]
    has_y = yy is not None

    def body(*refs):
        refs = list(refs)
        d_ref, p_ref, w_ref = refs[:3]
        y_ref = refs[3] if has_y else None
        dz_ref, dp_ref, db_ref, acc = refs[-4:]
        i = pl.program_id(0)
        dm = d_ref[...]
        po, pg = p_ref[:, :D].astype(F32), p_ref[:, D:].astype(F32)
        sg = _sigmoid(pg)
        dpre = jnp.concatenate([dm * sg, dm * po * sg * (1.0 - sg)], axis=-1)

        @pl.when(i == 0)
        def _():
            acc[...] = jnp.zeros_like(acc)

        acc[...] += _sum8(dpre)
        dpb = dpre.astype(BF16)
        dz = jnp.zeros((TM, D), F32)
        for s in range(4):
            dz = dz + _dot_nt(dpb[:, s * hw:(s + 1) * hw], w_ref[s])
        if has_y:
            dz = dz * _gelu_grad(y_ref[...].astype(F32))
        dz_ref[...] = dz
        dp_ref[...] = dpb

        @pl.when(i == nt - 1)
        def _():
            db_ref[...] = jnp.sum(acc[...], axis=0, keepdims=True)

    T = cfg.T
    in_specs = [spec, spec2, _wspec(w)] + ([spec] if has_y else [])
    args = [dmix, pre, w[0]] + ([yy] if has_y else [])
    return pl.pallas_call(
        body, name=name, grid=(nt,), in_specs=in_specs, out_specs=[spec, spec2, _full((1, 2 * D))],
        out_shape=[_sds((T, D)), _sds((T, 2 * D), BF16), _sds((1, 2 * D))],
        scratch_shapes=[pltpu.VMEM((8, 2 * D), F32)], compiler_params=_cp(1))(*args)


def _s5_du(cfg, name, rows, du, dy, dy_from_tile, tok, mod8, cols, dskip):
    D, TM = cfg.D, cfg.TM
    csh, csc = cols
    spec, nt, off = _tok_specs(cfg, rows, D)

    def body(du_ref, dy_ref, tok_ref, mod_ref, ds_ref, dh_ref, dd_ref, acc):
        i = pl.program_id(0)
        dyv = jnp.where(i + off >= dy_from_tile, dy_ref[...], 0.0)
        u = _mod(tok_ref[...], mod_ref[:, csh:csh + D], mod_ref[:, csc:csc + D])
        dh_ref[...] = du_ref[0] + du_ref[1] + ds_ref[...] * dyv

        @pl.when(i == 0)
        def _():
            acc[...] = jnp.zeros_like(acc)

        acc[...] += _sum8(dyv * u)

        @pl.when(i == nt - 1)
        def _():
            dd_ref[...] = jnp.sum(acc[...], axis=0, keepdims=True)

    T = cfg.T
    return pl.pallas_call(
        body, name=name, grid=(nt,),
        in_specs=[pl.BlockSpec((2, TM, D), lambda i: (0, i + off, 0)), _masked_spec(cfg, rows, D, dy_from_tile), spec,
                  _mod_spec(cfg, mod8, off), _full((1, D))],
        out_specs=[spec, _full((1, D))], out_shape=[_sds((T, D)), _sds((1, D))],
        scratch_shapes=[pltpu.VMEM((8, D), F32)], compiler_params=_cp(1))(du, dy, tok, mod8, dskip)


def _pw2_bwd(cfg, name, rows, dmix, cv, w, ln_g, ln_b):
    D, TM = cfg.D, cfg.TM
    spec, nt, off = _tok_specs(cfg, rows, D)
    kb = D // 4

    def body(d_ref, cv_ref, w_ref, g_ref, b_ref, dcv_ref, dmb_ref, sums_ref, acc):
        i = pl.program_id(0)
        dm = d_ref[...]
        dmb = dm.astype(BF16)
        ds = jnp.concatenate([_dot_nt(dmb, w_ref[q]) for q in range(4)], axis=-1)
        xhat, rstd = _ln_stats(cv_ref[...])
        nn = xhat * g_ref[...] + b_ref[...]
        sg = _sigmoid(nn)
        dn = ds * (sg * (1.0 + nn * (1.0 - sg)))
        dcv = _ln_bwd(dn * g_ref[...], xhat, rstd)

        @pl.when(i == 0)
        def _():
            acc[...] = jnp.zeros_like(acc)

        acc[0] += _sum8(dn * xhat)
        acc[1] += _sum8(dn)
        acc[2] += _sum8(dcv)
        acc[3] += _sum8(dm)
        dcv_ref[...] = dcv
        dmb_ref[...] = dmb

        @pl.when(i == nt - 1)
        def _():
            for q in range(4):
                sums_ref[q:q + 1, :] = jnp.sum(acc[q], axis=0, keepdims=True)

    T = cfg.T
    return pl.pallas_call(
        body, name=name, grid=(nt,),
        in_specs=[spec, spec, _wspec(w), _full((1, D)), _full((1, D))],
        out_specs=[spec, spec, _full((4, D))], out_shape=[_sds((T, D)), _sds((T, D), BF16), _sds((4, D))],
        scratch_shapes=[pltpu.VMEM((4, 8, D), F32)], compiler_params=_cp(1))(dmix, cv, w[0], ln_g, ln_b)


def _dwconv_bwd(cfg, name, rows, dcv, ag, w_dw):
    D, TM, HB, KW, half = cfg.D, cfg.TM, cfg.HB, cfg.KW, cfg.half
    prev_map, next_map, edge, off = _halo_maps(cfg, rows)
    spec, nt, _ = _tok_specs(cfg, rows, D)

    def body(dc_ref, dp_ref, dn_ref, ac_ref, ap_ref, an_ref, w_ref, dag_ref, dw_ref, extd, exta, acc):
        i = pl.program_id(0)
        is_first, is_last = edge(i)

        @pl.when(i >= 0)
        def _():
            p, n = _halo_fix(dp_ref[...], dn_ref[...], is_first, is_last)
            extd[0:HB, :] = p
            extd[HB:HB + TM, :] = dc_ref[...]
            extd[HB + TM:, :] = n
            p, n = _halo_fix(ap_ref[...], an_ref[...], is_first, is_last)
            exta[0:HB, :] = p
            exta[HB:HB + TM, :] = ac_ref[...]
            exta[HB + TM:, :] = n

        @pl.when(i == 0)
        def _():
            acc[...] = jnp.zeros_like(acc)

        cr = min(CONV_ROWS, TM)
        for r0 in range(0, TM, cr):
            for lc in range(D // LANES):
                ls = pl.ds(lc * LANES, LANES)
                dcur = dc_ref[r0:r0 + cr, ls]
                dag = jnp.zeros((cr, LANES), F32)
                for k in range(KW):
                    lo = r0 + HB + 8 * (half - k)
                    la = r0 + HB + 8 * (k - half)
                    dag = dag + w_ref[k:k + 1, ls] * extd[lo:lo + cr, ls]
                    acc[k, :, ls] += _sum8(dcur * exta[la:la + cr, ls])
                dag_ref[r0:r0 + cr, ls] = dag

        @pl.when(i == nt - 1)
        def _():
            for k in range(KW):
                dw_ref[k:k + 1, :] = jnp.sum(acc[k], axis=0, keepdims=True)

    T = cfg.T
    hp, hn = pl.BlockSpec((HB, D), prev_map), pl.BlockSpec((HB, D), next_map)
    return pl.pallas_call(
        body, name=name, grid=(nt,), in_specs=[spec, hp, hn, spec, hp, hn, _full((KW, D))],
        out_specs=[spec, _full((KW, D))], out_shape=[_sds((T, D)), _sds((KW, D))],
        scratch_shapes=[pltpu.VMEM((TM + 2 * HB, D), F32), pltpu.VMEM((TM + 2 * HB, D), F32), pltpu.VMEM((KW, 8, D), F32)],
        compiler_params=_cp(1))(dcv, dcv, dcv, ag, ag, ag, w_dw)


def _input_bwd(cfg, dres, dh, tok0, mod8, csc):
    D, TM = cfg.D, cfg.TM
    rows = cfg.rows(True)
    spec, nt, off = _tok_specs(cfg, rows, D)
    nct = cfg.Tc // TM
    st = _stream_of(cfg, off, TM)

    def body(dr_ref, dh_ref, t_ref, mod_ref, gx_ref, dsc_ref, dsh_ref):
        i = pl.program_id(0)
        dhv = dh_ref[...]
        gx_ref[...] = dr_ref[...] + _rowscale(dhv, 1.0 + mod_ref[:, csc:csc + D])
        first = (i == 0) | (i == nct)
        s_sc, s_sh = _sum8(dhv * t_ref[...]), _sum8(dhv)

        @pl.when(first)
        def _():
            dsc_ref[...] = s_sc
            dsh_ref[...] = s_sh

        @pl.when(jnp.logical_not(first))
        def _():
            dsc_ref[...] += s_sc
            dsh_ref[...] += s_sh

    acc_spec = pl.BlockSpec((None, 8, D), lambda i: (st(i), 0, 0))
    return pl.pallas_call(
        body, name="input_bwd", grid=(nt,), in_specs=[spec, spec, spec, _mod_spec(cfg, mod8, off)],
        out_specs=[spec, acc_spec, acc_spec], out_shape=[_sds((cfg.T, D)), _sds((2, 8, D)), _sds((2, 8, D))],
        compiler_params=_cp(1))(dres, dh, tok0, mod8)


def _dmod_rows(dmod8):
    nl, _, _, w = dmod8.shape

    def body(d_ref, o_ref):
        xs = d_ref[1]
        cs = d_ref[0]
        o_ref[...] = jnp.zeros((8, w), F32)
        o_ref[0:1, :] = jnp.sum(xs[0:4], axis=0, keepdims=True)
        o_ref[1:2, :] = jnp.sum(xs[4:8], axis=0, keepdims=True)
        o_ref[2:3, :] = jnp.sum(cs, axis=0, keepdims=True)

    return pl.pallas_call(body, name="dmod_rows", grid=(nl,),
                          in_specs=[pl.BlockSpec((None, 2, 8, w), lambda l: (l, 0, 0, 0))],
                          out_specs=pl.BlockSpec((None, 8, w), lambda l: (l, 0, 0)), out_shape=_sds((nl, 8, w)),
                          compiler_params=_cp(1))(dmod8)


def _x_only(acc):
    return jnp.concatenate([jnp.zeros_like(acc[:1]), acc[1:]], axis=0)


def _pack(parts):
    bufs, meta, off = [], [], 0
    for p in parts:
        n = p.size
        rows = -(-n // (8 * LANES)) * 8
        flat = p.reshape(-1).astype(F32)
        if rows * LANES != n:
            flat = jnp.pad(flat, (0, rows * LANES - n))
        flat = flat.reshape(rows, LANES)
        bufs.append(flat)
        meta.append((off, rows, p.shape))
        off += rows
    if off % 16:
        bufs.append(jnp.zeros((8, LANES), F32))
    return jnp.concatenate(bufs, axis=0), meta


def _unpack(buf, meta):
    out = []
    for off, rows, shape in meta:
        n = 1
        for s in shape:
            n *= s
        out.append(buf[off:off + rows].reshape(-1)[:n].reshape(shape))
    return out


def kernel(x, c, ctx, c_ctx, w_ada, b_ada, ln_gain, ln_bias, s5_lam_re, s5_lam_im, s5_log_dt, s5_b_re, s5_b_im, s5_c_re, s5_c_im, s5_d, s5_w_glu, s5_b_glu, cv_w_pw1, cv_b_pw1, cv_w_dw, cv_b_dw, cv_ln_g, cv_ln_b, cv_w_pw2, cv_b_pw2, mlp_w1, mlp_w2, loss_target, m_c_ctx, m_w_ada, m_b_ada, m_ln_gain, m_ln_bias, m_s5_lam_re, m_s5_lam_im, m_s5_log_dt, m_s5_b_re, m_s5_b_im, m_s5_c_re, m_s5_c_im, m_s5_d, m_s5_w_glu, m_s5_b_glu, m_cv_w_pw1, m_cv_b_pw1, m_cv_w_dw, m_cv_b_dw, m_cv_ln_g, m_cv_ln_b, m_cv_w_pw2, m_cv_b_pw2, m_mlp_w1, m_mlp_w2, v_c_ctx, v_w_ada, v_b_ada, v_ln_gain, v_ln_bias, v_s5_lam_re, v_s5_lam_im, v_s5_log_dt, v_s5_b_re, v_s5_b_im, v_s5_c_re, v_s5_c_im, v_s5_d, v_s5_w_glu, v_s5_b_glu, v_cv_w_pw1, v_cv_b_pw1, v_cv_w_dw, v_cv_b_dw, v_cv_ln_g, v_cv_ln_b, v_cv_w_pw2, v_cv_b_pw2, v_mlp_w1, v_mlp_w2):
    cfg = _Cfg(x, ctx, mlp_w1, cv_w_dw)
    D, T, Tc, Tx, B = cfg.D, cfg.T, cfg.Tc, cfg.Tx, cfg.B
    ax, ay, ac = lax.axis_index("x"), lax.axis_index("y"), lax.axis_index("c")
    shard = 2 * ax + ay
    dev = 4 * ax + 2 * ay + ac
    Ds = D // 4
    Wa = w_ada.shape[2]

    c_pad = jnp.concatenate([c, jnp.zeros((8 - B, D), F32)], axis=0)
    c_gath = _allgather8("gather_c", c_pad).reshape(8, 8, D)[:, :B].reshape(8 * B, D)
    c_all = jnp.concatenate([c_gath, c_ctx[None], jnp.zeros((7, D), F32)], axis=0)
    b_sh = lax.dynamic_slice_in_dim(b_ada, shard * Wa, Wa, axis=1)[:, None, :]
    mod_sh = _ada_fwd(c_all, w_ada, b_sh)
    mod_g = _allgather8("gather_mod", mod_sh.reshape(DEPTH * 24, Wa)).reshape(4, 2, DEPTH, 24, Wa)[:, 0]
    mods = mod_g.transpose(1, 2, 0, 3).reshape(DEPTH, 24, 4 * Wa)
    mine = lax.dynamic_slice_in_dim(mods, B * dev, B, axis=1)
    mod8 = jnp.stack([jnp.broadcast_to(mods[:, 16:17], (DEPTH, 8, 6 * D)), jnp.repeat(mine, 4, axis=1)], axis=1)
    SH1, SC1, G1, SH2, SC2, G2 = (k * D for k in range(6))

    small_parts = [ln_gain.reshape(-1, Ds), ln_bias.reshape(-1, Ds), cv_b_pw1.reshape(-1, Ds), cv_w_dw.reshape(-1, Ds),
                   cv_b_dw, cv_ln_g, cv_ln_b, cv_b_pw2]
    small_rows = [p.shape[0] for p in small_parts]
    sm = jnp.concatenate(small_parts, axis=0)
    pad_r = -sm.shape[0] % 8
    sm = jnp.pad(sm, ((0, pad_r), (0, 0)))
    sm_g = _allgather8("gather_small", sm).reshape(4, 2, sm.shape[0], Ds)[:, 0]
    pieces, o = [], 0
    for nr in small_rows:
        pieces.append(sm_g[:, o:o + nr])
        o += nr

    def unshard(p, lead):
        return p.reshape((4,) + lead + (Ds,)).transpose(tuple(range(1, len(lead) + 1)) + (0, len(lead) + 1)).reshape(lead + (4 * Ds,))

    ln_gain_f = unshard(pieces[0], (DEPTH, 2))
    ln_bias_f = unshard(pieces[1], (DEPTH, 2))
    nconv = cv_w_dw.shape[0]
    b_pw1_f = pieces[2].reshape(4, nconv, 2 * D // 4).transpose(1, 0, 2).reshape(nconv, 2 * D)
    w_dw_f = unshard(pieces[3], (nconv, cfg.KW))
    b_dw_f, cvg_f, cvb_f, b_pw2_f = (unshard(p, (nconv,)) for p in pieces[4:8])

    ns5 = s5_w_glu.shape[0]
    assert mlp_w1.shape[1:] == mlp_w2.shape[1:]
    fam_a = _place_shard("place_w1", mlp_w1, None, 0, 2 * DEPTH)
    fam_a = _place_shard("place_w2", mlp_w2, fam_a, DEPTH, 2 * DEPTH)
    fam_b = _place_shard("place_wglu", s5_w_glu, None, 0, ns5 + nconv)
    fam_b = _place_shard("place_wpw1", cv_w_pw1, fam_b, ns5, ns5 + nconv)
    fam_c = _place_shard("place_wpw2", cv_w_pw2, None, 0, nconv)
    ov = _Overlap()
    gather_tokens = ov.add("gather_b", _gather_gen("gatherb", [fam_b])) + ov.add("gather", _gather_gen("gatherw", [fam_a, fam_c]))

    pos = jnp.broadcast_to(_pos_embed(cfg.L // GRID_W, D)[None], (B, cfg.L, D))
    tok_in = jnp.concatenate([_to_perm(ctx), _to_perm(x)], axis=0)
    pos_in = jnp.concatenate([jnp.zeros((Tc, D), F32), _to_perm(pos)], axis=0)
    tok0 = _ew("add_pos", lambda a, b: a + b, [tok_in, pos_in], [_sds((T, D))])[0]
    mod8 = _tie(mod8, gather_tokens)
    tgt = _to_perm(loss_target)

    def lead(t):
        return t.reshape((2 * ns5,) + t.shape[2:])

    s5_lay = _s5_layouts(cfg, lead(s5_lam_re), lead(s5_lam_im), lead(s5_log_dt), lead(s5_b_re), lead(s5_b_im))
    abr, abi, bbr, bbi = _disc_fwd(*s5_lay)
    acoef_all, acoef_adj_all = _coef_rows(cfg, abr, abi, False), _coef_rows(cfg, abr, abi, True)
    bf_all, cf_all = _blockdiag_b(cfg, bbr, bbi), _blockdiag_c(cfg, lead(s5_c_re), lead(s5_c_im))
    s5p = [dict(acoef=acoef_all[2 * j:2 * j + 2], acoef_adj=acoef_adj_all[2 * j:2 * j + 2], bf=bf_all[2 * j:2 * j + 2],
                cf=cf_all[2 * j:2 * j + 2]) for j in range(ns5)]

    kinds = ["s5" if i % 2 == 0 else "conv" for i in range(DEPTH)]
    tok = tok0
    saved = []
    s5_j = cv_j = 0
    for i in range(DEPTH):
        later_s5 = any(k == "s5" for k in kinds[i + 1:])
        rows = cfg.rows(later_s5)
        m8 = mod8[i]
        sv = dict(tok=tok, rows=rows, kind=kinds[i])
        g0, b0 = ln_gain_f[i, 0][None], ln_bias_f[i, 0][None]
        g1, b1 = ln_gain_f[i, 1][None], ln_bias_f[i, 1][None]
        if kinds[i] == "s5":
            j = s5_j
            s5_j += 1
            p = s5p[j]
            y, ck = _s5_forward(cfg, f"l{i}", tok, m8, SH1, SC1, p["bf"], p["cf"], p["acoef"], ov.point if i == 0 else None)
            m8g = m8
            if i == 0:
                m8g = _tie(m8, ov.point(y))
                (wb_full,) = ov.finish("gather_b", y)
            wg = (wb_full, j)
            x1, r1, mix, zz, zb, yy = _glu_ln(cfg, f"l{i}_glu", rows, tok, y, m8g, (SH1, SC1, G1), s5_d[j][None], wg,
                                              s5_b_glu[j][None], g0, b0)
            sv.update(j=j, ck=ck, zz=zz, zb=zb, yy=yy, wg=wg)
            if i == 0:
                wa_full, wc_full = ov.finish("gather", x1)
        else:
            j = cv_j
            cv_j += 1
            w1c, w2c = (wb_full, ns5 + j), (wc_full, j)
            aa, ag, hb = _pw1_glu(cfg, f"l{i}_pw1", rows, tok, m8, (SH1, SC1), w1c, b_pw1_f[j][None])
            cvv, sb = _dwconv_ln(cfg, f"l{i}_dw", rows, ag, w_dw_f[j], b_dw_f[j][None], cvg_f[j][None], cvb_f[j][None])
            x1, r1, mix = _pw2_ln(cfg, f"l{i}_pw2", rows, sb, tok, m8, G1, w2c, b_pw2_f[j][None], g0, b0)
            sv.update(j=j, aa=aa, ag=ag, hb=hb, cvv=cvv, sb=sb, w1c=w1c, w2c=w2c)
        w1m, w2m = (wa_full, i), (wa_full, DEPTH + i)
        x2, r2, mout, am, hm = _mlp_ln(cfg, f"l{i}_mlp", rows, x1, m8, (SH2, SC2, G2), w1m, w2m, g1, b1)
        sv.update(r1=r1, mix=mix, x1=x1, r2=r2, mout=mout, am=am, hm=hm, w1m=w1m, w2m=w2m, g0=g0, b0=b0, g1=g1, b1=b1)
        saved.append(sv)
        tok = x2

    loss8, dxf = _loss(cfg, tok, tgt)
    loss = lax.psum(loss8[0, 0], ("x", "y", "c"))

    dmod8 = [None] * DEPTH
    g_ln_gain = [[None, None] for _ in range(DEPTH)]
    g_ln_bias = [[None, None] for _ in range(DEPTH)]
    g_s5 = [None] * ns5
    g_cv = [None] * nconv
    dres, dh = dxf, None
    pend = []
    for i in reversed(range(DEPTH)):
        sv = saved[i]
        rows = sv["rows"]
        m8 = mod8[i]
        nxt_m8 = mod8[i + 1] if i + 1 < DEPTH else None
        ctx_ok = True if i + 1 >= DEPTH else (saved[i + 1]["rows"][0] == 0)
        if rows[0] != 0:
            ctx_ok = True
        dprev, dbr, dgn, dbs, dg2, dsc_n, dsh_n = _lnb(
            cfg, f"l{i}_lnb2", rows, dres, ctx_ok, dh, sv["r2"], sv["mout"],
            _tie(jnp.concatenate([sv["g1"], sv["b1"]], 0), pend), m8, G2, nxt_m8, SC1, dbr_dtype=BF16)
        if rows[0] != 0:
            dg2, dsc_n, dsh_n = (_x_only(t) for t in (dg2, dsc_n, dsh_n))
        g_ln_gain[i][1], g_ln_bias[i][1] = dgn[0], dbs[0]
        if i + 1 < DEPTH:
            dmod8[i + 1]["sc1"], dmod8[i + 1]["sh1"] = dsc_n, dsh_n
        dmod8[i] = dict(g2=dg2)
        dh2, dhid = _mlp_bwd(cfg, f"l{i}_mlpb", rows, dbr, sv["am"], sv["w1m"], sv["w2m"])
        pend = ov.point(dh2)
        ga = _wgrad(cfg, f"l{i}_gw1", rows, sv["hm"], dhid, "col", (4, 2, D, cfg.F // 4), 0)
        ga = _wgrad(cfg, f"l{i}_gw2", rows, sv["am"], dbr, "row", ga, 1)
        dprev1, dbr1, dgn, dbs, dg1, dsc2, dsh2 = _lnb(
            cfg, f"l{i}_lnb1", rows, dprev, True, dh2, sv["r1"], sv["mix"],
            _tie(jnp.concatenate([sv["g0"], sv["b0"]], 0), pend), m8, G1, m8, SC2)
        if rows[0] != 0:
            dg1, dsc2, dsh2 = (_x_only(t) for t in (dg1, dsc2, dsh2))
        g_ln_gain[i][0], g_ln_bias[i][0] = dgn[0], dbs[0]
        dmod8[i].update(g1=dg1, sc2=dsc2, sh2=dsh2)
        j = sv["j"]
        if sv["kind"] == "s5":
            p = s5p[j]
            dyy, dzz, dbglu = _glu_bwd(cfg, f"l{i}_glub", rows, dbr1, sv["zz"], sv["wg"], sv["yy"])
            pend = ov.point(dyy)
            gb = _wgrad(cfg, f"l{i}_gwg", rows, sv["zb"], dzz, "col", (4, 1, D, D // 2), 0)
            du, (da, dbf, dcf) = _s5_backward(cfg, f"l{i}", dyy, rows[0] == 0, sv["tok"], m8, SH1, SC1, p["bf"], p["cf"],
                                              p["acoef"], _tie(p["acoef_adj"], pend), sv["ck"])
            dh, dds = _s5_du(cfg, f"l{i}_du", cfg.rows(True), du, dyy, rows[0] // cfg.TM, sv["tok"], m8, (SH1, SC1),
                             s5_d[j][None])
            g_s5[j] = dict(da=da, dbf=dbf, dcf=dcf, dd=dds[0], dbglu=dbglu[0])
            layer_grads = [ga, gb]
        else:
            dcv, dmb, sums = _pw2_bwd(cfg, f"l{i}_pw2b", rows, dbr1, sv["cvv"], sv["w2c"], cvg_f[j][None], cvb_f[j][None])
            pend = ov.point(dcv)
            gc = _wgrad(cfg, f"l{i}_gwp2", rows, sv["sb"], dmb, "row", (4, 1, D // 4, D), 0)
            dag, dwdw = _dwconv_bwd(cfg, f"l{i}_dwb", rows, dcv, sv["ag"], _tie(w_dw_f[j], pend))
            dh, daa, dbpw1 = _glu_bwd(cfg, f"l{i}_pw1b", rows, dag, sv["aa"], sv["w1c"])
            gb = _wgrad(cfg, f"l{i}_gwp1", rows, sv["hb"], daa, "col", (4, 1, D, D // 2), 0)
            g_cv[j] = dict(ln_g=sums[0], ln_b=sums[1], b_dw=sums[2], b_pw2=sums[3], w_dw=dwdw, b_pw1=dbpw1[0])
            layer_grads = [ga, gb, gc]
        dres = dprev1
        pend = ov.point(dh) + ov.add(f"rs{i}", _reduce_scatter_gen(f"gw{i}", layer_grads))
    gx_perm, dsc0, dsh0 = _input_bwd(cfg, dres, dh, tok0, _tie(mod8[0], pend), SC1)
    dmod8[0]["sc1"], dmod8[0]["sh1"] = dsc0, dsh0
    grad_x = _from_perm(gx_perm[Tc:], B, cfg.L)

    zero28 = jnp.zeros((2, 8, D), F32)
    dm8 = jnp.stack([jnp.concatenate([dmod8[i].get(k, zero28) for k in ("sh1", "sc1", "g1", "sh2", "sc2", "g2")], axis=-1)
                     for i in range(DEPTH)])
    dm_rows = _dmod_rows(dm8)
    dm_tab = jnp.zeros((DEPTH, 24, 6 * D), F32)
    dm_tab = lax.dynamic_update_slice_in_dim(dm_tab, dm_rows[:, 0:B], B * dev, axis=1)
    dm_tab = lax.dynamic_update_slice_in_dim(dm_tab, dm_rows[:, 2:3], 16, axis=1)

    dbbr, dbbi = _diag_b(cfg, jnp.concatenate([g["dbf"] for g in g_s5], axis=0))
    dcr, dci = _diag_c(cfg, jnp.concatenate([g["dcf"] for g in g_s5], axis=0))
    nd = 2 * ns5
    da_s = _sublane_sum("s5_dasum", jnp.concatenate([g["da"] for g in g_s5], axis=0).reshape(2 * nd, 8, cfg.NS))
    eye_parts = [da_s, dbbr, dbbi, dcr, dci,
                 jnp.stack([g["dd"] for g in g_s5]), jnp.stack([g["dbglu"] for g in g_s5])]
    for j in range(nconv):
        g = g_cv[j]
        eye_parts += [g["ln_g"], g["ln_b"], g["b_dw"], g["b_pw2"], g["w_dw"], g["b_pw1"]]
    eye_parts += [jnp.stack([jnp.stack(r) for r in g_ln_gain]), jnp.stack([jnp.stack(r) for r in g_ln_bias]), dm_tab]
    buf, meta = _pack(eye_parts)
    buf = _tie(buf, ov.point(gx_perm))
    red_buf = _allreduce8("small", buf, ov.point)
    reduced = [ov.finish(f"rs{i}", red_buf) for i in range(DEPTH)]
    red = _unpack(red_buf, meta)

    grads = {}
    da, dbbr, dbbi, dcr, dci, dd, dbglu = red[0:7]
    k = 7
    da_s = da.reshape(nd, 2, cfg.NS)
    g_abr = da_s[:, 0].reshape(nd, cfg.G, cfg.P).transpose(2, 0, 1).reshape(cfg.P, nd * cfg.G)
    g_abi = da_s[:, 1].reshape(nd, cfg.G, cfg.P).transpose(2, 0, 1).reshape(cfg.P, nd * cfg.G)
    glr, gli, gldt, gbr, gbi = _disc_bwd(*s5_lay, g_abr, g_abi, dbbr, dbbi)
    grads.update(s5_lam_re=glr.reshape(cfg.P, nd, cfg.G).transpose(1, 2, 0), s5_lam_im=gli.reshape(cfg.P, nd, cfg.G).transpose(1, 2, 0),
                 s5_log_dt=gldt, s5_b_re=gbr.reshape(S5_GROUP, cfg.P, nd, cfg.G).transpose(2, 3, 1, 0),
                 s5_b_im=gbi.reshape(S5_GROUP, cfg.P, nd, cfg.G).transpose(2, 3, 1, 0), s5_c_re=dcr, s5_c_im=dci,
                 s5_d=dd, s5_b_glu=dbglu)

    def my_cols(full, width):
        return lax.dynamic_slice_in_dim(full, shard * width, width, axis=full.ndim - 1)

    cvs = {n: [] for n in ("ln_g", "ln_b", "b_dw", "b_pw2", "w_dw", "b_pw1")}
    for j in range(nconv):
        for n, val in zip(("ln_g", "ln_b", "b_dw", "b_pw2", "w_dw", "b_pw1"), red[k:k + 6]):
            cvs[n].append(val)
        k += 6
    grads.update(cv_ln_g=my_cols(jnp.stack(cvs["ln_g"]), Ds), cv_ln_b=my_cols(jnp.stack(cvs["ln_b"]), Ds),
                 cv_b_dw=my_cols(jnp.stack(cvs["b_dw"]), Ds), cv_b_pw2=my_cols(jnp.stack(cvs["b_pw2"]), Ds),
                 cv_w_dw=my_cols(jnp.stack(cvs["w_dw"]), Ds), cv_b_pw1=my_cols(jnp.stack(cvs["b_pw1"]), 2 * D // 4))
    grads.update(ln_gain=my_cols(red[k], Ds), ln_bias=my_cols(red[k + 1], Ds))
    dm_all = red[k + 2]

    dm_sh = lax.dynamic_slice_in_dim(dm_all, shard * Wa, Wa, axis=2)
    gw_ada, dcond = _ada_bwd(c_all, dm_sh, w_ada)
    grads["w_ada"] = gw_ada
    grads["b_ada"] = _colsum_groups("ada_bsum", dm_all)
    dc_part = dcond[0:1]
    dc_buf = jnp.concatenate([jnp.where(ac == 0, dc_part, 0.0), jnp.zeros((7, D), F32)], axis=0)
    dc_tot = _allreduce8("cctx", dc_buf.reshape(8 * D // LANES, LANES)).reshape(8, D)[0:1]
    grads["c_ctx"] = _ew("cctx_grad", lambda g, cv: g * (_sigmoid(cv) * (1.0 + cv * (1.0 - _sigmoid(cv)))),
                         [jnp.broadcast_to(dc_tot, (8, D)), jnp.broadcast_to(c_ctx[None], (8, D))], [_sds((8, D))])[0][0]

    s5_layers = [i for i in range(DEPTH) if kinds[i] == "s5"]
    cv_layers = [i for i in range(DEPTH) if kinds[i] == "conv"]
    grads.update(mlp_w1=jnp.stack([reduced[i][0][0] for i in range(DEPTH)]),
                 mlp_w2=jnp.stack([reduced[i][0][1] for i in range(DEPTH)]),
                 s5_w_glu=jnp.stack([reduced[i][1][0] for i in s5_layers]),
                 cv_w_pw1=jnp.stack([reduced[i][1][0] for i in cv_layers]),
                 cv_w_pw2=jnp.stack([reduced[i][2][0] for i in cv_layers]))

    weights = dict(c_ctx=c_ctx, w_ada=w_ada, b_ada=b_ada, ln_gain=ln_gain, ln_bias=ln_bias, s5_lam_re=s5_lam_re,
                   s5_lam_im=s5_lam_im, s5_log_dt=s5_log_dt, s5_b_re=s5_b_re, s5_b_im=s5_b_im, s5_c_re=s5_c_re,
                   s5_c_im=s5_c_im, s5_d=s5_d, s5_w_glu=s5_w_glu, s5_b_glu=s5_b_glu, cv_w_pw1=cv_w_pw1, cv_b_pw1=cv_b_pw1,
                   cv_w_dw=cv_w_dw, cv_b_dw=cv_b_dw, cv_ln_g=cv_ln_g, cv_ln_b=cv_ln_b, cv_w_pw2=cv_w_pw2, cv_b_pw2=cv_b_pw2,
                   mlp_w1=mlp_w1, mlp_w2=mlp_w2)
    ms = dict(c_ctx=m_c_ctx, w_ada=m_w_ada, b_ada=m_b_ada, ln_gain=m_ln_gain, ln_bias=m_ln_bias, s5_lam_re=m_s5_lam_re,
              s5_lam_im=m_s5_lam_im, s5_log_dt=m_s5_log_dt, s5_b_re=m_s5_b_re, s5_b_im=m_s5_b_im, s5_c_re=m_s5_c_re,
              s5_c_im=m_s5_c_im, s5_d=m_s5_d, s5_w_glu=m_s5_w_glu, s5_b_glu=m_s5_b_glu, cv_w_pw1=m_cv_w_pw1,
              cv_b_pw1=m_cv_b_pw1, cv_w_dw=m_cv_w_dw, cv_b_dw=m_cv_b_dw, cv_ln_g=m_cv_ln_g, cv_ln_b=m_cv_ln_b,
              cv_w_pw2=m_cv_w_pw2, cv_b_pw2=m_cv_b_pw2, mlp_w1=m_mlp_w1, mlp_w2=m_mlp_w2)
    vs = dict(c_ctx=v_c_ctx, w_ada=v_w_ada, b_ada=v_b_ada, ln_gain=v_ln_gain, ln_bias=v_ln_bias, s5_lam_re=v_s5_lam_re,
              s5_lam_im=v_s5_lam_im, s5_log_dt=v_s5_log_dt, s5_b_re=v_s5_b_re, s5_b_im=v_s5_b_im, s5_c_re=v_s5_c_re,
              s5_c_im=v_s5_c_im, s5_d=v_s5_d, s5_w_glu=v_s5_w_glu, s5_b_glu=v_s5_b_glu, cv_w_pw1=v_cv_w_pw1,
              cv_b_pw1=v_cv_b_pw1, cv_w_dw=v_cv_w_dw, cv_b_dw=v_cv_b_dw, cv_ln_g=v_cv_ln_g, cv_ln_b=v_cv_ln_b,
              cv_w_pw2=v_cv_w_pw2, cv_b_pw2=v_cv_b_pw2, mlp_w1=v_mlp_w1, mlp_w2=v_mlp_w2)
    names = list(weights)
    deltas, new_m, new_v = {}, {}, {}
    for n in names:
        g = grads[n].reshape(weights[n].shape)
        grads[n] = g
        deltas[n], new_m[n], new_v[n] = _adamw("adamw_" + n, weights[n], g, ms[n], vs[n])
    return (loss, grad_x, *[grads[n] for n in names], *[deltas[n] for n in names], *[new_m[n] for n in names],
            *[new_v[n] for n in names])


def _sublane_sum(name, a):
    n, _, w = a.shape

    def body(a_ref, o_ref):
        for q in range(n):
            o_ref[q:q + 1, :] = jnp.sum(a_ref[q], axis=0, keepdims=True)

    return pl.pallas_call(body, name=name, out_shape=_sds((n, w)))(a)


def _colsum_groups(name, dm_all):
    nl, nr, w = dm_all.shape

    def body(d_ref, o_ref):
        o_ref[...] = jnp.zeros((8, w), F32) + jnp.sum(d_ref[...], axis=0, keepdims=True)

    out = pl.pallas_call(body, name=name, grid=(nl,), in_specs=[pl.BlockSpec((None, nr, w), lambda l: (l, 0, 0))],
                         out_specs=pl.BlockSpec((None, 8, w), lambda l: (l, 0, 0)), out_shape=_sds((nl, 8, w)),
                         compiler_params=_cp(1))(dm_all)
    return out[:, 0]
```

```python
import math

import jax
import jax.numpy as jnp
from jax import lax
from jax.experimental import pallas as pl
from jax.experimental.pallas import tpu as pltpu

F32 = jnp.float32
BF16 = jnp.bfloat16
MESH = pl.DeviceIdType.MESH
ANY = pl.BlockSpec(memory_space=pl.ANY)

DEPTH = 4
S5_GROUP = 16
S5_STATE = 64
GRID_W = 64
POS_TEMP = 10000.0
LAMBDA_RE_MAX = -1e-4
LN_EPS = 1e-5
DN_ALPHA = (2.0 * DEPTH) ** 0.25
ADAM_LR, ADAM_B1, ADAM_B2, ADAM_EPS, ADAM_WD, ADAM_STEP = 0.001, 0.9, 0.999, 1e-08, 0.01, 10

SUBLANES = 8
LANES = 128
OCT_CH = 128
OCT_ST = 512
CONV_ROWS = 64
VMEM_LIMIT = 56 * 1024 * 1024


def _cp(n_axes):
    return pltpu.CompilerParams(dimension_semantics=("arbitrary",) * n_axes, vmem_limit_bytes=VMEM_LIMIT)


def _full(shape, single=False):
    nd = len(shape)
    if single:
        return pl.BlockSpec(shape, lambda *i: (0,) * nd, pipeline_mode=pl.Buffered(1))
    return pl.BlockSpec(shape, lambda *i: (0,) * nd)


def _sds(shape, dtype=F32):
    return jax.ShapeDtypeStruct(tuple(shape), dtype)


def _mod(x, sh8, sc8):
    r, d = x.shape
    return (x.reshape(r // 8, 8, d) * (1.0 + sc8[None]) + sh8[None]).reshape(r, d)


def _rowscale(x, g8):
    r, d = x.shape
    return (x.reshape(r // 8, 8, d) * g8[None]).reshape(r, d)


def _sum8(x):
    r, w = x.shape
    return jnp.sum(x.reshape(r // 8, 8, w), axis=0)


def _ln_stats(r):
    mu = jnp.mean(r, axis=-1, keepdims=True)
    xc = r - mu
    var = jnp.mean(xc * xc, axis=-1, keepdims=True)
    rstd = lax.rsqrt(var + LN_EPS)
    return xc * rstd, rstd


def _ln_bwd(dxh, xhat, rstd):
    m1 = jnp.mean(dxh, axis=-1, keepdims=True)
    m2 = jnp.mean(dxh * xhat, axis=-1, keepdims=True)
    return rstd * (dxh - m1 - xhat * m2)


def _sigmoid(x):
    return 1.0 / (1.0 + jnp.exp(-x))


def _gelu(y):
    return 0.5 * y * (1.0 + lax.erf(y * (1.0 / math.sqrt(2.0))))


def _gelu_grad(y):
    return 0.5 * (1.0 + lax.erf(y * (1.0 / math.sqrt(2.0)))) + y * jnp.exp(-0.5 * y * y) * (1.0 / math.sqrt(2.0 * math.pi))


def _dot(a, b):
    return jnp.dot(a, b, preferred_element_type=F32)


def _dot_nt(a, b):
    return lax.dot_general(a, b, (((1,), (1,)), ((), ())), preferred_element_type=F32)


def _dot_tn(a, b):
    return lax.dot_general(a, b, (((0,), (0,)), ((), ())), preferred_element_type=F32)


class _Cfg:
    def __init__(self, x, ctx, mlp_w1, cv_w_dw):
        self.B, self.L, self.D = x.shape
        self.Lc = ctx.shape[1]
        assert self.B * 4 == SUBLANES, "two examples per device, four chunks each"
        self.F = mlp_w1.shape[2] * 4
        self.KW = cv_w_dw.shape[1]
        self.half = self.KW // 2
        self.G = self.D // S5_GROUP
        self.P = S5_STATE
        self.NS = self.G * self.P
        self.NO = self.D // OCT_CH
        assert self.NO % 2 == 0
        self.nx = self.L // 4
        self.nc = self.Lc // 4
        self.Tc = self.B * self.Lc
        self.Tx = self.B * self.L
        self.T = self.Tc + self.Tx
        self.TM = 512 if self.Tc % 512 == 0 else self.Tc
        assert self.Tx % self.TM == 0 and self.TM % 16 == 0
        self.HB = self.TM // 2
        assert SUBLANES * self.half <= self.HB
        self.TW = 512 if (self.Tc % 512 == 0 and self.Tx % 512 == 0) else self.TM

    def ti(self, n):
        t = 32 if self.nc % 32 == 0 else self.nc
        assert n % t == 0 and self.Tc % (8 * t) == 0
        return t

    def rows(self, ctx_too):
        return (0, self.T) if ctx_too else (self.Tc, self.Tx)


def _allgather8(name, x_shard):
    m_per, n = x_shard.shape
    assert m_per % 8 == 0

    def body(x_ref, out_ref, send_sems, recv_sems, local_sem):
        x, y, c = lax.axis_index("x"), lax.axis_index("y"), lax.axis_index("c")
        me, sibling = (x, y, c), (x, y, 1 - c)
        chips = [(1 - x, y), (x, 1 - y), (1 - x, 1 - y)]

        def rows(px, py, pc):
            return out_ref.at[pl.ds((4 * px + 2 * py + pc) * m_per, m_per), :]

        def copy(k, block, to, src=None):
            return pltpu.make_async_remote_copy(
                src_ref=rows(*block) if src is None else src, dst_ref=rows(*block),
                send_sem=send_sems.at[k], recv_sem=recv_sems.at[k], device_id=to, device_id_type=MESH)

        mine = pltpu.make_async_copy(x_ref, rows(*me), local_sem)
        mine.start()
        first = [copy(0, me, sibling, src=x_ref)]
        first += [copy(1 + j, me, (*chip, c), src=x_ref) for j, chip in enumerate(chips)]
        for cp in first:
            cp.start()
        passed = [copy(4 + j, (*chip, c), sibling) for j, chip in enumerate(chips)]
        for j, chip in enumerate(chips):
            copy(1 + j, (*chip, c), me).wait_recv()
            passed[j].start()
        copy(0, sibling, me).wait_recv()
        for j, chip in enumerate(chips):
            copy(4 + j, (*chip, 1 - c), me).wait_recv()
        for cp in first + passed:
            cp.wait_send()
        mine.wait()

    return pl.pallas_call(
        body, name=name, out_shape=_sds((8 * m_per, n), x_shard.dtype),
        in_specs=[pl.BlockSpec(memory_space=pltpu.VMEM)], out_specs=pl.BlockSpec(memory_space=pltpu.VMEM),
        scratch_shapes=[pltpu.SemaphoreType.DMA((7,)), pltpu.SemaphoreType.DMA((7,)), pltpu.SemaphoreType.DMA],
        compiler_params=pltpu.CompilerParams(vmem_limit_bytes=VMEM_LIMIT),
    )(x_shard)


def _flip(v, m):
    return v + m - 2 * v * m


def _peer(axis):
    x, y, c = lax.axis_index("x"), lax.axis_index("y"), lax.axis_index("c")
    if axis == "c":
        return (x, y, 1 - c)
    if axis == "xy":
        return (_flip(x, 1 - c), _flip(y, c), c)
    if axis == "yx":
        return (_flip(x, c), _flip(y, 1 - c), c)
    raise ValueError(axis)


def _pair_exchange(name, axis, inputs, out_shapes, aliases, plan):
    n_in = len(inputs)
    n_out = len(out_shapes)

    def body(*refs):
        ins, outs = refs[:n_in], refs[n_in:n_in + n_out]
        send_sems, recv_sems, local_sems = refs[n_in + n_out:]
        x, y, c = lax.axis_index("x"), lax.axis_index("y"), lax.axis_index("c")
        remote, local = plan(x, y, c, ins, outs)
        lcs = [pltpu.make_async_copy(s, d, local_sems.at[k]) for k, (s, d) in enumerate(local)]
        for cp in lcs:
            cp.start()
        rcs = [pltpu.make_async_remote_copy(src_ref=s, dst_ref=d, send_sem=send_sems.at[k], recv_sem=recv_sems.at[k],
                                            device_id=_peer(axis), device_id_type=MESH) for k, (s, d) in enumerate(remote)]
        for cp in rcs:
            cp.start()
        for cp in rcs:
            cp.wait()
        for cp in lcs:
            cp.wait()

    n_remote, n_local = plan.counts
    return pl.pallas_call(
        body, name=name, out_shape=tuple(out_shapes),
        in_specs=[ANY] * n_in, out_specs=tuple([ANY] * n_out),
        input_output_aliases=dict(aliases),
        scratch_shapes=[pltpu.SemaphoreType.DMA((n_remote,)), pltpu.SemaphoreType.DMA((n_remote,)),
                        pltpu.SemaphoreType.DMA((max(n_local, 1),))],
    )(*inputs)


def _plan(n_remote, n_local=0):
    def deco(fn):
        fn.counts = (n_remote, n_local)
        return fn
    return deco


HBM = pl.BlockSpec(memory_space=pltpu.HBM)
SEM = pl.BlockSpec(memory_space=pltpu.SEMAPHORE)


def _split_start(name, axis, bufs, plan):
    nb = len(bufs)
    n = plan.counts[0]

    def body(*refs):
        ins, send_sem, recv_sem, token = refs[:nb], refs[nb], refs[nb + 1], refs[-1]
        x, y, c = lax.axis_index("x"), lax.axis_index("y"), lax.axis_index("c")
        for k, (s, d) in enumerate(plan(x, y, c, ins)):
            pltpu.make_async_remote_copy(src_ref=s, dst_ref=d, send_sem=send_sem.at[k], recv_sem=recv_sem.at[k],
                                         device_id=_peer(axis), device_id_type=MESH).start()
        token[...] = jnp.zeros_like(token)

    outs = pl.pallas_call(
        body, name=name,
        out_shape=(pltpu.SemaphoreType.DMA((n,)), pltpu.SemaphoreType.DMA((n,)),
                   *[pltpu.HBM(b.shape, b.dtype) for b in bufs], _sds((8, LANES))),
        in_specs=[HBM] * nb, out_specs=(SEM, SEM, *([HBM] * nb), pl.BlockSpec(memory_space=pltpu.VMEM)),
        input_output_aliases={i: 2 + i for i in range(nb)},
        compiler_params=pltpu.CompilerParams(has_side_effects=pltpu.SideEffectType.DATAFLOW_SIDE_EFFECTING),
    )(*[pltpu.with_memory_space_constraint(b, pltpu.HBM) for b in bufs])
    return dict(name=name, axis=axis, plan=plan, send=outs[0], recv=outs[1], bufs=list(outs[2:2 + nb]), token=outs[-1])


def _split_wait(h, after):
    bufs, plan, axis = h["bufs"], h["plan"], h["axis"]
    nb = len(bufs)

    def body(*refs):
        ins, send_sem, recv_sem = refs[:nb], refs[nb], refs[nb + 1]
        x, y, c = lax.axis_index("x"), lax.axis_index("y"), lax.axis_index("c")
        for k, (s, d) in enumerate(plan(x, y, c, ins)):
            cp = pltpu.make_async_remote_copy(src_ref=s, dst_ref=d, send_sem=send_sem.at[k], recv_sem=recv_sem.at[k],
                                              device_id=_peer(axis), device_id_type=MESH)
            cp.wait_send()
            cp.wait_recv()

    outs = pl.pallas_call(
        body, name=h["name"] + "_wait", out_shape=tuple(pltpu.HBM(b.shape, b.dtype) for b in bufs),
        in_specs=[HBM] * nb + [SEM, SEM, ANY], out_specs=tuple([HBM] * nb),
        input_output_aliases={i: i for i in range(nb)},
        compiler_params=pltpu.CompilerParams(has_side_effects=pltpu.SideEffectType.DATAFLOW_SIDE_EFFECTING),
    )(*bufs, h["send"], h["recv"], after)
    return list(outs)


def _tie(small, tokens):
    for t in tokens:
        small = small + t[0, 0]
    return small


class _Overlap:
    def __init__(self):
        self.live = {}
        self.done = {}

    def add(self, key, gen):
        self.live[key] = gen
        return [next(gen)]

    def point(self, arr):
        tokens = []
        for key in list(self.live):
            try:
                tokens.append(self.live[key].send(arr))
            except StopIteration as e:
                self.done[key] = e.value
                del self.live[key]
        return tokens

    def finish(self, key, arr):
        while key in self.live:
            try:
                self.live[key].send(arr)
            except StopIteration as e:
                self.done[key] = e.value
                del self.live[key]
        return self.done.pop(key)


def _gather_gen(tag, fams):
    nf = len(fams)
    shapes = [f.shape for f in fams]
    views = [f.reshape(4, 2, -1, f.shape[-1]) for f in fams]

    @_plan(nf)
    def plan1(x, y, c, refs):
        s = 2 * x + y
        return [(refs[k].at[s, c], refs[k].at[s, c]) for k in range(nf)]

    @_plan(2 * nf)
    def plan2(x, y, c, refs):
        shards = [2 * x + y, 2 * _flip(x, 1 - c) + _flip(y, c)]
        return [(refs[k].at[s, c], refs[k].at[s, c]) for k in range(nf) for s in shards]

    @_plan(3 * nf)
    def plan3(x, y, c, refs):
        shards = [2 * (1 - x) + y, 2 * x + (1 - y), 2 * (1 - x) + (1 - y)]
        return [(refs[k].at[s, c], refs[k].at[s, c]) for k in range(nf) for s in shards]

    for rnd, (axis, plan) in enumerate((("xy", plan1), ("yx", plan2), ("c", plan3))):
        h = _split_start(f"{tag}_g{rnd}", axis, views, plan)
        after = yield h["token"]
        views = _split_wait(h, after)
    return [v.reshape(sh) for v, sh in zip(views, shapes)]


def _reduce_scatter_gen(tag, grads):
    ng = len(grads)
    flat = [g.reshape(4, 2, -1, g.shape[-1]) for g in grads]

    def empty(shape, dtype):
        return lax.empty(tuple(shape), dtype)

    @_plan(ng)
    def plan1(x, y, c, refs):
        return [(refs[k].at[:, 1 - c], refs[ng + k]) for k in range(ng)]

    h = _split_start(tag + "_r0", "c", flat + [empty((4,) + f.shape[2:], F32) for f in flat], plan1)
    after = yield h["token"]
    bufs = _split_wait(h, after)
    p1 = [_sel_add(f"{tag}_add1_{k}", bufs[k], lambda j, sc: (j, sc[2]), bufs[ng + k], True) for k in range(ng)]

    def sent1(kk, x, y, c):
        return ((1 - c) * kk + c * (1 - x), (1 - c) * (1 - y) + c * kk)

    def kept1(j, sc):
        x, y, c = sc[0], sc[1], sc[2]
        return ((1 - c) * j + c * x, (1 - c) * y + c * j)

    @_plan(2 * ng)
    def plan2(x, y, c, refs):
        return [(refs[k].at[sent1(kk, x, y, c)], refs[ng + k].at[kk]) for k in range(ng) for kk in range(2)]

    v1 = [pb.reshape(2, 2, pb.shape[1], pb.shape[2]) for p, pb in p1]
    h = _split_start(tag + "_r1", "yx", v1 + [empty((2,) + v.shape[2:], BF16) for v in v1], plan2)
    after = yield h["token"]
    bufs = _split_wait(h, after)
    p2 = [_sel_add(f"{tag}_add2_{k}", p1[k][0].reshape(2, 2, p1[k][0].shape[1], p1[k][0].shape[2]), kept1, bufs[ng + k], True)
          for k in range(ng)]

    @_plan(ng)
    def plan3(x, y, c, refs):
        return [(refs[k].at[(1 - c) * (1 - x) + c * (1 - y)], refs[ng + k]) for k in range(ng)]

    h = _split_start(tag + "_r2", "xy", [qb for q, qb in p2] + [empty(qb.shape[1:], BF16) for q, qb in p2], plan3)
    after = yield h["token"]
    bufs = _split_wait(h, after)
    fin = [_sel_add(f"{tag}_add3_{k}", p2[k][0], lambda j, sc: ((1 - sc[2]) * sc[0] + sc[2] * sc[1],), bufs[ng + k][None],
                    False, out_slots=(2, lambda j, sc: sc[2]))[0] for k in range(ng)]

    @_plan(ng)
    def plan4(x, y, c, refs):
        return [(refs[k].at[c], refs[k].at[c]) for k in range(ng)]

    h = _split_start(tag + "_r3", "c", fin, plan4)
    after = yield h["token"]
    full = _split_wait(h, after)
    return [full[k].reshape(grads[k].shape[1:]) for k in range(ng)]


def _xyc():
    return jnp.stack([lax.axis_index("x"), lax.axis_index("y"), lax.axis_index("c")]).astype(jnp.int32)


def _place_shard(name, w, fam, slot0, n_slots):
    n, kk, nn = w.shape
    kt = 256 if kk % 256 == 0 else kk

    def body(scal, w_ref, *rest):
        rest[-1][...] = w_ref[...].astype(BF16)

    in_specs = [pl.BlockSpec((None, kt, nn), lambda t, i, sc: (t, i, 0))]
    args = [_xyc(), w]
    aliases = {}
    if fam is not None:
        in_specs.append(ANY)
        args.append(fam)
        aliases = {2: 0}
    gs = pltpu.PrefetchScalarGridSpec(
        num_scalar_prefetch=1, grid=(n, kk // kt), in_specs=in_specs,
        out_specs=pl.BlockSpec((None, None, kt, nn), lambda t, i, sc: (2 * sc[0] + sc[1], slot0 + t, i, 0)))
    return pl.pallas_call(body, name=name, grid_spec=gs, out_shape=_sds((4, n_slots, kk, nn), BF16),
                          input_output_aliases=aliases, compiler_params=_cp(2))(*args)


def _sel_add(name, a, a_sel, r, emit_bf16, out_slots=None):
    nr, rows, w = r.shape
    tr = 256 if rows % 256 == 0 else rows

    def body(scal, a_ref, r_ref, *outs):
        s = a_ref[...] + r_ref[...].astype(F32)
        outs[0][...] = s
        if emit_bf16:
            outs[1][...] = s.astype(BF16)

    lead = a.ndim - 2
    a_block = (None,) * lead + (tr, w)
    n_out, o_fn = out_slots if out_slots is not None else (nr, lambda j, sc: j)
    out_shape = [_sds((n_out, rows, w), F32)] + ([_sds((nr, rows, w), BF16)] if emit_bf16 else [])
    out_specs = [pl.BlockSpec((None, tr, w), lambda j, t, sc: (o_fn(j, sc), t, 0))]
    if emit_bf16:
        out_specs.append(pl.BlockSpec((None, tr, w), lambda j, t, sc: (j, t, 0)))
    gs = pltpu.PrefetchScalarGridSpec(
        num_scalar_prefetch=1, grid=(nr, rows // tr),
        in_specs=[pl.BlockSpec(a_block, lambda j, t, sc: tuple(a_sel(j, sc)) + (t, 0)),
                  pl.BlockSpec((None, tr, w), lambda j, t, sc: (j, t, 0))],
        out_specs=out_specs)
    return pl.pallas_call(body, name=name, grid_spec=gs, out_shape=out_shape, compiler_params=_cp(2))(_xyc(), a, r)


def _allreduce8(tag, buf, point=None):
    rows, w = buf.shape
    assert rows % 16 == 0
    step = (lambda a: _tie(a, point(a))) if point is not None else (lambda a: a)
    one = lambda: _plan(1)(lambda x, y, c, ins, outs_: ([(ins[0], outs_[0])], []))
    (got,) = _pair_exchange(f"{tag}_ar_c", "c", [buf], [_sds(buf.shape, F32)], {}, one())
    cur = _ew(f"{tag}_aradd_c", lambda a, b: a + b, [buf, got], [_sds(buf.shape, F32)])[0]
    cur = step(cur).reshape(2, rows // 2, w)
    mine = _plan(1)(lambda x, y, c, ins, outs_: ([(ins[0].at[c], outs_[0])], []))
    (got,) = _pair_exchange(f"{tag}_ar_1", "xy", [cur], [_sds(cur.shape[1:], F32)], {}, mine)
    (h1,) = _sel_add(f"{tag}_aradd_1", cur, lambda j, sc: (sc[2],), got[None], False)
    h1 = step(h1)
    (got,) = _pair_exchange(f"{tag}_ar_2", "yx", [h1[0]], [_sds(h1.shape[1:], F32)], {}, one())
    (h2,) = _sel_add(f"{tag}_aradd_2", h1, lambda j, sc: (0,), got[None], False, out_slots=(2, lambda j, sc: sc[2]))
    swap = _plan(1)(lambda x, y, c, ins, outs_: ([(ins[0].at[c], outs_[0].at[c])], []))
    (full,) = _pair_exchange(f"{tag}_ar_c2", "c", [h2], [_sds(h2.shape, F32)], {0: 0}, swap)
    return full.reshape(rows, w)


def _ew(name, fn, ins, outs):
    rows, w = ins[0].shape
    tr = rows
    for cand in (512, 256, 128, 64, 32, 16, 8):
        if rows % cand == 0 and rows > cand and cand * w * 4 <= (1 << 20):
            tr = cand
            break
    n_in = len(ins)

    def body(*refs):
        vals = fn(*[r[...] for r in refs[:n_in]])
        if not isinstance(vals, (tuple, list)):
            vals = (vals,)
        for o, v in zip(refs[n_in:], vals):
            o[...] = v.astype(o.dtype)

    spec = pl.BlockSpec((tr, w), lambda i: (i, 0))
    return pl.pallas_call(body, name=name, grid=(rows // tr,), in_specs=[spec] * n_in,
                          out_specs=[spec] * len(outs), out_shape=list(outs), compiler_params=_cp(1))(*ins)


def _ew3(name, fn, ins, n_out):
    aa, bb, cc = ins[0].shape
    pad_bytes = (-(-bb // SUBLANES) * SUBLANES) * (-(-cc // LANES) * LANES) * 4
    ta = 1
    for cand in range(aa, 0, -1):
        if aa % cand == 0 and cand * pad_bytes <= (1 << 20):
            ta = cand
            break
    n_in = len(ins)

    def body(*refs):
        vals = fn(*[r[...] for r in refs[:n_in]])
        for o, v in zip(refs[n_in:], vals):
            o[...] = v

    spec = pl.BlockSpec((ta, bb, cc), lambda i: (i, 0, 0))
    return pl.pallas_call(body, name=name, grid=(aa // ta,), in_specs=[spec] * n_in, out_specs=[spec] * n_out,
                          out_shape=[_sds((aa, bb, cc))] * n_out, compiler_params=_cp(1))(*ins)


def _view_for_ew(a):
    if a.ndim == 1:
        return a.reshape(1, -1)
    if a.ndim == 2:
        return a
    if a.shape[-1] % LANES == 0 and a.shape[-2] % SUBLANES == 0:
        return a.reshape(-1, a.shape[-1])
    return a.reshape(-1, a.shape[-2], a.shape[-1])


def _adamw(name, w, g, m, v):
    def fn(w, g, m, v):
        m = ADAM_B1 * m + (1.0 - ADAM_B1) * g
        v = ADAM_B2 * v + (1.0 - ADAM_B2) * (g * g)
        m_hat = m / (1.0 - ADAM_B1 ** ADAM_STEP)
        v_hat = v / (1.0 - ADAM_B2 ** ADAM_STEP)
        delta = -ADAM_LR * (m_hat / (jnp.sqrt(v_hat) + ADAM_EPS) + ADAM_WD * w)
        return delta, m, v

    shp = w.shape
    a = [_view_for_ew(t) for t in (w, g, m, v)]
    if a[0].ndim == 3:
        o = _ew3(name, fn, a, 3)
    else:
        o = _ew(name, fn, a, [_sds(a[0].shape)] * 3)
    return tuple(t.reshape(shp) for t in o)


def _to_perm(a):
    b, ls, d = a.shape
    n = ls // 4
    return a.reshape(b * 4, n, d).swapaxes(0, 1).reshape(n * 8, d)


def _from_perm(p, b, ls):
    n = ls // 4
    return p.reshape(n, b * 4, p.shape[-1]).swapaxes(0, 1).reshape(b, ls, p.shape[-1])


def _pos_embed(rows, dim):
    def sincos(pos, d):
        quarter = d // 2
        omega = POS_TEMP ** (-jnp.arange(quarter, dtype=F32) / quarter)
        ang = pos[:, None] * omega[None, :]
        return jnp.concatenate([jnp.sin(ang), jnp.cos(ang)], axis=-1)

    row_idx = jnp.repeat(jnp.arange(rows), GRID_W).astype(F32)
    col_idx = jnp.tile(jnp.arange(GRID_W), rows).astype(F32)
    return jnp.concatenate([sincos(row_idx, dim // 2), sincos(col_idx, dim // 2)], axis=-1)


def _stream_of(cfg, off_tiles, tile_rows):
    nct = cfg.Tc // tile_rows
    return lambda i: jnp.where(i + off_tiles >= nct, 1, 0)


def _ada_fwd(c_all, w_ada, b_shard):
    nl, d, w = w_ada.shape
    tn = 512 if w % 512 == 0 else w

    def body(c_ref, w_ref, b_ref, o_ref):
        cv = c_ref[...]
        cond = (cv * _sigmoid(cv)).astype(BF16)
        o_ref[...] = _dot(cond, w_ref[...].astype(BF16)) + b_ref[...]

    return pl.pallas_call(
        body, name="ada_fwd", grid=(nl, w // tn),
        in_specs=[_full(c_all.shape), pl.BlockSpec((None, d, tn), lambda l, j: (l, 0, j)),
                  pl.BlockSpec((None, 1, tn), lambda l, j: (l, 0, j))],
        out_specs=pl.BlockSpec((None, c_all.shape[0], tn), lambda l, j: (l, 0, j)),
        out_shape=_sds((nl, c_all.shape[0], w)), compiler_params=_cp(2))(c_all, w_ada, b_shard)


def _ada_bwd(c_all, dmod_shard, w_ada):
    nl, d, w = w_ada.shape
    tn = 512 if w % 512 == 0 else w
    nr = c_all.shape[0]

    def body(c_ref, dm_ref, w_ref, gw_ref, dc_ref):
        j = pl.program_id(0) * (w // tn) + pl.program_id(1)
        cv = c_ref[...]
        cond = (cv * _sigmoid(cv)).astype(BF16)
        dm = dm_ref[...].astype(BF16)
        gw_ref[...] = _dot_tn(cond, dm)
        part = _dot_nt(dm[16:24], w_ref[...].astype(BF16))

        @pl.when(j == 0)
        def _():
            dc_ref[...] = part

        @pl.when(j > 0)
        def _():
            dc_ref[...] += part

    return pl.pallas_call(
        body, name="ada_bwd", grid=(nl, w // tn),
        in_specs=[_full(c_all.shape), pl.BlockSpec((None, nr, tn), lambda l, j: (l, 0, j)),
                  pl.BlockSpec((None, d, tn), lambda l, j: (l, 0, j))],
        out_specs=[pl.BlockSpec((None, d, tn), lambda l, j: (l, 0, j)), _full((8, d))],
        out_shape=[_sds((nl, d, w)), _sds((8, d))], compiler_params=_cp(2))(c_all, dmod_shard, w_ada)


def _disc(lr, li, ldt, br, bi):
    lr = jnp.minimum(lr, LAMBDA_RE_MAX)
    dt = jnp.exp(ldt)
    mag = jnp.exp(lr * dt)
    abr = mag * jnp.cos(li * dt)
    abi = mag * jnp.sin(li * dt)
    den = lr * lr + li * li
    nr = abr - 1.0
    ni = abi
    cr = (nr * lr + ni * li) / den
    ci = (ni * lr - nr * li) / den
    return abr, abi, cr[None] * br - ci[None] * bi, cr[None] * bi + ci[None] * br


def _disc_fwd(lr, li, ldt, br, bi):
    def body(a, b, c, d, e, o1, o2, o3, o4):
        r = _disc(a[...], b[...], c[...], d[...], e[...])
        o1[...], o2[...], o3[...], o4[...] = r

    return pl.pallas_call(body, name="s5_disc_fwd", out_shape=[_sds(lr.shape), _sds(lr.shape), _sds(br.shape), _sds(br.shape)])(
        lr, li, ldt, br, bi)


def _disc_bwd(lr, li, ldt, br, bi, g_abr, g_abi, g_bbr, g_bbi):
    def body(a, b, c, d, e, g1, g2, g3, g4, o1, o2, o3, o4, o5):
        _, vjp = jax.vjp(_disc, a[...], b[...], c[...], d[...], e[...])
        r = vjp((g1[...], g2[...], g3[...], g4[...]))
        o1[...], o2[...], o3[...], o4[...], o5[...] = r

    return pl.pallas_call(
        body, name="s5_disc_bwd",
        out_shape=[_sds(lr.shape), _sds(li.shape), _sds(ldt.shape), _sds(br.shape), _sds(bi.shape)])(
        lr, li, ldt, br, bi, g_abr, g_abi, g_bbr, g_bbi)


def _s5_layouts(cfg, lam_re, lam_im, log_dt, b_re, b_im):
    P, G = cfg.P, cfg.G
    nd = lam_re.shape[0]
    lr = lam_re.transpose(2, 0, 1).reshape(P, nd * G)
    li = lam_im.transpose(2, 0, 1).reshape(P, nd * G)
    ldt = log_dt.reshape(1, nd * G)
    br = b_re.transpose(3, 2, 0, 1).reshape(S5_GROUP, P, nd * G)
    bi = b_im.transpose(3, 2, 0, 1).reshape(S5_GROUP, P, nd * G)
    return lr, li, ldt, br, bi


def _coef_rows(cfg, abr, abi, conj):
    nd = abr.shape[1] // cfg.G

    def one(t):
        return t.reshape(cfg.P, nd, cfg.G).transpose(1, 2, 0).reshape(nd, cfg.NS)
    a = jnp.stack([one(abr), -one(abi) if conj else one(abi)], axis=1)
    return jnp.broadcast_to(a[:, :, None, :], (nd, 2, SUBLANES, cfg.NS))


def _blockdiag_b(cfg, bbr, bbi):
    eye = jnp.eye(8, dtype=F32)
    nd = bbr.shape[2] // cfg.G

    def one(t):
        t = t.reshape(S5_GROUP, cfg.P, nd, cfg.G).transpose(2, 3, 0, 1)
        t = t.reshape(nd, cfg.NO, 8, S5_GROUP, cfg.P)
        return jnp.einsum("dogcp,gh->dogchp", t, eye).reshape(nd, cfg.NO, OCT_CH, OCT_ST)

    return jnp.concatenate([one(bbr), one(bbi)], axis=-1).astype(BF16)


def _blockdiag_c(cfg, c_re, c_im):
    eye = jnp.eye(8, dtype=F32)
    nd = c_re.shape[0]

    def one(t):
        t = t.transpose(0, 1, 3, 2).reshape(nd, cfg.NO, 8, cfg.P, S5_GROUP)
        return jnp.einsum("dogpc,gh->dogphc", t, eye).reshape(nd, cfg.NO, OCT_ST, OCT_CH)

    return jnp.concatenate([one(c_re), -one(c_im)], axis=2).astype(BF16)


def _diag_b(cfg, dbf):
    eye = jnp.eye(8, dtype=F32)
    nd = dbf.shape[0]

    def one(t):
        t = t.reshape(nd, cfg.NO, 8, S5_GROUP, 8, cfg.P)
        t = jnp.einsum("dogchp,gh->dogcp", t, eye).reshape(nd, cfg.G, S5_GROUP, cfg.P)
        return t.transpose(2, 3, 0, 1).reshape(S5_GROUP, cfg.P, nd * cfg.G)

    return one(dbf[..., :OCT_ST]), one(dbf[..., OCT_ST:])


def _diag_c(cfg, dcft):
    eye = jnp.eye(8, dtype=F32)
    nd = dcft.shape[0]

    def one(t):
        t = t.reshape(nd, cfg.NO, 8, S5_GROUP, 8, cfg.P)
        return jnp.einsum("dohcgp,gh->dogcp", t, eye).reshape(nd, cfg.G, S5_GROUP, cfg.P)

    return one(dcft[..., :OCT_ST]), -one(dcft[..., OCT_ST:])


def _recur(buf, st, a_ref, n_oct, ti, rev, store, flat=False):
    for o in range(0, n_oct, 2):
        cols = [(pl.ds(oo * 2 * OCT_ST, OCT_ST), pl.ds(oo * 2 * OCT_ST + OCT_ST, OCT_ST)) for oo in (o, o + 1)]
        scol = [pl.ds(oo * OCT_ST, OCT_ST) for oo in (o, o + 1)]
        coef = [(a_ref[0, :, sc], a_ref[1, :, sc]) for sc in scol]
        init = (st[0, :, scol[0]], st[1, :, scol[0]], st[0, :, scol[1]], st[1, :, scol[1]])

        def step(i4, carry, cols=cols, coef=coef):
            carry = list(carry)
            for q in range(unroll):
                i = i4 * unroll + q
                r = pl.multiple_of((i + rev * (ti - 1 - 2 * i)) * 8, 8)
                for s in range(2):
                    sr, si = carry[2 * s], carry[2 * s + 1]
                    ar, ai = coef[s]
                    zr = buf[pl.ds(r, 8), cols[s][0]]
                    zi = buf[pl.ds(r, 8), cols[s][1]]
                    nr = ar * sr - ai * si + zr
                    ni = ar * si + ai * sr + zi
                    if store:
                        buf[pl.ds(r, 8), cols[s][0]] = nr
                        buf[pl.ds(r, 8), cols[s][1]] = ni
                    carry[2 * s], carry[2 * s + 1] = nr, ni
            return tuple(carry)

        unroll = 4 if ti % 4 == 0 else 1
        if flat:
            fin = init
            for i4 in range(ti // unroll):
                fin = step(i4, fin)
        else:
            fin = lax.fori_loop(0, ti // unroll, step, init)
        st[0, :, scol[0]] = fin[0]
        st[1, :, scol[0]] = fin[1]
        st[0, :, scol[1]] = fin[2]
        st[1, :, scol[1]] = fin[3]


def _s5_fwd_pass(cfg, name, tok, mod8, col_sh, col_sc, bf, acoef, r0, n, s_init=None, cf=None, y_prev=None):
    D, NO, NS = cfg.D, cfg.NO, cfg.NS
    ti = cfg.ti(n)
    nb = n // ti
    R = 8 * ti
    ob = r0 // R
    second = s_init is not None
    blk = lambda d, j: ob + j + d * (nb - 1 - 2 * j)

    def body(*refs):
        if second:
            tok_ref, mod_ref, bf_ref, a_ref, si_ref, cf_ref, yp_ref, y_ref, ck_ref, fin_ref, zbuf, st = refs
        else:
            tok_ref, mod_ref, bf_ref, a_ref, fin_ref, zbuf, st = refs
        d = pl.program_id(0)
        j = pl.program_id(1)

        @pl.when(j == 0)
        def _():
            if second:
                st[...] = si_ref[...]
            else:
                st[...] = jnp.zeros_like(st)

        if second:
            ck_ref[...] = st[...]
        u = _mod(tok_ref[...], mod_ref[:, col_sh:col_sh + D], mod_ref[:, col_sc:col_sc + D]).astype(BF16)
        for o in range(NO):
            zbuf[:, o * 1024:(o + 1) * 1024] = _dot(u[:, o * OCT_CH:(o + 1) * OCT_CH], bf_ref[o])
        _recur(zbuf, st, a_ref, NO, ti, d, second, flat=True)
        if second:
            for o in range(NO):
                y_ref[:, o * OCT_CH:(o + 1) * OCT_CH] = _dot(zbuf[:, o * 1024:(o + 1) * 1024].astype(BF16), cf_ref[o])

        @pl.when(j == nb - 1)
        def _():
            fin_ref[...] = st[...]

    st_spec = pl.BlockSpec((None, 2, 8, NS), lambda d, j: (d, 0, 0, 0))
    in_specs = [pl.BlockSpec((R, D), lambda d, j: (blk(d, j), 0)), _full(mod8.shape),
                pl.BlockSpec((None, NO, OCT_CH, 1024), lambda d, j: (d, 0, 0, 0)), st_spec]
    args = [tok, mod8, bf, acoef]
    scratch = [pltpu.VMEM((R, NO * 1024), F32), pltpu.VMEM((2, 8, NS), F32)]
    if not second:
        return pl.pallas_call(body, name=name, grid=(2, nb), in_specs=in_specs, out_specs=st_spec,
                              out_shape=_sds((2, 2, 8, NS)), scratch_shapes=scratch, compiler_params=_cp(2))(*args)
    in_specs += [st_spec, pl.BlockSpec((None, NO, 1024, OCT_CH), lambda d, j: (d, 0, 0, 0))]
    args += [s_init, cf]
    aliases = {}
    if y_prev is not None:
        in_specs.append(ANY)
        args.append(y_prev)
        aliases = {6: 0}
    else:
        in_specs.append(_full((8, LANES)))
        args.append(jnp.zeros((8, LANES), F32))
    out_specs = [pl.BlockSpec((None, R, D), lambda d, j: (d, blk(d, j), 0)),
                 pl.BlockSpec((None, None, 2, 8, NS), lambda d, j: (d, j + d * (nb - 1 - 2 * j), 0, 0, 0)), st_spec]
    out_shape = [_sds((2, cfg.T, D)), _sds((2, nb, 2, 8, NS)), _sds((2, 2, 8, NS))]
    return pl.pallas_call(body, name=name, grid=(2, nb), in_specs=in_specs, out_specs=out_specs, out_shape=out_shape,
                          input_output_aliases=aliases, scratch_shapes=scratch, compiler_params=_cp(2))(*args)


def _s5_chain(cfg, name, fin_local, acoef, n, inc, prev_fin=None):
    NS = cfg.NS
    nsq = int(round(math.log2(n)))
    assert 2 ** nsq == n

    def body(*refs):
        if prev_fin is not None:
            f_ref, a_ref, p_ref, o_ref = refs
        else:
            f_ref, a_ref, o_ref = refs
        for d in range(2):
            pr, pi = a_ref[d, 0, 0:1, :], a_ref[d, 1, 0:1, :]
            for _ in range(nsq):
                pr, pi = pr * pr - pi * pi, 2.0 * pr * pi
            for b in range(2):
                order = [4 * b + k for k in range(4)]
                if not inc[d]:
                    order = order[::-1]
                if prev_fin is not None:
                    last = order[-1]
                    sr, si = p_ref[d, 0, last:last + 1, :], p_ref[d, 1, last:last + 1, :]
                else:
                    sr = jnp.zeros((1, NS), F32)
                    si = jnp.zeros((1, NS), F32)
                for k in order:
                    o_ref[d, 0, k:k + 1, :] = sr
                    o_ref[d, 1, k:k + 1, :] = si
                    fr, fi = f_ref[d, 0, k:k + 1, :], f_ref[d, 1, k:k + 1, :]
                    sr, si = pr * sr - pi * si + fr, pr * si + pi * sr + fi

    args = [fin_local, acoef] + ([prev_fin] if prev_fin is not None else [])
    return pl.pallas_call(body, name=name, out_shape=_sds((2, 2, 8, NS)))(*args)


def _s5_forward(cfg, tag, tok, mod8, col_sh, col_sc, bf, cf, acoef, point=None):
    saved = {}
    fin_prev = None
    y = None
    for ph, (r0, n) in (("c", (0, cfg.nc)), ("x", (cfg.Tc, cfg.nx))):
        m8 = mod8[0 if ph == "c" else 1]
        loc = _s5_fwd_pass(cfg, f"{tag}_scan1{ph}", tok, m8, col_sh, col_sc, bf, acoef, r0, n)
        if point is not None and ph == "x":
            m8 = _tie(m8, point(loc))
        s_in = _s5_chain(cfg, f"{tag}_chain{ph}", loc, acoef, n, (True, False), fin_prev)
        y, ck, fin_prev = _s5_fwd_pass(cfg, f"{tag}_scan2{ph}", tok, m8, col_sh, col_sc, bf, acoef, r0, n, s_in, cf, y)
        saved[ph] = ck
    return y, saved


def _s5_bwd_pass(cfg, name, dy, tok, mod8, col_sh, col_sc, bf, cf, acoef, acoef_adj, r0, n, g_init=None, ck=None,
                 du_prev=None):
    D, NO, NS = cfg.D, cfg.NO, cfg.NS
    ti = cfg.ti(n)
    nb = n // ti
    R = 8 * ti
    ob = r0 // R
    second = g_init is not None
    has_dy = dy is not None
    blk = lambda d, j: ob + j + (1 - d) * (nb - 1 - 2 * j)

    def body(*refs):
        refs = list(refs)
        dy_ref = refs.pop(0) if has_dy else None
        if second:
            (tok_ref, mod_ref, bf_ref, cf_ref, a_ref, aa_ref, gi_ref, ck_ref, dup_ref,
             du_ref, da_ref, dbf_ref, dcf_ref, gfin_ref, qbuf, zbuf, gst, hst) = refs
        else:
            cf_ref, aa_ref, gfin_ref, qbuf, gst = refs
        d = pl.program_id(0)
        j = pl.program_id(1)

        @pl.when(j == 0)
        def _():
            if second:
                gst[...] = gi_ref[...]
                da_ref[...] = jnp.zeros_like(da_ref)
                dbf_ref[...] = jnp.zeros_like(dbf_ref)
                dcf_ref[...] = jnp.zeros_like(dcf_ref)
            else:
                gst[...] = jnp.zeros_like(gst)

        if has_dy:
            dyb = dy_ref[...].astype(BF16)
            for o in range(NO):
                qbuf[:, o * 1024:(o + 1) * 1024] = _dot_nt(dyb[:, o * OCT_CH:(o + 1) * OCT_CH], cf_ref[o])
        else:
            qbuf[...] = jnp.zeros_like(qbuf)
        _recur(qbuf, gst, aa_ref, NO, ti, 1 - d, second, flat=True)

        if second:
            u = _mod(tok_ref[...], mod_ref[:, col_sh:col_sh + D], mod_ref[:, col_sc:col_sc + D]).astype(BF16)
            for o in range(NO):
                zbuf[:, o * 1024:(o + 1) * 1024] = _dot(u[:, o * OCT_CH:(o + 1) * OCT_CH], bf_ref[o])
            hst[...] = ck_ref[...]
            _recur(zbuf, hst, a_ref, NO, ti, d, True, flat=True)

            g_off, h_off = (1 - d) * 8, d * 8
            edge = pl.multiple_of(d * (R - 8), 8)
            for o in range(0, NO, 2):
                cols = [(pl.ds(oo * 1024, OCT_ST), pl.ds(oo * 1024 + OCT_ST, OCT_ST)) for oo in (o, o + 1)]
                scol = [pl.ds(oo * OCT_ST, OCT_ST) for oo in (o, o + 1)]
                init = []
                for s in range(2):
                    er, ei = qbuf[pl.ds(edge, 8), cols[s][0]], qbuf[pl.ds(edge, 8), cols[s][1]]
                    kr, ki = ck_ref[0, :, scol[s]], ck_ref[1, :, scol[s]]
                    init += [er * kr + ei * ki, ei * kr - er * ki]

                def stp(i, carry, cols=cols):
                    rg = pl.multiple_of(i * 8 + g_off, 8)
                    rh = pl.multiple_of(i * 8 + h_off, 8)
                    out = []
                    for s in range(2):
                        gr, gi = qbuf[pl.ds(rg, 8), cols[s][0]], qbuf[pl.ds(rg, 8), cols[s][1]]
                        hr, hi = zbuf[pl.ds(rh, 8), cols[s][0]], zbuf[pl.ds(rh, 8), cols[s][1]]
                        out += [carry[2 * s] + (gr * hr + gi * hi), carry[2 * s + 1] + (gi * hr - gr * hi)]
                    return tuple(out)

                fin = tuple(init)
                for i in range(ti - 1):
                    fin = stp(i, fin)
                for s in range(2):
                    da_ref[0, :, scol[s]] += fin[2 * s]
                    da_ref[1, :, scol[s]] += fin[2 * s + 1]

            for o in range(NO):
                gb = qbuf[:, o * 1024:(o + 1) * 1024].astype(BF16)
                uo = u[:, o * OCT_CH:(o + 1) * OCT_CH]
                dbf_ref[o] += _dot_tn(uo, gb)
                if has_dy:
                    dcf_ref[o] += _dot_tn(dyb[:, o * OCT_CH:(o + 1) * OCT_CH], zbuf[:, o * 1024:(o + 1) * 1024].astype(BF16))
                du_ref[:, o * OCT_CH:(o + 1) * OCT_CH] = _dot_nt(gb, bf_ref[o])

        @pl.when(j == nb - 1)
        def _():
            gfin_ref[...] = gst[...]

    st_spec = pl.BlockSpec((None, 2, 8, NS), lambda d, j: (d, 0, 0, 0))
    row_spec = pl.BlockSpec((R, D), lambda d, j: (blk(d, j), 0))
    bf_spec = pl.BlockSpec((None, NO, OCT_CH, 1024), lambda d, j: (d, 0, 0, 0))
    cf_spec = pl.BlockSpec((None, NO, 1024, OCT_CH), lambda d, j: (d, 0, 0, 0))
    in_specs, args = [], []
    if has_dy:
        in_specs.append(row_spec)
        args.append(dy)
    if not second:
        in_specs += [cf_spec, st_spec]
        args += [cf, acoef_adj]
        return pl.pallas_call(body, name=name, grid=(2, nb), in_specs=in_specs, out_specs=st_spec,
                              out_shape=_sds((2, 2, 8, NS)),
                              scratch_shapes=[pltpu.VMEM((R, NO * 1024), F32), pltpu.VMEM((2, 8, NS), F32)],
                              compiler_params=_cp(2))(*args)
    ck_spec = pl.BlockSpec((None, None, 2, 8, NS), lambda d, j: (d, j + (1 - d) * (nb - 1 - 2 * j), 0, 0, 0))
    in_specs += [row_spec, _full(mod8.shape), bf_spec, cf_spec, st_spec, st_spec, st_spec, ck_spec]
    args += [tok, mod8, bf, cf, acoef, acoef_adj, g_init, ck]
    n_before = len(args)
    aliases = {}
    if du_prev is not None:
        in_specs.append(ANY)
        args.append(du_prev)
        aliases = {n_before: 0}
    else:
        in_specs.append(_full((8, LANES)))
        args.append(jnp.zeros((8, LANES), F32))
    out_specs = [pl.BlockSpec((None, R, D), lambda d, j: (d, blk(d, j), 0)), st_spec, bf_spec, bf_spec, st_spec]
    out_shape = [_sds((2, cfg.T, D)), _sds((2, 2, 8, NS)), _sds((2, NO, OCT_CH, 1024)), _sds((2, NO, OCT_CH, 1024)),
                 _sds((2, 2, 8, NS))]
    scratch = [pltpu.VMEM((R, NO * 1024), F32), pltpu.VMEM((R, NO * 1024), F32), pltpu.VMEM((2, 8, NS), F32),
               pltpu.VMEM((2, 8, NS), F32)]
    return pl.pallas_call(body, name=name, grid=(2, nb), in_specs=in_specs, out_specs=out_specs, out_shape=out_shape,
                          input_output_aliases=aliases, scratch_shapes=scratch, compiler_params=_cp(2))(*args)


def _s5_backward(cfg, tag, dy, dy_ctx, tok, mod8, col_sh, col_sc, bf, cf, acoef, acoef_adj, saved):
    g_prev = None
    acc = None
    du = None
    for ph, (r0, n) in (("x", (cfg.Tc, cfg.nx)), ("c", (0, cfg.nc))):
        m8 = mod8[0 if ph == "c" else 1]
        dyp = dy if (ph == "x" or dy_ctx) else None
        loc = _s5_bwd_pass(cfg, f"{tag}_adjA{ph}", dyp, tok, m8, col_sh, col_sc, bf, cf, acoef, acoef_adj, r0, n)
        g_in = _s5_chain(cfg, f"{tag}_adjchain{ph}", loc, acoef_adj, n, (False, True), g_prev)
        du, da, dbf, dcf, g_prev = _s5_bwd_pass(cfg, f"{tag}_adjB{ph}", dyp, tok, m8, col_sh, col_sc, bf, cf, acoef,
                                                acoef_adj, r0, n, g_in, saved[ph], du)
        new = (da, dbf, dcf)
        if acc is None:
            acc = new
        else:
            acc = tuple(_ew(f"{tag}_accsum{q}", lambda a, b: a + b, [a.reshape(-1, a.shape[-1]), b.reshape(-1, b.shape[-1])],
                            [_sds((a.size // a.shape[-1], a.shape[-1]))])[0].reshape(a.shape)
                        for q, (a, b) in enumerate(zip(acc, new)))
    return du, acc


def _tok_specs(cfg, rows, width, tile=None):
    tm = tile or cfg.TM
    off = rows[0] // tm
    return pl.BlockSpec((tm, width), lambda i: (i + off, 0)), rows[1] // tm, off


def _mod_spec(cfg, mod8, off):
    st = _stream_of(cfg, off, cfg.TM)
    return pl.BlockSpec((None, 8, mod8.shape[-1]), lambda i: (st(i), 0, 0))


def _wspec(w):
    fam, slot = w
    _, _, kk, nn = fam.shape
    return pl.BlockSpec((4, None, kk, nn), lambda *i: (0, slot, 0, 0), pipeline_mode=pl.Buffered(1))


def _glu_ln(cfg, name, rows, tok, y, mod8, cols, dskip, w, b, gain, bias):
    D, TM = cfg.D, cfg.TM
    csh, csc, cg = cols
    spec, nt, off = _tok_specs(cfg, rows, D)
    spec2, _, _ = _tok_specs(cfg, rows, 2 * D)

    def body(tok_ref, y_ref, mod_ref, ds_ref, w_ref, b_ref, g_ref, bi_ref, x1_ref, r1_ref, mix_ref, zz_ref, zb_ref, yy_ref):
        t = tok_ref[...]
        u = _mod(t, mod_ref[:, csh:csh + D], mod_ref[:, csc:csc + D])
        yy = ds_ref[...] * u + y_ref[0] + y_ref[1]
        zb = _gelu(yy).astype(BF16)
        zz = jnp.concatenate([_dot(zb, w_ref[s]) for s in range(4)], axis=-1) + b_ref[...]
        mix = zz[:, :D] * _sigmoid(zz[:, D:])
        r1 = DN_ALPHA * t + _rowscale(mix, mod_ref[:, cg:cg + D])
        xhat, _ = _ln_stats(r1)
        x1_ref[...] = xhat * g_ref[...] + bi_ref[...]
        r1_ref[...] = r1
        mix_ref[...] = mix.astype(BF16)
        zz_ref[...] = zz.astype(BF16)
        zb_ref[...] = zb
        yy_ref[...] = yy.astype(BF16)

    T = cfg.T
    return pl.pallas_call(
        body, name=name, grid=(nt,),
        in_specs=[spec, pl.BlockSpec((2, TM, D), lambda i: (0, i + off, 0)), _mod_spec(cfg, mod8, off), _full((1, D)),
                  _wspec(w), _full((1, 2 * D)), _full((1, D)), _full((1, D))],
        out_specs=[spec, spec, spec, spec2, spec, spec],
        out_shape=[_sds((T, D)), _sds((T, D)), _sds((T, D), BF16), _sds((T, 2 * D), BF16), _sds((T, D), BF16),
                   _sds((T, D), BF16)],
        compiler_params=_cp(1))(tok, y, mod8, dskip, w[0], b, gain, bias)


def _mlp_ln(cfg, name, rows, x1, mod8, cols, w1, w2, gain, bias):
    D, TM = cfg.D, cfg.TM
    csh, csc, cg = cols
    spec, nt, off = _tok_specs(cfg, rows, D)
    specf, _, _ = _tok_specs(cfg, rows, cfg.F)
    fb = cfg.F // 4

    def body(x_ref, mod_ref, w1_ref, w2_ref, g_ref, bi_ref, x2_ref, r2_ref, out_ref, a_ref, h_ref):
        t = x_ref[...]
        h = _mod(t, mod_ref[:, csh:csh + D], mod_ref[:, csc:csc + D]).astype(BF16)
        out = jnp.zeros((TM, D), F32)
        for s in range(4):
            hid = jnp.maximum(_dot(h, w1_ref[s]), 0.0)
            a = (hid * hid).astype(BF16)
            a_ref[:, s * fb:(s + 1) * fb] = a
            out = out + _dot(a, w2_ref[s])
        r2 = DN_ALPHA * t + _rowscale(out, mod_ref[:, cg:cg + D])
        xhat, _ = _ln_stats(r2)
        x2_ref[...] = xhat * g_ref[...] + bi_ref[...]
        r2_ref[...] = r2
        out_ref[...] = out.astype(BF16)
        h_ref[...] = h

    T = cfg.T
    return pl.pallas_call(
        body, name=name, grid=(nt,),
        in_specs=[spec, _mod_spec(cfg, mod8, off), _wspec(w1), _wspec(w2), _full((1, D)), _full((1, D))],
        out_specs=[spec, spec, spec, specf, spec],
        out_shape=[_sds((T, D)), _sds((T, D)), _sds((T, D), BF16), _sds((T, cfg.F), BF16), _sds((T, D), BF16)],
        compiler_params=_cp(1))(x1, mod8, w1[0], w2[0], gain, bias)


def _pw1_glu(cfg, name, rows, tok, mod8, cols, w, b):
    D, TM = cfg.D, cfg.TM
    csh, csc = cols
    spec, nt, off = _tok_specs(cfg, rows, D)
    spec2, _, _ = _tok_specs(cfg, rows, 2 * D)

    def body(tok_ref, mod_ref, w_ref, b_ref, aa_ref, ag_ref, h_ref):
        h = _mod(tok_ref[...], mod_ref[:, csh:csh + D], mod_ref[:, csc:csc + D]).astype(BF16)
        aa = jnp.concatenate([_dot(h, w_ref[s]) for s in range(4)], axis=-1) + b_ref[...]
        aa_ref[...] = aa.astype(BF16)
        ag_ref[...] = aa[:, :D] * _sigmoid(aa[:, D:])
        h_ref[...] = h

    T = cfg.T
    return pl.pallas_call(
        body, name=name, grid=(nt,),
        in_specs=[spec, _mod_spec(cfg, mod8, off), _wspec(w), _full((1, 2 * D))],
        out_specs=[spec2, spec, spec],
        out_shape=[_sds((T, 2 * D), BF16), _sds((T, D)), _sds((T, D), BF16)], compiler_params=_cp(1))(tok, mod8, w[0], b)


def _halo_maps(cfg, rows):
    TM, HB = cfg.TM, cfg.HB
    off = rows[0] // TM
    nct = cfg.Tc // TM
    ntx = cfg.Tx // TM

    def phase(i):
        t = i + off
        is_x = t >= nct
        first = jnp.where(is_x, nct, 0)
        cnt = jnp.where(is_x, ntx, nct)
        return t, first, cnt

    def prev(i):
        t, first, cnt = phase(i)
        return jnp.where(t == first, 2 * (first + cnt) - 1, 2 * t - 1), 0

    def nxt(i):
        t, first, cnt = phase(i)
        return jnp.where(t == first + cnt - 1, 2 * first, 2 * t + 2), 0

    def edge(i):
        t, first, cnt = phase(i)
        return t == first, t == first + cnt - 1

    return prev, nxt, edge, off


def _halo_fix(prev, nxt, is_first, is_last):
    hb, d = prev.shape
    k = lax.broadcasted_iota(jnp.int32, (hb // 8, 8, d), 1)
    p3 = prev.reshape(hb // 8, 8, d)
    n3 = nxt.reshape(hb // 8, 8, d)
    p_roll = jnp.where((k % 4) == 0, 0.0, pltpu.roll(p3, 1, 1))
    n_roll = jnp.where((k % 4) == 3, 0.0, pltpu.roll(n3, 7, 1))
    p3 = jnp.where(is_first, p_roll, p3)
    n3 = jnp.where(is_last, n_roll, n3)
    return p3.reshape(hb, d), n3.reshape(hb, d)


def _dwconv_ln(cfg, name, rows, ag, w_dw, b_dw, ln_g, ln_b):
    D, TM, HB, KW, half = cfg.D, cfg.TM, cfg.HB, cfg.KW, cfg.half
    prev_map, next_map, edge, off = _halo_maps(cfg, rows)
    spec, nt, _ = _tok_specs(cfg, rows, D)

    def body(cur_ref, prev_ref, next_ref, w_ref, b_ref, g_ref, bi_ref, cv_ref, s_ref, ext):
        i = pl.program_id(0)
        is_first, is_last = edge(i)

        @pl.when(i >= 0)
        def _():
            p, n = _halo_fix(prev_ref[...], next_ref[...], is_first, is_last)
            ext[0:HB, :] = p
            ext[HB:HB + TM, :] = cur_ref[...]
            ext[HB + TM:, :] = n

        acc = jnp.zeros((TM, D), F32)
        for k in range(KW):
            lo = HB + 8 * (k - half)
            acc = acc + w_ref[k:k + 1, :] * ext[lo:lo + TM, :]
        cv_ref[...] = acc + b_ref[...]
        xhat, _ = _ln_stats(cv_ref[...])
        nn = xhat * g_ref[...] + bi_ref[...]
        s_ref[...] = (nn * _sigmoid(nn)).astype(BF16)

    T = cfg.T
    return pl.pallas_call(
        body, name=name, grid=(nt,),
        in_specs=[spec, pl.BlockSpec((HB, D), prev_map), pl.BlockSpec((HB, D), next_map), _full((KW, D)),
                  _full((1, D)), _full((1, D)), _full((1, D))],
        out_specs=[spec, spec], out_shape=[_sds((T, D)), _sds((T, D), BF16)],
        scratch_shapes=[pltpu.VMEM((TM + 2 * HB, D), F32)], compiler_params=_cp(1))(ag, ag, ag, w_dw, b_dw, ln_g, ln_b)


def _pw2_ln(cfg, name, rows, s, tok, mod8, cg, w, b, gain, bias):
    D, TM = cfg.D, cfg.TM
    spec, nt, off = _tok_specs(cfg, rows, D)
    kb = D // 4

    def body(s_ref, tok_ref, mod_ref, w_ref, b_ref, g_ref, bi_ref, x1_ref, r1_ref, mix_ref):
        sv = s_ref[...]
        mix = b_ref[...] + jnp.zeros((TM, D), F32)
        for q in range(4):
            mix = mix + _dot(sv[:, q * kb:(q + 1) * kb], w_ref[q])
        r1 = DN_ALPHA * tok_ref[...] + _rowscale(mix, mod_ref[:, cg:cg + D])
        xhat, _ = _ln_stats(r1)
        x1_ref[...] = xhat * g_ref[...] + bi_ref[...]
        r1_ref[...] = r1
        mix_ref[...] = mix.astype(BF16)

    T = cfg.T
    return pl.pallas_call(
        body, name=name, grid=(nt,),
        in_specs=[spec, spec, _mod_spec(cfg, mod8, off), _wspec(w), _full((1, D)), _full((1, D)), _full((1, D))],
        out_specs=[spec, spec, spec], out_shape=[_sds((T, D)), _sds((T, D)), _sds((T, D), BF16)],
        compiler_params=_cp(1))(s, tok, mod8, w[0], b, gain, bias)


def _loss(cfg, xf, tgt):
    D, TM = cfg.D, cfg.TM
    spec, nt, off = _tok_specs(cfg, cfg.rows(False), D)

    def body(x_ref, t_ref, l_ref, dx_ref, acc):
        i = pl.program_id(0)
        dlt = x_ref[...] - t_ref[...]

        @pl.when(i == 0)
        def _():
            acc[...] = jnp.zeros_like(acc)

        acc[...] += _sum8(dlt * dlt)
        dx_ref[...] = dlt * (1.0 / D)

        @pl.when(i == nt - 1)
        def _():
            l_ref[...] = jnp.zeros((8, LANES), F32) + jnp.sum(acc[...]) * (0.5 / D)

    return pl.pallas_call(
        body, name="loss", grid=(nt,),
        in_specs=[spec, pl.BlockSpec((TM, D), lambda i: (i, 0))],
        out_specs=[_full((8, LANES)), spec], out_shape=[_sds((8, LANES)), _sds((cfg.T, D))],
        scratch_shapes=[pltpu.VMEM((8, D), F32)], compiler_params=_cp(1))(xf, tgt)


def _masked_spec(cfg, rows, width, valid_from_tile):
    tm = cfg.TM
    off = rows[0] // tm
    return pl.BlockSpec((tm, width), lambda i: (jnp.maximum(i + off, valid_from_tile), 0))


def _lnb(cfg, name, rows, dres, dres_ctx_ok, dh, r, aux, gain, mod_gate, cg, mod_next, csc, dbr_dtype=F32):
    D, TM = cfg.D, cfg.TM
    spec, nt, off = _tok_specs(cfg, rows, D)
    nct = cfg.Tc // TM
    has_dres, has_dh = dres is not None, dh is not None

    def body(*refs):
        refs = list(refs)
        dres_ref = refs.pop(0) if has_dres else None
        dh_ref = refs.pop(0) if has_dh else None
        r_ref, aux_ref, g_ref, mg_ref = refs[:4]
        refs = refs[4:]
        mn_ref = refs.pop(0) if has_dh else None
        dprev_ref, dbr_ref, dgain_ref, dbias_ref, dg_ref, dsc_ref, dsh_ref, acc_g, acc_b = refs
        i = pl.program_id(0)
        t = i + off
        first_of_stream = (i == 0) | (t == nct)
        xhat, rstd = _ln_stats(r_ref[...])
        dy = jnp.zeros((TM, D), F32)
        if has_dres:
            dv = dres_ref[...]
            if not dres_ctx_ok:
                dv = jnp.where(t >= nct, dv, 0.0)
            dy = dy + dv
        if has_dh:
            dhv = dh_ref[...]
            dy = dy + _rowscale(dhv, 1.0 + mn_ref[:, csc:csc + D])
            x_out = xhat * g_ref[0:1, :] + g_ref[1:2, :]
            s_sc, s_sh = _sum8(dhv * x_out), _sum8(dhv)
        else:
            s_sc = s_sh = jnp.zeros((8, D), F32)
        dr = _ln_bwd(dy * g_ref[0:1, :], xhat, rstd)
        s_g = _sum8(dr * aux_ref[...].astype(F32))

        @pl.when(i == 0)
        def _():
            acc_g[...] = jnp.zeros_like(acc_g)
            acc_b[...] = jnp.zeros_like(acc_b)

        acc_g[...] += _sum8(dy * xhat)
        acc_b[...] += _sum8(dy)

        @pl.when(first_of_stream)
        def _():
            dg_ref[...] = s_g
            dsc_ref[...] = s_sc
            dsh_ref[...] = s_sh

        @pl.when(jnp.logical_not(first_of_stream))
        def _():
            dg_ref[...] += s_g
            dsc_ref[...] += s_sc
            dsh_ref[...] += s_sh

        dprev_ref[...] = DN_ALPHA * dr
        dbr_ref[...] = _rowscale(dr, mg_ref[:, cg:cg + D]).astype(dbr_dtype)

        @pl.when(i == nt - 1)
        def _():
            dgain_ref[...] = jnp.sum(acc_g[...], axis=0, keepdims=True)
            dbias_ref[...] = jnp.sum(acc_b[...], axis=0, keepdims=True)

    st = _stream_of(cfg, off, TM)
    in_specs, args = [], []
    if has_dres:
        in_specs.append(spec if dres_ctx_ok else _masked_spec(cfg, rows, D, nct))
        args.append(dres)
    if has_dh:
        in_specs.append(spec)
        args.append(dh)
    in_specs += [spec, spec, _full((2, D)), _mod_spec(cfg, mod_gate, off)]
    args += [r, aux, gain, mod_gate]
    if has_dh:
        in_specs.append(_mod_spec(cfg, mod_next, off))
        args.append(mod_next)
    acc_spec = pl.BlockSpec((None, 8, D), lambda i: (st(i), 0, 0))
    T = cfg.T
    return pl.pallas_call(
        body, name=name, grid=(nt,), in_specs=in_specs,
        out_specs=[spec, spec, _full((1, D)), _full((1, D)), acc_spec, acc_spec, acc_spec],
        out_shape=[_sds((T, D)), _sds((T, D), dbr_dtype), _sds((1, D)), _sds((1, D)), _sds((2, 8, D)), _sds((2, 8, D)),
                   _sds((2, 8, D))],
        scratch_shapes=[pltpu.VMEM((8, D), F32), pltpu.VMEM((8, D), F32)], compiler_params=_cp(1))(*args)


def _mlp_bwd(cfg, name, rows, dout, a, w1, w2):
    D, TM = cfg.D, cfg.TM
    spec, nt, off = _tok_specs(cfg, rows, D)
    specf, _, _ = _tok_specs(cfg, rows, cfg.F)
    fb = cfg.F // 4

    def body(d_ref, a_ref, w1_ref, w2_ref, dh_ref, dhid_ref):
        dout = d_ref[...]
        dh = jnp.zeros((TM, D), F32)
        for s in range(4):
            da = _dot_nt(dout, w2_ref[s])
            dhid = (da * (2.0 * jnp.sqrt(a_ref[:, s * fb:(s + 1) * fb].astype(F32)))).astype(BF16)
            dhid_ref[:, s * fb:(s + 1) * fb] = dhid
            dh = dh + _dot_nt(dhid, w1_ref[s])
        dh_ref[...] = dh

    T = cfg.T
    return pl.pallas_call(
        body, name=name, grid=(nt,),
        in_specs=[spec, specf, _wspec(w1), _wspec(w2)],
        out_specs=[spec, specf],
        out_shape=[_sds((T, D)), _sds((T, cfg.F), BF16)], compiler_params=_cp(1))(dout, a, w1[0], w2[0])


def _wgrad(cfg, name, rows, a, b, mode, fam, slot):
    tw = cfg.TW
    off = rows[0] // tw
    ntile = rows[1] // tw
    tps = next(q for q in (4, 3, 2, 1) if ntile % q == 0)
    nt = ntile // tps
    fresh = not hasattr(fam, "dtype")
    fam_shape = tuple(fam) if fresh else fam.shape
    _, n, kk, nn = fam_shape

    def body(*refs):
        a_refs, b_refs, o_ref = refs[:tps], refs[tps:2 * tps], refs[-1]
        t = pl.program_id(1)
        part = _dot_tn(a_refs[0][...], b_refs[0][...])
        for q in range(1, tps):
            part = part + _dot_tn(a_refs[q][...], b_refs[q][...])

        @pl.when(t == 0)
        def _():
            o_ref[...] = part

        @pl.when(t > 0)
        def _():
            o_ref[...] += part

    def row(q):
        return lambda s, t: t * tps + q + off

    if mode == "col":
        a_specs = [pl.BlockSpec((tw, kk), lambda s, t, r=row(q): (r(s, t), 0)) for q in range(tps)]
        b_specs = [pl.BlockSpec((tw, nn), lambda s, t, r=row(q): (r(s, t), s)) for q in range(tps)]
    else:
        a_specs = [pl.BlockSpec((tw, kk), lambda s, t, r=row(q): (r(s, t), s)) for q in range(tps)]
        b_specs = [pl.BlockSpec((tw, nn), lambda s, t, r=row(q): (r(s, t), 0)) for q in range(tps)]
    out_spec = pl.BlockSpec((None, None, kk, nn), lambda s, t: (s, slot, 0, 0))
    ins = [a] * tps + [b] * tps
    if fresh:
        return pl.pallas_call(body, name=name, grid=(4, nt), in_specs=a_specs + b_specs, out_specs=out_spec,
                              out_shape=_sds(fam_shape), compiler_params=_cp(2))(*ins)
    return pl.pallas_call(body, name=name, grid=(4, nt), in_specs=a_specs + b_specs + [ANY], out_specs=out_spec,
                          out_shape=_sds(fam_shape), input_output_aliases={2 * tps: 0}, compiler_params=_cp(2))(*ins, fam)


def _glu_bwd(cfg, name, rows, dmix, pre, w, yy=None):
    D, TM = cfg.D, cfg.TM
    spec, nt, off = _tok_specs(cfg, rows, D)
    spec2, _, _ = _tok_specs(cfg, rows, 2 * D)
    hw = w[0].shape[-1]
    has_y = yy is not None

    def body(*refs):
        refs = list(refs)
        d_ref, p_ref, w_ref = refs[:3]
        y_ref = refs[3] if has_y else None
        dz_ref, dp_ref, db_ref, acc = refs[-4:]
        i = pl.program_id(0)
        dm = d_ref[...]
        po, pg = p_ref[:, :D].astype(F32), p_ref[:, D:].astype(F32)
        sg = _sigmoid(pg)
        dpre = jnp.concatenate([dm * sg, dm * po * sg * (1.0 - sg)], axis=-1)

        @pl.when(i == 0)
        def _():
            acc[...] = jnp.zeros_like(acc)

        acc[...] += _sum8(dpre)
        dpb = dpre.astype(BF16)
        dz = jnp.zeros((TM, D), F32)
        for s in range(4):
            dz = dz + _dot_nt(dpb[:, s * hw:(s + 1) * hw], w_ref[s])
        if has_y:
            dz = dz * _gelu_grad(y_ref[...].astype(F32))
        dz_ref[...] = dz
        dp_ref[...] = dpb

        @pl.when(i == nt - 1)
        def _():
            db_ref[...] = jnp.sum(acc[...], axis=0, keepdims=True)

    T = cfg.T
    in_specs = [spec, spec2, _wspec(w)] + ([spec] if has_y else [])
    args = [dmix, pre, w[0]] + ([yy] if has_y else [])
    return pl.pallas_call(
        body, name=name, grid=(nt,), in_specs=in_specs, out_specs=[spec, spec2, _full((1, 2 * D))],
        out_shape=[_sds((T, D)), _sds((T, 2 * D), BF16), _sds((1, 2 * D))],
        scratch_shapes=[pltpu.VMEM((8, 2 * D), F32)], compiler_params=_cp(1))(*args)


def _s5_du(cfg, name, rows, du, dy, dy_from_tile, tok, mod8, cols, dskip):
    D, TM = cfg.D, cfg.TM
    csh, csc = cols
    spec, nt, off = _tok_specs(cfg, rows, D)

    def body(du_ref, dy_ref, tok_ref, mod_ref, ds_ref, dh_ref, dd_ref, acc):
        i = pl.program_id(0)
        dyv = jnp.where(i + off >= dy_from_tile, dy_ref[...], 0.0)
        u = _mod(tok_ref[...], mod_ref[:, csh:csh + D], mod_ref[:, csc:csc + D])
        dh_ref[...] = du_ref[0] + du_ref[1] + ds_ref[...] * dyv

        @pl.when(i == 0)
        def _():
            acc[...] = jnp.zeros_like(acc)

        acc[...] += _sum8(dyv * u)

        @pl.when(i == nt - 1)
        def _():
            dd_ref[...] = jnp.sum(acc[...], axis=0, keepdims=True)

    T = cfg.T
    return pl.pallas_call(
        body, name=name, grid=(nt,),
        in_specs=[pl.BlockSpec((2, TM, D), lambda i: (0, i + off, 0)), _masked_spec(cfg, rows, D, dy_from_tile), spec,
                  _mod_spec(cfg, mod8, off), _full((1, D))],
        out_specs=[spec, _full((1, D))], out_shape=[_sds((T, D)), _sds((1, D))],
        scratch_shapes=[pltpu.VMEM((8, D), F32)], compiler_params=_cp(1))(du, dy, tok, mod8, dskip)


def _pw2_bwd(cfg, name, rows, dmix, cv, w, ln_g, ln_b):
    D, TM = cfg.D, cfg.TM
    spec, nt, off = _tok_specs(cfg, rows, D)
    kb = D // 4

    def body(d_ref, cv_ref, w_ref, g_ref, b_ref, dcv_ref, dmb_ref, sums_ref, acc):
        i = pl.program_id(0)
        dm = d_ref[...]
        dmb = dm.astype(BF16)
        ds = jnp.concatenate([_dot_nt(dmb, w_ref[q]) for q in range(4)], axis=-1)
        xhat, rstd = _ln_stats(cv_ref[...])
        nn = xhat * g_ref[...] + b_ref[...]
        sg = _sigmoid(nn)
        dn = ds * (sg * (1.0 + nn * (1.0 - sg)))
        dcv = _ln_bwd(dn * g_ref[...], xhat, rstd)

        @pl.when(i == 0)
        def _():
            acc[...] = jnp.zeros_like(acc)

        acc[0] += _sum8(dn * xhat)
        acc[1] += _sum8(dn)
        acc[2] += _sum8(dcv)
        acc[3] += _sum8(dm)
        dcv_ref[...] = dcv
        dmb_ref[...] = dmb

        @pl.when(i == nt - 1)
        def _():
            for q in range(4):
                sums_ref[q:q + 1, :] = jnp.sum(acc[q], axis=0, keepdims=True)

    T = cfg.T
    return pl.pallas_call(
        body, name=name, grid=(nt,),
        in_specs=[spec, spec, _wspec(w), _full((1, D)), _full((1, D))],
        out_specs=[spec, spec, _full((4, D))], out_shape=[_sds((T, D)), _sds((T, D), BF16), _sds((4, D))],
        scratch_shapes=[pltpu.VMEM((4, 8, D), F32)], compiler_params=_cp(1))(dmix, cv, w[0], ln_g, ln_b)


def _dwconv_bwd(cfg, name, rows, dcv, ag, w_dw):
    D, TM, HB, KW, half = cfg.D, cfg.TM, cfg.HB, cfg.KW, cfg.half
    prev_map, next_map, edge, off = _halo_maps(cfg, rows)
    spec, nt, _ = _tok_specs(cfg, rows, D)

    def body(dc_ref, dp_ref, dn_ref, ac_ref, ap_ref, an_ref, w_ref, dag_ref, dw_ref, extd, exta, acc):
        i = pl.program_id(0)
        is_first, is_last = edge(i)

        @pl.when(i >= 0)
        def _():
            p, n = _halo_fix(dp_ref[...], dn_ref[...], is_first, is_last)
            extd[0:HB, :] = p
            extd[HB:HB + TM, :] = dc_ref[...]
            extd[HB + TM:, :] = n
            p, n = _halo_fix(ap_ref[...], an_ref[...], is_first, is_last)
            exta[0:HB, :] = p
            exta[HB:HB + TM, :] = ac_ref[...]
            exta[HB + TM:, :] = n

        @pl.when(i == 0)
        def _():
            acc[...] = jnp.zeros_like(acc)

        cr = min(CONV_ROWS, TM)
        for r0 in range(0, TM, cr):
            for lc in range(D // LANES):
                ls = pl.ds(lc * LANES, LANES)
                dcur = dc_ref[r0:r0 + cr, ls]
                dag = jnp.zeros((cr, LANES), F32)
                for k in range(KW):
                    lo = r0 + HB + 8 * (half - k)
                    la = r0 + HB + 8 * (k - half)
                    dag = dag + w_ref[k:k + 1, ls] * extd[lo:lo + cr, ls]
                    acc[k, :, ls] += _sum8(dcur * exta[la:la + cr, ls])
                dag_ref[r0:r0 + cr, ls] = dag

        @pl.when(i == nt - 1)
        def _():
            for k in range(KW):
                dw_ref[k:k + 1, :] = jnp.sum(acc[k], axis=0, keepdims=True)

    T = cfg.T
    hp, hn = pl.BlockSpec((HB, D), prev_map), pl.BlockSpec((HB, D), next_map)
    return pl.pallas_call(
        body, name=name, grid=(nt,), in_specs=[spec, hp, hn, spec, hp, hn, _full((KW, D))],
        out_specs=[spec, _full((KW, D))], out_shape=[_sds((T, D)), _sds((KW, D))],
        scratch_shapes=[pltpu.VMEM((TM + 2 * HB, D), F32), pltpu.VMEM((TM + 2 * HB, D), F32), pltpu.VMEM((KW, 8, D), F32)],
        compiler_params=_cp(1))(dcv, dcv, dcv, ag, ag, ag, w_dw)


def _input_bwd(cfg, dres, dh, tok0, mod8, csc):
    D, TM = cfg.D, cfg.TM
    rows = cfg.rows(True)
    spec, nt, off = _tok_specs(cfg, rows, D)
    nct = cfg.Tc // TM
    st = _stream_of(cfg, off, TM)

    def body(dr_ref, dh_ref, t_ref, mod_ref, gx_ref, dsc_ref, dsh_ref):
        i = pl.program_id(0)
        dhv = dh_ref[...]
        gx_ref[...] = dr_ref[...] + _rowscale(dhv, 1.0 + mod_ref[:, csc:csc + D])
        first = (i == 0) | (i == nct)
        s_sc, s_sh = _sum8(dhv * t_ref[...]), _sum8(dhv)

        @pl.when(first)
        def _():
            dsc_ref[...] = s_sc
            dsh_ref[...] = s_sh

        @pl.when(jnp.logical_not(first))
        def _():
            dsc_ref[...] += s_sc
            dsh_ref[...] += s_sh

    acc_spec = pl.BlockSpec((None, 8, D), lambda i: (st(i), 0, 0))
    return pl.pallas_call(
        body, name="input_bwd", grid=(nt,), in_specs=[spec, spec, spec, _mod_spec(cfg, mod8, off)],
        out_specs=[spec, acc_spec, acc_spec], out_shape=[_sds((cfg.T, D)), _sds((2, 8, D)), _sds((2, 8, D))],
        compiler_params=_cp(1))(dres, dh, tok0, mod8)


def _dmod_rows(dmod8):
    nl, _, _, w = dmod8.shape

    def body(d_ref, o_ref):
        xs = d_ref[1]
        cs = d_ref[0]
        o_ref[...] = jnp.zeros((8, w), F32)
        o_ref[0:1, :] = jnp.sum(xs[0:4], axis=0, keepdims=True)
        o_ref[1:2, :] = jnp.sum(xs[4:8], axis=0, keepdims=True)
        o_ref[2:3, :] = jnp.sum(cs, axis=0, keepdims=True)

    return pl.pallas_call(body, name="dmod_rows", grid=(nl,),
                          in_specs=[pl.BlockSpec((None, 2, 8, w), lambda l: (l, 0, 0, 0))],
                          out_specs=pl.BlockSpec((None, 8, w), lambda l: (l, 0, 0)), out_shape=_sds((nl, 8, w)),
                          compiler_params=_cp(1))(dmod8)


def _x_only(acc):
    return jnp.concatenate([jnp.zeros_like(acc[:1]), acc[1:]], axis=0)


def _pack(parts):
    bufs, meta, off = [], [], 0
    for p in parts:
        n = p.size
        rows = -(-n // (8 * LANES)) * 8
        flat = p.reshape(-1).astype(F32)
        if rows * LANES != n:
            flat = jnp.pad(flat, (0, rows * LANES - n))
        flat = flat.reshape(rows, LANES)
        bufs.append(flat)
        meta.append((off, rows, p.shape))
        off += rows
    if off % 16:
        bufs.append(jnp.zeros((8, LANES), F32))
    return jnp.concatenate(bufs, axis=0), meta


def _unpack(buf, meta):
    out = []
    for off, rows, shape in meta:
        n = 1
        for s in shape:
            n *= s
        out.append(buf[off:off + rows].reshape(-1)[:n].reshape(shape))
    return out


def kernel(x, c, ctx, c_ctx, w_ada, b_ada, ln_gain, ln_bias, s5_lam_re, s5_lam_im, s5_log_dt, s5_b_re, s5_b_im, s5_c_re, s5_c_im, s5_d, s5_w_glu, s5_b_glu, cv_w_pw1, cv_b_pw1, cv_w_dw, cv_b_dw, cv_ln_g, cv_ln_b, cv_w_pw2, cv_b_pw2, mlp_w1, mlp_w2, loss_target, m_c_ctx, m_w_ada, m_b_ada, m_ln_gain, m_ln_bias, m_s5_lam_re, m_s5_lam_im, m_s5_log_dt, m_s5_b_re, m_s5_b_im, m_s5_c_re, m_s5_c_im, m_s5_d, m_s5_w_glu, m_s5_b_glu, m_cv_w_pw1, m_cv_b_pw1, m_cv_w_dw, m_cv_b_dw, m_cv_ln_g, m_cv_ln_b, m_cv_w_pw2, m_cv_b_pw2, m_mlp_w1, m_mlp_w2, v_c_ctx, v_w_ada, v_b_ada, v_ln_gain, v_ln_bias, v_s5_lam_re, v_s5_lam_im, v_s5_log_dt, v_s5_b_re, v_s5_b_im, v_s5_c_re, v_s5_c_im, v_s5_d, v_s5_w_glu, v_s5_b_glu, v_cv_w_pw1, v_cv_b_pw1, v_cv_w_dw, v_cv_b_dw, v_cv_ln_g, v_cv_ln_b, v_cv_w_pw2, v_cv_b_pw2, v_mlp_w1, v_mlp_w2):
    cfg = _Cfg(x, ctx, mlp_w1, cv_w_dw)
    D, T, Tc, Tx, B = cfg.D, cfg.T, cfg.Tc, cfg.Tx, cfg.B
    ax, ay, ac = lax.axis_index("x"), lax.axis_index("y"), lax.axis_index("c")
    shard = 2 * ax + ay
    dev = 4 * ax + 2 * ay + ac
    Ds = D // 4
    Wa = w_ada.shape[2]

    c_pad = jnp.concatenate([c, jnp.zeros((8 - B, D), F32)], axis=0)
    small_parts = [ln_gain.reshape(-1, Ds), ln_bias.reshape(-1, Ds), cv_b_pw1.reshape(-1, Ds), cv_w_dw.reshape(-1, Ds),
                   cv_b_dw, cv_ln_g, cv_ln_b, cv_b_pw2]
    small_rows = [p.shape[0] for p in small_parts]
    sm = jnp.concatenate(small_parts, axis=0)
    sm = jnp.pad(sm, ((0, -sm.shape[0] % 8), (0, 0)))
    c_rows = 8 * D // Ds
    both = _allgather8("gather_in", jnp.concatenate([c_pad.reshape(c_rows, Ds), sm], axis=0)).reshape(8, c_rows + sm.shape[0], Ds)
    c_gath = both[:, :c_rows].reshape(8, 8, D)[:, :B].reshape(8 * B, D)
    c_all = jnp.concatenate([c_gath, c_ctx[None], jnp.zeros((7, D), F32)], axis=0)
    b_sh = lax.dynamic_slice_in_dim(b_ada, shard * Wa, Wa, axis=1)[:, None, :]
    mod_sh = _ada_fwd(c_all, w_ada, b_sh)
    mod_g = _allgather8("gather_mod", mod_sh.reshape(DEPTH * 24, Wa)).reshape(4, 2, DEPTH, 24, Wa)[:, 0]
    mods = mod_g.transpose(1, 2, 0, 3).reshape(DEPTH, 24, 4 * Wa)
    mine = lax.dynamic_slice_in_dim(mods, B * dev, B, axis=1)
    mod8 = jnp.stack([jnp.broadcast_to(mods[:, 16:17], (DEPTH, 8, 6 * D)), jnp.repeat(mine, 4, axis=1)], axis=1)
    SH1, SC1, G1, SH2, SC2, G2 = (k * D for k in range(6))

    sm_g = both[:, c_rows:].reshape(4, 2, sm.shape[0], Ds)[:, 0]
    pieces, o = [], 0
    for nr in small_rows:
        pieces.append(sm_g[:, o:o + nr])
        o += nr

    def unshard(p, lead):
        return p.reshape((4,) + lead + (Ds,)).transpose(tuple(range(1, len(lead) + 1)) + (0, len(lead) + 1)).reshape(lead + (4 * Ds,))

    ln_gain_f = unshard(pieces[0], (DEPTH, 2))
    ln_bias_f = unshard(pieces[1], (DEPTH, 2))
    nconv = cv_w_dw.shape[0]
    b_pw1_f = pieces[2].reshape(4, nconv, 2 * D // 4).transpose(1, 0, 2).reshape(nconv, 2 * D)
    w_dw_f = unshard(pieces[3], (nconv, cfg.KW))
    b_dw_f, cvg_f, cvb_f, b_pw2_f = (unshard(p, (nconv,)) for p in pieces[4:8])

    ns5 = s5_w_glu.shape[0]
    assert mlp_w1.shape[1:] == mlp_w2.shape[1:]
    fam_a = _place_shard("place_w1", mlp_w1, None, 0, 2 * DEPTH)
    fam_a = _place_shard("place_w2", mlp_w2, fam_a, DEPTH, 2 * DEPTH)
    fam_b = _place_shard("place_wglu", s5_w_glu, None, 0, ns5 + nconv)
    fam_b = _place_shard("place_wpw1", cv_w_pw1, fam_b, ns5, ns5 + nconv)
    fam_c = _place_shard("place_wpw2", cv_w_pw2, None, 0, nconv)
    ov = _Overlap()
    gather_tokens = ov.add("gather_b", _gather_gen("gatherb", [fam_b])) + ov.add("gather", _gather_gen("gatherw", [fam_a, fam_c]))

    pos = jnp.broadcast_to(_pos_embed(cfg.L // GRID_W, D)[None], (B, cfg.L, D))
    tok_in = jnp.concatenate([_to_perm(ctx), _to_perm(x)], axis=0)
    pos_in = jnp.concatenate([jnp.zeros((Tc, D), F32), _to_perm(pos)], axis=0)
    tok0 = _ew("add_pos", lambda a, b: a + b, [tok_in, pos_in], [_sds((T, D))])[0]
    mod8 = _tie(mod8, gather_tokens)
    tgt = _to_perm(loss_target)

    def lead(t):
        return t.reshape((2 * ns5,) + t.shape[2:])

    s5_lay = _s5_layouts(cfg, lead(s5_lam_re), lead(s5_lam_im), lead(s5_log_dt), lead(s5_b_re), lead(s5_b_im))
    abr, abi, bbr, bbi = _disc_fwd(*s5_lay)
    acoef_all, acoef_adj_all = _coef_rows(cfg, abr, abi, False), _coef_rows(cfg, abr, abi, True)
    bf_all, cf_all = _blockdiag_b(cfg, bbr, bbi), _blockdiag_c(cfg, lead(s5_c_re), lead(s5_c_im))
    s5p = [dict(acoef=acoef_all[2 * j:2 * j + 2], acoef_adj=acoef_adj_all[2 * j:2 * j + 2], bf=bf_all[2 * j:2 * j + 2],
                cf=cf_all[2 * j:2 * j + 2]) for j in range(ns5)]

    kinds = ["s5" if i % 2 == 0 else "conv" for i in range(DEPTH)]
    tok = tok0
    saved = []
    s5_j = cv_j = 0
    for i in range(DEPTH):
        later_s5 = any(k == "s5" for k in kinds[i + 1:])
        rows = cfg.rows(later_s5)
        m8 = mod8[i]
        sv = dict(tok=tok, rows=rows, kind=kinds[i])
        g0, b0 = ln_gain_f[i, 0][None], ln_bias_f[i, 0][None]
        g1, b1 = ln_gain_f[i, 1][None], ln_bias_f[i, 1][None]
        if kinds[i] == "s5":
            j = s5_j
            s5_j += 1
            p = s5p[j]
            y, ck = _s5_forward(cfg, f"l{i}", tok, m8, SH1, SC1, p["bf"], p["cf"], p["acoef"], ov.point if i == 0 else None)
            m8g = m8
            if i == 0:
                m8g = _tie(m8, ov.point(y))
                (wb_full,) = ov.finish("gather_b", y)
            wg = (wb_full, j)
            x1, r1, mix, zz, zb, yy = _glu_ln(cfg, f"l{i}_glu", rows, tok, y, m8g, (SH1, SC1, G1), s5_d[j][None], wg,
                                              s5_b_glu[j][None], g0, b0)
            sv.update(j=j, ck=ck, zz=zz, zb=zb, yy=yy, wg=wg)
            if i == 0:
                wa_full, wc_full = ov.finish("gather", x1)
        else:
            j = cv_j
            cv_j += 1
            w1c, w2c = (wb_full, ns5 + j), (wc_full, j)
            aa, ag, hb = _pw1_glu(cfg, f"l{i}_pw1", rows, tok, m8, (SH1, SC1), w1c, b_pw1_f[j][None])
            cvv, sb = _dwconv_ln(cfg, f"l{i}_dw", rows, ag, w_dw_f[j], b_dw_f[j][None], cvg_f[j][None], cvb_f[j][None])
            x1, r1, mix = _pw2_ln(cfg, f"l{i}_pw2", rows, sb, tok, m8, G1, w2c, b_pw2_f[j][None], g0, b0)
            sv.update(j=j, aa=aa, ag=ag, hb=hb, cvv=cvv, sb=sb, w1c=w1c, w2c=w2c)
        w1m, w2m = (wa_full, i), (wa_full, DEPTH + i)
        x2, r2, mout, am, hm = _mlp_ln(cfg, f"l{i}_mlp", rows, x1, m8, (SH2, SC2, G2), w1m, w2m, g1, b1)
        sv.update(r1=r1, mix=mix, x1=x1, r2=r2, mout=mout, am=am, hm=hm, w1m=w1m, w2m=w2m, g0=g0, b0=b0, g1=g1, b1=b1)
        saved.append(sv)
        tok = x2

    loss8, dxf = _loss(cfg, tok, tgt)
    loss = lax.psum(loss8[0, 0], ("x", "y", "c"))

    dmod8 = [None] * DEPTH
    g_ln_gain = [[None, None] for _ in range(DEPTH)]
    g_ln_bias = [[None, None] for _ in range(DEPTH)]
    g_s5 = [None] * ns5
    g_cv = [None] * nconv
    dres, dh = dxf, None
    pend = []
    for i in reversed(range(DEPTH)):
        sv = saved[i]
        rows = sv["rows"]
        m8 = mod8[i]
        nxt_m8 = mod8[i + 1] if i + 1 < DEPTH else None
        ctx_ok = True if i + 1 >= DEPTH else (saved[i + 1]["rows"][0] == 0)
        if rows[0] != 0:
            ctx_ok = True
        dprev, dbr, dgn, dbs, dg2, dsc_n, dsh_n = _lnb(
            cfg, f"l{i}_lnb2", rows, dres, ctx_ok, dh, sv["r2"], sv["mout"],
            _tie(jnp.concatenate([sv["g1"], sv["b1"]], 0), pend), m8, G2, nxt_m8, SC1, dbr_dtype=BF16)
        if rows[0] != 0:
            dg2, dsc_n, dsh_n = (_x_only(t) for t in (dg2, dsc_n, dsh_n))
        g_ln_gain[i][1], g_ln_bias[i][1] = dgn[0], dbs[0]
        if i + 1 < DEPTH:
            dmod8[i + 1]["sc1"], dmod8[i + 1]["sh1"] = dsc_n, dsh_n
        dmod8[i] = dict(g2=dg2)
        dh2, dhid = _mlp_bwd(cfg, f"l{i}_mlpb", rows, dbr, sv["am"], sv["w1m"], sv["w2m"])
        pend = ov.point(dh2)
        ga = _wgrad(cfg, f"l{i}_gw1", rows, sv["hm"], dhid, "col", (4, 2, D, cfg.F // 4), 0)
        ga = _wgrad(cfg, f"l{i}_gw2", rows, sv["am"], dbr, "row", ga, 1)
        dprev1, dbr1, dgn, dbs, dg1, dsc2, dsh2 = _lnb(
            cfg, f"l{i}_lnb1", rows, dprev, True, dh2, sv["r1"], sv["mix"],
            _tie(jnp.concatenate([sv["g0"], sv["b0"]], 0), pend), m8, G1, m8, SC2)
        if rows[0] != 0:
            dg1, dsc2, dsh2 = (_x_only(t) for t in (dg1, dsc2, dsh2))
        g_ln_gain[i][0], g_ln_bias[i][0] = dgn[0], dbs[0]
        dmod8[i].update(g1=dg1, sc2=dsc2, sh2=dsh2)
        j = sv["j"]
        if sv["kind"] == "s5":
            p = s5p[j]
            dyy, dzz, dbglu = _glu_bwd(cfg, f"l{i}_glub", rows, dbr1, sv["zz"], sv["wg"], sv["yy"])
            pend = ov.point(dyy)
            gb = _wgrad(cfg, f"l{i}_gwg", rows, sv["zb"], dzz, "col", (4, 1, D, D // 2), 0)
            du, (da, dbf, dcf) = _s5_backward(cfg, f"l{i}", dyy, rows[0] == 0, sv["tok"], m8, SH1, SC1, p["bf"], p["cf"],
                                              p["acoef"], _tie(p["acoef_adj"], pend), sv["ck"])
            dh, dds = _s5_du(cfg, f"l{i}_du", cfg.rows(True), du, dyy, rows[0] // cfg.TM, sv["tok"], m8, (SH1, SC1),
                             s5_d[j][None])
            g_s5[j] = dict(da=da, dbf=dbf, dcf=dcf, dd=dds[0], dbglu=dbglu[0])
            layer_grads = [ga, gb]
        else:
            dcv, dmb, sums = _pw2_bwd(cfg, f"l{i}_pw2b", rows, dbr1, sv["cvv"], sv["w2c"], cvg_f[j][None], cvb_f[j][None])
            pend = ov.point(dcv)
            gc = _wgrad(cfg, f"l{i}_gwp2", rows, sv["sb"], dmb, "row", (4, 1, D // 4, D), 0)
            dag, dwdw = _dwconv_bwd(cfg, f"l{i}_dwb", rows, dcv, sv["ag"], _tie(w_dw_f[j], pend))
            dh, daa, dbpw1 = _glu_bwd(cfg, f"l{i}_pw1b", rows, dag, sv["aa"], sv["w1c"])
            gb = _wgrad(cfg, f"l{i}_gwp1", rows, sv["hb"], daa, "col", (4, 1, D, D // 2), 0)
            g_cv[j] = dict(ln_g=sums[0], ln_b=sums[1], b_dw=sums[2], b_pw2=sums[3], w_dw=dwdw, b_pw1=dbpw1[0])
            layer_grads = [ga, gb, gc]
        dres = dprev1
        pend = ov.point(dh) + ov.add(f"rs{i}", _reduce_scatter_gen(f"gw{i}", layer_grads))
    gx_perm, dsc0, dsh0 = _input_bwd(cfg, dres, dh, tok0, _tie(mod8[0], pend), SC1)
    dmod8[0]["sc1"], dmod8[0]["sh1"] = dsc0, dsh0
    grad_x = _from_perm(gx_perm[Tc:], B, cfg.L)

    zero28 = jnp.zeros((2, 8, D), F32)
    dm8 = jnp.stack([jnp.concatenate([dmod8[i].get(k, zero28) for k in ("sh1", "sc1", "g1", "sh2", "sc2", "g2")], axis=-1)
                     for i in range(DEPTH)])
    dm_rows = _dmod_rows(dm8)
    dm_tab = jnp.zeros((DEPTH, 24, 6 * D), F32)
    dm_tab = lax.dynamic_update_slice_in_dim(dm_tab, dm_rows[:, 0:B], B * dev, axis=1)
    dm_tab = lax.dynamic_update_slice_in_dim(dm_tab, dm_rows[:, 2:3], 16, axis=1)

    dbbr, dbbi = _diag_b(cfg, jnp.concatenate([g["dbf"] for g in g_s5], axis=0))
    dcr, dci = _diag_c(cfg, jnp.concatenate([g["dcf"] for g in g_s5], axis=0))
    nd = 2 * ns5
    da_s = _sublane_sum("s5_dasum", jnp.concatenate([g["da"] for g in g_s5], axis=0).reshape(2 * nd, 8, cfg.NS))
    eye_parts = [da_s, dbbr, dbbi, dcr, dci,
                 jnp.stack([g["dd"] for g in g_s5]), jnp.stack([g["dbglu"] for g in g_s5])]
    for j in range(nconv):
        g = g_cv[j]
        eye_parts += [g["ln_g"], g["ln_b"], g["b_dw"], g["b_pw2"], g["w_dw"], g["b_pw1"]]
    eye_parts += [jnp.stack([jnp.stack(r) for r in g_ln_gain]), jnp.stack([jnp.stack(r) for r in g_ln_bias]), dm_tab]
    buf, meta = _pack(eye_parts)
    buf = _tie(buf, ov.point(gx_perm))
    red_buf = _allreduce8("small", buf, ov.point)
    reduced = [ov.finish(f"rs{i}", red_buf) for i in range(DEPTH)]
    red = _unpack(red_buf, meta)

    grads = {}
    da, dbbr, dbbi, dcr, dci, dd, dbglu = red[0:7]
    k = 7
    da_s = da.reshape(nd, 2, cfg.NS)
    g_abr = da_s[:, 0].reshape(nd, cfg.G, cfg.P).transpose(2, 0, 1).reshape(cfg.P, nd * cfg.G)
    g_abi = da_s[:, 1].reshape(nd, cfg.G, cfg.P).transpose(2, 0, 1).reshape(cfg.P, nd * cfg.G)
    glr, gli, gldt, gbr, gbi = _disc_bwd(*s5_lay, g_abr, g_abi, dbbr, dbbi)
    grads.update(s5_lam_re=glr.reshape(cfg.P, nd, cfg.G).transpose(1, 2, 0), s5_lam_im=gli.reshape(cfg.P, nd, cfg.G).transpose(1, 2, 0),
                 s5_log_dt=gldt, s5_b_re=gbr.reshape(S5_GROUP, cfg.P, nd, cfg.G).transpose(2, 3, 1, 0),
                 s5_b_im=gbi.reshape(S5_GROUP, cfg.P, nd, cfg.G).transpose(2, 3, 1, 0), s5_c_re=dcr, s5_c_im=dci,
                 s5_d=dd, s5_b_glu=dbglu)

    def my_cols(full, width):
        return lax.dynamic_slice_in_dim(full, shard * width, width, axis=full.ndim - 1)

    cvs = {n: [] for n in ("ln_g", "ln_b", "b_dw", "b_pw2", "w_dw", "b_pw1")}
    for j in range(nconv):
        for n, val in zip(("ln_g", "ln_b", "b_dw", "b_pw2", "w_dw", "b_pw1"), red[k:k + 6]):
            cvs[n].append(val)
        k += 6
    grads.update(cv_ln_g=my_cols(jnp.stack(cvs["ln_g"]), Ds), cv_ln_b=my_cols(jnp.stack(cvs["ln_b"]), Ds),
                 cv_b_dw=my_cols(jnp.stack(cvs["b_dw"]), Ds), cv_b_pw2=my_cols(jnp.stack(cvs["b_pw2"]), Ds),
                 cv_w_dw=my_cols(jnp.stack(cvs["w_dw"]), Ds), cv_b_pw1=my_cols(jnp.stack(cvs["b_pw1"]), 2 * D // 4))
    grads.update(ln_gain=my_cols(red[k], Ds), ln_bias=my_cols(red[k + 1], Ds))
    dm_all = red[k + 2]

    dm_sh = lax.dynamic_slice_in_dim(dm_all, shard * Wa, Wa, axis=2)
    gw_ada, dcond = _ada_bwd(c_all, dm_sh, w_ada)
    grads["w_ada"] = gw_ada
    grads["b_ada"] = _colsum_groups("ada_bsum", dm_all)
    dc_part = dcond[0:1]
    dc_buf = jnp.concatenate([jnp.where(ac == 0, dc_part, 0.0), jnp.zeros((7, D), F32)], axis=0)
    dc_tot = _allreduce8("cctx", dc_buf.reshape(8 * D // LANES, LANES)).reshape(8, D)[0:1]
    grads["c_ctx"] = _ew("cctx_grad", lambda g, cv: g * (_sigmoid(cv) * (1.0 + cv * (1.0 - _sigmoid(cv)))),
                         [jnp.broadcast_to(dc_tot, (8, D)), jnp.broadcast_to(c_ctx[None], (8, D))], [_sds((8, D))])[0][0]

    s5_layers = [i for i in range(DEPTH) if kinds[i] == "s5"]
    cv_layers = [i for i in range(DEPTH) if kinds[i] == "conv"]
    grads.update(mlp_w1=jnp.stack([reduced[i][0][0] for i in range(DEPTH)]),
                 mlp_w2=jnp.stack([reduced[i][0][1] for i in range(DEPTH)]),
                 s5_w_glu=jnp.stack([reduced[i][1][0] for i in s5_layers]),
                 cv_w_pw1=jnp.stack([reduced[i][1][0] for i in cv_layers]),
                 cv_w_pw2=jnp.stack([reduced[i][2][0] for i in cv_layers]))

    weights = dict(c_ctx=c_ctx, w_ada=w_ada, b_ada=b_ada, ln_gain=ln_gain, ln_bias=ln_bias, s5_lam_re=s5_lam_re,
                   s5_lam_im=s5_lam_im, s5_log_dt=s5_log_dt, s5_b_re=s5_b_re, s5_b_im=s5_b_im, s5_c_re=s5_c_re,
                   s5_c_im=s5_c_im, s5_d=s5_d, s5_w_glu=s5_w_glu, s5_b_glu=s5_b_glu, cv_w_pw1=cv_w_pw1, cv_b_pw1=cv_b_pw1,
                   cv_w_dw=cv_w_dw, cv_b_dw=cv_b_dw, cv_ln_g=cv_ln_g, cv_ln_b=cv_ln_b, cv_w_pw2=cv_w_pw2, cv_b_pw2=cv_b_pw2,
                   mlp_w1=mlp_w1, mlp_w2=mlp_w2)
    ms = dict(c_ctx=m_c_ctx, w_ada=m_w_ada, b_ada=m_b_ada, ln_gain=m_ln_gain, ln_bias=m_ln_bias, s5_lam_re=m_s5_lam_re,
              s5_lam_im=m_s5_lam_im, s5_log_dt=m_s5_log_dt, s5_b_re=m_s5_b_re, s5_b_im=m_s5_b_im, s5_c_re=m_s5_c_re,
              s5_c_im=m_s5_c_im, s5_d=m_s5_d, s5_w_glu=m_s5_w_glu, s5_b_glu=m_s5_b_glu, cv_w_pw1=m_cv_w_pw1,
              cv_b_pw1=m_cv_b_pw1, cv_w_dw=m_cv_w_dw, cv_b_dw=m_cv_b_dw, cv_ln_g=m_cv_ln_g, cv_ln_b=m_cv_ln_b,
              cv_w_pw2=m_cv_w_pw2, cv_b_pw2=m_cv_b_pw2, mlp_w1=m_mlp_w1, mlp_w2=m_mlp_w2)
    vs = dict(c_ctx=v_c_ctx, w_ada=v_w_ada, b_ada=v_b_ada, ln_gain=v_ln_gain, ln_bias=v_ln_bias, s5_lam_re=v_s5_lam_re,
              s5_lam_im=v_s5_lam_im, s5_log_dt=v_s5_log_dt, s5_b_re=v_s5_b_re, s5_b_im=v_s5_b_im, s5_c_re=v_s5_c_re,
              s5_c_im=v_s5_c_im, s5_d=v_s5_d, s5_w_glu=v_s5_w_glu, s5_b_glu=v_s5_b_glu, cv_w_pw1=v_cv_w_pw1,
              cv_b_pw1=v_cv_b_pw1, cv_w_dw=v_cv_w_dw, cv_b_dw=v_cv_b_dw, cv_ln_g=v_cv_ln_g, cv_ln_b=v_cv_ln_b,
              cv_w_pw2=v_cv_w_pw2, cv_b_pw2=v_cv_b_pw2, mlp_w1=v_mlp_w1, mlp_w2=v_mlp_w2)
    names = list(weights)
    deltas, new_m, new_v = {}, {}, {}
    for n in names:
        g = grads[n].reshape(weights[n].shape)
        grads[n] = g
        deltas[n], new_m[n], new_v[n] = _adamw("adamw_" + n, weights[n], g, ms[n], vs[n])
    return (loss, grad_x, *[grads[n] for n in names], *[deltas[n] for n in names], *[new_m[n] for n in names],
            *[new_v[n] for n in names])


def _sublane_sum(name, a):
    n, _, w = a.shape

    def body(a_ref, o_ref):
        for q in range(n):
            o_ref[q:q + 1, :] = jnp.sum(a_ref[q], axis=0, keepdims=True)

    return pl.pallas_call(body, name=name, out_shape=_sds((n, w)))(a)


def _colsum_groups(name, dm_all):
    nl, nr, w = dm_all.shape

    def body(d_ref, o_ref):
        o_ref[...] = jnp.zeros((8, w), F32) + jnp.sum(d_ref[...], axis=0, keepdims=True)

    out = pl.pallas_call(body, name=name, grid=(nl,), in_specs=[pl.BlockSpec((None, nr, w), lambda l: (l, 0, 0))],
                         out_specs=pl.BlockSpec((None, 8, w), lambda l: (l, 0, 0)), out_shape=_sds((nl, 8, w)),
                         compiler_params=_cp(1))(dm_all)
    return out[:, 0]
```
